```python
import jax, jax.numpy as jnp
from jax import lax
import numpy as np

D_MODEL = 1024
BATCH = 1
SEQ = 16384
DEPTH = 2
DEC_BATCH = 128
DEC_SEQ = 8
PAST_LEN = 16384
PAGE_SIZE = 128

N_A = DEPTH // 2
N_B = DEPTH - N_A
MEM_LEN = 256
MEM_HEADS = 4
HEAD_DIM = 64
MEM_WIDTH = MEM_HEADS * HEAD_DIM
MAIN_WIDTH = D_MODEL - MEM_WIDTH
GLA_HEADS = 4
GLA_DV = MAIN_WIDTH // GLA_HEADS
GLA_DK = GLA_DV // 2
GLA_KEY_WIDTH = GLA_HEADS * GLA_DK
GLA_GATE_RANK = 16
GLA_TAU = 16.0
GLA_CHUNK = 64
SWA_HEADS = MAIN_WIDTH // HEAD_DIM
SWA_KV_HEADS = 4
SWA_GROUP = SWA_HEADS // SWA_KV_HEADS
KV_WIDTH = SWA_KV_HEADS * HEAD_DIM
WINDOW = 128
N_GROUPS = 4
EXPERTS_PER_GROUP = 8
N_EXPERTS = N_GROUPS * EXPERTS_PER_GROUP
TOP_K = 2
D_EXPERT = 512
EPS = 1e-6
A_SPLITS = [GLA_KEY_WIDTH, 2 * GLA_KEY_WIDTH, 2 * GLA_KEY_WIDTH + MAIN_WIDTH,
            2 * GLA_KEY_WIDTH + 2 * MAIN_WIDTH, 2 * GLA_KEY_WIDTH + 2 * MAIN_WIDTH + GLA_GATE_RANK]
A_WIDTH = A_SPLITS[-1] + MEM_WIDTH
B_WIDTH = MAIN_WIDTH + MEM_WIDTH

kernel_name = 'yoco_gla_swa_sink_hmoe_mem_step'


def _rmsnorm(x, g):
    xf = x.astype(jnp.float32)
    y = xf * lax.rsqrt(jnp.mean(xf * xf, axis=-1, keepdims=True) + EPS)
    return (y * g.astype(jnp.float32)).astype(x.dtype)


def _alibi_slopes(n):
    return 2.0 ** (-8.0 * jnp.arange(1, n + 1, dtype=jnp.float32) / n)


def _gla_chunked(q, k, v, log_a, s0):
    B, T, H, _ = q.shape
    dv = v.shape[-1]
    C = min(GLA_CHUNK, T)
    n = -(-T // C)
    pad = n * C - T

    def prep(a):
        a = jnp.pad(a.astype(jnp.float32), ((0, 0), (0, pad), (0, 0), (0, 0)))
        return a.reshape(B, n, C, H, a.shape[-1]).transpose(1, 0, 3, 2, 4)

    qc, kc, vc, gc = prep(q), prep(k), prep(v), prep(log_a)
    causal = jnp.tril(jnp.ones((C, C), dtype=bool))

    def step(S, inp):
        qi, ki, vi, gi = inp
        b = jnp.cumsum(gi, axis=2)
        qt = qi * jnp.exp(b)
        kt = ki * jnp.exp(-b)
        A = jnp.where(causal, jnp.einsum('bhtd,bhsd->bhts', qt, kt), 0.0)
        o = jnp.einsum('bhts,bhsv->bhtv', A, vi) + jnp.einsum('bhtd,bhdv->bhtv', qt, S)
        b_end = b[:, :, -1]
        kd = ki * jnp.exp(b_end[:, :, None] - b)
        S = jnp.exp(b_end)[..., None] * S + jnp.einsum('bhsd,bhsv->bhdv', kd, vi)
        return S, o

    S, o = lax.scan(step, s0.astype(jnp.float32), (qc, kc, vc, gc))
    o = o.transpose(1, 0, 3, 2, 4).reshape(B, n * C, H, dv)[:, :T]
    return o, S


def _gla_mixer(hn, w_in, w_lr, b_lr, g_norm, s0):
    B, T, _ = hn.shape
    q, k, v, og, lr, mq = jnp.split(hn @ w_in, A_SPLITS, axis=-1)
    q = q.reshape(B, T, GLA_HEADS, GLA_DK) * (GLA_DK ** -0.5)
    k = k.reshape(B, T, GLA_HEADS, GLA_DK)
    v = v.reshape(B, T, GLA_HEADS, GLA_DV)
    log_a = jax.nn.log_sigmoid((lr @ w_lr + b_lr).astype(jnp.float32)) / GLA_TAU
    log_a = log_a.reshape(B, T, GLA_HEADS, GLA_DK)
    o, s = _gla_chunked(q, k, v, log_a, s0)
    o = _rmsnorm(o.astype(hn.dtype), g_norm) * jax.nn.silu(og.reshape(B, T, GLA_HEADS, GLA_DV))
    return o.reshape(B, T, MAIN_WIDTH), mq, s


def _mem_kv(mem, g_norm, w_kv, kn_g):
    B, M, _ = mem.shape
    kv = _rmsnorm(mem, g_norm) @ w_kv
    k = _rmsnorm(kv[..., :MEM_WIDTH].reshape(B, M, MEM_HEADS, HEAD_DIM), kn_g)
    v = kv[..., MEM_WIDTH:].reshape(B, M, MEM_HEADS, HEAD_DIM)
    return k, v


def _mem_attend(mq, qn_g, mk, mv):
    B, T, _ = mq.shape
    q = _rmsnorm(mq.reshape(B, T, MEM_HEADS, HEAD_DIM), qn_g)
    s = jnp.einsum('bthd,bmhd->bhtm', q, mk).astype(jnp.float32) * (HEAD_DIM ** -0.5)
    p = jax.nn.softmax(s, axis=-1).astype(mv.dtype)
    return jnp.einsum('bhtm,bmhd->bthd', p, mv).reshape(B, T, MEM_WIDTH)


def _shared_kv(h, g_norm, w_kv, kn_g):
    B, T, _ = h.shape
    kv = _rmsnorm(h, g_norm) @ w_kv
    k = _rmsnorm(kv[..., :KV_WIDTH].reshape(B, T, SWA_KV_HEADS, HEAD_DIM), kn_g)
    v = kv[..., KV_WIDTH:].reshape(B, T, SWA_KV_HEADS, HEAD_DIM)
    return k, v


def _swa_queries(hn, w_in, qn_g):
    B, T, _ = hn.shape
    proj = hn @ w_in
    q = _rmsnorm(proj[..., :MAIN_WIDTH].reshape(B, T, SWA_HEADS, HEAD_DIM), qn_g)
    return q, proj[..., MAIN_WIDTH:]


def _sink_window_attention(qg, kk, vv, dist, valid, sinks, slopes):
    s = jnp.einsum('...qkgd,...skd->...kgqs', qg, kk).astype(jnp.float32) * (HEAD_DIM ** -0.5)
    s = s - slopes.reshape(SWA_KV_HEADS, SWA_GROUP, 1, 1) * dist.astype(jnp.float32)
    s = jnp.where(valid, s, -jnp.inf)
    sink = sinks.astype(jnp.float32).reshape(SWA_KV_HEADS, SWA_GROUP, 1, 1)
    m = jnp.maximum(jnp.max(s, axis=-1, keepdims=True), sink)
    e = jnp.exp(s - m)
    p = e / (jnp.sum(e, axis=-1, keepdims=True) + jnp.exp(sink - m))
    return jnp.einsum('...kgqs,...skd->...qkgd', p.astype(vv.dtype), vv)


def _swa_prompt(q, k, v, sinks, slopes):
    B, T = q.shape[:2]
    nb = T // WINDOW
    qb = q.reshape(B, nb, WINDOW, SWA_KV_HEADS, SWA_GROUP, HEAD_DIM)
    kb = k.reshape(B, nb, WINDOW, SWA_KV_HEADS, HEAD_DIM)
    vb = v.reshape(B, nb, WINDOW, SWA_KV_HEADS, HEAD_DIM)
    shift = lambda a: jnp.concatenate([jnp.zeros_like(a[:, :1]), a[:, :-1]], axis=1)
    kk = jnp.concatenate([shift(kb), kb], axis=2)
    vv = jnp.concatenate([shift(vb), vb], axis=2)
    i = jnp.arange(WINDOW)[:, None]
    j = jnp.arange(2 * WINDOW)[None, :]
    dist = i - j + WINDOW
    blk = jnp.arange(nb)[:, None, None]
    valid = (dist >= 0) & (dist <= WINDOW) & ((blk > 0) | (j >= WINDOW))
    valid = valid[None, :, None, None]
    o = _sink_window_attention(qb, kk, vv, dist, valid, sinks, slopes)
    return o.reshape(B, T, MAIN_WIDTH)


def _swa_sample(q, k_new, v_new, k_buf, v_buf, sinks, slopes):
    B, Tn = q.shape[:2]
    Wb = k_buf.shape[1]
    kk = jnp.concatenate([k_buf, k_new], axis=1)
    vv = jnp.concatenate([v_buf, v_new], axis=1)
    dist = jnp.arange(Tn)[:, None] + Wb - jnp.arange(Wb + Tn)[None, :]
    valid = (dist >= 0) & (dist <= WINDOW)
    qg = q.reshape(B, Tn, SWA_KV_HEADS, SWA_GROUP, HEAD_DIM)
    o = _sink_window_attention(qg, kk, vv, dist, valid, sinks, slopes)
    return o.reshape(B, Tn, MAIN_WIDTH)


def _moe(x, w_rg, b_rg, w_re, b_re, w_g, w_u, w_d):
    shp = x.shape
    xt = x.reshape(-1, shp[-1])
    N = xt.shape[0]
    lg = (xt @ w_rg).astype(jnp.float32) + b_rg.astype(jnp.float32)
    pg = jax.nn.softmax(lg, axis=-1)
    grp = jnp.argmax(lg, axis=-1)
    pg_sel = jnp.take_along_axis(pg, grp[:, None], axis=1)
    le = ((xt @ w_re).astype(jnp.float32) + b_re.astype(jnp.float32)).reshape(N, N_GROUPS, EXPERTS_PER_GROUP)
    le = jnp.take_along_axis(le, grp[:, None, None], axis=1)[:, 0]
    tv, ti = lax.top_k(le, TOP_K)
    gate = jax.nn.softmax(tv, axis=-1) * pg_sel
    eid = (grp[:, None] * EXPERTS_PER_GROUP + ti).reshape(-1)
    order = jnp.argsort(eid)
    tok = order // TOP_K
    sizes = jnp.bincount(eid, length=N_EXPERTS).astype(jnp.int32)
    xs = xt[tok]
    hg = lax.ragged_dot(xs, w_g, sizes)
    hu = lax.ragged_dot(xs, w_u, sizes)
    ys = lax.ragged_dot(jax.nn.silu(hg) * hu, w_d, sizes)
    ys = ys * gate.reshape(-1)[order][:, None].astype(ys.dtype)
    return jnp.zeros_like(xt).at[tok].add(ys).reshape(shp)


def setup_inputs(seed: int = 0) -> dict:
    key = jax.random.key(seed)
    ks = iter(jax.random.split(key, 48))
    D = D_MODEL
    nrm = lambda shape, scale: jax.random.normal(next(ks), shape, jnp.float32) * scale
    gain = lambda shape: 1.0 + 0.02 * jax.random.normal(next(ks), shape, jnp.float32)
    w_buf = min(WINDOW, PAST_LEN)
    return {
        'x_prompt': nrm((BATCH, SEQ, D), 1.0),
        'x_sample': nrm((DEC_BATCH, DEC_SEQ, D), 1.0),
        'state_gla': nrm((N_A, DEC_BATCH, GLA_HEADS, GLA_DK, GLA_DV), GLA_DK ** -0.5),
        'cache_win_k': nrm((DEC_BATCH, w_buf, SWA_KV_HEADS, HEAD_DIM), 1.0),
        'cache_win_v': nrm((DEC_BATCH, w_buf, SWA_KV_HEADS, HEAD_DIM), 1.0),
        'cache_mem_k': nrm((DEPTH, DEC_BATCH, MEM_LEN, MEM_HEADS, HEAD_DIM), 1.0),
        'cache_mem_v': nrm((DEPTH, DEC_BATCH, MEM_LEN, MEM_HEADS, HEAD_DIM), 1.0),
        'mem_prompt': nrm((BATCH, MEM_LEN, D), 1.0),
        'norm_mix_g': gain((DEPTH, D)),
        'norm_ffn_g': gain((DEPTH, D)),
        'norm_mem_g': gain((DEPTH, D)),
        'w_mem_kv': nrm((DEPTH, D, 2 * MEM_WIDTH), D ** -0.5),
        'mem_qn_g': gain((DEPTH, HEAD_DIM)),
        'mem_kn_g': gain((DEPTH, HEAD_DIM)),
        'w_out': nrm((DEPTH, D_MODEL, D), D_MODEL ** -0.5),
        'w_in_a': nrm((N_A, D, A_WIDTH), D ** -0.5),
        'w_gate_lr': nrm((N_A, GLA_GATE_RANK, GLA_KEY_WIDTH), GLA_GATE_RANK ** -0.5),
        'b_gate_lr': nrm((N_A, GLA_KEY_WIDTH), 0.1),
        'gla_norm_g': gain((N_A, GLA_DV)),
        'w_in_b': nrm((N_B, D, B_WIDTH), D ** -0.5),
        'swa_qn_g': gain((N_B, HEAD_DIM)),
        'swa_sinks': nrm((N_B, SWA_HEADS), 0.5),
        'norm_kv_g': gain((D,)),
        'w_kv': nrm((D, 2 * KV_WIDTH), D ** -0.5),
        'swa_kn_g': gain((HEAD_DIM,)),
        'w_router_group': nrm((DEPTH, D, N_GROUPS), D ** -0.5),
        'b_router_group': nrm((DEPTH, N_GROUPS), 0.01),
        'w_router_expert': nrm((DEPTH, D, N_EXPERTS), D ** -0.5),
        'b_router_expert': nrm((DEPTH, N_EXPERTS), 0.01),
        'w_exp_gate': nrm((DEPTH, N_EXPERTS, D, D_EXPERT), D ** -0.5),
        'w_exp_up': nrm((DEPTH, N_EXPERTS, D, D_EXPERT), D ** -0.5),
        'w_exp_down': nrm((DEPTH, N_EXPERTS, D_EXPERT, D), D_EXPERT ** -0.5),
    }


def reference(x_prompt, x_sample, state_gla, cache_win_k, cache_win_v, cache_mem_k, cache_mem_v,
              mem_prompt, norm_mix_g, norm_ffn_g, norm_mem_g, w_mem_kv, mem_qn_g, mem_kn_g, w_out,
              w_in_a, w_gate_lr, b_gate_lr, gla_norm_g, w_in_b, swa_qn_g, swa_sinks, norm_kv_g, w_kv,
              swa_kn_g, w_router_group, b_router_group, w_router_expert, b_router_expert,
              w_exp_gate, w_exp_up, w_exp_down):
    slopes = _alibi_slopes(SWA_HEADS)

    h = x_prompt
    B = h.shape[0]
    gla_p, memk_p, memv_p = [], [], []
    for l in range(DEPTH):
        if l == N_A:
            k_sh, v_sh = _shared_kv(h, norm_kv_g, w_kv, swa_kn_g)
        hn = _rmsnorm(h, norm_mix_g[l])
        mk, mv = _mem_kv(mem_prompt, norm_mem_g[l], w_mem_kv[l], mem_kn_g[l])
        memk_p.append(mk)
        memv_p.append(mv)
        if l < N_A:
            s0 = jnp.zeros((B, GLA_HEADS, GLA_DK, GLA_DV), jnp.float32)
            main, mq, s = _gla_mixer(hn, w_in_a[l], w_gate_lr[l], b_gate_lr[l], gla_norm_g[l], s0)
            gla_p.append(s.astype(h.dtype))
        else:
            lb = l - N_A
            q, mq = _swa_queries(hn, w_in_b[lb], swa_qn_g[lb])
            main = _swa_prompt(q, k_sh, v_sh, swa_sinks[lb], slopes)
        mo = _mem_attend(mq, mem_qn_g[l], mk, mv)
        h = h + jnp.concatenate([main, mo], axis=-1) @ w_out[l]
        h = h + _moe(_rmsnorm(h, norm_ffn_g[l]), w_router_group[l], b_router_group[l],
                     w_router_expert[l], b_router_expert[l], w_exp_gate[l], w_exp_up[l], w_exp_down[l])
    y_prompt = h
    state_gla_prompt = jnp.stack(gla_p, axis=0)
    win_k_prompt = k_sh[:, -WINDOW:]
    win_v_prompt = v_sh[:, -WINDOW:]
    mem_k_prompt = jnp.stack(memk_p, axis=0)
    mem_v_prompt = jnp.stack(memv_p, axis=0)

    h = x_sample
    gla_s = []
    for l in range(DEPTH):
        if l == N_A:
            k_new, v_new = _shared_kv(h, norm_kv_g, w_kv, swa_kn_g)
        hn = _rmsnorm(h, norm_mix_g[l])
        if l < N_A:
            main, mq, s = _gla_mixer(hn, w_in_a[l], w_gate_lr[l], b_gate_lr[l], gla_norm_g[l], state_gla[l])
            gla_s.append(s.astype(state_gla.dtype))
        else:
            lb = l - N_A
            q, mq = _swa_queries(hn, w_in_b[lb], swa_qn_g[lb])
            main = _swa_sample(q, k_new, v_new, cache_win_k, cache_win_v, swa_sinks[lb], slopes)
        mo = _mem_attend(mq, mem_qn_g[l], cache_mem_k[l], cache_mem_v[l])
        h = h + jnp.concatenate([main, mo], axis=-1) @ w_out[l]
        h = h + _moe(_rmsnorm(h, norm_ffn_g[l]), w_router_group[l], b_router_group[l],
                     w_router_expert[l], b_router_expert[l], w_exp_gate[l], w_exp_up[l], w_exp_down[l])
    y_sample = h
    state_gla_sample = jnp.stack(gla_s, axis=0)
    w_buf = cache_win_k.shape[1]
    win_k_sample = jnp.concatenate([cache_win_k, k_new], axis=1)[:, -w_buf:]
    win_v_sample = jnp.concatenate([cache_win_v, v_new], axis=1)[:, -w_buf:]

    return (y_prompt, y_sample, state_gla_prompt, state_gla_sample, win_k_prompt, win_v_prompt,
            win_k_sample, win_v_sample, mem_k_prompt, mem_v_prompt)
```

```python
import functools

import numpy as np
import jax
import jax.numpy as jnp
from jax import lax
from jax.experimental import pallas as pl
from jax.experimental.pallas import tpu as pltpu

F32 = jnp.float32
BF16 = jnp.bfloat16

D_MODEL = 1024
MEM_LEN = 256
MEM_HEADS = 4
HEAD_DIM = 64
MEM_WIDTH = MEM_HEADS * HEAD_DIM
MAIN_WIDTH = D_MODEL - MEM_WIDTH
GLA_HEADS = 4
GLA_DV = MAIN_WIDTH // GLA_HEADS
GLA_DK = GLA_DV // 2
GLA_DK_PAD = 128
GLA_KEY_WIDTH = GLA_HEADS * GLA_DK
GLA_KEY_PAD = GLA_HEADS * GLA_DK_PAD
GLA_GATE_RANK = 16
GLA_TAU = 16.0
GLA_CHUNK = 64
SWA_HEADS = MAIN_WIDTH // HEAD_DIM
SWA_KV_HEADS = 4
SWA_GROUP = SWA_HEADS // SWA_KV_HEADS
KV_WIDTH = SWA_KV_HEADS * HEAD_DIM
WINDOW = 128
N_GROUPS = 4
EXPERTS_PER_GROUP = 8
N_EXPERTS = N_GROUPS * EXPERTS_PER_GROUP
D_EXPERT = 512
EPS = 1e-6
LANES = 128
NEG_BIG = -1e30
VMEM_LIMIT = 56 * 1024 * 1024
MOE_TILE = 256
ROUTER_LANE0 = N_GROUPS


def _bdot(a, b):
    return jnp.dot(a.astype(BF16), b.astype(BF16), preferred_element_type=F32)


def _bdot_nt(a, b):
    return lax.dot_general(a.astype(BF16), b.astype(BF16), (((1,), (1,)), ((), ())),
                           preferred_element_type=F32)


def _bdot_tn(a, b):
    return lax.dot_general(a.astype(BF16), b.astype(BF16), (((0,), (0,)), ((), ())),
                           preferred_element_type=F32)


def _split(x, n):
    parts = []
    for _ in range(n - 1):
        p = x.astype(BF16)
        parts.append(p)
        x = x - p.astype(F32)
    parts.append(x.astype(BF16))
    return parts


def _exact_left_dot(m, x, n=3):
    out = None
    for p in _split(x, n):
        t = jnp.dot(m, p, preferred_element_type=F32)
        out = t if out is None else out + t
    return out


def _seg_mean(x2, bd):
    out = None
    for p in _split(x2, 2):
        t = jnp.dot(p, bd, preferred_element_type=F32)
        out = t if out is None else out + t
    return out


def _rms_scale(x):
    return lax.rsqrt(jnp.mean(x * x, axis=-1, keepdims=True) + EPS)


def _row_tile(n, cap=512):
    t = cap
    while t > 8 and n % t:
        t //= 2
    assert n % t == 0, n
    return t


def _params(sem):
    return pltpu.CompilerParams(dimension_semantics=sem, vmem_limit_bytes=VMEM_LIMIT)


def _const(shape):
    nd = len(shape)
    return pl.BlockSpec(shape, lambda *_: (0,) * nd)


def _block_diag_mean(width):
    i = np.arange(width)
    return jnp.asarray((i[:, None] // HEAD_DIM == i[None, :] // HEAD_DIM) / HEAD_DIM, BF16)


def _head_masks(n_heads):
    i = np.arange(n_heads * HEAD_DIM)
    return jnp.asarray((i[None, :] // HEAD_DIM == np.arange(n_heads)[:, None]), F32)[:, None, :]


def _mem_kv_kernel(mem_ref, g_ref, w_ref, kng_ref, bd_ref, k_ref, v_ref):
    x = mem_ref[0]
    hn = x * _rms_scale(x) * g_ref[0]
    kv = _bdot(hn, w_ref[0])
    k = kv[:, :MEM_WIDTH]
    k = k * lax.rsqrt(_seg_mean(k * k, bd_ref[...]) + EPS) * kng_ref[0]
    k_ref[0, 0] = k
    v_ref[0, 0] = kv[:, MEM_WIDTH:]


def _mem_kv(mem, g, w, kng):
    depth, (b, m, d) = w.shape[0], mem.shape
    out = jax.ShapeDtypeStruct((depth, b, m, MEM_WIDTH), F32)
    blk = pl.BlockSpec((1, 1, m, MEM_WIDTH), lambda l, i: (l, i, 0, 0))
    return pl.pallas_call(
        _mem_kv_kernel,
        grid=(depth, b),
        in_specs=[pl.BlockSpec((1, m, d), lambda l, i: (i, 0, 0)),
                  pl.BlockSpec((1, 1, d), lambda l, i: (l, 0, 0)),
                  pl.BlockSpec((1, d, 2 * MEM_WIDTH), lambda l, i: (l, 0, 0)),
                  pl.BlockSpec((1, 1, MEM_WIDTH), lambda l, i: (l, 0, 0)),
                  _const((MEM_WIDTH, MEM_WIDTH))],
        out_specs=[blk, blk],
        out_shape=[out, out],
        compiler_params=_params(("arbitrary", "arbitrary")),
        name="mem_kv",
    )(mem, g.reshape(depth, 1, d), w.astype(BF16),
      jnp.tile(kng, (1, MEM_HEADS)).reshape(depth, 1, MEM_WIDTH), _block_diag_mean(MEM_WIDTH))


def _inproj_a_kernel(x_ref, g_ref, wq_ref, wk_ref, wv_ref, wog_ref, wlr_ref, wmq_ref, wgl_ref, bgl_ref,
                     q_ref, k_ref, la_ref, v_ref, og_ref, mq_ref):
    x = x_ref[...]
    hn = (x * _rms_scale(x) * g_ref[...]).astype(BF16)
    q_ref[...] = jnp.dot(hn, wq_ref[...], preferred_element_type=F32) * (GLA_DK ** -0.5)
    k_ref[...] = jnp.dot(hn, wk_ref[...], preferred_element_type=F32)
    for h in range(GLA_HEADS):
        v_ref[h] = jnp.dot(hn, wv_ref[h], preferred_element_type=F32)
        og_ref[h] = jnp.dot(hn, wog_ref[h], preferred_element_type=F32)
    lr = jnp.dot(hn, wlr_ref[...], preferred_element_type=F32)
    z = _bdot(lr, wgl_ref[...]) + bgl_ref[...]
    la_ref[...] = (jnp.minimum(z, 0.0) - jnp.log(1.0 + jnp.exp(-jnp.abs(z)))) * (1.0 / GLA_TAU)
    mq_ref[...] = jnp.dot(hn, wmq_ref[...], preferred_element_type=F32)


def _pad_heads(w, width, pad):
    lead = w.shape[:-1]
    w = w.reshape(*lead, GLA_HEADS, width)
    w = jnp.pad(w, [(0, 0)] * len(lead) + [(0, 0), (0, pad - width)])
    return w.reshape(*lead, GLA_HEADS * pad)


def _inproj_a(x, g, w_in, w_lr, b_lr):
    n, d = x.shape
    tm = _row_tile(n)
    c0, c1, c2, c3, c4 = (GLA_KEY_WIDTH, 2 * GLA_KEY_WIDTH, 2 * GLA_KEY_WIDTH + MAIN_WIDTH,
                          2 * GLA_KEY_WIDTH + 2 * MAIN_WIDTH,
                          2 * GLA_KEY_WIDTH + 2 * MAIN_WIDTH + GLA_GATE_RANK)
    wb = w_in.astype(BF16)
    wq = _pad_heads(wb[:, :c0], GLA_DK, GLA_DK_PAD)
    wk = _pad_heads(wb[:, c0:c1], GLA_DK, GLA_DK_PAD)
    wv = wb[:, c1:c2].reshape(d, GLA_HEADS, GLA_DV).transpose(1, 0, 2)
    wog = wb[:, c2:c3].reshape(d, GLA_HEADS, GLA_DV).transpose(1, 0, 2)
    wlr = jnp.pad(wb[:, c3:c4], ((0, 0), (0, LANES - GLA_GATE_RANK)))
    wmq = wb[:, c4:]
    wgl = jnp.pad(_pad_heads(w_lr.astype(BF16), GLA_DK, GLA_DK_PAD), ((0, LANES - GLA_GATE_RANK), (0, 0)))
    bgl = _pad_heads(b_lr.reshape(1, -1), GLA_DK, GLA_DK_PAD)
    row = lambda w: pl.BlockSpec((tm, w), lambda i: (i, 0))
    hrow = pl.BlockSpec((GLA_HEADS, tm, GLA_DV), lambda i: (0, i, 0))
    key = jax.ShapeDtypeStruct((n, GLA_KEY_PAD), F32)
    val = jax.ShapeDtypeStruct((GLA_HEADS, n, GLA_DV), F32)
    return pl.pallas_call(
        _inproj_a_kernel,
        grid=(n // tm,),
        in_specs=[row(d), _const((1, d)), _const(wq.shape), _const(wk.shape), _const(wv.shape),
                  _const(wog.shape), _const(wlr.shape), _const(wmq.shape), _const(wgl.shape),
                  _const(bgl.shape)],
        out_specs=[row(GLA_KEY_PAD), row(GLA_KEY_PAD), row(GLA_KEY_PAD), hrow, hrow, row(MEM_WIDTH)],
        out_shape=[key, key, key, val, val, jax.ShapeDtypeStruct((n, MEM_WIDTH), F32)],
        compiler_params=_params(("parallel",)),
        name="inproj_a",
    )(x, g.reshape(1, d), wq, wk, wv, wog, wlr, wmq, wgl, bgl)


def _gla_kernel(q_ref, k_ref, la_ref, v_ref, og_ref, s0_ref, gn_ref, mcum_ref, mall_ref, sel_ref,
                o_ref, sout_ref, s_ref, *, chunk, n_sub, n_seg):
    j = pl.program_id(1)
    seg = chunk // n_seg

    @pl.when(j == 0)
    def _():
        s_ref[...] = jnp.zeros_like(s_ref)
        s_ref[:, :, :GLA_DK, :] = s0_ref[...]

    mcum = mcum_ref[...]
    causal = mcum.astype(F32) > 0.0
    row = lax.broadcasted_iota(jnp.int32, (chunk, GLA_DK_PAD), 0)
    gn = gn_ref[...]
    for c in range(n_sub):
        rows = slice(c * chunk, (c + 1) * chunk)
        la = la_ref[rows, :]
        b = _exact_left_dot(mcum, la)
        b_end = _exact_left_dot(mall_ref[...], la)
        e_end = jnp.exp(_exact_left_dot(sel_ref[...], la)).T
        q = q_ref[rows, :]
        k = k_ref[rows, :]
        qt = q * jnp.exp(b)
        kt = (k * jnp.exp(-b)).astype(BF16)
        kd = k * jnp.exp(b_end - b)
        for h in range(GLA_HEADS):
            cols = slice(h * GLA_DK_PAD, (h + 1) * GLA_DK_PAD)
            qh = qt[:, cols]
            vb = v_ref[h, rows, :].astype(BF16)
            a = jnp.where(causal, _bdot_nt(qh, kt[:, cols]), 0.0)
            o = _bdot(a, vb)
            inter = []
            for s in range(n_seg):
                st = s_ref[s, h]
                inter.append(_bdot(qh[s * seg:(s + 1) * seg], st))
                kds = kd[:, cols]
                if n_seg > 1:
                    kds = jnp.where((row >= s * seg) & (row < (s + 1) * seg), kds, 0.0)
                s_ref[s, h] = e_end[cols, s:s + 1] * st + _bdot_tn(kds, vb)
            o = o + (inter[0] if n_seg == 1 else jnp.concatenate(inter, axis=0))
            on = o * lax.rsqrt(jnp.mean(o * o, axis=-1, keepdims=True) + EPS) * gn
            og = og_ref[h, rows, :]
            o_ref[h, rows, :] = on * (og * jax.nn.sigmoid(og))

    @pl.when(j == pl.num_programs(1) - 1)
    def _():
        sout_ref[...] = s_ref[:, :, :GLA_DK, :]


def _gla(q, k, la, v, og, s0, gnorm, *, row_off, seq, n_seg, n_sub):
    batch = s0.shape[0]
    chunk = GLA_CHUNK
    assert chunk % n_seg == 0 and batch % n_seg == 0
    seg = chunk // n_seg
    step_rows = n_sub * chunk
    if n_seg > 1:
        assert seq == seg and n_sub == 1
        t_steps = 1
    else:
        assert seq % step_rows == 0
        t_steps = seq // step_rows
    assert row_off % step_rows == 0
    off = row_off // step_rows
    i = np.arange(chunk)
    same = (i[:, None] // seg) == (i[None, :] // seg)
    mcum = jnp.asarray(same & (i[None, :] <= i[:, None]), BF16)
    mall = jnp.asarray(same, BF16)
    sel = jnp.asarray((i[None, :] // seg) == np.arange(LANES)[:, None], BF16)
    ridx = lambda g, j: (off + g * t_steps + j, 0)
    hidx = lambda g, j: (0, off + g * t_steps + j, 0)
    key_spec = pl.BlockSpec((step_rows, GLA_KEY_PAD), ridx)
    val_spec = pl.BlockSpec((GLA_HEADS, step_rows, GLA_DV), hidx)
    st_spec = pl.BlockSpec((n_seg, GLA_HEADS, GLA_DK, GLA_DV), lambda g, j: (g, 0, 0, 0))
    in_specs = [key_spec, key_spec, key_spec, val_spec, val_spec, st_spec, _const((1, GLA_DV)),
                _const((chunk, chunk)), _const((chunk, chunk)), _const((LANES, chunk))]
    args = [q, k, la, v, og, s0, gnorm.reshape(1, GLA_DV), mcum, mall, sel]
    out_spec = pl.BlockSpec((GLA_HEADS, step_rows, GLA_DV), lambda g, j: (0, g * t_steps + j, 0))
    return pl.pallas_call(
        functools.partial(_gla_kernel, chunk=chunk, n_sub=n_sub, n_seg=n_seg),
        grid=(batch // n_seg, t_steps),
        in_specs=in_specs,
        out_specs=[out_spec, st_spec],
        out_shape=[jax.ShapeDtypeStruct((GLA_HEADS, batch * seq, GLA_DV), F32),
                   jax.ShapeDtypeStruct(s0.shape, F32)],
        scratch_shapes=[pltpu.VMEM((n_seg, GLA_HEADS, GLA_DK_PAD, GLA_DV), F32)],
        compiler_params=_params(("arbitrary", "arbitrary")),
        name="gla",
    )(*args)


def _mem_attn_kernel(q_ref, k_ref, v_ref, g_ref, bd_ref, hm_ref, o_ref, *, tm, bb):
    g = g_ref[...]
    for i in range(bb):
        rows = slice(i * tm, (i + 1) * tm)
        q = q_ref[rows, :]
        qn = q * lax.rsqrt(_seg_mean(q * q, bd_ref[...]) + EPS) * g
        qs = jnp.concatenate([(qn * hm_ref[h]).astype(BF16) for h in range(MEM_HEADS)], axis=0)
        s = _bdot_nt(qs, k_ref[i])
        e = jnp.exp(s - jnp.max(s, axis=-1, keepdims=True))
        p = e / jnp.sum(e, axis=-1, keepdims=True)
        o = _bdot(p, v_ref[i])
        acc = o[:tm] * hm_ref[0]
        for h in range(1, MEM_HEADS):
            acc = acc + o[h * tm:(h + 1) * tm] * hm_ref[h]
        o_ref[rows, :] = acc


def _mem_attn(mq, mk, mv, qng, *, row_off, seq, tm, bb):
    batch, m, _ = mk.shape
    assert seq % tm == 0 and batch % bb == 0 and (bb == 1 or seq == tm)
    t_steps = seq // tm
    step_rows = bb * tm
    assert row_off % step_rows == 0
    off = row_off // step_rows
    row_spec = pl.BlockSpec((step_rows, MEM_WIDTH), lambda g, j: (off + g * t_steps + j, 0))
    kv_spec = pl.BlockSpec((bb, m, MEM_WIDTH), lambda g, j: (g, 0, 0))
    in_specs = [row_spec, kv_spec, kv_spec, _const((1, MEM_WIDTH)), _const((MEM_WIDTH, MEM_WIDTH)),
                _const((MEM_HEADS, 1, MEM_WIDTH))]
    args = [mq, mk, mv, (jnp.tile(qng, MEM_HEADS) * HEAD_DIM ** -0.5).reshape(1, MEM_WIDTH),
            _block_diag_mean(MEM_WIDTH), _head_masks(MEM_HEADS)]
    return pl.pallas_call(
        functools.partial(_mem_attn_kernel, tm=tm, bb=bb),
        grid=(batch // bb, t_steps),
        in_specs=in_specs,
        out_specs=pl.BlockSpec((step_rows, MEM_WIDTH), lambda g, j: (g * t_steps + j, 0)),
        out_shape=jax.ShapeDtypeStruct((batch * seq, MEM_WIDTH), F32),
        compiler_params=_params(("parallel", "parallel")),
        name="mem_attn",
    )(*args)


def _outproj_kernel(h_ref, main_p_ref, mo_p_ref, main_s_ref, mo_s_ref, wmain_ref, wmo_ref, o_ref, *,
                    heads, prompt_tiles):
    def project(main_ref, mo_ref):
        acc = h_ref[...] + _bdot(mo_ref[...], wmo_ref[...])
        if heads:
            for h in range(heads):
                acc = acc + _bdot(main_ref[h], wmain_ref[h])
        else:
            acc = acc + _bdot(main_ref[...], wmain_ref[...])
        o_ref[...] = acc

    @pl.when(pl.program_id(0) < prompt_tiles)
    def _():
        project(main_p_ref, mo_p_ref)

    @pl.when(pl.program_id(0) >= prompt_tiles)
    def _():
        project(main_s_ref, mo_s_ref)


def _outproj(h, main_p, mo_p, main_s, mo_s, w_main, w_mo):
    n, d = h.shape
    n_p, n_s = mo_p.shape[0], mo_s.shape[0]
    tm = _row_tile(n_s)
    assert n_p % tm == 0 and n_p + n_s == n
    pt = n_p // tm
    heads = main_p.shape[0] if main_p.ndim == 3 else 0
    p_idx = lambda i: jnp.minimum(i, pt - 1)
    s_idx = lambda i: jnp.maximum(i - pt, 0)
    if heads:
        main_spec = lambda idx: pl.BlockSpec((heads, tm, main_p.shape[2]), lambda i: (0, idx(i), 0))
    else:
        main_spec = lambda idx: pl.BlockSpec((tm, main_p.shape[1]), lambda i: (idx(i), 0))
    mo_spec = lambda idx: pl.BlockSpec((tm, MEM_WIDTH), lambda i: (idx(i), 0))
    row = pl.BlockSpec((tm, d), lambda i: (i, 0))
    return pl.pallas_call(
        functools.partial(_outproj_kernel, heads=heads, prompt_tiles=pt),
        grid=(n // tm,),
        in_specs=[row, main_spec(p_idx), mo_spec(p_idx), main_spec(s_idx), mo_spec(s_idx),
                  _const(w_main.shape), _const(w_mo.shape)],
        out_specs=row,
        out_shape=jax.ShapeDtypeStruct((n, d), F32),
        compiler_params=_params(("parallel",)),
        name="outproj",
    )(h, main_p, mo_p, main_s, mo_s, w_main.astype(BF16), w_mo.astype(BF16))


def _router_kernel(h_ref, g_ref, whi_ref, wlo_ref, b_ref, tril_ref, xn_ref, mi_ref, mf_ref, cnt_ref, carry_ref):
    i = pl.program_id(0)

    @pl.when(i == 0)
    def _():
        carry_ref[...] = jnp.zeros_like(carry_ref)

    x = h_ref[...]
    xn = x * _rms_scale(x) * g_ref[...]
    xn_ref[...] = xn
    x_hi, x_lo = _split(xn, 2)
    logits = (jnp.dot(x_hi, whi_ref[...], preferred_element_type=F32)
              + jnp.dot(x_hi, wlo_ref[...], preferred_element_type=F32)
              + jnp.dot(x_lo, whi_ref[...], preferred_element_type=F32)) + b_ref[...]
    tm = x.shape[0]
    lane = lax.broadcasted_iota(jnp.int32, (tm, LANES), 1)
    far = jnp.int32(2 * LANES)

    def first_max(vals):
        m = jnp.max(vals, axis=-1, keepdims=True)
        return m, jnp.min(jnp.where(vals == m, lane, far), axis=-1, keepdims=True)

    gl = jnp.where(lane < N_GROUPS, logits, -jnp.inf)
    gmax, grp = first_max(gl)
    pg_sel = 1.0 / jnp.sum(jnp.exp(gl - gmax), axis=-1, keepdims=True)
    lo = ROUTER_LANE0 + grp * EXPERTS_PER_GROUP
    el = jnp.where((lane >= lo) & (lane < lo + EXPERTS_PER_GROUP), logits, -jnp.inf)
    m1, i1 = first_max(el)
    m2, i2 = first_max(jnp.where(lane == i1, -jnp.inf, el))
    e2 = jnp.exp(m2 - m1)
    g1 = pg_sel / (1.0 + e2)
    g2 = pg_sel * e2 / (1.0 + e2)

    oh1 = lane == i1
    oh2 = lane == i2
    picked = jnp.where(oh1 | oh2, 1.0, 0.0)
    before = jnp.dot(tril_ref[...], picked.astype(BF16), preferred_element_type=F32) + carry_ref[...]
    rank1 = jnp.sum(jnp.where(oh1, before, 0.0), axis=-1, keepdims=True)
    rank2 = jnp.sum(jnp.where(oh2, before, 0.0), axis=-1, keepdims=True)
    carry = carry_ref[...] + jnp.sum(picked, axis=0, keepdims=True)
    carry_ref[...] = carry
    cnt_ref[...] = carry

    zi = jnp.zeros((tm, LANES), jnp.int32)
    mi = jnp.where(lane == 0, i1 - ROUTER_LANE0, zi)
    mi = jnp.where(lane == 1, i2 - ROUTER_LANE0, mi)
    mi = jnp.where(lane == 2, rank1.astype(jnp.int32), mi)
    mi = jnp.where(lane == 3, rank2.astype(jnp.int32), mi)
    mi_ref[...] = mi
    zf = jnp.zeros((tm, LANES), F32)
    mf_ref[...] = jnp.where(lane == 0, g1, jnp.where(lane == 1, g2, zf))


def _router(h, g, w_rg, b_rg, w_re, b_re):
    n, d = h.shape
    tm = _row_tile(n)
    n_real = N_GROUPS + N_EXPERTS
    w = jnp.pad(jnp.concatenate([w_rg, w_re], axis=1), ((0, 0), (0, LANES - n_real)))
    b = jnp.pad(jnp.concatenate([b_rg, b_re]), (0, LANES - n_real)).reshape(1, LANES)
    w_hi = w.astype(BF16)
    w_lo = (w - w_hi.astype(F32)).astype(BF16)
    i = np.arange(tm)
    tril = jnp.asarray(i[None, :] < i[:, None], BF16)
    row = lambda width: pl.BlockSpec((tm, width), lambda i: (i, 0))
    return pl.pallas_call(
        _router_kernel,
        grid=(n // tm,),
        in_specs=[row(d), _const((1, d)), _const((d, LANES)), _const((d, LANES)), _const((1, LANES)),
                  _const((tm, tm))],
        out_specs=[row(d), row(LANES), row(LANES), _const((1, LANES))],
        out_shape=[jax.ShapeDtypeStruct((n, d), F32), jax.ShapeDtypeStruct((n, LANES), jnp.int32),
                   jax.ShapeDtypeStruct((n, LANES), F32), jax.ShapeDtypeStruct((1, LANES), F32)],
        scratch_shapes=[pltpu.VMEM((1, LANES), F32)],
        compiler_params=_params(("arbitrary",)),
        name="moe_router",
    )(h, g.reshape(1, d), w_hi, w_lo, b, tril)


def _expert_kernel(te_ref, nu_ref, src_ref, xn_hbm, wg_ref, wu_ref, wd_ref, ys_ref, xbuf, sem):
    i = pl.program_id(0)
    tm = xbuf.shape[0]

    @pl.when(i < nu_ref[0])
    def _():
        base = i * tm

        def issue(r, carry):
            tok = src_ref[base + r]
            pltpu.make_async_copy(xn_hbm.at[pl.ds(tok, 1)], xbuf.at[pl.ds(r, 1)], sem).start()
            return carry

        lax.fori_loop(0, tm, issue, 0)
        pltpu.make_async_copy(xn_hbm.at[pl.ds(0, tm)], xbuf, sem).wait()
        x = xbuf[...].astype(BF16)
        hg = jnp.dot(x, wg_ref[0].astype(BF16), preferred_element_type=F32)
        hu = jnp.dot(x, wu_ref[0].astype(BF16), preferred_element_type=F32)
        act = (hg * jax.nn.sigmoid(hg) * hu).astype(BF16)
        ys_ref[...] = jnp.dot(act, wd_ref[0].astype(BF16), preferred_element_type=F32)

    @pl.when(i >= nu_ref[0])
    def _():
        ys_ref[...] = jnp.zeros_like(ys_ref)


def _experts(xn, tile_expert, n_used, src_tok, w_g, w_u, w_d):
    n, d = xn.shape
    n_tiles = tile_expert.shape[0]
    tm = MOE_TILE
    wmap = lambda i, te, nu, src: (te[i], 0, 0)
    return pl.pallas_call(
        _expert_kernel,
        grid_spec=pltpu.PrefetchScalarGridSpec(
            num_scalar_prefetch=3,
            grid=(n_tiles,),
            in_specs=[pl.BlockSpec(memory_space=pl.ANY),
                      pl.BlockSpec((1, d, D_EXPERT), wmap),
                      pl.BlockSpec((1, d, D_EXPERT), wmap),
                      pl.BlockSpec((1, D_EXPERT, d), wmap)],
            out_specs=pl.BlockSpec((tm, d), lambda i, te, nu, src: (i, 0)),
            scratch_shapes=[pltpu.VMEM((tm, d), F32), pltpu.SemaphoreType.DMA],
        ),
        out_shape=jax.ShapeDtypeStruct((n_tiles * tm, d), F32),
        compiler_params=_params(("arbitrary",)),
        name="moe_experts",
    )(tile_expert, n_used, src_tok, xn, w_g, w_u, w_d)


def _combine_kernel(pos_ref, h_ref, gate_ref, ys_hbm, o_ref, buf, sem):
    i = pl.program_id(0)
    tm = h_ref.shape[0]
    base = i * tm * 2

    def issue(r, carry):
        for kk in range(2):
            p = pos_ref[base + 2 * r + kk]
            pltpu.make_async_copy(ys_hbm.at[pl.ds(p, 1)], buf.at[kk, pl.ds(r, 1)], sem).start()
        return carry

    lax.fori_loop(0, tm, issue, 0)
    for kk in range(2):
        pltpu.make_async_copy(ys_hbm.at[pl.ds(0, tm)], buf.at[kk], sem).wait()
    g = gate_ref[...]
    o_ref[...] = h_ref[...] + g[:, 0:1] * buf[0] + g[:, 1:2] * buf[1]


def _combine(h, gates, ys, pos_flat):
    n, d = h.shape
    tm = _row_tile(n, 256)
    return pl.pallas_call(
        _combine_kernel,
        grid_spec=pltpu.PrefetchScalarGridSpec(
            num_scalar_prefetch=1,
            grid=(n // tm,),
            in_specs=[pl.BlockSpec((tm, d), lambda i, pos: (i, 0)),
                      pl.BlockSpec((tm, LANES), lambda i, pos: (i, 0)),
                      pl.BlockSpec(memory_space=pl.ANY)],
            out_specs=pl.BlockSpec((tm, d), lambda i, pos: (i, 0)),
            scratch_shapes=[pltpu.VMEM((2, tm, d), F32), pltpu.SemaphoreType.DMA],
        ),
        out_shape=jax.ShapeDtypeStruct((n, d), F32),
        compiler_params=_params(("arbitrary",)),
        name="moe_combine",
    )(pos_flat, h, gates, ys)


def _moe(h, g, w_rg, b_rg, w_re, b_re, w_g, w_u, w_d):
    n, _ = h.shape
    tm = MOE_TILE
    xn, meta_i, gates, cnt = _router(h, g, w_rg, b_rg, w_re, b_re)
    eid = meta_i[:, 0:2]
    rank = meta_i[:, 2:4]
    counts = cnt[0, ROUTER_LANE0:ROUTER_LANE0 + N_EXPERTS].astype(jnp.int32)
    tiles_per = (counts + tm - 1) // tm
    tile_end = jnp.cumsum(tiles_per)
    offsets = (tile_end - tiles_per) * tm
    n_tiles = (2 * n + N_EXPERTS * (tm - 1)) // tm
    n_used = tile_end[-1]
    tile_ids = jnp.minimum(jnp.arange(n_tiles, dtype=jnp.int32), n_used - 1)
    tile_expert = jnp.searchsorted(tile_end, tile_ids, side="right").astype(jnp.int32)
    pos = (offsets[eid] + rank).reshape(-1)
    tok = jnp.arange(2 * n, dtype=jnp.int32) // 2
    src_tok = jnp.zeros((n_tiles * tm,), jnp.int32).at[pos].set(tok)
    ys = _experts(xn, tile_expert, n_used.reshape(1), src_tok, w_g, w_u, w_d)
    return _combine(h, gates, ys, pos)


def _inproj_b_kernel(x_ref, gkv_ref, gmix_ref, wkv_ref, win_ref, kng_ref, qng_ref, bdk_ref, bdq_ref,
                     q_ref, mq_ref, k_ref, v_ref):
    x = x_ref[...]
    xr = x * _rms_scale(x)
    kv = _bdot(xr * gkv_ref[...], wkv_ref[...])
    k = kv[:, :KV_WIDTH]
    k_ref[...] = k * lax.rsqrt(_seg_mean(k * k, bdk_ref[...]) + EPS) * kng_ref[...]
    v_ref[...] = kv[:, KV_WIDTH:]
    proj = _bdot(xr * gmix_ref[...], win_ref[...])
    q = proj[:, :MAIN_WIDTH]
    q_ref[...] = q * lax.rsqrt(_seg_mean(q * q, bdq_ref[...]) + EPS) * qng_ref[...]
    mq_ref[...] = proj[:, MAIN_WIDTH:]


def _swa_perm():
    g, kh, dd = np.meshgrid(np.arange(SWA_GROUP), np.arange(SWA_KV_HEADS), np.arange(HEAD_DIM), indexing="ij")
    return ((kh * SWA_GROUP + g) * HEAD_DIM + dd).reshape(-1)


def _inproj_b(x, g_kv, g_mix, w_kv, w_in, kng, qng):
    n, d = x.shape
    tm = _row_tile(n)
    perm = _swa_perm()
    w_in_p = jnp.concatenate([w_in[:, :MAIN_WIDTH][:, perm], w_in[:, MAIN_WIDTH:]], axis=1).astype(BF16)
    qng_t = (jnp.tile(qng, SWA_HEADS) * HEAD_DIM ** -0.5).reshape(1, MAIN_WIDTH)
    row = lambda w: pl.BlockSpec((tm, w), lambda i: (i, 0))
    return pl.pallas_call(
        _inproj_b_kernel,
        grid=(n // tm,),
        in_specs=[row(d), _const((1, d)), _const((1, d)), _const((d, 2 * KV_WIDTH)), _const((d, d)),
                  _const((1, KV_WIDTH)), _const((1, MAIN_WIDTH)), _const((KV_WIDTH, KV_WIDTH)),
                  _const((MAIN_WIDTH, MAIN_WIDTH))],
        out_specs=[row(MAIN_WIDTH), row(MEM_WIDTH), row(KV_WIDTH), row(KV_WIDTH)],
        out_shape=[jax.ShapeDtypeStruct((n, MAIN_WIDTH), F32), jax.ShapeDtypeStruct((n, MEM_WIDTH), F32),
                   jax.ShapeDtypeStruct((n, KV_WIDTH), F32), jax.ShapeDtypeStruct((n, KV_WIDTH), F32)],
        compiler_params=_params(("parallel",)),
        name="inproj_b",
    )(x, g_kv.reshape(1, d), g_mix.reshape(1, d), w_kv.astype(BF16), w_in_p,
      jnp.tile(kng, SWA_KV_HEADS).reshape(1, KV_WIDTH), qng_t,
      _block_diag_mean(KV_WIDTH), _block_diag_mean(MAIN_WIDTH))


def _swa_kernel(q_ref, kp_ref, ko_ref, vp_ref, vo_ref, bias_ref, sink_ref, hm_ref, o_ref, *, tq, mask_first):
    kk = jnp.concatenate([kp_ref[...], ko_ref[...]], axis=0).astype(BF16)
    vv = jnp.concatenate([vp_ref[...], vo_ref[...]], axis=0).astype(BF16)
    n_keys = kk.shape[0]
    q = q_ref[...]
    if mask_first:
        key = lax.broadcasted_iota(jnp.int32, (SWA_GROUP * tq, n_keys), 1)
        has_prev = (pl.program_id(0) > 0) | (key >= WINDOW)
    acc = [None] * SWA_GROUP
    for kh in range(SWA_KV_HEADS):
        hm = hm_ref[kh]
        qs = jnp.concatenate([(q[:, g * KV_WIDTH:(g + 1) * KV_WIDTH] * hm).astype(BF16)
                              for g in range(SWA_GROUP)], axis=0)
        s = _bdot_nt(qs, kk) + bias_ref[kh]
        if mask_first:
            s = jnp.where(has_prev, s, NEG_BIG)
        sink = sink_ref[kh]
        m = jnp.maximum(jnp.max(s, axis=-1, keepdims=True), sink)
        e = jnp.exp(s - m)
        p = e / (jnp.sum(e, axis=-1, keepdims=True) + jnp.exp(sink - m))
        o = _bdot(p, vv)
        for g in range(SWA_GROUP):
            t = o[g * tq:(g + 1) * tq] * hm
            acc[g] = t if acc[g] is None else acc[g] + t
    for g in range(SWA_GROUP):
        o_ref[:, g * KV_WIDTH:(g + 1) * KV_WIDTH] = acc[g]


def _swa(q, k_prev, v_prev, k_own, v_own, sinks, *, row_off, n_blocks, tq, prev_index, mask_first):
    assert row_off % tq == 0
    off = row_off // tq
    n_keys = WINDOW + tq
    slopes = 2.0 ** (-8.0 * np.arange(1, SWA_HEADS + 1, dtype=np.float64) / SWA_HEADS)
    i = np.arange(tq)[:, None]
    j = np.arange(n_keys)[None, :]
    dist = i + WINDOW - j
    valid = (dist >= 0) & (dist <= WINDOW)
    bias = np.stack([np.concatenate([np.where(valid, -slopes[kh * SWA_GROUP + g] * dist, NEG_BIG)
                                     for g in range(SWA_GROUP)], axis=0)
                     for kh in range(SWA_KV_HEADS)]).astype(np.float32)
    sink_col = jnp.repeat(sinks.reshape(SWA_KV_HEADS, SWA_GROUP), tq, axis=1)[..., None].astype(F32)
    own = lambda w: pl.BlockSpec((tq, w), lambda b: (off + b, 0))
    prev = pl.BlockSpec((WINDOW, KV_WIDTH), lambda b: (prev_index(b), 0))
    in_specs = [own(MAIN_WIDTH), prev, own(KV_WIDTH), prev, own(KV_WIDTH),
                _const(bias.shape), _const(sink_col.shape), _const((SWA_KV_HEADS, 1, KV_WIDTH))]
    args = [q, k_prev, k_own, v_prev, v_own, jnp.asarray(bias), sink_col, _head_masks(SWA_KV_HEADS)]
    return pl.pallas_call(
        functools.partial(_swa_kernel, tq=tq, mask_first=mask_first),
        grid=(n_blocks,),
        in_specs=in_specs,
        out_specs=pl.BlockSpec((tq, MAIN_WIDTH), lambda b: (b, 0)),
        out_shape=jax.ShapeDtypeStruct((n_blocks * tq, MAIN_WIDTH), F32),
        compiler_params=_params(("arbitrary",)),
        name="swa",
    )(*args)


def kernel(x_prompt, x_sample, state_gla, cache_win_k, cache_win_v, cache_mem_k, cache_mem_v, mem_prompt, norm_mix_g, norm_ffn_g, norm_mem_g, w_mem_kv, mem_qn_g, mem_kn_g, w_out, w_in_a, w_gate_lr, b_gate_lr, gla_norm_g, w_in_b, swa_qn_g, swa_sinks, norm_kv_g, w_kv, swa_kn_g, w_router_group, b_router_group, w_router_expert, b_router_expert, w_exp_gate, w_exp_up, w_exp_down):
    bp, tp, d = x_prompt.shape
    bs, ts, _ = x_sample.shape
    assert bp == 1 and tp % WINDOW == 0 and ts * (GLA_CHUNK // ts) == GLA_CHUNK
    n_p, n_s = bp * tp, bs * ts
    w_buf = cache_win_k.shape[1]
    assert w_buf == WINDOW
    h = jnp.concatenate([x_prompt.reshape(n_p, d), x_sample.reshape(n_s, d)], axis=0)

    mem_k_p, mem_v_p = _mem_kv(mem_prompt, norm_mem_g, w_mem_kv, mem_kn_g)
    cmk = cache_mem_k.reshape(cache_mem_k.shape[0], bs, MEM_LEN, MEM_WIDTH)
    cmv = cache_mem_v.reshape(cache_mem_v.shape[0], bs, MEM_LEN, MEM_WIDTH)

    def mem_attend(mq, l):
        tm_p = _row_tile(tp, 256)
        mo_p = _mem_attn(mq, mem_k_p[l], mem_v_p[l], mem_qn_g[l], row_off=0, seq=tp, tm=tm_p, bb=1)
        mo_s = _mem_attn(mq, cmk[l], cmv[l], mem_qn_g[l], row_off=n_p, seq=ts, tm=ts, bb=8)
        return mo_p, mo_s

    def moe(h, l):
        return _moe(h, norm_ffn_g[l], w_router_group[l], b_router_group[l], w_router_expert[l],
                    b_router_expert[l], w_exp_gate[l], w_exp_up[l], w_exp_down[l])

    q, k, la, v, og, mq = _inproj_a(h, norm_mix_g[0], w_in_a[0], w_gate_lr[0], b_gate_lr[0])
    zero_state = jnp.zeros((bp, GLA_HEADS, GLA_DK, GLA_DV), F32)
    n_sub = max(1, min(4, tp // GLA_CHUNK))
    main_p, gla_p = _gla(q, k, la, v, og, zero_state, gla_norm_g[0], row_off=0, seq=tp, n_seg=1, n_sub=n_sub)
    main_s, gla_s = _gla(q, k, la, v, og, state_gla[0], gla_norm_g[0], row_off=n_p, seq=ts,
                         n_seg=GLA_CHUNK // ts, n_sub=1)
    mo_p, mo_s = mem_attend(mq, 0)
    w_o = w_out[0]
    h = _outproj(h, main_p, mo_p, main_s, mo_s, w_o[:MAIN_WIDTH].reshape(GLA_HEADS, GLA_DV, d),
                 w_o[MAIN_WIDTH:])
    h = moe(h, 0)

    q, mq, k_sh, v_sh = _inproj_b(h, norm_kv_g, norm_mix_g[1], w_kv, w_in_b[0], swa_kn_g, swa_qn_g[0])
    ck = cache_win_k.reshape(bs * w_buf, KV_WIDTH)
    cv = cache_win_v.reshape(bs * w_buf, KV_WIDTH)
    main_p = _swa(q, k_sh, v_sh, k_sh, v_sh, swa_sinks[0], row_off=0, n_blocks=n_p // WINDOW, tq=WINDOW,
                  prev_index=lambda b: jnp.maximum(b - 1, 0), mask_first=True)
    main_s = _swa(q, ck, cv, k_sh, v_sh, swa_sinks[0], row_off=n_p, n_blocks=bs, tq=ts,
                  prev_index=lambda b: b, mask_first=False)
    mo_p, mo_s = mem_attend(mq, 1)
    w_o = w_out[1]
    h = _outproj(h, main_p, mo_p, main_s, mo_s, w_o[:MAIN_WIDTH][_swa_perm()], w_o[MAIN_WIDTH:])
    h = moe(h, 1)

    y_prompt = h[:n_p].reshape(bp, tp, d)
    y_sample = h[n_p:].reshape(bs, ts, d)
    k_new = k_sh[n_p:].reshape(bs, ts, SWA_KV_HEADS, HEAD_DIM)
    v_new = v_sh[n_p:].reshape(bs, ts, SWA_KV_HEADS, HEAD_DIM)
    win_k_s = jnp.concatenate([cache_win_k, k_new], axis=1)[:, -w_buf:]
    win_v_s = jnp.concatenate([cache_win_v, v_new], axis=1)[:, -w_buf:]
    win_k_p = k_sh[n_p - WINDOW:n_p].reshape(bp, WINDOW, SWA_KV_HEADS, HEAD_DIM)
    win_v_p = v_sh[n_p - WINDOW:n_p].reshape(bp, WINDOW, SWA_KV_HEADS, HEAD_DIM)
    mem_shape = (mem_k_p.shape[0], bp, MEM_LEN, MEM_HEADS, HEAD_DIM)
    return (y_prompt, y_sample, gla_p[None], gla_s[None], win_k_p, win_v_p, win_k_s, win_v_s,
            mem_k_p.reshape(mem_shape), mem_v_p.reshape(mem_shape))
```

```python
import functools

import numpy as np
import jax
import jax.numpy as jnp
from jax import lax
from jax.experimental import pallas as pl
from jax.experimental.pallas import tpu as pltpu

F32 = jnp.float32
BF16 = jnp.bfloat16

D_MODEL = 1024
MEM_LEN = 256
MEM_HEADS = 4
HEAD_DIM = 64
MEM_WIDTH = MEM_HEADS * HEAD_DIM
MAIN_WIDTH = D_MODEL - MEM_WIDTH
GLA_HEADS = 4
GLA_DV = MAIN_WIDTH // GLA_HEADS
GLA_DK = GLA_DV // 2
GLA_DK_PAD = 128
GLA_KEY_WIDTH = GLA_HEADS * GLA_DK
GLA_KEY_PAD = GLA_HEADS * GLA_DK_PAD
GLA_GATE_RANK = 16
GLA_TAU = 16.0
GLA_CHUNK = 64
SWA_HEADS = MAIN_WIDTH // HEAD_DIM
SWA_KV_HEADS = 4
SWA_GROUP = SWA_HEADS // SWA_KV_HEADS
KV_WIDTH = SWA_KV_HEADS * HEAD_DIM
WINDOW = 128
N_GROUPS = 4
EXPERTS_PER_GROUP = 8
N_EXPERTS = N_GROUPS * EXPERTS_PER_GROUP
D_EXPERT = 512
EPS = 1e-6
LANES = 128
NEG_BIG = -1e30
VMEM_LIMIT = 56 * 1024 * 1024
MOE_TILE = 256
ROUTER_LANE0 = N_GROUPS
ROUTER_META_ROWS = 8
SLABS = D_MODEL // LANES


def _bdot(a, b):
    return jnp.dot(a.astype(BF16), b.astype(BF16), preferred_element_type=F32)


def _bdot_nt(a, b):
    return lax.dot_general(a.astype(BF16), b.astype(BF16), (((1,), (1,)), ((), ())),
                           preferred_element_type=F32)


def _bdot_tn(a, b):
    return lax.dot_general(a.astype(BF16), b.astype(BF16), (((0,), (0,)), ((), ())),
                           preferred_element_type=F32)


def _split(x, n):
    parts = []
    for _ in range(n - 1):
        p = x.astype(BF16)
        parts.append(p)
        x = x - p.astype(F32)
    parts.append(x.astype(BF16))
    return parts


def _exact_left_dot(m, x, n=3):
    out = None
    for p in _split(x, n):
        t = jnp.dot(m, p, preferred_element_type=F32)
        out = t if out is None else out + t
    return out


def _seg_mean(x2, bd):
    out = None
    for p in _split(x2, 2):
        t = jnp.dot(p, bd, preferred_element_type=F32)
        out = t if out is None else out + t
    return out


def _rms_scale(x):
    return lax.rsqrt(jnp.mean(x * x, axis=-1, keepdims=True) + EPS)


def _row_tile(n, cap=512):
    t = cap
    while t > 8 and n % t:
        t //= 2
    assert n % t == 0, n
    return t


def _params(sem):
    return pltpu.CompilerParams(dimension_semantics=sem, vmem_limit_bytes=VMEM_LIMIT)


def _const(shape):
    nd = len(shape)
    return pl.BlockSpec(shape, lambda *_: (0,) * nd)


def _block_diag_mean(width):
    i = np.arange(width)
    return jnp.asarray((i[:, None] // HEAD_DIM == i[None, :] // HEAD_DIM) / HEAD_DIM, BF16)


def _head_masks(n_heads):
    i = np.arange(n_heads * HEAD_DIM)
    return jnp.asarray((i[None, :] // HEAD_DIM == np.arange(n_heads)[:, None]), F32)[:, None, :]


def _mem_kv_kernel(mem_ref, g_ref, w_ref, kng_ref, bd_ref, k_ref, v_ref):
    x = mem_ref[0]
    hn = x * _rms_scale(x) * g_ref[0]
    kv = _bdot(hn, w_ref[0])
    k = kv[:, :MEM_WIDTH]
    k = k * lax.rsqrt(_seg_mean(k * k, bd_ref[...]) + EPS) * kng_ref[0]
    k_ref[0, 0] = k
    v_ref[0, 0] = kv[:, MEM_WIDTH:]


def _mem_kv(mem, g, w, kng):
    depth, (b, m, d) = w.shape[0], mem.shape
    out = jax.ShapeDtypeStruct((depth, b, m, MEM_WIDTH), F32)
    blk = pl.BlockSpec((1, 1, m, MEM_WIDTH), lambda l, i: (l, i, 0, 0))
    return pl.pallas_call(
        _mem_kv_kernel,
        grid=(depth, b),
        in_specs=[pl.BlockSpec((1, m, d), lambda l, i: (i, 0, 0)),
                  pl.BlockSpec((1, 1, d), lambda l, i: (l, 0, 0)),
                  pl.BlockSpec((1, d, 2 * MEM_WIDTH), lambda l, i: (l, 0, 0)),
                  pl.BlockSpec((1, 1, MEM_WIDTH), lambda l, i: (l, 0, 0)),
                  _const((MEM_WIDTH, MEM_WIDTH))],
        out_specs=[blk, blk],
        out_shape=[out, out],
        compiler_params=_params(("arbitrary", "arbitrary")),
        name="mem_kv",
    )(mem, g.reshape(depth, 1, d), w.astype(BF16),
      jnp.tile(kng, (1, MEM_HEADS)).reshape(depth, 1, MEM_WIDTH), _block_diag_mean(MEM_WIDTH))


def _inproj_a_kernel(x_ref, g_ref, wq_ref, wk_ref, wv_ref, wog_ref, wlr_ref, wmq_ref, wgl_ref, bgl_ref,
                     q_ref, k_ref, la_ref, v_ref, og_ref, mq_ref):
    x = x_ref[...]
    hn = (x * _rms_scale(x) * g_ref[...]).astype(BF16)
    q_ref[...] = jnp.dot(hn, wq_ref[...], preferred_element_type=F32) * (GLA_DK ** -0.5)
    k_ref[...] = jnp.dot(hn, wk_ref[...], preferred_element_type=F32)
    for h in range(GLA_HEADS):
        v_ref[h] = jnp.dot(hn, wv_ref[h], preferred_element_type=F32)
        og_ref[h] = jnp.dot(hn, wog_ref[h], preferred_element_type=F32)
    lr = jnp.dot(hn, wlr_ref[...], preferred_element_type=F32)
    z = _bdot(lr, wgl_ref[...]) + bgl_ref[...]
    la_ref[...] = (jnp.minimum(z, 0.0) - jnp.log(1.0 + jnp.exp(-jnp.abs(z)))) * (1.0 / GLA_TAU)
    mq_ref[...] = jnp.dot(hn, wmq_ref[...], preferred_element_type=F32)


def _pad_heads(w, width, pad):
    lead = w.shape[:-1]
    w = w.reshape(*lead, GLA_HEADS, width)
    w = jnp.pad(w, [(0, 0)] * len(lead) + [(0, 0), (0, pad - width)])
    return w.reshape(*lead, GLA_HEADS * pad)


def _inproj_a(x, g, w_in, w_lr, b_lr):
    n, d = x.shape
    tm = _row_tile(n)
    c0, c1, c2, c3, c4 = (GLA_KEY_WIDTH, 2 * GLA_KEY_WIDTH, 2 * GLA_KEY_WIDTH + MAIN_WIDTH,
                          2 * GLA_KEY_WIDTH + 2 * MAIN_WIDTH,
                          2 * GLA_KEY_WIDTH + 2 * MAIN_WIDTH + GLA_GATE_RANK)
    wb = w_in.astype(BF16)
    wq = _pad_heads(wb[:, :c0], GLA_DK, GLA_DK_PAD)
    wk = _pad_heads(wb[:, c0:c1], GLA_DK, GLA_DK_PAD)
    wv = wb[:, c1:c2].reshape(d, GLA_HEADS, GLA_DV).transpose(1, 0, 2)
    wog = wb[:, c2:c3].reshape(d, GLA_HEADS, GLA_DV).transpose(1, 0, 2)
    wlr = jnp.pad(wb[:, c3:c4], ((0, 0), (0, LANES - GLA_GATE_RANK)))
    wmq = wb[:, c4:]
    wgl = jnp.pad(_pad_heads(w_lr.astype(BF16), GLA_DK, GLA_DK_PAD), ((0, LANES - GLA_GATE_RANK), (0, 0)))
    bgl = _pad_heads(b_lr.reshape(1, -1), GLA_DK, GLA_DK_PAD)
    row = lambda w: pl.BlockSpec((tm, w), lambda i: (i, 0))
    hrow = pl.BlockSpec((GLA_HEADS, tm, GLA_DV), lambda i: (0, i, 0))
    key = jax.ShapeDtypeStruct((n, GLA_KEY_PAD), F32)
    val = jax.ShapeDtypeStruct((GLA_HEADS, n, GLA_DV), F32)
    return pl.pallas_call(
        _inproj_a_kernel,
        grid=(n // tm,),
        in_specs=[row(d), _const((1, d)), _const(wq.shape), _const(wk.shape), _const(wv.shape),
                  _const(wog.shape), _const(wlr.shape), _const(wmq.shape), _const(wgl.shape),
                  _const(bgl.shape)],
        out_specs=[row(GLA_KEY_PAD), row(GLA_KEY_PAD), row(GLA_KEY_PAD), hrow, hrow, row(MEM_WIDTH)],
        out_shape=[key, key, key, val, val, jax.ShapeDtypeStruct((n, MEM_WIDTH), F32)],
        compiler_params=_params(("parallel",)),
        name="inproj_a",
    )(x, g.reshape(1, d), wq, wk, wv, wog, wlr, wmq, wgl, bgl)


def _gla_kernel(q_ref, k_ref, la_ref, v_ref, og_ref, s0_ref, gn_ref, mcum_ref, mall_ref, sel_ref,
                o_ref, sout_ref, s_ref, *, chunk, n_sub, n_seg):
    j = pl.program_id(1)
    seg = chunk // n_seg

    @pl.when(j == 0)
    def _():
        s_ref[...] = jnp.zeros_like(s_ref)
        s_ref[:, :, :GLA_DK, :] = s0_ref[...]

    mcum = mcum_ref[...]
    causal = mcum.astype(F32) > 0.0
    row = lax.broadcasted_iota(jnp.int32, (chunk, GLA_DK_PAD), 0)
    gn = gn_ref[...]
    for c in range(n_sub):
        rows = slice(c * chunk, (c + 1) * chunk)
        la = la_ref[rows, :]
        b = _exact_left_dot(mcum, la)
        b_end = _exact_left_dot(mall_ref[...], la)
        e_end = jnp.exp(_exact_left_dot(sel_ref[...], la)).T
        q = q_ref[rows, :]
        k = k_ref[rows, :]
        qt = q * jnp.exp(b)
        kt = (k * jnp.exp(-b)).astype(BF16)
        kd = k * jnp.exp(b_end - b)
        for h in range(GLA_HEADS):
            cols = slice(h * GLA_DK_PAD, (h + 1) * GLA_DK_PAD)
            qh = qt[:, cols]
            vb = v_ref[h, rows, :].astype(BF16)
            a = jnp.where(causal, _bdot_nt(qh, kt[:, cols]), 0.0)
            o = _bdot(a, vb)
            inter = []
            for s in range(n_seg):
                st = s_ref[s, h]
                inter.append(_bdot(qh[s * seg:(s + 1) * seg], st))
                kds = kd[:, cols]
                if n_seg > 1:
                    kds = jnp.where((row >= s * seg) & (row < (s + 1) * seg), kds, 0.0)
                s_ref[s, h] = e_end[cols, s:s + 1] * st + _bdot_tn(kds, vb)
            o = o + (inter[0] if n_seg == 1 else jnp.concatenate(inter, axis=0))
            on = o * lax.rsqrt(jnp.mean(o * o, axis=-1, keepdims=True) + EPS) * gn
            og = og_ref[h, rows, :]
            o_ref[h, rows, :] = on * (og * jax.nn.sigmoid(og))

    @pl.when(j == pl.num_programs(1) - 1)
    def _():
        sout_ref[...] = s_ref[:, :, :GLA_DK, :]


def _gla(q, k, la, v, og, s0, gnorm, *, row_off, seq, n_seg, n_sub):
    batch = s0.shape[0]
    chunk = GLA_CHUNK
    assert chunk % n_seg == 0 and batch % n_seg == 0
    seg = chunk // n_seg
    step_rows = n_sub * chunk
    if n_seg > 1:
        assert seq == seg and n_sub == 1
        t_steps = 1
    else:
        assert seq % step_rows == 0
        t_steps = seq // step_rows
    assert row_off % step_rows == 0
    off = row_off // step_rows
    i = np.arange(chunk)
    same = (i[:, None] // seg) == (i[None, :] // seg)
    mcum = jnp.asarray(same & (i[None, :] <= i[:, None]), BF16)
    mall = jnp.asarray(same, BF16)
    sel = jnp.asarray((i[None, :] // seg) == np.arange(LANES)[:, None], BF16)
    ridx = lambda g, j: (off + g * t_steps + j, 0)
    hidx = lambda g, j: (0, off + g * t_steps + j, 0)
    key_spec = pl.BlockSpec((step_rows, GLA_KEY_PAD), ridx)
    val_spec = pl.BlockSpec((GLA_HEADS, step_rows, GLA_DV), hidx)
    st_spec = pl.BlockSpec((n_seg, GLA_HEADS, GLA_DK, GLA_DV), lambda g, j: (g, 0, 0, 0))
    in_specs = [key_spec, key_spec, key_spec, val_spec, val_spec, st_spec, _const((1, GLA_DV)),
                _const((chunk, chunk)), _const((chunk, chunk)), _const((LANES, chunk))]
    args = [q, k, la, v, og, s0, gnorm.reshape(1, GLA_DV), mcum, mall, sel]
    out_spec = pl.BlockSpec((GLA_HEADS, step_rows, GLA_DV), lambda g, j: (0, g * t_steps + j, 0))
    return pl.pallas_call(
        functools.partial(_gla_kernel, chunk=chunk, n_sub=n_sub, n_seg=n_seg),
        grid=(batch // n_seg, t_steps),
        in_specs=in_specs,
        out_specs=[out_spec, st_spec],
        out_shape=[jax.ShapeDtypeStruct((GLA_HEADS, batch * seq, GLA_DV), F32),
                   jax.ShapeDtypeStruct(s0.shape, F32)],
        scratch_shapes=[pltpu.VMEM((n_seg, GLA_HEADS, GLA_DK_PAD, GLA_DV), F32)],
        compiler_params=_params(("arbitrary", "arbitrary")),
        name="gla",
    )(*args)


def _mem_attn_kernel(q_ref, k_ref, v_ref, g_ref, bd_ref, hm_ref, o_ref, *, tm, bb):
    g = g_ref[...]
    for i in range(bb):
        rows = slice(i * tm, (i + 1) * tm)
        q = q_ref[rows, :]
        qn = q * lax.rsqrt(_seg_mean(q * q, bd_ref[...]) + EPS) * g
        qs = jnp.concatenate([(qn * hm_ref[h]).astype(BF16) for h in range(MEM_HEADS)], axis=0)
        s = _bdot_nt(qs, k_ref[i])
        e = jnp.exp(s - jnp.max(s, axis=-1, keepdims=True))
        p = e / jnp.sum(e, axis=-1, keepdims=True)
        o = _bdot(p, v_ref[i])
        acc = o[:tm] * hm_ref[0]
        for h in range(1, MEM_HEADS):
            acc = acc + o[h * tm:(h + 1) * tm] * hm_ref[h]
        o_ref[rows, :] = acc


def _mem_attn(mq, mk, mv, qng, *, row_off, seq, tm, bb):
    batch, m, _ = mk.shape
    assert seq % tm == 0 and batch % bb == 0 and (bb == 1 or seq == tm)
    t_steps = seq // tm
    step_rows = bb * tm
    assert row_off % step_rows == 0
    off = row_off // step_rows
    row_spec = pl.BlockSpec((step_rows, MEM_WIDTH), lambda g, j: (off + g * t_steps + j, 0))
    kv_spec = pl.BlockSpec((bb, m, MEM_WIDTH), lambda g, j: (g, 0, 0))
    in_specs = [row_spec, kv_spec, kv_spec, _const((1, MEM_WIDTH)), _const((MEM_WIDTH, MEM_WIDTH)),
                _const((MEM_HEADS, 1, MEM_WIDTH))]
    args = [mq, mk, mv, (jnp.tile(qng, MEM_HEADS) * HEAD_DIM ** -0.5).reshape(1, MEM_WIDTH),
            _block_diag_mean(MEM_WIDTH), _head_masks(MEM_HEADS)]
    return pl.pallas_call(
        functools.partial(_mem_attn_kernel, tm=tm, bb=bb),
        grid=(batch // bb, t_steps),
        in_specs=in_specs,
        out_specs=pl.BlockSpec((step_rows, MEM_WIDTH), lambda g, j: (g * t_steps + j, 0)),
        out_shape=jax.ShapeDtypeStruct((batch * seq, MEM_WIDTH), F32),
        compiler_params=_params(("parallel", "parallel")),
        name="mem_attn",
    )(*args)


def _outproj_kernel(h_ref, main_p_ref, mo_p_ref, main_s_ref, mo_s_ref, wmain_ref, wmo_ref, o_ref, *,
                    heads, prompt_tiles):
    def project(main_ref, mo_ref):
        acc = h_ref[...] + _bdot(mo_ref[...], wmo_ref[...])
        if heads:
            for h in range(heads):
                acc = acc + _bdot(main_ref[h], wmain_ref[h])
        else:
            acc = acc + _bdot(main_ref[...], wmain_ref[...])
        o_ref[...] = acc

    @pl.when(pl.program_id(0) < prompt_tiles)
    def _():
        project(main_p_ref, mo_p_ref)

    @pl.when(pl.program_id(0) >= prompt_tiles)
    def _():
        project(main_s_ref, mo_s_ref)


def _outproj(h, main_p, mo_p, main_s, mo_s, w_main, w_mo):
    n, d = h.shape
    n_p, n_s = mo_p.shape[0], mo_s.shape[0]
    tm = _row_tile(n_s)
    assert n_p % tm == 0 and n_p + n_s == n
    pt = n_p // tm
    heads = main_p.shape[0] if main_p.ndim == 3 else 0
    p_idx = lambda i: jnp.minimum(i, pt - 1)
    s_idx = lambda i: jnp.maximum(i - pt, 0)
    if heads:
        main_spec = lambda idx: pl.BlockSpec((heads, tm, main_p.shape[2]), lambda i: (0, idx(i), 0))
    else:
        main_spec = lambda idx: pl.BlockSpec((tm, main_p.shape[1]), lambda i: (idx(i), 0))
    mo_spec = lambda idx: pl.BlockSpec((tm, MEM_WIDTH), lambda i: (idx(i), 0))
    row = pl.BlockSpec((tm, d), lambda i: (i, 0))
    return pl.pallas_call(
        functools.partial(_outproj_kernel, heads=heads, prompt_tiles=pt),
        grid=(n // tm,),
        in_specs=[row, main_spec(p_idx), mo_spec(p_idx), main_spec(s_idx), mo_spec(s_idx),
                  _const(w_main.shape), _const(w_mo.shape)],
        out_specs=row,
        out_shape=jax.ShapeDtypeStruct((n, d), F32),
        compiler_params=_params(("parallel",)),
        name="outproj",
    )(h, main_p, mo_p, main_s, mo_s, w_main.astype(BF16), w_mo.astype(BF16))


def _router_kernel(h_ref, g_ref, whi_ref, wlo_ref, b_ref, tril_ref, mi_ref, mf_ref, cnt_ref, carry_ref):
    i = pl.program_id(0)

    @pl.when(i == 0)
    def _():
        carry_ref[...] = jnp.zeros_like(carry_ref)

    x = h_ref[...]
    xn = x * _rms_scale(x) * g_ref[...]
    x_hi, x_lo = _split(xn, 2)
    logits = (jnp.dot(x_hi, whi_ref[...], preferred_element_type=F32)
              + jnp.dot(x_hi, wlo_ref[...], preferred_element_type=F32)
              + jnp.dot(x_lo, whi_ref[...], preferred_element_type=F32)) + b_ref[...]
    tm = x.shape[0]
    lane = lax.broadcasted_iota(jnp.int32, (tm, LANES), 1)
    far = jnp.int32(2 * LANES)

    def first_max(vals):
        m = jnp.max(vals, axis=-1, keepdims=True)
        return m, jnp.min(jnp.where(vals == m, lane, far), axis=-1, keepdims=True)

    gl = jnp.where(lane < N_GROUPS, logits, -jnp.inf)
    gmax, grp = first_max(gl)
    pg_sel = 1.0 / jnp.sum(jnp.exp(gl - gmax), axis=-1, keepdims=True)
    lo = ROUTER_LANE0 + grp * EXPERTS_PER_GROUP
    el = jnp.where((lane >= lo) & (lane < lo + EXPERTS_PER_GROUP), logits, -jnp.inf)
    m1, i1 = first_max(el)
    m2, i2 = first_max(jnp.where(lane == i1, -jnp.inf, el))
    e2 = jnp.exp(m2 - m1)
    g1 = pg_sel / (1.0 + e2)
    g2 = pg_sel * e2 / (1.0 + e2)

    oh1 = lane == i1
    oh2 = lane == i2
    picked = jnp.where(oh1 | oh2, 1.0, 0.0)
    before = jnp.dot(tril_ref[...], picked.astype(BF16), preferred_element_type=F32) + carry_ref[...]
    rank1 = jnp.sum(jnp.where(oh1, before, 0.0), axis=-1, keepdims=True)
    rank2 = jnp.sum(jnp.where(oh2, before, 0.0), axis=-1, keepdims=True)
    carry = carry_ref[...] + jnp.sum(picked, axis=0, keepdims=True)
    carry_ref[...] = carry
    cnt_ref[...] = carry

    zi = jnp.zeros((tm, LANES), jnp.int32)
    mi = jnp.where(lane == 0, i1 - ROUTER_LANE0, zi)
    mi = jnp.where(lane == 1, i2 - ROUTER_LANE0, mi)
    mi = jnp.where(lane == 2, rank1.astype(jnp.int32), mi)
    mi = jnp.where(lane == 3, rank2.astype(jnp.int32), mi)
    mi_ref[...] = mi.T[:ROUTER_META_ROWS]
    zf = jnp.zeros((tm, LANES), F32)
    mf_ref[...] = jnp.where(lane == 0, g1, jnp.where(lane == 1, g2, zf))


def _router(h, g, w_rg, b_rg, w_re, b_re):
    n, d = h.shape
    tm = _row_tile(n)
    n_real = N_GROUPS + N_EXPERTS
    w = jnp.pad(jnp.concatenate([w_rg, w_re], axis=1), ((0, 0), (0, LANES - n_real)))
    b = jnp.pad(jnp.concatenate([b_rg, b_re]), (0, LANES - n_real)).reshape(1, LANES)
    w_hi = w.astype(BF16)
    w_lo = (w - w_hi.astype(F32)).astype(BF16)
    i = np.arange(tm)
    tril = jnp.asarray(i[None, :] < i[:, None], BF16)
    row = lambda width: pl.BlockSpec((tm, width), lambda i: (i, 0))
    return pl.pallas_call(
        _router_kernel,
        grid=(n // tm,),
        in_specs=[row(d), _const((1, d)), _const((d, LANES)), _const((d, LANES)), _const((1, LANES)),
                  _const((tm, tm))],
        out_specs=[pl.BlockSpec((ROUTER_META_ROWS, tm), lambda i: (0, i)), row(LANES), _const((1, LANES))],
        out_shape=[jax.ShapeDtypeStruct((ROUTER_META_ROWS, n), jnp.int32),
                   jax.ShapeDtypeStruct((n, LANES), F32), jax.ShapeDtypeStruct((1, LANES), F32)],
        scratch_shapes=[pltpu.VMEM((1, LANES), F32)],
        compiler_params=_params(("arbitrary",)),
        name="moe_router",
    )(h, g.reshape(1, d), w_hi, w_lo, b, tril)


def _to_rows(ref, rows, lead=()):
    return jnp.concatenate([ref[lead + (pl.ds(s, rows, stride=SLABS), slice(None))] for s in range(SLABS)],
                           axis=1)


def _from_rows(ref, x, rows, lead=()):
    for s in range(SLABS):
        ref[lead + (pl.ds(s, rows, stride=SLABS), slice(None))] = x[:, s * LANES:(s + 1) * LANES]


def _dispatch_kernel(pos_ref, h_ref, g_ref, xs_hbm, buf, sem, *, tm, n, steps):
    i = pl.program_id(0)
    slot = lax.rem(i, 2)

    def wait_slot(sl):
        for _ in range(2):
            pltpu.make_async_copy(buf.at[sl], xs_hbm.at[pl.ds(0, tm * SLABS)], sem.at[sl]).wait()

    @pl.when(i >= 2)
    def _():
        wait_slot(slot)

    x = h_ref[...]
    _from_rows(buf, x * _rms_scale(x) * g_ref[...], tm, (slot,))

    def issue(r, carry):
        src = buf.at[slot, pl.ds(pl.multiple_of(r * SLABS, SLABS), SLABS)]
        for kk in range(2):
            p = pos_ref[kk * n + i * tm + r]
            dst = xs_hbm.at[pl.ds(pl.multiple_of(p * SLABS, SLABS), SLABS)]
            pltpu.make_async_copy(src, dst, sem.at[slot]).start()
        return carry

    lax.fori_loop(0, tm, issue, 0)

    @pl.when(i == steps - 1)
    def _():
        wait_slot(slot)
        if steps > 1:
            wait_slot(1 - slot)


def _dispatch(h, g, pos_flat):
    n, d = h.shape
    tm = _row_tile(n, MOE_TILE)
    steps = n // tm
    return pl.pallas_call(
        functools.partial(_dispatch_kernel, tm=tm, n=n, steps=steps),
        grid_spec=pltpu.PrefetchScalarGridSpec(
            num_scalar_prefetch=1,
            grid=(steps,),
            in_specs=[pl.BlockSpec((tm, d), lambda i, pos: (i, 0)),
                      pl.BlockSpec((1, d), lambda i, pos: (0, 0))],
            out_specs=pl.BlockSpec(memory_space=pl.ANY),
            scratch_shapes=[pltpu.VMEM((2, tm * SLABS, LANES), F32), pltpu.SemaphoreType.DMA((2,))],
        ),
        out_shape=jax.ShapeDtypeStruct((2 * n * SLABS, LANES), F32),
        compiler_params=_params(("arbitrary",)),
        name="moe_dispatch",
    )(pos_flat, h, g.reshape(1, d))


ITEM_FIRST, ITEM_LAST, ITEM_NEW_EXPERT = 1, 2, 4


def _expert_kernel(tile_ref, exp_ref, lo_ref, hi_ref, flag_ref, n_ref, xs_ref, wg_ref, wu_ref, wd_ref, ys_ref,
                   wgb, wub, wdb, acc, *, tm):
    w = pl.program_id(0)

    @pl.when(w < n_ref[0])
    def _():
        flags = flag_ref[w]

        @pl.when((flags & ITEM_NEW_EXPERT) != 0)
        def _():
            wgb[...] = wg_ref[0, 0].astype(BF16)
            wub[...] = wu_ref[0, 0].astype(BF16)
            wdb[...] = wd_ref[0, 0].astype(BF16)

        x = _to_rows(xs_ref, tm).astype(BF16)
        hg = jnp.dot(x, wgb[...], preferred_element_type=F32)
        hu = jnp.dot(x, wub[...], preferred_element_type=F32)
        act = (hg * jax.nn.sigmoid(hg) * hu).astype(BF16)
        y = jnp.dot(act, wdb[...], preferred_element_type=F32)
        row = lax.broadcasted_iota(jnp.int32, (tm, 1), 0)
        y = jnp.where((row >= lo_ref[w]) & (row < hi_ref[w]), y, 0.0)
        first = (flags & ITEM_FIRST) != 0

        @pl.when(first)
        def _():
            acc[...] = y

        @pl.when(jnp.logical_not(first))
        def _():
            acc[...] += y

        @pl.when((flags & ITEM_LAST) != 0)
        def _():
            _from_rows(ys_ref, acc[...], tm)


def _experts(xs, items, w_g, w_u, w_d, layer):
    tile, expert, lo, hi, flags, n_items = items
    d = D_MODEL
    tm = MOE_TILE
    tok_spec = pl.BlockSpec((tm * SLABS, LANES), lambda w, tile, *_: (tile[w], 0))
    wmap = lambda w, tile, expert, *_: (layer, expert[w], 0, 0)
    return pl.pallas_call(
        functools.partial(_expert_kernel, tm=tm),
        grid_spec=pltpu.PrefetchScalarGridSpec(
            num_scalar_prefetch=6,
            grid=(tile.shape[0],),
            in_specs=[tok_spec,
                      pl.BlockSpec((1, 1, d, D_EXPERT), wmap),
                      pl.BlockSpec((1, 1, d, D_EXPERT), wmap),
                      pl.BlockSpec((1, 1, D_EXPERT, d), wmap)],
            out_specs=tok_spec,
            scratch_shapes=[pltpu.VMEM((d, D_EXPERT), BF16), pltpu.VMEM((d, D_EXPERT), BF16),
                            pltpu.VMEM((D_EXPERT, d), BF16), pltpu.VMEM((tm, d), F32)],
        ),
        out_shape=jax.ShapeDtypeStruct(xs.shape, F32),
        compiler_params=_params(("arbitrary",)),
        name="moe_experts",
    )(tile, expert, lo, hi, flags, n_items, xs, w_g, w_u, w_d)


def _combine_kernel(pos_ref, h_ref, gate_ref, ys_hbm, o_ref, buf, sem, *, tm, n, steps):
    i = pl.program_id(0)
    slot = lax.rem(i, 2)

    def issue(step, sl):
        def body(r, carry):
            for kk in range(2):
                p = pos_ref[kk * n + step * tm + r]
                src = ys_hbm.at[pl.ds(pl.multiple_of(p * SLABS, SLABS), SLABS)]
                dst = buf.at[sl, kk, pl.ds(pl.multiple_of(r * SLABS, SLABS), SLABS)]
                pltpu.make_async_copy(src, dst, sem.at[sl]).start()
            return carry

        lax.fori_loop(0, tm, body, 0)

    @pl.when(i == 0)
    def _():
        issue(0, 0)

    @pl.when(i + 1 < steps)
    def _():
        issue(i + 1, 1 - slot)

    for kk in range(2):
        pltpu.make_async_copy(ys_hbm.at[pl.ds(0, tm * SLABS)], buf.at[slot, kk], sem.at[slot]).wait()
    g = gate_ref[...]
    o_ref[...] = (h_ref[...] + g[:, 0:1] * _to_rows(buf, tm, (slot, 0))
                  + g[:, 1:2] * _to_rows(buf, tm, (slot, 1)))


def _combine(h, gates, ys, pos_flat):
    n, d = h.shape
    tm = _row_tile(n, MOE_TILE)
    steps = n // tm
    return pl.pallas_call(
        functools.partial(_combine_kernel, tm=tm, n=n, steps=steps),
        grid_spec=pltpu.PrefetchScalarGridSpec(
            num_scalar_prefetch=1,
            grid=(steps,),
            in_specs=[pl.BlockSpec((tm, d), lambda i, pos: (i, 0)),
                      pl.BlockSpec((tm, LANES), lambda i, pos: (i, 0)),
                      pl.BlockSpec(memory_space=pl.ANY)],
            out_specs=pl.BlockSpec((tm, d), lambda i, pos: (i, 0)),
            scratch_shapes=[pltpu.VMEM((2, 2, tm * SLABS, LANES), F32), pltpu.SemaphoreType.DMA((2,))],
        ),
        out_shape=jax.ShapeDtypeStruct((n, d), F32),
        compiler_params=_params(("arbitrary",)),
        name="moe_combine",
    )(pos_flat, h, gates, ys)


def _work_items(counts, n_slots):
    tm = MOE_TILE
    ends = jnp.cumsum(counts)
    starts = ends - counts
    first_tile = starts // tm
    n_tiles_e = jnp.where(counts > 0, (ends - 1) // tm - first_tile + 1, 0)
    item_end = jnp.cumsum(n_tiles_e)
    item_start = item_end - n_tiles_e
    n_items = item_end[-1]
    max_items = n_slots // tm + N_EXPERTS - 1
    w = jnp.minimum(jnp.arange(max_items, dtype=jnp.int32), n_items - 1)
    expert = jnp.sum(w[:, None] >= item_end[None, :], axis=1).astype(jnp.int32)
    tile = first_tile[expert] + w - item_start[expert]
    lo = jnp.maximum(starts[expert] - tile * tm, 0)
    hi = jnp.minimum(ends[expert] - tile * tm, tm)
    prev_tile = jnp.concatenate([jnp.full((1,), -1, jnp.int32), tile[:-1]])
    next_tile = jnp.concatenate([tile[1:], jnp.full((1,), -1, jnp.int32)])
    prev_expert = jnp.concatenate([jnp.full((1,), -1, jnp.int32), expert[:-1]])
    idx = jnp.arange(max_items, dtype=jnp.int32)
    flags = (jnp.where(tile != prev_tile, ITEM_FIRST, 0)
             | jnp.where((tile != next_tile) | (idx == n_items - 1), ITEM_LAST, 0)
             | jnp.where(expert != prev_expert, ITEM_NEW_EXPERT, 0))
    as_i32 = lambda a: a.astype(jnp.int32)
    return (as_i32(tile), expert, as_i32(lo), as_i32(hi), as_i32(flags), as_i32(n_items).reshape(1))


def _moe(h, g, w_rg, b_rg, w_re, b_re, w_g, w_u, w_d, layer):
    n, _ = h.shape
    meta, gates, cnt = _router(h, g, w_rg, b_rg, w_re, b_re)
    counts = cnt[0, ROUTER_LANE0:ROUTER_LANE0 + N_EXPERTS].astype(jnp.int32)
    starts = jnp.cumsum(counts) - counts
    pos = (starts[meta[0:2]] + meta[2:4]).reshape(-1)
    xs = _dispatch(h, g, pos)
    ys = _experts(xs, _work_items(counts, 2 * n), w_g, w_u, w_d, layer)
    return _combine(h, gates, ys, pos)


def _inproj_b_kernel(x_ref, gkv_ref, gmix_ref, wkv_ref, win_ref, kng_ref, qng_ref, bdk_ref, bdq_ref,
                     q_ref, mq_ref, k_ref, v_ref):
    x = x_ref[...]
    xr = x * _rms_scale(x)
    kv = _bdot(xr * gkv_ref[...], wkv_ref[...])
    k = kv[:, :KV_WIDTH]
    k_ref[...] = k * lax.rsqrt(_seg_mean(k * k, bdk_ref[...]) + EPS) * kng_ref[...]
    v_ref[...] = kv[:, KV_WIDTH:]
    proj = _bdot(xr * gmix_ref[...], win_ref[...])
    q = proj[:, :MAIN_WIDTH]
    q_ref[...] = q * lax.rsqrt(_seg_mean(q * q, bdq_ref[...]) + EPS) * qng_ref[...]
    mq_ref[...] = proj[:, MAIN_WIDTH:]


def _swa_perm():
    g, kh, dd = np.meshgrid(np.arange(SWA_GROUP), np.arange(SWA_KV_HEADS), np.arange(HEAD_DIM), indexing="ij")
    return ((kh * SWA_GROUP + g) * HEAD_DIM + dd).reshape(-1)


def _inproj_b(x, g_kv, g_mix, w_kv, w_in, kng, qng):
    n, d = x.shape
    tm = _row_tile(n)
    perm = _swa_perm()
    w_in_p = jnp.concatenate([w_in[:, :MAIN_WIDTH][:, perm], w_in[:, MAIN_WIDTH:]], axis=1).astype(BF16)
    qng_t = (jnp.tile(qng, SWA_HEADS) * HEAD_DIM ** -0.5).reshape(1, MAIN_WIDTH)
    row = lambda w: pl.BlockSpec((tm, w), lambda i: (i, 0))
    return pl.pallas_call(
        _inproj_b_kernel,
        grid=(n // tm,),
        in_specs=[row(d), _const((1, d)), _const((1, d)), _const((d, 2 * KV_WIDTH)), _const((d, d)),
                  _const((1, KV_WIDTH)), _const((1, MAIN_WIDTH)), _const((KV_WIDTH, KV_WIDTH)),
                  _const((MAIN_WIDTH, MAIN_WIDTH))],
        out_specs=[row(MAIN_WIDTH), row(MEM_WIDTH), row(KV_WIDTH), row(KV_WIDTH)],
        out_shape=[jax.ShapeDtypeStruct((n, MAIN_WIDTH), F32), jax.ShapeDtypeStruct((n, MEM_WIDTH), F32),
                   jax.ShapeDtypeStruct((n, KV_WIDTH), F32), jax.ShapeDtypeStruct((n, KV_WIDTH), F32)],
        compiler_params=_params(("parallel",)),
        name="inproj_b",
    )(x, g_kv.reshape(1, d), g_mix.reshape(1, d), w_kv.astype(BF16), w_in_p,
      jnp.tile(kng, SWA_KV_HEADS).reshape(1, KV_WIDTH), qng_t,
      _block_diag_mean(KV_WIDTH), _block_diag_mean(MAIN_WIDTH))


def _swa_kernel(q_ref, kp_ref, ko_ref, vp_ref, vo_ref, bias_ref, sink_ref, hm_ref, o_ref, *, tq, mask_first):
    kk = jnp.concatenate([kp_ref[...], ko_ref[...]], axis=0).astype(BF16)
    vv = jnp.concatenate([vp_ref[...], vo_ref[...]], axis=0).astype(BF16)
    n_keys = kk.shape[0]
    q = q_ref[...]
    if mask_first:
        key = lax.broadcasted_iota(jnp.int32, (SWA_GROUP * tq, n_keys), 1)
        has_prev = (pl.program_id(0) > 0) | (key >= WINDOW)
    acc = [None] * SWA_GROUP
    for kh in range(SWA_KV_HEADS):
        hm = hm_ref[kh]
        qs = jnp.concatenate([(q[:, g * KV_WIDTH:(g + 1) * KV_WIDTH] * hm).astype(BF16)
                              for g in range(SWA_GROUP)], axis=0)
        s = _bdot_nt(qs, kk) + bias_ref[kh]
        if mask_first:
            s = jnp.where(has_prev, s, NEG_BIG)
        sink = sink_ref[kh]
        m = jnp.maximum(jnp.max(s, axis=-1, keepdims=True), sink)
        e = jnp.exp(s - m)
        p = e / (jnp.sum(e, axis=-1, keepdims=True) + jnp.exp(sink - m))
        o = _bdot(p, vv)
        for g in range(SWA_GROUP):
            t = o[g * tq:(g + 1) * tq] * hm
            acc[g] = t if acc[g] is None else acc[g] + t
    for g in range(SWA_GROUP):
        o_ref[:, g * KV_WIDTH:(g + 1) * KV_WIDTH] = acc[g]


def _swa(q, k_prev, v_prev, k_own, v_own, sinks, *, row_off, n_blocks, tq, prev_index, mask_first):
    assert row_off % tq == 0
    off = row_off // tq
    n_keys = WINDOW + tq
    slopes = 2.0 ** (-8.0 * np.arange(1, SWA_HEADS + 1, dtype=np.float64) / SWA_HEADS)
    i = np.arange(tq)[:, None]
    j = np.arange(n_keys)[None, :]
    dist = i + WINDOW - j
    valid = (dist >= 0) & (dist <= WINDOW)
    bias = np.stack([np.concatenate([np.where(valid, -slopes[kh * SWA_GROUP + g] * dist, NEG_BIG)
                                     for g in range(SWA_GROUP)], axis=0)
                     for kh in range(SWA_KV_HEADS)]).astype(np.float32)
    sink_col = jnp.repeat(sinks.reshape(SWA_KV_HEADS, SWA_GROUP), tq, axis=1)[..., None].astype(F32)
    own = lambda w: pl.BlockSpec((tq, w), lambda b: (off + b, 0))
    prev = pl.BlockSpec((WINDOW, KV_WIDTH), lambda b: (prev_index(b), 0))
    in_specs = [own(MAIN_WIDTH), prev, own(KV_WIDTH), prev, own(KV_WIDTH),
                _const(bias.shape), _const(sink_col.shape), _const((SWA_KV_HEADS, 1, KV_WIDTH))]
    args = [q, k_prev, k_own, v_prev, v_own, jnp.asarray(bias), sink_col, _head_masks(SWA_KV_HEADS)]
    return pl.pallas_call(
        functools.partial(_swa_kernel, tq=tq, mask_first=mask_first),
        grid=(n_blocks,),
        in_specs=in_specs,
        out_specs=pl.BlockSpec((tq, MAIN_WIDTH), lambda b: (b, 0)),
        out_shape=jax.ShapeDtypeStruct((n_blocks * tq, MAIN_WIDTH), F32),
        compiler_params=_params(("arbitrary",)),
        name="swa",
    )(*args)


def kernel(x_prompt, x_sample, state_gla, cache_win_k, cache_win_v, cache_mem_k, cache_mem_v, mem_prompt, norm_mix_g, norm_ffn_g, norm_mem_g, w_mem_kv, mem_qn_g, mem_kn_g, w_out, w_in_a, w_gate_lr, b_gate_lr, gla_norm_g, w_in_b, swa_qn_g, swa_sinks, norm_kv_g, w_kv, swa_kn_g, w_router_group, b_router_group, w_router_expert, b_router_expert, w_exp_gate, w_exp_up, w_exp_down):
    bp, tp, d = x_prompt.shape
    bs, ts, _ = x_sample.shape
    assert bp == 1 and tp % WINDOW == 0 and ts * (GLA_CHUNK // ts) == GLA_CHUNK
    n_p, n_s = bp * tp, bs * ts
    w_buf = cache_win_k.shape[1]
    assert w_buf == WINDOW
    h = jnp.concatenate([x_prompt.reshape(n_p, d), x_sample.reshape(n_s, d)], axis=0)

    mem_k_p, mem_v_p = _mem_kv(mem_prompt, norm_mem_g, w_mem_kv, mem_kn_g)
    cmk = cache_mem_k.reshape(cache_mem_k.shape[0], bs, MEM_LEN, MEM_WIDTH)
    cmv = cache_mem_v.reshape(cache_mem_v.shape[0], bs, MEM_LEN, MEM_WIDTH)

    def mem_attend(mq, l):
        tm_p = _row_tile(tp, 256)
        mo_p = _mem_attn(mq, mem_k_p[l], mem_v_p[l], mem_qn_g[l], row_off=0, seq=tp, tm=tm_p, bb=1)
        mo_s = _mem_attn(mq, cmk[l], cmv[l], mem_qn_g[l], row_off=n_p, seq=ts, tm=ts, bb=8)
        return mo_p, mo_s

    def moe(h, l):
        return _moe(h, norm_ffn_g[l], w_router_group[l], b_router_group[l], w_router_expert[l],
                    b_router_expert[l], w_exp_gate, w_exp_up, w_exp_down, l)

    q, k, la, v, og, mq = _inproj_a(h, norm_mix_g[0], w_in_a[0], w_gate_lr[0], b_gate_lr[0])
    zero_state = jnp.zeros((bp, GLA_HEADS, GLA_DK, GLA_DV), F32)
    n_sub = max(1, min(4, tp // GLA_CHUNK))
    main_p, gla_p = _gla(q, k, la, v, og, zero_state, gla_norm_g[0], row_off=0, seq=tp, n_seg=1, n_sub=n_sub)
    main_s, gla_s = _gla(q, k, la, v, og, state_gla[0], gla_norm_g[0], row_off=n_p, seq=ts,
                         n_seg=GLA_CHUNK // ts, n_sub=1)
    mo_p, mo_s = mem_attend(mq, 0)
    w_o = w_out[0]
    h = _outproj(h, main_p, mo_p, main_s, mo_s, w_o[:MAIN_WIDTH].reshape(GLA_HEADS, GLA_DV, d),
                 w_o[MAIN_WIDTH:])
    h = moe(h, 0)

    q, mq, k_sh, v_sh = _inproj_b(h, norm_kv_g, norm_mix_g[1], w_kv, w_in_b[0], swa_kn_g, swa_qn_g[0])
    ck = cache_win_k.reshape(bs * w_buf, KV_WIDTH)
    cv = cache_win_v.reshape(bs * w_buf, KV_WIDTH)
    main_p = _swa(q, k_sh, v_sh, k_sh, v_sh, swa_sinks[0], row_off=0, n_blocks=n_p // WINDOW, tq=WINDOW,
                  prev_index=lambda b: jnp.maximum(b - 1, 0), mask_first=True)
    main_s = _swa(q, ck, cv, k_sh, v_sh, swa_sinks[0], row_off=n_p, n_blocks=bs, tq=ts,
                  prev_index=lambda b: b, mask_first=False)
    mo_p, mo_s = mem_attend(mq, 1)
    w_o = w_out[1]
    h = _outproj(h, main_p, mo_p, main_s, mo_s, w_o[:MAIN_WIDTH][_swa_perm()], w_o[MAIN_WIDTH:])
    h = moe(h, 1)

    y_prompt = h[:n_p].reshape(bp, tp, d)
    y_sample = h[n_p:].reshape(bs, ts, d)
    k_new = k_sh[n_p:].reshape(bs, ts, SWA_KV_HEADS, HEAD_DIM)
    v_new = v_sh[n_p:].reshape(bs, ts, SWA_KV_HEADS, HEAD_DIM)
    win_k_s = jnp.concatenate([cache_win_k, k_new], axis=1)[:, -w_buf:]
    win_v_s = jnp.concatenate([cache_win_v, v_new], axis=1)[:, -w_buf:]
    win_k_p = k_sh[n_p - WINDOW:n_p].reshape(bp, WINDOW, SWA_KV_HEADS, HEAD_DIM)
    win_v_p = v_sh[n_p - WINDOW:n_p].reshape(bp, WINDOW, SWA_KV_HEADS, HEAD_DIM)
    mem_shape = (mem_k_p.shape[0], bp, MEM_LEN, MEM_HEADS, HEAD_DIM)
    return (y_prompt, y_sample, gla_p[None], gla_s[None], win_k_p, win_v_p, win_k_s, win_v_s,
            mem_k_p.reshape(mem_shape), mem_v_p.reshape(mem_shape))
```

```python
import functools

import numpy as np
import jax
import jax.numpy as jnp
from jax import lax
from jax.experimental import pallas as pl
from jax.experimental.pallas import tpu as pltpu

F32 = jnp.float32
BF16 = jnp.bfloat16

D_MODEL = 1024
MEM_LEN = 256
MEM_HEADS = 4
HEAD_DIM = 64
MEM_WIDTH = MEM_HEADS * HEAD_DIM
MAIN_WIDTH = D_MODEL - MEM_WIDTH
GLA_HEADS = 4
GLA_DV = MAIN_WIDTH // GLA_HEADS
GLA_DK = GLA_DV // 2
GLA_DK_PAD = 128
GLA_KEY_WIDTH = GLA_HEADS * GLA_DK
GLA_KEY_PAD = GLA_HEADS * GLA_DK_PAD
GLA_GATE_RANK = 16
GLA_TAU = 16.0
GLA_CHUNK = 64
SWA_HEADS = MAIN_WIDTH // HEAD_DIM
SWA_KV_HEADS = 4
SWA_GROUP = SWA_HEADS // SWA_KV_HEADS
KV_WIDTH = SWA_KV_HEADS * HEAD_DIM
WINDOW = 128
N_GROUPS = 4
EXPERTS_PER_GROUP = 8
N_EXPERTS = N_GROUPS * EXPERTS_PER_GROUP
D_EXPERT = 512
EPS = 1e-6
LANES = 128
NEG_BIG = -1e30
VMEM_LIMIT = 56 * 1024 * 1024
MOE_TILE = 256
ROUTER_LANE0 = N_GROUPS
ROUTER_META_ROWS = 8
SLABS = D_MODEL // LANES


def _bdot(a, b):
    return jnp.dot(a.astype(BF16), b.astype(BF16), preferred_element_type=F32)


def _bdot_nt(a, b):
    return lax.dot_general(a.astype(BF16), b.astype(BF16), (((1,), (1,)), ((), ())),
                           preferred_element_type=F32)


def _bdot_tn(a, b):
    return lax.dot_general(a.astype(BF16), b.astype(BF16), (((0,), (0,)), ((), ())),
                           preferred_element_type=F32)


def _split(x, n):
    parts = []
    for _ in range(n - 1):
        p = x.astype(BF16)
        parts.append(p)
        x = x - p.astype(F32)
    parts.append(x.astype(BF16))
    return parts


def _exact_left_dot(m, x, n=3):
    out = None
    for p in _split(x, n):
        t = jnp.dot(m, p, preferred_element_type=F32)
        out = t if out is None else out + t
    return out


def _seg_mean(x2, bd):
    out = None
    for p in _split(x2, 2):
        t = jnp.dot(p, bd, preferred_element_type=F32)
        out = t if out is None else out + t
    return out


def _rms_scale(x):
    return lax.rsqrt(jnp.mean(x * x, axis=-1, keepdims=True) + EPS)


def _row_tile(n, cap=512):
    t = cap
    while t > 8 and n % t:
        t //= 2
    assert n % t == 0, n
    return t


def _params(sem):
    return pltpu.CompilerParams(dimension_semantics=sem, vmem_limit_bytes=VMEM_LIMIT)


def _const(shape):
    nd = len(shape)
    return pl.BlockSpec(shape, lambda *_: (0,) * nd)


def _group_specs(tm, width, prompt_tiles, lead=None):
    p_idx = lambda i, *_: jnp.minimum(i, prompt_tiles - 1)
    s_idx = lambda i, *_: jnp.maximum(i - prompt_tiles, 0)
    if lead is None:
        return [pl.BlockSpec((tm, width), lambda i, *_, f=f: (f(i), 0)) for f in (p_idx, s_idx)]
    return [pl.BlockSpec((lead, tm, width), lambda i, *_, f=f: (0, f(i), 0)) for f in (p_idx, s_idx)]


def _block_diag_mean(width):
    i = np.arange(width)
    return jnp.asarray((i[:, None] // HEAD_DIM == i[None, :] // HEAD_DIM) / HEAD_DIM, BF16)


def _head_masks(n_heads):
    i = np.arange(n_heads * HEAD_DIM)
    return jnp.asarray((i[None, :] // HEAD_DIM == np.arange(n_heads)[:, None]), F32)[:, None, :]


def _mem_kv_kernel(mem_ref, g_ref, w_ref, kng_ref, bd_ref, k_ref, v_ref):
    x = mem_ref[0]
    hn = x * _rms_scale(x) * g_ref[0]
    kv = _bdot(hn, w_ref[0])
    k = kv[:, :MEM_WIDTH]
    k = k * lax.rsqrt(_seg_mean(k * k, bd_ref[...]) + EPS) * kng_ref[0]
    k_ref[0, 0] = k
    v_ref[0, 0] = kv[:, MEM_WIDTH:]


def _mem_kv(mem, g, w, kng):
    depth, (b, m, d) = w.shape[0], mem.shape
    out = jax.ShapeDtypeStruct((depth, b, m, MEM_WIDTH), F32)
    blk = pl.BlockSpec((1, 1, m, MEM_WIDTH), lambda l, i: (l, i, 0, 0))
    return pl.pallas_call(
        _mem_kv_kernel,
        grid=(depth, b),
        in_specs=[pl.BlockSpec((1, m, d), lambda l, i: (i, 0, 0)),
                  pl.BlockSpec((1, 1, d), lambda l, i: (l, 0, 0)),
                  pl.BlockSpec((1, d, 2 * MEM_WIDTH), lambda l, i: (l, 0, 0)),
                  pl.BlockSpec((1, 1, MEM_WIDTH), lambda l, i: (l, 0, 0)),
                  _const((MEM_WIDTH, MEM_WIDTH))],
        out_specs=[blk, blk],
        out_shape=[out, out],
        compiler_params=_params(("arbitrary", "arbitrary")),
        name="mem_kv",
    )(mem, g.reshape(depth, 1, d), w.astype(BF16),
      jnp.tile(kng, (1, MEM_HEADS)).reshape(depth, 1, MEM_WIDTH), _block_diag_mean(MEM_WIDTH))


def _inproj_a_kernel(xp_ref, xs_ref, g_ref, wq_ref, wk_ref, wv_ref, wog_ref, wlr_ref, wmq_ref, wgl_ref, bgl_ref,
                     q_ref, k_ref, la_ref, v_ref, og_ref, mq_ref, *, prompt_tiles):
    x = jnp.where(pl.program_id(0) < prompt_tiles, xp_ref[...], xs_ref[...])
    hn = (x * _rms_scale(x) * g_ref[...]).astype(BF16)
    q_ref[...] = jnp.dot(hn, wq_ref[...], preferred_element_type=F32) * (GLA_DK ** -0.5)
    k_ref[...] = jnp.dot(hn, wk_ref[...], preferred_element_type=F32)
    for h in range(GLA_HEADS):
        v_ref[h] = jnp.dot(hn, wv_ref[h], preferred_element_type=F32)
        og_ref[h] = jnp.dot(hn, wog_ref[h], preferred_element_type=F32)
    lr = jnp.dot(hn, wlr_ref[...], preferred_element_type=F32)
    z = _bdot(lr, wgl_ref[...]) + bgl_ref[...]
    la_ref[...] = (jnp.minimum(z, 0.0) - jnp.log(1.0 + jnp.exp(-jnp.abs(z)))) * (1.0 / GLA_TAU)
    mq_ref[...] = jnp.dot(hn, wmq_ref[...], preferred_element_type=F32)


def _pad_heads(w, width, pad):
    lead = w.shape[:-1]
    w = w.reshape(*lead, GLA_HEADS, width)
    w = jnp.pad(w, [(0, 0)] * len(lead) + [(0, 0), (0, pad - width)])
    return w.reshape(*lead, GLA_HEADS * pad)


def _inproj_a(x_p, x_s, g, w_in, w_lr, b_lr):
    (n_p, d), n_s = x_p.shape, x_s.shape[0]
    n = n_p + n_s
    tm = _row_tile(n_s)
    assert n_p % tm == 0
    pt = n_p // tm
    c0, c1, c2, c3, c4 = (GLA_KEY_WIDTH, 2 * GLA_KEY_WIDTH, 2 * GLA_KEY_WIDTH + MAIN_WIDTH,
                          2 * GLA_KEY_WIDTH + 2 * MAIN_WIDTH,
                          2 * GLA_KEY_WIDTH + 2 * MAIN_WIDTH + GLA_GATE_RANK)
    wb = w_in.astype(BF16)
    wq = _pad_heads(wb[:, :c0], GLA_DK, GLA_DK_PAD)
    wk = _pad_heads(wb[:, c0:c1], GLA_DK, GLA_DK_PAD)
    wv = wb[:, c1:c2].reshape(d, GLA_HEADS, GLA_DV).transpose(1, 0, 2)
    wog = wb[:, c2:c3].reshape(d, GLA_HEADS, GLA_DV).transpose(1, 0, 2)
    wlr = jnp.pad(wb[:, c3:c4], ((0, 0), (0, LANES - GLA_GATE_RANK)))
    wmq = wb[:, c4:]
    wgl = jnp.pad(_pad_heads(w_lr.astype(BF16), GLA_DK, GLA_DK_PAD), ((0, LANES - GLA_GATE_RANK), (0, 0)))
    bgl = _pad_heads(b_lr.reshape(1, -1), GLA_DK, GLA_DK_PAD)
    row = lambda w: pl.BlockSpec((tm, w), lambda i: (i, 0))
    hrow = pl.BlockSpec((GLA_HEADS, tm, GLA_DV), lambda i: (0, i, 0))
    key = jax.ShapeDtypeStruct((n, GLA_KEY_PAD), F32)
    val = jax.ShapeDtypeStruct((GLA_HEADS, n, GLA_DV), F32)
    return pl.pallas_call(
        functools.partial(_inproj_a_kernel, prompt_tiles=pt),
        grid=(n // tm,),
        in_specs=_group_specs(tm, d, pt) + [
            _const((1, d)), _const(wq.shape), _const(wk.shape), _const(wv.shape),
            _const(wog.shape), _const(wlr.shape), _const(wmq.shape), _const(wgl.shape),
            _const(bgl.shape)],
        out_specs=[row(GLA_KEY_PAD), row(GLA_KEY_PAD), row(GLA_KEY_PAD), hrow, hrow, row(MEM_WIDTH)],
        out_shape=[key, key, key, val, val, jax.ShapeDtypeStruct((n, MEM_WIDTH), F32)],
        compiler_params=_params(("parallel",)),
        name="inproj_a",
    )(x_p, x_s, g.reshape(1, d), wq, wk, wv, wog, wlr, wmq, wgl, bgl)


def _gla_kernel(q_ref, k_ref, la_ref, v_ref, og_ref, s0_ref, gn_ref, mcum_ref, mall_ref, sel_ref,
                o_ref, sout_ref, s_ref, *, chunk, n_sub, n_seg):
    j = pl.program_id(1)
    seg = chunk // n_seg

    @pl.when(j == 0)
    def _():
        s_ref[...] = jnp.zeros_like(s_ref)
        s_ref[:, :, :GLA_DK, :] = s0_ref[...]

    mcum = mcum_ref[...]
    causal = mcum.astype(F32) > 0.0
    row = lax.broadcasted_iota(jnp.int32, (chunk, GLA_DK_PAD), 0)
    gn = gn_ref[...]
    for c in range(n_sub):
        rows = slice(c * chunk, (c + 1) * chunk)
        la = la_ref[rows, :]
        b = _exact_left_dot(mcum, la)
        b_end = _exact_left_dot(mall_ref[...], la)
        e_end = jnp.exp(_exact_left_dot(sel_ref[...], la)).T
        q = q_ref[rows, :]
        k = k_ref[rows, :]
        qt = q * jnp.exp(b)
        kt = (k * jnp.exp(-b)).astype(BF16)
        kd = k * jnp.exp(b_end - b)
        for h in range(GLA_HEADS):
            cols = slice(h * GLA_DK_PAD, (h + 1) * GLA_DK_PAD)
            qh = qt[:, cols]
            vb = v_ref[h, rows, :].astype(BF16)
            a = jnp.where(causal, _bdot_nt(qh, kt[:, cols]), 0.0)
            o = _bdot(a, vb)
            inter = []
            for s in range(n_seg):
                st = s_ref[s, h]
                inter.append(_bdot(qh[s * seg:(s + 1) * seg], st))
                kds = kd[:, cols]
                if n_seg > 1:
                    kds = jnp.where((row >= s * seg) & (row < (s + 1) * seg), kds, 0.0)
                s_ref[s, h] = e_end[cols, s:s + 1] * st + _bdot_tn(kds, vb)
            o = o + (inter[0] if n_seg == 1 else jnp.concatenate(inter, axis=0))
            on = o * lax.rsqrt(jnp.mean(o * o, axis=-1, keepdims=True) + EPS) * gn
            og = og_ref[h, rows, :]
            o_ref[h, rows, :] = on * (og * jax.nn.sigmoid(og))

    @pl.when(j == pl.num_programs(1) - 1)
    def _():
        sout_ref[...] = s_ref[:, :, :GLA_DK, :]


def _gla(q, k, la, v, og, s0, gnorm, *, row_off, seq, n_seg, n_sub):
    batch = s0.shape[0]
    chunk = GLA_CHUNK
    assert chunk % n_seg == 0 and batch % n_seg == 0
    seg = chunk // n_seg
    step_rows = n_sub * chunk
    if n_seg > 1:
        assert seq == seg and n_sub == 1
        t_steps = 1
    else:
        assert seq % step_rows == 0
        t_steps = seq // step_rows
    assert row_off % step_rows == 0
    off = row_off // step_rows
    i = np.arange(chunk)
    same = (i[:, None] // seg) == (i[None, :] // seg)
    mcum = jnp.asarray(same & (i[None, :] <= i[:, None]), BF16)
    mall = jnp.asarray(same, BF16)
    sel = jnp.asarray((i[None, :] // seg) == np.arange(LANES)[:, None], BF16)
    ridx = lambda g, j: (off + g * t_steps + j, 0)
    hidx = lambda g, j: (0, off + g * t_steps + j, 0)
    key_spec = pl.BlockSpec((step_rows, GLA_KEY_PAD), ridx)
    val_spec = pl.BlockSpec((GLA_HEADS, step_rows, GLA_DV), hidx)
    st_spec = pl.BlockSpec((n_seg, GLA_HEADS, GLA_DK, GLA_DV), lambda g, j: (g, 0, 0, 0))
    in_specs = [key_spec, key_spec, key_spec, val_spec, val_spec, st_spec, _const((1, GLA_DV)),
                _const((chunk, chunk)), _const((chunk, chunk)), _const((LANES, chunk))]
    args = [q, k, la, v, og, s0, gnorm.reshape(1, GLA_DV), mcum, mall, sel]
    out_spec = pl.BlockSpec((GLA_HEADS, step_rows, GLA_DV), lambda g, j: (0, g * t_steps + j, 0))
    return pl.pallas_call(
        functools.partial(_gla_kernel, chunk=chunk, n_sub=n_sub, n_seg=n_seg),
        grid=(batch // n_seg, t_steps),
        in_specs=in_specs,
        out_specs=[out_spec, st_spec],
        out_shape=[jax.ShapeDtypeStruct((GLA_HEADS, batch * seq, GLA_DV), F32),
                   jax.ShapeDtypeStruct(s0.shape, F32)],
        scratch_shapes=[pltpu.VMEM((n_seg, GLA_HEADS, GLA_DK_PAD, GLA_DV), F32)],
        compiler_params=_params(("arbitrary", "arbitrary")),
        name="gla",
    )(*args)


def _mem_attn_kernel(q_ref, k_ref, v_ref, g_ref, bd_ref, hm_ref, o_ref, *, tm, bb):
    g = g_ref[...]
    for i in range(bb):
        rows = slice(i * tm, (i + 1) * tm)
        q = q_ref[rows, :]
        qn = q * lax.rsqrt(_seg_mean(q * q, bd_ref[...]) + EPS) * g
        qs = jnp.concatenate([(qn * hm_ref[h]).astype(BF16) for h in range(MEM_HEADS)], axis=0)
        s = _bdot_nt(qs, k_ref[i])
        e = jnp.exp(s - jnp.max(s, axis=-1, keepdims=True))
        p = e / jnp.sum(e, axis=-1, keepdims=True)
        o = _bdot(p, v_ref[i])
        acc = o[:tm] * hm_ref[0]
        for h in range(1, MEM_HEADS):
            acc = acc + o[h * tm:(h + 1) * tm] * hm_ref[h]
        o_ref[rows, :] = acc


def _mem_attn(mq, mk, mv, qng, *, row_off, seq, tm, bb, layer):
    depth, batch, m, _ = mk.shape
    mk = mk.reshape(depth * batch, m, MEM_WIDTH)
    mv = mv.reshape(depth * batch, m, MEM_WIDTH)
    kv_off = layer * batch // bb
    assert seq % tm == 0 and batch % bb == 0 and (bb == 1 or seq == tm)
    t_steps = seq // tm
    step_rows = bb * tm
    assert row_off % step_rows == 0
    off = row_off // step_rows
    row_spec = pl.BlockSpec((step_rows, MEM_WIDTH), lambda g, j: (off + g * t_steps + j, 0))
    kv_spec = pl.BlockSpec((bb, m, MEM_WIDTH), lambda g, j: (kv_off + g, 0, 0))
    in_specs = [row_spec, kv_spec, kv_spec, _const((1, MEM_WIDTH)), _const((MEM_WIDTH, MEM_WIDTH)),
                _const((MEM_HEADS, 1, MEM_WIDTH))]
    args = [mq, mk, mv, (jnp.tile(qng, MEM_HEADS) * HEAD_DIM ** -0.5).reshape(1, MEM_WIDTH),
            _block_diag_mean(MEM_WIDTH), _head_masks(MEM_HEADS)]
    return pl.pallas_call(
        functools.partial(_mem_attn_kernel, tm=tm, bb=bb),
        grid=(batch // bb, t_steps),
        in_specs=in_specs,
        out_specs=pl.BlockSpec((step_rows, MEM_WIDTH), lambda g, j: (g * t_steps + j, 0)),
        out_shape=jax.ShapeDtypeStruct((batch * seq, MEM_WIDTH), F32),
        compiler_params=_params(("parallel", "parallel")),
        name="mem_attn",
    )(*args)


def _outproj_kernel(*refs, heads, prompt_tiles, split_residual):
    if split_residual:
        hp_ref, hs_ref = refs[:2]
        refs = refs[2:]
    else:
        hp_ref = hs_ref = refs[0]
        refs = refs[1:]
    main_p_ref, main_s_ref, mo_p_ref, mo_s_ref, wmain_ref, wmo_ref, o_ref = refs

    def project(h_ref, main_ref, mo_ref):
        acc = h_ref[...] + _bdot(mo_ref[...], wmo_ref[...])
        if heads:
            for h in range(heads):
                acc = acc + _bdot(main_ref[h], wmain_ref[h])
        else:
            acc = acc + _bdot(main_ref[...], wmain_ref[...])
        o_ref[...] = acc

    @pl.when(pl.program_id(0) < prompt_tiles)
    def _():
        project(hp_ref, main_p_ref, mo_p_ref)

    @pl.when(pl.program_id(0) >= prompt_tiles)
    def _():
        project(hs_ref, main_s_ref, mo_s_ref)


def _outproj(h, main_p, mo_p, main_s, mo_s, w_main, w_mo):
    n_p, n_s = mo_p.shape[0], mo_s.shape[0]
    n, d = n_p + n_s, w_mo.shape[1]
    tm = _row_tile(n_s)
    assert n_p % tm == 0
    pt = n_p // tm
    heads = main_p.shape[0] if main_p.ndim == 3 else 0
    row = pl.BlockSpec((tm, d), lambda i: (i, 0))
    split = isinstance(h, tuple)
    h_specs, h_args = (_group_specs(tm, d, pt), list(h)) if split else ([row], [h])
    main_specs = _group_specs(tm, main_p.shape[-1], pt, lead=heads or None)
    return pl.pallas_call(
        functools.partial(_outproj_kernel, heads=heads, prompt_tiles=pt, split_residual=split),
        grid=(n // tm,),
        in_specs=h_specs + main_specs + _group_specs(tm, MEM_WIDTH, pt) + [_const(w_main.shape),
                                                                           _const(w_mo.shape)],
        out_specs=row,
        out_shape=jax.ShapeDtypeStruct((n, d), F32),
        compiler_params=_params(("parallel",)),
        name="outproj",
    )(*h_args, main_p, main_s, mo_p, mo_s, w_main.astype(BF16), w_mo.astype(BF16))


def _router_kernel(h_ref, g_ref, whi_ref, wlo_ref, b_ref, tril_ref, mi_ref, mf_ref, cnt_ref, carry_ref):
    i = pl.program_id(0)

    @pl.when(i == 0)
    def _():
        carry_ref[...] = jnp.zeros_like(carry_ref)

    x = h_ref[...]
    xn = x * _rms_scale(x) * g_ref[...]
    x_hi, x_lo = _split(xn, 2)
    logits = (jnp.dot(x_hi, whi_ref[...], preferred_element_type=F32)
              + jnp.dot(x_hi, wlo_ref[...], preferred_element_type=F32)
              + jnp.dot(x_lo, whi_ref[...], preferred_element_type=F32)) + b_ref[...]
    tm = x.shape[0]
    lane = lax.broadcasted_iota(jnp.int32, (tm, LANES), 1)
    far = jnp.int32(2 * LANES)

    def first_max(vals):
        m = jnp.max(vals, axis=-1, keepdims=True)
        return m, jnp.min(jnp.where(vals == m, lane, far), axis=-1, keepdims=True)

    gl = jnp.where(lane < N_GROUPS, logits, -jnp.inf)
    gmax, grp = first_max(gl)
    pg_sel = 1.0 / jnp.sum(jnp.exp(gl - gmax), axis=-1, keepdims=True)
    lo = ROUTER_LANE0 + grp * EXPERTS_PER_GROUP
    el = jnp.where((lane >= lo) & (lane < lo + EXPERTS_PER_GROUP), logits, -jnp.inf)
    m1, i1 = first_max(el)
    m2, i2 = first_max(jnp.where(lane == i1, -jnp.inf, el))
    e2 = jnp.exp(m2 - m1)
    g1 = pg_sel / (1.0 + e2)
    g2 = pg_sel * e2 / (1.0 + e2)

    oh1 = lane == i1
    oh2 = lane == i2
    picked = jnp.where(oh1 | oh2, 1.0, 0.0)
    before = jnp.dot(tril_ref[...], picked.astype(BF16), preferred_element_type=F32) + carry_ref[...]
    rank1 = jnp.sum(jnp.where(oh1, before, 0.0), axis=-1, keepdims=True)
    rank2 = jnp.sum(jnp.where(oh2, before, 0.0), axis=-1, keepdims=True)
    carry = carry_ref[...] + jnp.sum(picked, axis=0, keepdims=True)
    carry_ref[...] = carry
    cnt_ref[...] = carry

    zi = jnp.zeros((tm, LANES), jnp.int32)
    mi = jnp.where(lane == 0, i1 - ROUTER_LANE0, zi)
    mi = jnp.where(lane == 1, i2 - ROUTER_LANE0, mi)
    mi = jnp.where(lane == 2, rank1.astype(jnp.int32), mi)
    mi = jnp.where(lane == 3, rank2.astype(jnp.int32), mi)
    mi_ref[...] = mi.T[:ROUTER_META_ROWS]
    zf = jnp.zeros((tm, LANES), F32)
    mf_ref[...] = jnp.where(lane == 0, g1, jnp.where(lane == 1, g2, zf))


def _router(h, g, w_rg, b_rg, w_re, b_re):
    n, d = h.shape
    tm = _row_tile(n)
    n_real = N_GROUPS + N_EXPERTS
    w = jnp.pad(jnp.concatenate([w_rg, w_re], axis=1), ((0, 0), (0, LANES - n_real)))
    b = jnp.pad(jnp.concatenate([b_rg, b_re]), (0, LANES - n_real)).reshape(1, LANES)
    w_hi = w.astype(BF16)
    w_lo = (w - w_hi.astype(F32)).astype(BF16)
    i = np.arange(tm)
    tril = jnp.asarray(i[None, :] < i[:, None], BF16)
    row = lambda width: pl.BlockSpec((tm, width), lambda i: (i, 0))
    return pl.pallas_call(
        _router_kernel,
        grid=(n // tm,),
        in_specs=[row(d), _const((1, d)), _const((d, LANES)), _const((d, LANES)), _const((1, LANES)),
                  _const((tm, tm))],
        out_specs=[pl.BlockSpec((ROUTER_META_ROWS, tm), lambda i: (0, i)), row(LANES), _const((1, LANES))],
        out_shape=[jax.ShapeDtypeStruct((ROUTER_META_ROWS, n), jnp.int32),
                   jax.ShapeDtypeStruct((n, LANES), F32), jax.ShapeDtypeStruct((1, LANES), F32)],
        scratch_shapes=[pltpu.VMEM((1, LANES), F32)],
        compiler_params=_params(("arbitrary",)),
        name="moe_router",
    )(h, g.reshape(1, d), w_hi, w_lo, b, tril)


def _to_rows(ref, rows, lead=()):
    return jnp.concatenate([ref[lead + (pl.ds(s, rows, stride=SLABS), slice(None))] for s in range(SLABS)],
                           axis=1)


def _from_rows(ref, x, rows, lead=()):
    for s in range(SLABS):
        ref[lead + (pl.ds(s, rows, stride=SLABS), slice(None))] = x[:, s * LANES:(s + 1) * LANES]


def _dispatch_kernel(pos_ref, h_ref, g_ref, xs_hbm, buf, sem, *, tm, n, steps):
    i = pl.program_id(0)
    slot = lax.rem(i, 2)

    def wait_slot(sl):
        for _ in range(2):
            pltpu.make_async_copy(buf.at[sl], xs_hbm.at[pl.ds(0, tm * SLABS)], sem.at[sl]).wait()

    @pl.when(i >= 2)
    def _():
        wait_slot(slot)

    x = h_ref[...]
    _from_rows(buf, x * _rms_scale(x) * g_ref[...], tm, (slot,))

    def issue(r, carry):
        src = buf.at[slot, pl.ds(pl.multiple_of(r * SLABS, SLABS), SLABS)]
        for kk in range(2):
            p = pos_ref[kk * n + i * tm + r]
            dst = xs_hbm.at[pl.ds(pl.multiple_of(p * SLABS, SLABS), SLABS)]
            pltpu.make_async_copy(src, dst, sem.at[slot]).start()
        return carry

    lax.fori_loop(0, tm, issue, 0)

    @pl.when(i == steps - 1)
    def _():
        wait_slot(slot)
        if steps > 1:
            wait_slot(1 - slot)


def _dispatch(h, g, pos_flat):
    n, d = h.shape
    tm = _row_tile(n, MOE_TILE)
    steps = n // tm
    return pl.pallas_call(
        functools.partial(_dispatch_kernel, tm=tm, n=n, steps=steps),
        grid_spec=pltpu.PrefetchScalarGridSpec(
            num_scalar_prefetch=1,
            grid=(steps,),
            in_specs=[pl.BlockSpec((tm, d), lambda i, pos: (i, 0)),
                      pl.BlockSpec((1, d), lambda i, pos: (0, 0))],
            out_specs=pl.BlockSpec(memory_space=pl.ANY),
            scratch_shapes=[pltpu.VMEM((2, tm * SLABS, LANES), F32), pltpu.SemaphoreType.DMA((2,))],
        ),
        out_shape=jax.ShapeDtypeStruct((2 * n * SLABS, LANES), F32),
        compiler_params=_params(("arbitrary",)),
        name="moe_dispatch",
    )(pos_flat, h, g.reshape(1, d))


ITEM_FIRST, ITEM_LAST, ITEM_NEW_EXPERT = 1, 2, 4


def _expert_kernel(tile_ref, exp_ref, lo_ref, hi_ref, flag_ref, n_ref, xs_ref, wg_ref, wu_ref, wd_ref, ys_ref,
                   wgb, wub, wdb, acc, *, tm):
    w = pl.program_id(0)

    @pl.when(w < n_ref[0])
    def _():
        flags = flag_ref[w]

        @pl.when((flags & ITEM_NEW_EXPERT) != 0)
        def _():
            wgb[...] = wg_ref[0, 0].astype(BF16)
            wub[...] = wu_ref[0, 0].astype(BF16)
            wdb[...] = wd_ref[0, 0].astype(BF16)

        x = _to_rows(xs_ref, tm).astype(BF16)
        hg = jnp.dot(x, wgb[...], preferred_element_type=F32)
        hu = jnp.dot(x, wub[...], preferred_element_type=F32)
        act = (hg * jax.nn.sigmoid(hg) * hu).astype(BF16)
        y = jnp.dot(act, wdb[...], preferred_element_type=F32)
        row = lax.broadcasted_iota(jnp.int32, (tm, 1), 0)
        y = jnp.where((row >= lo_ref[w]) & (row < hi_ref[w]), y, 0.0)
        first = (flags & ITEM_FIRST) != 0

        @pl.when(first)
        def _():
            acc[...] = y

        @pl.when(jnp.logical_not(first))
        def _():
            acc[...] += y

        @pl.when((flags & ITEM_LAST) != 0)
        def _():
            _from_rows(ys_ref, acc[...], tm)


def _experts(xs, items, w_g, w_u, w_d, layer):
    tile, expert, lo, hi, flags, n_items = items
    d = D_MODEL
    tm = MOE_TILE
    tok_spec = pl.BlockSpec((tm * SLABS, LANES), lambda w, tile, *_: (tile[w], 0))
    wmap = lambda w, tile, expert, *_: (layer, expert[w], 0, 0)
    return pl.pallas_call(
        functools.partial(_expert_kernel, tm=tm),
        grid_spec=pltpu.PrefetchScalarGridSpec(
            num_scalar_prefetch=6,
            grid=(tile.shape[0],),
            in_specs=[tok_spec,
                      pl.BlockSpec((1, 1, d, D_EXPERT), wmap),
                      pl.BlockSpec((1, 1, d, D_EXPERT), wmap),
                      pl.BlockSpec((1, 1, D_EXPERT, d), wmap)],
            out_specs=tok_spec,
            scratch_shapes=[pltpu.VMEM((d, D_EXPERT), BF16), pltpu.VMEM((d, D_EXPERT), BF16),
                            pltpu.VMEM((D_EXPERT, d), BF16), pltpu.VMEM((tm, d), F32)],
        ),
        out_shape=jax.ShapeDtypeStruct(xs.shape, F32),
        compiler_params=_params(("arbitrary",)),
        name="moe_experts",
    )(tile, expert, lo, hi, flags, n_items, xs, w_g, w_u, w_d)


def _combine_kernel(pos_ref, h_ref, gate_ref, ys_hbm, *refs, tm, n, steps, prompt_tiles):
    out_refs, (buf, sem) = refs[:-2], refs[-2:]
    i = pl.program_id(0)
    slot = lax.rem(i, 2)

    def issue(step, sl):
        def body(r, carry):
            for kk in range(2):
                p = pos_ref[kk * n + step * tm + r]
                src = ys_hbm.at[pl.ds(pl.multiple_of(p * SLABS, SLABS), SLABS)]
                dst = buf.at[sl, kk, pl.ds(pl.multiple_of(r * SLABS, SLABS), SLABS)]
                pltpu.make_async_copy(src, dst, sem.at[sl]).start()
            return carry

        lax.fori_loop(0, tm, body, 0)

    @pl.when(i == 0)
    def _():
        issue(0, 0)

    @pl.when(i + 1 < steps)
    def _():
        issue(i + 1, 1 - slot)

    for kk in range(2):
        pltpu.make_async_copy(ys_hbm.at[pl.ds(0, tm * SLABS)], buf.at[slot, kk], sem.at[slot]).wait()
    g = gate_ref[...]
    out = (h_ref[...] + g[:, 0:1] * _to_rows(buf, tm, (slot, 0))
           + g[:, 1:2] * _to_rows(buf, tm, (slot, 1)))
    if len(out_refs) == 1:
        out_refs[0][...] = out
    else:
        @pl.when(i < prompt_tiles)
        def _():
            out_refs[0][...] = out

        @pl.when(i >= prompt_tiles)
        def _():
            out_refs[1][...] = out


def _combine(h, gates, ys, pos_flat, split_rows=None):
    n, d = h.shape
    tm = _row_tile(n if split_rows is None else n - split_rows, MOE_TILE)
    steps = n // tm
    row = pl.BlockSpec((tm, d), lambda i, pos: (i, 0))
    if split_rows is None:
        pt, out_specs, out_shape = 0, row, jax.ShapeDtypeStruct((n, d), F32)
    else:
        assert split_rows % tm == 0
        pt = split_rows // tm
        out_specs = _group_specs(tm, d, pt)
        out_shape = [jax.ShapeDtypeStruct((split_rows, d), F32), jax.ShapeDtypeStruct((n - split_rows, d), F32)]
    return pl.pallas_call(
        functools.partial(_combine_kernel, tm=tm, n=n, steps=steps, prompt_tiles=pt),
        grid_spec=pltpu.PrefetchScalarGridSpec(
            num_scalar_prefetch=1,
            grid=(steps,),
            in_specs=[row, pl.BlockSpec((tm, LANES), lambda i, pos: (i, 0)),
                      pl.BlockSpec(memory_space=pl.ANY)],
            out_specs=out_specs,
            scratch_shapes=[pltpu.VMEM((2, 2, tm * SLABS, LANES), F32), pltpu.SemaphoreType.DMA((2,))],
        ),
        out_shape=out_shape,
        compiler_params=_params(("arbitrary",)),
        name="moe_combine",
    )(pos_flat, h, gates, ys)


def _lookup(table, idx):
    out = jnp.zeros(idx.shape, table.dtype)
    for e in range(table.shape[0]):
        out = jnp.where(idx == e, table[e], out)
    return out


def _work_items(counts, n_slots):
    tm = MOE_TILE
    ends = jnp.cumsum(counts)
    starts = ends - counts
    first_tile = starts // tm
    n_tiles_e = jnp.where(counts > 0, (ends - 1) // tm - first_tile + 1, 0)
    item_end = jnp.cumsum(n_tiles_e)
    item_start = item_end - n_tiles_e
    n_items = item_end[-1]
    max_items = n_slots // tm + N_EXPERTS - 1
    w = jnp.minimum(jnp.arange(max_items, dtype=jnp.int32), n_items - 1)
    expert = jnp.sum(w[:, None] >= item_end[None, :], axis=1).astype(jnp.int32)
    tile = _lookup(first_tile, expert) + w - _lookup(item_start, expert)
    lo = jnp.maximum(_lookup(starts, expert) - tile * tm, 0)
    hi = jnp.minimum(_lookup(ends, expert) - tile * tm, tm)
    prev_tile = jnp.concatenate([jnp.full((1,), -1, jnp.int32), tile[:-1]])
    next_tile = jnp.concatenate([tile[1:], jnp.full((1,), -1, jnp.int32)])
    prev_expert = jnp.concatenate([jnp.full((1,), -1, jnp.int32), expert[:-1]])
    idx = jnp.arange(max_items, dtype=jnp.int32)
    flags = (jnp.where(tile != prev_tile, ITEM_FIRST, 0)
             | jnp.where((tile != next_tile) | (idx == n_items - 1), ITEM_LAST, 0)
             | jnp.where(expert != prev_expert, ITEM_NEW_EXPERT, 0))
    as_i32 = lambda a: a.astype(jnp.int32)
    return (as_i32(tile), expert, as_i32(lo), as_i32(hi), as_i32(flags), as_i32(n_items).reshape(1))


def _moe(h, g, w_rg, b_rg, w_re, b_re, w_g, w_u, w_d, layer, split_rows=None):
    n, _ = h.shape
    meta, gates, cnt = _router(h, g, w_rg, b_rg, w_re, b_re)
    counts = cnt[0, ROUTER_LANE0:ROUTER_LANE0 + N_EXPERTS].astype(jnp.int32)
    starts = jnp.cumsum(counts) - counts
    pos = (_lookup(starts, meta[0:2]) + meta[2:4]).reshape(-1)
    xs = _dispatch(h, g, pos)
    ys = _experts(xs, _work_items(counts, 2 * n), w_g, w_u, w_d, layer)
    return _combine(h, gates, ys, pos, split_rows)


def _inproj_b_kernel(x_ref, gkv_ref, gmix_ref, wkv_ref, win_ref, kng_ref, qng_ref, bdk_ref, bdq_ref,
                     q_ref, mq_ref, k_ref, v_ref):
    x = x_ref[...]
    xr = x * _rms_scale(x)
    kv = _bdot(xr * gkv_ref[...], wkv_ref[...])
    k = kv[:, :KV_WIDTH]
    k_ref[...] = k * lax.rsqrt(_seg_mean(k * k, bdk_ref[...]) + EPS) * kng_ref[...]
    v_ref[...] = kv[:, KV_WIDTH:]
    proj = _bdot(xr * gmix_ref[...], win_ref[...])
    q = proj[:, :MAIN_WIDTH]
    q_ref[...] = q * lax.rsqrt(_seg_mean(q * q, bdq_ref[...]) + EPS) * qng_ref[...]
    mq_ref[...] = proj[:, MAIN_WIDTH:]


def _swa_perm():
    g, kh, dd = np.meshgrid(np.arange(SWA_GROUP), np.arange(SWA_KV_HEADS), np.arange(HEAD_DIM), indexing="ij")
    return ((kh * SWA_GROUP + g) * HEAD_DIM + dd).reshape(-1)


def _inproj_b(x, g_kv, g_mix, w_kv, w_in, kng, qng):
    n, d = x.shape
    tm = _row_tile(n)
    perm = _swa_perm()
    w_in_p = jnp.concatenate([w_in[:, :MAIN_WIDTH][:, perm], w_in[:, MAIN_WIDTH:]], axis=1).astype(BF16)
    qng_t = (jnp.tile(qng, SWA_HEADS) * HEAD_DIM ** -0.5).reshape(1, MAIN_WIDTH)
    row = lambda w: pl.BlockSpec((tm, w), lambda i: (i, 0))
    return pl.pallas_call(
        _inproj_b_kernel,
        grid=(n // tm,),
        in_specs=[row(d), _const((1, d)), _const((1, d)), _const((d, 2 * KV_WIDTH)), _const((d, d)),
                  _const((1, KV_WIDTH)), _const((1, MAIN_WIDTH)), _const((KV_WIDTH, KV_WIDTH)),
                  _const((MAIN_WIDTH, MAIN_WIDTH))],
        out_specs=[row(MAIN_WIDTH), row(MEM_WIDTH), row(KV_WIDTH), row(KV_WIDTH)],
        out_shape=[jax.ShapeDtypeStruct((n, MAIN_WIDTH), F32), jax.ShapeDtypeStruct((n, MEM_WIDTH), F32),
                   jax.ShapeDtypeStruct((n, KV_WIDTH), F32), jax.ShapeDtypeStruct((n, KV_WIDTH), F32)],
        compiler_params=_params(("parallel",)),
        name="inproj_b",
    )(x, g_kv.reshape(1, d), g_mix.reshape(1, d), w_kv.astype(BF16), w_in_p,
      jnp.tile(kng, SWA_KV_HEADS).reshape(1, KV_WIDTH), qng_t,
      _block_diag_mean(KV_WIDTH), _block_diag_mean(MAIN_WIDTH))


def _swa_kernel(q_ref, kp_ref, ko_ref, vp_ref, vo_ref, bias_ref, sink_ref, hm_ref, o_ref, *, tq, mask_first):
    kk = jnp.concatenate([kp_ref[...], ko_ref[...]], axis=0).astype(BF16)
    vv = jnp.concatenate([vp_ref[...], vo_ref[...]], axis=0).astype(BF16)
    n_keys = kk.shape[0]
    q = q_ref[...]
    if mask_first:
        key = lax.broadcasted_iota(jnp.int32, (SWA_GROUP * tq, n_keys), 1)
        has_prev = (pl.program_id(0) > 0) | (key >= WINDOW)
    acc = [None] * SWA_GROUP
    for kh in range(SWA_KV_HEADS):
        hm = hm_ref[kh]
        qs = jnp.concatenate([(q[:, g * KV_WIDTH:(g + 1) * KV_WIDTH] * hm).astype(BF16)
                              for g in range(SWA_GROUP)], axis=0)
        s = _bdot_nt(qs, kk) + bias_ref[kh]
        if mask_first:
            s = jnp.where(has_prev, s, NEG_BIG)
        sink = sink_ref[kh]
        m = jnp.maximum(jnp.max(s, axis=-1, keepdims=True), sink)
        e = jnp.exp(s - m)
        p = e / (jnp.sum(e, axis=-1, keepdims=True) + jnp.exp(sink - m))
        o = _bdot(p, vv)
        for g in range(SWA_GROUP):
            t = o[g * tq:(g + 1) * tq] * hm
            acc[g] = t if acc[g] is None else acc[g] + t
    for g in range(SWA_GROUP):
        o_ref[:, g * KV_WIDTH:(g + 1) * KV_WIDTH] = acc[g]


def _swa(q, k_prev, v_prev, k_own, v_own, sinks, *, row_off, n_blocks, tq, prev_index, mask_first):
    assert row_off % tq == 0
    off = row_off // tq
    n_keys = WINDOW + tq
    slopes = 2.0 ** (-8.0 * np.arange(1, SWA_HEADS + 1, dtype=np.float64) / SWA_HEADS)
    i = np.arange(tq)[:, None]
    j = np.arange(n_keys)[None, :]
    dist = i + WINDOW - j
    valid = (dist >= 0) & (dist <= WINDOW)
    bias = np.stack([np.concatenate([np.where(valid, -slopes[kh * SWA_GROUP + g] * dist, NEG_BIG)
                                     for g in range(SWA_GROUP)], axis=0)
                     for kh in range(SWA_KV_HEADS)]).astype(np.float32)
    sink_col = jnp.repeat(sinks.reshape(SWA_KV_HEADS, SWA_GROUP), tq, axis=1)[..., None].astype(F32)
    own = lambda w: pl.BlockSpec((tq, w), lambda b: (off + b, 0))
    prev = pl.BlockSpec((WINDOW, KV_WIDTH), lambda b: (prev_index(b), 0))
    in_specs = [own(MAIN_WIDTH), prev, own(KV_WIDTH), prev, own(KV_WIDTH),
                _const(bias.shape), _const(sink_col.shape), _const((SWA_KV_HEADS, 1, KV_WIDTH))]
    args = [q, k_prev, k_own, v_prev, v_own, jnp.asarray(bias), sink_col, _head_masks(SWA_KV_HEADS)]
    return pl.pallas_call(
        functools.partial(_swa_kernel, tq=tq, mask_first=mask_first),
        grid=(n_blocks,),
        in_specs=in_specs,
        out_specs=pl.BlockSpec((tq, MAIN_WIDTH), lambda b: (b, 0)),
        out_shape=jax.ShapeDtypeStruct((n_blocks * tq, MAIN_WIDTH), F32),
        compiler_params=_params(("arbitrary",)),
        name="swa",
    )(*args)


def kernel(x_prompt, x_sample, state_gla, cache_win_k, cache_win_v, cache_mem_k, cache_mem_v, mem_prompt, norm_mix_g, norm_ffn_g, norm_mem_g, w_mem_kv, mem_qn_g, mem_kn_g, w_out, w_in_a, w_gate_lr, b_gate_lr, gla_norm_g, w_in_b, swa_qn_g, swa_sinks, norm_kv_g, w_kv, swa_kn_g, w_router_group, b_router_group, w_router_expert, b_router_expert, w_exp_gate, w_exp_up, w_exp_down):
    bp, tp, d = x_prompt.shape
    bs, ts, _ = x_sample.shape
    assert bp == 1 and tp % WINDOW == 0 and ts * (GLA_CHUNK // ts) == GLA_CHUNK
    n_p, n_s = bp * tp, bs * ts
    w_buf = cache_win_k.shape[1]
    assert w_buf == WINDOW
    x_p, x_s = x_prompt.reshape(n_p, d), x_sample.reshape(n_s, d)

    mem_k_p, mem_v_p = _mem_kv(mem_prompt, norm_mem_g, w_mem_kv, mem_kn_g)
    cmk = cache_mem_k.reshape(cache_mem_k.shape[0], bs, MEM_LEN, MEM_WIDTH)
    cmv = cache_mem_v.reshape(cache_mem_v.shape[0], bs, MEM_LEN, MEM_WIDTH)

    def mem_attend(mq, l):
        tm_p = _row_tile(tp, 256)
        mo_p = _mem_attn(mq, mem_k_p, mem_v_p, mem_qn_g[l], row_off=0, seq=tp, tm=tm_p, bb=1, layer=l)
        mo_s = _mem_attn(mq, cmk, cmv, mem_qn_g[l], row_off=n_p, seq=ts, tm=ts, bb=8, layer=l)
        return mo_p, mo_s

    def moe(h, l, split_rows=None):
        return _moe(h, norm_ffn_g[l], w_router_group[l], b_router_group[l], w_router_expert[l],
                    b_router_expert[l], w_exp_gate, w_exp_up, w_exp_down, l, split_rows)

    q, k, la, v, og, mq = _inproj_a(x_p, x_s, norm_mix_g[0], w_in_a[0], w_gate_lr[0], b_gate_lr[0])
    zero_state = jnp.zeros((bp, GLA_HEADS, GLA_DK, GLA_DV), F32)
    n_sub = max(1, min(4, tp // GLA_CHUNK))
    main_p, gla_p = _gla(q, k, la, v, og, zero_state, gla_norm_g[0], row_off=0, seq=tp, n_seg=1, n_sub=n_sub)
    main_s, gla_s = _gla(q, k, la, v, og, state_gla[0], gla_norm_g[0], row_off=n_p, seq=ts,
                         n_seg=GLA_CHUNK // ts, n_sub=1)
    mo_p, mo_s = mem_attend(mq, 0)
    w_o = w_out[0]
    h = _outproj((x_p, x_s), main_p, mo_p, main_s, mo_s, w_o[:MAIN_WIDTH].reshape(GLA_HEADS, GLA_DV, d),
                 w_o[MAIN_WIDTH:])
    h = moe(h, 0)

    q, mq, k_sh, v_sh = _inproj_b(h, norm_kv_g, norm_mix_g[1], w_kv, w_in_b[0], swa_kn_g, swa_qn_g[0])
    ck = cache_win_k.reshape(bs * w_buf, KV_WIDTH)
    cv = cache_win_v.reshape(bs * w_buf, KV_WIDTH)
    main_p = _swa(q, k_sh, v_sh, k_sh, v_sh, swa_sinks[0], row_off=0, n_blocks=n_p // WINDOW, tq=WINDOW,
                  prev_index=lambda b: jnp.maximum(b - 1, 0), mask_first=True)
    main_s = _swa(q, ck, cv, k_sh, v_sh, swa_sinks[0], row_off=n_p, n_blocks=bs, tq=ts,
                  prev_index=lambda b: b, mask_first=False)
    mo_p, mo_s = mem_attend(mq, 1)
    w_o = w_out[1]
    h = _outproj(h, main_p, mo_p, main_s, mo_s, w_o[:MAIN_WIDTH][_swa_perm()], w_o[MAIN_WIDTH:])
    y_p, y_s = moe(h, 1, split_rows=n_p)

    y_prompt = y_p.reshape(bp, tp, d)
    y_sample = y_s.reshape(bs, ts, d)
    k_new = k_sh[n_p:].reshape(bs, ts, SWA_KV_HEADS, HEAD_DIM)
    v_new = v_sh[n_p:].reshape(bs, ts, SWA_KV_HEADS, HEAD_DIM)
    win_k_s = jnp.concatenate([cache_win_k, k_new], axis=1)[:, -w_buf:]
    win_v_s = jnp.concatenate([cache_win_v, v_new], axis=1)[:, -w_buf:]
    win_k_p = k_sh[n_p - WINDOW:n_p].reshape(bp, WINDOW, SWA_KV_HEADS, HEAD_DIM)
    win_v_p = v_sh[n_p - WINDOW:n_p].reshape(bp, WINDOW, SWA_KV_HEADS, HEAD_DIM)
    mem_shape = (mem_k_p.shape[0], bp, MEM_LEN, MEM_HEADS, HEAD_DIM)
    return (y_prompt, y_sample, gla_p[None], gla_s[None], win_k_p, win_v_p, win_k_s, win_v_s,
            mem_k_p.reshape(mem_shape), mem_v_p.reshape(mem_shape))
```

```python
import functools

import numpy as np
import jax
import jax.numpy as jnp
from jax import lax
from jax.experimental import pallas as pl
from jax.experimental.pallas import tpu as pltpu

F32 = jnp.float32
BF16 = jnp.bfloat16

D_MODEL = 1024
MEM_LEN = 256
MEM_HEADS = 4
HEAD_DIM = 64
MEM_WIDTH = MEM_HEADS * HEAD_DIM
MAIN_WIDTH = D_MODEL - MEM_WIDTH
GLA_HEADS = 4
GLA_DV = MAIN_WIDTH // GLA_HEADS
GLA_DK = GLA_DV // 2
GLA_DK_PAD = 128
GLA_KEY_WIDTH = GLA_HEADS * GLA_DK
GLA_KEY_PAD = GLA_HEADS * GLA_DK_PAD
GLA_GATE_RANK = 16
GLA_TAU = 16.0
GLA_CHUNK = 64
SWA_HEADS = MAIN_WIDTH // HEAD_DIM
SWA_KV_HEADS = 4
SWA_GROUP = SWA_HEADS // SWA_KV_HEADS
KV_WIDTH = SWA_KV_HEADS * HEAD_DIM
WINDOW = 128
N_GROUPS = 4
EXPERTS_PER_GROUP = 8
N_EXPERTS = N_GROUPS * EXPERTS_PER_GROUP
D_EXPERT = 512
EPS = 1e-6
LANES = 128
NEG_BIG = -1e30
VMEM_LIMIT = 56 * 1024 * 1024
MOE_TILE = 256
ROUTER_LANE0 = N_GROUPS
ROUTER_META_ROWS = 8
SLABS = D_MODEL // LANES


def _bdot(a, b):
    return jnp.dot(a.astype(BF16), b.astype(BF16), preferred_element_type=F32)


def _bdot_nt(a, b):
    return lax.dot_general(a.astype(BF16), b.astype(BF16), (((1,), (1,)), ((), ())),
                           preferred_element_type=F32)


def _bdot_tn(a, b):
    return lax.dot_general(a.astype(BF16), b.astype(BF16), (((0,), (0,)), ((), ())),
                           preferred_element_type=F32)


def _split(x, n):
    parts = []
    for _ in range(n - 1):
        p = x.astype(BF16)
        parts.append(p)
        x = x - p.astype(F32)
    parts.append(x.astype(BF16))
    return parts


def _exact_left_dot(m, x, n=3):
    out = None
    for p in _split(x, n):
        t = jnp.dot(m, p, preferred_element_type=F32)
        out = t if out is None else out + t
    return out


def _seg_mean(x2, bd):
    out = None
    for p in _split(x2, 2):
        t = jnp.dot(p, bd, preferred_element_type=F32)
        out = t if out is None else out + t
    return out


def _rms_scale(x):
    return lax.rsqrt(jnp.mean(x * x, axis=-1, keepdims=True) + EPS)


def _row_tile(n, cap=512):
    t = cap
    while t > 8 and n % t:
        t //= 2
    assert n % t == 0, n
    return t


def _params(sem):
    return pltpu.CompilerParams(dimension_semantics=sem, vmem_limit_bytes=VMEM_LIMIT)


def _const(shape):
    nd = len(shape)
    return pl.BlockSpec(shape, lambda *_: (0,) * nd)


def _group_specs(tm, width, prompt_tiles, lead=None):
    p_idx = lambda i, *_: jnp.minimum(i, prompt_tiles - 1)
    s_idx = lambda i, *_: jnp.maximum(i - prompt_tiles, 0)
    if lead is None:
        return [pl.BlockSpec((tm, width), lambda i, *_, f=f: (f(i), 0)) for f in (p_idx, s_idx)]
    return [pl.BlockSpec((lead, tm, width), lambda i, *_, f=f: (0, f(i), 0)) for f in (p_idx, s_idx)]


def _block_diag_mean(width):
    i = np.arange(width)
    return jnp.asarray((i[:, None] // HEAD_DIM == i[None, :] // HEAD_DIM) / HEAD_DIM, BF16)


def _head_masks(n_heads):
    i = np.arange(n_heads * HEAD_DIM)
    return jnp.asarray((i[None, :] // HEAD_DIM == np.arange(n_heads)[:, None]), F32)[:, None, :]


def _mem_kv_kernel(mem_ref, g_ref, w_ref, kng_ref, bd_ref, k_ref, v_ref):
    x = mem_ref[0]
    hn = x * _rms_scale(x) * g_ref[0]
    kv = _bdot(hn, w_ref[0])
    k = kv[:, :MEM_WIDTH]
    k = k * lax.rsqrt(_seg_mean(k * k, bd_ref[...]) + EPS) * kng_ref[0]
    k_ref[0, 0] = k
    v_ref[0, 0] = kv[:, MEM_WIDTH:]


def _mem_kv(mem, g, w, kng):
    depth, (b, m, d) = w.shape[0], mem.shape
    out = jax.ShapeDtypeStruct((depth, b, m, MEM_WIDTH), F32)
    blk = pl.BlockSpec((1, 1, m, MEM_WIDTH), lambda l, i: (l, i, 0, 0))
    return pl.pallas_call(
        _mem_kv_kernel,
        grid=(depth, b),
        in_specs=[pl.BlockSpec((1, m, d), lambda l, i: (i, 0, 0)),
                  pl.BlockSpec((1, 1, d), lambda l, i: (l, 0, 0)),
                  pl.BlockSpec((1, d, 2 * MEM_WIDTH), lambda l, i: (l, 0, 0)),
                  pl.BlockSpec((1, 1, MEM_WIDTH), lambda l, i: (l, 0, 0)),
                  _const((MEM_WIDTH, MEM_WIDTH))],
        out_specs=[blk, blk],
        out_shape=[out, out],
        compiler_params=_params(("arbitrary", "arbitrary")),
        name="mem_kv",
    )(mem, g.reshape(depth, 1, d), w.astype(BF16),
      jnp.tile(kng, (1, MEM_HEADS)).reshape(depth, 1, MEM_WIDTH), _block_diag_mean(MEM_WIDTH))


def _inproj_a_kernel(xp_ref, xs_ref, g_ref, wq_ref, wk_ref, wv_ref, wog_ref, wlr_ref, wmq_ref, wgl_ref, bgl_ref,
                     q_ref, k_ref, la_ref, v_ref, og_ref, mq_ref, *, prompt_tiles):
    x = jnp.where(pl.program_id(0) < prompt_tiles, xp_ref[...], xs_ref[...])
    hn = (x * _rms_scale(x) * g_ref[...]).astype(BF16)
    q_ref[...] = jnp.dot(hn, wq_ref[...], preferred_element_type=F32) * (GLA_DK ** -0.5)
    k_ref[...] = jnp.dot(hn, wk_ref[...], preferred_element_type=F32)
    for h in range(GLA_HEADS):
        v_ref[h] = jnp.dot(hn, wv_ref[h], preferred_element_type=F32)
        og_ref[h] = jnp.dot(hn, wog_ref[h], preferred_element_type=F32)
    lr = jnp.dot(hn, wlr_ref[...], preferred_element_type=F32)
    z = _bdot(lr, wgl_ref[...]) + bgl_ref[...]
    la_ref[...] = (jnp.minimum(z, 0.0) - jnp.log(1.0 + jnp.exp(-jnp.abs(z)))) * (1.0 / GLA_TAU)
    mq_ref[...] = jnp.dot(hn, wmq_ref[...], preferred_element_type=F32)


def _pad_heads(w, width, pad):
    lead = w.shape[:-1]
    w = w.reshape(*lead, GLA_HEADS, width)
    w = jnp.pad(w, [(0, 0)] * len(lead) + [(0, 0), (0, pad - width)])
    return w.reshape(*lead, GLA_HEADS * pad)


def _inproj_a(x_p, x_s, g, w_in, w_lr, b_lr):
    (n_p, d), n_s = x_p.shape, x_s.shape[0]
    n = n_p + n_s
    tm = _row_tile(n_s)
    assert n_p % tm == 0
    pt = n_p // tm
    c0, c1, c2, c3, c4 = (GLA_KEY_WIDTH, 2 * GLA_KEY_WIDTH, 2 * GLA_KEY_WIDTH + MAIN_WIDTH,
                          2 * GLA_KEY_WIDTH + 2 * MAIN_WIDTH,
                          2 * GLA_KEY_WIDTH + 2 * MAIN_WIDTH + GLA_GATE_RANK)
    wb = w_in.astype(BF16)
    wq = _pad_heads(wb[:, :c0], GLA_DK, GLA_DK_PAD)
    wk = _pad_heads(wb[:, c0:c1], GLA_DK, GLA_DK_PAD)
    wv = wb[:, c1:c2].reshape(d, GLA_HEADS, GLA_DV).transpose(1, 0, 2)
    wog = wb[:, c2:c3].reshape(d, GLA_HEADS, GLA_DV).transpose(1, 0, 2)
    wlr = jnp.pad(wb[:, c3:c4], ((0, 0), (0, LANES - GLA_GATE_RANK)))
    wmq = wb[:, c4:]
    wgl = jnp.pad(_pad_heads(w_lr.astype(BF16), GLA_DK, GLA_DK_PAD), ((0, LANES - GLA_GATE_RANK), (0, 0)))
    bgl = _pad_heads(b_lr.reshape(1, -1), GLA_DK, GLA_DK_PAD)
    row = lambda w: pl.BlockSpec((tm, w), lambda i: (i, 0))
    hrow = pl.BlockSpec((GLA_HEADS, tm, GLA_DV), lambda i: (0, i, 0))
    key = jax.ShapeDtypeStruct((n, GLA_KEY_PAD), F32)
    val = jax.ShapeDtypeStruct((GLA_HEADS, n, GLA_DV), F32)
    return pl.pallas_call(
        functools.partial(_inproj_a_kernel, prompt_tiles=pt),
        grid=(n // tm,),
        in_specs=_group_specs(tm, d, pt) + [
            _const((1, d)), _const(wq.shape), _const(wk.shape), _const(wv.shape),
            _const(wog.shape), _const(wlr.shape), _const(wmq.shape), _const(wgl.shape),
            _const(bgl.shape)],
        out_specs=[row(GLA_KEY_PAD), row(GLA_KEY_PAD), row(GLA_KEY_PAD), hrow, hrow, row(MEM_WIDTH)],
        out_shape=[key, key, key, val, val, jax.ShapeDtypeStruct((n, MEM_WIDTH), F32)],
        compiler_params=_params(("parallel",)),
        name="inproj_a",
    )(x_p, x_s, g.reshape(1, d), wq, wk, wv, wog, wlr, wmq, wgl, bgl)


def _gla_kernel(q_ref, k_ref, la_ref, v_ref, og_ref, s0_ref, gn_ref, mcum_ref, mall_ref, sel_ref,
                o_ref, sout_ref, s_ref, *, chunk, n_sub, n_seg):
    j = pl.program_id(1)
    seg = chunk // n_seg

    @pl.when(j == 0)
    def _():
        s_ref[...] = jnp.zeros_like(s_ref)
        s_ref[:, :, :GLA_DK, :] = s0_ref[...]

    mcum = mcum_ref[...]
    causal = mcum.astype(F32) > 0.0
    row = lax.broadcasted_iota(jnp.int32, (chunk, GLA_DK_PAD), 0)
    gn = gn_ref[...]
    hcols = [slice(h * GLA_DK_PAD, (h + 1) * GLA_DK_PAD) for h in range(GLA_HEADS)]
    crows = [slice(c * chunk, (c + 1) * chunk) for c in range(n_sub)]
    qts, kts, kds, e_ends = [], [], [], []
    for rows in crows:
        la = la_ref[rows, :]
        b = _exact_left_dot(mcum, la)
        b_end = _exact_left_dot(mall_ref[...], la)
        e_ends.append(jnp.exp(_exact_left_dot(sel_ref[...], la)).T)
        k = k_ref[rows, :]
        qts.append(q_ref[rows, :] * jnp.exp(b))
        kts.append((k * jnp.exp(-b)).astype(BF16))
        kds.append(k * jnp.exp(b_end - b))
    vbs = [[v_ref[h, rows, :].astype(BF16) for h in range(GLA_HEADS)] for rows in crows]
    scores = [[_bdot_nt(qts[c][:, cols], kts[c][:, cols]) for cols in hcols] for c in range(n_sub)]
    kvs = []
    for c in range(n_sub):
        per_head = []
        for h, cols in enumerate(hcols):
            per_seg = []
            for s in range(n_seg):
                kd = kds[c][:, cols]
                if n_seg > 1:
                    kd = jnp.where((row >= s * seg) & (row < (s + 1) * seg), kd, 0.0)
                per_seg.append(_bdot_tn(kd, vbs[c][h]))
            per_head.append(per_seg)
        kvs.append(per_head)
    state = [[s_ref[s, h] for s in range(n_seg)] for h in range(GLA_HEADS)]
    inters = []
    for c in range(n_sub):
        per_head = []
        for h, cols in enumerate(hcols):
            parts = []
            for s in range(n_seg):
                parts.append(_bdot(qts[c][s * seg:(s + 1) * seg, cols], state[h][s]))
                state[h][s] = e_ends[c][cols, s:s + 1] * state[h][s] + kvs[c][h][s]
            per_head.append(parts[0] if n_seg == 1 else jnp.concatenate(parts, axis=0))
        inters.append(per_head)
    for h in range(GLA_HEADS):
        for s in range(n_seg):
            s_ref[s, h] = state[h][s]
    for c, rows in enumerate(crows):
        for h in range(GLA_HEADS):
            a = jnp.where(causal, scores[c][h], 0.0)
            o = _bdot(a, vbs[c][h]) + inters[c][h]
            on = o * lax.rsqrt(jnp.mean(o * o, axis=-1, keepdims=True) + EPS) * gn
            og = og_ref[h, rows, :]
            o_ref[h, rows, :] = on * (og * jax.nn.sigmoid(og))

    @pl.when(j == pl.num_programs(1) - 1)
    def _():
        sout_ref[...] = s_ref[:, :, :GLA_DK, :]


def _gla(q, k, la, v, og, s0, gnorm, *, row_off, seq, n_seg, n_sub):
    batch = s0.shape[0]
    chunk = GLA_CHUNK
    assert chunk % n_seg == 0 and batch % n_seg == 0
    seg = chunk // n_seg
    step_rows = n_sub * chunk
    if n_seg > 1:
        assert seq == seg and n_sub == 1
        t_steps = 1
    else:
        assert seq % step_rows == 0
        t_steps = seq // step_rows
    assert row_off % step_rows == 0
    off = row_off // step_rows
    i = np.arange(chunk)
    same = (i[:, None] // seg) == (i[None, :] // seg)
    mcum = jnp.asarray(same & (i[None, :] <= i[:, None]), BF16)
    mall = jnp.asarray(same, BF16)
    sel = jnp.asarray((i[None, :] // seg) == np.arange(LANES)[:, None], BF16)
    ridx = lambda g, j: (off + g * t_steps + j, 0)
    hidx = lambda g, j: (0, off + g * t_steps + j, 0)
    key_spec = pl.BlockSpec((step_rows, GLA_KEY_PAD), ridx)
    val_spec = pl.BlockSpec((GLA_HEADS, step_rows, GLA_DV), hidx)
    st_spec = pl.BlockSpec((n_seg, GLA_HEADS, GLA_DK, GLA_DV), lambda g, j: (g, 0, 0, 0))
    in_specs = [key_spec, key_spec, key_spec, val_spec, val_spec, st_spec, _const((1, GLA_DV)),
                _const((chunk, chunk)), _const((chunk, chunk)), _const((LANES, chunk))]
    args = [q, k, la, v, og, s0, gnorm.reshape(1, GLA_DV), mcum, mall, sel]
    out_spec = pl.BlockSpec((GLA_HEADS, step_rows, GLA_DV), lambda g, j: (0, g * t_steps + j, 0))
    return pl.pallas_call(
        functools.partial(_gla_kernel, chunk=chunk, n_sub=n_sub, n_seg=n_seg),
        grid=(batch // n_seg, t_steps),
        in_specs=in_specs,
        out_specs=[out_spec, st_spec],
        out_shape=[jax.ShapeDtypeStruct((GLA_HEADS, batch * seq, GLA_DV), F32),
                   jax.ShapeDtypeStruct(s0.shape, F32)],
        scratch_shapes=[pltpu.VMEM((n_seg, GLA_HEADS, GLA_DK_PAD, GLA_DV), F32)],
        compiler_params=_params(("arbitrary", "arbitrary")),
        name="gla",
    )(*args)


def _mem_attn_kernel(q_ref, k_ref, v_ref, g_ref, bd_ref, hm_ref, o_ref, *, tm, bb):
    g = g_ref[...]
    sub = min(tm, 128)
    units = [(i, i * tm + r) for i in range(bb) for r in range(0, tm, sub)]
    scores = []
    for i, r in units:
        q = q_ref[r:r + sub, :]
        qn = q * lax.rsqrt(_seg_mean(q * q, bd_ref[...]) + EPS) * g
        qs = jnp.concatenate([(qn * hm_ref[h]).astype(BF16) for h in range(MEM_HEADS)], axis=0)
        scores.append(_bdot_nt(qs, k_ref[i]))
    probs = []
    for s in scores:
        e = jnp.exp(s - jnp.max(s, axis=-1, keepdims=True))
        probs.append(e * (1.0 / jnp.sum(e, axis=-1, keepdims=True)))
    outs = [_bdot(p, v_ref[i]) for (i, _), p in zip(units, probs)]
    for (_, r), o in zip(units, outs):
        acc = o[:sub] * hm_ref[0]
        for h in range(1, MEM_HEADS):
            acc = acc + o[h * sub:(h + 1) * sub] * hm_ref[h]
        o_ref[r:r + sub, :] = acc


def _mem_attn(mq, mk, mv, qng, *, row_off, seq, tm, bb, layer):
    depth, batch, m, _ = mk.shape
    mk = mk.reshape(depth * batch, m, MEM_WIDTH)
    mv = mv.reshape(depth * batch, m, MEM_WIDTH)
    kv_off = layer * batch // bb
    assert seq % tm == 0 and batch % bb == 0 and (bb == 1 or seq == tm)
    t_steps = seq // tm
    step_rows = bb * tm
    assert row_off % step_rows == 0
    off = row_off // step_rows
    row_spec = pl.BlockSpec((step_rows, MEM_WIDTH), lambda g, j: (off + g * t_steps + j, 0))
    kv_spec = pl.BlockSpec((bb, m, MEM_WIDTH), lambda g, j: (kv_off + g, 0, 0))
    in_specs = [row_spec, kv_spec, kv_spec, _const((1, MEM_WIDTH)), _const((MEM_WIDTH, MEM_WIDTH)),
                _const((MEM_HEADS, 1, MEM_WIDTH))]
    args = [mq, mk, mv, (jnp.tile(qng, MEM_HEADS) * HEAD_DIM ** -0.5).reshape(1, MEM_WIDTH),
            _block_diag_mean(MEM_WIDTH), _head_masks(MEM_HEADS)]
    return pl.pallas_call(
        functools.partial(_mem_attn_kernel, tm=tm, bb=bb),
        grid=(batch // bb, t_steps),
        in_specs=in_specs,
        out_specs=pl.BlockSpec((step_rows, MEM_WIDTH), lambda g, j: (g * t_steps + j, 0)),
        out_shape=jax.ShapeDtypeStruct((batch * seq, MEM_WIDTH), F32),
        compiler_params=_params(("parallel", "parallel")),
        name="mem_attn",
    )(*args)


def _outproj_kernel(*refs, heads, prompt_tiles, split_residual):
    if split_residual:
        hp_ref, hs_ref = refs[:2]
        refs = refs[2:]
    else:
        hp_ref = hs_ref = refs[0]
        refs = refs[1:]
    main_p_ref, main_s_ref, mo_p_ref, mo_s_ref, wmain_ref, wmo_ref, o_ref = refs

    def project(h_ref, main_ref, mo_ref):
        acc = h_ref[...] + _bdot(mo_ref[...], wmo_ref[...])
        if heads:
            for h in range(heads):
                acc = acc + _bdot(main_ref[h], wmain_ref[h])
        else:
            acc = acc + _bdot(main_ref[...], wmain_ref[...])
        o_ref[...] = acc

    @pl.when(pl.program_id(0) < prompt_tiles)
    def _():
        project(hp_ref, main_p_ref, mo_p_ref)

    @pl.when(pl.program_id(0) >= prompt_tiles)
    def _():
        project(hs_ref, main_s_ref, mo_s_ref)


def _outproj(h, main_p, mo_p, main_s, mo_s, w_main, w_mo):
    n_p, n_s = mo_p.shape[0], mo_s.shape[0]
    n, d = n_p + n_s, w_mo.shape[1]
    tm = _row_tile(n_s)
    assert n_p % tm == 0
    pt = n_p // tm
    heads = main_p.shape[0] if main_p.ndim == 3 else 0
    row = pl.BlockSpec((tm, d), lambda i: (i, 0))
    split = isinstance(h, tuple)
    h_specs, h_args = (_group_specs(tm, d, pt), list(h)) if split else ([row], [h])
    main_specs = _group_specs(tm, main_p.shape[-1], pt, lead=heads or None)
    return pl.pallas_call(
        functools.partial(_outproj_kernel, heads=heads, prompt_tiles=pt, split_residual=split),
        grid=(n // tm,),
        in_specs=h_specs + main_specs + _group_specs(tm, MEM_WIDTH, pt) + [_const(w_main.shape),
                                                                           _const(w_mo.shape)],
        out_specs=row,
        out_shape=jax.ShapeDtypeStruct((n, d), F32),
        compiler_params=_params(("parallel",)),
        name="outproj",
    )(*h_args, main_p, main_s, mo_p, mo_s, w_main.astype(BF16), w_mo.astype(BF16))


def _router_kernel(h_ref, g_ref, whi_ref, wlo_ref, b_ref, tril_ref, mi_ref, mf_ref, cnt_ref, carry_ref):
    i = pl.program_id(0)

    @pl.when(i == 0)
    def _():
        carry_ref[...] = jnp.zeros_like(carry_ref)

    x = h_ref[...]
    xn = x * _rms_scale(x) * g_ref[...]
    x_hi, x_lo = _split(xn, 2)
    logits = (jnp.dot(x_hi, whi_ref[...], preferred_element_type=F32)
              + jnp.dot(x_hi, wlo_ref[...], preferred_element_type=F32)
              + jnp.dot(x_lo, whi_ref[...], preferred_element_type=F32)) + b_ref[...]
    tm = x.shape[0]
    lane = lax.broadcasted_iota(jnp.int32, (tm, LANES), 1)
    far = jnp.int32(2 * LANES)

    def first_max(vals):
        m = jnp.max(vals, axis=-1, keepdims=True)
        return m, jnp.min(jnp.where(vals == m, lane, far), axis=-1, keepdims=True)

    gl = jnp.where(lane < N_GROUPS, logits, -jnp.inf)
    gmax, grp = first_max(gl)
    pg_sel = 1.0 / jnp.sum(jnp.exp(gl - gmax), axis=-1, keepdims=True)
    lo = ROUTER_LANE0 + grp * EXPERTS_PER_GROUP
    el = jnp.where((lane >= lo) & (lane < lo + EXPERTS_PER_GROUP), logits, -jnp.inf)
    m1, i1 = first_max(el)
    m2, i2 = first_max(jnp.where(lane == i1, -jnp.inf, el))
    e2 = jnp.exp(m2 - m1)
    g1 = pg_sel / (1.0 + e2)
    g2 = pg_sel * e2 / (1.0 + e2)

    oh1 = lane == i1
    oh2 = lane == i2
    picked = jnp.where(oh1 | oh2, 1.0, 0.0)
    before = jnp.dot(tril_ref[...], picked.astype(BF16), preferred_element_type=F32) + carry_ref[...]
    rank1 = jnp.sum(jnp.where(oh1, before, 0.0), axis=-1, keepdims=True)
    rank2 = jnp.sum(jnp.where(oh2, before, 0.0), axis=-1, keepdims=True)
    carry = carry_ref[...] + jnp.sum(picked, axis=0, keepdims=True)
    carry_ref[...] = carry
    cnt_ref[...] = carry

    zi = jnp.zeros((tm, LANES), jnp.int32)
    mi = jnp.where(lane == 0, i1 - ROUTER_LANE0, zi)
    mi = jnp.where(lane == 1, i2 - ROUTER_LANE0, mi)
    mi = jnp.where(lane == 2, rank1.astype(jnp.int32), mi)
    mi = jnp.where(lane == 3, rank2.astype(jnp.int32), mi)
    mi_ref[...] = mi.T[:ROUTER_META_ROWS]
    zf = jnp.zeros((tm, LANES), F32)
    mf_ref[...] = jnp.where(lane == 0, g1, jnp.where(lane == 1, g2, zf))


def _router(h, g, w_rg, b_rg, w_re, b_re):
    n, d = h.shape
    tm = _row_tile(n)
    n_real = N_GROUPS + N_EXPERTS
    w = jnp.pad(jnp.concatenate([w_rg, w_re], axis=1), ((0, 0), (0, LANES - n_real)))
    b = jnp.pad(jnp.concatenate([b_rg, b_re]), (0, LANES - n_real)).reshape(1, LANES)
    w_hi = w.astype(BF16)
    w_lo = (w - w_hi.astype(F32)).astype(BF16)
    i = np.arange(tm)
    tril = jnp.asarray(i[None, :] < i[:, None], BF16)
    row = lambda width: pl.BlockSpec((tm, width), lambda i: (i, 0))
    return pl.pallas_call(
        _router_kernel,
        grid=(n // tm,),
        in_specs=[row(d), _const((1, d)), _const((d, LANES)), _const((d, LANES)), _const((1, LANES)),
                  _const((tm, tm))],
        out_specs=[pl.BlockSpec((ROUTER_META_ROWS, tm), lambda i: (0, i)), row(LANES), _const((1, LANES))],
        out_shape=[jax.ShapeDtypeStruct((ROUTER_META_ROWS, n), jnp.int32),
                   jax.ShapeDtypeStruct((n, LANES), F32), jax.ShapeDtypeStruct((1, LANES), F32)],
        scratch_shapes=[pltpu.VMEM((1, LANES), F32)],
        compiler_params=_params(("arbitrary",)),
        name="moe_router",
    )(h, g.reshape(1, d), w_hi, w_lo, b, tril)


def _to_rows(ref, rows, lead=()):
    return jnp.concatenate([ref[lead + (pl.ds(s, rows, stride=SLABS), slice(None))] for s in range(SLABS)],
                           axis=1)


def _from_rows(ref, x, rows, lead=()):
    for s in range(SLABS):
        ref[lead + (pl.ds(s, rows, stride=SLABS), slice(None))] = x[:, s * LANES:(s + 1) * LANES]


def _dispatch_kernel(pos_ref, h_ref, g_ref, xs_hbm, buf, sem, *, tm, n, steps):
    i = pl.program_id(0)
    slot = lax.rem(i, 2)

    def wait_slot(sl):
        for _ in range(2):
            pltpu.make_async_copy(buf.at[sl], xs_hbm.at[pl.ds(0, tm * SLABS)], sem.at[sl]).wait()

    @pl.when(i >= 2)
    def _():
        wait_slot(slot)

    x = h_ref[...]
    _from_rows(buf, x * _rms_scale(x) * g_ref[...], tm, (slot,))

    def issue(r, carry):
        src = buf.at[slot, pl.ds(pl.multiple_of(r * SLABS, SLABS), SLABS)]
        for kk in range(2):
            p = pos_ref[kk * n + i * tm + r]
            dst = xs_hbm.at[pl.ds(pl.multiple_of(p * SLABS, SLABS), SLABS)]
            pltpu.make_async_copy(src, dst, sem.at[slot]).start()
        return carry

    lax.fori_loop(0, tm, issue, 0)

    @pl.when(i == steps - 1)
    def _():
        wait_slot(slot)
        if steps > 1:
            wait_slot(1 - slot)


def _dispatch(h, g, pos_flat):
    n, d = h.shape
    tm = _row_tile(n, MOE_TILE)
    steps = n // tm
    return pl.pallas_call(
        functools.partial(_dispatch_kernel, tm=tm, n=n, steps=steps),
        grid_spec=pltpu.PrefetchScalarGridSpec(
            num_scalar_prefetch=1,
            grid=(steps,),
            in_specs=[pl.BlockSpec((tm, d), lambda i, pos: (i, 0)),
                      pl.BlockSpec((1, d), lambda i, pos: (0, 0))],
            out_specs=pl.BlockSpec(memory_space=pl.ANY),
            scratch_shapes=[pltpu.VMEM((2, tm * SLABS, LANES), F32), pltpu.SemaphoreType.DMA((2,))],
        ),
        out_shape=jax.ShapeDtypeStruct((2 * n * SLABS, LANES), F32),
        compiler_params=_params(("arbitrary",)),
        name="moe_dispatch",
    )(pos_flat, h, g.reshape(1, d))


ITEM_FIRST, ITEM_LAST, ITEM_NEW_EXPERT = 1, 2, 4


def _expert_kernel(tile_ref, exp_ref, lo_ref, hi_ref, flag_ref, n_ref, xs_ref, wg_ref, wu_ref, wd_ref, ys_ref,
                   wgb, wub, wdb, acc, *, tm):
    w = pl.program_id(0)

    @pl.when(w < n_ref[0])
    def _():
        flags = flag_ref[w]

        @pl.when((flags & ITEM_NEW_EXPERT) != 0)
        def _():
            wgb[...] = wg_ref[0, 0].astype(BF16)
            wub[...] = wu_ref[0, 0].astype(BF16)
            wdb[...] = wd_ref[0, 0].astype(BF16)

        x = _to_rows(xs_ref, tm).astype(BF16)
        hg = jnp.dot(x, wgb[...], preferred_element_type=F32)
        hu = jnp.dot(x, wub[...], preferred_element_type=F32)
        act = (hg * jax.nn.sigmoid(hg) * hu).astype(BF16)
        y = jnp.dot(act, wdb[...], preferred_element_type=F32)
        row = lax.broadcasted_iota(jnp.int32, (tm, 1), 0)
        y = jnp.where((row >= lo_ref[w]) & (row < hi_ref[w]), y, 0.0)
        first = (flags & ITEM_FIRST) != 0

        @pl.when(first)
        def _():
            acc[...] = y

        @pl.when(jnp.logical_not(first))
        def _():
            acc[...] += y

        @pl.when((flags & ITEM_LAST) != 0)
        def _():
            _from_rows(ys_ref, acc[...], tm)


def _experts(xs, items, w_g, w_u, w_d, layer):
    tile, expert, lo, hi, flags, n_items = items
    d = D_MODEL
    tm = MOE_TILE
    tok_spec = pl.BlockSpec((tm * SLABS, LANES), lambda w, tile, *_: (tile[w], 0))
    wmap = lambda w, tile, expert, *_: (layer, expert[w], 0, 0)
    return pl.pallas_call(
        functools.partial(_expert_kernel, tm=tm),
        grid_spec=pltpu.PrefetchScalarGridSpec(
            num_scalar_prefetch=6,
            grid=(tile.shape[0],),
            in_specs=[tok_spec,
                      pl.BlockSpec((1, 1, d, D_EXPERT), wmap),
                      pl.BlockSpec((1, 1, d, D_EXPERT), wmap),
                      pl.BlockSpec((1, 1, D_EXPERT, d), wmap)],
            out_specs=tok_spec,
            scratch_shapes=[pltpu.VMEM((d, D_EXPERT), BF16), pltpu.VMEM((d, D_EXPERT), BF16),
                            pltpu.VMEM((D_EXPERT, d), BF16), pltpu.VMEM((tm, d), F32)],
        ),
        out_shape=jax.ShapeDtypeStruct(xs.shape, F32),
        compiler_params=_params(("arbitrary",)),
        name="moe_experts",
    )(tile, expert, lo, hi, flags, n_items, xs, w_g, w_u, w_d)


def _combine_kernel(pos_ref, h_ref, gate_ref, ys_hbm, *refs, tm, n, steps, prompt_tiles):
    out_refs, (buf, sem) = refs[:-2], refs[-2:]
    i = pl.program_id(0)
    slot = lax.rem(i, 2)

    def issue(step, sl):
        def body(r, carry):
            for kk in range(2):
                p = pos_ref[kk * n + step * tm + r]
                src = ys_hbm.at[pl.ds(pl.multiple_of(p * SLABS, SLABS), SLABS)]
                dst = buf.at[sl, kk, pl.ds(pl.multiple_of(r * SLABS, SLABS), SLABS)]
                pltpu.make_async_copy(src, dst, sem.at[sl]).start()
            return carry

        lax.fori_loop(0, tm, body, 0)

    @pl.when(i == 0)
    def _():
        issue(0, 0)

    @pl.when(i + 1 < steps)
    def _():
        issue(i + 1, 1 - slot)

    for kk in range(2):
        pltpu.make_async_copy(ys_hbm.at[pl.ds(0, tm * SLABS)], buf.at[slot, kk], sem.at[slot]).wait()
    g = gate_ref[...]
    out = (h_ref[...] + g[:, 0:1] * _to_rows(buf, tm, (slot, 0))
           + g[:, 1:2] * _to_rows(buf, tm, (slot, 1)))
    if len(out_refs) == 1:
        out_refs[0][...] = out
    else:
        @pl.when(i < prompt_tiles)
        def _():
            out_refs[0][...] = out

        @pl.when(i >= prompt_tiles)
        def _():
            out_refs[1][...] = out


def _combine(h, gates, ys, pos_flat, split_rows=None):
    n, d = h.shape
    tm = _row_tile(n if split_rows is None else n - split_rows, MOE_TILE)
    steps = n // tm
    row = pl.BlockSpec((tm, d), lambda i, pos: (i, 0))
    if split_rows is None:
        pt, out_specs, out_shape = 0, row, jax.ShapeDtypeStruct((n, d), F32)
    else:
        assert split_rows % tm == 0
        pt = split_rows // tm
        out_specs = _group_specs(tm, d, pt)
        out_shape = [jax.ShapeDtypeStruct((split_rows, d), F32), jax.ShapeDtypeStruct((n - split_rows, d), F32)]
    return pl.pallas_call(
        functools.partial(_combine_kernel, tm=tm, n=n, steps=steps, prompt_tiles=pt),
        grid_spec=pltpu.PrefetchScalarGridSpec(
            num_scalar_prefetch=1,
            grid=(steps,),
            in_specs=[row, pl.BlockSpec((tm, LANES), lambda i, pos: (i, 0)),
                      pl.BlockSpec(memory_space=pl.ANY)],
            out_specs=out_specs,
            scratch_shapes=[pltpu.VMEM((2, 2, tm * SLABS, LANES), F32), pltpu.SemaphoreType.DMA((2,))],
        ),
        out_shape=out_shape,
        compiler_params=_params(("arbitrary",)),
        name="moe_combine",
    )(pos_flat, h, gates, ys)


def _lookup(table, idx):
    out = jnp.zeros(idx.shape, table.dtype)
    for e in range(table.shape[0]):
        out = jnp.where(idx == e, table[e], out)
    return out


def _work_items(counts, n_slots):
    tm = MOE_TILE
    ends = jnp.cumsum(counts)
    starts = ends - counts
    first_tile = starts // tm
    n_tiles_e = jnp.where(counts > 0, (ends - 1) // tm - first_tile + 1, 0)
    item_end = jnp.cumsum(n_tiles_e)
    item_start = item_end - n_tiles_e
    n_items = item_end[-1]
    max_items = n_slots // tm + N_EXPERTS - 1
    w = jnp.minimum(jnp.arange(max_items, dtype=jnp.int32), n_items - 1)
    expert = jnp.sum(w[:, None] >= item_end[None, :], axis=1).astype(jnp.int32)
    tile = _lookup(first_tile, expert) + w - _lookup(item_start, expert)
    lo = jnp.maximum(_lookup(starts, expert) - tile * tm, 0)
    hi = jnp.minimum(_lookup(ends, expert) - tile * tm, tm)
    prev_tile = jnp.concatenate([jnp.full((1,), -1, jnp.int32), tile[:-1]])
    next_tile = jnp.concatenate([tile[1:], jnp.full((1,), -1, jnp.int32)])
    prev_expert = jnp.concatenate([jnp.full((1,), -1, jnp.int32), expert[:-1]])
    idx = jnp.arange(max_items, dtype=jnp.int32)
    flags = (jnp.where(tile != prev_tile, ITEM_FIRST, 0)
             | jnp.where((tile != next_tile) | (idx == n_items - 1), ITEM_LAST, 0)
             | jnp.where(expert != prev_expert, ITEM_NEW_EXPERT, 0))
    as_i32 = lambda a: a.astype(jnp.int32)
    return (as_i32(tile), expert, as_i32(lo), as_i32(hi), as_i32(flags), as_i32(n_items).reshape(1))


def _moe(h, g, w_rg, b_rg, w_re, b_re, w_g, w_u, w_d, layer, split_rows=None):
    n, _ = h.shape
    meta, gates, cnt = _router(h, g, w_rg, b_rg, w_re, b_re)
    counts = cnt[0, ROUTER_LANE0:ROUTER_LANE0 + N_EXPERTS].astype(jnp.int32)
    starts = jnp.cumsum(counts) - counts
    pos = (_lookup(starts, meta[0:2]) + meta[2:4]).reshape(-1)
    xs = _dispatch(h, g, pos)
    ys = _experts(xs, _work_items(counts, 2 * n), w_g, w_u, w_d, layer)
    return _combine(h, gates, ys, pos, split_rows)


def _inproj_b_kernel(x_ref, gkv_ref, gmix_ref, wkv_ref, win_ref, kng_ref, qng_ref, bdk_ref, bdq_ref,
                     q_ref, mq_ref, k_ref, v_ref, kt_ref):
    x = x_ref[...]
    xr = x * _rms_scale(x)
    kv = _bdot(xr * gkv_ref[...], wkv_ref[...])
    k = kv[:, :KV_WIDTH]
    k = k * lax.rsqrt(_seg_mean(k * k, bdk_ref[...]) + EPS) * kng_ref[...]
    k_ref[...] = k
    kt_ref[...] = k.T
    v_ref[...] = kv[:, KV_WIDTH:]
    proj = _bdot(xr * gmix_ref[...], win_ref[...])
    q = proj[:, :MAIN_WIDTH]
    q_ref[...] = q * lax.rsqrt(_seg_mean(q * q, bdq_ref[...]) + EPS) * qng_ref[...]
    mq_ref[...] = proj[:, MAIN_WIDTH:]


def _swa_perm():
    g, kh, dd = np.meshgrid(np.arange(SWA_GROUP), np.arange(SWA_KV_HEADS), np.arange(HEAD_DIM), indexing="ij")
    return ((kh * SWA_GROUP + g) * HEAD_DIM + dd).reshape(-1)


def _inproj_b(x, g_kv, g_mix, w_kv, w_in, kng, qng):
    n, d = x.shape
    tm = _row_tile(n)
    perm = _swa_perm()
    w_in_p = jnp.concatenate([w_in[:, :MAIN_WIDTH][:, perm], w_in[:, MAIN_WIDTH:]], axis=1).astype(BF16)
    qng_t = (jnp.tile(qng, SWA_HEADS) * HEAD_DIM ** -0.5).reshape(1, MAIN_WIDTH)
    row = lambda w: pl.BlockSpec((tm, w), lambda i: (i, 0))
    return pl.pallas_call(
        _inproj_b_kernel,
        grid=(n // tm,),
        in_specs=[row(d), _const((1, d)), _const((1, d)), _const((d, 2 * KV_WIDTH)), _const((d, d)),
                  _const((1, KV_WIDTH)), _const((1, MAIN_WIDTH)), _const((KV_WIDTH, KV_WIDTH)),
                  _const((MAIN_WIDTH, MAIN_WIDTH))],
        out_specs=[row(MAIN_WIDTH), row(MEM_WIDTH), row(KV_WIDTH), row(KV_WIDTH),
                   pl.BlockSpec((KV_WIDTH, tm), lambda i: (0, i))],
        out_shape=[jax.ShapeDtypeStruct((n, MAIN_WIDTH), F32), jax.ShapeDtypeStruct((n, MEM_WIDTH), F32),
                   jax.ShapeDtypeStruct((n, KV_WIDTH), F32), jax.ShapeDtypeStruct((n, KV_WIDTH), F32),
                   jax.ShapeDtypeStruct((KV_WIDTH, n), F32)],
        compiler_params=_params(("parallel",)),
        name="inproj_b",
    )(x, g_kv.reshape(1, d), g_mix.reshape(1, d), w_kv.astype(BF16), w_in_p,
      jnp.tile(kng, SWA_KV_HEADS).reshape(1, KV_WIDTH), qng_t,
      _block_diag_mean(KV_WIDTH), _block_diag_mean(MAIN_WIDTH))


def _softmax_with_sink(s, sink):
    m = jnp.maximum(jnp.max(s, axis=-1, keepdims=True), sink)
    e = jnp.exp(s - m)
    r = 1.0 / (jnp.sum(e, axis=-1, keepdims=True) + jnp.exp(sink - m))
    return (e * r).astype(BF16)


def _swa_bias(tq):
    slopes = 2.0 ** (-8.0 * np.arange(1, SWA_HEADS + 1, dtype=np.float64) / SWA_HEADS)
    dist = np.arange(tq)[:, None] + WINDOW - np.arange(WINDOW + tq)[None, :]
    valid = (dist >= 0) & (dist <= WINDOW)
    return np.stack([np.where(valid, -s * dist, NEG_BIG) for s in slopes]).astype(np.float32)


def _swa_prompt_kernel(sink_ref, q_ref, ktp_ref, kto_ref, vp_ref, vo_ref, bias_ref, hm_ref, o_ref, *, nb):
    w = WINDOW
    key = lax.broadcasted_iota(jnp.int32, (w, 2 * w), 1)
    has_prev = (pl.program_id(0) > 0) | (key >= w)
    heads = [(g, kh) for g in range(SWA_GROUP) for kh in range(SWA_KV_HEADS)]
    kts, vvs = [], []
    for b in range(nb):
        kt_prev = ktp_ref[...] if b == 0 else kto_ref[:, (b - 1) * w:b * w]
        v_prev = vp_ref[...] if b == 0 else vo_ref[(b - 1) * w:b * w, :]
        kts.append(jnp.concatenate([kt_prev, kto_ref[:, b * w:(b + 1) * w]], axis=1).astype(BF16))
        vvs.append(jnp.concatenate([v_prev, vo_ref[b * w:(b + 1) * w, :]], axis=0).astype(BF16))
    scores = [[jnp.dot((q_ref[b * w:(b + 1) * w, g * KV_WIDTH:(g + 1) * KV_WIDTH] * hm_ref[kh]).astype(BF16),
                       kts[b], preferred_element_type=F32) for g, kh in heads] for b in range(nb)]
    for b in range(nb):
        probs = []
        for (g, kh), s in zip(heads, scores[b]):
            h = kh * SWA_GROUP + g
            s = s + bias_ref[h]
            if b == 0:
                s = jnp.where(has_prev, s, NEG_BIG)
            probs.append(_softmax_with_sink(s, sink_ref[h]))
        outs = [jnp.dot(p, vvs[b], preferred_element_type=F32) for p in probs]
        for g in range(SWA_GROUP):
            acc = None
            for (cg, kh), o in zip(heads, outs):
                if cg == g:
                    t = o * hm_ref[kh]
                    acc = t if acc is None else acc + t
            o_ref[b * w:(b + 1) * w, g * KV_WIDTH:(g + 1) * KV_WIDTH] = acc


def _swa_prompt(q, kt, v, sinks, *, n_rows, nb):
    w = WINDOW
    step = nb * w
    assert n_rows % step == 0
    prev = lambda j, sink: jnp.maximum(j * nb - 1, 0)
    return pl.pallas_call(
        functools.partial(_swa_prompt_kernel, nb=nb),
        grid_spec=pltpu.PrefetchScalarGridSpec(
            num_scalar_prefetch=1,
            grid=(n_rows // step,),
            in_specs=[pl.BlockSpec((step, MAIN_WIDTH), lambda j, sink: (j, 0)),
                      pl.BlockSpec((KV_WIDTH, w), lambda j, sink: (0, prev(j, sink))),
                      pl.BlockSpec((KV_WIDTH, step), lambda j, sink: (0, j)),
                      pl.BlockSpec((w, KV_WIDTH), lambda j, sink: (prev(j, sink), 0)),
                      pl.BlockSpec((step, KV_WIDTH), lambda j, sink: (j, 0)),
                      pl.BlockSpec((SWA_HEADS, w, 2 * w), lambda j, sink: (0, 0, 0)),
                      pl.BlockSpec((SWA_KV_HEADS, 1, KV_WIDTH), lambda j, sink: (0, 0, 0))],
            out_specs=pl.BlockSpec((step, MAIN_WIDTH), lambda j, sink: (j, 0)),
        ),
        out_shape=jax.ShapeDtypeStruct((n_rows, MAIN_WIDTH), F32),
        compiler_params=_params(("arbitrary",)),
        name="swa_prompt",
    )(sinks.astype(F32), q, kt, kt, v, v, jnp.asarray(_swa_bias(w)), _head_masks(SWA_KV_HEADS))


def _swa_sample_kernel(q_ref, kp_ref, ko_ref, vp_ref, vo_ref, bias_ref, sink_ref, hm_ref, o_ref, *, nb, tq):
    w = WINDOW
    heads = [(kh, g) for kh in range(SWA_KV_HEADS) for g in range(SWA_GROUP)]
    kks, vvs, scores = [], [], []
    for i in range(nb):
        kks.append(jnp.concatenate([kp_ref[i * w:(i + 1) * w, :], ko_ref[i * tq:(i + 1) * tq, :]], axis=0))
        vvs.append(jnp.concatenate([vp_ref[i * w:(i + 1) * w, :], vo_ref[i * tq:(i + 1) * tq, :]], axis=0))
        qs = jnp.concatenate([q_ref[i * tq:(i + 1) * tq, g * KV_WIDTH:(g + 1) * KV_WIDTH] * hm_ref[kh]
                              for kh, g in heads], axis=0)
        scores.append(_bdot_nt(qs, kks[i]))
    probs = [_softmax_with_sink(s + bias_ref[...], sink_ref[...]) for s in scores]
    outs = [_bdot(p, vv) for p, vv in zip(probs, vvs)]
    for i in range(nb):
        for g in range(SWA_GROUP):
            acc = None
            for r, (kh, hg) in enumerate(heads):
                if hg == g:
                    t = outs[i][r * tq:(r + 1) * tq] * hm_ref[kh]
                    acc = t if acc is None else acc + t
            o_ref[i * tq:(i + 1) * tq, g * KV_WIDTH:(g + 1) * KV_WIDTH] = acc


def _swa_sample(q, k_win, v_win, k, v, sinks, *, row_off, batch, tq, nb):
    w = WINDOW
    assert batch % nb == 0 and row_off % (nb * tq) == 0
    off = row_off // (nb * tq)
    bias = jnp.asarray(_swa_bias(tq).reshape(SWA_HEADS * tq, w + tq))
    sink_col = jnp.repeat(sinks.astype(F32), tq).reshape(SWA_HEADS * tq, 1)
    own = lambda width: pl.BlockSpec((nb * tq, width), lambda b: (off + b, 0))
    win = pl.BlockSpec((nb * w, KV_WIDTH), lambda b: (b, 0))
    return pl.pallas_call(
        functools.partial(_swa_sample_kernel, nb=nb, tq=tq),
        grid=(batch // nb,),
        in_specs=[own(MAIN_WIDTH), win, own(KV_WIDTH), win, own(KV_WIDTH), _const(bias.shape),
                  _const(sink_col.shape), _const((SWA_KV_HEADS, 1, KV_WIDTH))],
        out_specs=pl.BlockSpec((nb * tq, MAIN_WIDTH), lambda b: (b, 0)),
        out_shape=jax.ShapeDtypeStruct((batch * tq, MAIN_WIDTH), F32),
        compiler_params=_params(("arbitrary",)),
        name="swa_sample",
    )(q, k_win, k, v_win, v, bias, sink_col, _head_masks(SWA_KV_HEADS))


def kernel(x_prompt, x_sample, state_gla, cache_win_k, cache_win_v, cache_mem_k, cache_mem_v, mem_prompt, norm_mix_g, norm_ffn_g, norm_mem_g, w_mem_kv, mem_qn_g, mem_kn_g, w_out, w_in_a, w_gate_lr, b_gate_lr, gla_norm_g, w_in_b, swa_qn_g, swa_sinks, norm_kv_g, w_kv, swa_kn_g, w_router_group, b_router_group, w_router_expert, b_router_expert, w_exp_gate, w_exp_up, w_exp_down):
    bp, tp, d = x_prompt.shape
    bs, ts, _ = x_sample.shape
    assert bp == 1 and tp % WINDOW == 0 and ts * (GLA_CHUNK // ts) == GLA_CHUNK
    n_p, n_s = bp * tp, bs * ts
    w_buf = cache_win_k.shape[1]
    assert w_buf == WINDOW
    x_p, x_s = x_prompt.reshape(n_p, d), x_sample.reshape(n_s, d)

    mem_k_p, mem_v_p = _mem_kv(mem_prompt, norm_mem_g, w_mem_kv, mem_kn_g)
    cmk = cache_mem_k.reshape(cache_mem_k.shape[0], bs, MEM_LEN, MEM_WIDTH)
    cmv = cache_mem_v.reshape(cache_mem_v.shape[0], bs, MEM_LEN, MEM_WIDTH)

    def mem_attend(mq, l):
        tm_p = _row_tile(tp, 512)
        mo_p = _mem_attn(mq, mem_k_p, mem_v_p, mem_qn_g[l], row_off=0, seq=tp, tm=tm_p, bb=1, layer=l)
        mo_s = _mem_attn(mq, cmk, cmv, mem_qn_g[l], row_off=n_p, seq=ts, tm=ts, bb=8, layer=l)
        return mo_p, mo_s

    def moe(h, l, split_rows=None):
        return _moe(h, norm_ffn_g[l], w_router_group[l], b_router_group[l], w_router_expert[l],
                    b_router_expert[l], w_exp_gate, w_exp_up, w_exp_down, l, split_rows)

    q, k, la, v, og, mq = _inproj_a(x_p, x_s, norm_mix_g[0], w_in_a[0], w_gate_lr[0], b_gate_lr[0])
    zero_state = jnp.zeros((bp, GLA_HEADS, GLA_DK, GLA_DV), F32)
    n_sub = max(1, min(4, tp // GLA_CHUNK))
    main_p, gla_p = _gla(q, k, la, v, og, zero_state, gla_norm_g[0], row_off=0, seq=tp, n_seg=1, n_sub=n_sub)
    main_s, gla_s = _gla(q, k, la, v, og, state_gla[0], gla_norm_g[0], row_off=n_p, seq=ts,
                         n_seg=GLA_CHUNK // ts, n_sub=1)
    mo_p, mo_s = mem_attend(mq, 0)
    w_o = w_out[0]
    h = _outproj((x_p, x_s), main_p, mo_p, main_s, mo_s, w_o[:MAIN_WIDTH].reshape(GLA_HEADS, GLA_DV, d),
                 w_o[MAIN_WIDTH:])
    h = moe(h, 0)

    q, mq, k_sh, v_sh, kt_sh = _inproj_b(h, norm_kv_g, norm_mix_g[1], w_kv, w_in_b[0], swa_kn_g, swa_qn_g[0])
    ck = cache_win_k.reshape(bs * w_buf, KV_WIDTH)
    cv = cache_win_v.reshape(bs * w_buf, KV_WIDTH)
    main_p = _swa_prompt(q, kt_sh, v_sh, swa_sinks[0], n_rows=n_p, nb=2)
    main_s = _swa_sample(q, ck, cv, k_sh, v_sh, swa_sinks[0], row_off=n_p, batch=bs, tq=ts, nb=8)
    mo_p, mo_s = mem_attend(mq, 1)
    w_o = w_out[1]
    h = _outproj(h, main_p, mo_p, main_s, mo_s, w_o[:MAIN_WIDTH][_swa_perm()], w_o[MAIN_WIDTH:])
    y_p, y_s = moe(h, 1, split_rows=n_p)

    y_prompt = y_p.reshape(bp, tp, d)
    y_sample = y_s.reshape(bs, ts, d)
    k_new = k_sh[n_p:].reshape(bs, ts, SWA_KV_HEADS, HEAD_DIM)
    v_new = v_sh[n_p:].reshape(bs, ts, SWA_KV_HEADS, HEAD_DIM)
    win_k_s = jnp.concatenate([cache_win_k, k_new], axis=1)[:, -w_buf:]
    win_v_s = jnp.concatenate([cache_win_v, v_new], axis=1)[:, -w_buf:]
    win_k_p = k_sh[n_p - WINDOW:n_p].reshape(bp, WINDOW, SWA_KV_HEADS, HEAD_DIM)
    win_v_p = v_sh[n_p - WINDOW:n_p].reshape(bp, WINDOW, SWA_KV_HEADS, HEAD_DIM)
    mem_shape = (mem_k_p.shape[0], bp, MEM_LEN, MEM_HEADS, HEAD_DIM)
    return (y_prompt, y_sample, gla_p[None], gla_s[None], win_k_p, win_v_p, win_k_s, win_v_s,
            mem_k_p.reshape(mem_shape), mem_v_p.reshape(mem_shape))
```

```python
import functools

import numpy as np
import jax
import jax.numpy as jnp
from jax import lax
from jax.experimental import pallas as pl
from jax.experimental.pallas import tpu as pltpu

F32 = jnp.float32
BF16 = jnp.bfloat16

D_MODEL = 1024
MEM_LEN = 256
MEM_HEADS = 4
HEAD_DIM = 64
MEM_WIDTH = MEM_HEADS * HEAD_DIM
MAIN_WIDTH = D_MODEL - MEM_WIDTH
GLA_HEADS = 4
GLA_DV = MAIN_WIDTH // GLA_HEADS
GLA_DK = GLA_DV // 2
GLA_DK_PAD = 128
GLA_KEY_WIDTH = GLA_HEADS * GLA_DK
GLA_KEY_PAD = GLA_HEADS * GLA_DK_PAD
GLA_GATE_RANK = 16
GLA_TAU = 16.0
GLA_CHUNK = 64
SWA_HEADS = MAIN_WIDTH // HEAD_DIM
SWA_KV_HEADS = 4
SWA_GROUP = SWA_HEADS // SWA_KV_HEADS
KV_WIDTH = SWA_KV_HEADS * HEAD_DIM
WINDOW = 128
N_GROUPS = 4
EXPERTS_PER_GROUP = 8
N_EXPERTS = N_GROUPS * EXPERTS_PER_GROUP
D_EXPERT = 512
EPS = 1e-6
LANES = 128
NEG_BIG = -1e30
VMEM_LIMIT = 56 * 1024 * 1024
MOE_TILE = 256
ROUTER_LANE0 = N_GROUPS
ROUTER_META_ROWS = 8
SLABS = D_MODEL // LANES
PACKED_SLABS = SLABS // 2


def _bdot(a, b):
    return jnp.dot(a.astype(BF16), b.astype(BF16), preferred_element_type=F32)


def _bdot_nt(a, b):
    return lax.dot_general(a.astype(BF16), b.astype(BF16), (((1,), (1,)), ((), ())),
                           preferred_element_type=F32)


def _bdot_tn(a, b):
    return lax.dot_general(a.astype(BF16), b.astype(BF16), (((0,), (0,)), ((), ())),
                           preferred_element_type=F32)


def _split(x, n):
    parts = []
    for _ in range(n - 1):
        p = x.astype(BF16)
        parts.append(p)
        x = x - p.astype(F32)
    parts.append(x.astype(BF16))
    return parts


def _exact_left_dot(m, x, n=3):
    out = None
    for p in _split(x, n):
        t = jnp.dot(m, p, preferred_element_type=F32)
        out = t if out is None else out + t
    return out


def _seg_mean(x2, bd):
    out = None
    for p in _split(x2, 2):
        t = jnp.dot(p, bd, preferred_element_type=F32)
        out = t if out is None else out + t
    return out


def _rms_scale(x):
    return lax.rsqrt(jnp.mean(x * x, axis=-1, keepdims=True) + EPS)


def _row_tile(n, cap=512):
    t = cap
    while t > 8 and n % t:
        t //= 2
    assert n % t == 0, n
    return t


def _params(sem):
    return pltpu.CompilerParams(dimension_semantics=sem, vmem_limit_bytes=VMEM_LIMIT)


def _const(shape):
    nd = len(shape)
    return pl.BlockSpec(shape, lambda *_: (0,) * nd)


def _group_specs(tm, width, prompt_tiles, lead=None):
    p_idx = lambda i, *_: jnp.minimum(i, prompt_tiles - 1)
    s_idx = lambda i, *_: jnp.maximum(i - prompt_tiles, 0)
    if lead is None:
        return [pl.BlockSpec((tm, width), lambda i, *_, f=f: (f(i), 0)) for f in (p_idx, s_idx)]
    return [pl.BlockSpec((lead, tm, width), lambda i, *_, f=f: (0, f(i), 0)) for f in (p_idx, s_idx)]


def _block_diag_mean(width):
    i = np.arange(width)
    return jnp.asarray((i[:, None] // HEAD_DIM == i[None, :] // HEAD_DIM) / HEAD_DIM, BF16)


def _head_masks(n_heads):
    i = np.arange(n_heads * HEAD_DIM)
    return jnp.asarray((i[None, :] // HEAD_DIM == np.arange(n_heads)[:, None]), F32)[:, None, :]


def _mem_kv_kernel(mem_ref, g_ref, w_ref, kng_ref, bd_ref, k_ref, v_ref):
    x = mem_ref[0]
    hn = x * _rms_scale(x) * g_ref[0]
    kv = _bdot(hn, w_ref[0])
    k = kv[:, :MEM_WIDTH]
    k = k * lax.rsqrt(_seg_mean(k * k, bd_ref[...]) + EPS) * kng_ref[0]
    k_ref[0, 0] = k.T
    v_ref[0, 0] = kv[:, MEM_WIDTH:].T


def _mem_kv(mem, g, w, kng):
    depth, (b, m, d) = w.shape[0], mem.shape
    out = jax.ShapeDtypeStruct((depth, b, m, MEM_WIDTH), F32)
    blk = pl.BlockSpec((1, 1, m, MEM_WIDTH), lambda l, i: (l, i, 0, 0))
    return pl.pallas_call(
        _mem_kv_kernel,
        grid=(depth, b),
        in_specs=[pl.BlockSpec((1, m, d), lambda l, i: (i, 0, 0)),
                  pl.BlockSpec((1, 1, d), lambda l, i: (l, 0, 0)),
                  pl.BlockSpec((1, d, 2 * MEM_WIDTH), lambda l, i: (l, 0, 0)),
                  pl.BlockSpec((1, 1, MEM_WIDTH), lambda l, i: (l, 0, 0)),
                  _const((MEM_WIDTH, MEM_WIDTH))],
        out_specs=[blk, blk],
        out_shape=[out, out],
        compiler_params=_params(("arbitrary", "arbitrary")),
        name="mem_kv",
    )(mem, g.reshape(depth, 1, d), w.astype(BF16),
      jnp.tile(kng, (1, MEM_HEADS)).reshape(depth, 1, MEM_WIDTH), _block_diag_mean(MEM_WIDTH))


def _inproj_a_kernel(xp_ref, xs_ref, g_ref, wq_ref, wk_ref, wv_ref, wog_ref, wlr_ref, wmq_ref, wgl_ref, bgl_ref,
                     q_ref, k_ref, la_ref, v_ref, og_ref, mq_ref, *, prompt_tiles):
    x = jnp.where(pl.program_id(0) < prompt_tiles, xp_ref[...], xs_ref[...])
    hn = (x * _rms_scale(x) * g_ref[...]).astype(BF16)
    q_ref[...] = jnp.dot(hn, wq_ref[...], preferred_element_type=F32) * (GLA_DK ** -0.5)
    k_ref[...] = jnp.dot(hn, wk_ref[...], preferred_element_type=F32)
    for h in range(GLA_HEADS):
        v_ref[h] = jnp.dot(hn, wv_ref[h], preferred_element_type=F32)
        og_ref[h] = jnp.dot(hn, wog_ref[h], preferred_element_type=F32)
    lr = jnp.dot(hn, wlr_ref[...], preferred_element_type=F32)
    z = _bdot(lr, wgl_ref[...]) + bgl_ref[...]
    la_ref[...] = (jnp.minimum(z, 0.0) - jnp.log(1.0 + jnp.exp(-jnp.abs(z)))) * (1.0 / GLA_TAU)
    mq_ref[...] = jnp.dot(hn, wmq_ref[...], preferred_element_type=F32)


def _pad_heads(w, width, pad):
    lead = w.shape[:-1]
    w = w.reshape(*lead, GLA_HEADS, width)
    w = jnp.pad(w, [(0, 0)] * len(lead) + [(0, 0), (0, pad - width)])
    return w.reshape(*lead, GLA_HEADS * pad)


def _inproj_a(x_p, x_s, g, w_in, w_lr, b_lr):
    (n_p, d), n_s = x_p.shape, x_s.shape[0]
    n = n_p + n_s
    tm = _row_tile(n_s)
    assert n_p % tm == 0
    pt = n_p // tm
    c0, c1, c2, c3, c4 = (GLA_KEY_WIDTH, 2 * GLA_KEY_WIDTH, 2 * GLA_KEY_WIDTH + MAIN_WIDTH,
                          2 * GLA_KEY_WIDTH + 2 * MAIN_WIDTH,
                          2 * GLA_KEY_WIDTH + 2 * MAIN_WIDTH + GLA_GATE_RANK)
    wb = w_in.astype(BF16)
    wq = _pad_heads(wb[:, :c0], GLA_DK, GLA_DK_PAD)
    wk = _pad_heads(wb[:, c0:c1], GLA_DK, GLA_DK_PAD)
    wv = wb[:, c1:c2].reshape(d, GLA_HEADS, GLA_DV).transpose(1, 0, 2)
    wog = wb[:, c2:c3].reshape(d, GLA_HEADS, GLA_DV).transpose(1, 0, 2)
    wlr = jnp.pad(wb[:, c3:c4], ((0, 0), (0, LANES - GLA_GATE_RANK)))
    wmq = wb[:, c4:]
    wgl = jnp.pad(_pad_heads(w_lr.astype(BF16), GLA_DK, GLA_DK_PAD), ((0, LANES - GLA_GATE_RANK), (0, 0)))
    bgl = _pad_heads(b_lr.reshape(1, -1), GLA_DK, GLA_DK_PAD)
    row = lambda w: pl.BlockSpec((tm, w), lambda i: (i, 0))
    hrow = pl.BlockSpec((GLA_HEADS, tm, GLA_DV), lambda i: (0, i, 0))
    key = jax.ShapeDtypeStruct((n, GLA_KEY_PAD), F32)
    val = jax.ShapeDtypeStruct((GLA_HEADS, n, GLA_DV), F32)
    return pl.pallas_call(
        functools.partial(_inproj_a_kernel, prompt_tiles=pt),
        grid=(n // tm,),
        in_specs=_group_specs(tm, d, pt) + [
            _const((1, d)), _const(wq.shape), _const(wk.shape), _const(wv.shape),
            _const(wog.shape), _const(wlr.shape), _const(wmq.shape), _const(wgl.shape),
            _const(bgl.shape)],
        out_specs=[row(GLA_KEY_PAD), row(GLA_KEY_PAD), row(GLA_KEY_PAD), hrow, hrow, row(MEM_WIDTH)],
        out_shape=[key, key, key, val, val, jax.ShapeDtypeStruct((n, MEM_WIDTH), F32)],
        compiler_params=_params(("parallel",)),
        name="inproj_a",
    )(x_p, x_s, g.reshape(1, d), wq, wk, wv, wog, wlr, wmq, wgl, bgl)


def _gla_kernel(q_ref, k_ref, la_ref, v_ref, og_ref, s0_ref, gn_ref, mcum_ref, mall_ref, sel_ref,
                o_ref, sout_ref, s_ref, *, chunk, n_sub, n_seg):
    j = pl.program_id(1)
    seg = chunk // n_seg

    @pl.when(j == 0)
    def _():
        s_ref[...] = jnp.zeros_like(s_ref)
        s_ref[:, :, :GLA_DK, :] = s0_ref[...]

    mcum = mcum_ref[...]
    causal = mcum.astype(F32) > 0.0
    row = lax.broadcasted_iota(jnp.int32, (chunk, GLA_DK_PAD), 0)
    gn = gn_ref[...]
    hcols = [slice(h * GLA_DK_PAD, (h + 1) * GLA_DK_PAD) for h in range(GLA_HEADS)]
    crows = [slice(c * chunk, (c + 1) * chunk) for c in range(n_sub)]
    qts, kts, kds, e_ends = [], [], [], []
    for rows in crows:
        la = la_ref[rows, :]
        b = _exact_left_dot(mcum, la)
        b_end = _exact_left_dot(mall_ref[...], la)
        e_ends.append(jnp.exp(_exact_left_dot(sel_ref[...], la)).T)
        k = k_ref[rows, :]
        qts.append(q_ref[rows, :] * jnp.exp(b))
        kts.append((k * jnp.exp(-b)).astype(BF16))
        kds.append(k * jnp.exp(b_end - b))
    vbs = [[v_ref[h, rows, :].astype(BF16) for h in range(GLA_HEADS)] for rows in crows]
    scores = [[_bdot_nt(qts[c][:, cols], kts[c][:, cols]) for cols in hcols] for c in range(n_sub)]
    kvs = []
    for c in range(n_sub):
        per_head = []
        for h, cols in enumerate(hcols):
            per_seg = []
            for s in range(n_seg):
                kd = kds[c][:, cols]
                if n_seg > 1:
                    kd = jnp.where((row >= s * seg) & (row < (s + 1) * seg), kd, 0.0)
                per_seg.append(_bdot_tn(kd, vbs[c][h]))
            per_head.append(per_seg)
        kvs.append(per_head)
    state = [[s_ref[s, h] for s in range(n_seg)] for h in range(GLA_HEADS)]
    inters = []
    for c in range(n_sub):
        per_head = []
        for h, cols in enumerate(hcols):
            parts = []
            for s in range(n_seg):
                parts.append(_bdot(qts[c][s * seg:(s + 1) * seg, cols], state[h][s]))
                state[h][s] = e_ends[c][cols, s:s + 1] * state[h][s] + kvs[c][h][s]
            per_head.append(parts[0] if n_seg == 1 else jnp.concatenate(parts, axis=0))
        inters.append(per_head)
    for h in range(GLA_HEADS):
        for s in range(n_seg):
            s_ref[s, h] = state[h][s]
    for c, rows in enumerate(crows):
        for h in range(GLA_HEADS):
            a = jnp.where(causal, scores[c][h], 0.0)
            o = _bdot(a, vbs[c][h]) + inters[c][h]
            on = o * lax.rsqrt(jnp.mean(o * o, axis=-1, keepdims=True) + EPS) * gn
            og = og_ref[h, rows, :]
            o_ref[h, rows, :] = on * (og * jax.nn.sigmoid(og))

    @pl.when(j == pl.num_programs(1) - 1)
    def _():
        sout_ref[...] = s_ref[:, :, :GLA_DK, :]


def _gla(q, k, la, v, og, s0, gnorm, *, row_off, seq, n_seg, n_sub):
    batch = s0.shape[0]
    chunk = GLA_CHUNK
    assert chunk % n_seg == 0 and batch % n_seg == 0
    seg = chunk // n_seg
    step_rows = n_sub * chunk
    if n_seg > 1:
        assert seq == seg and n_sub == 1
        t_steps = 1
    else:
        assert seq % step_rows == 0
        t_steps = seq // step_rows
    assert row_off % step_rows == 0
    off = row_off // step_rows
    i = np.arange(chunk)
    same = (i[:, None] // seg) == (i[None, :] // seg)
    mcum = jnp.asarray(same & (i[None, :] <= i[:, None]), BF16)
    mall = jnp.asarray(same, BF16)
    sel = jnp.asarray((i[None, :] // seg) == np.arange(LANES)[:, None], BF16)
    ridx = lambda g, j: (off + g * t_steps + j, 0)
    hidx = lambda g, j: (0, off + g * t_steps + j, 0)
    key_spec = pl.BlockSpec((step_rows, GLA_KEY_PAD), ridx)
    val_spec = pl.BlockSpec((GLA_HEADS, step_rows, GLA_DV), hidx)
    st_spec = pl.BlockSpec((n_seg, GLA_HEADS, GLA_DK, GLA_DV), lambda g, j: (g, 0, 0, 0))
    in_specs = [key_spec, key_spec, key_spec, val_spec, val_spec, st_spec, _const((1, GLA_DV)),
                _const((chunk, chunk)), _const((chunk, chunk)), _const((LANES, chunk))]
    args = [q, k, la, v, og, s0, gnorm.reshape(1, GLA_DV), mcum, mall, sel]
    out_spec = pl.BlockSpec((GLA_HEADS, step_rows, GLA_DV), lambda g, j: (0, g * t_steps + j, 0))
    return pl.pallas_call(
        functools.partial(_gla_kernel, chunk=chunk, n_sub=n_sub, n_seg=n_seg),
        grid=(batch // n_seg, t_steps),
        in_specs=in_specs,
        out_specs=[out_spec, st_spec],
        out_shape=[jax.ShapeDtypeStruct((GLA_HEADS, batch * seq, GLA_DV), F32),
                   jax.ShapeDtypeStruct(s0.shape, F32)],
        scratch_shapes=[pltpu.VMEM((n_seg, GLA_HEADS, GLA_DK_PAD, GLA_DV), F32)],
        compiler_params=_params(("arbitrary", "arbitrary")),
        name="gla",
    )(*args)


def _mem_attn_kernel(q_ref, k_ref, v_ref, g_ref, bd_ref, hm_ref, o_ref, *, tm, bb):
    g = g_ref[...]
    sub = min(tm, 128)
    units = [(i, i * tm + r) for i in range(bb) for r in range(0, tm, sub)]
    scores = []
    for i, r in units:
        q = q_ref[r:r + sub, :]
        qn = q * lax.rsqrt(_seg_mean(q * q, bd_ref[...]) + EPS) * g
        qs = jnp.concatenate([(qn * hm_ref[h]).astype(BF16) for h in range(MEM_HEADS)], axis=0)
        scores.append(_bdot(qs, k_ref[i]))
    probs = []
    for s in scores:
        e = jnp.exp(s - jnp.max(s, axis=-1, keepdims=True))
        probs.append(e * (1.0 / jnp.sum(e, axis=-1, keepdims=True)))
    outs = [_bdot_nt(p, v_ref[i]) for (i, _), p in zip(units, probs)]
    for (_, r), o in zip(units, outs):
        acc = o[:sub] * hm_ref[0]
        for h in range(1, MEM_HEADS):
            acc = acc + o[h * sub:(h + 1) * sub] * hm_ref[h]
        o_ref[r:r + sub, :] = acc


def _mem_attn(mq, mk, mv, qng, *, row_off, seq, tm, bb, layer):
    depth, batch, m, _ = mk.shape
    mk = mk.reshape(depth * batch, m, MEM_WIDTH)
    mv = mv.reshape(depth * batch, m, MEM_WIDTH)
    kv_off = layer * batch // bb
    assert seq % tm == 0 and batch % bb == 0 and (bb == 1 or seq == tm)
    t_steps = seq // tm
    step_rows = bb * tm
    assert row_off % step_rows == 0
    off = row_off // step_rows
    row_spec = pl.BlockSpec((step_rows, MEM_WIDTH), lambda g, j: (off + g * t_steps + j, 0))
    kv_spec = pl.BlockSpec((bb, m, MEM_WIDTH), lambda g, j: (kv_off + g, 0, 0))
    in_specs = [row_spec, kv_spec, kv_spec, _const((1, MEM_WIDTH)), _const((MEM_WIDTH, MEM_WIDTH)),
                _const((MEM_HEADS, 1, MEM_WIDTH))]
    args = [mq, mk, mv, (jnp.tile(qng, MEM_HEADS) * HEAD_DIM ** -0.5).reshape(1, MEM_WIDTH),
            _block_diag_mean(MEM_WIDTH), _head_masks(MEM_HEADS)]
    return pl.pallas_call(
        functools.partial(_mem_attn_kernel, tm=tm, bb=bb),
        grid=(batch // bb, t_steps),
        in_specs=in_specs,
        out_specs=pl.BlockSpec((step_rows, MEM_WIDTH), lambda g, j: (g * t_steps + j, 0)),
        out_shape=jax.ShapeDtypeStruct((batch * seq, MEM_WIDTH), F32),
        compiler_params=_params(("parallel", "parallel")),
        name="mem_attn",
    )(*args)


def _outproj_kernel(*refs, heads, prompt_tiles, split_residual):
    if split_residual:
        hp_ref, hs_ref = refs[:2]
        refs = refs[2:]
    else:
        hp_ref = hs_ref = refs[0]
        refs = refs[1:]
    main_p_ref, main_s_ref, mo_p_ref, mo_s_ref, wmain_ref, wmo_ref, o_ref = refs

    def project(h_ref, main_ref, mo_ref):
        acc = h_ref[...] + _bdot(mo_ref[...], wmo_ref[...])
        if heads:
            for h in range(heads):
                acc = acc + _bdot(main_ref[h], wmain_ref[h])
        else:
            acc = acc + _bdot(main_ref[...], wmain_ref[...])
        o_ref[...] = acc

    @pl.when(pl.program_id(0) < prompt_tiles)
    def _():
        project(hp_ref, main_p_ref, mo_p_ref)

    @pl.when(pl.program_id(0) >= prompt_tiles)
    def _():
        project(hs_ref, main_s_ref, mo_s_ref)


def _outproj(h, main_p, mo_p, main_s, mo_s, w_main, w_mo):
    n_p, n_s = mo_p.shape[0], mo_s.shape[0]
    n, d = n_p + n_s, w_mo.shape[1]
    tm = _row_tile(n_s)
    assert n_p % tm == 0
    pt = n_p // tm
    heads = main_p.shape[0] if main_p.ndim == 3 else 0
    row = pl.BlockSpec((tm, d), lambda i: (i, 0))
    split = isinstance(h, tuple)
    h_specs, h_args = (_group_specs(tm, d, pt), list(h)) if split else ([row], [h])
    main_specs = _group_specs(tm, main_p.shape[-1], pt, lead=heads or None)
    return pl.pallas_call(
        functools.partial(_outproj_kernel, heads=heads, prompt_tiles=pt, split_residual=split),
        grid=(n // tm,),
        in_specs=h_specs + main_specs + _group_specs(tm, MEM_WIDTH, pt) + [_const(w_main.shape),
                                                                           _const(w_mo.shape)],
        out_specs=row,
        out_shape=jax.ShapeDtypeStruct((n, d), F32),
        compiler_params=_params(("parallel",)),
        name="outproj",
    )(*h_args, main_p, main_s, mo_p, mo_s, w_main.astype(BF16), w_mo.astype(BF16))


def _router_kernel(h_ref, g_ref, whi_ref, wlo_ref, b_ref, tril_ref, mi_ref, mf_ref, cnt_ref, carry_ref):
    i = pl.program_id(0)

    @pl.when(i == 0)
    def _():
        carry_ref[...] = jnp.zeros_like(carry_ref)

    x = h_ref[...]
    xn = x * _rms_scale(x) * g_ref[...]
    x_hi, x_lo = _split(xn, 2)
    logits = (jnp.dot(x_hi, whi_ref[...], preferred_element_type=F32)
              + jnp.dot(x_hi, wlo_ref[...], preferred_element_type=F32)
              + jnp.dot(x_lo, whi_ref[...], preferred_element_type=F32)) + b_ref[...]
    tm = x.shape[0]
    lane = lax.broadcasted_iota(jnp.int32, (tm, LANES), 1)
    far = jnp.int32(2 * LANES)

    def first_max(vals):
        m = jnp.max(vals, axis=-1, keepdims=True)
        return m, jnp.min(jnp.where(vals == m, lane, far), axis=-1, keepdims=True)

    gl = jnp.where(lane < N_GROUPS, logits, -jnp.inf)
    gmax, grp = first_max(gl)
    pg_sel = 1.0 / jnp.sum(jnp.exp(gl - gmax), axis=-1, keepdims=True)
    lo = ROUTER_LANE0 + grp * EXPERTS_PER_GROUP
    el = jnp.where((lane >= lo) & (lane < lo + EXPERTS_PER_GROUP), logits, -jnp.inf)
    m1, i1 = first_max(el)
    m2, i2 = first_max(jnp.where(lane == i1, -jnp.inf, el))
    e2 = jnp.exp(m2 - m1)
    g1 = pg_sel / (1.0 + e2)
    g2 = pg_sel * e2 / (1.0 + e2)

    oh1 = lane == i1
    oh2 = lane == i2
    picked = jnp.where(oh1 | oh2, 1.0, 0.0)
    before = jnp.dot(tril_ref[...], picked.astype(BF16), preferred_element_type=F32) + carry_ref[...]
    rank1 = jnp.sum(jnp.where(oh1, before, 0.0), axis=-1, keepdims=True)
    rank2 = jnp.sum(jnp.where(oh2, before, 0.0), axis=-1, keepdims=True)
    carry = carry_ref[...] + jnp.sum(picked, axis=0, keepdims=True)
    carry_ref[...] = carry
    cnt_ref[...] = carry

    zi = jnp.zeros((tm, LANES), jnp.int32)
    mi = jnp.where(lane == 0, i1 - ROUTER_LANE0, zi)
    mi = jnp.where(lane == 1, i2 - ROUTER_LANE0, mi)
    mi = jnp.where(lane == 2, rank1.astype(jnp.int32), mi)
    mi = jnp.where(lane == 3, rank2.astype(jnp.int32), mi)
    mi_ref[...] = mi.T[:ROUTER_META_ROWS]
    zf = jnp.zeros((tm, LANES), F32)
    mf_ref[...] = jnp.where(lane == 0, g1, jnp.where(lane == 1, g2, zf))


def _router(h, g, w_rg, b_rg, w_re, b_re):
    n, d = h.shape
    tm = _row_tile(n)
    n_real = N_GROUPS + N_EXPERTS
    w = jnp.pad(jnp.concatenate([w_rg, w_re], axis=1), ((0, 0), (0, LANES - n_real)))
    b = jnp.pad(jnp.concatenate([b_rg, b_re]), (0, LANES - n_real)).reshape(1, LANES)
    w_hi = w.astype(BF16)
    w_lo = (w - w_hi.astype(F32)).astype(BF16)
    i = np.arange(tm)
    tril = jnp.asarray(i[None, :] < i[:, None], BF16)
    row = lambda width: pl.BlockSpec((tm, width), lambda i: (i, 0))
    return pl.pallas_call(
        _router_kernel,
        grid=(n // tm,),
        in_specs=[row(d), _const((1, d)), _const((d, LANES)), _const((d, LANES)), _const((1, LANES)),
                  _const((tm, tm))],
        out_specs=[pl.BlockSpec((ROUTER_META_ROWS, tm), lambda i: (0, i)), row(LANES), _const((1, LANES))],
        out_shape=[jax.ShapeDtypeStruct((ROUTER_META_ROWS, n), jnp.int32),
                   jax.ShapeDtypeStruct((n, LANES), F32), jax.ShapeDtypeStruct((1, LANES), F32)],
        scratch_shapes=[pltpu.VMEM((1, LANES), F32)],
        compiler_params=_params(("arbitrary",)),
        name="moe_router",
    )(h, g.reshape(1, d), w_hi, w_lo, b, tril)


def _to_rows(ref, rows, lead=()):
    return jnp.concatenate([ref[lead + (pl.ds(s, rows, stride=SLABS), slice(None))] for s in range(SLABS)],
                           axis=1)


def _from_rows(ref, x, rows, lead=()):
    for s in range(SLABS):
        ref[lead + (pl.ds(s, rows, stride=SLABS), slice(None))] = x[:, s * LANES:(s + 1) * LANES]


def _pack_rows(ref, x, rows, lead=()):
    u32 = jnp.uint32
    for w in range(PACKED_SLABS):
        lo = x[:, (2 * w) * LANES:(2 * w + 1) * LANES].astype(BF16).astype(F32)
        hi = x[:, (2 * w + 1) * LANES:(2 * w + 2) * LANES].astype(BF16).astype(F32)
        word = (lax.bitcast_convert_type(lo, u32) >> 16) | (lax.bitcast_convert_type(hi, u32) & u32(0xFFFF0000))
        ref[lead + (pl.ds(w, rows, stride=PACKED_SLABS), slice(None))] = word


def _unpack_rows(ref, rows, lead=()):
    u32 = jnp.uint32
    slabs = []
    for w in range(PACKED_SLABS):
        word = ref[lead + (pl.ds(w, rows, stride=PACKED_SLABS), slice(None))]
        slabs.append(lax.bitcast_convert_type(word << 16, F32).astype(BF16))
        slabs.append(lax.bitcast_convert_type(word & u32(0xFFFF0000), F32).astype(BF16))
    return jnp.concatenate(slabs, axis=1)


def _dispatch_kernel(pos_ref, h_ref, g_ref, xs_hbm, buf, sem, *, tm, n, steps):
    i = pl.program_id(0)
    slot = lax.rem(i, 2)
    ps = PACKED_SLABS

    def wait_slot(sl):
        for _ in range(2):
            pltpu.make_async_copy(buf.at[sl], xs_hbm.at[pl.ds(0, tm * ps)], sem.at[sl]).wait()

    @pl.when(i >= 2)
    def _():
        wait_slot(slot)

    x = h_ref[...]
    _pack_rows(buf, x * _rms_scale(x) * g_ref[...], tm, (slot,))

    def issue(r, carry):
        src = buf.at[slot, pl.ds(pl.multiple_of(r * ps, ps), ps)]
        for kk in range(2):
            p = pos_ref[kk * n + i * tm + r]
            dst = xs_hbm.at[pl.ds(pl.multiple_of(p * ps, ps), ps)]
            pltpu.make_async_copy(src, dst, sem.at[slot]).start(priority=kk)
        return carry

    lax.fori_loop(0, tm, issue, 0)

    @pl.when(i == steps - 1)
    def _():
        wait_slot(slot)
        if steps > 1:
            wait_slot(1 - slot)


def _dispatch(h, g, pos_flat):
    n, d = h.shape
    tm = _row_tile(n, MOE_TILE)
    steps = n // tm
    return pl.pallas_call(
        functools.partial(_dispatch_kernel, tm=tm, n=n, steps=steps),
        grid_spec=pltpu.PrefetchScalarGridSpec(
            num_scalar_prefetch=1,
            grid=(steps,),
            in_specs=[pl.BlockSpec((tm, d), lambda i, pos: (i, 0)),
                      pl.BlockSpec((1, d), lambda i, pos: (0, 0))],
            out_specs=pl.BlockSpec(memory_space=pl.ANY),
            scratch_shapes=[pltpu.VMEM((2, tm * PACKED_SLABS, LANES), jnp.uint32),
                            pltpu.SemaphoreType.DMA((2,))],
        ),
        out_shape=jax.ShapeDtypeStruct((2 * n * PACKED_SLABS, LANES), jnp.uint32),
        compiler_params=_params(("arbitrary",)),
        name="moe_dispatch",
    )(pos_flat, h, g.reshape(1, d))


ITEM_FIRST, ITEM_LAST, ITEM_NEW_EXPERT = 1, 2, 4


def _expert_kernel(tile_ref, exp_ref, lo_ref, hi_ref, flag_ref, n_ref, xs_ref, wg_ref, wu_ref, wd_ref, ys_ref,
                   wgb, wub, wdb, acc, *, tm):
    w = pl.program_id(0)

    @pl.when(w < n_ref[0])
    def _():
        flags = flag_ref[w]

        @pl.when((flags & ITEM_NEW_EXPERT) != 0)
        def _():
            wgb[...] = wg_ref[0, 0].astype(BF16)
            wub[...] = wu_ref[0, 0].astype(BF16)
            wdb[...] = wd_ref[0, 0].astype(BF16)

        x = _unpack_rows(xs_ref, tm)
        hg = jnp.dot(x, wgb[...], preferred_element_type=F32)
        hu = jnp.dot(x, wub[...], preferred_element_type=F32)
        act = (hg * jax.nn.sigmoid(hg) * hu).astype(BF16)
        y = jnp.dot(act, wdb[...], preferred_element_type=F32)
        row = lax.broadcasted_iota(jnp.int32, (tm, 1), 0)
        y = jnp.where((row >= lo_ref[w]) & (row < hi_ref[w]), y, 0.0)
        first = (flags & ITEM_FIRST) != 0

        @pl.when(first)
        def _():
            acc[...] = y

        @pl.when(jnp.logical_not(first))
        def _():
            acc[...] += y

        @pl.when((flags & ITEM_LAST) != 0)
        def _():
            _from_rows(ys_ref, acc[...], tm)


def _experts(xs, items, w_g, w_u, w_d, layer):
    tile, expert, lo, hi, flags, n_items = items
    d = D_MODEL
    tm = MOE_TILE
    tok_spec = pl.BlockSpec((tm * SLABS, LANES), lambda w, tile, *_: (tile[w], 0))
    packed_spec = pl.BlockSpec((tm * PACKED_SLABS, LANES), lambda w, tile, *_: (tile[w], 0))
    wmap = lambda w, tile, expert, *_: (layer, expert[w], 0, 0)
    return pl.pallas_call(
        functools.partial(_expert_kernel, tm=tm),
        grid_spec=pltpu.PrefetchScalarGridSpec(
            num_scalar_prefetch=6,
            grid=(tile.shape[0],),
            in_specs=[packed_spec,
                      pl.BlockSpec((1, 1, d, D_EXPERT), wmap),
                      pl.BlockSpec((1, 1, d, D_EXPERT), wmap),
                      pl.BlockSpec((1, 1, D_EXPERT, d), wmap)],
            out_specs=tok_spec,
            scratch_shapes=[pltpu.VMEM((d, D_EXPERT), BF16), pltpu.VMEM((d, D_EXPERT), BF16),
                            pltpu.VMEM((D_EXPERT, d), BF16), pltpu.VMEM((tm, d), F32)],
        ),
        out_shape=jax.ShapeDtypeStruct((xs.shape[0] // PACKED_SLABS * SLABS, LANES), F32),
        compiler_params=_params(("arbitrary",)),
        name="moe_experts",
    )(tile, expert, lo, hi, flags, n_items, xs, w_g, w_u, w_d)


def _combine_kernel(pos_ref, h_ref, gate_ref, ys_hbm, *refs, tm, n, steps, prompt_tiles):
    out_refs, (buf, sem) = refs[:-2], refs[-2:]
    i = pl.program_id(0)
    slot = lax.rem(i, 2)

    def issue(step, sl):
        def body(r, carry):
            for kk in range(2):
                p = pos_ref[kk * n + step * tm + r]
                src = ys_hbm.at[pl.ds(pl.multiple_of(p * SLABS, SLABS), SLABS)]
                dst = buf.at[sl, kk, pl.ds(pl.multiple_of(r * SLABS, SLABS), SLABS)]
                pltpu.make_async_copy(src, dst, sem.at[sl]).start(priority=kk)
            return carry

        lax.fori_loop(0, tm, body, 0)

    @pl.when(i == 0)
    def _():
        issue(0, 0)

    @pl.when(i + 1 < steps)
    def _():
        issue(i + 1, 1 - slot)

    for kk in range(2):
        pltpu.make_async_copy(ys_hbm.at[pl.ds(0, tm * SLABS)], buf.at[slot, kk], sem.at[slot]).wait()
    g = gate_ref[...]
    out = (h_ref[...] + g[:, 0:1] * _to_rows(buf, tm, (slot, 0))
           + g[:, 1:2] * _to_rows(buf, tm, (slot, 1)))
    if len(out_refs) == 1:
        out_refs[0][...] = out
    else:
        @pl.when(i < prompt_tiles)
        def _():
            out_refs[0][...] = out

        @pl.when(i >= prompt_tiles)
        def _():
            out_refs[1][...] = out


def _combine(h, gates, ys, pos_flat, split_rows=None):
    n, d = h.shape
    tm = _row_tile(n if split_rows is None else n - split_rows, MOE_TILE)
    steps = n // tm
    row = pl.BlockSpec((tm, d), lambda i, pos: (i, 0))
    if split_rows is None:
        pt, out_specs, out_shape = 0, row, jax.ShapeDtypeStruct((n, d), F32)
    else:
        assert split_rows % tm == 0
        pt = split_rows // tm
        out_specs = _group_specs(tm, d, pt)
        out_shape = [jax.ShapeDtypeStruct((split_rows, d), F32), jax.ShapeDtypeStruct((n - split_rows, d), F32)]
    return pl.pallas_call(
        functools.partial(_combine_kernel, tm=tm, n=n, steps=steps, prompt_tiles=pt),
        grid_spec=pltpu.PrefetchScalarGridSpec(
            num_scalar_prefetch=1,
            grid=(steps,),
            in_specs=[row, pl.BlockSpec((tm, LANES), lambda i, pos: (i, 0)),
                      pl.BlockSpec(memory_space=pl.ANY)],
            out_specs=out_specs,
            scratch_shapes=[pltpu.VMEM((2, 2, tm * SLABS, LANES), F32), pltpu.SemaphoreType.DMA((2,))],
        ),
        out_shape=out_shape,
        compiler_params=_params(("arbitrary",)),
        name="moe_combine",
    )(pos_flat, h, gates, ys)


def _lookup(tables, idx):
    hit = idx[:, None] == jnp.arange(tables.shape[1], dtype=idx.dtype)[None, :]
    return jnp.sum(jnp.where(hit[None], tables[:, None, :], 0), axis=2)


def _slot_pos_kernel(starts_ref, meta_ref, pos_ref):
    expert = meta_ref[0:2, :]
    start = jnp.zeros(expert.shape, jnp.int32)
    for e in range(N_EXPERTS):
        start = jnp.where(expert == e, starts_ref[e], start)
    pos_ref[...] = start + meta_ref[2:4, :]


def _slot_positions(starts, meta):
    n = meta.shape[1]
    return pl.pallas_call(
        _slot_pos_kernel,
        grid_spec=pltpu.PrefetchScalarGridSpec(
            num_scalar_prefetch=1,
            grid=(1,),
            in_specs=[pl.BlockSpec((ROUTER_META_ROWS, n), lambda i, starts: (0, 0))],
            out_specs=pl.BlockSpec((2, n), lambda i, starts: (0, 0)),
        ),
        out_shape=jax.ShapeDtypeStruct((2, n), jnp.int32),
        compiler_params=_params(("arbitrary",)),
        name="moe_slot_pos",
    )(starts, meta)


def _work_items(counts, n_slots):
    tm = MOE_TILE
    ends = jnp.cumsum(counts)
    starts = ends - counts
    first_tile = starts // tm
    n_tiles_e = jnp.where(counts > 0, (ends - 1) // tm - first_tile + 1, 0)
    item_end = jnp.cumsum(n_tiles_e)
    item_start = item_end - n_tiles_e
    n_items = item_end[-1]
    max_items = n_slots // tm + N_EXPERTS - 1
    w = jnp.minimum(jnp.arange(max_items, dtype=jnp.int32), n_items - 1)
    expert = jnp.sum(w[:, None] >= item_end[None, :], axis=1).astype(jnp.int32)
    first_tile_w, item_start_w, start_w, end_w = _lookup(jnp.stack([first_tile, item_start, starts, ends]), expert)
    tile = first_tile_w + w - item_start_w
    lo = jnp.maximum(start_w - tile * tm, 0)
    hi = jnp.minimum(end_w - tile * tm, tm)
    prev_tile = jnp.concatenate([jnp.full((1,), -1, jnp.int32), tile[:-1]])
    next_tile = jnp.concatenate([tile[1:], jnp.full((1,), -1, jnp.int32)])
    prev_expert = jnp.concatenate([jnp.full((1,), -1, jnp.int32), expert[:-1]])
    idx = jnp.arange(max_items, dtype=jnp.int32)
    flags = (jnp.where(tile != prev_tile, ITEM_FIRST, 0)
             | jnp.where((tile != next_tile) | (idx == n_items - 1), ITEM_LAST, 0)
             | jnp.where(expert != prev_expert, ITEM_NEW_EXPERT, 0))
    as_i32 = lambda a: a.astype(jnp.int32)
    return (as_i32(tile), expert, as_i32(lo), as_i32(hi), as_i32(flags), as_i32(n_items).reshape(1))


def _moe(h, g, w_rg, b_rg, w_re, b_re, w_g, w_u, w_d, layer, split_rows=None):
    n, _ = h.shape
    meta, gates, cnt = _router(h, g, w_rg, b_rg, w_re, b_re)
    counts = cnt[0, ROUTER_LANE0:ROUTER_LANE0 + N_EXPERTS].astype(jnp.int32)
    starts = jnp.cumsum(counts) - counts
    pos = _slot_positions(starts, meta).reshape(-1)
    xs = _dispatch(h, g, pos)
    ys = _experts(xs, _work_items(counts, 2 * n), w_g, w_u, w_d, layer)
    return _combine(h, gates, ys, pos, split_rows)


def _inproj_b_kernel(x_ref, gkv_ref, gmix_ref, wkv_ref, win_ref, kng_ref, qng_ref, bdk_ref, bdq_ref,
                     q_ref, mq_ref, k_ref, v_ref, kt_ref):
    x = x_ref[...]
    xr = x * _rms_scale(x)
    kv = _bdot(xr * gkv_ref[...], wkv_ref[...])
    k = kv[:, :KV_WIDTH]
    k = k * lax.rsqrt(_seg_mean(k * k, bdk_ref[...]) + EPS) * kng_ref[...]
    k_ref[...] = k
    kt_ref[...] = k.T
    v_ref[...] = kv[:, KV_WIDTH:]
    proj = _bdot(xr * gmix_ref[...], win_ref[...])
    q = proj[:, :MAIN_WIDTH]
    q_ref[...] = q * lax.rsqrt(_seg_mean(q * q, bdq_ref[...]) + EPS) * qng_ref[...]
    mq_ref[...] = proj[:, MAIN_WIDTH:]


def _swa_perm():
    g, kh, dd = np.meshgrid(np.arange(SWA_GROUP), np.arange(SWA_KV_HEADS), np.arange(HEAD_DIM), indexing="ij")
    return ((kh * SWA_GROUP + g) * HEAD_DIM + dd).reshape(-1)


def _inproj_b(x, g_kv, g_mix, w_kv, w_in, kng, qng):
    n, d = x.shape
    tm = _row_tile(n)
    perm = _swa_perm()
    w_in_p = jnp.concatenate([w_in[:, :MAIN_WIDTH][:, perm], w_in[:, MAIN_WIDTH:]], axis=1).astype(BF16)
    qng_t = (jnp.tile(qng, SWA_HEADS) * HEAD_DIM ** -0.5).reshape(1, MAIN_WIDTH)
    row = lambda w: pl.BlockSpec((tm, w), lambda i: (i, 0))
    return pl.pallas_call(
        _inproj_b_kernel,
        grid=(n // tm,),
        in_specs=[row(d), _const((1, d)), _const((1, d)), _const((d, 2 * KV_WIDTH)), _const((d, d)),
                  _const((1, KV_WIDTH)), _const((1, MAIN_WIDTH)), _const((KV_WIDTH, KV_WIDTH)),
                  _const((MAIN_WIDTH, MAIN_WIDTH))],
        out_specs=[row(MAIN_WIDTH), row(MEM_WIDTH), row(KV_WIDTH), row(KV_WIDTH),
                   pl.BlockSpec((KV_WIDTH, tm), lambda i: (0, i))],
        out_shape=[jax.ShapeDtypeStruct((n, MAIN_WIDTH), F32), jax.ShapeDtypeStruct((n, MEM_WIDTH), F32),
                   jax.ShapeDtypeStruct((n, KV_WIDTH), F32), jax.ShapeDtypeStruct((n, KV_WIDTH), F32),
                   jax.ShapeDtypeStruct((KV_WIDTH, n), F32)],
        compiler_params=_params(("parallel",)),
        name="inproj_b",
    )(x, g_kv.reshape(1, d), g_mix.reshape(1, d), w_kv.astype(BF16), w_in_p,
      jnp.tile(kng, SWA_KV_HEADS).reshape(1, KV_WIDTH), qng_t,
      _block_diag_mean(KV_WIDTH), _block_diag_mean(MAIN_WIDTH))


def _softmax_with_sink(s, sink):
    m = jnp.maximum(jnp.max(s, axis=-1, keepdims=True), sink)
    e = jnp.exp(s - m)
    r = 1.0 / (jnp.sum(e, axis=-1, keepdims=True) + jnp.exp(sink - m))
    return (e * r).astype(BF16)


def _swa_bias(tq):
    slopes = 2.0 ** (-8.0 * np.arange(1, SWA_HEADS + 1, dtype=np.float64) / SWA_HEADS)
    dist = np.arange(tq)[:, None] + WINDOW - np.arange(WINDOW + tq)[None, :]
    valid = (dist >= 0) & (dist <= WINDOW)
    return np.stack([np.where(valid, -s * dist, NEG_BIG) for s in slopes]).astype(np.float32)


def _swa_prompt_kernel(sink_ref, q_ref, ktp_ref, kto_ref, vp_ref, vo_ref, bias_ref, hm_ref, o_ref, *, nb):
    w = WINDOW
    key = lax.broadcasted_iota(jnp.int32, (w, 2 * w), 1)
    has_prev = (pl.program_id(0) > 0) | (key >= w)
    heads = [(g, kh) for g in range(SWA_GROUP) for kh in range(SWA_KV_HEADS)]
    kts, vvs = [], []
    for b in range(nb):
        kt_prev = ktp_ref[...] if b == 0 else kto_ref[:, (b - 1) * w:b * w]
        v_prev = vp_ref[...] if b == 0 else vo_ref[(b - 1) * w:b * w, :]
        kts.append(jnp.concatenate([kt_prev, kto_ref[:, b * w:(b + 1) * w]], axis=1).astype(BF16))
        vvs.append(jnp.concatenate([v_prev, vo_ref[b * w:(b + 1) * w, :]], axis=0).astype(BF16))
    scores = [[jnp.dot((q_ref[b * w:(b + 1) * w, g * KV_WIDTH:(g + 1) * KV_WIDTH] * hm_ref[kh]).astype(BF16),
                       kts[b], preferred_element_type=F32) for g, kh in heads] for b in range(nb)]
    for b in range(nb):
        probs = []
        for (g, kh), s in zip(heads, scores[b]):
            h = kh * SWA_GROUP + g
            s = s + bias_ref[h]
            if b == 0:
                s = jnp.where(has_prev, s, NEG_BIG)
            probs.append(_softmax_with_sink(s, sink_ref[h]))
        outs = [jnp.dot(p, vvs[b], preferred_element_type=F32) for p in probs]
        for g in range(SWA_GROUP):
            acc = None
            for (cg, kh), o in zip(heads, outs):
                if cg == g:
                    t = o * hm_ref[kh]
                    acc = t if acc is None else acc + t
            o_ref[b * w:(b + 1) * w, g * KV_WIDTH:(g + 1) * KV_WIDTH] = acc


def _swa_prompt(q, kt, v, sinks, *, n_rows, nb):
    w = WINDOW
    step = nb * w
    assert n_rows % step == 0
    prev = lambda j, sink: jnp.maximum(j * nb - 1, 0)
    return pl.pallas_call(
        functools.partial(_swa_prompt_kernel, nb=nb),
        grid_spec=pltpu.PrefetchScalarGridSpec(
            num_scalar_prefetch=1,
            grid=(n_rows // step,),
            in_specs=[pl.BlockSpec((step, MAIN_WIDTH), lambda j, sink: (j, 0)),
                      pl.BlockSpec((KV_WIDTH, w), lambda j, sink: (0, prev(j, sink))),
                      pl.BlockSpec((KV_WIDTH, step), lambda j, sink: (0, j)),
                      pl.BlockSpec((w, KV_WIDTH), lambda j, sink: (prev(j, sink), 0)),
                      pl.BlockSpec((step, KV_WIDTH), lambda j, sink: (j, 0)),
                      pl.BlockSpec((SWA_HEADS, w, 2 * w), lambda j, sink: (0, 0, 0)),
                      pl.BlockSpec((SWA_KV_HEADS, 1, KV_WIDTH), lambda j, sink: (0, 0, 0))],
            out_specs=pl.BlockSpec((step, MAIN_WIDTH), lambda j, sink: (j, 0)),
        ),
        out_shape=jax.ShapeDtypeStruct((n_rows, MAIN_WIDTH), F32),
        compiler_params=_params(("arbitrary",)),
        name="swa_prompt",
    )(sinks.astype(F32), q, kt, kt, v, v, jnp.asarray(_swa_bias(w)), _head_masks(SWA_KV_HEADS))


def _swa_sample_kernel(q_ref, kp_ref, ko_ref, vp_ref, vo_ref, bias_ref, sink_ref, hm_ref, o_ref, *, nb, tq):
    w = WINDOW
    heads = [(kh, g) for kh in range(SWA_KV_HEADS) for g in range(SWA_GROUP)]
    kks, vvs, scores = [], [], []
    for i in range(nb):
        win = slice(i * KV_WIDTH, (i + 1) * KV_WIDTH)
        kks.append(jnp.concatenate([kp_ref[win, :].T, ko_ref[i * tq:(i + 1) * tq, :]], axis=0))
        vvs.append(jnp.concatenate([vp_ref[win, :].T, vo_ref[i * tq:(i + 1) * tq, :]], axis=0))
        qs = jnp.concatenate([q_ref[i * tq:(i + 1) * tq, g * KV_WIDTH:(g + 1) * KV_WIDTH] * hm_ref[kh]
                              for kh, g in heads], axis=0)
        scores.append(_bdot_nt(qs, kks[i]))
    probs = [_softmax_with_sink(s + bias_ref[...], sink_ref[...]) for s in scores]
    outs = [_bdot(p, vv) for p, vv in zip(probs, vvs)]
    for i in range(nb):
        for g in range(SWA_GROUP):
            acc = None
            for r, (kh, hg) in enumerate(heads):
                if hg == g:
                    t = outs[i][r * tq:(r + 1) * tq] * hm_ref[kh]
                    acc = t if acc is None else acc + t
            o_ref[i * tq:(i + 1) * tq, g * KV_WIDTH:(g + 1) * KV_WIDTH] = acc


def _swa_sample(q, k_win, v_win, k, v, sinks, *, row_off, batch, tq, nb):
    w = WINDOW
    assert batch % nb == 0 and row_off % (nb * tq) == 0
    off = row_off // (nb * tq)
    bias = jnp.asarray(_swa_bias(tq).reshape(SWA_HEADS * tq, w + tq))
    sink_col = jnp.repeat(sinks.astype(F32), tq).reshape(SWA_HEADS * tq, 1)
    own = lambda width: pl.BlockSpec((nb * tq, width), lambda b: (off + b, 0))
    win = pl.BlockSpec((nb * KV_WIDTH, w), lambda b: (b, 0))
    return pl.pallas_call(
        functools.partial(_swa_sample_kernel, nb=nb, tq=tq),
        grid=(batch // nb,),
        in_specs=[own(MAIN_WIDTH), win, own(KV_WIDTH), win, own(KV_WIDTH), _const(bias.shape),
                  _const(sink_col.shape), _const((SWA_KV_HEADS, 1, KV_WIDTH))],
        out_specs=pl.BlockSpec((nb * tq, MAIN_WIDTH), lambda b: (b, 0)),
        out_shape=jax.ShapeDtypeStruct((batch * tq, MAIN_WIDTH), F32),
        compiler_params=_params(("arbitrary",)),
        name="swa_sample",
    )(q, k_win, k, v_win, v, bias, sink_col, _head_masks(SWA_KV_HEADS))


def kernel(x_prompt, x_sample, state_gla, cache_win_k, cache_win_v, cache_mem_k, cache_mem_v, mem_prompt, norm_mix_g, norm_ffn_g, norm_mem_g, w_mem_kv, mem_qn_g, mem_kn_g, w_out, w_in_a, w_gate_lr, b_gate_lr, gla_norm_g, w_in_b, swa_qn_g, swa_sinks, norm_kv_g, w_kv, swa_kn_g, w_router_group, b_router_group, w_router_expert, b_router_expert, w_exp_gate, w_exp_up, w_exp_down):
    bp, tp, d = x_prompt.shape
    bs, ts, _ = x_sample.shape
    assert bp == 1 and tp % WINDOW == 0 and ts * (GLA_CHUNK // ts) == GLA_CHUNK
    n_p, n_s = bp * tp, bs * ts
    w_buf = cache_win_k.shape[1]
    assert w_buf == WINDOW
    x_p, x_s = x_prompt.reshape(n_p, d), x_sample.reshape(n_s, d)

    mem_k_p, mem_v_p = _mem_kv(mem_prompt, norm_mem_g, w_mem_kv, mem_kn_g)
    feature_major = lambda c: jnp.moveaxis(c, -3, -1).reshape(*c.shape[:-3], c.shape[-2] * c.shape[-1], c.shape[-3])
    cmk, cmv = feature_major(cache_mem_k), feature_major(cache_mem_v)

    def mem_attend(mq, l):
        tm_p = _row_tile(tp, 512)
        mo_p = _mem_attn(mq, mem_k_p, mem_v_p, mem_qn_g[l], row_off=0, seq=tp, tm=tm_p, bb=1, layer=l)
        mo_s = _mem_attn(mq, cmk, cmv, mem_qn_g[l], row_off=n_p, seq=ts, tm=ts, bb=8, layer=l)
        return mo_p, mo_s

    def moe(h, l, split_rows=None):
        return _moe(h, norm_ffn_g[l], w_router_group[l], b_router_group[l], w_router_expert[l],
                    b_router_expert[l], w_exp_gate, w_exp_up, w_exp_down, l, split_rows)

    q, k, la, v, og, mq = _inproj_a(x_p, x_s, norm_mix_g[0], w_in_a[0], w_gate_lr[0], b_gate_lr[0])
    zero_state = jnp.zeros((bp, GLA_HEADS, GLA_DK, GLA_DV), F32)
    n_sub = max(1, min(4, tp // GLA_CHUNK))
    main_p, gla_p = _gla(q, k, la, v, og, zero_state, gla_norm_g[0], row_off=0, seq=tp, n_seg=1, n_sub=n_sub)
    main_s, gla_s = _gla(q, k, la, v, og, state_gla[0], gla_norm_g[0], row_off=n_p, seq=ts,
                         n_seg=GLA_CHUNK // ts, n_sub=1)
    mo_p, mo_s = mem_attend(mq, 0)
    w_o = w_out[0]
    h = _outproj((x_p, x_s), main_p, mo_p, main_s, mo_s, w_o[:MAIN_WIDTH].reshape(GLA_HEADS, GLA_DV, d),
                 w_o[MAIN_WIDTH:])
    h = moe(h, 0)

    q, mq, k_sh, v_sh, kt_sh = _inproj_b(h, norm_kv_g, norm_mix_g[1], w_kv, w_in_b[0], swa_kn_g, swa_qn_g[0])
    ck = feature_major(cache_win_k).reshape(bs * KV_WIDTH, w_buf)
    cv = feature_major(cache_win_v).reshape(bs * KV_WIDTH, w_buf)
    main_p = _swa_prompt(q, kt_sh, v_sh, swa_sinks[0], n_rows=n_p, nb=2)
    main_s = _swa_sample(q, ck, cv, k_sh, v_sh, swa_sinks[0], row_off=n_p, batch=bs, tq=ts, nb=8)
    mo_p, mo_s = mem_attend(mq, 1)
    w_o = w_out[1]
    h = _outproj(h, main_p, mo_p, main_s, mo_s, w_o[:MAIN_WIDTH][_swa_perm()], w_o[MAIN_WIDTH:])
    y_p, y_s = moe(h, 1, split_rows=n_p)

    y_prompt = y_p.reshape(bp, tp, d)
    y_sample = y_s.reshape(bs, ts, d)
    k_new = k_sh[n_p:].reshape(bs, ts, SWA_KV_HEADS, HEAD_DIM)
    v_new = v_sh[n_p:].reshape(bs, ts, SWA_KV_HEADS, HEAD_DIM)
    win_k_s = jnp.concatenate([cache_win_k, k_new], axis=1)[:, -w_buf:]
    win_v_s = jnp.concatenate([cache_win_v, v_new], axis=1)[:, -w_buf:]
    win_k_p = k_sh[n_p - WINDOW:n_p].reshape(bp, WINDOW, SWA_KV_HEADS, HEAD_DIM)
    win_v_p = v_sh[n_p - WINDOW:n_p].reshape(bp, WINDOW, SWA_KV_HEADS, HEAD_DIM)
    token_major = lambda c: jnp.moveaxis(c.reshape(*c.shape[:-2], MEM_HEADS, HEAD_DIM, c.shape[-1]), -1, -3)
    return (y_prompt, y_sample, gla_p[None], gla_s[None], win_k_p, win_v_p, win_k_s, win_v_s,
            token_major(mem_k_p), token_major(mem_v_p))
```

```python
import functools

import numpy as np
import jax
import jax.numpy as jnp
from jax import lax
from jax.experimental import pallas as pl
from jax.experimental.pallas import tpu as pltpu

F32 = jnp.float32
BF16 = jnp.bfloat16

D_MODEL = 1024
MEM_LEN = 256
MEM_HEADS = 4
HEAD_DIM = 64
MEM_WIDTH = MEM_HEADS * HEAD_DIM
MAIN_WIDTH = D_MODEL - MEM_WIDTH
GLA_HEADS = 4
GLA_DV = MAIN_WIDTH // GLA_HEADS
GLA_DK = GLA_DV // 2
GLA_DK_PAD = 128
GLA_KEY_WIDTH = GLA_HEADS * GLA_DK
GLA_KEY_PAD = GLA_HEADS * GLA_DK_PAD
GLA_GATE_RANK = 16
GLA_TAU = 16.0
GLA_CHUNK = 64
SWA_HEADS = MAIN_WIDTH // HEAD_DIM
SWA_KV_HEADS = 4
SWA_GROUP = SWA_HEADS // SWA_KV_HEADS
KV_WIDTH = SWA_KV_HEADS * HEAD_DIM
WINDOW = 128
N_GROUPS = 4
EXPERTS_PER_GROUP = 8
N_EXPERTS = N_GROUPS * EXPERTS_PER_GROUP
D_EXPERT = 512
EPS = 1e-6
LANES = 128
NEG_BIG = -1e30
VMEM_LIMIT = 56 * 1024 * 1024
MOE_TILE = 512
ROUTER_LANE0 = N_GROUPS
ROUTER_META_ROWS = 8
SLABS = D_MODEL // LANES
PACKED_SLABS = SLABS // 2


def _bdot(a, b):
    return jnp.dot(a.astype(BF16), b.astype(BF16), preferred_element_type=F32)


def _bdot_nt(a, b):
    return lax.dot_general(a.astype(BF16), b.astype(BF16), (((1,), (1,)), ((), ())),
                           preferred_element_type=F32)


def _bdot_tn(a, b):
    return lax.dot_general(a.astype(BF16), b.astype(BF16), (((0,), (0,)), ((), ())),
                           preferred_element_type=F32)


def _split(x, n):
    parts = []
    for _ in range(n - 1):
        p = x.astype(BF16)
        parts.append(p)
        x = x - p.astype(F32)
    parts.append(x.astype(BF16))
    return parts


def _exact_left_dot(m, x, n=3):
    out = None
    for p in _split(x, n):
        t = jnp.dot(m, p, preferred_element_type=F32)
        out = t if out is None else out + t
    return out


def _seg_mean(x2, bd):
    out = None
    for p in _split(x2, 2):
        t = jnp.dot(p, bd, preferred_element_type=F32)
        out = t if out is None else out + t
    return out


def _rms_scale(x):
    return lax.rsqrt(jnp.mean(x * x, axis=-1, keepdims=True) + EPS)


def _row_tile(n, cap=512):
    t = cap
    while t > 8 and n % t:
        t //= 2
    assert n % t == 0, n
    return t


def _params(sem):
    return pltpu.CompilerParams(dimension_semantics=sem, vmem_limit_bytes=VMEM_LIMIT)


def _const(shape):
    nd = len(shape)
    return pl.BlockSpec(shape, lambda *_: (0,) * nd)


def _group_specs(tm, width, prompt_tiles, lead=None):
    p_idx = lambda i, *_: jnp.minimum(i, prompt_tiles - 1)
    s_idx = lambda i, *_: jnp.maximum(i - prompt_tiles, 0)
    if lead is None:
        return [pl.BlockSpec((tm, width), lambda i, *_, f=f: (f(i), 0)) for f in (p_idx, s_idx)]
    return [pl.BlockSpec((lead, tm, width), lambda i, *_, f=f: (0, f(i), 0)) for f in (p_idx, s_idx)]


def _block_diag_mean(width):
    i = np.arange(width)
    return jnp.asarray((i[:, None] // HEAD_DIM == i[None, :] // HEAD_DIM) / HEAD_DIM, BF16)


def _head_masks(n_heads):
    i = np.arange(n_heads * HEAD_DIM)
    return jnp.asarray((i[None, :] // HEAD_DIM == np.arange(n_heads)[:, None]), F32)[:, None, :]


def _mem_kv_kernel(mem_ref, g_ref, w_ref, kng_ref, bd_ref, k_ref, v_ref):
    x = mem_ref[0]
    hn = x * _rms_scale(x) * g_ref[0]
    kv = _bdot(hn, w_ref[0])
    k = kv[:, :MEM_WIDTH]
    k = k * lax.rsqrt(_seg_mean(k * k, bd_ref[...]) + EPS) * kng_ref[0]
    k_ref[0, 0] = k.T
    v_ref[0, 0] = kv[:, MEM_WIDTH:].T


def _mem_kv(mem, g, w, kng):
    depth, (b, m, d) = w.shape[0], mem.shape
    out = jax.ShapeDtypeStruct((depth, b, m, MEM_WIDTH), F32)
    blk = pl.BlockSpec((1, 1, m, MEM_WIDTH), lambda l, i: (l, i, 0, 0))
    return pl.pallas_call(
        _mem_kv_kernel,
        grid=(depth, b),
        in_specs=[pl.BlockSpec((1, m, d), lambda l, i: (i, 0, 0)),
                  pl.BlockSpec((1, 1, d), lambda l, i: (l, 0, 0)),
                  pl.BlockSpec((1, d, 2 * MEM_WIDTH), lambda l, i: (l, 0, 0)),
                  pl.BlockSpec((1, 1, MEM_WIDTH), lambda l, i: (l, 0, 0)),
                  _const((MEM_WIDTH, MEM_WIDTH))],
        out_specs=[blk, blk],
        out_shape=[out, out],
        compiler_params=_params(("arbitrary", "arbitrary")),
        name="mem_kv",
    )(mem, g.reshape(depth, 1, d), w.astype(BF16),
      jnp.tile(kng, (1, MEM_HEADS)).reshape(depth, 1, MEM_WIDTH), _block_diag_mean(MEM_WIDTH))


def _inproj_a_kernel(xp_ref, xs_ref, g_ref, wq_ref, wk_ref, wv_ref, wog_ref, wlr_ref, wmq_ref, wgl_ref, bgl_ref,
                     q_ref, k_ref, la_ref, v_ref, og_ref, mq_ref, *, prompt_tiles):
    x = jnp.where(pl.program_id(0) < prompt_tiles, xp_ref[...], xs_ref[...])
    hn = (x * _rms_scale(x) * g_ref[...]).astype(BF16)
    q_ref[...] = jnp.dot(hn, wq_ref[...], preferred_element_type=F32) * (GLA_DK ** -0.5)
    k_ref[...] = jnp.dot(hn, wk_ref[...], preferred_element_type=F32)
    for h in range(GLA_HEADS):
        v_ref[h] = jnp.dot(hn, wv_ref[h], preferred_element_type=F32).astype(BF16)
        og_ref[h] = jnp.dot(hn, wog_ref[h], preferred_element_type=F32)
    lr = jnp.dot(hn, wlr_ref[...], preferred_element_type=F32)
    z = _bdot(lr, wgl_ref[...]) + bgl_ref[...]
    la_ref[...] = (jnp.minimum(z, 0.0) - jnp.log(1.0 + jnp.exp(-jnp.abs(z)))) * (1.0 / GLA_TAU)
    mq_ref[...] = jnp.dot(hn, wmq_ref[...], preferred_element_type=F32)


def _pad_heads(w, width, pad):
    lead = w.shape[:-1]
    w = w.reshape(*lead, GLA_HEADS, width)
    w = jnp.pad(w, [(0, 0)] * len(lead) + [(0, 0), (0, pad - width)])
    return w.reshape(*lead, GLA_HEADS * pad)


def _inproj_a(x_p, x_s, g, w_in, w_lr, b_lr):
    (n_p, d), n_s = x_p.shape, x_s.shape[0]
    n = n_p + n_s
    tm = _row_tile(n_s)
    assert n_p % tm == 0
    pt = n_p // tm
    c0, c1, c2, c3, c4 = (GLA_KEY_WIDTH, 2 * GLA_KEY_WIDTH, 2 * GLA_KEY_WIDTH + MAIN_WIDTH,
                          2 * GLA_KEY_WIDTH + 2 * MAIN_WIDTH,
                          2 * GLA_KEY_WIDTH + 2 * MAIN_WIDTH + GLA_GATE_RANK)
    wb = w_in.astype(BF16)
    wq = _pad_heads(wb[:, :c0], GLA_DK, GLA_DK_PAD)
    wk = _pad_heads(wb[:, c0:c1], GLA_DK, GLA_DK_PAD)
    wv = wb[:, c1:c2].reshape(d, GLA_HEADS, GLA_DV).transpose(1, 0, 2)
    wog = wb[:, c2:c3].reshape(d, GLA_HEADS, GLA_DV).transpose(1, 0, 2)
    wlr = jnp.pad(wb[:, c3:c4], ((0, 0), (0, LANES - GLA_GATE_RANK)))
    wmq = wb[:, c4:]
    wgl = jnp.pad(_pad_heads(w_lr.astype(BF16), GLA_DK, GLA_DK_PAD), ((0, LANES - GLA_GATE_RANK), (0, 0)))
    bgl = _pad_heads(b_lr.reshape(1, -1), GLA_DK, GLA_DK_PAD)
    row = lambda w: pl.BlockSpec((tm, w), lambda i: (i, 0))
    hrow = pl.BlockSpec((GLA_HEADS, tm, GLA_DV), lambda i: (0, i, 0))
    key = jax.ShapeDtypeStruct((n, GLA_KEY_PAD), F32)
    val = jax.ShapeDtypeStruct((GLA_HEADS, n, GLA_DV), F32)
    return pl.pallas_call(
        functools.partial(_inproj_a_kernel, prompt_tiles=pt),
        grid=(n // tm,),
        in_specs=_group_specs(tm, d, pt) + [
            _const((1, d)), _const(wq.shape), _const(wk.shape), _const(wv.shape),
            _const(wog.shape), _const(wlr.shape), _const(wmq.shape), _const(wgl.shape),
            _const(bgl.shape)],
        out_specs=[row(GLA_KEY_PAD), row(GLA_KEY_PAD), row(GLA_KEY_PAD), hrow, hrow, row(MEM_WIDTH)],
        out_shape=[key, key, key, jax.ShapeDtypeStruct(val.shape, BF16), val,
                   jax.ShapeDtypeStruct((n, MEM_WIDTH), F32)],
        compiler_params=_params(("parallel",)),
        name="inproj_a",
    )(x_p, x_s, g.reshape(1, d), wq, wk, wv, wog, wlr, wmq, wgl, bgl)


def _gla_kernel(q_ref, k_ref, la_ref, v_ref, og_ref, s0_ref, gn_ref, mcum_ref, mall_ref, sel_ref,
                o_ref, sout_ref, s_ref, *, chunk, n_sub, n_seg):
    j = pl.program_id(1)
    seg = chunk // n_seg

    @pl.when(j == 0)
    def _():
        s_ref[...] = jnp.zeros_like(s_ref)
        s_ref[:, :, :GLA_DK, :] = s0_ref[...]

    mcum = mcum_ref[...]
    causal = mcum.astype(F32) > 0.0
    row = lax.broadcasted_iota(jnp.int32, (chunk, GLA_DK_PAD), 0)
    gn = gn_ref[...]
    hcols = [slice(h * GLA_DK_PAD, (h + 1) * GLA_DK_PAD) for h in range(GLA_HEADS)]
    crows = [slice(c * chunk, (c + 1) * chunk) for c in range(n_sub)]
    qts, kts, kds, e_ends = [], [], [], []
    for rows in crows:
        la = la_ref[rows, :]
        b = _exact_left_dot(mcum, la)
        b_end = _exact_left_dot(mall_ref[...], la)
        e_ends.append(jnp.exp(_exact_left_dot(sel_ref[...], la)).T)
        k = k_ref[rows, :]
        qts.append(q_ref[rows, :] * jnp.exp(b))
        kts.append((k * jnp.exp(-b)).astype(BF16))
        kds.append(k * jnp.exp(b_end - b))
    vbs = [[v_ref[h, rows, :].astype(BF16) for h in range(GLA_HEADS)] for rows in crows]
    scores = [[_bdot_nt(qts[c][:, cols], kts[c][:, cols]) for cols in hcols] for c in range(n_sub)]
    kvs = []
    for c in range(n_sub):
        per_head = []
        for h, cols in enumerate(hcols):
            per_seg = []
            for s in range(n_seg):
                kd = kds[c][:, cols]
                if n_seg > 1:
                    kd = jnp.where((row >= s * seg) & (row < (s + 1) * seg), kd, 0.0)
                per_seg.append(_bdot_tn(kd, vbs[c][h]))
            per_head.append(per_seg)
        kvs.append(per_head)
    state = [[s_ref[s, h] for s in range(n_seg)] for h in range(GLA_HEADS)]
    inters = []
    for c in range(n_sub):
        per_head = []
        for h, cols in enumerate(hcols):
            parts = []
            for s in range(n_seg):
                parts.append(_bdot(qts[c][s * seg:(s + 1) * seg, cols], state[h][s]))
                state[h][s] = e_ends[c][cols, s:s + 1] * state[h][s] + kvs[c][h][s]
            per_head.append(parts[0] if n_seg == 1 else jnp.concatenate(parts, axis=0))
        inters.append(per_head)
    for h in range(GLA_HEADS):
        for s in range(n_seg):
            s_ref[s, h] = state[h][s]
    for c, rows in enumerate(crows):
        for h in range(GLA_HEADS):
            a = jnp.where(causal, scores[c][h], 0.0)
            o = _bdot(a, vbs[c][h]) + inters[c][h]
            on = o * lax.rsqrt(jnp.mean(o * o, axis=-1, keepdims=True) + EPS) * gn
            og = og_ref[h, rows, :]
            o_ref[h, rows, :] = (on * (og * jax.nn.sigmoid(og))).astype(BF16)

    @pl.when(j == pl.num_programs(1) - 1)
    def _():
        sout_ref[...] = s_ref[:, :, :GLA_DK, :]


def _gla(q, k, la, v, og, s0, gnorm, *, row_off, seq, n_seg, n_sub):
    batch = s0.shape[0]
    chunk = GLA_CHUNK
    assert chunk % n_seg == 0 and batch % n_seg == 0
    seg = chunk // n_seg
    step_rows = n_sub * chunk
    if n_seg > 1:
        assert seq == seg and n_sub == 1
        t_steps = 1
    else:
        assert seq % step_rows == 0
        t_steps = seq // step_rows
    assert row_off % step_rows == 0
    off = row_off // step_rows
    i = np.arange(chunk)
    same = (i[:, None] // seg) == (i[None, :] // seg)
    mcum = jnp.asarray(same & (i[None, :] <= i[:, None]), BF16)
    mall = jnp.asarray(same, BF16)
    sel = jnp.asarray((i[None, :] // seg) == np.arange(LANES)[:, None], BF16)
    ridx = lambda g, j: (off + g * t_steps + j, 0)
    hidx = lambda g, j: (0, off + g * t_steps + j, 0)
    key_spec = pl.BlockSpec((step_rows, GLA_KEY_PAD), ridx)
    val_spec = pl.BlockSpec((GLA_HEADS, step_rows, GLA_DV), hidx)
    st_spec = pl.BlockSpec((n_seg, GLA_HEADS, GLA_DK, GLA_DV), lambda g, j: (g, 0, 0, 0))
    in_specs = [key_spec, key_spec, key_spec, val_spec, val_spec, st_spec, _const((1, GLA_DV)),
                _const((chunk, chunk)), _const((chunk, chunk)), _const((LANES, chunk))]
    args = [q, k, la, v, og, s0, gnorm.reshape(1, GLA_DV), mcum, mall, sel]
    out_spec = pl.BlockSpec((GLA_HEADS, step_rows, GLA_DV), lambda g, j: (0, g * t_steps + j, 0))
    return pl.pallas_call(
        functools.partial(_gla_kernel, chunk=chunk, n_sub=n_sub, n_seg=n_seg),
        grid=(batch // n_seg, t_steps),
        in_specs=in_specs,
        out_specs=[out_spec, st_spec],
        out_shape=[jax.ShapeDtypeStruct((GLA_HEADS, batch * seq, GLA_DV), BF16),
                   jax.ShapeDtypeStruct(s0.shape, F32)],
        scratch_shapes=[pltpu.VMEM((n_seg, GLA_HEADS, GLA_DK_PAD, GLA_DV), F32)],
        compiler_params=_params(("arbitrary", "arbitrary")),
        name="gla",
    )(*args)


def _mem_attn_kernel(q_ref, k_ref, v_ref, g_ref, bd_ref, hm_ref, o_ref, *, tm, bb):
    g = g_ref[...]
    sub = min(tm, 128)
    units = [(i, i * tm + r) for i in range(bb) for r in range(0, tm, sub)]
    scores = []
    for i, r in units:
        q = q_ref[r:r + sub, :]
        qn = q * lax.rsqrt(_seg_mean(q * q, bd_ref[...]) + EPS) * g
        qs = jnp.concatenate([(qn * hm_ref[h]).astype(BF16) for h in range(MEM_HEADS)], axis=0)
        scores.append(_bdot(qs, k_ref[i]))
    probs = []
    for s in scores:
        e = jnp.exp(s - jnp.max(s, axis=-1, keepdims=True))
        probs.append(e * (1.0 / jnp.sum(e, axis=-1, keepdims=True)))
    outs = [_bdot_nt(p, v_ref[i]) for (i, _), p in zip(units, probs)]
    rows = []
    for o in outs:
        acc = o[:sub] * hm_ref[0]
        for h in range(1, MEM_HEADS):
            acc = acc + o[h * sub:(h + 1) * sub] * hm_ref[h]
        rows.append(acc)
    o_ref[...] = jnp.concatenate(rows, axis=0).astype(BF16)


def _mem_attn(mq, mk, mv, qng, *, row_off, seq, tm, bb, layer):
    depth, batch, m, _ = mk.shape
    mk = mk.reshape(depth * batch, m, MEM_WIDTH)
    mv = mv.reshape(depth * batch, m, MEM_WIDTH)
    kv_off = layer * batch // bb
    assert seq % tm == 0 and batch % bb == 0 and (bb == 1 or seq == tm)
    t_steps = seq // tm
    step_rows = bb * tm
    assert row_off % step_rows == 0
    off = row_off // step_rows
    row_spec = pl.BlockSpec((step_rows, MEM_WIDTH), lambda g, j: (off + g * t_steps + j, 0))
    kv_spec = pl.BlockSpec((bb, m, MEM_WIDTH), lambda g, j: (kv_off + g, 0, 0))
    in_specs = [row_spec, kv_spec, kv_spec, _const((1, MEM_WIDTH)), _const((MEM_WIDTH, MEM_WIDTH)),
                _const((MEM_HEADS, 1, MEM_WIDTH))]
    args = [mq, mk, mv, (jnp.tile(qng, MEM_HEADS) * HEAD_DIM ** -0.5).reshape(1, MEM_WIDTH),
            _block_diag_mean(MEM_WIDTH), _head_masks(MEM_HEADS)]
    return pl.pallas_call(
        functools.partial(_mem_attn_kernel, tm=tm, bb=bb),
        grid=(batch // bb, t_steps),
        in_specs=in_specs,
        out_specs=pl.BlockSpec((step_rows, MEM_WIDTH), lambda g, j: (g * t_steps + j, 0)),
        out_shape=jax.ShapeDtypeStruct((batch * seq, MEM_WIDTH), BF16),
        compiler_params=_params(("parallel", "parallel")),
        name="mem_attn",
    )(*args)


def _outproj_kernel(*refs, heads, prompt_tiles, split_residual):
    if split_residual:
        hp_ref, hs_ref = refs[:2]
        refs = refs[2:]
    else:
        hp_ref = hs_ref = refs[0]
        refs = refs[1:]
    main_p_ref, main_s_ref, mo_p_ref, mo_s_ref, wmain_ref, wmo_ref, o_ref = refs

    def project(h_ref, main_ref, mo_ref):
        acc = h_ref[...] + _bdot(mo_ref[...], wmo_ref[...])
        if heads:
            for h in range(heads):
                acc = acc + _bdot(main_ref[h], wmain_ref[h])
        else:
            acc = acc + _bdot(main_ref[...], wmain_ref[...])
        o_ref[...] = acc

    @pl.when(pl.program_id(0) < prompt_tiles)
    def _():
        project(hp_ref, main_p_ref, mo_p_ref)

    @pl.when(pl.program_id(0) >= prompt_tiles)
    def _():
        project(hs_ref, main_s_ref, mo_s_ref)


def _outproj(h, main_p, mo_p, main_s, mo_s, w_main, w_mo):
    n_p, n_s = mo_p.shape[0], mo_s.shape[0]
    n, d = n_p + n_s, w_mo.shape[1]
    tm = _row_tile(n_s)
    assert n_p % tm == 0
    pt = n_p // tm
    heads = main_p.shape[0] if main_p.ndim == 3 else 0
    row = pl.BlockSpec((tm, d), lambda i: (i, 0))
    split = isinstance(h, tuple)
    h_specs, h_args = (_group_specs(tm, d, pt), list(h)) if split else ([row], [h])
    main_specs = _group_specs(tm, main_p.shape[-1], pt, lead=heads or None)
    return pl.pallas_call(
        functools.partial(_outproj_kernel, heads=heads, prompt_tiles=pt, split_residual=split),
        grid=(n // tm,),
        in_specs=h_specs + main_specs + _group_specs(tm, MEM_WIDTH, pt) + [_const(w_main.shape),
                                                                           _const(w_mo.shape)],
        out_specs=row,
        out_shape=jax.ShapeDtypeStruct((n, d), F32),
        compiler_params=_params(("parallel",)),
        name="outproj",
    )(*h_args, main_p, main_s, mo_p, mo_s, w_main.astype(BF16), w_mo.astype(BF16))


def _router_kernel(h_ref, g_ref, whi_ref, wlo_ref, b_ref, tril_ref, mi_ref, mf_ref, cnt_ref, carry_ref):
    i = pl.program_id(0)

    @pl.when(i == 0)
    def _():
        carry_ref[...] = jnp.zeros_like(carry_ref)

    x = h_ref[...]
    xn = x * _rms_scale(x) * g_ref[...]
    x_hi, x_lo = _split(xn, 2)
    logits = (jnp.dot(x_hi, whi_ref[...], preferred_element_type=F32)
              + jnp.dot(x_hi, wlo_ref[...], preferred_element_type=F32)
              + jnp.dot(x_lo, whi_ref[...], preferred_element_type=F32)) + b_ref[...]
    tm = x.shape[0]
    lane = lax.broadcasted_iota(jnp.int32, (tm, LANES), 1)
    far = jnp.int32(2 * LANES)

    def first_max(vals):
        m = jnp.max(vals, axis=-1, keepdims=True)
        return m, jnp.min(jnp.where(vals == m, lane, far), axis=-1, keepdims=True)

    gl = jnp.where(lane < N_GROUPS, logits, -jnp.inf)
    gmax, grp = first_max(gl)
    pg_sel = 1.0 / jnp.sum(jnp.exp(gl - gmax), axis=-1, keepdims=True)
    lo = ROUTER_LANE0 + grp * EXPERTS_PER_GROUP
    el = jnp.where((lane >= lo) & (lane < lo + EXPERTS_PER_GROUP), logits, -jnp.inf)
    m1, i1 = first_max(el)
    m2, i2 = first_max(jnp.where(lane == i1, -jnp.inf, el))
    e2 = jnp.exp(m2 - m1)
    g1 = pg_sel / (1.0 + e2)
    g2 = pg_sel * e2 / (1.0 + e2)

    oh1 = lane == i1
    oh2 = lane == i2
    picked = jnp.where(oh1 | oh2, 1.0, 0.0)
    before = jnp.dot(tril_ref[...], picked.astype(BF16), preferred_element_type=F32) + carry_ref[...]
    rank1 = jnp.sum(jnp.where(oh1, before, 0.0), axis=-1, keepdims=True)
    rank2 = jnp.sum(jnp.where(oh2, before, 0.0), axis=-1, keepdims=True)
    carry = carry_ref[...] + jnp.sum(picked, axis=0, keepdims=True)
    carry_ref[...] = carry
    cnt_ref[...] = carry

    zi = jnp.zeros((tm, LANES), jnp.int32)
    mi = jnp.where(lane == 0, i1 - ROUTER_LANE0, zi)
    mi = jnp.where(lane == 1, i2 - ROUTER_LANE0, mi)
    mi = jnp.where(lane == 2, rank1.astype(jnp.int32), mi)
    mi = jnp.where(lane == 3, rank2.astype(jnp.int32), mi)
    mi_ref[...] = mi.T[:ROUTER_META_ROWS]
    zf = jnp.zeros((tm, LANES), F32)
    mf_ref[...] = jnp.where(lane == 0, g1, jnp.where(lane == 1, g2, zf))


def _router(h, g, w_rg, b_rg, w_re, b_re):
    n, d = h.shape
    tm = _row_tile(n)
    n_real = N_GROUPS + N_EXPERTS
    w = jnp.pad(jnp.concatenate([w_rg, w_re], axis=1), ((0, 0), (0, LANES - n_real)))
    b = jnp.pad(jnp.concatenate([b_rg, b_re]), (0, LANES - n_real)).reshape(1, LANES)
    w_hi = w.astype(BF16)
    w_lo = (w - w_hi.astype(F32)).astype(BF16)
    i = np.arange(tm)
    tril = jnp.asarray(i[None, :] < i[:, None], BF16)
    row = lambda width: pl.BlockSpec((tm, width), lambda i: (i, 0))
    return pl.pallas_call(
        _router_kernel,
        grid=(n // tm,),
        in_specs=[row(d), _const((1, d)), _const((d, LANES)), _const((d, LANES)), _const((1, LANES)),
                  _const((tm, tm))],
        out_specs=[pl.BlockSpec((ROUTER_META_ROWS, tm), lambda i: (0, i)), row(LANES), _const((1, LANES))],
        out_shape=[jax.ShapeDtypeStruct((ROUTER_META_ROWS, n), jnp.int32),
                   jax.ShapeDtypeStruct((n, LANES), F32), jax.ShapeDtypeStruct((1, LANES), F32)],
        scratch_shapes=[pltpu.VMEM((1, LANES), F32)],
        compiler_params=_params(("arbitrary",)),
        name="moe_router",
    )(h, g.reshape(1, d), w_hi, w_lo, b, tril)


def _to_rows(ref, rows, lead=()):
    return jnp.concatenate([ref[lead + (pl.ds(s, rows, stride=SLABS), slice(None))] for s in range(SLABS)],
                           axis=1)


def _from_rows(ref, x, rows, lead=()):
    for s in range(SLABS):
        ref[lead + (pl.ds(s, rows, stride=SLABS), slice(None))] = x[:, s * LANES:(s + 1) * LANES]


def _pack_rows(ref, x, rows, lead=()):
    u32 = jnp.uint32
    for w in range(PACKED_SLABS):
        lo = x[:, (2 * w) * LANES:(2 * w + 1) * LANES].astype(BF16).astype(F32)
        hi = x[:, (2 * w + 1) * LANES:(2 * w + 2) * LANES].astype(BF16).astype(F32)
        word = (lax.bitcast_convert_type(lo, u32) >> 16) | (lax.bitcast_convert_type(hi, u32) & u32(0xFFFF0000))
        ref[lead + (pl.ds(w, rows, stride=PACKED_SLABS), slice(None))] = word


def _unpack_rows(ref, rows, lead=()):
    u32 = jnp.uint32
    slabs = []
    for w in range(PACKED_SLABS):
        word = ref[lead + (pl.ds(w, rows, stride=PACKED_SLABS), slice(None))]
        slabs.append(lax.bitcast_convert_type(word << 16, F32).astype(BF16))
        slabs.append(lax.bitcast_convert_type(word & u32(0xFFFF0000), F32).astype(BF16))
    return jnp.concatenate(slabs, axis=1)


def _dispatch_kernel(pos_ref, h_ref, g_ref, xs_hbm, buf, sem, *, tm, n, steps):
    i = pl.program_id(0)
    slot = lax.rem(i, 2)
    ps = PACKED_SLABS

    def wait_slot(sl):
        for _ in range(2):
            pltpu.make_async_copy(buf.at[sl], xs_hbm.at[pl.ds(0, tm * ps)], sem.at[sl]).wait()

    @pl.when(i >= 2)
    def _():
        wait_slot(slot)

    x = h_ref[...]
    _pack_rows(buf, x * _rms_scale(x) * g_ref[...], tm, (slot,))

    def issue(r, carry):
        src = buf.at[slot, pl.ds(pl.multiple_of(r * ps, ps), ps)]
        for kk in range(2):
            p = pos_ref[kk * n + i * tm + r]
            dst = xs_hbm.at[pl.ds(pl.multiple_of(p * ps, ps), ps)]
            pltpu.make_async_copy(src, dst, sem.at[slot]).start(priority=kk)
        return carry

    lax.fori_loop(0, tm, issue, 0)

    @pl.when(i == steps - 1)
    def _():
        wait_slot(slot)
        if steps > 1:
            wait_slot(1 - slot)


def _dispatch(h, g, pos_flat):
    n, d = h.shape
    tm = _row_tile(n, MOE_TILE)
    steps = n // tm
    return pl.pallas_call(
        functools.partial(_dispatch_kernel, tm=tm, n=n, steps=steps),
        grid_spec=pltpu.PrefetchScalarGridSpec(
            num_scalar_prefetch=1,
            grid=(steps,),
            in_specs=[pl.BlockSpec((tm, d), lambda i, pos: (i, 0)),
                      pl.BlockSpec((1, d), lambda i, pos: (0, 0))],
            out_specs=pl.BlockSpec(memory_space=pl.ANY),
            scratch_shapes=[pltpu.VMEM((2, tm * PACKED_SLABS, LANES), jnp.uint32),
                            pltpu.SemaphoreType.DMA((2,))],
        ),
        out_shape=jax.ShapeDtypeStruct((2 * n * PACKED_SLABS, LANES), jnp.uint32),
        compiler_params=_params(("arbitrary",)),
        name="moe_dispatch",
    )(pos_flat, h, g.reshape(1, d))


ITEM_FIRST, ITEM_LAST, ITEM_NEW_EXPERT = 1, 2, 4


def _expert_kernel(tile_ref, exp_ref, lo_ref, hi_ref, flag_ref, n_ref, xs_ref, wg_ref, wu_ref, wd_ref, ys_ref,
                   wgb, wub, wdb, acc, *, tm):
    w = pl.program_id(0)

    @pl.when(w < n_ref[0])
    def _():
        flags = flag_ref[w]

        @pl.when((flags & ITEM_NEW_EXPERT) != 0)
        def _():
            wgb[...] = wg_ref[0, 0].astype(BF16)
            wub[...] = wu_ref[0, 0].astype(BF16)
            wdb[...] = wd_ref[0, 0].astype(BF16)

        x = _unpack_rows(xs_ref, tm)
        hg = jnp.dot(x, wgb[...], preferred_element_type=F32)
        hu = jnp.dot(x, wub[...], preferred_element_type=F32)
        act = (hg * jax.nn.sigmoid(hg) * hu).astype(BF16)
        y = jnp.dot(act, wdb[...], preferred_element_type=F32)
        row = lax.broadcasted_iota(jnp.int32, (tm, 1), 0)
        y = jnp.where((row >= lo_ref[w]) & (row < hi_ref[w]), y, 0.0)
        first = (flags & ITEM_FIRST) != 0

        @pl.when(first)
        def _():
            acc[...] = y

        @pl.when(jnp.logical_not(first))
        def _():
            acc[...] += y

        @pl.when((flags & ITEM_LAST) != 0)
        def _():
            _from_rows(ys_ref, acc[...], tm)


def _experts(xs, items, w_g, w_u, w_d, layer):
    tile, expert, lo, hi, flags, n_items = items
    d = D_MODEL
    tm = MOE_TILE
    tok_spec = pl.BlockSpec((tm * SLABS, LANES), lambda w, tile, *_: (tile[w], 0))
    packed_spec = pl.BlockSpec((tm * PACKED_SLABS, LANES), lambda w, tile, *_: (tile[w], 0))
    wmap = lambda w, tile, expert, *_: (layer, expert[w], 0, 0)
    return pl.pallas_call(
        functools.partial(_expert_kernel, tm=tm),
        grid_spec=pltpu.PrefetchScalarGridSpec(
            num_scalar_prefetch=6,
            grid=(tile.shape[0],),
            in_specs=[packed_spec,
                      pl.BlockSpec((1, 1, d, D_EXPERT), wmap),
                      pl.BlockSpec((1, 1, d, D_EXPERT), wmap),
                      pl.BlockSpec((1, 1, D_EXPERT, d), wmap)],
            out_specs=tok_spec,
            scratch_shapes=[pltpu.VMEM((d, D_EXPERT), BF16), pltpu.VMEM((d, D_EXPERT), BF16),
                            pltpu.VMEM((D_EXPERT, d), BF16), pltpu.VMEM((tm, d), F32)],
        ),
        out_shape=jax.ShapeDtypeStruct((xs.shape[0] // PACKED_SLABS * SLABS, LANES), F32),
        compiler_params=_params(("arbitrary",)),
        name="moe_experts",
    )(tile, expert, lo, hi, flags, n_items, xs, w_g, w_u, w_d)


def _combine_kernel(pos_ref, h_ref, gate_ref, ys_hbm, *refs, tm, n, steps, prompt_tiles):
    out_refs, (buf, sem) = refs[:-2], refs[-2:]
    i = pl.program_id(0)
    slot = lax.rem(i, 2)

    def issue(step, sl):
        def body(r, carry):
            for kk in range(2):
                p = pos_ref[kk * n + step * tm + r]
                src = ys_hbm.at[pl.ds(pl.multiple_of(p * SLABS, SLABS), SLABS)]
                dst = buf.at[sl, kk, pl.ds(pl.multiple_of(r * SLABS, SLABS), SLABS)]
                pltpu.make_async_copy(src, dst, sem.at[sl]).start(priority=kk)
            return carry

        lax.fori_loop(0, tm, body, 0)

    @pl.when(i == 0)
    def _():
        issue(0, 0)

    @pl.when(i + 1 < steps)
    def _():
        issue(i + 1, 1 - slot)

    for kk in range(2):
        pltpu.make_async_copy(ys_hbm.at[pl.ds(0, tm * SLABS)], buf.at[slot, kk], sem.at[slot]).wait()
    g = gate_ref[...]
    out = (h_ref[...] + g[:, 0:1] * _to_rows(buf, tm, (slot, 0))
           + g[:, 1:2] * _to_rows(buf, tm, (slot, 1)))
    if len(out_refs) == 1:
        out_refs[0][...] = out
    else:
        @pl.when(i < prompt_tiles)
        def _():
            out_refs[0][...] = out

        @pl.when(i >= prompt_tiles)
        def _():
            out_refs[1][...] = out


def _combine(h, gates, ys, pos_flat, split_rows=None):
    n, d = h.shape
    tm = _row_tile(n if split_rows is None else n - split_rows, MOE_TILE)
    steps = n // tm
    row = pl.BlockSpec((tm, d), lambda i, pos: (i, 0))
    if split_rows is None:
        pt, out_specs, out_shape = 0, row, jax.ShapeDtypeStruct((n, d), F32)
    else:
        assert split_rows % tm == 0
        pt = split_rows // tm
        out_specs = _group_specs(tm, d, pt)
        out_shape = [jax.ShapeDtypeStruct((split_rows, d), F32), jax.ShapeDtypeStruct((n - split_rows, d), F32)]
    return pl.pallas_call(
        functools.partial(_combine_kernel, tm=tm, n=n, steps=steps, prompt_tiles=pt),
        grid_spec=pltpu.PrefetchScalarGridSpec(
            num_scalar_prefetch=1,
            grid=(steps,),
            in_specs=[row, pl.BlockSpec((tm, LANES), lambda i, pos: (i, 0)),
                      pl.BlockSpec(memory_space=pl.ANY)],
            out_specs=out_specs,
            scratch_shapes=[pltpu.VMEM((2, 2, tm * SLABS, LANES), F32), pltpu.SemaphoreType.DMA((2,))],
        ),
        out_shape=out_shape,
        compiler_params=_params(("arbitrary",)),
        name="moe_combine",
    )(pos_flat, h, gates, ys)


def _lookup(tables, idx):
    hit = idx[:, None] == jnp.arange(tables.shape[1], dtype=idx.dtype)[None, :]
    return jnp.sum(jnp.where(hit[None], tables[:, None, :], 0), axis=2)


def _slot_pos_kernel(starts_ref, meta_ref, pos_ref):
    expert = meta_ref[0:2, :]
    start = jnp.zeros(expert.shape, jnp.int32)
    for e in range(N_EXPERTS):
        start = jnp.where(expert == e, starts_ref[e], start)
    pos_ref[...] = start + meta_ref[2:4, :]


def _slot_positions(starts, meta):
    n = meta.shape[1]
    return pl.pallas_call(
        _slot_pos_kernel,
        grid_spec=pltpu.PrefetchScalarGridSpec(
            num_scalar_prefetch=1,
            grid=(1,),
            in_specs=[pl.BlockSpec((ROUTER_META_ROWS, n), lambda i, starts: (0, 0))],
            out_specs=pl.BlockSpec((2, n), lambda i, starts: (0, 0)),
        ),
        out_shape=jax.ShapeDtypeStruct((2, n), jnp.int32),
        compiler_params=_params(("arbitrary",)),
        name="moe_slot_pos",
    )(starts, meta)


def _work_items(counts, n_slots):
    tm = MOE_TILE
    assert n_slots % tm == 0, (n_slots, tm)
    ends = jnp.cumsum(counts)
    starts = ends - counts
    first_tile = starts // tm
    n_tiles_e = jnp.where(counts > 0, (ends - 1) // tm - first_tile + 1, 0)
    item_end = jnp.cumsum(n_tiles_e)
    item_start = item_end - n_tiles_e
    n_items = item_end[-1]
    max_items = n_slots // tm + N_EXPERTS - 1
    w = jnp.minimum(jnp.arange(max_items, dtype=jnp.int32), n_items - 1)
    expert = jnp.sum(w[:, None] >= item_end[None, :], axis=1).astype(jnp.int32)
    first_tile_w, item_start_w, start_w, end_w = _lookup(jnp.stack([first_tile, item_start, starts, ends]), expert)
    tile = first_tile_w + w - item_start_w
    lo = jnp.maximum(start_w - tile * tm, 0)
    hi = jnp.minimum(end_w - tile * tm, tm)
    prev_tile = jnp.concatenate([jnp.full((1,), -1, jnp.int32), tile[:-1]])
    next_tile = jnp.concatenate([tile[1:], jnp.full((1,), -1, jnp.int32)])
    prev_expert = jnp.concatenate([jnp.full((1,), -1, jnp.int32), expert[:-1]])
    idx = jnp.arange(max_items, dtype=jnp.int32)
    flags = (jnp.where(tile != prev_tile, ITEM_FIRST, 0)
             | jnp.where((tile != next_tile) | (idx == n_items - 1), ITEM_LAST, 0)
             | jnp.where(expert != prev_expert, ITEM_NEW_EXPERT, 0))
    as_i32 = lambda a: a.astype(jnp.int32)
    return (as_i32(tile), expert, as_i32(lo), as_i32(hi), as_i32(flags), as_i32(n_items).reshape(1))


def _moe(h, g, w_rg, b_rg, w_re, b_re, w_g, w_u, w_d, layer, split_rows=None):
    n, _ = h.shape
    meta, gates, cnt = _router(h, g, w_rg, b_rg, w_re, b_re)
    counts = cnt[0, ROUTER_LANE0:ROUTER_LANE0 + N_EXPERTS].astype(jnp.int32)
    starts = jnp.cumsum(counts) - counts
    pos = _slot_positions(starts, meta).reshape(-1)
    xs = _dispatch(h, g, pos)
    ys = _experts(xs, _work_items(counts, 2 * n), w_g, w_u, w_d, layer)
    return _combine(h, gates, ys, pos, split_rows)


def _inproj_b_kernel(x_ref, gkv_ref, gmix_ref, wkv_ref, win_ref, kng_ref, qng_ref, bdk_ref, bdq_ref,
                     q_ref, mq_ref, k_ref, v_ref, kt_ref):
    x = x_ref[...]
    xr = x * _rms_scale(x)
    kv = _bdot(xr * gkv_ref[...], wkv_ref[...])
    k = kv[:, :KV_WIDTH]
    k = k * lax.rsqrt(_seg_mean(k * k, bdk_ref[...]) + EPS) * kng_ref[...]
    k_ref[...] = k
    kt_ref[...] = k.T.astype(BF16)
    v_ref[...] = kv[:, KV_WIDTH:]
    proj = _bdot(xr * gmix_ref[...], win_ref[...])
    q = proj[:, :MAIN_WIDTH]
    q_ref[...] = (q * lax.rsqrt(_seg_mean(q * q, bdq_ref[...]) + EPS) * qng_ref[...]).astype(BF16)
    mq_ref[...] = proj[:, MAIN_WIDTH:]


def _swa_perm():
    g, kh, dd = np.meshgrid(np.arange(SWA_GROUP), np.arange(SWA_KV_HEADS), np.arange(HEAD_DIM), indexing="ij")
    return ((kh * SWA_GROUP + g) * HEAD_DIM + dd).reshape(-1)


def _inproj_b(x, g_kv, g_mix, w_kv, w_in, kng, qng):
    n, d = x.shape
    tm = _row_tile(n)
    perm = _swa_perm()
    w_in_p = jnp.concatenate([w_in[:, :MAIN_WIDTH][:, perm], w_in[:, MAIN_WIDTH:]], axis=1).astype(BF16)
    qng_t = (jnp.tile(qng, SWA_HEADS) * HEAD_DIM ** -0.5).reshape(1, MAIN_WIDTH)
    row = lambda w: pl.BlockSpec((tm, w), lambda i: (i, 0))
    return pl.pallas_call(
        _inproj_b_kernel,
        grid=(n // tm,),
        in_specs=[row(d), _const((1, d)), _const((1, d)), _const((d, 2 * KV_WIDTH)), _const((d, d)),
                  _const((1, KV_WIDTH)), _const((1, MAIN_WIDTH)), _const((KV_WIDTH, KV_WIDTH)),
                  _const((MAIN_WIDTH, MAIN_WIDTH))],
        out_specs=[row(MAIN_WIDTH), row(MEM_WIDTH), row(KV_WIDTH), row(KV_WIDTH),
                   pl.BlockSpec((KV_WIDTH, tm), lambda i: (0, i))],
        out_shape=[jax.ShapeDtypeStruct((n, MAIN_WIDTH), BF16), jax.ShapeDtypeStruct((n, MEM_WIDTH), F32),
                   jax.ShapeDtypeStruct((n, KV_WIDTH), F32), jax.ShapeDtypeStruct((n, KV_WIDTH), F32),
                   jax.ShapeDtypeStruct((KV_WIDTH, n), BF16)],
        compiler_params=_params(("parallel",)),
        name="inproj_b",
    )(x, g_kv.reshape(1, d), g_mix.reshape(1, d), w_kv.astype(BF16), w_in_p,
      jnp.tile(kng, SWA_KV_HEADS).reshape(1, KV_WIDTH), qng_t,
      _block_diag_mean(KV_WIDTH), _block_diag_mean(MAIN_WIDTH))


def _softmax_with_sink(s, sink):
    m = jnp.maximum(jnp.max(s, axis=-1, keepdims=True), sink)
    e = jnp.exp(s - m)
    r = 1.0 / (jnp.sum(e, axis=-1, keepdims=True) + jnp.exp(sink - m))
    return (e * r).astype(BF16)


def _swa_bias(tq):
    slopes = 2.0 ** (-8.0 * np.arange(1, SWA_HEADS + 1, dtype=np.float64) / SWA_HEADS)
    dist = np.arange(tq)[:, None] + WINDOW - np.arange(WINDOW + tq)[None, :]
    valid = (dist >= 0) & (dist <= WINDOW)
    return np.stack([np.where(valid, -s * dist, NEG_BIG) for s in slopes]).astype(np.float32)


def _swa_prompt_kernel(sink_ref, q_ref, ktp_ref, kto_ref, vp_ref, vo_ref, bias_ref, hm_ref, o_ref, *, nb):
    w = WINDOW
    key = lax.broadcasted_iota(jnp.int32, (w, 2 * w), 1)
    has_prev = (pl.program_id(0) > 0) | (key >= w)
    heads = [(g, kh) for g in range(SWA_GROUP) for kh in range(SWA_KV_HEADS)]
    kts, vvs = [], []
    for b in range(nb):
        kt_prev = ktp_ref[...] if b == 0 else kto_ref[:, (b - 1) * w:b * w]
        v_prev = vp_ref[...] if b == 0 else vo_ref[(b - 1) * w:b * w, :]
        kts.append(jnp.concatenate([kt_prev, kto_ref[:, b * w:(b + 1) * w]], axis=1).astype(BF16))
        vvs.append(jnp.concatenate([v_prev, vo_ref[b * w:(b + 1) * w, :]], axis=0).astype(BF16))
    scores = [[jnp.dot(q_ref[b * w:(b + 1) * w, g * KV_WIDTH:(g + 1) * KV_WIDTH] * hm_ref[kh].astype(BF16),
                       kts[b], preferred_element_type=F32) for g, kh in heads] for b in range(nb)]
    for b in range(nb):
        probs = []
        for (g, kh), s in zip(heads, scores[b]):
            h = kh * SWA_GROUP + g
            s = s + bias_ref[h]
            if b == 0:
                s = jnp.where(has_prev, s, NEG_BIG)
            probs.append(_softmax_with_sink(s, sink_ref[h]))
        outs = [jnp.dot(p, vvs[b], preferred_element_type=F32) for p in probs]
        for g in range(SWA_GROUP):
            acc = None
            for (cg, kh), o in zip(heads, outs):
                if cg == g:
                    t = o * hm_ref[kh]
                    acc = t if acc is None else acc + t
            o_ref[b * w:(b + 1) * w, g * KV_WIDTH:(g + 1) * KV_WIDTH] = acc.astype(BF16)


def _swa_prompt(q, kt, v, sinks, *, n_rows, nb):
    w = WINDOW
    step = nb * w
    assert n_rows % step == 0
    prev = lambda j, sink: jnp.maximum(j * nb - 1, 0)
    return pl.pallas_call(
        functools.partial(_swa_prompt_kernel, nb=nb),
        grid_spec=pltpu.PrefetchScalarGridSpec(
            num_scalar_prefetch=1,
            grid=(n_rows // step,),
            in_specs=[pl.BlockSpec((step, MAIN_WIDTH), lambda j, sink: (j, 0)),
                      pl.BlockSpec((KV_WIDTH, w), lambda j, sink: (0, prev(j, sink))),
                      pl.BlockSpec((KV_WIDTH, step), lambda j, sink: (0, j)),
                      pl.BlockSpec((w, KV_WIDTH), lambda j, sink: (prev(j, sink), 0)),
                      pl.BlockSpec((step, KV_WIDTH), lambda j, sink: (j, 0)),
                      pl.BlockSpec((SWA_HEADS, w, 2 * w), lambda j, sink: (0, 0, 0)),
                      pl.BlockSpec((SWA_KV_HEADS, 1, KV_WIDTH), lambda j, sink: (0, 0, 0))],
            out_specs=pl.BlockSpec((step, MAIN_WIDTH), lambda j, sink: (j, 0)),
        ),
        out_shape=jax.ShapeDtypeStruct((n_rows, MAIN_WIDTH), BF16),
        compiler_params=_params(("arbitrary",)),
        name="swa_prompt",
    )(sinks.astype(F32), q, kt, kt, v, v, jnp.asarray(_swa_bias(w)), _head_masks(SWA_KV_HEADS))


def _swa_sample_kernel(q_ref, kp_ref, ko_ref, vp_ref, vo_ref, bias_ref, sink_ref, hm_ref, o_ref, *, nb, tq):
    w = WINDOW
    heads = [(kh, g) for kh in range(SWA_KV_HEADS) for g in range(SWA_GROUP)]
    kks, vvs, scores = [], [], []
    q = q_ref[...].astype(F32)
    for i in range(nb):
        win = slice(i * KV_WIDTH, (i + 1) * KV_WIDTH)
        kks.append(jnp.concatenate([kp_ref[win, :].T, ko_ref[i * tq:(i + 1) * tq, :]], axis=0))
        vvs.append(jnp.concatenate([vp_ref[win, :].T, vo_ref[i * tq:(i + 1) * tq, :]], axis=0))
        qs = jnp.concatenate([q[i * tq:(i + 1) * tq, g * KV_WIDTH:(g + 1) * KV_WIDTH] * hm_ref[kh]
                              for kh, g in heads], axis=0)
        scores.append(_bdot_nt(qs, kks[i]))
    probs = [_softmax_with_sink(s + bias_ref[...], sink_ref[...]) for s in scores]
    outs = [_bdot(p, vv) for p, vv in zip(probs, vvs)]
    for g in range(SWA_GROUP):
        rows = []
        for i in range(nb):
            acc = None
            for r, (kh, hg) in enumerate(heads):
                if hg == g:
                    t = outs[i][r * tq:(r + 1) * tq] * hm_ref[kh]
                    acc = t if acc is None else acc + t
            rows.append(acc)
        o_ref[:, g * KV_WIDTH:(g + 1) * KV_WIDTH] = jnp.concatenate(rows, axis=0).astype(BF16)


def _swa_sample(q, k_win, v_win, k, v, sinks, *, row_off, batch, tq, nb):
    w = WINDOW
    assert batch % nb == 0 and row_off % (nb * tq) == 0
    off = row_off // (nb * tq)
    bias = jnp.asarray(_swa_bias(tq).reshape(SWA_HEADS * tq, w + tq))
    sink_col = jnp.repeat(sinks.astype(F32), tq).reshape(SWA_HEADS * tq, 1)
    own = lambda width: pl.BlockSpec((nb * tq, width), lambda b: (off + b, 0))
    win = pl.BlockSpec((nb * KV_WIDTH, w), lambda b: (b, 0))
    return pl.pallas_call(
        functools.partial(_swa_sample_kernel, nb=nb, tq=tq),
        grid=(batch // nb,),
        in_specs=[own(MAIN_WIDTH), win, own(KV_WIDTH), win, own(KV_WIDTH), _const(bias.shape),
                  _const(sink_col.shape), _const((SWA_KV_HEADS, 1, KV_WIDTH))],
        out_specs=pl.BlockSpec((nb * tq, MAIN_WIDTH), lambda b: (b, 0)),
        out_shape=jax.ShapeDtypeStruct((batch * tq, MAIN_WIDTH), BF16),
        compiler_params=_params(("arbitrary",)),
        name="swa_sample",
    )(q, k_win, k, v_win, v, bias, sink_col, _head_masks(SWA_KV_HEADS))


def kernel(x_prompt, x_sample, state_gla, cache_win_k, cache_win_v, cache_mem_k, cache_mem_v, mem_prompt, norm_mix_g, norm_ffn_g, norm_mem_g, w_mem_kv, mem_qn_g, mem_kn_g, w_out, w_in_a, w_gate_lr, b_gate_lr, gla_norm_g, w_in_b, swa_qn_g, swa_sinks, norm_kv_g, w_kv, swa_kn_g, w_router_group, b_router_group, w_router_expert, b_router_expert, w_exp_gate, w_exp_up, w_exp_down):
    bp, tp, d = x_prompt.shape
    bs, ts, _ = x_sample.shape
    assert bp == 1 and tp % WINDOW == 0 and ts * (GLA_CHUNK // ts) == GLA_CHUNK
    n_p, n_s = bp * tp, bs * ts
    w_buf = cache_win_k.shape[1]
    assert w_buf == WINDOW
    x_p, x_s = x_prompt.reshape(n_p, d), x_sample.reshape(n_s, d)

    mem_k_p, mem_v_p = _mem_kv(mem_prompt, norm_mem_g, w_mem_kv, mem_kn_g)
    feature_major = lambda c: jnp.moveaxis(c, -3, -1).reshape(*c.shape[:-3], c.shape[-2] * c.shape[-1], c.shape[-3])
    cmk, cmv = feature_major(cache_mem_k), feature_major(cache_mem_v)

    def mem_attend(mq, l):
        tm_p = _row_tile(tp, 512)
        mo_p = _mem_attn(mq, mem_k_p, mem_v_p, mem_qn_g[l], row_off=0, seq=tp, tm=tm_p, bb=1, layer=l)
        mo_s = _mem_attn(mq, cmk, cmv, mem_qn_g[l], row_off=n_p, seq=ts, tm=ts, bb=8, layer=l)
        return mo_p, mo_s

    def moe(h, l, split_rows=None):
        return _moe(h, norm_ffn_g[l], w_router_group[l], b_router_group[l], w_router_expert[l],
                    b_router_expert[l], w_exp_gate, w_exp_up, w_exp_down, l, split_rows)

    q, k, la, v, og, mq = _inproj_a(x_p, x_s, norm_mix_g[0], w_in_a[0], w_gate_lr[0], b_gate_lr[0])
    zero_state = jnp.zeros((bp, GLA_HEADS, GLA_DK, GLA_DV), F32)
    n_sub = max(1, min(8, tp // GLA_CHUNK))
    main_p, gla_p = _gla(q, k, la, v, og, zero_state, gla_norm_g[0], row_off=0, seq=tp, n_seg=1, n_sub=n_sub)
    main_s, gla_s = _gla(q, k, la, v, og, state_gla[0], gla_norm_g[0], row_off=n_p, seq=ts,
                         n_seg=GLA_CHUNK // ts, n_sub=1)
    mo_p, mo_s = mem_attend(mq, 0)
    w_o = w_out[0]
    h = _outproj((x_p, x_s), main_p, mo_p, main_s, mo_s, w_o[:MAIN_WIDTH].reshape(GLA_HEADS, GLA_DV, d),
                 w_o[MAIN_WIDTH:])
    h = moe(h, 0)

    q, mq, k_sh, v_sh, kt_sh = _inproj_b(h, norm_kv_g, norm_mix_g[1], w_kv, w_in_b[0], swa_kn_g, swa_qn_g[0])
    ck = feature_major(cache_win_k).reshape(bs * KV_WIDTH, w_buf)
    cv = feature_major(cache_win_v).reshape(bs * KV_WIDTH, w_buf)
    main_p = _swa_prompt(q, kt_sh, v_sh, swa_sinks[0], n_rows=n_p, nb=4)
    main_s = _swa_sample(q, ck, cv, k_sh, v_sh, swa_sinks[0], row_off=n_p, batch=bs, tq=ts, nb=8)
    mo_p, mo_s = mem_attend(mq, 1)
    w_o = w_out[1]
    h = _outproj(h, main_p, mo_p, main_s, mo_s, w_o[:MAIN_WIDTH][_swa_perm()], w_o[MAIN_WIDTH:])
    y_p, y_s = moe(h, 1, split_rows=n_p)

    y_prompt = y_p.reshape(bp, tp, d)
    y_sample = y_s.reshape(bs, ts, d)
    k_new = k_sh[n_p:].reshape(bs, ts, SWA_KV_HEADS, HEAD_DIM)
    v_new = v_sh[n_p:].reshape(bs, ts, SWA_KV_HEADS, HEAD_DIM)
    win_k_s = jnp.concatenate([cache_win_k, k_new], axis=1)[:, -w_buf:]
    win_v_s = jnp.concatenate([cache_win_v, v_new], axis=1)[:, -w_buf:]
    win_k_p = k_sh[n_p - WINDOW:n_p].reshape(bp, WINDOW, SWA_KV_HEADS, HEAD_DIM)
    win_v_p = v_sh[n_p - WINDOW:n_p].reshape(bp, WINDOW, SWA_KV_HEADS, HEAD_DIM)
    token_major = lambda c: jnp.moveaxis(c.reshape(*c.shape[:-2], MEM_HEADS, HEAD_DIM, c.shape[-1]), -1, -3)
    return (y_prompt, y_sample, gla_p[None], gla_s[None], win_k_p, win_v_p, win_k_s, win_v_s,
            token_major(mem_k_p), token_major(mem_v_p))
```

```python
import functools
import math

import numpy as np
import jax
import jax.numpy as jnp
from jax import lax
from jax.experimental import pallas as pl
from jax.experimental.pallas import tpu as pltpu

F32 = jnp.float32
BF16 = jnp.bfloat16

D_MODEL = 1024
MEM_LEN = 256
MEM_HEADS = 4
HEAD_DIM = 64
MEM_WIDTH = MEM_HEADS * HEAD_DIM
MAIN_WIDTH = D_MODEL - MEM_WIDTH
GLA_HEADS = 4
GLA_DV = MAIN_WIDTH // GLA_HEADS
GLA_DK = GLA_DV // 2
GLA_DK_PAD = 128
GLA_KEY_WIDTH = GLA_HEADS * GLA_DK
GLA_KEY_PAD = GLA_HEADS * GLA_DK_PAD
GLA_GATE_RANK = 16
GLA_TAU = 16.0
GLA_CHUNK = 64
SWA_HEADS = MAIN_WIDTH // HEAD_DIM
SWA_KV_HEADS = 4
SWA_GROUP = SWA_HEADS // SWA_KV_HEADS
KV_WIDTH = SWA_KV_HEADS * HEAD_DIM
WINDOW = 128
N_GROUPS = 4
EXPERTS_PER_GROUP = 8
N_EXPERTS = N_GROUPS * EXPERTS_PER_GROUP
D_EXPERT = 512
EPS = 1e-6
LANES = 128
NEG_BIG = -1e30
VMEM_LIMIT = 56 * 1024 * 1024
MOE_TILE = 512
ROUTER_LANE0 = N_GROUPS
ROUTER_META_ROWS = 8
SLABS = D_MODEL // LANES
PACKED_SLABS = SLABS // 2


def _bdot(a, b):
    return jnp.dot(a.astype(BF16), b.astype(BF16), preferred_element_type=F32)


def _bdot_nt(a, b):
    return lax.dot_general(a.astype(BF16), b.astype(BF16), (((1,), (1,)), ((), ())),
                           preferred_element_type=F32)


def _bdot_tn(a, b):
    return lax.dot_general(a.astype(BF16), b.astype(BF16), (((0,), (0,)), ((), ())),
                           preferred_element_type=F32)


def _split(x, n):
    parts = []
    for _ in range(n - 1):
        p = x.astype(BF16)
        parts.append(p)
        x = x - p.astype(F32)
    parts.append(x.astype(BF16))
    return parts


def _exact_left_dot(m, x, n=3):
    out = None
    for p in _split(x, n):
        t = jnp.dot(m, p, preferred_element_type=F32)
        out = t if out is None else out + t
    return out


def _seg_mean(x2, bd):
    out = None
    for p in _split(x2, 2):
        t = jnp.dot(p, bd, preferred_element_type=F32)
        out = t if out is None else out + t
    return out


def _rms_scale(x):
    return lax.rsqrt(jnp.mean(x * x, axis=-1, keepdims=True) + EPS)


def _row_tile(n, cap=512):
    t = cap
    while t > 8 and n % t:
        t //= 2
    assert n % t == 0, n
    return t


def _params(sem):
    return pltpu.CompilerParams(dimension_semantics=sem, vmem_limit_bytes=VMEM_LIMIT)


def _const(shape):
    nd = len(shape)
    return pl.BlockSpec(shape, lambda *_: (0,) * nd)


def _group_specs(tm, width, prompt_tiles, lead=None):
    p_idx = lambda i, *_: jnp.minimum(i, prompt_tiles - 1)
    s_idx = lambda i, *_: jnp.maximum(i - prompt_tiles, 0)
    if lead is None:
        return [pl.BlockSpec((tm, width), lambda i, *_, f=f: (f(i), 0)) for f in (p_idx, s_idx)]
    return [pl.BlockSpec((lead, tm, width), lambda i, *_, f=f: (0, f(i), 0)) for f in (p_idx, s_idx)]


def _block_diag_mean(width):
    i = np.arange(width)
    return jnp.asarray((i[:, None] // HEAD_DIM == i[None, :] // HEAD_DIM) / HEAD_DIM, BF16)


def _head_masks(n_heads):
    i = np.arange(n_heads * HEAD_DIM)
    return jnp.asarray((i[None, :] // HEAD_DIM == np.arange(n_heads)[:, None]), F32)[:, None, :]


def _mem_kv_kernel(mem_ref, g_ref, w_ref, kng_ref, bd_ref, k_ref, v_ref):
    x = mem_ref[0]
    hn = x * _rms_scale(x) * g_ref[0]
    kv = _bdot(hn, w_ref[0])
    k = kv[:, :MEM_WIDTH]
    k = k * lax.rsqrt(_seg_mean(k * k, bd_ref[...]) + EPS) * kng_ref[0]
    k_ref[0, 0] = k.T
    v_ref[0, 0] = kv[:, MEM_WIDTH:].T


def _mem_kv(mem, g, w, kng):
    depth, (b, m, d) = w.shape[0], mem.shape
    out = jax.ShapeDtypeStruct((depth, b, m, MEM_WIDTH), F32)
    blk = pl.BlockSpec((1, 1, m, MEM_WIDTH), lambda l, i: (l, i, 0, 0))
    return pl.pallas_call(
        _mem_kv_kernel,
        grid=(depth, b),
        in_specs=[pl.BlockSpec((1, m, d), lambda l, i: (i, 0, 0)),
                  pl.BlockSpec((1, 1, d), lambda l, i: (l, 0, 0)),
                  pl.BlockSpec((1, d, 2 * MEM_WIDTH), lambda l, i: (l, 0, 0)),
                  pl.BlockSpec((1, 1, MEM_WIDTH), lambda l, i: (l, 0, 0)),
                  _const((MEM_WIDTH, MEM_WIDTH))],
        out_specs=[blk, blk],
        out_shape=[out, out],
        compiler_params=_params(("arbitrary", "arbitrary")),
        name="mem_kv",
    )(mem, g.reshape(depth, 1, d), w.astype(BF16),
      jnp.tile(kng, (1, MEM_HEADS)).reshape(depth, 1, MEM_WIDTH), _block_diag_mean(MEM_WIDTH))


def _inproj_a_kernel(xp_ref, xs_ref, g_ref, wq_ref, wk_ref, wv_ref, wog_ref, wlr_ref, wmq_ref, wgl_ref, bgl_ref,
                     q_ref, k_ref, la_ref, v_ref, og_ref, mq_ref, *, prompt_tiles):
    x = jnp.where(pl.program_id(0) < prompt_tiles, xp_ref[...], xs_ref[...])
    hn = (x * _rms_scale(x) * g_ref[...]).astype(BF16)
    q_ref[...] = jnp.dot(hn, wq_ref[...], preferred_element_type=F32) * (GLA_DK ** -0.5)
    k_ref[...] = jnp.dot(hn, wk_ref[...], preferred_element_type=F32)
    for h in range(GLA_HEADS):
        v_ref[h] = jnp.dot(hn, wv_ref[h], preferred_element_type=F32).astype(BF16)
        og_ref[h] = jnp.dot(hn, wog_ref[h], preferred_element_type=F32)
    lr = jnp.dot(hn, wlr_ref[...], preferred_element_type=F32)
    z = _bdot(lr, wgl_ref[...]) + bgl_ref[...]
    la_ref[...] = (jnp.minimum(z, 0.0) - jnp.log(1.0 + jnp.exp(-jnp.abs(z)))) * (1.0 / GLA_TAU)
    mq_ref[...] = jnp.dot(hn, wmq_ref[...], preferred_element_type=F32)


def _pad_heads(w, width, pad):
    lead = w.shape[:-1]
    w = w.reshape(*lead, GLA_HEADS, width)
    w = jnp.pad(w, [(0, 0)] * len(lead) + [(0, 0), (0, pad - width)])
    return w.reshape(*lead, GLA_HEADS * pad)


def _inproj_a(x_p, x_s, g, w_in, w_lr, b_lr):
    (n_p, d), n_s = x_p.shape, x_s.shape[0]
    n = n_p + n_s
    tm = _row_tile(n_s)
    assert n_p % tm == 0
    pt = n_p // tm
    c0, c1, c2, c3, c4 = (GLA_KEY_WIDTH, 2 * GLA_KEY_WIDTH, 2 * GLA_KEY_WIDTH + MAIN_WIDTH,
                          2 * GLA_KEY_WIDTH + 2 * MAIN_WIDTH,
                          2 * GLA_KEY_WIDTH + 2 * MAIN_WIDTH + GLA_GATE_RANK)
    wb = w_in.astype(BF16)
    wq = _pad_heads(wb[:, :c0], GLA_DK, GLA_DK_PAD)
    wk = _pad_heads(wb[:, c0:c1], GLA_DK, GLA_DK_PAD)
    wv = wb[:, c1:c2].reshape(d, GLA_HEADS, GLA_DV).transpose(1, 0, 2)
    wog = wb[:, c2:c3].reshape(d, GLA_HEADS, GLA_DV).transpose(1, 0, 2)
    wlr = jnp.pad(wb[:, c3:c4], ((0, 0), (0, LANES - GLA_GATE_RANK)))
    wmq = wb[:, c4:]
    wgl = jnp.pad(_pad_heads(w_lr.astype(BF16), GLA_DK, GLA_DK_PAD), ((0, LANES - GLA_GATE_RANK), (0, 0)))
    bgl = _pad_heads(b_lr.reshape(1, -1), GLA_DK, GLA_DK_PAD)
    row = lambda w: pl.BlockSpec((tm, w), lambda i: (i, 0))
    hrow = pl.BlockSpec((GLA_HEADS, tm, GLA_DV), lambda i: (0, i, 0))
    key = jax.ShapeDtypeStruct((n, GLA_KEY_PAD), F32)
    val = jax.ShapeDtypeStruct((GLA_HEADS, n, GLA_DV), F32)
    return pl.pallas_call(
        functools.partial(_inproj_a_kernel, prompt_tiles=pt),
        grid=(n // tm,),
        in_specs=_group_specs(tm, d, pt) + [
            _const((1, d)), _const(wq.shape), _const(wk.shape), _const(wv.shape),
            _const(wog.shape), _const(wlr.shape), _const(wmq.shape), _const(wgl.shape),
            _const(bgl.shape)],
        out_specs=[row(GLA_KEY_PAD), row(GLA_KEY_PAD), row(GLA_KEY_PAD), hrow, hrow, row(MEM_WIDTH)],
        out_shape=[key, key, key, jax.ShapeDtypeStruct(val.shape, BF16), val,
                   jax.ShapeDtypeStruct((n, MEM_WIDTH), F32)],
        compiler_params=_params(("parallel",)),
        name="inproj_a",
    )(x_p, x_s, g.reshape(1, d), wq, wk, wv, wog, wlr, wmq, wgl, bgl)


def _gla_kernel(q_ref, k_ref, la_ref, v_ref, og_ref, s0_ref, gn_ref, mcum_ref, mall_ref, sel_ref,
                o_ref, sout_ref, s_ref, *, chunk, n_sub, n_seg):
    j = pl.program_id(1)
    seg = chunk // n_seg

    @pl.when(j == 0)
    def _():
        s_ref[...] = jnp.zeros_like(s_ref)
        s_ref[:, :, :GLA_DK, :] = s0_ref[...]

    mcum = mcum_ref[...]
    causal = mcum.astype(F32) > 0.0
    row = lax.broadcasted_iota(jnp.int32, (chunk, GLA_DK_PAD), 0)
    gn = gn_ref[...]
    hcols = [slice(h * GLA_DK_PAD, (h + 1) * GLA_DK_PAD) for h in range(GLA_HEADS)]
    crows = [slice(c * chunk, (c + 1) * chunk) for c in range(n_sub)]
    qts, kts, kds, e_ends = [], [], [], []
    for rows in crows:
        la = la_ref[rows, :]
        b = _exact_left_dot(mcum, la)
        b_end = _exact_left_dot(mall_ref[...], la)
        e_ends.append(jnp.exp(_exact_left_dot(sel_ref[...], la)).T)
        k = k_ref[rows, :]
        qts.append(q_ref[rows, :] * jnp.exp(b))
        kts.append((k * jnp.exp(-b)).astype(BF16))
        kds.append(k * jnp.exp(b_end - b))
    vbs = [[v_ref[h, rows, :].astype(BF16) for h in range(GLA_HEADS)] for rows in crows]
    scores = [[_bdot_nt(qts[c][:, cols], kts[c][:, cols]) for cols in hcols] for c in range(n_sub)]
    kvs = []
    for c in range(n_sub):
        per_head = []
        for h, cols in enumerate(hcols):
            per_seg = []
            for s in range(n_seg):
                kd = kds[c][:, cols]
                if n_seg > 1:
                    kd = jnp.where((row >= s * seg) & (row < (s + 1) * seg), kd, 0.0)
                per_seg.append(_bdot_tn(kd, vbs[c][h]))
            per_head.append(per_seg)
        kvs.append(per_head)
    state = [[s_ref[s, h] for s in range(n_seg)] for h in range(GLA_HEADS)]
    inters = []
    for c in range(n_sub):
        per_head = []
        for h, cols in enumerate(hcols):
            parts = []
            for s in range(n_seg):
                parts.append(_bdot(qts[c][s * seg:(s + 1) * seg, cols], state[h][s]))
                state[h][s] = e_ends[c][cols, s:s + 1] * state[h][s] + kvs[c][h][s]
            per_head.append(parts[0] if n_seg == 1 else jnp.concatenate(parts, axis=0))
        inters.append(per_head)
    for h in range(GLA_HEADS):
        for s in range(n_seg):
            s_ref[s, h] = state[h][s]
    for c, rows in enumerate(crows):
        for h in range(GLA_HEADS):
            a = jnp.where(causal, scores[c][h], 0.0)
            o = _bdot(a, vbs[c][h]) + inters[c][h]
            on = o * lax.rsqrt(jnp.mean(o * o, axis=-1, keepdims=True) + EPS) * gn
            og = og_ref[h, rows, :]
            o_ref[h, rows, :] = (on * (og * jax.nn.sigmoid(og))).astype(BF16)

    @pl.when(j == pl.num_programs(1) - 1)
    def _():
        sout_ref[...] = s_ref[:, :, :GLA_DK, :]


def _gla(q, k, la, v, og, s0, gnorm, *, row_off, seq, n_seg, n_sub):
    batch = s0.shape[0]
    chunk = GLA_CHUNK
    assert chunk % n_seg == 0 and batch % n_seg == 0
    seg = chunk // n_seg
    step_rows = n_sub * chunk
    if n_seg > 1:
        assert seq == seg and n_sub == 1
        t_steps = 1
    else:
        assert seq % step_rows == 0
        t_steps = seq // step_rows
    assert row_off % step_rows == 0
    off = row_off // step_rows
    i = np.arange(chunk)
    same = (i[:, None] // seg) == (i[None, :] // seg)
    mcum = jnp.asarray(same & (i[None, :] <= i[:, None]), BF16)
    mall = jnp.asarray(same, BF16)
    sel = jnp.asarray((i[None, :] // seg) == np.arange(LANES)[:, None], BF16)
    ridx = lambda g, j: (off + g * t_steps + j, 0)
    hidx = lambda g, j: (0, off + g * t_steps + j, 0)
    key_spec = pl.BlockSpec((step_rows, GLA_KEY_PAD), ridx)
    val_spec = pl.BlockSpec((GLA_HEADS, step_rows, GLA_DV), hidx)
    st_spec = pl.BlockSpec((n_seg, GLA_HEADS, GLA_DK, GLA_DV), lambda g, j: (g, 0, 0, 0))
    in_specs = [key_spec, key_spec, key_spec, val_spec, val_spec, st_spec, _const((1, GLA_DV)),
                _const((chunk, chunk)), _const((chunk, chunk)), _const((LANES, chunk))]
    args = [q, k, la, v, og, s0, gnorm.reshape(1, GLA_DV), mcum, mall, sel]
    out_spec = pl.BlockSpec((GLA_HEADS, step_rows, GLA_DV), lambda g, j: (0, g * t_steps + j, 0))
    return pl.pallas_call(
        functools.partial(_gla_kernel, chunk=chunk, n_sub=n_sub, n_seg=n_seg),
        grid=(batch // n_seg, t_steps),
        in_specs=in_specs,
        out_specs=[out_spec, st_spec],
        out_shape=[jax.ShapeDtypeStruct((GLA_HEADS, batch * seq, GLA_DV), BF16),
                   jax.ShapeDtypeStruct(s0.shape, F32)],
        scratch_shapes=[pltpu.VMEM((n_seg, GLA_HEADS, GLA_DK_PAD, GLA_DV), F32)],
        compiler_params=_params(("arbitrary", "arbitrary")),
        name="gla",
    )(*args)


def _mem_attn_kernel(q_ref, k_ref, v_ref, g_ref, bd_ref, hm_ref, o_ref, *, tm, bb):
    g = g_ref[...]
    sub = min(tm, 128)
    units = [(i, i * tm + r) for i in range(bb) for r in range(0, tm, sub)]
    scores = []
    for i, r in units:
        q = q_ref[r:r + sub, :]
        qn = q * lax.rsqrt(_seg_mean(q * q, bd_ref[...]) + EPS) * g
        qs = jnp.concatenate([(qn * hm_ref[h]).astype(BF16) for h in range(MEM_HEADS)], axis=0)
        scores.append(_bdot(qs, k_ref[i]))
    probs = []
    for s in scores:
        e = jnp.exp(s - jnp.max(s, axis=-1, keepdims=True))
        probs.append(e * (1.0 / jnp.sum(e, axis=-1, keepdims=True)))
    outs = [_bdot_nt(p, v_ref[i]) for (i, _), p in zip(units, probs)]
    rows = []
    for o in outs:
        acc = o[:sub] * hm_ref[0]
        for h in range(1, MEM_HEADS):
            acc = acc + o[h * sub:(h + 1) * sub] * hm_ref[h]
        rows.append(acc)
    o_ref[...] = jnp.concatenate(rows, axis=0).astype(BF16)


def _mem_attn(mq, mk, mv, qng, *, row_off, seq, tm, bb, layer):
    depth, batch, m, _ = mk.shape
    mk = mk.reshape(depth * batch, m, MEM_WIDTH)
    mv = mv.reshape(depth * batch, m, MEM_WIDTH)
    kv_off = layer * batch // bb
    assert seq % tm == 0 and batch % bb == 0 and (bb == 1 or seq == tm)
    t_steps = seq // tm
    step_rows = bb * tm
    assert row_off % step_rows == 0
    off = row_off // step_rows
    row_spec = pl.BlockSpec((step_rows, MEM_WIDTH), lambda g, j: (off + g * t_steps + j, 0))
    kv_spec = pl.BlockSpec((bb, m, MEM_WIDTH), lambda g, j: (kv_off + g, 0, 0))
    in_specs = [row_spec, kv_spec, kv_spec, _const((1, MEM_WIDTH)), _const((MEM_WIDTH, MEM_WIDTH)),
                _const((MEM_HEADS, 1, MEM_WIDTH))]
    args = [mq, mk, mv, (jnp.tile(qng, MEM_HEADS) * HEAD_DIM ** -0.5).reshape(1, MEM_WIDTH),
            _block_diag_mean(MEM_WIDTH), _head_masks(MEM_HEADS)]
    return pl.pallas_call(
        functools.partial(_mem_attn_kernel, tm=tm, bb=bb),
        grid=(batch // bb, t_steps),
        in_specs=in_specs,
        out_specs=pl.BlockSpec((step_rows, MEM_WIDTH), lambda g, j: (g * t_steps + j, 0)),
        out_shape=jax.ShapeDtypeStruct((batch * seq, MEM_WIDTH), BF16),
        compiler_params=_params(("parallel", "parallel")),
        name="mem_attn",
    )(*args)


def _outproj_kernel(*refs, heads, prompt_tiles, split_residual):
    if split_residual:
        hp_ref, hs_ref = refs[:2]
        refs = refs[2:]
    else:
        hp_ref = hs_ref = refs[0]
        refs = refs[1:]
    main_p_ref, main_s_ref, mo_p_ref, mo_s_ref, wmain_ref, wmo_ref, o_ref = refs

    def project(h_ref, main_ref, mo_ref):
        acc = h_ref[...] + _bdot(mo_ref[...], wmo_ref[...])
        if heads:
            for h in range(heads):
                acc = acc + _bdot(main_ref[h], wmain_ref[h])
        else:
            acc = acc + _bdot(main_ref[...], wmain_ref[...])
        o_ref[...] = acc

    @pl.when(pl.program_id(0) < prompt_tiles)
    def _():
        project(hp_ref, main_p_ref, mo_p_ref)

    @pl.when(pl.program_id(0) >= prompt_tiles)
    def _():
        project(hs_ref, main_s_ref, mo_s_ref)


def _outproj(h, main_p, mo_p, main_s, mo_s, w_main, w_mo):
    n_p, n_s = mo_p.shape[0], mo_s.shape[0]
    n, d = n_p + n_s, w_mo.shape[1]
    tm = _row_tile(n_s)
    assert n_p % tm == 0
    pt = n_p // tm
    heads = main_p.shape[0] if main_p.ndim == 3 else 0
    row = pl.BlockSpec((tm, d), lambda i: (i, 0))
    split = isinstance(h, tuple)
    h_specs, h_args = (_group_specs(tm, d, pt), list(h)) if split else ([row], [h])
    main_specs = _group_specs(tm, main_p.shape[-1], pt, lead=heads or None)
    return pl.pallas_call(
        functools.partial(_outproj_kernel, heads=heads, prompt_tiles=pt, split_residual=split),
        grid=(n // tm,),
        in_specs=h_specs + main_specs + _group_specs(tm, MEM_WIDTH, pt) + [_const(w_main.shape),
                                                                           _const(w_mo.shape)],
        out_specs=row,
        out_shape=jax.ShapeDtypeStruct((n, d), F32),
        compiler_params=_params(("parallel",)),
        name="outproj",
    )(*h_args, main_p, main_s, mo_p, mo_s, w_main.astype(BF16), w_mo.astype(BF16))


def _router_kernel(h_ref, g_ref, whi_ref, wlo_ref, b_ref, tril_ref, before_ref,
                   mi_ref, mf_ref, cnt_ref, tt_ref, carry_ref):
    i = pl.program_id(0)

    @pl.when(i == 0)
    def _():
        carry_ref[...] = jnp.zeros_like(carry_ref)

    x = h_ref[...]
    xn = x * _rms_scale(x) * g_ref[...]
    x_hi, x_lo = _split(xn, 2)
    logits = (jnp.dot(x_hi, whi_ref[...], preferred_element_type=F32)
              + jnp.dot(x_hi, wlo_ref[...], preferred_element_type=F32)
              + jnp.dot(x_lo, whi_ref[...], preferred_element_type=F32)) + b_ref[...]
    tm = x.shape[0]
    lane = lax.broadcasted_iota(jnp.int32, (tm, LANES), 1)
    far = jnp.int32(2 * LANES)

    def first_max(vals):
        m = jnp.max(vals, axis=-1, keepdims=True)
        return m, jnp.min(jnp.where(vals == m, lane, far), axis=-1, keepdims=True)

    gl = jnp.where(lane < N_GROUPS, logits, -jnp.inf)
    gmax, grp = first_max(gl)
    pg_sel = 1.0 / jnp.sum(jnp.exp(gl - gmax), axis=-1, keepdims=True)
    lo = ROUTER_LANE0 + grp * EXPERTS_PER_GROUP
    el = jnp.where((lane >= lo) & (lane < lo + EXPERTS_PER_GROUP), logits, -jnp.inf)
    m1, i1 = first_max(el)
    m2, i2 = first_max(jnp.where(lane == i1, -jnp.inf, el))
    e2 = jnp.exp(m2 - m1)
    g1 = pg_sel / (1.0 + e2)
    g2 = pg_sel * e2 / (1.0 + e2)

    oh1 = lane == i1
    oh2 = lane == i2
    picked = jnp.where(oh1 | oh2, 1.0, 0.0)
    earlier = jnp.dot(tril_ref[...], picked.astype(BF16), preferred_element_type=F32)
    cnt_tile = jnp.sum(picked, axis=0, keepdims=True)
    c8 = jnp.broadcast_to(cnt_tile, (8, LANES))
    c_hi = jnp.floor(c8 * (1.0 / 32.0))
    c_lo = c8 - 32.0 * c_hi
    first = (32.0 * jnp.dot(c_hi.astype(BF16), before_ref[...], preferred_element_type=F32)
             + jnp.dot(c_lo.astype(BF16), before_ref[...], preferred_element_type=F32))[0:1]
    local = first + earlier
    lpos1 = jnp.sum(jnp.where(oh1, local, 0.0), axis=-1, keepdims=True)
    lpos2 = jnp.sum(jnp.where(oh2, local, 0.0), axis=-1, keepdims=True)
    carry_before = carry_ref[...]
    carry = carry_before + cnt_tile
    carry_ref[...] = carry
    cnt_ref[...] = carry

    row8 = lax.broadcasted_iota(jnp.int32, (8, LANES), 0)
    zero8 = jnp.zeros((8, LANES), F32)
    table = jnp.where(row8 == 0, carry_before, jnp.where(row8 == 1, cnt_tile, jnp.where(row8 == 2, first, zero8)))
    tt_ref[...] = table.astype(jnp.int32)
    zi = jnp.zeros((tm, LANES), jnp.int32)
    mi = jnp.where(lane == 0, lpos1.astype(jnp.int32), jnp.where(lane == 1, lpos2.astype(jnp.int32), zi))
    mi_ref[...] = mi.T[:ROUTER_META_ROWS]
    zf = jnp.zeros((tm, LANES), F32)
    mf_ref[...] = jnp.where(lane == 0, g1, jnp.where(lane == 1, g2,
                            jnp.where(lane == 2, lpos1, jnp.where(lane == 3, lpos2, zf))))


def _router(h, g, w_rg, b_rg, w_re, b_re, tm):
    n, d = h.shape
    assert n % tm == 0 and 2 * tm <= 32 * 32
    n_real = N_GROUPS + N_EXPERTS
    w = jnp.pad(jnp.concatenate([w_rg, w_re], axis=1), ((0, 0), (0, LANES - n_real)))
    b = jnp.pad(jnp.concatenate([b_rg, b_re]), (0, LANES - n_real)).reshape(1, LANES)
    w_hi = w.astype(BF16)
    w_lo = (w - w_hi.astype(F32)).astype(BF16)
    i = np.arange(tm)
    tril = jnp.asarray(i[None, :] < i[:, None], BF16)
    lane = np.arange(LANES)
    before = jnp.asarray(lane[:, None] < lane[None, :], BF16)
    row = lambda width: pl.BlockSpec((tm, width), lambda i: (i, 0))
    return pl.pallas_call(
        _router_kernel,
        grid=(n // tm,),
        in_specs=[row(d), _const((1, d)), _const((d, LANES)), _const((d, LANES)), _const((1, LANES)),
                  _const((tm, tm)), _const((LANES, LANES))],
        out_specs=[pl.BlockSpec((ROUTER_META_ROWS, tm), lambda i: (0, i)), row(LANES), _const((1, LANES)),
                   pl.BlockSpec((8, LANES), lambda i: (i, 0))],
        out_shape=[jax.ShapeDtypeStruct((ROUTER_META_ROWS, n), jnp.int32),
                   jax.ShapeDtypeStruct((n, LANES), F32), jax.ShapeDtypeStruct((1, LANES), F32),
                   jax.ShapeDtypeStruct((n // tm * 8, LANES), jnp.int32)],
        scratch_shapes=[pltpu.VMEM((1, LANES), F32)],
        compiler_params=_params(("arbitrary",)),
        name="moe_router",
    )(h, g.reshape(1, d), w_hi, w_lo, b, tril, before)


def _to_rows(ref, rows, lead=()):
    return jnp.concatenate([ref[lead + (pl.ds(s, rows, stride=SLABS), slice(None))] for s in range(SLABS)],
                           axis=1)


def _from_rows(ref, x, rows, lead=()):
    for s in range(SLABS):
        ref[lead + (pl.ds(s, rows, stride=SLABS), slice(None))] = x[:, s * LANES:(s + 1) * LANES]


def _pack_rows(ref, x, rows, lead=()):
    u32 = jnp.uint32
    for w in range(PACKED_SLABS):
        lo = x[:, (2 * w) * LANES:(2 * w + 1) * LANES].astype(BF16).astype(F32)
        hi = x[:, (2 * w + 1) * LANES:(2 * w + 2) * LANES].astype(BF16).astype(F32)
        word = (lax.bitcast_convert_type(lo, u32) >> 16) | (lax.bitcast_convert_type(hi, u32) & u32(0xFFFF0000))
        ref[lead + (pl.ds(w, rows, stride=PACKED_SLABS), slice(None))] = word


def _unpack_rows(ref, rows, lead=()):
    u32 = jnp.uint32
    slabs = []
    for w in range(PACKED_SLABS):
        word = ref[lead + (pl.ds(w, rows, stride=PACKED_SLABS), slice(None))]
        slabs.append(lax.bitcast_convert_type(word << 16, F32).astype(BF16))
        slabs.append(lax.bitcast_convert_type(word & u32(0xFFFF0000), F32).astype(BF16))
    return jnp.concatenate(slabs, axis=1)


RUN_FIELDS = 3
RUN_CHUNK_BITS = 6


def _copy_runs(runs_ref, tile, local_rows, global_rows, sem, *, to_global):
    ps = PACKED_SLABS
    base = tile * (RUN_FIELDS * N_EXPERTS)

    def piece(g0, l0, off, size):
        g = global_rows(pl.multiple_of((g0 + off) * ps, ps), size * ps)
        l = local_rows(pl.multiple_of((l0 + off) * ps, ps), size * ps)
        src, dst = (l, g) if to_global else (g, l)
        pltpu.make_async_copy(src, dst, sem).start()

    def per_expert(e, carry):
        g0 = runs_ref[base + e]
        length = runs_ref[base + N_EXPERTS + e]
        l0 = runs_ref[base + 2 * N_EXPERTS + e]
        big = 1 << RUN_CHUNK_BITS

        def big_piece(c, inner):
            piece(g0, l0, c * big, big)
            return inner

        n_big = length >> RUN_CHUNK_BITS
        lax.fori_loop(0, n_big, big_piece, 0)
        off = n_big * big
        for bit in reversed(range(RUN_CHUNK_BITS)):
            size = 1 << bit

            @pl.when((length & size) != 0)
            def _(off=off, size=size):
                piece(g0, l0, off, size)

            off = off + (length & size)
        return carry

    lax.fori_loop(0, N_EXPERTS, per_expert, 0)


def _dispatch_kernel(runs_ref, h_ref, g_ref, meta_ref, xs_hbm, buf, sem, *, tm, steps):
    i = pl.program_id(0)
    slot = lax.rem(i, 2)
    ns = 2 * tm

    def wait_slot(sl):
        pltpu.make_async_copy(buf.at[sl], xs_hbm.at[pl.ds(0, ns * PACKED_SLABS)], sem.at[sl]).wait()

    @pl.when(i >= 2)
    def _():
        wait_slot(slot)

    x = h_ref[...]
    xn = (x * _rms_scale(x) * g_ref[...]).astype(BF16)
    j = lax.broadcasted_iota(jnp.int32, (ns, tm), 0)
    pick = jnp.where((j == meta_ref[0:1, :]) | (j == meta_ref[1:2, :]), 1.0, 0.0).astype(BF16)
    _pack_rows(buf, jnp.dot(pick, xn, preferred_element_type=F32), ns, (slot,))
    _copy_runs(runs_ref, i, lambda start, size: buf.at[slot, pl.ds(start, size)],
               lambda start, size: xs_hbm.at[pl.ds(start, size)], sem.at[slot], to_global=True)

    @pl.when(i == steps - 1)
    def _():
        wait_slot(slot)
        if steps > 1:
            wait_slot(1 - slot)


def _dispatch(h, g, meta, runs, tm):
    n, d = h.shape
    steps = n // tm
    return pl.pallas_call(
        functools.partial(_dispatch_kernel, tm=tm, steps=steps),
        grid_spec=pltpu.PrefetchScalarGridSpec(
            num_scalar_prefetch=1,
            grid=(steps,),
            in_specs=[pl.BlockSpec((tm, d), lambda i, runs: (i, 0)),
                      pl.BlockSpec((1, d), lambda i, runs: (0, 0)),
                      pl.BlockSpec((ROUTER_META_ROWS, tm), lambda i, runs: (0, i))],
            out_specs=pl.BlockSpec(memory_space=pl.ANY),
            scratch_shapes=[pltpu.VMEM((2, 2 * tm * PACKED_SLABS, LANES), jnp.uint32),
                            pltpu.SemaphoreType.DMA((2,))],
        ),
        out_shape=jax.ShapeDtypeStruct((2 * n * PACKED_SLABS, LANES), jnp.uint32),
        compiler_params=_params(("arbitrary",)),
        name="moe_dispatch",
    )(runs, h, g.reshape(1, d), meta)


ITEM_FIRST, ITEM_LAST, ITEM_NEW_EXPERT = 1, 2, 4


def _expert_kernel(tile_ref, exp_ref, lo_ref, hi_ref, flag_ref, n_ref, xs_ref, wg_ref, wu_ref, wd_ref, ys_ref,
                   wgb, wub, wdb, acc, *, tm):
    w = pl.program_id(0)

    @pl.when(w < n_ref[0])
    def _():
        flags = flag_ref[w]

        @pl.when((flags & ITEM_NEW_EXPERT) != 0)
        def _():
            wgb[...] = wg_ref[0, 0].astype(BF16)
            wub[...] = wu_ref[0, 0].astype(BF16)
            wdb[...] = wd_ref[0, 0].astype(BF16)

        x = _unpack_rows(xs_ref, tm)
        hg = jnp.dot(x, wgb[...], preferred_element_type=F32)
        hu = jnp.dot(x, wub[...], preferred_element_type=F32)
        act = (hg * jax.nn.sigmoid(hg) * hu).astype(BF16)
        y = jnp.dot(act, wdb[...], preferred_element_type=F32)
        row = lax.broadcasted_iota(jnp.int32, (tm, 1), 0)
        y = jnp.where((row >= lo_ref[w]) & (row < hi_ref[w]), y, 0.0)
        first = (flags & ITEM_FIRST) != 0

        @pl.when(first)
        def _():
            acc[...] = y

        @pl.when(jnp.logical_not(first))
        def _():
            acc[...] += y

        @pl.when((flags & ITEM_LAST) != 0)
        def _():
            _pack_rows(ys_ref, acc[...], tm)


def _experts(xs, items, w_g, w_u, w_d, layer):
    tile, expert, lo, hi, flags, n_items = items
    d = D_MODEL
    tm = MOE_TILE
    packed_spec = pl.BlockSpec((tm * PACKED_SLABS, LANES), lambda w, tile, *_: (tile[w], 0))
    wmap = lambda w, tile, expert, *_: (layer, expert[w], 0, 0)
    return pl.pallas_call(
        functools.partial(_expert_kernel, tm=tm),
        grid_spec=pltpu.PrefetchScalarGridSpec(
            num_scalar_prefetch=6,
            grid=(tile.shape[0],),
            in_specs=[packed_spec,
                      pl.BlockSpec((1, 1, d, D_EXPERT), wmap),
                      pl.BlockSpec((1, 1, d, D_EXPERT), wmap),
                      pl.BlockSpec((1, 1, D_EXPERT, d), wmap)],
            out_specs=packed_spec,
            scratch_shapes=[pltpu.VMEM((d, D_EXPERT), BF16), pltpu.VMEM((d, D_EXPERT), BF16),
                            pltpu.VMEM((D_EXPERT, d), BF16), pltpu.VMEM((tm, d), F32)],
        ),
        out_shape=jax.ShapeDtypeStruct(xs.shape, jnp.uint32),
        compiler_params=_params(("arbitrary",)),
        name="moe_experts",
    )(tile, expert, lo, hi, flags, n_items, xs, w_g, w_u, w_d)


def _combine_kernel(runs_ref, h_ref, gate_ref, ys_hbm, *refs, tm, steps, prompt_tiles):
    out_refs, (buf, sem) = refs[:-2], refs[-2:]
    i = pl.program_id(0)
    slot = lax.rem(i, 2)
    ns = 2 * tm

    def issue(step, sl):
        _copy_runs(runs_ref, step, lambda start, size: buf.at[sl, pl.ds(start, size)],
                   lambda start, size: ys_hbm.at[pl.ds(start, size)], sem.at[sl], to_global=False)

    @pl.when(i == 0)
    def _():
        issue(0, 0)

    @pl.when(i + 1 < steps)
    def _():
        issue(i + 1, 1 - slot)

    pltpu.make_async_copy(ys_hbm.at[pl.ds(0, ns * PACKED_SLABS)], buf.at[slot], sem.at[slot]).wait()
    y = _unpack_rows(buf, ns, (slot,))
    g = gate_ref[...]
    j = lax.broadcasted_iota(jnp.int32, (tm, ns), 1)
    out = h_ref[...]
    for kk in range(2):
        pick = jnp.where(j == g[:, 2 + kk:3 + kk].astype(jnp.int32), 1.0, 0.0).astype(BF16)
        out = out + g[:, kk:kk + 1] * jnp.dot(pick, y, preferred_element_type=F32)
    if len(out_refs) == 1:
        out_refs[0][...] = out
    else:
        @pl.when(i < prompt_tiles)
        def _():
            out_refs[0][...] = out

        @pl.when(i >= prompt_tiles)
        def _():
            out_refs[1][...] = out


def _combine(h, gates, ys, runs, tm, split_rows=None):
    n, d = h.shape
    steps = n // tm
    row = pl.BlockSpec((tm, d), lambda i, pos: (i, 0))
    if split_rows is None:
        pt, out_specs, out_shape = 0, row, jax.ShapeDtypeStruct((n, d), F32)
    else:
        assert split_rows % tm == 0
        pt = split_rows // tm
        out_specs = _group_specs(tm, d, pt)
        out_shape = [jax.ShapeDtypeStruct((split_rows, d), F32), jax.ShapeDtypeStruct((n - split_rows, d), F32)]
    return pl.pallas_call(
        functools.partial(_combine_kernel, tm=tm, steps=steps, prompt_tiles=pt),
        grid_spec=pltpu.PrefetchScalarGridSpec(
            num_scalar_prefetch=1,
            grid=(steps,),
            in_specs=[row, pl.BlockSpec((tm, LANES), lambda i, pos: (i, 0)),
                      pl.BlockSpec(memory_space=pl.ANY)],
            out_specs=out_specs,
            scratch_shapes=[pltpu.VMEM((2, 2 * tm * PACKED_SLABS, LANES), jnp.uint32),
                            pltpu.SemaphoreType.DMA((2,))],
        ),
        out_shape=out_shape,
        compiler_params=_params(("arbitrary",)),
        name="moe_combine",
    )(runs, h, gates, ys)


def _lookup(tables, idx):
    hit = idx[:, None] == jnp.arange(tables.shape[1], dtype=idx.dtype)[None, :]
    return jnp.sum(jnp.where(hit[None], tables[:, None, :], 0), axis=2)


def _work_items(counts, n_slots):
    tm = MOE_TILE
    assert n_slots % tm == 0, (n_slots, tm)
    ends = jnp.cumsum(counts)
    starts = ends - counts
    first_tile = starts // tm
    n_tiles_e = jnp.where(counts > 0, (ends - 1) // tm - first_tile + 1, 0)
    item_end = jnp.cumsum(n_tiles_e)
    item_start = item_end - n_tiles_e
    n_items = item_end[-1]
    max_items = n_slots // tm + N_EXPERTS - 1
    w = jnp.minimum(jnp.arange(max_items, dtype=jnp.int32), n_items - 1)
    expert = jnp.sum(w[:, None] >= item_end[None, :], axis=1).astype(jnp.int32)
    first_tile_w, item_start_w, start_w, end_w = _lookup(jnp.stack([first_tile, item_start, starts, ends]), expert)
    tile = first_tile_w + w - item_start_w
    lo = jnp.maximum(start_w - tile * tm, 0)
    hi = jnp.minimum(end_w - tile * tm, tm)
    prev_tile = jnp.concatenate([jnp.full((1,), -1, jnp.int32), tile[:-1]])
    next_tile = jnp.concatenate([tile[1:], jnp.full((1,), -1, jnp.int32)])
    prev_expert = jnp.concatenate([jnp.full((1,), -1, jnp.int32), expert[:-1]])
    idx = jnp.arange(max_items, dtype=jnp.int32)
    flags = (jnp.where(tile != prev_tile, ITEM_FIRST, 0)
             | jnp.where((tile != next_tile) | (idx == n_items - 1), ITEM_LAST, 0)
             | jnp.where(expert != prev_expert, ITEM_NEW_EXPERT, 0))
    as_i32 = lambda a: a.astype(jnp.int32)
    return (as_i32(tile), expert, as_i32(lo), as_i32(hi), as_i32(flags), as_i32(n_items).reshape(1))


def _moe(h, g, w_rg, b_rg, w_re, b_re, w_g, w_u, w_d, layer, split_rows=None):
    n, _ = h.shape
    tm = _row_tile(math.gcd(n, split_rows or n), MOE_TILE)
    meta, gates, cnt, tables = _router(h, g, w_rg, b_rg, w_re, b_re, tm)
    experts = slice(ROUTER_LANE0, ROUTER_LANE0 + N_EXPERTS)
    counts = cnt[0, experts].astype(jnp.int32)
    starts = jnp.cumsum(counts) - counts
    tables = tables.reshape(n // tm, 8, LANES)[:, :RUN_FIELDS, experts]
    runs = tables.at[:, 0, :].add(starts[None, :]).reshape(-1)
    xs = _dispatch(h, g, meta, runs, tm)
    ys = _experts(xs, _work_items(counts, 2 * n), w_g, w_u, w_d, layer)
    return _combine(h, gates, ys, runs, tm, split_rows)


def _inproj_b_kernel(x_ref, gkv_ref, gmix_ref, wkv_ref, win_ref, kng_ref, qng_ref, bdk_ref, bdq_ref,
                     q_ref, mq_ref, k_ref, v_ref, kt_ref):
    x = x_ref[...]
    xr = x * _rms_scale(x)
    kv = _bdot(xr * gkv_ref[...], wkv_ref[...])
    k = kv[:, :KV_WIDTH]
    k = k * lax.rsqrt(_seg_mean(k * k, bdk_ref[...]) + EPS) * kng_ref[...]
    k_ref[...] = k
    kt_ref[...] = k.T.astype(BF16)
    v_ref[...] = kv[:, KV_WIDTH:]
    proj = _bdot(xr * gmix_ref[...], win_ref[...])
    q = proj[:, :MAIN_WIDTH]
    q_ref[...] = (q * lax.rsqrt(_seg_mean(q * q, bdq_ref[...]) + EPS) * qng_ref[...]).astype(BF16)
    mq_ref[...] = proj[:, MAIN_WIDTH:]


def _swa_perm():
    g, kh, dd = np.meshgrid(np.arange(SWA_GROUP), np.arange(SWA_KV_HEADS), np.arange(HEAD_DIM), indexing="ij")
    return ((kh * SWA_GROUP + g) * HEAD_DIM + dd).reshape(-1)


def _inproj_b(x, g_kv, g_mix, w_kv, w_in, kng, qng):
    n, d = x.shape
    tm = _row_tile(n)
    perm = _swa_perm()
    w_in_p = jnp.concatenate([w_in[:, :MAIN_WIDTH][:, perm], w_in[:, MAIN_WIDTH:]], axis=1).astype(BF16)
    qng_t = (jnp.tile(qng, SWA_HEADS) * HEAD_DIM ** -0.5).reshape(1, MAIN_WIDTH)
    row = lambda w: pl.BlockSpec((tm, w), lambda i: (i, 0))
    return pl.pallas_call(
        _inproj_b_kernel,
        grid=(n // tm,),
        in_specs=[row(d), _const((1, d)), _const((1, d)), _const((d, 2 * KV_WIDTH)), _const((d, d)),
                  _const((1, KV_WIDTH)), _const((1, MAIN_WIDTH)), _const((KV_WIDTH, KV_WIDTH)),
                  _const((MAIN_WIDTH, MAIN_WIDTH))],
        out_specs=[row(MAIN_WIDTH), row(MEM_WIDTH), row(KV_WIDTH), row(KV_WIDTH),
                   pl.BlockSpec((KV_WIDTH, tm), lambda i: (0, i))],
        out_shape=[jax.ShapeDtypeStruct((n, MAIN_WIDTH), BF16), jax.ShapeDtypeStruct((n, MEM_WIDTH), F32),
                   jax.ShapeDtypeStruct((n, KV_WIDTH), F32), jax.ShapeDtypeStruct((n, KV_WIDTH), F32),
                   jax.ShapeDtypeStruct((KV_WIDTH, n), BF16)],
        compiler_params=_params(("parallel",)),
        name="inproj_b",
    )(x, g_kv.reshape(1, d), g_mix.reshape(1, d), w_kv.astype(BF16), w_in_p,
      jnp.tile(kng, SWA_KV_HEADS).reshape(1, KV_WIDTH), qng_t,
      _block_diag_mean(KV_WIDTH), _block_diag_mean(MAIN_WIDTH))


def _softmax_with_sink(s, sink):
    m = jnp.maximum(jnp.max(s, axis=-1, keepdims=True), sink)
    e = jnp.exp(s - m)
    r = 1.0 / (jnp.sum(e, axis=-1, keepdims=True) + jnp.exp(sink - m))
    return (e * r).astype(BF16)


def _swa_bias(tq):
    slopes = 2.0 ** (-8.0 * np.arange(1, SWA_HEADS + 1, dtype=np.float64) / SWA_HEADS)
    dist = np.arange(tq)[:, None] + WINDOW - np.arange(WINDOW + tq)[None, :]
    valid = (dist >= 0) & (dist <= WINDOW)
    return np.stack([np.where(valid, -s * dist, NEG_BIG) for s in slopes]).astype(np.float32)


def _swa_prompt_kernel(sink_ref, q_ref, ktp_ref, kto_ref, vp_ref, vo_ref, bias_ref, hm_ref, o_ref, *, nb):
    w = WINDOW
    key = lax.broadcasted_iota(jnp.int32, (w, 2 * w), 1)
    has_prev = (pl.program_id(0) > 0) | (key >= w)
    heads = [(g, kh) for g in range(SWA_GROUP) for kh in range(SWA_KV_HEADS)]
    kts, vvs = [], []
    for b in range(nb):
        kt_prev = ktp_ref[...] if b == 0 else kto_ref[:, (b - 1) * w:b * w]
        v_prev = vp_ref[...] if b == 0 else vo_ref[(b - 1) * w:b * w, :]
        kts.append(jnp.concatenate([kt_prev, kto_ref[:, b * w:(b + 1) * w]], axis=1).astype(BF16))
        vvs.append(jnp.concatenate([v_prev, vo_ref[b * w:(b + 1) * w, :]], axis=0).astype(BF16))
    scores = [[jnp.dot(q_ref[b * w:(b + 1) * w, g * KV_WIDTH:(g + 1) * KV_WIDTH] * hm_ref[kh].astype(BF16),
                       kts[b], preferred_element_type=F32) for g, kh in heads] for b in range(nb)]
    for b in range(nb):
        probs = []
        for (g, kh), s in zip(heads, scores[b]):
            h = kh * SWA_GROUP + g
            s = s + bias_ref[h]
            if b == 0:
                s = jnp.where(has_prev, s, NEG_BIG)
            probs.append(_softmax_with_sink(s, sink_ref[h]))
        outs = [jnp.dot(p, vvs[b], preferred_element_type=F32) for p in probs]
        for g in range(SWA_GROUP):
            acc = None
            for (cg, kh), o in zip(heads, outs):
                if cg == g:
                    t = o * hm_ref[kh]
                    acc = t if acc is None else acc + t
            o_ref[b * w:(b + 1) * w, g * KV_WIDTH:(g + 1) * KV_WIDTH] = acc.astype(BF16)


def _swa_prompt(q, kt, v, sinks, *, n_rows, nb):
    w = WINDOW
    step = nb * w
    assert n_rows % step == 0
    prev = lambda j, sink: jnp.maximum(j * nb - 1, 0)
    return pl.pallas_call(
        functools.partial(_swa_prompt_kernel, nb=nb),
        grid_spec=pltpu.PrefetchScalarGridSpec(
            num_scalar_prefetch=1,
            grid=(n_rows // step,),
            in_specs=[pl.BlockSpec((step, MAIN_WIDTH), lambda j, sink: (j, 0)),
                      pl.BlockSpec((KV_WIDTH, w), lambda j, sink: (0, prev(j, sink))),
                      pl.BlockSpec((KV_WIDTH, step), lambda j, sink: (0, j)),
                      pl.BlockSpec((w, KV_WIDTH), lambda j, sink: (prev(j, sink), 0)),
                      pl.BlockSpec((step, KV_WIDTH), lambda j, sink: (j, 0)),
                      pl.BlockSpec((SWA_HEADS, w, 2 * w), lambda j, sink: (0, 0, 0)),
                      pl.BlockSpec((SWA_KV_HEADS, 1, KV_WIDTH), lambda j, sink: (0, 0, 0))],
            out_specs=pl.BlockSpec((step, MAIN_WIDTH), lambda j, sink: (j, 0)),
        ),
        out_shape=jax.ShapeDtypeStruct((n_rows, MAIN_WIDTH), BF16),
        compiler_params=_params(("arbitrary",)),
        name="swa_prompt",
    )(sinks.astype(F32), q, kt, kt, v, v, jnp.asarray(_swa_bias(w)), _head_masks(SWA_KV_HEADS))


def _swa_sample_kernel(q_ref, kp_ref, ko_ref, vp_ref, vo_ref, bias_ref, sink_ref, hm_ref, o_ref, *, nb, tq):
    w = WINDOW
    heads = [(kh, g) for kh in range(SWA_KV_HEADS) for g in range(SWA_GROUP)]
    kks, vvs, scores = [], [], []
    q = q_ref[...].astype(F32)
    for i in range(nb):
        win = slice(i * KV_WIDTH, (i + 1) * KV_WIDTH)
        kks.append(jnp.concatenate([kp_ref[win, :].T, ko_ref[i * tq:(i + 1) * tq, :]], axis=0))
        vvs.append(jnp.concatenate([vp_ref[win, :].T, vo_ref[i * tq:(i + 1) * tq, :]], axis=0))
        qs = jnp.concatenate([q[i * tq:(i + 1) * tq, g * KV_WIDTH:(g + 1) * KV_WIDTH] * hm_ref[kh]
                              for kh, g in heads], axis=0)
        scores.append(_bdot_nt(qs, kks[i]))
    probs = [_softmax_with_sink(s + bias_ref[...], sink_ref[...]) for s in scores]
    outs = [_bdot(p, vv) for p, vv in zip(probs, vvs)]
    for g in range(SWA_GROUP):
        rows = []
        for i in range(nb):
            acc = None
            for r, (kh, hg) in enumerate(heads):
                if hg == g:
                    t = outs[i][r * tq:(r + 1) * tq] * hm_ref[kh]
                    acc = t if acc is None else acc + t
            rows.append(acc)
        o_ref[:, g * KV_WIDTH:(g + 1) * KV_WIDTH] = jnp.concatenate(rows, axis=0).astype(BF16)


def _swa_sample(q, k_win, v_win, k, v, sinks, *, row_off, batch, tq, nb):
    w = WINDOW
    assert batch % nb == 0 and row_off % (nb * tq) == 0
    off = row_off // (nb * tq)
    bias = jnp.asarray(_swa_bias(tq).reshape(SWA_HEADS * tq, w + tq))
    sink_col = jnp.repeat(sinks.astype(F32), tq).reshape(SWA_HEADS * tq, 1)
    own = lambda width: pl.BlockSpec((nb * tq, width), lambda b: (off + b, 0))
    win = pl.BlockSpec((nb * KV_WIDTH, w), lambda b: (b, 0))
    return pl.pallas_call(
        functools.partial(_swa_sample_kernel, nb=nb, tq=tq),
        grid=(batch // nb,),
        in_specs=[own(MAIN_WIDTH), win, own(KV_WIDTH), win, own(KV_WIDTH), _const(bias.shape),
                  _const(sink_col.shape), _const((SWA_KV_HEADS, 1, KV_WIDTH))],
        out_specs=pl.BlockSpec((nb * tq, MAIN_WIDTH), lambda b: (b, 0)),
        out_shape=jax.ShapeDtypeStruct((batch * tq, MAIN_WIDTH), BF16),
        compiler_params=_params(("arbitrary",)),
        name="swa_sample",
    )(q, k_win, k, v_win, v, bias, sink_col, _head_masks(SWA_KV_HEADS))


def kernel(x_prompt, x_sample, state_gla, cache_win_k, cache_win_v, cache_mem_k, cache_mem_v, mem_prompt, norm_mix_g, norm_ffn_g, norm_mem_g, w_mem_kv, mem_qn_g, mem_kn_g, w_out, w_in_a, w_gate_lr, b_gate_lr, gla_norm_g, w_in_b, swa_qn_g, swa_sinks, norm_kv_g, w_kv, swa_kn_g, w_router_group, b_router_group, w_router_expert, b_router_expert, w_exp_gate, w_exp_up, w_exp_down):
    bp, tp, d = x_prompt.shape
    bs, ts, _ = x_sample.shape
    assert bp == 1 and tp % WINDOW == 0 and ts * (GLA_CHUNK // ts) == GLA_CHUNK
    n_p, n_s = bp * tp, bs * ts
    w_buf = cache_win_k.shape[1]
    assert w_buf == WINDOW
    x_p, x_s = x_prompt.reshape(n_p, d), x_sample.reshape(n_s, d)

    mem_k_p, mem_v_p = _mem_kv(mem_prompt, norm_mem_g, w_mem_kv, mem_kn_g)
    feature_major = lambda c: jnp.moveaxis(c, -3, -1).reshape(*c.shape[:-3], c.shape[-2] * c.shape[-1], c.shape[-3])
    cmk, cmv = feature_major(cache_mem_k), feature_major(cache_mem_v)

    def mem_attend(mq, l):
        tm_p = _row_tile(tp, 512)
        mo_p = _mem_attn(mq, mem_k_p, mem_v_p, mem_qn_g[l], row_off=0, seq=tp, tm=tm_p, bb=1, layer=l)
        mo_s = _mem_attn(mq, cmk, cmv, mem_qn_g[l], row_off=n_p, seq=ts, tm=ts, bb=8, layer=l)
        return mo_p, mo_s

    def moe(h, l, split_rows=None):
        return _moe(h, norm_ffn_g[l], w_router_group[l], b_router_group[l], w_router_expert[l],
                    b_router_expert[l], w_exp_gate, w_exp_up, w_exp_down, l, split_rows)

    q, k, la, v, og, mq = _inproj_a(x_p, x_s, norm_mix_g[0], w_in_a[0], w_gate_lr[0], b_gate_lr[0])
    zero_state = jnp.zeros((bp, GLA_HEADS, GLA_DK, GLA_DV), F32)
    n_sub = max(1, min(8, tp // GLA_CHUNK))
    main_p, gla_p = _gla(q, k, la, v, og, zero_state, gla_norm_g[0], row_off=0, seq=tp, n_seg=1, n_sub=n_sub)
    main_s, gla_s = _gla(q, k, la, v, og, state_gla[0], gla_norm_g[0], row_off=n_p, seq=ts,
                         n_seg=GLA_CHUNK // ts, n_sub=1)
    mo_p, mo_s = mem_attend(mq, 0)
    w_o = w_out[0]
    h = _outproj((x_p, x_s), main_p, mo_p, main_s, mo_s, w_o[:MAIN_WIDTH].reshape(GLA_HEADS, GLA_DV, d),
                 w_o[MAIN_WIDTH:])
    h = moe(h, 0)

    q, mq, k_sh, v_sh, kt_sh = _inproj_b(h, norm_kv_g, norm_mix_g[1], w_kv, w_in_b[0], swa_kn_g, swa_qn_g[0])
    ck = feature_major(cache_win_k).reshape(bs * KV_WIDTH, w_buf)
    cv = feature_major(cache_win_v).reshape(bs * KV_WIDTH, w_buf)
    main_p = _swa_prompt(q, kt_sh, v_sh, swa_sinks[0], n_rows=n_p, nb=4)
    main_s = _swa_sample(q, ck, cv, k_sh, v_sh, swa_sinks[0], row_off=n_p, batch=bs, tq=ts, nb=8)
    mo_p, mo_s = mem_attend(mq, 1)
    w_o = w_out[1]
    h = _outproj(h, main_p, mo_p, main_s, mo_s, w_o[:MAIN_WIDTH][_swa_perm()], w_o[MAIN_WIDTH:])
    y_p, y_s = moe(h, 1, split_rows=n_p)

    y_prompt = y_p.reshape(bp, tp, d)
    y_sample = y_s.reshape(bs, ts, d)
    k_new = k_sh[n_p:].reshape(bs, ts, SWA_KV_HEADS, HEAD_DIM)
    v_new = v_sh[n_p:].reshape(bs, ts, SWA_KV_HEADS, HEAD_DIM)
    win_k_s = jnp.concatenate([cache_win_k, k_new], axis=1)[:, -w_buf:]
    win_v_s = jnp.concatenate([cache_win_v, v_new], axis=1)[:, -w_buf:]
    win_k_p = k_sh[n_p - WINDOW:n_p].reshape(bp, WINDOW, SWA_KV_HEADS, HEAD_DIM)
    win_v_p = v_sh[n_p - WINDOW:n_p].reshape(bp, WINDOW, SWA_KV_HEADS, HEAD_DIM)
    token_major = lambda c: jnp.moveaxis(c.reshape(*c.shape[:-2], MEM_HEADS, HEAD_DIM, c.shape[-1]), -1, -3)
    return (y_prompt, y_sample, gla_p[None], gla_s[None], win_k_p, win_v_p, win_k_s, win_v_s,
            token_major(mem_k_p), token_major(mem_v_p))
```

```python
import functools
import math

import numpy as np
import jax
import jax.numpy as jnp
from jax import lax
from jax.experimental import pallas as pl
from jax.experimental.pallas import tpu as pltpu

F32 = jnp.float32
BF16 = jnp.bfloat16

D_MODEL = 1024
MEM_LEN = 256
MEM_HEADS = 4
HEAD_DIM = 64
MEM_WIDTH = MEM_HEADS * HEAD_DIM
MAIN_WIDTH = D_MODEL - MEM_WIDTH
GLA_HEADS = 4
GLA_DV = MAIN_WIDTH // GLA_HEADS
GLA_DK = GLA_DV // 2
GLA_DK_PAD = 128
GLA_KEY_WIDTH = GLA_HEADS * GLA_DK
GLA_KEY_PAD = GLA_HEADS * GLA_DK_PAD
GLA_GATE_RANK = 16
GLA_TAU = 16.0
GLA_CHUNK = 64
SWA_HEADS = MAIN_WIDTH // HEAD_DIM
SWA_KV_HEADS = 4
SWA_GROUP = SWA_HEADS // SWA_KV_HEADS
KV_WIDTH = SWA_KV_HEADS * HEAD_DIM
WINDOW = 128
N_GROUPS = 4
EXPERTS_PER_GROUP = 8
N_EXPERTS = N_GROUPS * EXPERTS_PER_GROUP
D_EXPERT = 512
EPS = 1e-6
LANES = 128
NEG_BIG = -1e30
VMEM_LIMIT = 56 * 1024 * 1024
MOE_TILE = 512
MOE_SUB = 128
ROUTER_LANE0 = N_GROUPS
ROUTER_META_ROWS = 8
SLABS = D_MODEL // LANES
PACKED_SLABS = SLABS // 2


def _bdot(a, b):
    return jnp.dot(a.astype(BF16), b.astype(BF16), preferred_element_type=F32)


def _bdot_nt(a, b):
    return lax.dot_general(a.astype(BF16), b.astype(BF16), (((1,), (1,)), ((), ())),
                           preferred_element_type=F32)


def _bdot_tn(a, b):
    return lax.dot_general(a.astype(BF16), b.astype(BF16), (((0,), (0,)), ((), ())),
                           preferred_element_type=F32)


def _split(x, n):
    parts = []
    for _ in range(n - 1):
        p = x.astype(BF16)
        parts.append(p)
        x = x - p.astype(F32)
    parts.append(x.astype(BF16))
    return parts


def _exact_left_dot(m, x, n=3):
    out = None
    for p in _split(x, n):
        t = jnp.dot(m, p, preferred_element_type=F32)
        out = t if out is None else out + t
    return out


def _seg_mean(x2, bd):
    out = None
    for p in _split(x2, 2):
        t = jnp.dot(p, bd, preferred_element_type=F32)
        out = t if out is None else out + t
    return out


def _rms_scale(x):
    return lax.rsqrt(jnp.mean(x * x, axis=-1, keepdims=True) + EPS)


def _row_tile(n, cap=512):
    t = cap
    while t > 8 and n % t:
        t //= 2
    assert n % t == 0, n
    return t


def _params(sem):
    return pltpu.CompilerParams(dimension_semantics=sem, vmem_limit_bytes=VMEM_LIMIT)


def _const(shape):
    nd = len(shape)
    return pl.BlockSpec(shape, lambda *_: (0,) * nd)


def _group_specs(tm, width, prompt_tiles, lead=None):
    p_idx = lambda i, *_: jnp.minimum(i, prompt_tiles - 1)
    s_idx = lambda i, *_: jnp.maximum(i - prompt_tiles, 0)
    if lead is None:
        return [pl.BlockSpec((tm, width), lambda i, *_, f=f: (f(i), 0)) for f in (p_idx, s_idx)]
    return [pl.BlockSpec((lead, tm, width), lambda i, *_, f=f: (0, f(i), 0)) for f in (p_idx, s_idx)]


def _block_diag_mean(width):
    i = np.arange(width)
    return jnp.asarray((i[:, None] // HEAD_DIM == i[None, :] // HEAD_DIM) / HEAD_DIM, BF16)


def _head_masks(n_heads):
    i = np.arange(n_heads * HEAD_DIM)
    return jnp.asarray((i[None, :] // HEAD_DIM == np.arange(n_heads)[:, None]), F32)[:, None, :]


def _mem_kv_kernel(mem_ref, g_ref, w_ref, kng_ref, bd_ref, k_ref, v_ref):
    x = mem_ref[0]
    hn = x * _rms_scale(x) * g_ref[0]
    kv = _bdot(hn, w_ref[0])
    k = kv[:, :MEM_WIDTH]
    k = k * lax.rsqrt(_seg_mean(k * k, bd_ref[...]) + EPS) * kng_ref[0]
    k_ref[0, 0] = k.T
    v_ref[0, 0] = kv[:, MEM_WIDTH:].T


def _mem_kv(mem, g, w, kng):
    depth, (b, m, d) = w.shape[0], mem.shape
    out = jax.ShapeDtypeStruct((depth, b, m, MEM_WIDTH), F32)
    blk = pl.BlockSpec((1, 1, m, MEM_WIDTH), lambda l, i: (l, i, 0, 0))
    return pl.pallas_call(
        _mem_kv_kernel,
        grid=(depth, b),
        in_specs=[pl.BlockSpec((1, m, d), lambda l, i: (i, 0, 0)),
                  pl.BlockSpec((1, 1, d), lambda l, i: (l, 0, 0)),
                  pl.BlockSpec((1, d, 2 * MEM_WIDTH), lambda l, i: (l, 0, 0)),
                  pl.BlockSpec((1, 1, MEM_WIDTH), lambda l, i: (l, 0, 0)),
                  _const((MEM_WIDTH, MEM_WIDTH))],
        out_specs=[blk, blk],
        out_shape=[out, out],
        compiler_params=_params(("arbitrary", "arbitrary")),
        name="mem_kv",
    )(mem, g.reshape(depth, 1, d), w.astype(BF16),
      jnp.tile(kng, (1, MEM_HEADS)).reshape(depth, 1, MEM_WIDTH), _block_diag_mean(MEM_WIDTH))


def _inproj_a_kernel(xp_ref, xs_ref, g_ref, wq_ref, wk_ref, wv_ref, wog_ref, wlr_ref, wmq_ref, wgl_ref, bgl_ref,
                     q_ref, k_ref, la_ref, v_ref, og_ref, mq_ref, *, prompt_tiles):
    x = jnp.where(pl.program_id(0) < prompt_tiles, xp_ref[...], xs_ref[...])
    hn = (x * _rms_scale(x) * g_ref[...]).astype(BF16)
    q_ref[...] = jnp.dot(hn, wq_ref[...], preferred_element_type=F32) * (GLA_DK ** -0.5)
    k_ref[...] = jnp.dot(hn, wk_ref[...], preferred_element_type=F32)
    for h in range(GLA_HEADS):
        v_ref[h] = jnp.dot(hn, wv_ref[h], preferred_element_type=F32).astype(BF16)
        og_ref[h] = jnp.dot(hn, wog_ref[h], preferred_element_type=F32)
    lr = jnp.dot(hn, wlr_ref[...], preferred_element_type=F32)
    z = _bdot(lr, wgl_ref[...]) + bgl_ref[...]
    la_ref[...] = (jnp.minimum(z, 0.0) - jnp.log(1.0 + jnp.exp(-jnp.abs(z)))) * (1.0 / GLA_TAU)
    mq_ref[...] = jnp.dot(hn, wmq_ref[...], preferred_element_type=F32)


def _pad_heads(w, width, pad):
    lead = w.shape[:-1]
    w = w.reshape(*lead, GLA_HEADS, width)
    w = jnp.pad(w, [(0, 0)] * len(lead) + [(0, 0), (0, pad - width)])
    return w.reshape(*lead, GLA_HEADS * pad)


def _inproj_a(x_p, x_s, g, w_in, w_lr, b_lr):
    (n_p, d), n_s = x_p.shape, x_s.shape[0]
    n = n_p + n_s
    tm = _row_tile(n_s)
    assert n_p % tm == 0
    pt = n_p // tm
    c0, c1, c2, c3, c4 = (GLA_KEY_WIDTH, 2 * GLA_KEY_WIDTH, 2 * GLA_KEY_WIDTH + MAIN_WIDTH,
                          2 * GLA_KEY_WIDTH + 2 * MAIN_WIDTH,
                          2 * GLA_KEY_WIDTH + 2 * MAIN_WIDTH + GLA_GATE_RANK)
    wb = w_in.astype(BF16)
    wq = _pad_heads(wb[:, :c0], GLA_DK, GLA_DK_PAD)
    wk = _pad_heads(wb[:, c0:c1], GLA_DK, GLA_DK_PAD)
    wv = wb[:, c1:c2].reshape(d, GLA_HEADS, GLA_DV).transpose(1, 0, 2)
    wog = wb[:, c2:c3].reshape(d, GLA_HEADS, GLA_DV).transpose(1, 0, 2)
    wlr = jnp.pad(wb[:, c3:c4], ((0, 0), (0, LANES - GLA_GATE_RANK)))
    wmq = wb[:, c4:]
    wgl = jnp.pad(_pad_heads(w_lr.astype(BF16), GLA_DK, GLA_DK_PAD), ((0, LANES - GLA_GATE_RANK), (0, 0)))
    bgl = _pad_heads(b_lr.reshape(1, -1), GLA_DK, GLA_DK_PAD)
    row = lambda w: pl.BlockSpec((tm, w), lambda i: (i, 0))
    hrow = pl.BlockSpec((GLA_HEADS, tm, GLA_DV), lambda i: (0, i, 0))
    key = jax.ShapeDtypeStruct((n, GLA_KEY_PAD), F32)
    val = jax.ShapeDtypeStruct((GLA_HEADS, n, GLA_DV), F32)
    return pl.pallas_call(
        functools.partial(_inproj_a_kernel, prompt_tiles=pt),
        grid=(n // tm,),
        in_specs=_group_specs(tm, d, pt) + [
            _const((1, d)), _const(wq.shape), _const(wk.shape), _const(wv.shape),
            _const(wog.shape), _const(wlr.shape), _const(wmq.shape), _const(wgl.shape),
            _const(bgl.shape)],
        out_specs=[row(GLA_KEY_PAD), row(GLA_KEY_PAD), row(GLA_KEY_PAD), hrow, hrow, row(MEM_WIDTH)],
        out_shape=[key, key, key, jax.ShapeDtypeStruct(val.shape, BF16), val,
                   jax.ShapeDtypeStruct((n, MEM_WIDTH), F32)],
        compiler_params=_params(("parallel",)),
        name="inproj_a",
    )(x_p, x_s, g.reshape(1, d), wq, wk, wv, wog, wlr, wmq, wgl, bgl)


def _gla_kernel(q_ref, k_ref, la_ref, v_ref, og_ref, s0_ref, gn_ref, mcum_ref, mall_ref, sel_ref,
                o_ref, sout_ref, s_ref, *, chunk, n_sub, n_seg):
    j = pl.program_id(1)
    seg = chunk // n_seg

    @pl.when(j == 0)
    def _():
        s_ref[...] = jnp.zeros_like(s_ref)
        s_ref[:, :, :GLA_DK, :] = s0_ref[...]

    mcum = mcum_ref[...]
    causal = mcum.astype(F32) > 0.0
    row = lax.broadcasted_iota(jnp.int32, (chunk, GLA_DK_PAD), 0)
    gn = gn_ref[...]
    hcols = [slice(h * GLA_DK_PAD, (h + 1) * GLA_DK_PAD) for h in range(GLA_HEADS)]
    crows = [slice(c * chunk, (c + 1) * chunk) for c in range(n_sub)]
    qts, kts, kds, e_ends = [], [], [], []
    for rows in crows:
        la = la_ref[rows, :]
        b = _exact_left_dot(mcum, la)
        b_end = _exact_left_dot(mall_ref[...], la)
        e_ends.append(jnp.exp(_exact_left_dot(sel_ref[...], la)).T)
        k = k_ref[rows, :]
        qts.append(q_ref[rows, :] * jnp.exp(b))
        kts.append((k * jnp.exp(-b)).astype(BF16))
        kds.append(k * jnp.exp(b_end - b))
    vbs = [[v_ref[h, rows, :].astype(BF16) for h in range(GLA_HEADS)] for rows in crows]
    scores = [[_bdot_nt(qts[c][:, cols], kts[c][:, cols]) for cols in hcols] for c in range(n_sub)]
    kvs = []
    for c in range(n_sub):
        per_head = []
        for h, cols in enumerate(hcols):
            per_seg = []
            for s in range(n_seg):
                kd = kds[c][:, cols]
                if n_seg > 1:
                    kd = jnp.where((row >= s * seg) & (row < (s + 1) * seg), kd, 0.0)
                per_seg.append(_bdot_tn(kd, vbs[c][h]))
            per_head.append(per_seg)
        kvs.append(per_head)
    state = [[s_ref[s, h] for s in range(n_seg)] for h in range(GLA_HEADS)]
    inters = []
    for c in range(n_sub):
        per_head = []
        for h, cols in enumerate(hcols):
            parts = []
            for s in range(n_seg):
                parts.append(_bdot(qts[c][s * seg:(s + 1) * seg, cols], state[h][s]))
                state[h][s] = e_ends[c][cols, s:s + 1] * state[h][s] + kvs[c][h][s]
            per_head.append(parts[0] if n_seg == 1 else jnp.concatenate(parts, axis=0))
        inters.append(per_head)
    for h in range(GLA_HEADS):
        for s in range(n_seg):
            s_ref[s, h] = state[h][s]
    for c, rows in enumerate(crows):
        for h in range(GLA_HEADS):
            a = jnp.where(causal, scores[c][h], 0.0)
            o = _bdot(a, vbs[c][h]) + inters[c][h]
            on = o * lax.rsqrt(jnp.mean(o * o, axis=-1, keepdims=True) + EPS) * gn
            og = og_ref[h, rows, :]
            o_ref[h, rows, :] = (on * (og * jax.nn.sigmoid(og))).astype(BF16)

    @pl.when(j == pl.num_programs(1) - 1)
    def _():
        sout_ref[...] = s_ref[:, :, :GLA_DK, :]


def _gla(q, k, la, v, og, s0, gnorm, *, row_off, seq, n_seg, n_sub):
    batch = s0.shape[0]
    chunk = GLA_CHUNK
    assert chunk % n_seg == 0 and batch % n_seg == 0
    seg = chunk // n_seg
    step_rows = n_sub * chunk
    if n_seg > 1:
        assert seq == seg and n_sub == 1
        t_steps = 1
    else:
        assert seq % step_rows == 0
        t_steps = seq // step_rows
    assert row_off % step_rows == 0
    off = row_off // step_rows
    i = np.arange(chunk)
    same = (i[:, None] // seg) == (i[None, :] // seg)
    mcum = jnp.asarray(same & (i[None, :] <= i[:, None]), BF16)
    mall = jnp.asarray(same, BF16)
    sel = jnp.asarray((i[None, :] // seg) == np.arange(LANES)[:, None], BF16)
    ridx = lambda g, j: (off + g * t_steps + j, 0)
    hidx = lambda g, j: (0, off + g * t_steps + j, 0)
    key_spec = pl.BlockSpec((step_rows, GLA_KEY_PAD), ridx)
    val_spec = pl.BlockSpec((GLA_HEADS, step_rows, GLA_DV), hidx)
    st_spec = pl.BlockSpec((n_seg, GLA_HEADS, GLA_DK, GLA_DV), lambda g, j: (g, 0, 0, 0))
    in_specs = [key_spec, key_spec, key_spec, val_spec, val_spec, st_spec, _const((1, GLA_DV)),
                _const((chunk, chunk)), _const((chunk, chunk)), _const((LANES, chunk))]
    args = [q, k, la, v, og, s0, gnorm.reshape(1, GLA_DV), mcum, mall, sel]
    out_spec = pl.BlockSpec((GLA_HEADS, step_rows, GLA_DV), lambda g, j: (0, g * t_steps + j, 0))
    return pl.pallas_call(
        functools.partial(_gla_kernel, chunk=chunk, n_sub=n_sub, n_seg=n_seg),
        grid=(batch // n_seg, t_steps),
        in_specs=in_specs,
        out_specs=[out_spec, st_spec],
        out_shape=[jax.ShapeDtypeStruct((GLA_HEADS, batch * seq, GLA_DV), BF16),
                   jax.ShapeDtypeStruct(s0.shape, F32)],
        scratch_shapes=[pltpu.VMEM((n_seg, GLA_HEADS, GLA_DK_PAD, GLA_DV), F32)],
        compiler_params=_params(("arbitrary", "arbitrary")),
        name="gla",
    )(*args)


def _mem_attn_kernel(q_ref, k_ref, v_ref, g_ref, bd_ref, hm_ref, o_ref, *, tm, bb):
    g = g_ref[...]
    sub = min(tm, 128)
    units = [(i, i * tm + r) for i in range(bb) for r in range(0, tm, sub)]
    scores = []
    for i, r in units:
        q = q_ref[r:r + sub, :]
        qn = q * lax.rsqrt(_seg_mean(q * q, bd_ref[...]) + EPS) * g
        qs = jnp.concatenate([(qn * hm_ref[h]).astype(BF16) for h in range(MEM_HEADS)], axis=0)
        scores.append(_bdot(qs, k_ref[i]))
    probs = []
    for s in scores:
        e = jnp.exp(s - jnp.max(s, axis=-1, keepdims=True))
        probs.append(e * (1.0 / jnp.sum(e, axis=-1, keepdims=True)))
    outs = [_bdot_nt(p, v_ref[i]) for (i, _), p in zip(units, probs)]
    rows = []
    for o in outs:
        acc = o[:sub] * hm_ref[0]
        for h in range(1, MEM_HEADS):
            acc = acc + o[h * sub:(h + 1) * sub] * hm_ref[h]
        rows.append(acc)
    o_ref[...] = jnp.concatenate(rows, axis=0).astype(BF16)


def _mem_attn(mq, mk, mv, qng, *, row_off, seq, tm, bb, layer):
    depth, batch, m, _ = mk.shape
    mk = mk.reshape(depth * batch, m, MEM_WIDTH)
    mv = mv.reshape(depth * batch, m, MEM_WIDTH)
    kv_off = layer * batch // bb
    assert seq % tm == 0 and batch % bb == 0 and (bb == 1 or seq == tm)
    t_steps = seq // tm
    step_rows = bb * tm
    assert row_off % step_rows == 0
    off = row_off // step_rows
    row_spec = pl.BlockSpec((step_rows, MEM_WIDTH), lambda g, j: (off + g * t_steps + j, 0))
    kv_spec = pl.BlockSpec((bb, m, MEM_WIDTH), lambda g, j: (kv_off + g, 0, 0))
    in_specs = [row_spec, kv_spec, kv_spec, _const((1, MEM_WIDTH)), _const((MEM_WIDTH, MEM_WIDTH)),
                _const((MEM_HEADS, 1, MEM_WIDTH))]
    args = [mq, mk, mv, (jnp.tile(qng, MEM_HEADS) * HEAD_DIM ** -0.5).reshape(1, MEM_WIDTH),
            _block_diag_mean(MEM_WIDTH), _head_masks(MEM_HEADS)]
    return pl.pallas_call(
        functools.partial(_mem_attn_kernel, tm=tm, bb=bb),
        grid=(batch // bb, t_steps),
        in_specs=in_specs,
        out_specs=pl.BlockSpec((step_rows, MEM_WIDTH), lambda g, j: (g * t_steps + j, 0)),
        out_shape=jax.ShapeDtypeStruct((batch * seq, MEM_WIDTH), BF16),
        compiler_params=_params(("parallel", "parallel")),
        name="mem_attn",
    )(*args)


def _outproj_kernel(*refs, heads, prompt_tiles, split_residual):
    if split_residual:
        hp_ref, hs_ref = refs[:2]
        refs = refs[2:]
    else:
        hp_ref = hs_ref = refs[0]
        refs = refs[1:]
    main_p_ref, main_s_ref, mo_p_ref, mo_s_ref, wmain_ref, wmo_ref, o_ref = refs

    def project(h_ref, main_ref, mo_ref):
        acc = h_ref[...] + _bdot(mo_ref[...], wmo_ref[...])
        if heads:
            for h in range(heads):
                acc = acc + _bdot(main_ref[h], wmain_ref[h])
        else:
            acc = acc + _bdot(main_ref[...], wmain_ref[...])
        o_ref[...] = acc

    @pl.when(pl.program_id(0) < prompt_tiles)
    def _():
        project(hp_ref, main_p_ref, mo_p_ref)

    @pl.when(pl.program_id(0) >= prompt_tiles)
    def _():
        project(hs_ref, main_s_ref, mo_s_ref)


def _outproj(h, main_p, mo_p, main_s, mo_s, w_main, w_mo):
    n_p, n_s = mo_p.shape[0], mo_s.shape[0]
    n, d = n_p + n_s, w_mo.shape[1]
    tm = _row_tile(n_s)
    assert n_p % tm == 0
    pt = n_p // tm
    heads = main_p.shape[0] if main_p.ndim == 3 else 0
    row = pl.BlockSpec((tm, d), lambda i: (i, 0))
    split = isinstance(h, tuple)
    h_specs, h_args = (_group_specs(tm, d, pt), list(h)) if split else ([row], [h])
    main_specs = _group_specs(tm, main_p.shape[-1], pt, lead=heads or None)
    return pl.pallas_call(
        functools.partial(_outproj_kernel, heads=heads, prompt_tiles=pt, split_residual=split),
        grid=(n // tm,),
        in_specs=h_specs + main_specs + _group_specs(tm, MEM_WIDTH, pt) + [_const(w_main.shape),
                                                                           _const(w_mo.shape)],
        out_specs=row,
        out_shape=jax.ShapeDtypeStruct((n, d), F32),
        compiler_params=_params(("parallel",)),
        name="outproj",
    )(*h_args, main_p, main_s, mo_p, mo_s, w_main.astype(BF16), w_mo.astype(BF16))


def _router_kernel(h_ref, g_ref, whi_ref, wlo_ref, b_ref, tril_ref, before_ref,
                   mi_ref, mf_ref, cnt_ref, tt_ref, carry_ref):
    i = pl.program_id(0)

    @pl.when(i == 0)
    def _():
        carry_ref[...] = jnp.zeros_like(carry_ref)

    x = h_ref[...]
    xn = x * _rms_scale(x) * g_ref[...]
    x_hi, x_lo = _split(xn, 2)
    logits = (jnp.dot(x_hi, whi_ref[...], preferred_element_type=F32)
              + jnp.dot(x_hi, wlo_ref[...], preferred_element_type=F32)
              + jnp.dot(x_lo, whi_ref[...], preferred_element_type=F32)) + b_ref[...]
    tm = x.shape[0]
    lane = lax.broadcasted_iota(jnp.int32, (tm, LANES), 1)
    far = jnp.int32(2 * LANES)

    def first_max(vals):
        m = jnp.max(vals, axis=-1, keepdims=True)
        return m, jnp.min(jnp.where(vals == m, lane, far), axis=-1, keepdims=True)

    gl = jnp.where(lane < N_GROUPS, logits, -jnp.inf)
    gmax, grp = first_max(gl)
    pg_sel = 1.0 / jnp.sum(jnp.exp(gl - gmax), axis=-1, keepdims=True)
    lo = ROUTER_LANE0 + grp * EXPERTS_PER_GROUP
    el = jnp.where((lane >= lo) & (lane < lo + EXPERTS_PER_GROUP), logits, -jnp.inf)
    m1, i1 = first_max(el)
    m2, i2 = first_max(jnp.where(lane == i1, -jnp.inf, el))
    e2 = jnp.exp(m2 - m1)
    g1 = pg_sel / (1.0 + e2)
    g2 = pg_sel * e2 / (1.0 + e2)

    oh1 = lane == i1
    oh2 = lane == i2
    picked = jnp.where(oh1 | oh2, 1.0, 0.0)
    earlier = jnp.dot(tril_ref[...], picked.astype(BF16), preferred_element_type=F32)
    cnt_tile = jnp.sum(picked, axis=0, keepdims=True)
    c8 = jnp.broadcast_to(cnt_tile, (8, LANES))
    c_hi = jnp.floor(c8 * (1.0 / 32.0))
    c_lo = c8 - 32.0 * c_hi
    first = (32.0 * jnp.dot(c_hi.astype(BF16), before_ref[...], preferred_element_type=F32)
             + jnp.dot(c_lo.astype(BF16), before_ref[...], preferred_element_type=F32))[0:1]
    local = first + earlier
    lpos1 = jnp.sum(jnp.where(oh1, local, 0.0), axis=-1, keepdims=True)
    lpos2 = jnp.sum(jnp.where(oh2, local, 0.0), axis=-1, keepdims=True)
    carry_before = carry_ref[...]
    carry = carry_before + cnt_tile
    carry_ref[...] = carry
    cnt_ref[...] = carry

    row8 = lax.broadcasted_iota(jnp.int32, (8, LANES), 0)
    zero8 = jnp.zeros((8, LANES), F32)
    table = jnp.where(row8 == 0, carry_before, jnp.where(row8 == 1, cnt_tile, jnp.where(row8 == 2, first, zero8)))
    tt_ref[...] = table.astype(jnp.int32)
    zi = jnp.zeros((tm, LANES), jnp.int32)
    mi = jnp.where(lane == 0, lpos1.astype(jnp.int32), jnp.where(lane == 1, lpos2.astype(jnp.int32), zi))
    mi_ref[...] = mi.T[:ROUTER_META_ROWS]
    zf = jnp.zeros((tm, LANES), F32)
    mf_ref[...] = jnp.where(lane == 0, g1, jnp.where(lane == 1, g2,
                            jnp.where(lane == 2, lpos1, jnp.where(lane == 3, lpos2, zf))))


def _router(h, g, w_rg, b_rg, w_re, b_re, tm):
    n, d = h.shape
    assert n % tm == 0 and 2 * tm <= 32 * 32
    n_real = N_GROUPS + N_EXPERTS
    w = jnp.pad(jnp.concatenate([w_rg, w_re], axis=1), ((0, 0), (0, LANES - n_real)))
    b = jnp.pad(jnp.concatenate([b_rg, b_re]), (0, LANES - n_real)).reshape(1, LANES)
    w_hi = w.astype(BF16)
    w_lo = (w - w_hi.astype(F32)).astype(BF16)
    i = np.arange(tm)
    tril = jnp.asarray(i[None, :] < i[:, None], BF16)
    lane = np.arange(LANES)
    before = jnp.asarray(lane[:, None] < lane[None, :], BF16)
    row = lambda width: pl.BlockSpec((tm, width), lambda i: (i, 0))
    return pl.pallas_call(
        _router_kernel,
        grid=(n // tm,),
        in_specs=[row(d), _const((1, d)), _const((d, LANES)), _const((d, LANES)), _const((1, LANES)),
                  _const((tm, tm)), _const((LANES, LANES))],
        out_specs=[pl.BlockSpec((ROUTER_META_ROWS, tm), lambda i: (0, i)), row(LANES), _const((1, LANES)),
                   pl.BlockSpec((8, LANES), lambda i: (i, 0))],
        out_shape=[jax.ShapeDtypeStruct((ROUTER_META_ROWS, n), jnp.int32),
                   jax.ShapeDtypeStruct((n, LANES), F32), jax.ShapeDtypeStruct((1, LANES), F32),
                   jax.ShapeDtypeStruct((n // tm * 8, LANES), jnp.int32)],
        scratch_shapes=[pltpu.VMEM((1, LANES), F32)],
        compiler_params=_params(("arbitrary",)),
        name="moe_router",
    )(h, g.reshape(1, d), w_hi, w_lo, b, tril, before)


def _to_rows(ref, rows, lead=()):
    return jnp.concatenate([ref[lead + (pl.ds(s, rows, stride=SLABS), slice(None))] for s in range(SLABS)],
                           axis=1)


def _from_rows(ref, x, rows, lead=()):
    for s in range(SLABS):
        ref[lead + (pl.ds(s, rows, stride=SLABS), slice(None))] = x[:, s * LANES:(s + 1) * LANES]


def _pack_rows(ref, x, rows, lead=(), row0=0):
    u32 = jnp.uint32
    for w in range(PACKED_SLABS):
        lo = x[:, (2 * w) * LANES:(2 * w + 1) * LANES].astype(BF16).astype(F32)
        hi = x[:, (2 * w + 1) * LANES:(2 * w + 2) * LANES].astype(BF16).astype(F32)
        word = (lax.bitcast_convert_type(lo, u32) >> 16) | (lax.bitcast_convert_type(hi, u32) & u32(0xFFFF0000))
        ref[lead + (pl.ds(row0 * PACKED_SLABS + w, rows, stride=PACKED_SLABS), slice(None))] = word


def _unpack_rows(ref, rows, lead=(), row0=0):
    u32 = jnp.uint32
    slabs = []
    for w in range(PACKED_SLABS):
        word = ref[lead + (pl.ds(row0 * PACKED_SLABS + w, rows, stride=PACKED_SLABS), slice(None))]
        slabs.append(lax.bitcast_convert_type(word << 16, F32).astype(BF16))
        slabs.append(lax.bitcast_convert_type(word & u32(0xFFFF0000), F32).astype(BF16))
    return jnp.concatenate(slabs, axis=1)


RUN_FIELDS = 3
RUN_CHUNK_BITS = 6


def _copy_runs(runs_ref, tile, local_rows, global_rows, sem, *, to_global):
    ps = PACKED_SLABS
    base = tile * (RUN_FIELDS * N_EXPERTS)

    def piece(g0, l0, off, size):
        g = global_rows(pl.multiple_of((g0 + off) * ps, ps), size * ps)
        l = local_rows(pl.multiple_of((l0 + off) * ps, ps), size * ps)
        src, dst = (l, g) if to_global else (g, l)
        pltpu.make_async_copy(src, dst, sem).start()

    def per_expert(e, carry):
        g0 = runs_ref[base + e]
        length = runs_ref[base + N_EXPERTS + e]
        l0 = runs_ref[base + 2 * N_EXPERTS + e]
        big = 1 << RUN_CHUNK_BITS

        def big_piece(c, inner):
            piece(g0, l0, c * big, big)
            return inner

        n_big = length >> RUN_CHUNK_BITS
        lax.fori_loop(0, n_big, big_piece, 0)
        off = n_big * big
        for bit in reversed(range(RUN_CHUNK_BITS)):
            size = 1 << bit

            @pl.when((length & size) != 0)
            def _(off=off, size=size):
                piece(g0, l0, off, size)

            off = off + (length & size)
        return carry

    lax.fori_loop(0, N_EXPERTS, per_expert, 0)


def _dispatch_kernel(runs_ref, h_ref, g_ref, meta_ref, xs_hbm, buf, sem, *, tm, steps):
    i = pl.program_id(0)
    slot = lax.rem(i, 2)
    ns = 2 * tm

    def wait_slot(sl):
        pltpu.make_async_copy(buf.at[sl], xs_hbm.at[pl.ds(0, ns * PACKED_SLABS)], sem.at[sl]).wait()

    @pl.when(i >= 2)
    def _():
        wait_slot(slot)

    x = h_ref[...]
    xn = (x * _rms_scale(x) * g_ref[...]).astype(BF16)
    j = lax.broadcasted_iota(jnp.int32, (ns, tm), 0)
    pick = jnp.where((j == meta_ref[0:1, :]) | (j == meta_ref[1:2, :]), 1.0, 0.0).astype(BF16)
    _pack_rows(buf, jnp.dot(pick, xn, preferred_element_type=F32), ns, (slot,))
    _copy_runs(runs_ref, i, lambda start, size: buf.at[slot, pl.ds(start, size)],
               lambda start, size: xs_hbm.at[pl.ds(start, size)], sem.at[slot], to_global=True)

    @pl.when(i == steps - 1)
    def _():
        wait_slot(slot)
        if steps > 1:
            wait_slot(1 - slot)


def _dispatch(h, g, meta, runs, tm):
    n, d = h.shape
    steps = n // tm
    return pl.pallas_call(
        functools.partial(_dispatch_kernel, tm=tm, steps=steps),
        grid_spec=pltpu.PrefetchScalarGridSpec(
            num_scalar_prefetch=1,
            grid=(steps,),
            in_specs=[pl.BlockSpec((tm, d), lambda i, runs: (i, 0)),
                      pl.BlockSpec((1, d), lambda i, runs: (0, 0)),
                      pl.BlockSpec((ROUTER_META_ROWS, tm), lambda i, runs: (0, i))],
            out_specs=pl.BlockSpec(memory_space=pl.ANY),
            scratch_shapes=[pltpu.VMEM((2, 2 * tm * PACKED_SLABS, LANES), jnp.uint32),
                            pltpu.SemaphoreType.DMA((2,))],
        ),
        out_shape=jax.ShapeDtypeStruct((2 * n * PACKED_SLABS, LANES), jnp.uint32),
        compiler_params=_params(("arbitrary",)),
        name="moe_dispatch",
    )(runs, h, g.reshape(1, d), meta)


ITEM_FIRST, ITEM_LAST, ITEM_NEW_EXPERT = 1, 2, 4


def _expert_kernel(tile_ref, exp_ref, lo_ref, hi_ref, flag_ref, n_ref, xs_ref, wg_ref, wu_ref, wd_ref, ys_ref,
                   wgb, wub, wdb, acc, *, tm, sub):
    w = pl.program_id(0)

    @pl.when(w == 0)
    def _():
        acc[...] = jnp.zeros_like(acc)

    @pl.when(w < n_ref[0])
    def _():
        flags = flag_ref[w]

        @pl.when((flags & ITEM_NEW_EXPERT) != 0)
        def _():
            wgb[...] = wg_ref[0, 0].astype(BF16)
            wub[...] = wu_ref[0, 0].astype(BF16)
            wdb[...] = wd_ref[0, 0].astype(BF16)

        first = (flags & ITEM_FIRST) != 0
        lo, hi = lo_ref[w], hi_ref[w]
        n_blocks = tm // sub

        def up(s):
            x = _unpack_rows(xs_ref, sub, row0=s * sub)
            return (jnp.dot(x, wgb[...], preferred_element_type=F32),
                    jnp.dot(x, wub[...], preferred_element_type=F32))

        ups = {0: up(0)}
        for s in range(n_blocks):
            if s + 1 < n_blocks:
                ups[s + 1] = up(s + 1)
            hg, hu = ups.pop(s)
            act = (hg * jax.nn.sigmoid(hg) * hu).astype(BF16)
            y = jnp.dot(act, wdb[...], preferred_element_type=F32)
            row = lax.broadcasted_iota(jnp.int32, (sub, 1), 0) + s * sub
            y = jnp.where((row >= lo) & (row < hi), y, 0.0)
            rows = slice(s * sub, (s + 1) * sub)
            total = jnp.where(first, y, acc[rows, :] + y)
            acc[rows, :] = total
            _pack_rows(ys_ref, total, sub, row0=s * sub)


def _experts(xs, items, w_g, w_u, w_d, layer):
    tile, expert, lo, hi, flags, n_items = items
    d = D_MODEL
    tm = MOE_TILE
    packed_spec = pl.BlockSpec((tm * PACKED_SLABS, LANES), lambda w, tile, *_: (tile[w], 0))
    wmap = lambda w, tile, expert, *_: (layer, expert[w], 0, 0)
    return pl.pallas_call(
        functools.partial(_expert_kernel, tm=tm, sub=MOE_SUB),
        grid_spec=pltpu.PrefetchScalarGridSpec(
            num_scalar_prefetch=6,
            grid=(tile.shape[0],),
            in_specs=[packed_spec,
                      pl.BlockSpec((1, 1, d, D_EXPERT), wmap),
                      pl.BlockSpec((1, 1, d, D_EXPERT), wmap),
                      pl.BlockSpec((1, 1, D_EXPERT, d), wmap)],
            out_specs=packed_spec,
            scratch_shapes=[pltpu.VMEM((d, D_EXPERT), BF16), pltpu.VMEM((d, D_EXPERT), BF16),
                            pltpu.VMEM((D_EXPERT, d), BF16), pltpu.VMEM((tm, d), F32)],
        ),
        out_shape=jax.ShapeDtypeStruct(xs.shape, jnp.uint32),
        compiler_params=_params(("arbitrary",)),
        name="moe_experts",
    )(tile, expert, lo, hi, flags, n_items, xs, w_g, w_u, w_d)


def _combine_kernel(runs_ref, h_ref, gate_ref, ys_hbm, *refs, tm, steps, prompt_tiles):
    out_refs, (buf, sem) = refs[:-2], refs[-2:]
    i = pl.program_id(0)
    slot = lax.rem(i, 2)
    ns = 2 * tm

    def issue(step, sl):
        _copy_runs(runs_ref, step, lambda start, size: buf.at[sl, pl.ds(start, size)],
                   lambda start, size: ys_hbm.at[pl.ds(start, size)], sem.at[sl], to_global=False)

    @pl.when(i == 0)
    def _():
        issue(0, 0)

    @pl.when(i + 1 < steps)
    def _():
        issue(i + 1, 1 - slot)

    pltpu.make_async_copy(ys_hbm.at[pl.ds(0, ns * PACKED_SLABS)], buf.at[slot], sem.at[slot]).wait()
    y = _unpack_rows(buf, ns, (slot,))
    g = gate_ref[...]
    j = lax.broadcasted_iota(jnp.int32, (tm, ns), 1)
    out = h_ref[...]
    for kk in range(2):
        pick = jnp.where(j == g[:, 2 + kk:3 + kk].astype(jnp.int32), 1.0, 0.0).astype(BF16)
        out = out + g[:, kk:kk + 1] * jnp.dot(pick, y, preferred_element_type=F32)
    if len(out_refs) == 1:
        out_refs[0][...] = out
    else:
        @pl.when(i < prompt_tiles)
        def _():
            out_refs[0][...] = out

        @pl.when(i >= prompt_tiles)
        def _():
            out_refs[1][...] = out


def _combine(h, gates, ys, runs, tm, split_rows=None):
    n, d = h.shape
    steps = n // tm
    row = pl.BlockSpec((tm, d), lambda i, pos: (i, 0))
    if split_rows is None:
        pt, out_specs, out_shape = 0, row, jax.ShapeDtypeStruct((n, d), F32)
    else:
        assert split_rows % tm == 0
        pt = split_rows // tm
        out_specs = _group_specs(tm, d, pt)
        out_shape = [jax.ShapeDtypeStruct((split_rows, d), F32), jax.ShapeDtypeStruct((n - split_rows, d), F32)]
    return pl.pallas_call(
        functools.partial(_combine_kernel, tm=tm, steps=steps, prompt_tiles=pt),
        grid_spec=pltpu.PrefetchScalarGridSpec(
            num_scalar_prefetch=1,
            grid=(steps,),
            in_specs=[row, pl.BlockSpec((tm, LANES), lambda i, pos: (i, 0)),
                      pl.BlockSpec(memory_space=pl.ANY)],
            out_specs=out_specs,
            scratch_shapes=[pltpu.VMEM((2, 2 * tm * PACKED_SLABS, LANES), jnp.uint32),
                            pltpu.SemaphoreType.DMA((2,))],
        ),
        out_shape=out_shape,
        compiler_params=_params(("arbitrary",)),
        name="moe_combine",
    )(runs, h, gates, ys)


def _lookup(tables, idx):
    hit = idx[:, None] == jnp.arange(tables.shape[1], dtype=idx.dtype)[None, :]
    return jnp.sum(jnp.where(hit[None], tables[:, None, :], 0), axis=2)


def _work_items(counts, n_slots):
    tm = MOE_TILE
    assert n_slots % tm == 0, (n_slots, tm)
    ends = jnp.cumsum(counts)
    starts = ends - counts
    first_tile = starts // tm
    n_tiles_e = jnp.where(counts > 0, (ends - 1) // tm - first_tile + 1, 0)
    item_end = jnp.cumsum(n_tiles_e)
    item_start = item_end - n_tiles_e
    n_items = item_end[-1]
    max_items = n_slots // tm + N_EXPERTS - 1
    w = jnp.minimum(jnp.arange(max_items, dtype=jnp.int32), n_items - 1)
    expert = jnp.sum(w[:, None] >= item_end[None, :], axis=1).astype(jnp.int32)
    first_tile_w, item_start_w, start_w, end_w = _lookup(jnp.stack([first_tile, item_start, starts, ends]), expert)
    tile = first_tile_w + w - item_start_w
    lo = jnp.maximum(start_w - tile * tm, 0)
    hi = jnp.minimum(end_w - tile * tm, tm)
    prev_tile = jnp.concatenate([jnp.full((1,), -1, jnp.int32), tile[:-1]])
    next_tile = jnp.concatenate([tile[1:], jnp.full((1,), -1, jnp.int32)])
    prev_expert = jnp.concatenate([jnp.full((1,), -1, jnp.int32), expert[:-1]])
    idx = jnp.arange(max_items, dtype=jnp.int32)
    flags = (jnp.where(tile != prev_tile, ITEM_FIRST, 0)
             | jnp.where((tile != next_tile) | (idx == n_items - 1), ITEM_LAST, 0)
             | jnp.where(expert != prev_expert, ITEM_NEW_EXPERT, 0))
    as_i32 = lambda a: a.astype(jnp.int32)
    return (as_i32(tile), expert, as_i32(lo), as_i32(hi), as_i32(flags), as_i32(n_items).reshape(1))


def _moe(h, g, w_rg, b_rg, w_re, b_re, w_g, w_u, w_d, layer, split_rows=None):
    n, _ = h.shape
    tm = _row_tile(math.gcd(n, split_rows or n), MOE_TILE)
    meta, gates, cnt, tables = _router(h, g, w_rg, b_rg, w_re, b_re, tm)
    experts = slice(ROUTER_LANE0, ROUTER_LANE0 + N_EXPERTS)
    counts = cnt[0, experts].astype(jnp.int32)
    starts = jnp.cumsum(counts) - counts
    tables = tables.reshape(n // tm, 8, LANES)[:, :RUN_FIELDS, experts]
    runs = tables.at[:, 0, :].add(starts[None, :]).reshape(-1)
    xs = _dispatch(h, g, meta, runs, tm)
    ys = _experts(xs, _work_items(counts, 2 * n), w_g, w_u, w_d, layer)
    return _combine(h, gates, ys, runs, tm, split_rows)


def _inproj_b_kernel(x_ref, gkv_ref, gmix_ref, wkv_ref, win_ref, kng_ref, qng_ref, bdk_ref, bdq_ref,
                     q_ref, mq_ref, k_ref, v_ref, kt_ref):
    x = x_ref[...]
    xr = x * _rms_scale(x)
    kv = _bdot(xr * gkv_ref[...], wkv_ref[...])
    k = kv[:, :KV_WIDTH]
    k = k * lax.rsqrt(_seg_mean(k * k, bdk_ref[...]) + EPS) * kng_ref[...]
    k_ref[...] = k
    kt_ref[...] = k.T.astype(BF16)
    v_ref[...] = kv[:, KV_WIDTH:]
    proj = _bdot(xr * gmix_ref[...], win_ref[...])
    q = proj[:, :MAIN_WIDTH]
    q_ref[...] = (q * lax.rsqrt(_seg_mean(q * q, bdq_ref[...]) + EPS) * qng_ref[...]).astype(BF16)
    mq_ref[...] = proj[:, MAIN_WIDTH:]


def _swa_perm():
    g, kh, dd = np.meshgrid(np.arange(SWA_GROUP), np.arange(SWA_KV_HEADS), np.arange(HEAD_DIM), indexing="ij")
    return ((kh * SWA_GROUP + g) * HEAD_DIM + dd).reshape(-1)


def _inproj_b(x, g_kv, g_mix, w_kv, w_in, kng, qng):
    n, d = x.shape
    tm = _row_tile(n)
    perm = _swa_perm()
    w_in_p = jnp.concatenate([w_in[:, :MAIN_WIDTH][:, perm], w_in[:, MAIN_WIDTH:]], axis=1).astype(BF16)
    qng_t = (jnp.tile(qng, SWA_HEADS) * HEAD_DIM ** -0.5).reshape(1, MAIN_WIDTH)
    row = lambda w: pl.BlockSpec((tm, w), lambda i: (i, 0))
    return pl.pallas_call(
        _inproj_b_kernel,
        grid=(n // tm,),
        in_specs=[row(d), _const((1, d)), _const((1, d)), _const((d, 2 * KV_WIDTH)), _const((d, d)),
                  _const((1, KV_WIDTH)), _const((1, MAIN_WIDTH)), _const((KV_WIDTH, KV_WIDTH)),
                  _const((MAIN_WIDTH, MAIN_WIDTH))],
        out_specs=[row(MAIN_WIDTH), row(MEM_WIDTH), row(KV_WIDTH), row(KV_WIDTH),
                   pl.BlockSpec((KV_WIDTH, tm), lambda i: (0, i))],
        out_shape=[jax.ShapeDtypeStruct((n, MAIN_WIDTH), BF16), jax.ShapeDtypeStruct((n, MEM_WIDTH), F32),
                   jax.ShapeDtypeStruct((n, KV_WIDTH), F32), jax.ShapeDtypeStruct((n, KV_WIDTH), F32),
                   jax.ShapeDtypeStruct((KV_WIDTH, n), BF16)],
        compiler_params=_params(("parallel",)),
        name="inproj_b",
    )(x, g_kv.reshape(1, d), g_mix.reshape(1, d), w_kv.astype(BF16), w_in_p,
      jnp.tile(kng, SWA_KV_HEADS).reshape(1, KV_WIDTH), qng_t,
      _block_diag_mean(KV_WIDTH), _block_diag_mean(MAIN_WIDTH))


def _softmax_with_sink(s, sink):
    m = jnp.maximum(jnp.max(s, axis=-1, keepdims=True), sink)
    e = jnp.exp(s - m)
    r = 1.0 / (jnp.sum(e, axis=-1, keepdims=True) + jnp.exp(sink - m))
    return (e * r).astype(BF16)


def _swa_bias(tq):
    slopes = 2.0 ** (-8.0 * np.arange(1, SWA_HEADS + 1, dtype=np.float64) / SWA_HEADS)
    dist = np.arange(tq)[:, None] + WINDOW - np.arange(WINDOW + tq)[None, :]
    valid = (dist >= 0) & (dist <= WINDOW)
    return np.stack([np.where(valid, -s * dist, NEG_BIG) for s in slopes]).astype(np.float32)


def _swa_prompt_kernel(sink_ref, q_ref, ktp_ref, kto_ref, vp_ref, vo_ref, bias_ref, hm_ref, o_ref, *, nb):
    w = WINDOW
    key = lax.broadcasted_iota(jnp.int32, (w, 2 * w), 1)
    has_prev = (pl.program_id(0) > 0) | (key >= w)
    heads = [(g, kh) for g in range(SWA_GROUP) for kh in range(SWA_KV_HEADS)]
    kts, vvs = [], []
    for b in range(nb):
        kt_prev = ktp_ref[...] if b == 0 else kto_ref[:, (b - 1) * w:b * w]
        v_prev = vp_ref[...] if b == 0 else vo_ref[(b - 1) * w:b * w, :]
        kts.append(jnp.concatenate([kt_prev, kto_ref[:, b * w:(b + 1) * w]], axis=1).astype(BF16))
        vvs.append(jnp.concatenate([v_prev, vo_ref[b * w:(b + 1) * w, :]], axis=0).astype(BF16))
    scores = [[jnp.dot(q_ref[b * w:(b + 1) * w, g * KV_WIDTH:(g + 1) * KV_WIDTH] * hm_ref[kh].astype(BF16),
                       kts[b], preferred_element_type=F32) for g, kh in heads] for b in range(nb)]
    for b in range(nb):
        probs = []
        for (g, kh), s in zip(heads, scores[b]):
            h = kh * SWA_GROUP + g
            s = s + bias_ref[h]
            if b == 0:
                s = jnp.where(has_prev, s, NEG_BIG)
            probs.append(_softmax_with_sink(s, sink_ref[h]))
        outs = [jnp.dot(p, vvs[b], preferred_element_type=F32) for p in probs]
        for g in range(SWA_GROUP):
            acc = None
            for (cg, kh), o in zip(heads, outs):
                if cg == g:
                    t = o * hm_ref[kh]
                    acc = t if acc is None else acc + t
            o_ref[b * w:(b + 1) * w, g * KV_WIDTH:(g + 1) * KV_WIDTH] = acc.astype(BF16)


def _swa_prompt(q, kt, v, sinks, *, n_rows, nb):
    w = WINDOW
    step = nb * w
    assert n_rows % step == 0
    prev = lambda j, sink: jnp.maximum(j * nb - 1, 0)
    return pl.pallas_call(
        functools.partial(_swa_prompt_kernel, nb=nb),
        grid_spec=pltpu.PrefetchScalarGridSpec(
            num_scalar_prefetch=1,
            grid=(n_rows // step,),
            in_specs=[pl.BlockSpec((step, MAIN_WIDTH), lambda j, sink: (j, 0)),
                      pl.BlockSpec((KV_WIDTH, w), lambda j, sink: (0, prev(j, sink))),
                      pl.BlockSpec((KV_WIDTH, step), lambda j, sink: (0, j)),
                      pl.BlockSpec((w, KV_WIDTH), lambda j, sink: (prev(j, sink), 0)),
                      pl.BlockSpec((step, KV_WIDTH), lambda j, sink: (j, 0)),
                      pl.BlockSpec((SWA_HEADS, w, 2 * w), lambda j, sink: (0, 0, 0)),
                      pl.BlockSpec((SWA_KV_HEADS, 1, KV_WIDTH), lambda j, sink: (0, 0, 0))],
            out_specs=pl.BlockSpec((step, MAIN_WIDTH), lambda j, sink: (j, 0)),
        ),
        out_shape=jax.ShapeDtypeStruct((n_rows, MAIN_WIDTH), BF16),
        compiler_params=_params(("arbitrary",)),
        name="swa_prompt",
    )(sinks.astype(F32), q, kt, kt, v, v, jnp.asarray(_swa_bias(w)), _head_masks(SWA_KV_HEADS))


def _swa_sample_kernel(q_ref, kp_ref, ko_ref, vp_ref, vo_ref, bias_ref, sink_ref, hm_ref, o_ref, *, nb, tq):
    w = WINDOW
    heads = [(kh, g) for kh in range(SWA_KV_HEADS) for g in range(SWA_GROUP)]
    kks, vvs, scores = [], [], []
    q = q_ref[...].astype(F32)
    for i in range(nb):
        win = slice(i * KV_WIDTH, (i + 1) * KV_WIDTH)
        kks.append(jnp.concatenate([kp_ref[win, :].T, ko_ref[i * tq:(i + 1) * tq, :]], axis=0))
        vvs.append(jnp.concatenate([vp_ref[win, :].T, vo_ref[i * tq:(i + 1) * tq, :]], axis=0))
        qs = jnp.concatenate([q[i * tq:(i + 1) * tq, g * KV_WIDTH:(g + 1) * KV_WIDTH] * hm_ref[kh]
                              for kh, g in heads], axis=0)
        scores.append(_bdot_nt(qs, kks[i]))
    probs = [_softmax_with_sink(s + bias_ref[...], sink_ref[...]) for s in scores]
    outs = [_bdot(p, vv) for p, vv in zip(probs, vvs)]
    for g in range(SWA_GROUP):
        rows = []
        for i in range(nb):
            acc = None
            for r, (kh, hg) in enumerate(heads):
                if hg == g:
                    t = outs[i][r * tq:(r + 1) * tq] * hm_ref[kh]
                    acc = t if acc is None else acc + t
            rows.append(acc)
        o_ref[:, g * KV_WIDTH:(g + 1) * KV_WIDTH] = jnp.concatenate(rows, axis=0).astype(BF16)


def _swa_sample(q, k_win, v_win, k, v, sinks, *, row_off, batch, tq, nb):
    w = WINDOW
    assert batch % nb == 0 and row_off % (nb * tq) == 0
    off = row_off // (nb * tq)
    bias = jnp.asarray(_swa_bias(tq).reshape(SWA_HEADS * tq, w + tq))
    sink_col = jnp.repeat(sinks.astype(F32), tq).reshape(SWA_HEADS * tq, 1)
    own = lambda width: pl.BlockSpec((nb * tq, width), lambda b: (off + b, 0))
    win = pl.BlockSpec((nb * KV_WIDTH, w), lambda b: (b, 0))
    return pl.pallas_call(
        functools.partial(_swa_sample_kernel, nb=nb, tq=tq),
        grid=(batch // nb,),
        in_specs=[own(MAIN_WIDTH), win, own(KV_WIDTH), win, own(KV_WIDTH), _const(bias.shape),
                  _const(sink_col.shape), _const((SWA_KV_HEADS, 1, KV_WIDTH))],
        out_specs=pl.BlockSpec((nb * tq, MAIN_WIDTH), lambda b: (b, 0)),
        out_shape=jax.ShapeDtypeStruct((batch * tq, MAIN_WIDTH), BF16),
        compiler_params=_params(("arbitrary",)),
        name="swa_sample",
    )(q, k_win, k, v_win, v, bias, sink_col, _head_masks(SWA_KV_HEADS))


def kernel(x_prompt, x_sample, state_gla, cache_win_k, cache_win_v, cache_mem_k, cache_mem_v, mem_prompt, norm_mix_g, norm_ffn_g, norm_mem_g, w_mem_kv, mem_qn_g, mem_kn_g, w_out, w_in_a, w_gate_lr, b_gate_lr, gla_norm_g, w_in_b, swa_qn_g, swa_sinks, norm_kv_g, w_kv, swa_kn_g, w_router_group, b_router_group, w_router_expert, b_router_expert, w_exp_gate, w_exp_up, w_exp_down):
    bp, tp, d = x_prompt.shape
    bs, ts, _ = x_sample.shape
    assert bp == 1 and tp % WINDOW == 0 and ts * (GLA_CHUNK // ts) == GLA_CHUNK
    n_p, n_s = bp * tp, bs * ts
    w_buf = cache_win_k.shape[1]
    assert w_buf == WINDOW
    x_p, x_s = x_prompt.reshape(n_p, d), x_sample.reshape(n_s, d)

    mem_k_p, mem_v_p = _mem_kv(mem_prompt, norm_mem_g, w_mem_kv, mem_kn_g)
    feature_major = lambda c: jnp.moveaxis(c, -3, -1).reshape(*c.shape[:-3], c.shape[-2] * c.shape[-1], c.shape[-3])
    cmk, cmv = feature_major(cache_mem_k), feature_major(cache_mem_v)

    def mem_attend(mq, l):
        tm_p = _row_tile(tp, 512)
        mo_p = _mem_attn(mq, mem_k_p, mem_v_p, mem_qn_g[l], row_off=0, seq=tp, tm=tm_p, bb=1, layer=l)
        mo_s = _mem_attn(mq, cmk, cmv, mem_qn_g[l], row_off=n_p, seq=ts, tm=ts, bb=8, layer=l)
        return mo_p, mo_s

    def moe(h, l, split_rows=None):
        return _moe(h, norm_ffn_g[l], w_router_group[l], b_router_group[l], w_router_expert[l],
                    b_router_expert[l], w_exp_gate, w_exp_up, w_exp_down, l, split_rows)

    q, k, la, v, og, mq = _inproj_a(x_p, x_s, norm_mix_g[0], w_in_a[0], w_gate_lr[0], b_gate_lr[0])
    zero_state = jnp.zeros((bp, GLA_HEADS, GLA_DK, GLA_DV), F32)
    n_sub = max(1, min(8, tp // GLA_CHUNK))
    main_p, gla_p = _gla(q, k, la, v, og, zero_state, gla_norm_g[0], row_off=0, seq=tp, n_seg=1, n_sub=n_sub)
    main_s, gla_s = _gla(q, k, la, v, og, state_gla[0], gla_norm_g[0], row_off=n_p, seq=ts,
                         n_seg=GLA_CHUNK // ts, n_sub=1)
    mo_p, mo_s = mem_attend(mq, 0)
    w_o = w_out[0]
    h = _outproj((x_p, x_s), main_p, mo_p, main_s, mo_s, w_o[:MAIN_WIDTH].reshape(GLA_HEADS, GLA_DV, d),
                 w_o[MAIN_WIDTH:])
    h = moe(h, 0)

    q, mq, k_sh, v_sh, kt_sh = _inproj_b(h, norm_kv_g, norm_mix_g[1], w_kv, w_in_b[0], swa_kn_g, swa_qn_g[0])
    ck = feature_major(cache_win_k).reshape(bs * KV_WIDTH, w_buf)
    cv = feature_major(cache_win_v).reshape(bs * KV_WIDTH, w_buf)
    main_p = _swa_prompt(q, kt_sh, v_sh, swa_sinks[0], n_rows=n_p, nb=4)
    main_s = _swa_sample(q, ck, cv, k_sh, v_sh, swa_sinks[0], row_off=n_p, batch=bs, tq=ts, nb=8)
    mo_p, mo_s = mem_attend(mq, 1)
    w_o = w_out[1]
    h = _outproj(h, main_p, mo_p, main_s, mo_s, w_o[:MAIN_WIDTH][_swa_perm()], w_o[MAIN_WIDTH:])
    y_p, y_s = moe(h, 1, split_rows=n_p)

    y_prompt = y_p.reshape(bp, tp, d)
    y_sample = y_s.reshape(bs, ts, d)
    k_new = k_sh[n_p:].reshape(bs, ts, SWA_KV_HEADS, HEAD_DIM)
    v_new = v_sh[n_p:].reshape(bs, ts, SWA_KV_HEADS, HEAD_DIM)
    win_k_s = jnp.concatenate([cache_win_k, k_new], axis=1)[:, -w_buf:]
    win_v_s = jnp.concatenate([cache_win_v, v_new], axis=1)[:, -w_buf:]
    win_k_p = k_sh[n_p - WINDOW:n_p].reshape(bp, WINDOW, SWA_KV_HEADS, HEAD_DIM)
    win_v_p = v_sh[n_p - WINDOW:n_p].reshape(bp, WINDOW, SWA_KV_HEADS, HEAD_DIM)
    token_major = lambda c: jnp.moveaxis(c.reshape(*c.shape[:-2], MEM_HEADS, HEAD_DIM, c.shape[-1]), -1, -3)
    return (y_prompt, y_sample, gla_p[None], gla_s[None], win_k_p, win_v_p, win_k_s, win_v_s,
            token_major(mem_k_p), token_major(mem_v_p))
```

```python
import functools
import math

import numpy as np
import jax
import jax.numpy as jnp
from jax import lax
from jax.experimental import pallas as pl
from jax.experimental.pallas import tpu as pltpu

F32 = jnp.float32
BF16 = jnp.bfloat16

D_MODEL = 1024
MEM_LEN = 256
MEM_HEADS = 4
HEAD_DIM = 64
MEM_WIDTH = MEM_HEADS * HEAD_DIM
MAIN_WIDTH = D_MODEL - MEM_WIDTH
GLA_HEADS = 4
GLA_DV = MAIN_WIDTH // GLA_HEADS
GLA_DK = GLA_DV // 2
GLA_DK_PAD = 128
GLA_KEY_WIDTH = GLA_HEADS * GLA_DK
GLA_KEY_PAD = GLA_HEADS * GLA_DK_PAD
GLA_GATE_RANK = 16
GLA_TAU = 16.0
GLA_CHUNK = 64
SWA_HEADS = MAIN_WIDTH // HEAD_DIM
SWA_KV_HEADS = 4
SWA_GROUP = SWA_HEADS // SWA_KV_HEADS
KV_WIDTH = SWA_KV_HEADS * HEAD_DIM
WINDOW = 128
N_GROUPS = 4
EXPERTS_PER_GROUP = 8
N_EXPERTS = N_GROUPS * EXPERTS_PER_GROUP
D_EXPERT = 512
EPS = 1e-6
LANES = 128
NEG_BIG = -1e30
VMEM_LIMIT = 56 * 1024 * 1024
MOE_TILE = 512
MOE_SUB = 128
ROUTER_LANE0 = N_GROUPS
ROUTER_META_ROWS = 8
SLABS = D_MODEL // LANES
PACKED_SLABS = SLABS // 2


def _bdot(a, b):
    return jnp.dot(a.astype(BF16), b.astype(BF16), preferred_element_type=F32)


def _bdot_nt(a, b):
    return lax.dot_general(a.astype(BF16), b.astype(BF16), (((1,), (1,)), ((), ())),
                           preferred_element_type=F32)


def _bdot_tn(a, b):
    return lax.dot_general(a.astype(BF16), b.astype(BF16), (((0,), (0,)), ((), ())),
                           preferred_element_type=F32)


def _split(x, n):
    parts = []
    for _ in range(n - 1):
        p = x.astype(BF16)
        parts.append(p)
        x = x - p.astype(F32)
    parts.append(x.astype(BF16))
    return parts


def _exact_left_dot(m, x, n=2):
    out = None
    for p in _split(x, n):
        t = jnp.dot(m, p, preferred_element_type=F32)
        out = t if out is None else out + t
    return out


def _seg_mean(x2, bd):
    out = None
    for p in _split(x2, 2):
        t = jnp.dot(p, bd, preferred_element_type=F32)
        out = t if out is None else out + t
    return out


def _rms_scale(x):
    return lax.rsqrt(jnp.mean(x * x, axis=-1, keepdims=True) + EPS)


def _row_tile(n, cap=512):
    t = cap
    while t > 8 and n % t:
        t //= 2
    assert n % t == 0, n
    return t


def _params(sem):
    return pltpu.CompilerParams(dimension_semantics=sem, vmem_limit_bytes=VMEM_LIMIT)


def _const(shape):
    nd = len(shape)
    return pl.BlockSpec(shape, lambda *_: (0,) * nd)


def _group_specs(tm, width, prompt_tiles, lead=None):
    p_idx = lambda i, *_: jnp.minimum(i, prompt_tiles - 1)
    s_idx = lambda i, *_: jnp.maximum(i - prompt_tiles, 0)
    if lead is None:
        return [pl.BlockSpec((tm, width), lambda i, *_, f=f: (f(i), 0)) for f in (p_idx, s_idx)]
    return [pl.BlockSpec((lead, tm, width), lambda i, *_, f=f: (0, f(i), 0)) for f in (p_idx, s_idx)]


def _block_diag_mean(width):
    i = np.arange(width)
    return jnp.asarray((i[:, None] // HEAD_DIM == i[None, :] // HEAD_DIM) / HEAD_DIM, BF16)


def _head_masks(n_heads):
    i = np.arange(n_heads * HEAD_DIM)
    return jnp.asarray((i[None, :] // HEAD_DIM == np.arange(n_heads)[:, None]), F32)[:, None, :]


def _mem_kv_kernel(mem_ref, g_ref, w_ref, kng_ref, bd_ref, k_ref, v_ref):
    x = mem_ref[0]
    hn = x * _rms_scale(x) * g_ref[0]
    kv = _bdot(hn, w_ref[0])
    k = kv[:, :MEM_WIDTH]
    k = k * lax.rsqrt(_seg_mean(k * k, bd_ref[...]) + EPS) * kng_ref[0]
    k_ref[0, 0] = k.T
    v_ref[0, 0] = kv[:, MEM_WIDTH:].T


def _mem_kv(mem, g, w, kng):
    depth, (b, m, d) = w.shape[0], mem.shape
    out = jax.ShapeDtypeStruct((depth, b, m, MEM_WIDTH), F32)
    blk = pl.BlockSpec((1, 1, m, MEM_WIDTH), lambda l, i: (l, i, 0, 0))
    return pl.pallas_call(
        _mem_kv_kernel,
        grid=(depth, b),
        in_specs=[pl.BlockSpec((1, m, d), lambda l, i: (i, 0, 0)),
                  pl.BlockSpec((1, 1, d), lambda l, i: (l, 0, 0)),
                  pl.BlockSpec((1, d, 2 * MEM_WIDTH), lambda l, i: (l, 0, 0)),
                  pl.BlockSpec((1, 1, MEM_WIDTH), lambda l, i: (l, 0, 0)),
                  _const((MEM_WIDTH, MEM_WIDTH))],
        out_specs=[blk, blk],
        out_shape=[out, out],
        compiler_params=_params(("arbitrary", "arbitrary")),
        name="mem_kv",
    )(mem, g.reshape(depth, 1, d), w.astype(BF16),
      jnp.tile(kng, (1, MEM_HEADS)).reshape(depth, 1, MEM_WIDTH), _block_diag_mean(MEM_WIDTH))


def _inproj_a_kernel(xp_ref, xs_ref, g_ref, wq_ref, wk_ref, wv_ref, wog_ref, wlr_ref, wmq_ref, wgl_ref, bgl_ref,
                     q_ref, k_ref, la_ref, v_ref, og_ref, mq_ref, *, prompt_tiles):
    x = jnp.where(pl.program_id(0) < prompt_tiles, xp_ref[...], xs_ref[...])
    hn = (x * _rms_scale(x) * g_ref[...]).astype(BF16)
    q_ref[...] = jnp.dot(hn, wq_ref[...], preferred_element_type=F32) * (GLA_DK ** -0.5)
    k_ref[...] = jnp.dot(hn, wk_ref[...], preferred_element_type=F32)
    for h in range(GLA_HEADS):
        v_ref[h] = jnp.dot(hn, wv_ref[h], preferred_element_type=F32).astype(BF16)
        og_ref[h] = jnp.dot(hn, wog_ref[h], preferred_element_type=F32)
    lr = jnp.dot(hn, wlr_ref[...], preferred_element_type=F32)
    z = _bdot(lr, wgl_ref[...]) + bgl_ref[...]
    la_ref[...] = (jnp.minimum(z, 0.0) - jnp.log(1.0 + jnp.exp(-jnp.abs(z)))) * (1.0 / GLA_TAU)
    mq_ref[...] = jnp.dot(hn, wmq_ref[...], preferred_element_type=F32)


def _pad_heads(w, width, pad):
    lead = w.shape[:-1]
    w = w.reshape(*lead, GLA_HEADS, width)
    w = jnp.pad(w, [(0, 0)] * len(lead) + [(0, 0), (0, pad - width)])
    return w.reshape(*lead, GLA_HEADS * pad)


def _inproj_a(x_p, x_s, g, w_in, w_lr, b_lr):
    (n_p, d), n_s = x_p.shape, x_s.shape[0]
    n = n_p + n_s
    tm = _row_tile(n_s)
    assert n_p % tm == 0
    pt = n_p // tm
    c0, c1, c2, c3, c4 = (GLA_KEY_WIDTH, 2 * GLA_KEY_WIDTH, 2 * GLA_KEY_WIDTH + MAIN_WIDTH,
                          2 * GLA_KEY_WIDTH + 2 * MAIN_WIDTH,
                          2 * GLA_KEY_WIDTH + 2 * MAIN_WIDTH + GLA_GATE_RANK)
    wb = w_in.astype(BF16)
    wq = _pad_heads(wb[:, :c0], GLA_DK, GLA_DK_PAD)
    wk = _pad_heads(wb[:, c0:c1], GLA_DK, GLA_DK_PAD)
    wv = wb[:, c1:c2].reshape(d, GLA_HEADS, GLA_DV).transpose(1, 0, 2)
    wog = wb[:, c2:c3].reshape(d, GLA_HEADS, GLA_DV).transpose(1, 0, 2)
    wlr = jnp.pad(wb[:, c3:c4], ((0, 0), (0, LANES - GLA_GATE_RANK)))
    wmq = wb[:, c4:]
    wgl = jnp.pad(_pad_heads(w_lr.astype(BF16), GLA_DK, GLA_DK_PAD), ((0, LANES - GLA_GATE_RANK), (0, 0)))
    bgl = _pad_heads(b_lr.reshape(1, -1), GLA_DK, GLA_DK_PAD)
    row = lambda w: pl.BlockSpec((tm, w), lambda i: (i, 0))
    hrow = pl.BlockSpec((GLA_HEADS, tm, GLA_DV), lambda i: (0, i, 0))
    key = jax.ShapeDtypeStruct((n, GLA_KEY_PAD), F32)
    val = jax.ShapeDtypeStruct((GLA_HEADS, n, GLA_DV), F32)
    return pl.pallas_call(
        functools.partial(_inproj_a_kernel, prompt_tiles=pt),
        grid=(n // tm,),
        in_specs=_group_specs(tm, d, pt) + [
            _const((1, d)), _const(wq.shape), _const(wk.shape), _const(wv.shape),
            _const(wog.shape), _const(wlr.shape), _const(wmq.shape), _const(wgl.shape),
            _const(bgl.shape)],
        out_specs=[row(GLA_KEY_PAD), row(GLA_KEY_PAD), row(GLA_KEY_PAD), hrow, hrow, row(MEM_WIDTH)],
        out_shape=[key, key, key, jax.ShapeDtypeStruct(val.shape, BF16), val,
                   jax.ShapeDtypeStruct((n, MEM_WIDTH), F32)],
        compiler_params=_params(("parallel",)),
        name="inproj_a",
    )(x_p, x_s, g.reshape(1, d), wq, wk, wv, wog, wlr, wmq, wgl, bgl)


def _gla_kernel(q_ref, k_ref, la_ref, v_ref, og_ref, s0_ref, gn_ref, mcum_ref, mall_ref, sel_ref,
                o_ref, sout_ref, s_ref, *, chunk, n_sub, n_seg):
    j = pl.program_id(1)
    seg = chunk // n_seg

    @pl.when(j == 0)
    def _():
        s_ref[...] = jnp.zeros_like(s_ref)
        s_ref[:, :, :GLA_DK, :] = s0_ref[...]

    mcum = mcum_ref[...]
    causal = mcum.astype(F32) > 0.0
    row = lax.broadcasted_iota(jnp.int32, (chunk, GLA_DK_PAD), 0)
    gn = gn_ref[...]
    hcols = [slice(h * GLA_DK_PAD, (h + 1) * GLA_DK_PAD) for h in range(GLA_HEADS)]
    crows = [slice(c * chunk, (c + 1) * chunk) for c in range(n_sub)]
    qts, kts, kds, e_ends = [], [], [], []
    for rows in crows:
        la = la_ref[rows, :]
        b = _exact_left_dot(mcum, la)
        if n_seg == 1:
            b_end = b[chunk - 1:chunk, :]
            e_ends.append(jnp.broadcast_to(jnp.exp(b_end), (LANES, b.shape[1])).T)
        else:
            b_end = _exact_left_dot(mall_ref[...], la)
            e_ends.append(jnp.exp(_exact_left_dot(sel_ref[...], la)).T)
        k = k_ref[rows, :]
        qts.append(q_ref[rows, :] * jnp.exp(b))
        kts.append((k * jnp.exp(-b)).astype(BF16))
        kds.append(k * jnp.exp(b_end - b))
    vbs = [[v_ref[h, rows, :].astype(BF16) for h in range(GLA_HEADS)] for rows in crows]
    scores = [[_bdot_nt(qts[c][:, cols], kts[c][:, cols]) for cols in hcols] for c in range(n_sub)]
    kvs = []
    for c in range(n_sub):
        per_head = []
        for h, cols in enumerate(hcols):
            per_seg = []
            for s in range(n_seg):
                kd = kds[c][:, cols]
                if n_seg > 1:
                    kd = jnp.where((row >= s * seg) & (row < (s + 1) * seg), kd, 0.0)
                per_seg.append(_bdot_tn(kd, vbs[c][h]))
            per_head.append(per_seg)
        kvs.append(per_head)
    state = [[s_ref[s, h] for s in range(n_seg)] for h in range(GLA_HEADS)]
    inters = []
    for c in range(n_sub):
        per_head = []
        for h, cols in enumerate(hcols):
            parts = []
            for s in range(n_seg):
                parts.append(_bdot(qts[c][s * seg:(s + 1) * seg, cols], state[h][s]))
                state[h][s] = e_ends[c][cols, s:s + 1] * state[h][s] + kvs[c][h][s]
            per_head.append(parts[0] if n_seg == 1 else jnp.concatenate(parts, axis=0))
        inters.append(per_head)
    for h in range(GLA_HEADS):
        for s in range(n_seg):
            s_ref[s, h] = state[h][s]
    for c, rows in enumerate(crows):
        for h in range(GLA_HEADS):
            a = jnp.where(causal, scores[c][h], 0.0)
            o = _bdot(a, vbs[c][h]) + inters[c][h]
            on = o * lax.rsqrt(jnp.mean(o * o, axis=-1, keepdims=True) + EPS) * gn
            og = og_ref[h, rows, :]
            o_ref[h, rows, :] = (on * (og * jax.nn.sigmoid(og))).astype(BF16)

    @pl.when(j == pl.num_programs(1) - 1)
    def _():
        sout_ref[...] = s_ref[:, :, :GLA_DK, :]


def _gla(q, k, la, v, og, s0, gnorm, *, row_off, seq, n_seg, n_sub):
    batch = s0.shape[0]
    chunk = GLA_CHUNK
    assert chunk % n_seg == 0 and batch % n_seg == 0
    seg = chunk // n_seg
    step_rows = n_sub * chunk
    if n_seg > 1:
        assert seq == seg and n_sub == 1
        t_steps = 1
    else:
        assert seq % step_rows == 0
        t_steps = seq // step_rows
    assert row_off % step_rows == 0
    off = row_off // step_rows
    i = np.arange(chunk)
    same = (i[:, None] // seg) == (i[None, :] // seg)
    mcum = jnp.asarray(same & (i[None, :] <= i[:, None]), BF16)
    mall = jnp.asarray(same, BF16)
    sel = jnp.asarray((i[None, :] // seg) == np.arange(LANES)[:, None], BF16)
    ridx = lambda g, j: (off + g * t_steps + j, 0)
    hidx = lambda g, j: (0, off + g * t_steps + j, 0)
    key_spec = pl.BlockSpec((step_rows, GLA_KEY_PAD), ridx)
    val_spec = pl.BlockSpec((GLA_HEADS, step_rows, GLA_DV), hidx)
    st_spec = pl.BlockSpec((n_seg, GLA_HEADS, GLA_DK, GLA_DV), lambda g, j: (g, 0, 0, 0))
    in_specs = [key_spec, key_spec, key_spec, val_spec, val_spec, st_spec, _const((1, GLA_DV)),
                _const((chunk, chunk)), _const((chunk, chunk)), _const((LANES, chunk))]
    args = [q, k, la, v, og, s0, gnorm.reshape(1, GLA_DV), mcum, mall, sel]
    out_spec = pl.BlockSpec((GLA_HEADS, step_rows, GLA_DV), lambda g, j: (0, g * t_steps + j, 0))
    return pl.pallas_call(
        functools.partial(_gla_kernel, chunk=chunk, n_sub=n_sub, n_seg=n_seg),
        grid=(batch // n_seg, t_steps),
        in_specs=in_specs,
        out_specs=[out_spec, st_spec],
        out_shape=[jax.ShapeDtypeStruct((GLA_HEADS, batch * seq, GLA_DV), BF16),
                   jax.ShapeDtypeStruct(s0.shape, F32)],
        scratch_shapes=[pltpu.VMEM((n_seg, GLA_HEADS, GLA_DK_PAD, GLA_DV), F32)],
        compiler_params=_params(("arbitrary", "arbitrary")),
        name="gla",
    )(*args)


def _mem_attn_kernel(q_ref, k_ref, v_ref, g_ref, bd_ref, hm_ref, o_ref, *, tm, bb):
    g = g_ref[...]
    sub = min(tm, 128)
    units = [(i, i * tm + r) for i in range(bb) for r in range(0, tm, sub)]
    scores = []
    for i, r in units:
        q = q_ref[r:r + sub, :]
        qn = q * lax.rsqrt(_seg_mean(q * q, bd_ref[...]) + EPS) * g
        qs = jnp.concatenate([(qn * hm_ref[h]).astype(BF16) for h in range(MEM_HEADS)], axis=0)
        scores.append(_bdot(qs, k_ref[i]))
    probs = []
    for s in scores:
        e = jnp.exp(s - jnp.max(s, axis=-1, keepdims=True))
        probs.append(e * (1.0 / jnp.sum(e, axis=-1, keepdims=True)))
    outs = [_bdot_nt(p, v_ref[i]) for (i, _), p in zip(units, probs)]
    rows = []
    for o in outs:
        acc = o[:sub] * hm_ref[0]
        for h in range(1, MEM_HEADS):
            acc = acc + o[h * sub:(h + 1) * sub] * hm_ref[h]
        rows.append(acc)
    o_ref[...] = jnp.concatenate(rows, axis=0).astype(BF16)


def _mem_attn(mq, mk, mv, qng, *, row_off, seq, tm, bb, layer):
    depth, batch, m, _ = mk.shape
    mk = mk.reshape(depth * batch, m, MEM_WIDTH)
    mv = mv.reshape(depth * batch, m, MEM_WIDTH)
    kv_off = layer * batch // bb
    assert seq % tm == 0 and batch % bb == 0 and (bb == 1 or seq == tm)
    t_steps = seq // tm
    step_rows = bb * tm
    assert row_off % step_rows == 0
    off = row_off // step_rows
    row_spec = pl.BlockSpec((step_rows, MEM_WIDTH), lambda g, j: (off + g * t_steps + j, 0))
    kv_spec = pl.BlockSpec((bb, m, MEM_WIDTH), lambda g, j: (kv_off + g, 0, 0))
    in_specs = [row_spec, kv_spec, kv_spec, _const((1, MEM_WIDTH)), _const((MEM_WIDTH, MEM_WIDTH)),
                _const((MEM_HEADS, 1, MEM_WIDTH))]
    args = [mq, mk, mv, (jnp.tile(qng, MEM_HEADS) * HEAD_DIM ** -0.5).reshape(1, MEM_WIDTH),
            _block_diag_mean(MEM_WIDTH), _head_masks(MEM_HEADS)]
    return pl.pallas_call(
        functools.partial(_mem_attn_kernel, tm=tm, bb=bb),
        grid=(batch // bb, t_steps),
        in_specs=in_specs,
        out_specs=pl.BlockSpec((step_rows, MEM_WIDTH), lambda g, j: (g * t_steps + j, 0)),
        out_shape=jax.ShapeDtypeStruct((batch * seq, MEM_WIDTH), BF16),
        compiler_params=_params(("parallel", "parallel")),
        name="mem_attn",
    )(*args)


def _outproj_kernel(*refs, heads, prompt_tiles, split_residual):
    if split_residual:
        hp_ref, hs_ref = refs[:2]
        refs = refs[2:]
    else:
        hp_ref = hs_ref = refs[0]
        refs = refs[1:]
    main_p_ref, main_s_ref, mo_p_ref, mo_s_ref, wmain_ref, wmo_ref, o_ref = refs

    def project(h_ref, main_ref, mo_ref):
        acc = h_ref[...] + _bdot(mo_ref[...], wmo_ref[...])
        if heads:
            for h in range(heads):
                acc = acc + _bdot(main_ref[h], wmain_ref[h])
        else:
            acc = acc + _bdot(main_ref[...], wmain_ref[...])
        o_ref[...] = acc

    @pl.when(pl.program_id(0) < prompt_tiles)
    def _():
        project(hp_ref, main_p_ref, mo_p_ref)

    @pl.when(pl.program_id(0) >= prompt_tiles)
    def _():
        project(hs_ref, main_s_ref, mo_s_ref)


def _outproj(h, main_p, mo_p, main_s, mo_s, w_main, w_mo):
    n_p, n_s = mo_p.shape[0], mo_s.shape[0]
    n, d = n_p + n_s, w_mo.shape[1]
    tm = _row_tile(n_s)
    assert n_p % tm == 0
    pt = n_p // tm
    heads = main_p.shape[0] if main_p.ndim == 3 else 0
    row = pl.BlockSpec((tm, d), lambda i: (i, 0))
    split = isinstance(h, tuple)
    h_specs, h_args = (_group_specs(tm, d, pt), list(h)) if split else ([row], [h])
    main_specs = _group_specs(tm, main_p.shape[-1], pt, lead=heads or None)
    return pl.pallas_call(
        functools.partial(_outproj_kernel, heads=heads, prompt_tiles=pt, split_residual=split),
        grid=(n // tm,),
        in_specs=h_specs + main_specs + _group_specs(tm, MEM_WIDTH, pt) + [_const(w_main.shape),
                                                                           _const(w_mo.shape)],
        out_specs=row,
        out_shape=jax.ShapeDtypeStruct((n, d), F32),
        compiler_params=_params(("parallel",)),
        name="outproj",
    )(*h_args, main_p, main_s, mo_p, mo_s, w_main.astype(BF16), w_mo.astype(BF16))


def _router_kernel(h_ref, g_ref, whi_ref, wlo_ref, b_ref, tril_ref, before_ref,
                   mi_ref, mf_ref, cnt_ref, tt_ref, carry_ref):
    i = pl.program_id(0)

    @pl.when(i == 0)
    def _():
        carry_ref[...] = jnp.zeros_like(carry_ref)

    x = h_ref[...]
    xn = x * _rms_scale(x) * g_ref[...]
    x_hi, x_lo = _split(xn, 2)
    logits = (jnp.dot(x_hi, whi_ref[...], preferred_element_type=F32)
              + jnp.dot(x_hi, wlo_ref[...], preferred_element_type=F32)
              + jnp.dot(x_lo, whi_ref[...], preferred_element_type=F32)) + b_ref[...]
    tm = x.shape[0]
    lane = lax.broadcasted_iota(jnp.int32, (tm, LANES), 1)
    far = jnp.int32(2 * LANES)

    def first_max(vals):
        m = jnp.max(vals, axis=-1, keepdims=True)
        return m, jnp.min(jnp.where(vals == m, lane, far), axis=-1, keepdims=True)

    gl = jnp.where(lane < N_GROUPS, logits, -jnp.inf)
    gmax, grp = first_max(gl)
    pg_sel = 1.0 / jnp.sum(jnp.exp(gl - gmax), axis=-1, keepdims=True)
    lo = ROUTER_LANE0 + grp * EXPERTS_PER_GROUP
    el = jnp.where((lane >= lo) & (lane < lo + EXPERTS_PER_GROUP), logits, -jnp.inf)
    m1, i1 = first_max(el)
    m2, i2 = first_max(jnp.where(lane == i1, -jnp.inf, el))
    e2 = jnp.exp(m2 - m1)
    g1 = pg_sel / (1.0 + e2)
    g2 = pg_sel * e2 / (1.0 + e2)

    oh1 = lane == i1
    oh2 = lane == i2
    picked = jnp.where(oh1 | oh2, 1.0, 0.0)
    earlier = jnp.dot(tril_ref[...], picked.astype(BF16), preferred_element_type=F32)
    cnt_tile = jnp.sum(picked, axis=0, keepdims=True)
    c8 = jnp.broadcast_to(cnt_tile, (8, LANES))
    c_hi = jnp.floor(c8 * (1.0 / 32.0))
    c_lo = c8 - 32.0 * c_hi
    first = (32.0 * jnp.dot(c_hi.astype(BF16), before_ref[...], preferred_element_type=F32)
             + jnp.dot(c_lo.astype(BF16), before_ref[...], preferred_element_type=F32))[0:1]
    local = first + earlier
    lpos1 = jnp.sum(jnp.where(oh1, local, 0.0), axis=-1, keepdims=True)
    lpos2 = jnp.sum(jnp.where(oh2, local, 0.0), axis=-1, keepdims=True)
    carry_before = carry_ref[...]
    carry = carry_before + cnt_tile
    carry_ref[...] = carry
    cnt_ref[...] = carry

    row8 = lax.broadcasted_iota(jnp.int32, (8, LANES), 0)
    zero8 = jnp.zeros((8, LANES), F32)
    table = jnp.where(row8 == 0, carry_before, jnp.where(row8 == 1, cnt_tile, jnp.where(row8 == 2, first, zero8)))
    tt_ref[...] = table.astype(jnp.int32)
    zi = jnp.zeros((tm, LANES), jnp.int32)
    mi = jnp.where(lane == 0, lpos1.astype(jnp.int32), jnp.where(lane == 1, lpos2.astype(jnp.int32), zi))
    mi_ref[...] = mi.T[:ROUTER_META_ROWS]
    zf = jnp.zeros((tm, LANES), F32)
    mf_ref[...] = jnp.where(lane == 0, g1, jnp.where(lane == 1, g2,
                            jnp.where(lane == 2, lpos1, jnp.where(lane == 3, lpos2, zf))))


def _router(h, g, w_rg, b_rg, w_re, b_re, tm):
    n, d = h.shape
    assert n % tm == 0 and 2 * tm <= 32 * 32
    n_real = N_GROUPS + N_EXPERTS
    w = jnp.pad(jnp.concatenate([w_rg, w_re], axis=1), ((0, 0), (0, LANES - n_real)))
    b = jnp.pad(jnp.concatenate([b_rg, b_re]), (0, LANES - n_real)).reshape(1, LANES)
    w_hi = w.astype(BF16)
    w_lo = (w - w_hi.astype(F32)).astype(BF16)
    i = np.arange(tm)
    tril = jnp.asarray(i[None, :] < i[:, None], BF16)
    lane = np.arange(LANES)
    before = jnp.asarray(lane[:, None] < lane[None, :], BF16)
    row = lambda width: pl.BlockSpec((tm, width), lambda i: (i, 0))
    return pl.pallas_call(
        _router_kernel,
        grid=(n // tm,),
        in_specs=[row(d), _const((1, d)), _const((d, LANES)), _const((d, LANES)), _const((1, LANES)),
                  _const((tm, tm)), _const((LANES, LANES))],
        out_specs=[pl.BlockSpec((ROUTER_META_ROWS, tm), lambda i: (0, i)), row(LANES), _const((1, LANES)),
                   pl.BlockSpec((8, LANES), lambda i: (i, 0))],
        out_shape=[jax.ShapeDtypeStruct((ROUTER_META_ROWS, n), jnp.int32),
                   jax.ShapeDtypeStruct((n, LANES), F32), jax.ShapeDtypeStruct((1, LANES), F32),
                   jax.ShapeDtypeStruct((n // tm * 8, LANES), jnp.int32)],
        scratch_shapes=[pltpu.VMEM((1, LANES), F32)],
        compiler_params=_params(("arbitrary",)),
        name="moe_router",
    )(h, g.reshape(1, d), w_hi, w_lo, b, tril, before)


def _to_rows(ref, rows, lead=()):
    return jnp.concatenate([ref[lead + (pl.ds(s, rows, stride=SLABS), slice(None))] for s in range(SLABS)],
                           axis=1)


def _from_rows(ref, x, rows, lead=()):
    for s in range(SLABS):
        ref[lead + (pl.ds(s, rows, stride=SLABS), slice(None))] = x[:, s * LANES:(s + 1) * LANES]


def _pack_rows(ref, x, rows, lead=(), row0=0):
    u32 = jnp.uint32
    for w in range(PACKED_SLABS):
        lo = x[:, (2 * w) * LANES:(2 * w + 1) * LANES].astype(BF16).astype(F32)
        hi = x[:, (2 * w + 1) * LANES:(2 * w + 2) * LANES].astype(BF16).astype(F32)
        word = (lax.bitcast_convert_type(lo, u32) >> 16) | (lax.bitcast_convert_type(hi, u32) & u32(0xFFFF0000))
        ref[lead + (pl.ds(row0 * PACKED_SLABS + w, rows, stride=PACKED_SLABS), slice(None))] = word


def _unpack_rows(ref, rows, lead=(), row0=0):
    u32 = jnp.uint32
    slabs = []
    for w in range(PACKED_SLABS):
        word = ref[lead + (pl.ds(row0 * PACKED_SLABS + w, rows, stride=PACKED_SLABS), slice(None))]
        slabs.append(lax.bitcast_convert_type(word << 16, F32).astype(BF16))
        slabs.append(lax.bitcast_convert_type(word & u32(0xFFFF0000), F32).astype(BF16))
    return jnp.concatenate(slabs, axis=1)


RUN_FIELDS = 3
RUN_CHUNK_BITS = 6


def _copy_runs(runs_ref, tile, local_rows, global_rows, sem, *, to_global):
    ps = PACKED_SLABS
    base = tile * (RUN_FIELDS * N_EXPERTS)

    def piece(g0, l0, off, size):
        g = global_rows(pl.multiple_of((g0 + off) * ps, ps), size * ps)
        l = local_rows(pl.multiple_of((l0 + off) * ps, ps), size * ps)
        src, dst = (l, g) if to_global else (g, l)
        pltpu.make_async_copy(src, dst, sem).start()

    def per_expert(e, carry):
        g0 = runs_ref[base + e]
        length = runs_ref[base + N_EXPERTS + e]
        l0 = runs_ref[base + 2 * N_EXPERTS + e]
        big = 1 << RUN_CHUNK_BITS

        def big_piece(c, inner):
            piece(g0, l0, c * big, big)
            return inner

        n_big = length >> RUN_CHUNK_BITS
        lax.fori_loop(0, n_big, big_piece, 0)
        off = n_big * big
        for bit in reversed(range(RUN_CHUNK_BITS)):
            size = 1 << bit

            @pl.when((length & size) != 0)
            def _(off=off, size=size):
                piece(g0, l0, off, size)

            off = off + (length & size)
        return carry

    lax.fori_loop(0, N_EXPERTS, per_expert, 0)


def _dispatch_kernel(runs_ref, h_ref, g_ref, meta_ref, xs_hbm, buf, sem, *, tm, steps):
    i = pl.program_id(0)
    slot = lax.rem(i, 2)
    ns = 2 * tm

    def wait_slot(sl):
        pltpu.make_async_copy(buf.at[sl], xs_hbm.at[pl.ds(0, ns * PACKED_SLABS)], sem.at[sl]).wait()

    @pl.when(i >= 2)
    def _():
        wait_slot(slot)

    x = h_ref[...]
    xn = (x * _rms_scale(x) * g_ref[...]).astype(BF16)
    j = lax.broadcasted_iota(jnp.int32, (ns, tm), 0)
    pick = jnp.where((j == meta_ref[0:1, :]) | (j == meta_ref[1:2, :]), 1.0, 0.0).astype(BF16)
    _pack_rows(buf, jnp.dot(pick, xn, preferred_element_type=F32), ns, (slot,))
    _copy_runs(runs_ref, i, lambda start, size: buf.at[slot, pl.ds(start, size)],
               lambda start, size: xs_hbm.at[pl.ds(start, size)], sem.at[slot], to_global=True)

    @pl.when(i == steps - 1)
    def _():
        wait_slot(slot)
        if steps > 1:
            wait_slot(1 - slot)


def _dispatch(h, g, meta, runs, tm):
    n, d = h.shape
    steps = n // tm
    return pl.pallas_call(
        functools.partial(_dispatch_kernel, tm=tm, steps=steps),
        grid_spec=pltpu.PrefetchScalarGridSpec(
            num_scalar_prefetch=1,
            grid=(steps,),
            in_specs=[pl.BlockSpec((tm, d), lambda i, runs: (i, 0)),
                      pl.BlockSpec((1, d), lambda i, runs: (0, 0)),
                      pl.BlockSpec((ROUTER_META_ROWS, tm), lambda i, runs: (0, i))],
            out_specs=pl.BlockSpec(memory_space=pl.ANY),
            scratch_shapes=[pltpu.VMEM((2, 2 * tm * PACKED_SLABS, LANES), jnp.uint32),
                            pltpu.SemaphoreType.DMA((2,))],
        ),
        out_shape=jax.ShapeDtypeStruct((2 * n * PACKED_SLABS, LANES), jnp.uint32),
        compiler_params=_params(("arbitrary",)),
        name="moe_dispatch",
    )(runs, h, g.reshape(1, d), meta)


ITEM_FIRST, ITEM_LAST, ITEM_NEW_EXPERT = 1, 2, 4


def _expert_kernel(tile_ref, exp_ref, lo_ref, hi_ref, flag_ref, wslot_ref, next_ref, n_ref,
                   xs_ref, wg_hbm, wu_hbm, wd_hbm, ys_ref,
                   wgf, wuf, wdf, wsem, wgb, wub, wdb, acc, *, tm, sub, layer):
    w = pl.program_id(0)

    @pl.when(w == 0)
    def _():
        acc[...] = jnp.zeros_like(acc)

    def weight_copies(e, slot):
        return [pltpu.make_async_copy(hbm.at[layer, e], buf.at[slot], wsem.at[slot])
                for hbm, buf in ((wg_hbm, wgf), (wu_hbm, wuf), (wd_hbm, wdf))]

    @pl.when(w < n_ref[0])
    def _():
        flags = flag_ref[w]

        @pl.when((flags & ITEM_NEW_EXPERT) != 0)
        def _():
            slot = wslot_ref[w]

            @pl.when(w == 0)
            def _():
                for c in weight_copies(exp_ref[w], slot):
                    c.start()

            for c in weight_copies(exp_ref[w], slot):
                c.wait()
            wgb[...] = wgf[slot].astype(BF16)
            wub[...] = wuf[slot].astype(BF16)
            wdb[...] = wdf[slot].astype(BF16)

            @pl.when(next_ref[w] >= 0)
            def _():
                for c in weight_copies(next_ref[w], 1 - slot):
                    c.start()

        first = (flags & ITEM_FIRST) != 0
        lo, hi = lo_ref[w], hi_ref[w]
        n_blocks = tm // sub

        def up(s):
            x = _unpack_rows(xs_ref, sub, row0=s * sub)
            return (jnp.dot(x, wgb[...], preferred_element_type=F32),
                    jnp.dot(x, wub[...], preferred_element_type=F32))

        ups = {0: up(0)}
        for s in range(n_blocks):
            if s + 1 < n_blocks:
                ups[s + 1] = up(s + 1)
            hg, hu = ups.pop(s)
            act = (hg * jax.nn.sigmoid(hg) * hu).astype(BF16)
            y = jnp.dot(act, wdb[...], preferred_element_type=F32)
            row = lax.broadcasted_iota(jnp.int32, (sub, 1), 0) + s * sub
            y = jnp.where((row >= lo) & (row < hi), y, 0.0)
            rows = slice(s * sub, (s + 1) * sub)
            total = jnp.where(first, y, acc[rows, :] + y)
            acc[rows, :] = total
            _pack_rows(ys_ref, total, sub, row0=s * sub)


def _experts(xs, items, w_g, w_u, w_d, layer):
    d = D_MODEL
    tm = MOE_TILE
    packed_spec = pl.BlockSpec((tm * PACKED_SLABS, LANES), lambda w, tile, *_: (tile[w], 0))
    any_spec = pl.BlockSpec(memory_space=pl.ANY)
    return pl.pallas_call(
        functools.partial(_expert_kernel, tm=tm, sub=MOE_SUB, layer=layer),
        grid_spec=pltpu.PrefetchScalarGridSpec(
            num_scalar_prefetch=len(items),
            grid=(items[0].shape[0],),
            in_specs=[packed_spec, any_spec, any_spec, any_spec],
            out_specs=packed_spec,
            scratch_shapes=[pltpu.VMEM((2, d, D_EXPERT), F32), pltpu.VMEM((2, d, D_EXPERT), F32),
                            pltpu.VMEM((2, D_EXPERT, d), F32), pltpu.SemaphoreType.DMA((2,)),
                            pltpu.VMEM((d, D_EXPERT), BF16), pltpu.VMEM((d, D_EXPERT), BF16),
                            pltpu.VMEM((D_EXPERT, d), BF16), pltpu.VMEM((tm, d), F32)],
        ),
        out_shape=jax.ShapeDtypeStruct(xs.shape, jnp.uint32),
        compiler_params=_params(("arbitrary",)),
        name="moe_experts",
    )(*items, xs, w_g, w_u, w_d)


def _combine_kernel(runs_ref, h_ref, gate_ref, ys_hbm, *refs, tm, steps, prompt_tiles):
    out_refs, (buf, sem) = refs[:-2], refs[-2:]
    i = pl.program_id(0)
    slot = lax.rem(i, 2)
    ns = 2 * tm

    def issue(step, sl):
        _copy_runs(runs_ref, step, lambda start, size: buf.at[sl, pl.ds(start, size)],
                   lambda start, size: ys_hbm.at[pl.ds(start, size)], sem.at[sl], to_global=False)

    @pl.when(i == 0)
    def _():
        issue(0, 0)

    @pl.when(i + 1 < steps)
    def _():
        issue(i + 1, 1 - slot)

    pltpu.make_async_copy(ys_hbm.at[pl.ds(0, ns * PACKED_SLABS)], buf.at[slot], sem.at[slot]).wait()
    y = _unpack_rows(buf, ns, (slot,))
    g = gate_ref[...]
    j = lax.broadcasted_iota(jnp.int32, (tm, ns), 1)
    out = h_ref[...]
    for kk in range(2):
        pick = jnp.where(j == g[:, 2 + kk:3 + kk].astype(jnp.int32), 1.0, 0.0).astype(BF16)
        out = out + g[:, kk:kk + 1] * jnp.dot(pick, y, preferred_element_type=F32)
    if len(out_refs) == 1:
        out_refs[0][...] = out
    else:
        @pl.when(i < prompt_tiles)
        def _():
            out_refs[0][...] = out

        @pl.when(i >= prompt_tiles)
        def _():
            out_refs[1][...] = out


def _combine(h, gates, ys, runs, tm, split_rows=None):
    n, d = h.shape
    steps = n // tm
    row = pl.BlockSpec((tm, d), lambda i, pos: (i, 0))
    if split_rows is None:
        pt, out_specs, out_shape = 0, row, jax.ShapeDtypeStruct((n, d), F32)
    else:
        assert split_rows % tm == 0
        pt = split_rows // tm
        out_specs = _group_specs(tm, d, pt)
        out_shape = [jax.ShapeDtypeStruct((split_rows, d), F32), jax.ShapeDtypeStruct((n - split_rows, d), F32)]
    return pl.pallas_call(
        functools.partial(_combine_kernel, tm=tm, steps=steps, prompt_tiles=pt),
        grid_spec=pltpu.PrefetchScalarGridSpec(
            num_scalar_prefetch=1,
            grid=(steps,),
            in_specs=[row, pl.BlockSpec((tm, LANES), lambda i, pos: (i, 0)),
                      pl.BlockSpec(memory_space=pl.ANY)],
            out_specs=out_specs,
            scratch_shapes=[pltpu.VMEM((2, 2 * tm * PACKED_SLABS, LANES), jnp.uint32),
                            pltpu.SemaphoreType.DMA((2,))],
        ),
        out_shape=out_shape,
        compiler_params=_params(("arbitrary",)),
        name="moe_combine",
    )(runs, h, gates, ys)


def _lookup(tables, idx):
    hit = idx[:, None] == jnp.arange(tables.shape[1], dtype=idx.dtype)[None, :]
    return jnp.sum(jnp.where(hit[None], tables[:, None, :], 0), axis=2)


def _work_items(counts, n_slots):
    tm = MOE_TILE
    assert n_slots % tm == 0, (n_slots, tm)
    ends = jnp.cumsum(counts)
    starts = ends - counts
    first_tile = starts // tm
    n_tiles_e = jnp.where(counts > 0, (ends - 1) // tm - first_tile + 1, 0)
    item_end = jnp.cumsum(n_tiles_e)
    item_start = item_end - n_tiles_e
    n_items = item_end[-1]
    max_items = n_slots // tm + N_EXPERTS - 1
    w = jnp.minimum(jnp.arange(max_items, dtype=jnp.int32), n_items - 1)
    expert = jnp.sum(w[:, None] >= item_end[None, :], axis=1).astype(jnp.int32)
    first_tile_w, item_start_w, start_w, end_w = _lookup(jnp.stack([first_tile, item_start, starts, ends]), expert)
    tile = first_tile_w + w - item_start_w
    lo = jnp.maximum(start_w - tile * tm, 0)
    hi = jnp.minimum(end_w - tile * tm, tm)
    prev_tile = jnp.concatenate([jnp.full((1,), -1, jnp.int32), tile[:-1]])
    next_tile = jnp.concatenate([tile[1:], jnp.full((1,), -1, jnp.int32)])
    prev_expert = jnp.concatenate([jnp.full((1,), -1, jnp.int32), expert[:-1]])
    idx = jnp.arange(max_items, dtype=jnp.int32)
    new_expert = expert != prev_expert
    flags = (jnp.where(tile != prev_tile, ITEM_FIRST, 0)
             | jnp.where((tile != next_tile) | (idx == n_items - 1), ITEM_LAST, 0)
             | jnp.where(new_expert, ITEM_NEW_EXPERT, 0))
    weight_slot = (jnp.cumsum(new_expert.astype(jnp.int32)) - 1) % 2
    ids = jnp.arange(N_EXPERTS, dtype=jnp.int32)
    later = (ids[None, :] > ids[:, None]) & (n_tiles_e[None, :] > 0)
    following = jnp.min(jnp.where(later, ids[None, :], N_EXPERTS), axis=1)
    following = jnp.where(following == N_EXPERTS, -1, following)
    next_expert, = _lookup(following[None, :], expert)
    as_i32 = lambda a: a.astype(jnp.int32)
    return (as_i32(tile), expert, as_i32(lo), as_i32(hi), as_i32(flags), as_i32(weight_slot),
            as_i32(next_expert), as_i32(n_items).reshape(1))


def _moe(h, g, w_rg, b_rg, w_re, b_re, w_g, w_u, w_d, layer, split_rows=None):
    n, _ = h.shape
    tm = _row_tile(math.gcd(n, split_rows or n), MOE_TILE)
    meta, gates, cnt, tables = _router(h, g, w_rg, b_rg, w_re, b_re, tm)
    experts = slice(ROUTER_LANE0, ROUTER_LANE0 + N_EXPERTS)
    counts = cnt[0, experts].astype(jnp.int32)
    starts = jnp.cumsum(counts) - counts
    tables = tables.reshape(n // tm, 8, LANES)[:, :RUN_FIELDS, experts]
    runs = tables.at[:, 0, :].add(starts[None, :]).reshape(-1)
    xs = _dispatch(h, g, meta, runs, tm)
    ys = _experts(xs, _work_items(counts, 2 * n), w_g, w_u, w_d, layer)
    return _combine(h, gates, ys, runs, tm, split_rows)


def _inproj_b_kernel(x_ref, gkv_ref, gmix_ref, wkv_ref, win_ref, kng_ref, qng_ref, bdk_ref, bdq_ref,
                     q_ref, mq_ref, k_ref, v_ref, kt_ref):
    x = x_ref[...]
    xr = x * _rms_scale(x)
    kv = _bdot(xr * gkv_ref[...], wkv_ref[...])
    k = kv[:, :KV_WIDTH]
    k = k * lax.rsqrt(_seg_mean(k * k, bdk_ref[...]) + EPS) * kng_ref[...]
    k_ref[...] = k
    kt_ref[...] = k.T.astype(BF16)
    v_ref[...] = kv[:, KV_WIDTH:]
    proj = _bdot(xr * gmix_ref[...], win_ref[...])
    q = proj[:, :MAIN_WIDTH]
    q_ref[...] = (q * lax.rsqrt(_seg_mean(q * q, bdq_ref[...]) + EPS) * qng_ref[...]).astype(BF16)
    mq_ref[...] = proj[:, MAIN_WIDTH:]


def _swa_perm():
    g, kh, dd = np.meshgrid(np.arange(SWA_GROUP), np.arange(SWA_KV_HEADS), np.arange(HEAD_DIM), indexing="ij")
    return ((kh * SWA_GROUP + g) * HEAD_DIM + dd).reshape(-1)


def _inproj_b(x, g_kv, g_mix, w_kv, w_in, kng, qng):
    n, d = x.shape
    tm = _row_tile(n)
    perm = _swa_perm()
    w_in_p = jnp.concatenate([w_in[:, :MAIN_WIDTH][:, perm], w_in[:, MAIN_WIDTH:]], axis=1).astype(BF16)
    qng_t = (jnp.tile(qng, SWA_HEADS) * HEAD_DIM ** -0.5).reshape(1, MAIN_WIDTH)
    row = lambda w: pl.BlockSpec((tm, w), lambda i: (i, 0))
    return pl.pallas_call(
        _inproj_b_kernel,
        grid=(n // tm,),
        in_specs=[row(d), _const((1, d)), _const((1, d)), _const((d, 2 * KV_WIDTH)), _const((d, d)),
                  _const((1, KV_WIDTH)), _const((1, MAIN_WIDTH)), _const((KV_WIDTH, KV_WIDTH)),
                  _const((MAIN_WIDTH, MAIN_WIDTH))],
        out_specs=[row(MAIN_WIDTH), row(MEM_WIDTH), row(KV_WIDTH), row(KV_WIDTH),
                   pl.BlockSpec((KV_WIDTH, tm), lambda i: (0, i))],
        out_shape=[jax.ShapeDtypeStruct((n, MAIN_WIDTH), BF16), jax.ShapeDtypeStruct((n, MEM_WIDTH), F32),
                   jax.ShapeDtypeStruct((n, KV_WIDTH), F32), jax.ShapeDtypeStruct((n, KV_WIDTH), F32),
                   jax.ShapeDtypeStruct((KV_WIDTH, n), BF16)],
        compiler_params=_params(("parallel",)),
        name="inproj_b",
    )(x, g_kv.reshape(1, d), g_mix.reshape(1, d), w_kv.astype(BF16), w_in_p,
      jnp.tile(kng, SWA_KV_HEADS).reshape(1, KV_WIDTH), qng_t,
      _block_diag_mean(KV_WIDTH), _block_diag_mean(MAIN_WIDTH))


def _softmax_with_sink(s, sink):
    m = jnp.maximum(jnp.max(s, axis=-1, keepdims=True), sink)
    e = jnp.exp(s - m)
    r = 1.0 / (jnp.sum(e, axis=-1, keepdims=True) + jnp.exp(sink - m))
    return (e * r).astype(BF16)


def _swa_bias(tq):
    slopes = 2.0 ** (-8.0 * np.arange(1, SWA_HEADS + 1, dtype=np.float64) / SWA_HEADS)
    dist = np.arange(tq)[:, None] + WINDOW - np.arange(WINDOW + tq)[None, :]
    valid = (dist >= 0) & (dist <= WINDOW)
    return np.stack([np.where(valid, -s * dist, NEG_BIG) for s in slopes]).astype(np.float32)


def _swa_prompt_kernel(sink_ref, q_ref, ktp_ref, kto_ref, vp_ref, vo_ref, bias_ref, hm_ref, o_ref, *, nb):
    w = WINDOW
    key = lax.broadcasted_iota(jnp.int32, (w, 2 * w), 1)
    has_prev = (pl.program_id(0) > 0) | (key >= w)
    heads = [(g, kh) for g in range(SWA_GROUP) for kh in range(SWA_KV_HEADS)]
    kts, vvs = [], []
    for b in range(nb):
        kt_prev = ktp_ref[...] if b == 0 else kto_ref[:, (b - 1) * w:b * w]
        v_prev = vp_ref[...] if b == 0 else vo_ref[(b - 1) * w:b * w, :]
        kts.append(jnp.concatenate([kt_prev, kto_ref[:, b * w:(b + 1) * w]], axis=1).astype(BF16))
        vvs.append(jnp.concatenate([v_prev, vo_ref[b * w:(b + 1) * w, :]], axis=0).astype(BF16))
    scores = [[jnp.dot(q_ref[b * w:(b + 1) * w, g * KV_WIDTH:(g + 1) * KV_WIDTH] * hm_ref[kh].astype(BF16),
                       kts[b], preferred_element_type=F32) for g, kh in heads] for b in range(nb)]
    for b in range(nb):
        probs = []
        for (g, kh), s in zip(heads, scores[b]):
            h = kh * SWA_GROUP + g
            s = s + bias_ref[h]
            if b == 0:
                s = jnp.where(has_prev, s, NEG_BIG)
            probs.append(_softmax_with_sink(s, sink_ref[h]))
        outs = [jnp.dot(p, vvs[b], preferred_element_type=F32) for p in probs]
        for g in range(SWA_GROUP):
            acc = None
            for (cg, kh), o in zip(heads, outs):
                if cg == g:
                    t = o * hm_ref[kh]
                    acc = t if acc is None else acc + t
            o_ref[b * w:(b + 1) * w, g * KV_WIDTH:(g + 1) * KV_WIDTH] = acc.astype(BF16)


def _swa_prompt(q, kt, v, sinks, *, n_rows, nb):
    w = WINDOW
    step = nb * w
    assert n_rows % step == 0
    prev = lambda j, sink: jnp.maximum(j * nb - 1, 0)
    return pl.pallas_call(
        functools.partial(_swa_prompt_kernel, nb=nb),
        grid_spec=pltpu.PrefetchScalarGridSpec(
            num_scalar_prefetch=1,
            grid=(n_rows // step,),
            in_specs=[pl.BlockSpec((step, MAIN_WIDTH), lambda j, sink: (j, 0)),
                      pl.BlockSpec((KV_WIDTH, w), lambda j, sink: (0, prev(j, sink))),
                      pl.BlockSpec((KV_WIDTH, step), lambda j, sink: (0, j)),
                      pl.BlockSpec((w, KV_WIDTH), lambda j, sink: (prev(j, sink), 0)),
                      pl.BlockSpec((step, KV_WIDTH), lambda j, sink: (j, 0)),
                      pl.BlockSpec((SWA_HEADS, w, 2 * w), lambda j, sink: (0, 0, 0)),
                      pl.BlockSpec((SWA_KV_HEADS, 1, KV_WIDTH), lambda j, sink: (0, 0, 0))],
            out_specs=pl.BlockSpec((step, MAIN_WIDTH), lambda j, sink: (j, 0)),
        ),
        out_shape=jax.ShapeDtypeStruct((n_rows, MAIN_WIDTH), BF16),
        compiler_params=_params(("arbitrary",)),
        name="swa_prompt",
    )(sinks.astype(F32), q, kt, kt, v, v, jnp.asarray(_swa_bias(w)), _head_masks(SWA_KV_HEADS))


def _swa_sample_kernel(q_ref, kp_ref, ko_ref, vp_ref, vo_ref, bias_ref, sink_ref, hm_ref, o_ref, *, nb, tq):
    w = WINDOW
    heads = [(kh, g) for kh in range(SWA_KV_HEADS) for g in range(SWA_GROUP)]
    kks, vvs, scores = [], [], []
    q = q_ref[...].astype(F32)
    for i in range(nb):
        win = slice(i * KV_WIDTH, (i + 1) * KV_WIDTH)
        kks.append(jnp.concatenate([kp_ref[win, :].T, ko_ref[i * tq:(i + 1) * tq, :]], axis=0))
        vvs.append(jnp.concatenate([vp_ref[win, :].T, vo_ref[i * tq:(i + 1) * tq, :]], axis=0))
        qs = jnp.concatenate([q[i * tq:(i + 1) * tq, g * KV_WIDTH:(g + 1) * KV_WIDTH] * hm_ref[kh]
                              for kh, g in heads], axis=0)
        scores.append(_bdot_nt(qs, kks[i]))
    probs = [_softmax_with_sink(s + bias_ref[...], sink_ref[...]) for s in scores]
    outs = [_bdot(p, vv) for p, vv in zip(probs, vvs)]
    for g in range(SWA_GROUP):
        rows = []
        for i in range(nb):
            acc = None
            for r, (kh, hg) in enumerate(heads):
                if hg == g:
                    t = outs[i][r * tq:(r + 1) * tq] * hm_ref[kh]
                    acc = t if acc is None else acc + t
            rows.append(acc)
        o_ref[:, g * KV_WIDTH:(g + 1) * KV_WIDTH] = jnp.concatenate(rows, axis=0).astype(BF16)


def _swa_sample(q, k_win, v_win, k, v, sinks, *, row_off, batch, tq, nb):
    w = WINDOW
    assert batch % nb == 0 and row_off % (nb * tq) == 0
    off = row_off // (nb * tq)
    bias = jnp.asarray(_swa_bias(tq).reshape(SWA_HEADS * tq, w + tq))
    sink_col = jnp.repeat(sinks.astype(F32), tq).reshape(SWA_HEADS * tq, 1)
    own = lambda width: pl.BlockSpec((nb * tq, width), lambda b: (off + b, 0))
    win = pl.BlockSpec((nb * KV_WIDTH, w), lambda b: (b, 0))
    return pl.pallas_call(
        functools.partial(_swa_sample_kernel, nb=nb, tq=tq),
        grid=(batch // nb,),
        in_specs=[own(MAIN_WIDTH), win, own(KV_WIDTH), win, own(KV_WIDTH), _const(bias.shape),
                  _const(sink_col.shape), _const((SWA_KV_HEADS, 1, KV_WIDTH))],
        out_specs=pl.BlockSpec((nb * tq, MAIN_WIDTH), lambda b: (b, 0)),
        out_shape=jax.ShapeDtypeStruct((batch * tq, MAIN_WIDTH), BF16),
        compiler_params=_params(("arbitrary",)),
        name="swa_sample",
    )(q, k_win, k, v_win, v, bias, sink_col, _head_masks(SWA_KV_HEADS))


def kernel(x_prompt, x_sample, state_gla, cache_win_k, cache_win_v, cache_mem_k, cache_mem_v, mem_prompt, norm_mix_g, norm_ffn_g, norm_mem_g, w_mem_kv, mem_qn_g, mem_kn_g, w_out, w_in_a, w_gate_lr, b_gate_lr, gla_norm_g, w_in_b, swa_qn_g, swa_sinks, norm_kv_g, w_kv, swa_kn_g, w_router_group, b_router_group, w_router_expert, b_router_expert, w_exp_gate, w_exp_up, w_exp_down):
    bp, tp, d = x_prompt.shape
    bs, ts, _ = x_sample.shape
    assert bp == 1 and tp % WINDOW == 0 and ts * (GLA_CHUNK // ts) == GLA_CHUNK
    n_p, n_s = bp * tp, bs * ts
    w_buf = cache_win_k.shape[1]
    assert w_buf == WINDOW
    x_p, x_s = x_prompt.reshape(n_p, d), x_sample.reshape(n_s, d)

    mem_k_p, mem_v_p = _mem_kv(mem_prompt, norm_mem_g, w_mem_kv, mem_kn_g)
    feature_major = lambda c: jnp.moveaxis(c, -3, -1).reshape(*c.shape[:-3], c.shape[-2] * c.shape[-1], c.shape[-3])
    cmk, cmv = feature_major(cache_mem_k), feature_major(cache_mem_v)

    def mem_attend(mq, l):
        tm_p = _row_tile(tp, 512)
        mo_p = _mem_attn(mq, mem_k_p, mem_v_p, mem_qn_g[l], row_off=0, seq=tp, tm=tm_p, bb=1, layer=l)
        mo_s = _mem_attn(mq, cmk, cmv, mem_qn_g[l], row_off=n_p, seq=ts, tm=ts, bb=8, layer=l)
        return mo_p, mo_s

    def moe(h, l, split_rows=None):
        return _moe(h, norm_ffn_g[l], w_router_group[l], b_router_group[l], w_router_expert[l],
                    b_router_expert[l], w_exp_gate, w_exp_up, w_exp_down, l, split_rows)

    q, k, la, v, og, mq = _inproj_a(x_p, x_s, norm_mix_g[0], w_in_a[0], w_gate_lr[0], b_gate_lr[0])
    zero_state = jnp.zeros((bp, GLA_HEADS, GLA_DK, GLA_DV), F32)
    n_sub = max(1, min(8, tp // GLA_CHUNK))
    main_p, gla_p = _gla(q, k, la, v, og, zero_state, gla_norm_g[0], row_off=0, seq=tp, n_seg=1, n_sub=n_sub)
    main_s, gla_s = _gla(q, k, la, v, og, state_gla[0], gla_norm_g[0], row_off=n_p, seq=ts,
                         n_seg=GLA_CHUNK // ts, n_sub=1)
    mo_p, mo_s = mem_attend(mq, 0)
    w_o = w_out[0]
    h = _outproj((x_p, x_s), main_p, mo_p, main_s, mo_s, w_o[:MAIN_WIDTH].reshape(GLA_HEADS, GLA_DV, d),
                 w_o[MAIN_WIDTH:])
    h = moe(h, 0)

    q, mq, k_sh, v_sh, kt_sh = _inproj_b(h, norm_kv_g, norm_mix_g[1], w_kv, w_in_b[0], swa_kn_g, swa_qn_g[0])
    ck = feature_major(cache_win_k).reshape(bs * KV_WIDTH, w_buf)
    cv = feature_major(cache_win_v).reshape(bs * KV_WIDTH, w_buf)
    main_p = _swa_prompt(q, kt_sh, v_sh, swa_sinks[0], n_rows=n_p, nb=4)
    main_s = _swa_sample(q, ck, cv, k_sh, v_sh, swa_sinks[0], row_off=n_p, batch=bs, tq=ts, nb=8)
    mo_p, mo_s = mem_attend(mq, 1)
    w_o = w_out[1]
    h = _outproj(h, main_p, mo_p, main_s, mo_s, w_o[:MAIN_WIDTH][_swa_perm()], w_o[MAIN_WIDTH:])
    y_p, y_s = moe(h, 1, split_rows=n_p)

    y_prompt = y_p.reshape(bp, tp, d)
    y_sample = y_s.reshape(bs, ts, d)
    k_new = k_sh[n_p:].reshape(bs, ts, SWA_KV_HEADS, HEAD_DIM)
    v_new = v_sh[n_p:].reshape(bs, ts, SWA_KV_HEADS, HEAD_DIM)
    win_k_s = jnp.concatenate([cache_win_k, k_new], axis=1)[:, -w_buf:]
    win_v_s = jnp.concatenate([cache_win_v, v_new], axis=1)[:, -w_buf:]
    win_k_p = k_sh[n_p - WINDOW:n_p].reshape(bp, WINDOW, SWA_KV_HEADS, HEAD_DIM)
    win_v_p = v_sh[n_p - WINDOW:n_p].reshape(bp, WINDOW, SWA_KV_HEADS, HEAD_DIM)
    token_major = lambda c: jnp.moveaxis(c.reshape(*c.shape[:-2], MEM_HEADS, HEAD_DIM, c.shape[-1]), -1, -3)
    return (y_prompt, y_sample, gla_p[None], gla_s[None], win_k_p, win_v_p, win_k_s, win_v_s,
            token_major(mem_k_p), token_major(mem_v_p))
```

```python
import functools
import math

import numpy as np
import jax
import jax.numpy as jnp
from jax import lax
from jax.experimental import pallas as pl
from jax.experimental.pallas import tpu as pltpu

F32 = jnp.float32
BF16 = jnp.bfloat16

D_MODEL = 1024
MEM_LEN = 256
MEM_HEADS = 4
HEAD_DIM = 64
MEM_WIDTH = MEM_HEADS * HEAD_DIM
MAIN_WIDTH = D_MODEL - MEM_WIDTH
GLA_HEADS = 4
GLA_DV = MAIN_WIDTH // GLA_HEADS
GLA_DK = GLA_DV // 2
GLA_DK_PAD = 128
GLA_KEY_WIDTH = GLA_HEADS * GLA_DK
GLA_KEY_PAD = GLA_HEADS * GLA_DK_PAD
GLA_GATE_RANK = 16
GLA_TAU = 16.0
GLA_CHUNK = 64
SWA_HEADS = MAIN_WIDTH // HEAD_DIM
SWA_KV_HEADS = 4
SWA_GROUP = SWA_HEADS // SWA_KV_HEADS
KV_WIDTH = SWA_KV_HEADS * HEAD_DIM
WINDOW = 128
N_GROUPS = 4
EXPERTS_PER_GROUP = 8
N_EXPERTS = N_GROUPS * EXPERTS_PER_GROUP
D_EXPERT = 512
EPS = 1e-6
LANES = 128
NEG_BIG = -1e30
VMEM_LIMIT = 56 * 1024 * 1024
MOE_TILE = 512
MOE_SUB = 128
ROUTER_LANE0 = N_GROUPS
ROUTER_META_ROWS = 8
ROUTER_ROWS = 40
SLABS = D_MODEL // LANES
PACKED_SLABS = SLABS // 2


def _bdot(a, b):
    return jnp.dot(a.astype(BF16), b.astype(BF16), preferred_element_type=F32)


def _bdot_nt(a, b):
    return lax.dot_general(a.astype(BF16), b.astype(BF16), (((1,), (1,)), ((), ())),
                           preferred_element_type=F32)


def _bdot_tn(a, b):
    return lax.dot_general(a.astype(BF16), b.astype(BF16), (((0,), (0,)), ((), ())),
                           preferred_element_type=F32)


def _split(x, n):
    parts = []
    for _ in range(n - 1):
        p = x.astype(BF16)
        parts.append(p)
        x = x - p.astype(F32)
    parts.append(x.astype(BF16))
    return parts


def _exact_left_dot(m, x, n=2):
    out = None
    for p in _split(x, n):
        t = jnp.dot(m, p, preferred_element_type=F32)
        out = t if out is None else out + t
    return out


def _seg_mean(x2, bd):
    out = None
    for p in _split(x2, 2):
        t = jnp.dot(p, bd, preferred_element_type=F32)
        out = t if out is None else out + t
    return out


def _rms_scale(x):
    return lax.rsqrt(jnp.mean(x * x, axis=-1, keepdims=True) + EPS)


def _row_tile(n, cap=512):
    t = cap
    while t > 8 and n % t:
        t //= 2
    assert n % t == 0, n
    return t


def _params(sem):
    return pltpu.CompilerParams(dimension_semantics=sem, vmem_limit_bytes=VMEM_LIMIT)


def _const(shape):
    nd = len(shape)
    return pl.BlockSpec(shape, lambda *_: (0,) * nd)


def _group_specs(tm, width, prompt_tiles, lead=None):
    p_idx = lambda i, *_: jnp.minimum(i, prompt_tiles - 1)
    s_idx = lambda i, *_: jnp.maximum(i - prompt_tiles, 0)
    if lead is None:
        return [pl.BlockSpec((tm, width), lambda i, *_, f=f: (f(i), 0)) for f in (p_idx, s_idx)]
    return [pl.BlockSpec((lead, tm, width), lambda i, *_, f=f: (0, f(i), 0)) for f in (p_idx, s_idx)]


def _block_diag_mean(width):
    i = np.arange(width)
    return jnp.asarray((i[:, None] // HEAD_DIM == i[None, :] // HEAD_DIM) / HEAD_DIM, BF16)


def _head_masks(n_heads):
    i = np.arange(n_heads * HEAD_DIM)
    return jnp.asarray((i[None, :] // HEAD_DIM == np.arange(n_heads)[:, None]), F32)[:, None, :]


def _mem_kv_kernel(mem_ref, g_ref, w_ref, kng_ref, bd_ref, k_ref, v_ref):
    x = mem_ref[0]
    hn = x * _rms_scale(x) * g_ref[0]
    kv = _bdot(hn, w_ref[0])
    k = kv[:, :MEM_WIDTH]
    k = k * lax.rsqrt(_seg_mean(k * k, bd_ref[...]) + EPS) * kng_ref[0]
    k_ref[0, 0] = k.T
    v_ref[0, 0] = kv[:, MEM_WIDTH:].T


def _mem_kv(mem, g, w, kng):
    depth, (b, m, d) = w.shape[0], mem.shape
    out = jax.ShapeDtypeStruct((depth, b, m, MEM_WIDTH), F32)
    blk = pl.BlockSpec((1, 1, m, MEM_WIDTH), lambda l, i: (l, i, 0, 0))
    return pl.pallas_call(
        _mem_kv_kernel,
        grid=(depth, b),
        in_specs=[pl.BlockSpec((1, m, d), lambda l, i: (i, 0, 0)),
                  pl.BlockSpec((1, 1, d), lambda l, i: (l, 0, 0)),
                  pl.BlockSpec((1, d, 2 * MEM_WIDTH), lambda l, i: (l, 0, 0)),
                  pl.BlockSpec((1, 1, MEM_WIDTH), lambda l, i: (l, 0, 0)),
                  _const((MEM_WIDTH, MEM_WIDTH))],
        out_specs=[blk, blk],
        out_shape=[out, out],
        compiler_params=_params(("arbitrary", "arbitrary")),
        name="mem_kv",
    )(mem, g.reshape(depth, 1, d), w.astype(BF16),
      jnp.tile(kng, (1, MEM_HEADS)).reshape(depth, 1, MEM_WIDTH), _block_diag_mean(MEM_WIDTH))


def _inproj_a_kernel(xp_ref, xs_ref, g_ref, wq_ref, wk_ref, wv_ref, wog_ref, wlr_ref, wmq_ref, wgl_ref, bgl_ref,
                     q_ref, k_ref, la_ref, v_ref, og_ref, mq_ref, *, prompt_tiles):
    x = jnp.where(pl.program_id(0) < prompt_tiles, xp_ref[...], xs_ref[...])
    hn = (x * _rms_scale(x) * g_ref[...]).astype(BF16)
    q_ref[...] = jnp.dot(hn, wq_ref[...], preferred_element_type=F32) * (GLA_DK ** -0.5)
    k_ref[...] = jnp.dot(hn, wk_ref[...], preferred_element_type=F32)
    for h in range(GLA_HEADS):
        v_ref[h] = jnp.dot(hn, wv_ref[h], preferred_element_type=F32).astype(BF16)
        og_ref[h] = jnp.dot(hn, wog_ref[h], preferred_element_type=F32)
    lr = jnp.dot(hn, wlr_ref[...], preferred_element_type=F32)
    z = _bdot(lr, wgl_ref[...]) + bgl_ref[...]
    la_ref[...] = (jnp.minimum(z, 0.0) - jnp.log(1.0 + jnp.exp(-jnp.abs(z)))) * (1.0 / GLA_TAU)
    mq_ref[...] = jnp.dot(hn, wmq_ref[...], preferred_element_type=F32)


def _pad_heads(w, width, pad):
    lead = w.shape[:-1]
    w = w.reshape(*lead, GLA_HEADS, width)
    w = jnp.pad(w, [(0, 0)] * len(lead) + [(0, 0), (0, pad - width)])
    return w.reshape(*lead, GLA_HEADS * pad)


def _inproj_a(x_p, x_s, g, w_in, w_lr, b_lr):
    (n_p, d), n_s = x_p.shape, x_s.shape[0]
    n = n_p + n_s
    tm = _row_tile(n_s)
    assert n_p % tm == 0
    pt = n_p // tm
    c0, c1, c2, c3, c4 = (GLA_KEY_WIDTH, 2 * GLA_KEY_WIDTH, 2 * GLA_KEY_WIDTH + MAIN_WIDTH,
                          2 * GLA_KEY_WIDTH + 2 * MAIN_WIDTH,
                          2 * GLA_KEY_WIDTH + 2 * MAIN_WIDTH + GLA_GATE_RANK)
    wb = w_in.astype(BF16)
    wq = _pad_heads(wb[:, :c0], GLA_DK, GLA_DK_PAD)
    wk = _pad_heads(wb[:, c0:c1], GLA_DK, GLA_DK_PAD)
    wv = wb[:, c1:c2].reshape(d, GLA_HEADS, GLA_DV).transpose(1, 0, 2)
    wog = wb[:, c2:c3].reshape(d, GLA_HEADS, GLA_DV).transpose(1, 0, 2)
    wlr = jnp.pad(wb[:, c3:c4], ((0, 0), (0, LANES - GLA_GATE_RANK)))
    wmq = wb[:, c4:]
    wgl = jnp.pad(_pad_heads(w_lr.astype(BF16), GLA_DK, GLA_DK_PAD), ((0, LANES - GLA_GATE_RANK), (0, 0)))
    bgl = _pad_heads(b_lr.reshape(1, -1), GLA_DK, GLA_DK_PAD)
    row = lambda w: pl.BlockSpec((tm, w), lambda i: (i, 0))
    hrow = pl.BlockSpec((GLA_HEADS, tm, GLA_DV), lambda i: (0, i, 0))
    key = jax.ShapeDtypeStruct((n, GLA_KEY_PAD), F32)
    val = jax.ShapeDtypeStruct((GLA_HEADS, n, GLA_DV), F32)
    return pl.pallas_call(
        functools.partial(_inproj_a_kernel, prompt_tiles=pt),
        grid=(n // tm,),
        in_specs=_group_specs(tm, d, pt) + [
            _const((1, d)), _const(wq.shape), _const(wk.shape), _const(wv.shape),
            _const(wog.shape), _const(wlr.shape), _const(wmq.shape), _const(wgl.shape),
            _const(bgl.shape)],
        out_specs=[row(GLA_KEY_PAD), row(GLA_KEY_PAD), row(GLA_KEY_PAD), hrow, hrow, row(MEM_WIDTH)],
        out_shape=[key, key, key, jax.ShapeDtypeStruct(val.shape, BF16), val,
                   jax.ShapeDtypeStruct((n, MEM_WIDTH), F32)],
        compiler_params=_params(("parallel",)),
        name="inproj_a",
    )(x_p, x_s, g.reshape(1, d), wq, wk, wv, wog, wlr, wmq, wgl, bgl)


def _gla_kernel(q_ref, k_ref, la_ref, v_ref, og_ref, s0_ref, gn_ref, mcum_ref, mall_ref, sel_ref,
                o_ref, sout_ref, s_ref, *, chunk, n_sub, n_seg):
    j = pl.program_id(1)
    seg = chunk // n_seg

    @pl.when(j == 0)
    def _():
        s_ref[...] = jnp.zeros_like(s_ref)
        s_ref[:, :, :GLA_DK, :] = s0_ref[...]

    mcum = mcum_ref[...]
    causal = mcum.astype(F32) > 0.0
    row = lax.broadcasted_iota(jnp.int32, (chunk, GLA_DK_PAD), 0)
    gn = gn_ref[...]
    hcols = [slice(h * GLA_DK_PAD, (h + 1) * GLA_DK_PAD) for h in range(GLA_HEADS)]
    crows = [slice(c * chunk, (c + 1) * chunk) for c in range(n_sub)]
    qts, kts, kds, e_ends = [], [], [], []
    for rows in crows:
        la = la_ref[rows, :]
        b = _exact_left_dot(mcum, la)
        if n_seg == 1:
            b_end = b[chunk - 1:chunk, :]
            e_ends.append(jnp.broadcast_to(jnp.exp(b_end), (LANES, b.shape[1])).T)
        else:
            b_end = _exact_left_dot(mall_ref[...], la)
            e_ends.append(jnp.exp(_exact_left_dot(sel_ref[...], la)).T)
        k = k_ref[rows, :]
        qts.append(q_ref[rows, :] * jnp.exp(b))
        kts.append((k * jnp.exp(-b)).astype(BF16))
        kds.append(k * jnp.exp(b_end - b))
    vbs = [[v_ref[h, rows, :].astype(BF16) for h in range(GLA_HEADS)] for rows in crows]
    scores = [[_bdot_nt(qts[c][:, cols], kts[c][:, cols]) for cols in hcols] for c in range(n_sub)]
    kvs = []
    for c in range(n_sub):
        per_head = []
        for h, cols in enumerate(hcols):
            per_seg = []
            for s in range(n_seg):
                kd = kds[c][:, cols]
                if n_seg > 1:
                    kd = jnp.where((row >= s * seg) & (row < (s + 1) * seg), kd, 0.0)
                per_seg.append(_bdot_tn(kd, vbs[c][h]))
            per_head.append(per_seg)
        kvs.append(per_head)
    state = [[s_ref[s, h] for s in range(n_seg)] for h in range(GLA_HEADS)]
    inters = []
    for c in range(n_sub):
        per_head = []
        for h, cols in enumerate(hcols):
            parts = []
            for s in range(n_seg):
                parts.append(_bdot(qts[c][s * seg:(s + 1) * seg, cols], state[h][s]))
                state[h][s] = e_ends[c][cols, s:s + 1] * state[h][s] + kvs[c][h][s]
            per_head.append(parts[0] if n_seg == 1 else jnp.concatenate(parts, axis=0))
        inters.append(per_head)
    for h in range(GLA_HEADS):
        for s in range(n_seg):
            s_ref[s, h] = state[h][s]
    for c, rows in enumerate(crows):
        for h in range(GLA_HEADS):
            a = jnp.where(causal, scores[c][h], 0.0)
            o = _bdot(a, vbs[c][h]) + inters[c][h]
            on = o * lax.rsqrt(jnp.mean(o * o, axis=-1, keepdims=True) + EPS) * gn
            og = og_ref[h, rows, :]
            o_ref[h, rows, :] = (on * (og * jax.nn.sigmoid(og))).astype(BF16)

    @pl.when(j == pl.num_programs(1) - 1)
    def _():
        sout_ref[...] = s_ref[:, :, :GLA_DK, :]


def _gla(q, k, la, v, og, s0, gnorm, *, row_off, seq, n_seg, n_sub):
    batch = s0.shape[0]
    chunk = GLA_CHUNK
    assert chunk % n_seg == 0 and batch % n_seg == 0
    seg = chunk // n_seg
    step_rows = n_sub * chunk
    if n_seg > 1:
        assert seq == seg and n_sub == 1
        t_steps = 1
    else:
        assert seq % step_rows == 0
        t_steps = seq // step_rows
    assert row_off % step_rows == 0
    off = row_off // step_rows
    i = np.arange(chunk)
    same = (i[:, None] // seg) == (i[None, :] // seg)
    mcum = jnp.asarray(same & (i[None, :] <= i[:, None]), BF16)
    mall = jnp.asarray(same, BF16)
    sel = jnp.asarray((i[None, :] // seg) == np.arange(LANES)[:, None], BF16)
    ridx = lambda g, j: (off + g * t_steps + j, 0)
    hidx = lambda g, j: (0, off + g * t_steps + j, 0)
    key_spec = pl.BlockSpec((step_rows, GLA_KEY_PAD), ridx)
    val_spec = pl.BlockSpec((GLA_HEADS, step_rows, GLA_DV), hidx)
    st_spec = pl.BlockSpec((n_seg, GLA_HEADS, GLA_DK, GLA_DV), lambda g, j: (g, 0, 0, 0))
    in_specs = [key_spec, key_spec, key_spec, val_spec, val_spec, st_spec, _const((1, GLA_DV)),
                _const((chunk, chunk)), _const((chunk, chunk)), _const((LANES, chunk))]
    args = [q, k, la, v, og, s0, gnorm.reshape(1, GLA_DV), mcum, mall, sel]
    out_spec = pl.BlockSpec((GLA_HEADS, step_rows, GLA_DV), lambda g, j: (0, g * t_steps + j, 0))
    return pl.pallas_call(
        functools.partial(_gla_kernel, chunk=chunk, n_sub=n_sub, n_seg=n_seg),
        grid=(batch // n_seg, t_steps),
        in_specs=in_specs,
        out_specs=[out_spec, st_spec],
        out_shape=[jax.ShapeDtypeStruct((GLA_HEADS, batch * seq, GLA_DV), BF16),
                   jax.ShapeDtypeStruct(s0.shape, F32)],
        scratch_shapes=[pltpu.VMEM((n_seg, GLA_HEADS, GLA_DK_PAD, GLA_DV), F32)],
        compiler_params=_params(("arbitrary", "arbitrary")),
        name="gla",
    )(*args)


def _mem_attn_kernel(q_ref, k_ref, v_ref, g_ref, bd_ref, hm_ref, o_ref, *, tm, bb):
    g = g_ref[...]
    sub = min(tm, 128)
    units = [(i, i * tm + r) for i in range(bb) for r in range(0, tm, sub)]
    scores = []
    for i, r in units:
        q = q_ref[r:r + sub, :]
        qn = q * lax.rsqrt(_seg_mean(q * q, bd_ref[...]) + EPS) * g
        qs = jnp.concatenate([(qn * hm_ref[h]).astype(BF16) for h in range(MEM_HEADS)], axis=0)
        scores.append(_bdot(qs, k_ref[i]))
    probs = []
    for s in scores:
        e = jnp.exp(s - jnp.max(s, axis=-1, keepdims=True))
        probs.append(e * (1.0 / jnp.sum(e, axis=-1, keepdims=True)))
    outs = [_bdot_nt(p, v_ref[i]) for (i, _), p in zip(units, probs)]
    rows = []
    for o in outs:
        acc = o[:sub] * hm_ref[0]
        for h in range(1, MEM_HEADS):
            acc = acc + o[h * sub:(h + 1) * sub] * hm_ref[h]
        rows.append(acc)
    o_ref[...] = jnp.concatenate(rows, axis=0).astype(BF16)


def _mem_attn(mq, mk, mv, qng, *, row_off, seq, tm, bb, layer):
    depth, batch, m, _ = mk.shape
    mk = mk.reshape(depth * batch, m, MEM_WIDTH)
    mv = mv.reshape(depth * batch, m, MEM_WIDTH)
    kv_off = layer * batch // bb
    assert seq % tm == 0 and batch % bb == 0 and (bb == 1 or seq == tm)
    t_steps = seq // tm
    step_rows = bb * tm
    assert row_off % step_rows == 0
    off = row_off // step_rows
    row_spec = pl.BlockSpec((step_rows, MEM_WIDTH), lambda g, j: (off + g * t_steps + j, 0))
    kv_spec = pl.BlockSpec((bb, m, MEM_WIDTH), lambda g, j: (kv_off + g, 0, 0))
    in_specs = [row_spec, kv_spec, kv_spec, _const((1, MEM_WIDTH)), _const((MEM_WIDTH, MEM_WIDTH)),
                _const((MEM_HEADS, 1, MEM_WIDTH))]
    args = [mq, mk, mv, (jnp.tile(qng, MEM_HEADS) * HEAD_DIM ** -0.5).reshape(1, MEM_WIDTH),
            _block_diag_mean(MEM_WIDTH), _head_masks(MEM_HEADS)]
    return pl.pallas_call(
        functools.partial(_mem_attn_kernel, tm=tm, bb=bb),
        grid=(batch // bb, t_steps),
        in_specs=in_specs,
        out_specs=pl.BlockSpec((step_rows, MEM_WIDTH), lambda g, j: (g * t_steps + j, 0)),
        out_shape=jax.ShapeDtypeStruct((batch * seq, MEM_WIDTH), BF16),
        compiler_params=_params(("parallel", "parallel")),
        name="mem_attn",
    )(*args)


def _outproj_kernel(*refs, heads, prompt_tiles, split_residual):
    n_h = 2 if split_residual else 1
    h_refs, refs = refs[:n_h], refs[n_h:]
    (main_p_ref, main_s_ref, mo_p_ref, mo_s_ref, wmain_ref, wmo_ref), refs = refs[:6], refs[6:]
    route_in, (o_ref, *route_out) = refs[:6], refs[6:]
    is_prompt = pl.program_id(0) < prompt_tiles
    pick = lambda p, s: jnp.where(is_prompt, p, s)
    tm = o_ref.shape[0]
    sub = min(tm, 128)
    blocks = []
    for r in range(0, tm, sub):
        rows = slice(r, r + sub)
        acc = pick(h_refs[0][rows, :], h_refs[1][rows, :]) if split_residual else h_refs[0][rows, :]
        acc = acc + _bdot(pick(mo_p_ref[rows, :], mo_s_ref[rows, :]), wmo_ref[...])
        if heads:
            for h in range(heads):
                acc = acc + _bdot(pick(main_p_ref[h, rows, :], main_s_ref[h, rows, :]), wmain_ref[h])
        else:
            acc = acc + _bdot(pick(main_p_ref[rows, :], main_s_ref[rows, :]), wmain_ref[...])
        o_ref[rows, :] = acc
        blocks.append(acc)
    _route_tile(blocks, *route_in, *route_out)


def _outproj(h, main_p, mo_p, main_s, mo_s, w_main, w_mo, router, tm):
    n_p, n_s = mo_p.shape[0], mo_s.shape[0]
    n, d = n_p + n_s, w_mo.shape[1]
    assert n_p % tm == 0 and n_s % tm == 0
    pt = n_p // tm
    heads = main_p.shape[0] if main_p.ndim == 3 else 0
    row = pl.BlockSpec((tm, d), lambda i: (i, 0))
    split = isinstance(h, tuple)
    h_specs, h_args = (_group_specs(tm, d, pt), list(h)) if split else ([row], [h])
    main_specs = _group_specs(tm, main_p.shape[-1], pt, lead=heads or None)
    r_args, r_in_specs, r_out_specs, r_out_shape, r_scratch = _route_operands(*router, n, tm)
    h_new, *routed = pl.pallas_call(
        functools.partial(_outproj_kernel, heads=heads, prompt_tiles=pt, split_residual=split),
        grid=(n // tm,),
        in_specs=(h_specs + main_specs + _group_specs(tm, MEM_WIDTH, pt)
                  + [_const(w_main.shape), _const(w_mo.shape)] + r_in_specs),
        out_specs=[row] + r_out_specs,
        out_shape=[jax.ShapeDtypeStruct((n, d), F32)] + r_out_shape,
        scratch_shapes=[r_scratch],
        compiler_params=_params(("arbitrary",)),
        name="outproj_route",
    )(*h_args, main_p, main_s, mo_p, mo_s, w_main.astype(BF16), w_mo.astype(BF16), *r_args)
    return h_new, routed


def _route_tile(x_blocks, g_ref, whi_ref, wlo_ref, b_ref, before_tok_ref, before_row_ref,
                mi_ref, mf_ref, cnt_ref, tt_ref, carry_ref):
    @pl.when(pl.program_id(0) == 0)
    def _():
        carry_ref[...] = jnp.zeros_like(carry_ref)

    nt = lambda a, b: lax.dot_general(a, b, (((1,), (1,)), ((), ())), preferred_element_type=F32)
    logits = []
    for x in x_blocks:
        x_hi, x_lo = _split(x * _rms_scale(x) * g_ref[...], 2)
        logits.append((nt(whi_ref[...], x_hi) + nt(wlo_ref[...], x_hi) + nt(whi_ref[...], x_lo))[:ROUTER_ROWS])
    logits = jnp.concatenate(logits, axis=1) + b_ref[...]
    tm = logits.shape[1]
    row = lax.broadcasted_iota(jnp.int32, (ROUTER_ROWS, tm), 0)
    far = jnp.int32(2 * LANES)

    def first_max(vals):
        m = jnp.max(vals, axis=0, keepdims=True)
        return m, jnp.min(jnp.where(vals == m, row, far), axis=0, keepdims=True)

    gl = jnp.where(row < N_GROUPS, logits, -jnp.inf)
    gmax, grp = first_max(gl)
    pg_sel = 1.0 / jnp.sum(jnp.exp(gl - gmax), axis=0, keepdims=True)
    lo = ROUTER_LANE0 + grp * EXPERTS_PER_GROUP
    el = jnp.where((row >= lo) & (row < lo + EXPERTS_PER_GROUP), logits, -jnp.inf)
    m1, i1 = first_max(el)
    m2, i2 = first_max(jnp.where(row == i1, -jnp.inf, el))
    e2 = jnp.exp(m2 - m1)
    g1 = pg_sel / (1.0 + e2)
    g2 = pg_sel * e2 / (1.0 + e2)

    oh1 = row == i1
    oh2 = row == i2
    picked = jnp.where(oh1 | oh2, 1.0, 0.0)
    earlier = jnp.dot(picked.astype(BF16), before_tok_ref[...], preferred_element_type=F32)
    cnt_col = jnp.sum(picked, axis=1, keepdims=True)
    cnt_tile = jnp.concatenate([jnp.broadcast_to(cnt_col, (ROUTER_ROWS, LANES)),
                                jnp.zeros((LANES - ROUTER_ROWS, LANES), F32)], axis=0)
    c_hi = jnp.floor(cnt_tile * (1.0 / 32.0))
    c_lo = cnt_tile - 32.0 * c_hi
    first = (32.0 * jnp.dot(before_row_ref[...], c_hi.astype(BF16), preferred_element_type=F32)
             + jnp.dot(before_row_ref[...], c_lo.astype(BF16), preferred_element_type=F32))
    local = first[:ROUTER_ROWS, 0:1] + earlier
    lpos1 = jnp.sum(jnp.where(oh1, local, 0.0), axis=0, keepdims=True)
    lpos2 = jnp.sum(jnp.where(oh2, local, 0.0), axis=0, keepdims=True)
    carry_before = carry_ref[...]
    carry = carry_before + cnt_tile
    carry_ref[...] = carry

    lane = lax.broadcasted_iota(jnp.int32, (LANES, LANES), 1)
    cols = jnp.where(lane == 0, carry_before, jnp.where(lane == 1, cnt_tile, jnp.where(lane == 2, first,
                     jnp.where(lane == 3, carry, 0.0))))
    tables = cols.T
    tt_ref[...] = tables[:8].astype(jnp.int32)
    cnt_ref[...] = tables[3:4]
    row8 = lax.broadcasted_iota(jnp.int32, (ROUTER_META_ROWS, tm), 0)
    zero8 = jnp.zeros((ROUTER_META_ROWS, tm), F32)
    mi_ref[...] = jnp.where(row8 == 0, lpos1, jnp.where(row8 == 1, lpos2, zero8)).astype(jnp.int32)
    stacked = jnp.where(row8 == 0, g1, jnp.where(row8 == 1, g2, jnp.where(row8 == 2, lpos1,
                        jnp.where(row8 == 3, lpos2, zero8))))
    mf_ref[...] = jnp.concatenate([stacked, jnp.zeros((LANES - ROUTER_META_ROWS, tm), F32)], axis=0).T


def _route_operands(g, w_rg, b_rg, w_re, b_re, n, tm):
    d = g.shape[0]
    assert n % tm == 0 and 2 * tm <= 32 * 32
    n_real = N_GROUPS + N_EXPERTS
    w = jnp.pad(jnp.concatenate([w_rg, w_re], axis=1), ((0, 0), (0, LANES - n_real))).T
    b = jnp.pad(jnp.concatenate([b_rg, b_re]), (0, ROUTER_ROWS - n_real)).reshape(ROUTER_ROWS, 1)
    w_hi = w.astype(BF16)
    w_lo = (w - w_hi.astype(F32)).astype(BF16)
    i = np.arange(tm)
    before_tok = jnp.asarray(i[:, None] < i[None, :], BF16)
    e = np.arange(LANES)
    before_row = jnp.asarray(e[None, :] < e[:, None], BF16)
    args = [g.reshape(1, d), w_hi, w_lo, b, before_tok, before_row]
    in_specs = [_const(a.shape) for a in args]
    out_specs = [pl.BlockSpec((ROUTER_META_ROWS, tm), lambda i: (0, i)), pl.BlockSpec((tm, LANES), lambda i: (i, 0)),
                 _const((1, LANES)), pl.BlockSpec((8, LANES), lambda i: (i, 0))]
    out_shape = [jax.ShapeDtypeStruct((ROUTER_META_ROWS, n), jnp.int32), jax.ShapeDtypeStruct((n, LANES), F32),
                 jax.ShapeDtypeStruct((1, LANES), F32), jax.ShapeDtypeStruct((n // tm * 8, LANES), jnp.int32)]
    return args, in_specs, out_specs, out_shape, pltpu.VMEM((LANES, LANES), F32)


def _pack_rows(ref, x, rows, lead=(), row0=0):
    u32 = jnp.uint32
    for w in range(PACKED_SLABS):
        lo = x[:, (2 * w) * LANES:(2 * w + 1) * LANES].astype(BF16).astype(F32)
        hi = x[:, (2 * w + 1) * LANES:(2 * w + 2) * LANES].astype(BF16).astype(F32)
        word = (lax.bitcast_convert_type(lo, u32) >> 16) | (lax.bitcast_convert_type(hi, u32) & u32(0xFFFF0000))
        ref[lead + (pl.ds(row0 * PACKED_SLABS + w, rows, stride=PACKED_SLABS), slice(None))] = word


def _unpack_rows(ref, rows, lead=(), row0=0):
    u32 = jnp.uint32
    slabs = []
    for w in range(PACKED_SLABS):
        word = ref[lead + (pl.ds(row0 * PACKED_SLABS + w, rows, stride=PACKED_SLABS), slice(None))]
        slabs.append(lax.bitcast_convert_type(word << 16, F32).astype(BF16))
        slabs.append(lax.bitcast_convert_type(word & u32(0xFFFF0000), F32).astype(BF16))
    return jnp.concatenate(slabs, axis=1)


RUN_FIELDS = 3
RUN_CHUNK_BITS = 6


def _copy_runs(runs_ref, tile, local_rows, global_rows, sem, *, to_global):
    ps = PACKED_SLABS
    base = tile * (RUN_FIELDS * N_EXPERTS)

    def piece(g0, l0, off, size):
        g = global_rows(pl.multiple_of((g0 + off) * ps, ps), size * ps)
        l = local_rows(pl.multiple_of((l0 + off) * ps, ps), size * ps)
        src, dst = (l, g) if to_global else (g, l)
        pltpu.make_async_copy(src, dst, sem).start()

    def per_expert(e, carry):
        g0 = runs_ref[base + e]
        length = runs_ref[base + N_EXPERTS + e]
        l0 = runs_ref[base + 2 * N_EXPERTS + e]
        big = 1 << RUN_CHUNK_BITS

        def big_piece(c, inner):
            piece(g0, l0, c * big, big)
            return inner

        n_big = length >> RUN_CHUNK_BITS
        lax.fori_loop(0, n_big, big_piece, 0)
        off = n_big * big
        for bit in reversed(range(RUN_CHUNK_BITS)):
            size = 1 << bit

            @pl.when((length & size) != 0)
            def _(off=off, size=size):
                piece(g0, l0, off, size)

            off = off + (length & size)
        return carry

    lax.fori_loop(0, N_EXPERTS, per_expert, 0)


def _dispatch_kernel(runs_ref, h_ref, g_ref, meta_ref, xs_hbm, buf, sem, *, tm, steps):
    i = pl.program_id(0)
    slot = lax.rem(i, 2)
    ns = 2 * tm

    def wait_slot(sl):
        pltpu.make_async_copy(buf.at[sl], xs_hbm.at[pl.ds(0, ns * PACKED_SLABS)], sem.at[sl]).wait()

    @pl.when(i >= 2)
    def _():
        wait_slot(slot)

    x = h_ref[...]
    xn = (x * _rms_scale(x) * g_ref[...]).astype(BF16)
    j = lax.broadcasted_iota(jnp.int32, (ns, tm), 0)
    pick = jnp.where((j == meta_ref[0:1, :]) | (j == meta_ref[1:2, :]), 1.0, 0.0).astype(BF16)
    _pack_rows(buf, jnp.dot(pick, xn, preferred_element_type=F32), ns, (slot,))
    _copy_runs(runs_ref, i, lambda start, size: buf.at[slot, pl.ds(start, size)],
               lambda start, size: xs_hbm.at[pl.ds(start, size)], sem.at[slot], to_global=True)

    @pl.when(i == steps - 1)
    def _():
        wait_slot(slot)
        if steps > 1:
            wait_slot(1 - slot)


def _dispatch(h, g, meta, runs, tm):
    n, d = h.shape
    steps = n // tm
    return pl.pallas_call(
        functools.partial(_dispatch_kernel, tm=tm, steps=steps),
        grid_spec=pltpu.PrefetchScalarGridSpec(
            num_scalar_prefetch=1,
            grid=(steps,),
            in_specs=[pl.BlockSpec((tm, d), lambda i, runs: (i, 0)),
                      pl.BlockSpec((1, d), lambda i, runs: (0, 0)),
                      pl.BlockSpec((ROUTER_META_ROWS, tm), lambda i, runs: (0, i))],
            out_specs=pl.BlockSpec(memory_space=pl.ANY),
            scratch_shapes=[pltpu.VMEM((2, 2 * tm * PACKED_SLABS, LANES), jnp.uint32),
                            pltpu.SemaphoreType.DMA((2,))],
        ),
        out_shape=jax.ShapeDtypeStruct((2 * n * PACKED_SLABS, LANES), jnp.uint32),
        compiler_params=_params(("arbitrary",)),
        name="moe_dispatch",
    )(runs, h, g.reshape(1, d), meta)


ITEM_FIRST, ITEM_LAST, ITEM_NEW_EXPERT = 1, 2, 4


def _expert_kernel(tile_ref, exp_ref, lo_ref, hi_ref, flag_ref, wslot_ref, next_ref, n_ref,
                   xs_ref, wg_hbm, wu_hbm, wd_hbm, ys_ref,
                   wgf, wuf, wdf, wsem, wgb, wub, wdb, acc, *, tm, sub, layer):
    w = pl.program_id(0)

    @pl.when(w == 0)
    def _():
        acc[...] = jnp.zeros_like(acc)

    def weight_copies(e, slot):
        return [pltpu.make_async_copy(hbm.at[layer, e], buf.at[slot], wsem.at[slot])
                for hbm, buf in ((wg_hbm, wgf), (wu_hbm, wuf), (wd_hbm, wdf))]

    @pl.when(w < n_ref[0])
    def _():
        flags = flag_ref[w]

        @pl.when((flags & ITEM_NEW_EXPERT) != 0)
        def _():
            slot = wslot_ref[w]

            @pl.when(w == 0)
            def _():
                for c in weight_copies(exp_ref[w], slot):
                    c.start()

            for c in weight_copies(exp_ref[w], slot):
                c.wait()
            wgb[...] = wgf[slot].astype(BF16)
            wub[...] = wuf[slot].astype(BF16)
            wdb[...] = wdf[slot].astype(BF16)

            @pl.when(next_ref[w] >= 0)
            def _():
                for c in weight_copies(next_ref[w], 1 - slot):
                    c.start()

        first = (flags & ITEM_FIRST) != 0
        lo, hi = lo_ref[w], hi_ref[w]
        n_blocks = tm // sub

        def up(s):
            x = _unpack_rows(xs_ref, sub, row0=s * sub)
            return (jnp.dot(x, wgb[...], preferred_element_type=F32),
                    jnp.dot(x, wub[...], preferred_element_type=F32))

        ups = {0: up(0)}
        for s in range(n_blocks):
            if s + 1 < n_blocks:
                ups[s + 1] = up(s + 1)
            hg, hu = ups.pop(s)
            act = (hg * jax.nn.sigmoid(hg) * hu).astype(BF16)
            y = jnp.dot(act, wdb[...], preferred_element_type=F32)
            row = lax.broadcasted_iota(jnp.int32, (sub, 1), 0) + s * sub
            y = jnp.where((row >= lo) & (row < hi), y, 0.0)
            rows = slice(s * sub, (s + 1) * sub)
            total = jnp.where(first, y, acc[rows, :] + y)
            acc[rows, :] = total
            _pack_rows(ys_ref, total, sub, row0=s * sub)


def _experts(xs, items, w_g, w_u, w_d, layer):
    d = D_MODEL
    tm = MOE_TILE
    packed_spec = pl.BlockSpec((tm * PACKED_SLABS, LANES), lambda w, tile, *_: (tile[w], 0))
    any_spec = pl.BlockSpec(memory_space=pl.ANY)
    return pl.pallas_call(
        functools.partial(_expert_kernel, tm=tm, sub=MOE_SUB, layer=layer),
        grid_spec=pltpu.PrefetchScalarGridSpec(
            num_scalar_prefetch=len(items),
            grid=(items[0].shape[0],),
            in_specs=[packed_spec, any_spec, any_spec, any_spec],
            out_specs=packed_spec,
            scratch_shapes=[pltpu.VMEM((2, d, D_EXPERT), F32), pltpu.VMEM((2, d, D_EXPERT), F32),
                            pltpu.VMEM((2, D_EXPERT, d), F32), pltpu.SemaphoreType.DMA((2,)),
                            pltpu.VMEM((d, D_EXPERT), BF16), pltpu.VMEM((d, D_EXPERT), BF16),
                            pltpu.VMEM((D_EXPERT, d), BF16), pltpu.VMEM((tm, d), F32)],
        ),
        out_shape=jax.ShapeDtypeStruct(xs.shape, jnp.uint32),
        compiler_params=_params(("arbitrary",)),
        name="moe_experts",
    )(*items, xs, w_g, w_u, w_d)


def _combine_kernel(runs_ref, h_ref, gate_ref, ys_hbm, *refs, tm, steps, prompt_tiles):
    out_refs, (buf, sem) = refs[:-2], refs[-2:]
    i = pl.program_id(0)
    slot = lax.rem(i, 2)
    ns = 2 * tm

    def issue(step, sl):
        _copy_runs(runs_ref, step, lambda start, size: buf.at[sl, pl.ds(start, size)],
                   lambda start, size: ys_hbm.at[pl.ds(start, size)], sem.at[sl], to_global=False)

    @pl.when(i == 0)
    def _():
        issue(0, 0)

    @pl.when(i + 1 < steps)
    def _():
        issue(i + 1, 1 - slot)

    pltpu.make_async_copy(ys_hbm.at[pl.ds(0, ns * PACKED_SLABS)], buf.at[slot], sem.at[slot]).wait()
    y = _unpack_rows(buf, ns, (slot,))
    g = gate_ref[...]
    j = lax.broadcasted_iota(jnp.int32, (tm, ns), 1)
    out = h_ref[...]
    for kk in range(2):
        pick = jnp.where(j == g[:, 2 + kk:3 + kk].astype(jnp.int32), 1.0, 0.0).astype(BF16)
        out = out + g[:, kk:kk + 1] * jnp.dot(pick, y, preferred_element_type=F32)
    if len(out_refs) == 1:
        out_refs[0][...] = out
    else:
        @pl.when(i < prompt_tiles)
        def _():
            out_refs[0][...] = out

        @pl.when(i >= prompt_tiles)
        def _():
            out_refs[1][...] = out


def _combine(h, gates, ys, runs, tm, split_rows=None):
    n, d = h.shape
    steps = n // tm
    row = pl.BlockSpec((tm, d), lambda i, pos: (i, 0))
    if split_rows is None:
        pt, out_specs, out_shape = 0, row, jax.ShapeDtypeStruct((n, d), F32)
    else:
        assert split_rows % tm == 0
        pt = split_rows // tm
        out_specs = _group_specs(tm, d, pt)
        out_shape = [jax.ShapeDtypeStruct((split_rows, d), F32), jax.ShapeDtypeStruct((n - split_rows, d), F32)]
    return pl.pallas_call(
        functools.partial(_combine_kernel, tm=tm, steps=steps, prompt_tiles=pt),
        grid_spec=pltpu.PrefetchScalarGridSpec(
            num_scalar_prefetch=1,
            grid=(steps,),
            in_specs=[row, pl.BlockSpec((tm, LANES), lambda i, pos: (i, 0)),
                      pl.BlockSpec(memory_space=pl.ANY)],
            out_specs=out_specs,
            scratch_shapes=[pltpu.VMEM((2, 2 * tm * PACKED_SLABS, LANES), jnp.uint32),
                            pltpu.SemaphoreType.DMA((2,))],
        ),
        out_shape=out_shape,
        compiler_params=_params(("arbitrary",)),
        name="moe_combine",
    )(runs, h, gates, ys)


def _lookup(tables, idx):
    hit = idx[:, None] == jnp.arange(tables.shape[1], dtype=idx.dtype)[None, :]
    return jnp.sum(jnp.where(hit[None], tables[:, None, :], 0), axis=2)


def _work_items(counts, n_slots):
    tm = MOE_TILE
    assert n_slots % tm == 0, (n_slots, tm)
    ends = jnp.cumsum(counts)
    starts = ends - counts
    first_tile = starts // tm
    n_tiles_e = jnp.where(counts > 0, (ends - 1) // tm - first_tile + 1, 0)
    item_end = jnp.cumsum(n_tiles_e)
    item_start = item_end - n_tiles_e
    n_items = item_end[-1]
    max_items = n_slots // tm + N_EXPERTS - 1
    w = jnp.minimum(jnp.arange(max_items, dtype=jnp.int32), n_items - 1)
    expert = jnp.sum(w[:, None] >= item_end[None, :], axis=1).astype(jnp.int32)
    first_tile_w, item_start_w, start_w, end_w = _lookup(jnp.stack([first_tile, item_start, starts, ends]), expert)
    tile = first_tile_w + w - item_start_w
    lo = jnp.maximum(start_w - tile * tm, 0)
    hi = jnp.minimum(end_w - tile * tm, tm)
    prev_tile = jnp.concatenate([jnp.full((1,), -1, jnp.int32), tile[:-1]])
    next_tile = jnp.concatenate([tile[1:], jnp.full((1,), -1, jnp.int32)])
    prev_expert = jnp.concatenate([jnp.full((1,), -1, jnp.int32), expert[:-1]])
    idx = jnp.arange(max_items, dtype=jnp.int32)
    new_expert = expert != prev_expert
    flags = (jnp.where(tile != prev_tile, ITEM_FIRST, 0)
             | jnp.where((tile != next_tile) | (idx == n_items - 1), ITEM_LAST, 0)
             | jnp.where(new_expert, ITEM_NEW_EXPERT, 0))
    weight_slot = (jnp.cumsum(new_expert.astype(jnp.int32)) - 1) % 2
    ids = jnp.arange(N_EXPERTS, dtype=jnp.int32)
    later = (ids[None, :] > ids[:, None]) & (n_tiles_e[None, :] > 0)
    following = jnp.min(jnp.where(later, ids[None, :], N_EXPERTS), axis=1)
    following = jnp.where(following == N_EXPERTS, -1, following)
    next_expert, = _lookup(following[None, :], expert)
    as_i32 = lambda a: a.astype(jnp.int32)
    return (as_i32(tile), expert, as_i32(lo), as_i32(hi), as_i32(flags), as_i32(weight_slot),
            as_i32(next_expert), as_i32(n_items).reshape(1))


def _moe(h, routed, g, w_g, w_u, w_d, layer, tm, split_rows=None):
    n, _ = h.shape
    meta, gates, cnt, tables = routed
    experts = slice(ROUTER_LANE0, ROUTER_LANE0 + N_EXPERTS)
    counts = cnt[0, experts].astype(jnp.int32)
    starts = jnp.cumsum(counts) - counts
    tables = tables.reshape(n // tm, 8, LANES)[:, :RUN_FIELDS, experts]
    runs = tables.at[:, 0, :].add(starts[None, :]).reshape(-1)
    xs = _dispatch(h, g, meta, runs, tm)
    ys = _experts(xs, _work_items(counts, 2 * n), w_g, w_u, w_d, layer)
    return _combine(h, gates, ys, runs, tm, split_rows)


def _inproj_b_kernel(x_ref, gkv_ref, gmix_ref, wkv_ref, win_ref, kng_ref, qng_ref, bdk_ref, bdq_ref,
                     q_ref, mq_ref, k_ref, v_ref, kt_ref):
    x = x_ref[...]
    xr = x * _rms_scale(x)
    kv = _bdot(xr * gkv_ref[...], wkv_ref[...])
    k = kv[:, :KV_WIDTH]
    k = k * lax.rsqrt(_seg_mean(k * k, bdk_ref[...]) + EPS) * kng_ref[...]
    k_ref[...] = k
    kt_ref[...] = k.T.astype(BF16)
    v_ref[...] = kv[:, KV_WIDTH:]
    proj = _bdot(xr * gmix_ref[...], win_ref[...])
    q = proj[:, :MAIN_WIDTH]
    q_ref[...] = (q * lax.rsqrt(_seg_mean(q * q, bdq_ref[...]) + EPS) * qng_ref[...]).astype(BF16)
    mq_ref[...] = proj[:, MAIN_WIDTH:]


def _swa_perm():
    g, kh, dd = np.meshgrid(np.arange(SWA_GROUP), np.arange(SWA_KV_HEADS), np.arange(HEAD_DIM), indexing="ij")
    return ((kh * SWA_GROUP + g) * HEAD_DIM + dd).reshape(-1)


def _inproj_b(x, g_kv, g_mix, w_kv, w_in, kng, qng):
    n, d = x.shape
    tm = _row_tile(n)
    perm = _swa_perm()
    w_in_p = jnp.concatenate([w_in[:, :MAIN_WIDTH][:, perm], w_in[:, MAIN_WIDTH:]], axis=1).astype(BF16)
    qng_t = (jnp.tile(qng, SWA_HEADS) * HEAD_DIM ** -0.5).reshape(1, MAIN_WIDTH)
    row = lambda w: pl.BlockSpec((tm, w), lambda i: (i, 0))
    return pl.pallas_call(
        _inproj_b_kernel,
        grid=(n // tm,),
        in_specs=[row(d), _const((1, d)), _const((1, d)), _const((d, 2 * KV_WIDTH)), _const((d, d)),
                  _const((1, KV_WIDTH)), _const((1, MAIN_WIDTH)), _const((KV_WIDTH, KV_WIDTH)),
                  _const((MAIN_WIDTH, MAIN_WIDTH))],
        out_specs=[row(MAIN_WIDTH), row(MEM_WIDTH), row(KV_WIDTH), row(KV_WIDTH),
                   pl.BlockSpec((KV_WIDTH, tm), lambda i: (0, i))],
        out_shape=[jax.ShapeDtypeStruct((n, MAIN_WIDTH), BF16), jax.ShapeDtypeStruct((n, MEM_WIDTH), F32),
                   jax.ShapeDtypeStruct((n, KV_WIDTH), F32), jax.ShapeDtypeStruct((n, KV_WIDTH), F32),
                   jax.ShapeDtypeStruct((KV_WIDTH, n), BF16)],
        compiler_params=_params(("parallel",)),
        name="inproj_b",
    )(x, g_kv.reshape(1, d), g_mix.reshape(1, d), w_kv.astype(BF16), w_in_p,
      jnp.tile(kng, SWA_KV_HEADS).reshape(1, KV_WIDTH), qng_t,
      _block_diag_mean(KV_WIDTH), _block_diag_mean(MAIN_WIDTH))


def _softmax_with_sink(s, sink):
    m = jnp.maximum(jnp.max(s, axis=-1, keepdims=True), sink)
    e = jnp.exp(s - m)
    r = 1.0 / (jnp.sum(e, axis=-1, keepdims=True) + jnp.exp(sink - m))
    return (e * r).astype(BF16)


def _swa_bias(tq):
    slopes = 2.0 ** (-8.0 * np.arange(1, SWA_HEADS + 1, dtype=np.float64) / SWA_HEADS)
    dist = np.arange(tq)[:, None] + WINDOW - np.arange(WINDOW + tq)[None, :]
    valid = (dist >= 0) & (dist <= WINDOW)
    return np.stack([np.where(valid, -s * dist, NEG_BIG) for s in slopes]).astype(np.float32)


def _swa_prompt_kernel(sink_ref, q_ref, ktp_ref, kto_ref, vp_ref, vo_ref, bias_ref, hm_ref, o_ref, *, nb):
    w = WINDOW
    key = lax.broadcasted_iota(jnp.int32, (w, 2 * w), 1)
    has_prev = (pl.program_id(0) > 0) | (key >= w)
    heads = [(g, kh) for g in range(SWA_GROUP) for kh in range(SWA_KV_HEADS)]
    kts, vvs = [], []
    for b in range(nb):
        kt_prev = ktp_ref[...] if b == 0 else kto_ref[:, (b - 1) * w:b * w]
        v_prev = vp_ref[...] if b == 0 else vo_ref[(b - 1) * w:b * w, :]
        kts.append(jnp.concatenate([kt_prev, kto_ref[:, b * w:(b + 1) * w]], axis=1).astype(BF16))
        vvs.append(jnp.concatenate([v_prev, vo_ref[b * w:(b + 1) * w, :]], axis=0).astype(BF16))
    scores = [[jnp.dot(q_ref[b * w:(b + 1) * w, g * KV_WIDTH:(g + 1) * KV_WIDTH] * hm_ref[kh].astype(BF16),
                       kts[b], preferred_element_type=F32) for g, kh in heads] for b in range(nb)]
    for b in range(nb):
        probs = []
        for (g, kh), s in zip(heads, scores[b]):
            h = kh * SWA_GROUP + g
            s = s + bias_ref[h]
            if b == 0:
                s = jnp.where(has_prev, s, NEG_BIG)
            probs.append(_softmax_with_sink(s, sink_ref[h]))
        outs = [jnp.dot(p, vvs[b], preferred_element_type=F32) for p in probs]
        for g in range(SWA_GROUP):
            acc = None
            for (cg, kh), o in zip(heads, outs):
                if cg == g:
                    t = o * hm_ref[kh]
                    acc = t if acc is None else acc + t
            o_ref[b * w:(b + 1) * w, g * KV_WIDTH:(g + 1) * KV_WIDTH] = acc.astype(BF16)


def _swa_prompt(q, kt, v, sinks, *, n_rows, nb):
    w = WINDOW
    step = nb * w
    assert n_rows % step == 0
    prev = lambda j, sink: jnp.maximum(j * nb - 1, 0)
    return pl.pallas_call(
        functools.partial(_swa_prompt_kernel, nb=nb),
        grid_spec=pltpu.PrefetchScalarGridSpec(
            num_scalar_prefetch=1,
            grid=(n_rows // step,),
            in_specs=[pl.BlockSpec((step, MAIN_WIDTH), lambda j, sink: (j, 0)),
                      pl.BlockSpec((KV_WIDTH, w), lambda j, sink: (0, prev(j, sink))),
                      pl.BlockSpec((KV_WIDTH, step), lambda j, sink: (0, j)),
                      pl.BlockSpec((w, KV_WIDTH), lambda j, sink: (prev(j, sink), 0)),
                      pl.BlockSpec((step, KV_WIDTH), lambda j, sink: (j, 0)),
                      pl.BlockSpec((SWA_HEADS, w, 2 * w), lambda j, sink: (0, 0, 0)),
                      pl.BlockSpec((SWA_KV_HEADS, 1, KV_WIDTH), lambda j, sink: (0, 0, 0))],
            out_specs=pl.BlockSpec((step, MAIN_WIDTH), lambda j, sink: (j, 0)),
        ),
        out_shape=jax.ShapeDtypeStruct((n_rows, MAIN_WIDTH), BF16),
        compiler_params=_params(("arbitrary",)),
        name="swa_prompt",
    )(sinks.astype(F32), q, kt, kt, v, v, jnp.asarray(_swa_bias(w)), _head_masks(SWA_KV_HEADS))


def _swa_sample_kernel(q_ref, kp_ref, ko_ref, vp_ref, vo_ref, bias_ref, sink_ref, hm_ref, o_ref, *, nb, tq):
    w = WINDOW
    heads = [(kh, g) for kh in range(SWA_KV_HEADS) for g in range(SWA_GROUP)]
    kks, vvs, scores = [], [], []
    q = q_ref[...].astype(F32)
    for i in range(nb):
        win = slice(i * KV_WIDTH, (i + 1) * KV_WIDTH)
        kks.append(jnp.concatenate([kp_ref[win, :].T, ko_ref[i * tq:(i + 1) * tq, :]], axis=0))
        vvs.append(jnp.concatenate([vp_ref[win, :].T, vo_ref[i * tq:(i + 1) * tq, :]], axis=0))
        qs = jnp.concatenate([q[i * tq:(i + 1) * tq, g * KV_WIDTH:(g + 1) * KV_WIDTH] * hm_ref[kh]
                              for kh, g in heads], axis=0)
        scores.append(_bdot_nt(qs, kks[i]))
    probs = [_softmax_with_sink(s + bias_ref[...], sink_ref[...]) for s in scores]
    outs = [_bdot(p, vv) for p, vv in zip(probs, vvs)]
    for g in range(SWA_GROUP):
        rows = []
        for i in range(nb):
            acc = None
            for r, (kh, hg) in enumerate(heads):
                if hg == g:
                    t = outs[i][r * tq:(r + 1) * tq] * hm_ref[kh]
                    acc = t if acc is None else acc + t
            rows.append(acc)
        o_ref[:, g * KV_WIDTH:(g + 1) * KV_WIDTH] = jnp.concatenate(rows, axis=0).astype(BF16)


def _swa_sample(q, k_win, v_win, k, v, sinks, *, row_off, batch, tq, nb):
    w = WINDOW
    assert batch % nb == 0 and row_off % (nb * tq) == 0
    off = row_off // (nb * tq)
    bias = jnp.asarray(_swa_bias(tq).reshape(SWA_HEADS * tq, w + tq))
    sink_col = jnp.repeat(sinks.astype(F32), tq).reshape(SWA_HEADS * tq, 1)
    own = lambda width: pl.BlockSpec((nb * tq, width), lambda b: (off + b, 0))
    win = pl.BlockSpec((nb * KV_WIDTH, w), lambda b: (b, 0))
    return pl.pallas_call(
        functools.partial(_swa_sample_kernel, nb=nb, tq=tq),
        grid=(batch // nb,),
        in_specs=[own(MAIN_WIDTH), win, own(KV_WIDTH), win, own(KV_WIDTH), _const(bias.shape),
                  _const(sink_col.shape), _const((SWA_KV_HEADS, 1, KV_WIDTH))],
        out_specs=pl.BlockSpec((nb * tq, MAIN_WIDTH), lambda b: (b, 0)),
        out_shape=jax.ShapeDtypeStruct((batch * tq, MAIN_WIDTH), BF16),
        compiler_params=_params(("arbitrary",)),
        name="swa_sample",
    )(q, k_win, k, v_win, v, bias, sink_col, _head_masks(SWA_KV_HEADS))


def kernel(x_prompt, x_sample, state_gla, cache_win_k, cache_win_v, cache_mem_k, cache_mem_v, mem_prompt, norm_mix_g, norm_ffn_g, norm_mem_g, w_mem_kv, mem_qn_g, mem_kn_g, w_out, w_in_a, w_gate_lr, b_gate_lr, gla_norm_g, w_in_b, swa_qn_g, swa_sinks, norm_kv_g, w_kv, swa_kn_g, w_router_group, b_router_group, w_router_expert, b_router_expert, w_exp_gate, w_exp_up, w_exp_down):
    bp, tp, d = x_prompt.shape
    bs, ts, _ = x_sample.shape
    assert bp == 1 and tp % WINDOW == 0 and ts * (GLA_CHUNK // ts) == GLA_CHUNK
    n_p, n_s = bp * tp, bs * ts
    w_buf = cache_win_k.shape[1]
    assert w_buf == WINDOW
    x_p, x_s = x_prompt.reshape(n_p, d), x_sample.reshape(n_s, d)

    mem_k_p, mem_v_p = _mem_kv(mem_prompt, norm_mem_g, w_mem_kv, mem_kn_g)
    feature_major = lambda c: jnp.moveaxis(c, -3, -1).reshape(*c.shape[:-3], c.shape[-2] * c.shape[-1], c.shape[-3])
    cmk, cmv = feature_major(cache_mem_k), feature_major(cache_mem_v)

    def mem_attend(mq, l):
        tm_p = _row_tile(tp, 512)
        mo_p = _mem_attn(mq, mem_k_p, mem_v_p, mem_qn_g[l], row_off=0, seq=tp, tm=tm_p, bb=1, layer=l)
        mo_s = _mem_attn(mq, cmk, cmv, mem_qn_g[l], row_off=n_p, seq=ts, tm=ts, bb=8, layer=l)
        return mo_p, mo_s

    tm = _row_tile(math.gcd(n_p, n_s), MOE_TILE)
    router = lambda l: (norm_ffn_g[l], w_router_group[l], b_router_group[l], w_router_expert[l],
                        b_router_expert[l])

    def moe(h, routed, l, split_rows=None):
        return _moe(h, routed, norm_ffn_g[l], w_exp_gate, w_exp_up, w_exp_down, l, tm, split_rows)

    q, k, la, v, og, mq = _inproj_a(x_p, x_s, norm_mix_g[0], w_in_a[0], w_gate_lr[0], b_gate_lr[0])
    zero_state = jnp.zeros((bp, GLA_HEADS, GLA_DK, GLA_DV), F32)
    n_sub = max(1, min(8, tp // GLA_CHUNK))
    main_p, gla_p = _gla(q, k, la, v, og, zero_state, gla_norm_g[0], row_off=0, seq=tp, n_seg=1, n_sub=n_sub)
    main_s, gla_s = _gla(q, k, la, v, og, state_gla[0], gla_norm_g[0], row_off=n_p, seq=ts,
                         n_seg=GLA_CHUNK // ts, n_sub=1)
    mo_p, mo_s = mem_attend(mq, 0)
    w_o = w_out[0]
    h, routed = _outproj((x_p, x_s), main_p, mo_p, main_s, mo_s, w_o[:MAIN_WIDTH].reshape(GLA_HEADS, GLA_DV, d),
                         w_o[MAIN_WIDTH:], router(0), tm)
    h = moe(h, routed, 0)

    q, mq, k_sh, v_sh, kt_sh = _inproj_b(h, norm_kv_g, norm_mix_g[1], w_kv, w_in_b[0], swa_kn_g, swa_qn_g[0])
    ck = feature_major(cache_win_k).reshape(bs * KV_WIDTH, w_buf)
    cv = feature_major(cache_win_v).reshape(bs * KV_WIDTH, w_buf)
    main_p = _swa_prompt(q, kt_sh, v_sh, swa_sinks[0], n_rows=n_p, nb=4)
    main_s = _swa_sample(q, ck, cv, k_sh, v_sh, swa_sinks[0], row_off=n_p, batch=bs, tq=ts, nb=8)
    mo_p, mo_s = mem_attend(mq, 1)
    w_o = w_out[1]
    h, routed = _outproj(h, main_p, mo_p, main_s, mo_s, w_o[:MAIN_WIDTH][_swa_perm()], w_o[MAIN_WIDTH:],
                         router(1), tm)
    y_p, y_s = moe(h, routed, 1, split_rows=n_p)

    y_prompt = y_p.reshape(bp, tp, d)
    y_sample = y_s.reshape(bs, ts, d)
    k_new = k_sh[n_p:].reshape(bs, ts, SWA_KV_HEADS, HEAD_DIM)
    v_new = v_sh[n_p:].reshape(bs, ts, SWA_KV_HEADS, HEAD_DIM)
    win_k_s = jnp.concatenate([cache_win_k, k_new], axis=1)[:, -w_buf:]
    win_v_s = jnp.concatenate([cache_win_v, v_new], axis=1)[:, -w_buf:]
    win_k_p = k_sh[n_p - WINDOW:n_p].reshape(bp, WINDOW, SWA_KV_HEADS, HEAD_DIM)
    win_v_p = v_sh[n_p - WINDOW:n_p].reshape(bp, WINDOW, SWA_KV_HEADS, HEAD_DIM)
    token_major = lambda c: jnp.moveaxis(c.reshape(*c.shape[:-2], MEM_HEADS, HEAD_DIM, c.shape[-1]), -1, -3)
    return (y_prompt, y_sample, gla_p[None], gla_s[None], win_k_p, win_v_p, win_k_s, win_v_s,
            token_major(mem_k_p), token_major(mem_v_p))
```

```python
import functools
import math

import numpy as np
import jax
import jax.numpy as jnp
from jax import lax
from jax.experimental import pallas as pl
from jax.experimental.pallas import tpu as pltpu

F32 = jnp.float32
BF16 = jnp.bfloat16

D_MODEL = 1024
MEM_LEN = 256
MEM_HEADS = 4
HEAD_DIM = 64
MEM_WIDTH = MEM_HEADS * HEAD_DIM
MAIN_WIDTH = D_MODEL - MEM_WIDTH
GLA_HEADS = 4
GLA_DV = MAIN_WIDTH // GLA_HEADS
GLA_DK = GLA_DV // 2
GLA_DK_PAD = 128
GLA_KEY_WIDTH = GLA_HEADS * GLA_DK
GLA_KEY_PAD = GLA_HEADS * GLA_DK_PAD
GLA_GATE_RANK = 16
GLA_TAU = 16.0
GLA_CHUNK = 64
SWA_HEADS = MAIN_WIDTH // HEAD_DIM
SWA_KV_HEADS = 4
SWA_GROUP = SWA_HEADS // SWA_KV_HEADS
KV_WIDTH = SWA_KV_HEADS * HEAD_DIM
WINDOW = 128
N_GROUPS = 4
EXPERTS_PER_GROUP = 8
N_EXPERTS = N_GROUPS * EXPERTS_PER_GROUP
D_EXPERT = 512
EPS = 1e-6
LANES = 128
NEG_BIG = -1e30
VMEM_LIMIT = 56 * 1024 * 1024
MOE_TILE = 512
MOE_SUB = 128
ROUTER_LANE0 = N_GROUPS
ROUTER_META_ROWS = 8
ROUTER_ROWS = 40
SLABS = D_MODEL // LANES
PACKED_SLABS = SLABS // 2


def _bdot(a, b):
    return jnp.dot(a.astype(BF16), b.astype(BF16), preferred_element_type=F32)


def _bdot_nt(a, b):
    return lax.dot_general(a.astype(BF16), b.astype(BF16), (((1,), (1,)), ((), ())),
                           preferred_element_type=F32)


def _bdot_tn(a, b):
    return lax.dot_general(a.astype(BF16), b.astype(BF16), (((0,), (0,)), ((), ())),
                           preferred_element_type=F32)


def _split(x, n):
    parts = []
    for _ in range(n - 1):
        p = x.astype(BF16)
        parts.append(p)
        x = x - p.astype(F32)
    parts.append(x.astype(BF16))
    return parts


def _exact_left_dot(m, x, n=2):
    out = None
    for p in _split(x, n):
        t = jnp.dot(m, p, preferred_element_type=F32)
        out = t if out is None else out + t
    return out


def _seg_mean(x2, bd):
    out = None
    for p in _split(x2, 2):
        t = jnp.dot(p, bd, preferred_element_type=F32)
        out = t if out is None else out + t
    return out


def _rms_scale(x):
    return lax.rsqrt(jnp.mean(x * x, axis=-1, keepdims=True) + EPS)


def _row_tile(n, cap=512):
    t = cap
    while t > 8 and n % t:
        t //= 2
    assert n % t == 0, n
    return t


def _params(sem):
    return pltpu.CompilerParams(dimension_semantics=sem, vmem_limit_bytes=VMEM_LIMIT)


def _const(shape):
    nd = len(shape)
    return pl.BlockSpec(shape, lambda *_: (0,) * nd)


def _group_specs(tm, width, prompt_tiles, lead=None):
    p_idx = lambda i, *_: jnp.minimum(i, prompt_tiles - 1)
    s_idx = lambda i, *_: jnp.maximum(i - prompt_tiles, 0)
    if lead is None:
        return [pl.BlockSpec((tm, width), lambda i, *_, f=f: (f(i), 0)) for f in (p_idx, s_idx)]
    return [pl.BlockSpec((lead, tm, width), lambda i, *_, f=f: (0, f(i), 0)) for f in (p_idx, s_idx)]


def _block_diag_mean(width):
    i = np.arange(width)
    return jnp.asarray((i[:, None] // HEAD_DIM == i[None, :] // HEAD_DIM) / HEAD_DIM, BF16)


def _head_masks(n_heads):
    i = np.arange(n_heads * HEAD_DIM)
    return jnp.asarray((i[None, :] // HEAD_DIM == np.arange(n_heads)[:, None]), F32)[:, None, :]


def _mem_kv_kernel(mem_ref, g_ref, w_ref, kng_ref, bd_ref, k_ref, v_ref):
    x = mem_ref[0]
    hn = x * _rms_scale(x) * g_ref[0]
    kv = _bdot(hn, w_ref[0])
    k = kv[:, :MEM_WIDTH]
    k = k * lax.rsqrt(_seg_mean(k * k, bd_ref[...]) + EPS) * kng_ref[0]
    k_ref[0, 0] = k.T
    v_ref[0, 0] = kv[:, MEM_WIDTH:].T


def _mem_kv(mem, g, w, kng):
    depth, (b, m, d) = w.shape[0], mem.shape
    out = jax.ShapeDtypeStruct((depth, b, m, MEM_WIDTH), F32)
    blk = pl.BlockSpec((1, 1, m, MEM_WIDTH), lambda l, i: (l, i, 0, 0))
    return pl.pallas_call(
        _mem_kv_kernel,
        grid=(depth, b),
        in_specs=[pl.BlockSpec((1, m, d), lambda l, i: (i, 0, 0)),
                  pl.BlockSpec((1, 1, d), lambda l, i: (l, 0, 0)),
                  pl.BlockSpec((1, d, 2 * MEM_WIDTH), lambda l, i: (l, 0, 0)),
                  pl.BlockSpec((1, 1, MEM_WIDTH), lambda l, i: (l, 0, 0)),
                  _const((MEM_WIDTH, MEM_WIDTH))],
        out_specs=[blk, blk],
        out_shape=[out, out],
        compiler_params=_params(("arbitrary", "arbitrary")),
        name="mem_kv",
    )(mem, g.reshape(depth, 1, d), w.astype(BF16),
      jnp.tile(kng, (1, MEM_HEADS)).reshape(depth, 1, MEM_WIDTH), _block_diag_mean(MEM_WIDTH))


def _inproj_a_kernel(xp_ref, xs_ref, g_ref, wq_ref, wk_ref, wv_ref, wog_ref, wlr_ref, wmq_ref, wgl_ref, bgl_ref,
                     q_ref, k_ref, la_ref, v_ref, og_ref, mq_ref, *, prompt_tiles):
    x = jnp.where(pl.program_id(0) < prompt_tiles, xp_ref[...], xs_ref[...])
    hn = (x * _rms_scale(x) * g_ref[...]).astype(BF16)
    q_ref[...] = jnp.dot(hn, wq_ref[...], preferred_element_type=F32) * (GLA_DK ** -0.5)
    k_ref[...] = jnp.dot(hn, wk_ref[...], preferred_element_type=F32)
    for h in range(GLA_HEADS):
        v_ref[h] = jnp.dot(hn, wv_ref[h], preferred_element_type=F32).astype(BF16)
        og_ref[h] = jnp.dot(hn, wog_ref[h], preferred_element_type=F32)
    lr = jnp.dot(hn, wlr_ref[...], preferred_element_type=F32)
    z = _bdot(lr, wgl_ref[...]) + bgl_ref[...]
    la_ref[...] = (jnp.minimum(z, 0.0) - jnp.log(1.0 + jnp.exp(-jnp.abs(z)))) * (1.0 / GLA_TAU)
    mq_ref[...] = jnp.dot(hn, wmq_ref[...], preferred_element_type=F32)


def _pad_heads(w, width, pad):
    lead = w.shape[:-1]
    w = w.reshape(*lead, GLA_HEADS, width)
    w = jnp.pad(w, [(0, 0)] * len(lead) + [(0, 0), (0, pad - width)])
    return w.reshape(*lead, GLA_HEADS * pad)


def _inproj_a(x_p, x_s, g, w_in, w_lr, b_lr):
    (n_p, d), n_s = x_p.shape, x_s.shape[0]
    n = n_p + n_s
    tm = _row_tile(n_s)
    assert n_p % tm == 0
    pt = n_p // tm
    c0, c1, c2, c3, c4 = (GLA_KEY_WIDTH, 2 * GLA_KEY_WIDTH, 2 * GLA_KEY_WIDTH + MAIN_WIDTH,
                          2 * GLA_KEY_WIDTH + 2 * MAIN_WIDTH,
                          2 * GLA_KEY_WIDTH + 2 * MAIN_WIDTH + GLA_GATE_RANK)
    wb = w_in.astype(BF16)
    wq = _pad_heads(wb[:, :c0], GLA_DK, GLA_DK_PAD)
    wk = _pad_heads(wb[:, c0:c1], GLA_DK, GLA_DK_PAD)
    wv = wb[:, c1:c2].reshape(d, GLA_HEADS, GLA_DV).transpose(1, 0, 2)
    wog = wb[:, c2:c3].reshape(d, GLA_HEADS, GLA_DV).transpose(1, 0, 2)
    wlr = jnp.pad(wb[:, c3:c4], ((0, 0), (0, LANES - GLA_GATE_RANK)))
    wmq = wb[:, c4:]
    wgl = jnp.pad(_pad_heads(w_lr.astype(BF16), GLA_DK, GLA_DK_PAD), ((0, LANES - GLA_GATE_RANK), (0, 0)))
    bgl = _pad_heads(b_lr.reshape(1, -1), GLA_DK, GLA_DK_PAD)
    row = lambda w: pl.BlockSpec((tm, w), lambda i: (i, 0))
    hrow = pl.BlockSpec((GLA_HEADS, tm, GLA_DV), lambda i: (0, i, 0))
    key = jax.ShapeDtypeStruct((n, GLA_KEY_PAD), F32)
    val = jax.ShapeDtypeStruct((GLA_HEADS, n, GLA_DV), F32)
    return pl.pallas_call(
        functools.partial(_inproj_a_kernel, prompt_tiles=pt),
        grid=(n // tm,),
        in_specs=_group_specs(tm, d, pt) + [
            _const((1, d)), _const(wq.shape), _const(wk.shape), _const(wv.shape),
            _const(wog.shape), _const(wlr.shape), _const(wmq.shape), _const(wgl.shape),
            _const(bgl.shape)],
        out_specs=[row(GLA_KEY_PAD), row(GLA_KEY_PAD), row(GLA_KEY_PAD), hrow, hrow, row(MEM_WIDTH)],
        out_shape=[key, key, key, jax.ShapeDtypeStruct(val.shape, BF16), val,
                   jax.ShapeDtypeStruct((n, MEM_WIDTH), F32)],
        compiler_params=_params(("parallel",)),
        name="inproj_a",
    )(x_p, x_s, g.reshape(1, d), wq, wk, wv, wog, wlr, wmq, wgl, bgl)


def _gla_kernel(q_ref, k_ref, la_ref, v_ref, og_ref, s0_ref, gn_ref, mcum_ref, mall_ref, sel_ref,
                o_ref, sout_ref, s_ref, *, chunk, n_sub, n_seg):
    j = pl.program_id(1)
    seg = chunk // n_seg

    @pl.when(j == 0)
    def _():
        s_ref[...] = jnp.zeros_like(s_ref)
        s_ref[:, :, :GLA_DK, :] = s0_ref[...]

    mcum = mcum_ref[...]
    causal = mcum.astype(F32) > 0.0
    row = lax.broadcasted_iota(jnp.int32, (chunk, GLA_DK_PAD), 0)
    gn = gn_ref[...]
    hcols = [slice(h * GLA_DK_PAD, (h + 1) * GLA_DK_PAD) for h in range(GLA_HEADS)]
    crows = [slice(c * chunk, (c + 1) * chunk) for c in range(n_sub)]
    qts, kts, kds, e_ends = [], [], [], []
    for rows in crows:
        la = la_ref[rows, :]
        b = _exact_left_dot(mcum, la)
        if n_seg == 1:
            b_end = b[chunk - 1:chunk, :]
            e_ends.append(jnp.broadcast_to(jnp.exp(b_end), (LANES, b.shape[1])).T)
        else:
            b_end = _exact_left_dot(mall_ref[...], la)
            e_ends.append(jnp.exp(_exact_left_dot(sel_ref[...], la)).T)
        k = k_ref[rows, :]
        qts.append(q_ref[rows, :] * jnp.exp(b))
        kts.append((k * jnp.exp(-b)).astype(BF16))
        kds.append(k * jnp.exp(b_end - b))
    vbs = [[v_ref[h, rows, :].astype(BF16) for h in range(GLA_HEADS)] for rows in crows]
    scores = [[_bdot_nt(qts[c][:, cols], kts[c][:, cols]) for cols in hcols] for c in range(n_sub)]
    kvs = []
    for c in range(n_sub):
        per_head = []
        for h, cols in enumerate(hcols):
            per_seg = []
            for s in range(n_seg):
                kd = kds[c][:, cols]
                if n_seg > 1:
                    kd = jnp.where((row >= s * seg) & (row < (s + 1) * seg), kd, 0.0)
                per_seg.append(_bdot_tn(kd, vbs[c][h]))
            per_head.append(per_seg)
        kvs.append(per_head)
    state = [[s_ref[s, h] for s in range(n_seg)] for h in range(GLA_HEADS)]
    inters = []
    for c in range(n_sub):
        per_head = []
        for h, cols in enumerate(hcols):
            parts = []
            for s in range(n_seg):
                parts.append(_bdot(qts[c][s * seg:(s + 1) * seg, cols], state[h][s]))
                state[h][s] = e_ends[c][cols, s:s + 1] * state[h][s] + kvs[c][h][s]
            per_head.append(parts[0] if n_seg == 1 else jnp.concatenate(parts, axis=0))
        inters.append(per_head)
    for h in range(GLA_HEADS):
        for s in range(n_seg):
            s_ref[s, h] = state[h][s]
    for c, rows in enumerate(crows):
        for h in range(GLA_HEADS):
            a = jnp.where(causal, scores[c][h], 0.0)
            o = _bdot(a, vbs[c][h]) + inters[c][h]
            on = o * lax.rsqrt(jnp.mean(o * o, axis=-1, keepdims=True) + EPS) * gn
            og = og_ref[h, rows, :]
            o_ref[h, rows, :] = (on * (og * jax.nn.sigmoid(og))).astype(BF16)

    @pl.when(j == pl.num_programs(1) - 1)
    def _():
        sout_ref[...] = s_ref[:, :, :GLA_DK, :]


def _gla(q, k, la, v, og, s0, gnorm, *, row_off, seq, n_seg, n_sub):
    batch = s0.shape[0]
    chunk = GLA_CHUNK
    assert chunk % n_seg == 0 and batch % n_seg == 0
    seg = chunk // n_seg
    step_rows = n_sub * chunk
    if n_seg > 1:
        assert seq == seg and n_sub == 1
        t_steps = 1
    else:
        assert seq % step_rows == 0
        t_steps = seq // step_rows
    assert row_off % step_rows == 0
    off = row_off // step_rows
    i = np.arange(chunk)
    same = (i[:, None] // seg) == (i[None, :] // seg)
    mcum = jnp.asarray(same & (i[None, :] <= i[:, None]), BF16)
    mall = jnp.asarray(same, BF16)
    sel = jnp.asarray((i[None, :] // seg) == np.arange(LANES)[:, None], BF16)
    ridx = lambda g, j: (off + g * t_steps + j, 0)
    hidx = lambda g, j: (0, off + g * t_steps + j, 0)
    key_spec = pl.BlockSpec((step_rows, GLA_KEY_PAD), ridx)
    val_spec = pl.BlockSpec((GLA_HEADS, step_rows, GLA_DV), hidx)
    st_spec = pl.BlockSpec((n_seg, GLA_HEADS, GLA_DK, GLA_DV), lambda g, j: (g, 0, 0, 0))
    in_specs = [key_spec, key_spec, key_spec, val_spec, val_spec, st_spec, _const((1, GLA_DV)),
                _const((chunk, chunk)), _const((chunk, chunk)), _const((LANES, chunk))]
    args = [q, k, la, v, og, s0, gnorm.reshape(1, GLA_DV), mcum, mall, sel]
    out_spec = pl.BlockSpec((GLA_HEADS, step_rows, GLA_DV), lambda g, j: (0, g * t_steps + j, 0))
    return pl.pallas_call(
        functools.partial(_gla_kernel, chunk=chunk, n_sub=n_sub, n_seg=n_seg),
        grid=(batch // n_seg, t_steps),
        in_specs=in_specs,
        out_specs=[out_spec, st_spec],
        out_shape=[jax.ShapeDtypeStruct((GLA_HEADS, batch * seq, GLA_DV), BF16),
                   jax.ShapeDtypeStruct(s0.shape, F32)],
        scratch_shapes=[pltpu.VMEM((n_seg, GLA_HEADS, GLA_DK_PAD, GLA_DV), F32)],
        compiler_params=_params(("arbitrary", "arbitrary")),
        name="gla",
    )(*args)


def _mem_attn_kernel(q_ref, k_ref, v_ref, g_ref, bd_ref, hm_ref, o_ref, *, tm, bb):
    g = g_ref[...]
    sub = min(tm, 128)
    units = [(i, i * tm + r) for i in range(bb) for r in range(0, tm, sub)]
    scores = []
    for i, r in units:
        q = q_ref[r:r + sub, :]
        qn = q * lax.rsqrt(_seg_mean(q * q, bd_ref[...]) + EPS) * g
        qs = jnp.concatenate([(qn * hm_ref[h]).astype(BF16) for h in range(MEM_HEADS)], axis=0)
        scores.append(_bdot(qs, k_ref[i]))
    probs = []
    for s in scores:
        e = jnp.exp(s - jnp.max(s, axis=-1, keepdims=True))
        probs.append(e * (1.0 / jnp.sum(e, axis=-1, keepdims=True)))
    outs = [_bdot_nt(p, v_ref[i]) for (i, _), p in zip(units, probs)]
    rows = []
    for o in outs:
        acc = o[:sub] * hm_ref[0]
        for h in range(1, MEM_HEADS):
            acc = acc + o[h * sub:(h + 1) * sub] * hm_ref[h]
        rows.append(acc)
    o_ref[...] = jnp.concatenate(rows, axis=0).astype(BF16)


def _mem_attn(mq, mk, mv, qng, *, row_off, seq, tm, bb, layer):
    depth, batch, m, _ = mk.shape
    mk = mk.reshape(depth * batch, m, MEM_WIDTH)
    mv = mv.reshape(depth * batch, m, MEM_WIDTH)
    kv_off = layer * batch // bb
    assert seq % tm == 0 and batch % bb == 0 and (bb == 1 or seq == tm)
    t_steps = seq // tm
    step_rows = bb * tm
    assert row_off % step_rows == 0
    off = row_off // step_rows
    row_spec = pl.BlockSpec((step_rows, MEM_WIDTH), lambda g, j: (off + g * t_steps + j, 0))
    kv_spec = pl.BlockSpec((bb, m, MEM_WIDTH), lambda g, j: (kv_off + g, 0, 0))
    in_specs = [row_spec, kv_spec, kv_spec, _const((1, MEM_WIDTH)), _const((MEM_WIDTH, MEM_WIDTH)),
                _const((MEM_HEADS, 1, MEM_WIDTH))]
    args = [mq, mk, mv, (jnp.tile(qng, MEM_HEADS) * HEAD_DIM ** -0.5).reshape(1, MEM_WIDTH),
            _block_diag_mean(MEM_WIDTH), _head_masks(MEM_HEADS)]
    return pl.pallas_call(
        functools.partial(_mem_attn_kernel, tm=tm, bb=bb),
        grid=(batch // bb, t_steps),
        in_specs=in_specs,
        out_specs=pl.BlockSpec((step_rows, MEM_WIDTH), lambda g, j: (g * t_steps + j, 0)),
        out_shape=jax.ShapeDtypeStruct((batch * seq, MEM_WIDTH), BF16),
        compiler_params=_params(("parallel", "parallel")),
        name="mem_attn",
    )(*args)


def _outproj_kernel(*refs, heads, prompt_tiles, split_residual):
    n_h = 2 if split_residual else 1
    h_refs, refs = refs[:n_h], refs[n_h:]
    (main_p_ref, main_s_ref, mo_p_ref, mo_s_ref, wmain_ref, wmo_ref), refs = refs[:6], refs[6:]
    route_in, (o_ref, *route_out) = refs[:6], refs[6:]
    is_prompt = pl.program_id(0) < prompt_tiles
    pick = lambda p, s: jnp.where(is_prompt, p, s)
    tm = o_ref.shape[0]
    sub = min(tm, 128)
    blocks = []
    for r in range(0, tm, sub):
        rows = slice(r, r + sub)
        acc = pick(h_refs[0][rows, :], h_refs[1][rows, :]) if split_residual else h_refs[0][rows, :]
        acc = acc + _bdot(pick(mo_p_ref[rows, :], mo_s_ref[rows, :]), wmo_ref[...])
        if heads:
            for h in range(heads):
                acc = acc + _bdot(pick(main_p_ref[h, rows, :], main_s_ref[h, rows, :]), wmain_ref[h])
        else:
            acc = acc + _bdot(pick(main_p_ref[rows, :], main_s_ref[rows, :]), wmain_ref[...])
        o_ref[rows, :] = acc
        blocks.append(acc)
    _route_tile(blocks, *route_in, *route_out)


def _outproj(h, main_p, mo_p, main_s, mo_s, w_main, w_mo, router, tm):
    n_p, n_s = mo_p.shape[0], mo_s.shape[0]
    n, d = n_p + n_s, w_mo.shape[1]
    assert n_p % tm == 0 and n_s % tm == 0
    pt = n_p // tm
    heads = main_p.shape[0] if main_p.ndim == 3 else 0
    row = pl.BlockSpec((tm, d), lambda i: (i, 0))
    split = isinstance(h, tuple)
    h_specs, h_args = (_group_specs(tm, d, pt), list(h)) if split else ([row], [h])
    main_specs = _group_specs(tm, main_p.shape[-1], pt, lead=heads or None)
    r_args, r_in_specs, r_out_specs, r_out_shape, r_scratch = _route_operands(*router, n, tm)
    h_new, *routed = pl.pallas_call(
        functools.partial(_outproj_kernel, heads=heads, prompt_tiles=pt, split_residual=split),
        grid=(n // tm,),
        in_specs=(h_specs + main_specs + _group_specs(tm, MEM_WIDTH, pt)
                  + [_const(w_main.shape), _const(w_mo.shape)] + r_in_specs),
        out_specs=[row] + r_out_specs,
        out_shape=[jax.ShapeDtypeStruct((n, d), F32)] + r_out_shape,
        scratch_shapes=[r_scratch],
        compiler_params=_params(("arbitrary",)),
        name="outproj_route",
    )(*h_args, main_p, main_s, mo_p, mo_s, w_main.astype(BF16), w_mo.astype(BF16), *r_args)
    return h_new, routed


def _route_tile(x_blocks, g_ref, whi_ref, wlo_ref, b_ref, before_tok_ref, before_row_ref,
                mi_ref, mf_ref, cnt_ref, tt_ref, carry_ref):
    @pl.when(pl.program_id(0) == 0)
    def _():
        carry_ref[...] = jnp.zeros_like(carry_ref)

    nt = lambda a, b: lax.dot_general(a, b, (((1,), (1,)), ((), ())), preferred_element_type=F32)
    logits = []
    for x in x_blocks:
        x_hi, x_lo = _split(x * _rms_scale(x) * g_ref[...], 2)
        logits.append((nt(whi_ref[...], x_hi) + nt(wlo_ref[...], x_hi) + nt(whi_ref[...], x_lo))[:ROUTER_ROWS])
    logits = jnp.concatenate(logits, axis=1) + b_ref[...]
    tm = logits.shape[1]
    row = lax.broadcasted_iota(jnp.int32, (ROUTER_ROWS, tm), 0)
    far = jnp.int32(2 * LANES)

    def first_max(vals):
        m = jnp.max(vals, axis=0, keepdims=True)
        return m, jnp.min(jnp.where(vals == m, row, far), axis=0, keepdims=True)

    gl = jnp.where(row < N_GROUPS, logits, -jnp.inf)
    gmax, grp = first_max(gl)
    pg_sel = 1.0 / jnp.sum(jnp.exp(gl - gmax), axis=0, keepdims=True)
    lo = ROUTER_LANE0 + grp * EXPERTS_PER_GROUP
    el = jnp.where((row >= lo) & (row < lo + EXPERTS_PER_GROUP), logits, -jnp.inf)
    m1, i1 = first_max(el)
    m2, i2 = first_max(jnp.where(row == i1, -jnp.inf, el))
    e2 = jnp.exp(m2 - m1)
    g1 = pg_sel / (1.0 + e2)
    g2 = pg_sel * e2 / (1.0 + e2)

    oh1 = row == i1
    oh2 = row == i2
    picked = jnp.where(oh1 | oh2, 1.0, 0.0)
    earlier = jnp.dot(picked.astype(BF16), before_tok_ref[...], preferred_element_type=F32)
    cnt_col = jnp.sum(picked, axis=1, keepdims=True)
    cnt_tile = jnp.concatenate([jnp.broadcast_to(cnt_col, (ROUTER_ROWS, LANES)),
                                jnp.zeros((LANES - ROUTER_ROWS, LANES), F32)], axis=0)
    c_hi = jnp.floor(cnt_tile * (1.0 / 32.0))
    c_lo = cnt_tile - 32.0 * c_hi
    first = (32.0 * jnp.dot(before_row_ref[...], c_hi.astype(BF16), preferred_element_type=F32)
             + jnp.dot(before_row_ref[...], c_lo.astype(BF16), preferred_element_type=F32))
    local = first[:ROUTER_ROWS, 0:1] + earlier
    lpos1 = jnp.sum(jnp.where(oh1, local, 0.0), axis=0, keepdims=True)
    lpos2 = jnp.sum(jnp.where(oh2, local, 0.0), axis=0, keepdims=True)
    carry_before = carry_ref[...]
    carry = carry_before + cnt_tile
    carry_ref[...] = carry

    lane = lax.broadcasted_iota(jnp.int32, (LANES, LANES), 1)
    cols = jnp.where(lane == 0, carry_before, jnp.where(lane == 1, cnt_tile, jnp.where(lane == 2, first,
                     jnp.where(lane == 3, carry, 0.0))))
    tables = cols.T
    tt_ref[...] = tables[:8].astype(jnp.int32)
    cnt_ref[...] = tables[3:4]
    row8 = lax.broadcasted_iota(jnp.int32, (ROUTER_META_ROWS, tm), 0)
    zero8 = jnp.zeros((ROUTER_META_ROWS, tm), F32)
    mi_ref[...] = jnp.where(row8 == 0, lpos1, jnp.where(row8 == 1, lpos2, zero8)).astype(jnp.int32)
    stacked = jnp.where(row8 == 0, g1, jnp.where(row8 == 1, g2, jnp.where(row8 == 2, lpos1,
                        jnp.where(row8 == 3, lpos2, zero8))))
    mf_ref[...] = jnp.concatenate([stacked, jnp.zeros((LANES - ROUTER_META_ROWS, tm), F32)], axis=0).T


def _route_operands(g, w_rg, b_rg, w_re, b_re, n, tm):
    d = g.shape[0]
    assert n % tm == 0 and 2 * tm <= 32 * 32
    n_real = N_GROUPS + N_EXPERTS
    w = jnp.pad(jnp.concatenate([w_rg, w_re], axis=1), ((0, 0), (0, LANES - n_real))).T
    b = jnp.pad(jnp.concatenate([b_rg, b_re]), (0, ROUTER_ROWS - n_real)).reshape(ROUTER_ROWS, 1)
    w_hi = w.astype(BF16)
    w_lo = (w - w_hi.astype(F32)).astype(BF16)
    i = np.arange(tm)
    before_tok = jnp.asarray(i[:, None] < i[None, :], BF16)
    e = np.arange(LANES)
    before_row = jnp.asarray(e[None, :] < e[:, None], BF16)
    args = [g.reshape(1, d), w_hi, w_lo, b, before_tok, before_row]
    in_specs = [_const(a.shape) for a in args]
    out_specs = [pl.BlockSpec((ROUTER_META_ROWS, tm), lambda i: (0, i)), pl.BlockSpec((tm, LANES), lambda i: (i, 0)),
                 _const((1, LANES)), pl.BlockSpec((8, LANES), lambda i: (i, 0))]
    out_shape = [jax.ShapeDtypeStruct((ROUTER_META_ROWS, n), jnp.int32), jax.ShapeDtypeStruct((n, LANES), F32),
                 jax.ShapeDtypeStruct((1, LANES), F32), jax.ShapeDtypeStruct((n // tm * 8, LANES), jnp.int32)]
    return args, in_specs, out_specs, out_shape, pltpu.VMEM((LANES, LANES), F32)


def _pack_rows(ref, x, rows, lead=(), row0=0):
    u32 = jnp.uint32
    for w in range(PACKED_SLABS):
        lo = x[:, (2 * w) * LANES:(2 * w + 1) * LANES].astype(BF16).astype(F32)
        hi = x[:, (2 * w + 1) * LANES:(2 * w + 2) * LANES].astype(BF16).astype(F32)
        word = (lax.bitcast_convert_type(lo, u32) >> 16) | (lax.bitcast_convert_type(hi, u32) & u32(0xFFFF0000))
        ref[lead + (pl.ds(row0 * PACKED_SLABS + w, rows, stride=PACKED_SLABS), slice(None))] = word


def _unpack_rows(ref, rows, lead=(), row0=0):
    u32 = jnp.uint32
    slabs = []
    for w in range(PACKED_SLABS):
        word = ref[lead + (pl.ds(row0 * PACKED_SLABS + w, rows, stride=PACKED_SLABS), slice(None))]
        slabs.append(lax.bitcast_convert_type(word << 16, F32).astype(BF16))
        slabs.append(lax.bitcast_convert_type(word & u32(0xFFFF0000), F32).astype(BF16))
    return jnp.concatenate(slabs, axis=1)


RUN_FIELDS = 3
RUN_CHUNK_BITS = 6


def _copy_runs(runs_ref, tile, local_rows, global_rows, sem, *, to_global):
    ps = PACKED_SLABS
    base = tile * (RUN_FIELDS * N_EXPERTS)

    def piece(g0, l0, off, size):
        g = global_rows(pl.multiple_of((g0 + off) * ps, ps), size * ps)
        l = local_rows(pl.multiple_of((l0 + off) * ps, ps), size * ps)
        src, dst = (l, g) if to_global else (g, l)
        pltpu.make_async_copy(src, dst, sem).start()

    def per_expert(e, carry):
        g0 = runs_ref[base + e]
        length = runs_ref[base + N_EXPERTS + e]
        l0 = runs_ref[base + 2 * N_EXPERTS + e]
        big = 1 << RUN_CHUNK_BITS

        def big_piece(c, inner):
            piece(g0, l0, c * big, big)
            return inner

        n_big = length >> RUN_CHUNK_BITS
        lax.fori_loop(0, n_big, big_piece, 0)
        off = n_big * big
        for bit in reversed(range(RUN_CHUNK_BITS)):
            size = 1 << bit

            @pl.when((length & size) != 0)
            def _(off=off, size=size):
                piece(g0, l0, off, size)

            off = off + (length & size)
        return carry

    lax.fori_loop(0, N_EXPERTS, per_expert, 0)


def _dispatch_kernel(runs_ref, h_ref, g_ref, meta_ref, xs_hbm, buf, sem, *, tm, steps):
    i = pl.program_id(0)
    slot = lax.rem(i, 2)
    ns = 2 * tm

    def wait_slot(sl):
        pltpu.make_async_copy(buf.at[sl], xs_hbm.at[pl.ds(0, ns * PACKED_SLABS)], sem.at[sl]).wait()

    @pl.when(i >= 2)
    def _():
        wait_slot(slot)

    x = h_ref[...]
    xn = (x * _rms_scale(x) * g_ref[...]).astype(BF16)
    j = lax.broadcasted_iota(jnp.int32, (ns, tm), 0)
    pick = jnp.where((j == meta_ref[0:1, :]) | (j == meta_ref[1:2, :]), 1.0, 0.0).astype(BF16)
    _pack_rows(buf, jnp.dot(pick, xn, preferred_element_type=F32), ns, (slot,))
    _copy_runs(runs_ref, i, lambda start, size: buf.at[slot, pl.ds(start, size)],
               lambda start, size: xs_hbm.at[pl.ds(start, size)], sem.at[slot], to_global=True)

    @pl.when(i == steps - 1)
    def _():
        wait_slot(slot)
        if steps > 1:
            wait_slot(1 - slot)


def _dispatch(h, g, meta, runs, tm):
    n, d = h.shape
    steps = n // tm
    return pl.pallas_call(
        functools.partial(_dispatch_kernel, tm=tm, steps=steps),
        grid_spec=pltpu.PrefetchScalarGridSpec(
            num_scalar_prefetch=1,
            grid=(steps,),
            in_specs=[pl.BlockSpec((tm, d), lambda i, runs: (i, 0)),
                      pl.BlockSpec((1, d), lambda i, runs: (0, 0)),
                      pl.BlockSpec((ROUTER_META_ROWS, tm), lambda i, runs: (0, i))],
            out_specs=pl.BlockSpec(memory_space=pl.ANY),
            scratch_shapes=[pltpu.VMEM((2, 2 * tm * PACKED_SLABS, LANES), jnp.uint32),
                            pltpu.SemaphoreType.DMA((2,))],
        ),
        out_shape=jax.ShapeDtypeStruct((2 * n * PACKED_SLABS, LANES), jnp.uint32),
        compiler_params=_params(("arbitrary",)),
        name="moe_dispatch",
    )(runs, h, g.reshape(1, d), meta)


ITEM_FIRST, ITEM_LAST, ITEM_NEW_EXPERT = 1, 2, 4


def _expert_kernel(tile_ref, exp_ref, lo_ref, hi_ref, flag_ref, wslot_ref, next_ref, n_ref,
                   xs_ref, wg_hbm, wu_hbm, wd_hbm, ys_ref,
                   wgf, wuf, wdf, wsem, wgb, wub, wdb, acc, *, tm, sub, layer):
    w = pl.program_id(0)

    @pl.when(w == 0)
    def _():
        acc[...] = jnp.zeros_like(acc)

    def weight_copies(e, slot):
        return [pltpu.make_async_copy(hbm.at[layer, e], buf.at[slot], wsem.at[slot])
                for hbm, buf in ((wg_hbm, wgf), (wu_hbm, wuf), (wd_hbm, wdf))]

    @pl.when(w < n_ref[0])
    def _():
        flags = flag_ref[w]

        @pl.when((flags & ITEM_NEW_EXPERT) != 0)
        def _():
            slot = wslot_ref[w]

            @pl.when(w == 0)
            def _():
                for c in weight_copies(exp_ref[w], slot):
                    c.start()

            for c in weight_copies(exp_ref[w], slot):
                c.wait()
            wgb[...] = wgf[slot].astype(BF16)
            wub[...] = wuf[slot].astype(BF16)
            wdb[...] = wdf[slot].astype(BF16)

            @pl.when(next_ref[w] >= 0)
            def _():
                for c in weight_copies(next_ref[w], 1 - slot):
                    c.start()

        first = (flags & ITEM_FIRST) != 0
        lo, hi = lo_ref[w], hi_ref[w]
        n_blocks = tm // sub

        def up(s):
            x = _unpack_rows(xs_ref, sub, row0=s * sub)
            return (jnp.dot(x, wgb[...], preferred_element_type=F32),
                    jnp.dot(x, wub[...], preferred_element_type=F32))

        ups = {0: up(0)}
        for s in range(n_blocks):
            if s + 1 < n_blocks:
                ups[s + 1] = up(s + 1)
            hg, hu = ups.pop(s)
            act = (hg * jax.nn.sigmoid(hg) * hu).astype(BF16)
            y = jnp.dot(act, wdb[...], preferred_element_type=F32)
            row = lax.broadcasted_iota(jnp.int32, (sub, 1), 0) + s * sub
            y = jnp.where((row >= lo) & (row < hi), y, 0.0)
            rows = slice(s * sub, (s + 1) * sub)
            total = jnp.where(first, y, acc[rows, :] + y)
            acc[rows, :] = total
            _pack_rows(ys_ref, total, sub, row0=s * sub)


def _experts(xs, items, w_g, w_u, w_d, layer):
    d = D_MODEL
    tm = MOE_TILE
    packed_spec = pl.BlockSpec((tm * PACKED_SLABS, LANES), lambda w, tile, *_: (tile[w], 0))
    any_spec = pl.BlockSpec(memory_space=pl.ANY)
    return pl.pallas_call(
        functools.partial(_expert_kernel, tm=tm, sub=MOE_SUB, layer=layer),
        grid_spec=pltpu.PrefetchScalarGridSpec(
            num_scalar_prefetch=len(items),
            grid=(items[0].shape[0],),
            in_specs=[packed_spec, any_spec, any_spec, any_spec],
            out_specs=packed_spec,
            scratch_shapes=[pltpu.VMEM((2, d, D_EXPERT), F32), pltpu.VMEM((2, d, D_EXPERT), F32),
                            pltpu.VMEM((2, D_EXPERT, d), F32), pltpu.SemaphoreType.DMA((2,)),
                            pltpu.VMEM((d, D_EXPERT), BF16), pltpu.VMEM((d, D_EXPERT), BF16),
                            pltpu.VMEM((D_EXPERT, d), BF16), pltpu.VMEM((tm, d), F32)],
        ),
        out_shape=jax.ShapeDtypeStruct(xs.shape, jnp.uint32),
        compiler_params=_params(("arbitrary",)),
        name="moe_experts",
    )(*items, xs, w_g, w_u, w_d)


def _combine_kernel(runs_ref, h_ref, gate_ref, ys_hbm, *refs, tm, steps, prompt_tiles):
    out_refs, (buf, sem) = refs[:-2], refs[-2:]
    i = pl.program_id(0)
    slot = lax.rem(i, 2)
    ns = 2 * tm

    def issue(step, sl):
        _copy_runs(runs_ref, step, lambda start, size: buf.at[sl, pl.ds(start, size)],
                   lambda start, size: ys_hbm.at[pl.ds(start, size)], sem.at[sl], to_global=False)

    @pl.when(i == 0)
    def _():
        issue(0, 0)

    @pl.when(i + 1 < steps)
    def _():
        issue(i + 1, 1 - slot)

    pltpu.make_async_copy(ys_hbm.at[pl.ds(0, ns * PACKED_SLABS)], buf.at[slot], sem.at[slot]).wait()
    y = _unpack_rows(buf, ns, (slot,))
    g = gate_ref[...]
    j = lax.broadcasted_iota(jnp.int32, (tm, ns), 1)
    mix = (jnp.where(j == g[:, 2:3].astype(jnp.int32), g[:, 0:1], 0.0)
           + jnp.where(j == g[:, 3:4].astype(jnp.int32), g[:, 1:2], 0.0)).astype(BF16)
    out = h_ref[...] + jnp.dot(mix, y, preferred_element_type=F32)
    if len(out_refs) == 1:
        out_refs[0][...] = out
    else:
        @pl.when(i < prompt_tiles)
        def _():
            out_refs[0][...] = out

        @pl.when(i >= prompt_tiles)
        def _():
            out_refs[1][...] = out


def _combine(h, gates, ys, runs, tm, split_rows=None):
    n, d = h.shape
    steps = n // tm
    row = pl.BlockSpec((tm, d), lambda i, pos: (i, 0))
    if split_rows is None:
        pt, out_specs, out_shape = 0, row, jax.ShapeDtypeStruct((n, d), F32)
    else:
        assert split_rows % tm == 0
        pt = split_rows // tm
        out_specs = _group_specs(tm, d, pt)
        out_shape = [jax.ShapeDtypeStruct((split_rows, d), F32), jax.ShapeDtypeStruct((n - split_rows, d), F32)]
    return pl.pallas_call(
        functools.partial(_combine_kernel, tm=tm, steps=steps, prompt_tiles=pt),
        grid_spec=pltpu.PrefetchScalarGridSpec(
            num_scalar_prefetch=1,
            grid=(steps,),
            in_specs=[row, pl.BlockSpec((tm, LANES), lambda i, pos: (i, 0)),
                      pl.BlockSpec(memory_space=pl.ANY)],
            out_specs=out_specs,
            scratch_shapes=[pltpu.VMEM((2, 2 * tm * PACKED_SLABS, LANES), jnp.uint32),
                            pltpu.SemaphoreType.DMA((2,))],
        ),
        out_shape=out_shape,
        compiler_params=_params(("arbitrary",)),
        name="moe_combine",
    )(runs, h, gates, ys)


def _lookup(tables, idx):
    hit = idx[:, None] == jnp.arange(tables.shape[1], dtype=idx.dtype)[None, :]
    return jnp.sum(jnp.where(hit[None], tables[:, None, :], 0), axis=2)


def _work_items(counts, n_slots):
    tm = MOE_TILE
    assert n_slots % tm == 0, (n_slots, tm)
    ends = jnp.cumsum(counts)
    starts = ends - counts
    first_tile = starts // tm
    n_tiles_e = jnp.where(counts > 0, (ends - 1) // tm - first_tile + 1, 0)
    item_end = jnp.cumsum(n_tiles_e)
    item_start = item_end - n_tiles_e
    n_items = item_end[-1]
    max_items = n_slots // tm + N_EXPERTS - 1
    w = jnp.minimum(jnp.arange(max_items, dtype=jnp.int32), n_items - 1)
    expert = jnp.sum(w[:, None] >= item_end[None, :], axis=1).astype(jnp.int32)
    first_tile_w, item_start_w, start_w, end_w = _lookup(jnp.stack([first_tile, item_start, starts, ends]), expert)
    tile = first_tile_w + w - item_start_w
    lo = jnp.maximum(start_w - tile * tm, 0)
    hi = jnp.minimum(end_w - tile * tm, tm)
    prev_tile = jnp.concatenate([jnp.full((1,), -1, jnp.int32), tile[:-1]])
    next_tile = jnp.concatenate([tile[1:], jnp.full((1,), -1, jnp.int32)])
    prev_expert = jnp.concatenate([jnp.full((1,), -1, jnp.int32), expert[:-1]])
    idx = jnp.arange(max_items, dtype=jnp.int32)
    new_expert = expert != prev_expert
    flags = (jnp.where(tile != prev_tile, ITEM_FIRST, 0)
             | jnp.where((tile != next_tile) | (idx == n_items - 1), ITEM_LAST, 0)
             | jnp.where(new_expert, ITEM_NEW_EXPERT, 0))
    weight_slot = (jnp.cumsum(new_expert.astype(jnp.int32)) - 1) % 2
    ids = jnp.arange(N_EXPERTS, dtype=jnp.int32)
    later = (ids[None, :] > ids[:, None]) & (n_tiles_e[None, :] > 0)
    following = jnp.min(jnp.where(later, ids[None, :], N_EXPERTS), axis=1)
    following = jnp.where(following == N_EXPERTS, -1, following)
    next_expert, = _lookup(following[None, :], expert)
    as_i32 = lambda a: a.astype(jnp.int32)
    return (as_i32(tile), expert, as_i32(lo), as_i32(hi), as_i32(flags), as_i32(weight_slot),
            as_i32(next_expert), as_i32(n_items).reshape(1))


def _moe(h, routed, g, w_g, w_u, w_d, layer, tm, split_rows=None):
    n, _ = h.shape
    meta, gates, cnt, tables = routed
    experts = slice(ROUTER_LANE0, ROUTER_LANE0 + N_EXPERTS)
    counts = cnt[0, experts].astype(jnp.int32)
    starts = jnp.cumsum(counts) - counts
    tables = tables.reshape(n // tm, 8, LANES)[:, :RUN_FIELDS, experts]
    runs = tables.at[:, 0, :].add(starts[None, :]).reshape(-1)
    xs = _dispatch(h, g, meta, runs, tm)
    ys = _experts(xs, _work_items(counts, 2 * n), w_g, w_u, w_d, layer)
    return _combine(h, gates, ys, runs, tm, split_rows)


def _inproj_b_kernel(x_ref, gkv_ref, gmix_ref, wkv_ref, win_ref, kng_ref, qng_ref, bdk_ref, hsum_ref, hexp_ref,
                     q_ref, mq_ref, k_ref, v_ref, kt_ref):
    x = x_ref[...]
    xr = x * _rms_scale(x)
    kv = _bdot(xr * gkv_ref[...], wkv_ref[...])
    k = kv[:, :KV_WIDTH]
    k = k * lax.rsqrt(_seg_mean(k * k, bdk_ref[...]) + EPS) * kng_ref[...]
    k_ref[...] = k
    kt_ref[...] = k.T.astype(BF16)
    v_ref[...] = kv[:, KV_WIDTH:]
    proj = _bdot(xr * gmix_ref[...], win_ref[...])
    q = proj[:, :MAIN_WIDTH]
    ms = None
    for p in _split(q * q, 2):
        t = jnp.dot(p, hsum_ref[...], preferred_element_type=F32)
        ms = t if ms is None else ms + t
    scale = None
    for p in _split(lax.rsqrt(ms + EPS), 2):
        t = jnp.dot(p, hexp_ref[...], preferred_element_type=F32)
        scale = t if scale is None else scale + t
    q_ref[...] = (q * scale * qng_ref[...]).astype(BF16)
    mq_ref[...] = proj[:, MAIN_WIDTH:]


def _swa_perm():
    g, kh, dd = np.meshgrid(np.arange(SWA_GROUP), np.arange(SWA_KV_HEADS), np.arange(HEAD_DIM), indexing="ij")
    return ((kh * SWA_GROUP + g) * HEAD_DIM + dd).reshape(-1)


def _inproj_b(x, g_kv, g_mix, w_kv, w_in, kng, qng):
    n, d = x.shape
    tm = _row_tile(n)
    perm = _swa_perm()
    w_in_p = jnp.concatenate([w_in[:, :MAIN_WIDTH][:, perm], w_in[:, MAIN_WIDTH:]], axis=1).astype(BF16)
    qng_t = (jnp.tile(qng, SWA_HEADS) * HEAD_DIM ** -0.5).reshape(1, MAIN_WIDTH)
    member = (np.arange(MAIN_WIDTH)[:, None] // HEAD_DIM == np.arange(LANES)[None, :]).astype(np.float32)
    row = lambda w: pl.BlockSpec((tm, w), lambda i: (i, 0))
    return pl.pallas_call(
        _inproj_b_kernel,
        grid=(n // tm,),
        in_specs=[row(d), _const((1, d)), _const((1, d)), _const((d, 2 * KV_WIDTH)), _const((d, d)),
                  _const((1, KV_WIDTH)), _const((1, MAIN_WIDTH)), _const((KV_WIDTH, KV_WIDTH)),
                  _const((MAIN_WIDTH, LANES)), _const((LANES, MAIN_WIDTH))],
        out_specs=[row(MAIN_WIDTH), row(MEM_WIDTH), row(KV_WIDTH), row(KV_WIDTH),
                   pl.BlockSpec((KV_WIDTH, tm), lambda i: (0, i))],
        out_shape=[jax.ShapeDtypeStruct((n, MAIN_WIDTH), BF16), jax.ShapeDtypeStruct((n, MEM_WIDTH), F32),
                   jax.ShapeDtypeStruct((n, KV_WIDTH), F32), jax.ShapeDtypeStruct((n, KV_WIDTH), F32),
                   jax.ShapeDtypeStruct((KV_WIDTH, n), BF16)],
        compiler_params=_params(("parallel",)),
        name="inproj_b",
    )(x, g_kv.reshape(1, d), g_mix.reshape(1, d), w_kv.astype(BF16), w_in_p,
      jnp.tile(kng, SWA_KV_HEADS).reshape(1, KV_WIDTH), qng_t,
      _block_diag_mean(KV_WIDTH), jnp.asarray(member / HEAD_DIM, BF16), jnp.asarray(member.T, BF16))


def _softmax_with_sink(s, sink):
    m = jnp.maximum(jnp.max(s, axis=-1, keepdims=True), sink)
    e = jnp.exp(s - m)
    r = 1.0 / (jnp.sum(e, axis=-1, keepdims=True) + jnp.exp(sink - m))
    return (e * r).astype(BF16)


def _swa_bias(tq):
    slopes = 2.0 ** (-8.0 * np.arange(1, SWA_HEADS + 1, dtype=np.float64) / SWA_HEADS)
    dist = np.arange(tq)[:, None] + WINDOW - np.arange(WINDOW + tq)[None, :]
    valid = (dist >= 0) & (dist <= WINDOW)
    return np.stack([np.where(valid, -s * dist, NEG_BIG) for s in slopes]).astype(np.float32)


def _swa_prompt_kernel(sink_ref, q_ref, ktp_ref, kto_ref, vp_ref, vo_ref, bias_ref, hm_ref, o_ref, *, nb):
    w = WINDOW
    key = lax.broadcasted_iota(jnp.int32, (w, 2 * w), 1)
    has_prev = (pl.program_id(0) > 0) | (key >= w)
    heads = [(g, kh) for g in range(SWA_GROUP) for kh in range(SWA_KV_HEADS)]
    kts, vvs = [], []
    for b in range(nb):
        kt_prev = ktp_ref[...] if b == 0 else kto_ref[:, (b - 1) * w:b * w]
        v_prev = vp_ref[...] if b == 0 else vo_ref[(b - 1) * w:b * w, :]
        kts.append(jnp.concatenate([kt_prev, kto_ref[:, b * w:(b + 1) * w]], axis=1).astype(BF16))
        vvs.append(jnp.concatenate([v_prev, vo_ref[b * w:(b + 1) * w, :]], axis=0).astype(BF16))
    scores = [[jnp.dot(q_ref[b * w:(b + 1) * w, g * KV_WIDTH:(g + 1) * KV_WIDTH] * hm_ref[kh].astype(BF16),
                       kts[b], preferred_element_type=F32) for g, kh in heads] for b in range(nb)]
    for b in range(nb):
        probs = []
        for (g, kh), s in zip(heads, scores[b]):
            h = kh * SWA_GROUP + g
            s = s + bias_ref[h]
            if b == 0:
                s = jnp.where(has_prev, s, NEG_BIG)
            probs.append(_softmax_with_sink(s, sink_ref[h]))
        outs = [jnp.dot(p, vvs[b], preferred_element_type=F32) for p in probs]
        for g in range(SWA_GROUP):
            acc = None
            for (cg, kh), o in zip(heads, outs):
                if cg == g:
                    t = o * hm_ref[kh]
                    acc = t if acc is None else acc + t
            o_ref[b * w:(b + 1) * w, g * KV_WIDTH:(g + 1) * KV_WIDTH] = acc.astype(BF16)


def _swa_prompt(q, kt, v, sinks, *, n_rows, nb):
    w = WINDOW
    step = nb * w
    assert n_rows % step == 0
    prev = lambda j, sink: jnp.maximum(j * nb - 1, 0)
    return pl.pallas_call(
        functools.partial(_swa_prompt_kernel, nb=nb),
        grid_spec=pltpu.PrefetchScalarGridSpec(
            num_scalar_prefetch=1,
            grid=(n_rows // step,),
            in_specs=[pl.BlockSpec((step, MAIN_WIDTH), lambda j, sink: (j, 0)),
                      pl.BlockSpec((KV_WIDTH, w), lambda j, sink: (0, prev(j, sink))),
                      pl.BlockSpec((KV_WIDTH, step), lambda j, sink: (0, j)),
                      pl.BlockSpec((w, KV_WIDTH), lambda j, sink: (prev(j, sink), 0)),
                      pl.BlockSpec((step, KV_WIDTH), lambda j, sink: (j, 0)),
                      pl.BlockSpec((SWA_HEADS, w, 2 * w), lambda j, sink: (0, 0, 0)),
                      pl.BlockSpec((SWA_KV_HEADS, 1, KV_WIDTH), lambda j, sink: (0, 0, 0))],
            out_specs=pl.BlockSpec((step, MAIN_WIDTH), lambda j, sink: (j, 0)),
        ),
        out_shape=jax.ShapeDtypeStruct((n_rows, MAIN_WIDTH), BF16),
        compiler_params=_params(("arbitrary",)),
        name="swa_prompt",
    )(sinks.astype(F32), q, kt, kt, v, v, jnp.asarray(_swa_bias(w)), _head_masks(SWA_KV_HEADS))


def _swa_sample_kernel(q_ref, kp_ref, ko_ref, vp_ref, vo_ref, bias_ref, sink_ref, hm_ref, o_ref, *, nb, tq):
    w = WINDOW
    heads = [(kh, g) for kh in range(SWA_KV_HEADS) for g in range(SWA_GROUP)]
    kks, vvs, scores = [], [], []
    q = q_ref[...].astype(F32)
    for i in range(nb):
        win = slice(i * KV_WIDTH, (i + 1) * KV_WIDTH)
        kks.append(jnp.concatenate([kp_ref[win, :].T, ko_ref[i * tq:(i + 1) * tq, :]], axis=0))
        vvs.append(jnp.concatenate([vp_ref[win, :].T, vo_ref[i * tq:(i + 1) * tq, :]], axis=0))
        qs = jnp.concatenate([q[i * tq:(i + 1) * tq, g * KV_WIDTH:(g + 1) * KV_WIDTH] * hm_ref[kh]
                              for kh, g in heads], axis=0)
        scores.append(_bdot_nt(qs, kks[i]))
    probs = [_softmax_with_sink(s + bias_ref[...], sink_ref[...]) for s in scores]
    outs = [_bdot(p, vv) for p, vv in zip(probs, vvs)]
    for g in range(SWA_GROUP):
        rows = []
        for i in range(nb):
            acc = None
            for r, (kh, hg) in enumerate(heads):
                if hg == g:
                    t = outs[i][r * tq:(r + 1) * tq] * hm_ref[kh]
                    acc = t if acc is None else acc + t
            rows.append(acc)
        o_ref[:, g * KV_WIDTH:(g + 1) * KV_WIDTH] = jnp.concatenate(rows, axis=0).astype(BF16)


def _swa_sample(q, k_win, v_win, k, v, sinks, *, row_off, batch, tq, nb):
    w = WINDOW
    assert batch % nb == 0 and row_off % (nb * tq) == 0
    off = row_off // (nb * tq)
    bias = jnp.asarray(_swa_bias(tq).reshape(SWA_HEADS * tq, w + tq))
    sink_col = jnp.repeat(sinks.astype(F32), tq).reshape(SWA_HEADS * tq, 1)
    own = lambda width: pl.BlockSpec((nb * tq, width), lambda b: (off + b, 0))
    win = pl.BlockSpec((nb * KV_WIDTH, w), lambda b: (b, 0))
    return pl.pallas_call(
        functools.partial(_swa_sample_kernel, nb=nb, tq=tq),
        grid=(batch // nb,),
        in_specs=[own(MAIN_WIDTH), win, own(KV_WIDTH), win, own(KV_WIDTH), _const(bias.shape),
                  _const(sink_col.shape), _const((SWA_KV_HEADS, 1, KV_WIDTH))],
        out_specs=pl.BlockSpec((nb * tq, MAIN_WIDTH), lambda b: (b, 0)),
        out_shape=jax.ShapeDtypeStruct((batch * tq, MAIN_WIDTH), BF16),
        compiler_params=_params(("arbitrary",)),
        name="swa_sample",
    )(q, k_win, k, v_win, v, bias, sink_col, _head_masks(SWA_KV_HEADS))


def kernel(x_prompt, x_sample, state_gla, cache_win_k, cache_win_v, cache_mem_k, cache_mem_v, mem_prompt, norm_mix_g, norm_ffn_g, norm_mem_g, w_mem_kv, mem_qn_g, mem_kn_g, w_out, w_in_a, w_gate_lr, b_gate_lr, gla_norm_g, w_in_b, swa_qn_g, swa_sinks, norm_kv_g, w_kv, swa_kn_g, w_router_group, b_router_group, w_router_expert, b_router_expert, w_exp_gate, w_exp_up, w_exp_down):
    bp, tp, d = x_prompt.shape
    bs, ts, _ = x_sample.shape
    assert bp == 1 and tp % WINDOW == 0 and ts * (GLA_CHUNK // ts) == GLA_CHUNK
    n_p, n_s = bp * tp, bs * ts
    w_buf = cache_win_k.shape[1]
    assert w_buf == WINDOW
    x_p, x_s = x_prompt.reshape(n_p, d), x_sample.reshape(n_s, d)

    mem_k_p, mem_v_p = _mem_kv(mem_prompt, norm_mem_g, w_mem_kv, mem_kn_g)
    feature_major = lambda c: jnp.moveaxis(c, -3, -1).reshape(*c.shape[:-3], c.shape[-2] * c.shape[-1], c.shape[-3])
    cmk, cmv = feature_major(cache_mem_k), feature_major(cache_mem_v)

    def mem_attend(mq, l):
        tm_p = _row_tile(tp, 512)
        mo_p = _mem_attn(mq, mem_k_p, mem_v_p, mem_qn_g[l], row_off=0, seq=tp, tm=tm_p, bb=1, layer=l)
        mo_s = _mem_attn(mq, cmk, cmv, mem_qn_g[l], row_off=n_p, seq=ts, tm=ts, bb=16, layer=l)
        return mo_p, mo_s

    tm = _row_tile(math.gcd(n_p, n_s), MOE_TILE)
    router = lambda l: (norm_ffn_g[l], w_router_group[l], b_router_group[l], w_router_expert[l],
                        b_router_expert[l])

    def moe(h, routed, l, split_rows=None):
        return _moe(h, routed, norm_ffn_g[l], w_exp_gate, w_exp_up, w_exp_down, l, tm, split_rows)

    q, k, la, v, og, mq = _inproj_a(x_p, x_s, norm_mix_g[0], w_in_a[0], w_gate_lr[0], b_gate_lr[0])
    zero_state = jnp.zeros((bp, GLA_HEADS, GLA_DK, GLA_DV), F32)
    n_sub = max(1, min(8, tp // GLA_CHUNK))
    main_p, gla_p = _gla(q, k, la, v, og, zero_state, gla_norm_g[0], row_off=0, seq=tp, n_seg=1, n_sub=n_sub)
    main_s, gla_s = _gla(q, k, la, v, og, state_gla[0], gla_norm_g[0], row_off=n_p, seq=ts,
                         n_seg=GLA_CHUNK // ts, n_sub=1)
    mo_p, mo_s = mem_attend(mq, 0)
    w_o = w_out[0]
    h, routed = _outproj((x_p, x_s), main_p, mo_p, main_s, mo_s, w_o[:MAIN_WIDTH].reshape(GLA_HEADS, GLA_DV, d),
                         w_o[MAIN_WIDTH:], router(0), tm)
    h = moe(h, routed, 0)

    q, mq, k_sh, v_sh, kt_sh = _inproj_b(h, norm_kv_g, norm_mix_g[1], w_kv, w_in_b[0], swa_kn_g, swa_qn_g[0])
    ck = feature_major(cache_win_k).reshape(bs * KV_WIDTH, w_buf)
    cv = feature_major(cache_win_v).reshape(bs * KV_WIDTH, w_buf)
    main_p = _swa_prompt(q, kt_sh, v_sh, swa_sinks[0], n_rows=n_p, nb=4)
    main_s = _swa_sample(q, ck, cv, k_sh, v_sh, swa_sinks[0], row_off=n_p, batch=bs, tq=ts, nb=16)
    mo_p, mo_s = mem_attend(mq, 1)
    w_o = w_out[1]
    h, routed = _outproj(h, main_p, mo_p, main_s, mo_s, w_o[:MAIN_WIDTH][_swa_perm()], w_o[MAIN_WIDTH:],
                         router(1), tm)
    y_p, y_s = moe(h, routed, 1, split_rows=n_p)

    y_prompt = y_p.reshape(bp, tp, d)
    y_sample = y_s.reshape(bs, ts, d)
    k_new = k_sh[n_p:].reshape(bs, ts, SWA_KV_HEADS, HEAD_DIM)
    v_new = v_sh[n_p:].reshape(bs, ts, SWA_KV_HEADS, HEAD_DIM)
    win_k_s = jnp.concatenate([cache_win_k, k_new], axis=1)[:, -w_buf:]
    win_v_s = jnp.concatenate([cache_win_v, v_new], axis=1)[:, -w_buf:]
    win_k_p = k_sh[n_p - WINDOW:n_p].reshape(bp, WINDOW, SWA_KV_HEADS, HEAD_DIM)
    win_v_p = v_sh[n_p - WINDOW:n_p].reshape(bp, WINDOW, SWA_KV_HEADS, HEAD_DIM)
    token_major = lambda c: jnp.moveaxis(c.reshape(*c.shape[:-2], MEM_HEADS, HEAD_DIM, c.shape[-1]), -1, -3)
    return (y_prompt, y_sample, gla_p[None], gla_s[None], win_k_p, win_v_p, win_k_s, win_v_s,
            token_major(mem_k_p), token_major(mem_v_p))
```

```python
import functools
import math

import numpy as np
import jax
import jax.numpy as jnp
from jax import lax
from jax.experimental import pallas as pl
from jax.experimental.pallas import tpu as pltpu

F32 = jnp.float32
BF16 = jnp.bfloat16

D_MODEL = 1024
MEM_LEN = 256
MEM_HEADS = 4
HEAD_DIM = 64
MEM_WIDTH = MEM_HEADS * HEAD_DIM
MAIN_WIDTH = D_MODEL - MEM_WIDTH
GLA_HEADS = 4
GLA_DV = MAIN_WIDTH // GLA_HEADS
GLA_DK = GLA_DV // 2
GLA_DK_PAD = 128
GLA_KEY_WIDTH = GLA_HEADS * GLA_DK
GLA_KEY_PAD = GLA_HEADS * GLA_DK_PAD
GLA_GATE_RANK = 16
GLA_TAU = 16.0
GLA_CHUNK = 64
SWA_HEADS = MAIN_WIDTH // HEAD_DIM
SWA_KV_HEADS = 4
SWA_GROUP = SWA_HEADS // SWA_KV_HEADS
KV_WIDTH = SWA_KV_HEADS * HEAD_DIM
WINDOW = 128
N_GROUPS = 4
EXPERTS_PER_GROUP = 8
N_EXPERTS = N_GROUPS * EXPERTS_PER_GROUP
D_EXPERT = 512
EPS = 1e-6
LANES = 128
NEG_BIG = -1e30
VMEM_LIMIT = 56 * 1024 * 1024
MOE_TILE = 512
MOE_SUB = 128
ROUTER_LANE0 = N_GROUPS
ROUTER_META_ROWS = 8
ROUTER_ROWS = 40
SLABS = D_MODEL // LANES
PACKED_SLABS = SLABS // 2


def _bdot(a, b):
    return jnp.dot(a.astype(BF16), b.astype(BF16), preferred_element_type=F32)


def _bdot_nt(a, b):
    return lax.dot_general(a.astype(BF16), b.astype(BF16), (((1,), (1,)), ((), ())),
                           preferred_element_type=F32)


def _bdot_tn(a, b):
    return lax.dot_general(a.astype(BF16), b.astype(BF16), (((0,), (0,)), ((), ())),
                           preferred_element_type=F32)


def _split(x, n):
    parts = []
    for _ in range(n - 1):
        p = x.astype(BF16)
        parts.append(p)
        x = x - p.astype(F32)
    parts.append(x.astype(BF16))
    return parts


def _exact_left_dot(m, x, n=2):
    out = None
    for p in _split(x, n):
        t = jnp.dot(m, p, preferred_element_type=F32)
        out = t if out is None else out + t
    return out


def _seg_mean(x2, bd):
    out = None
    for p in _split(x2, 2):
        t = jnp.dot(p, bd, preferred_element_type=F32)
        out = t if out is None else out + t
    return out


def _rms_scale(x):
    return lax.rsqrt(jnp.mean(x * x, axis=-1, keepdims=True) + EPS)


def _row_tile(n, cap=512):
    t = cap
    while t > 8 and n % t:
        t //= 2
    assert n % t == 0, n
    return t


def _params(sem):
    return pltpu.CompilerParams(dimension_semantics=sem, vmem_limit_bytes=VMEM_LIMIT)


def _const(shape):
    nd = len(shape)
    return pl.BlockSpec(shape, lambda *_: (0,) * nd)


def _group_specs(tm, width, prompt_tiles, lead=None):
    p_idx = lambda i, *_: jnp.minimum(i, prompt_tiles - 1)
    s_idx = lambda i, *_: jnp.maximum(i - prompt_tiles, 0)
    if lead is None:
        return [pl.BlockSpec((tm, width), lambda i, *_, f=f: (f(i), 0)) for f in (p_idx, s_idx)]
    return [pl.BlockSpec((lead, tm, width), lambda i, *_, f=f: (0, f(i), 0)) for f in (p_idx, s_idx)]


def _block_diag_mean(width):
    i = np.arange(width)
    return jnp.asarray((i[:, None] // HEAD_DIM == i[None, :] // HEAD_DIM) / HEAD_DIM, BF16)


def _head_masks(n_heads):
    i = np.arange(n_heads * HEAD_DIM)
    return jnp.asarray((i[None, :] // HEAD_DIM == np.arange(n_heads)[:, None]), F32)[:, None, :]


def _mem_kv_kernel(mem_ref, g_ref, w_ref, kng_ref, bd_ref, k_ref, v_ref):
    x = mem_ref[0]
    hn = x * _rms_scale(x) * g_ref[0]
    kv = _bdot(hn, w_ref[0])
    k = kv[:, :MEM_WIDTH]
    k = k * lax.rsqrt(_seg_mean(k * k, bd_ref[...]) + EPS) * kng_ref[0]
    k_ref[0, 0] = k.T
    v_ref[0, 0] = kv[:, MEM_WIDTH:].T


def _mem_kv(mem, g, w, kng):
    depth, (b, m, d) = w.shape[0], mem.shape
    out = jax.ShapeDtypeStruct((depth, b, m, MEM_WIDTH), F32)
    blk = pl.BlockSpec((1, 1, m, MEM_WIDTH), lambda l, i: (l, i, 0, 0))
    return pl.pallas_call(
        _mem_kv_kernel,
        grid=(depth, b),
        in_specs=[pl.BlockSpec((1, m, d), lambda l, i: (i, 0, 0)),
                  pl.BlockSpec((1, 1, d), lambda l, i: (l, 0, 0)),
                  pl.BlockSpec((1, d, 2 * MEM_WIDTH), lambda l, i: (l, 0, 0)),
                  pl.BlockSpec((1, 1, MEM_WIDTH), lambda l, i: (l, 0, 0)),
                  _const((MEM_WIDTH, MEM_WIDTH))],
        out_specs=[blk, blk],
        out_shape=[out, out],
        compiler_params=_params(("arbitrary", "arbitrary")),
        name="mem_kv",
    )(mem, g.reshape(depth, 1, d), w.astype(BF16),
      jnp.tile(kng, (1, MEM_HEADS)).reshape(depth, 1, MEM_WIDTH), _block_diag_mean(MEM_WIDTH))


def _inproj_a_kernel(xp_ref, xs_ref, g_ref, wq_ref, wk_ref, wv_ref, wog_ref, wlr_ref, wmq_ref, wgl_ref, bgl_ref,
                     q_ref, k_ref, la_ref, v_ref, og_ref, mq_ref, *, prompt_tiles):
    x = jnp.where(pl.program_id(0) < prompt_tiles, xp_ref[...], xs_ref[...])
    hn = (x * _rms_scale(x) * g_ref[...]).astype(BF16)
    q_ref[...] = jnp.dot(hn, wq_ref[...], preferred_element_type=F32) * (GLA_DK ** -0.5)
    k_ref[...] = jnp.dot(hn, wk_ref[...], preferred_element_type=F32)
    for h in range(GLA_HEADS):
        v_ref[h] = jnp.dot(hn, wv_ref[h], preferred_element_type=F32).astype(BF16)
        og_ref[h] = jnp.dot(hn, wog_ref[h], preferred_element_type=F32)
    lr = jnp.dot(hn, wlr_ref[...], preferred_element_type=F32)
    z = _bdot(lr, wgl_ref[...]) + bgl_ref[...]
    la_ref[...] = (jnp.minimum(z, 0.0) - jnp.log(1.0 + jnp.exp(-jnp.abs(z)))) * (1.0 / GLA_TAU)
    mq_ref[...] = jnp.dot(hn, wmq_ref[...], preferred_element_type=F32)


def _pad_heads(w, width, pad):
    lead = w.shape[:-1]
    w = w.reshape(*lead, GLA_HEADS, width)
    w = jnp.pad(w, [(0, 0)] * len(lead) + [(0, 0), (0, pad - width)])
    return w.reshape(*lead, GLA_HEADS * pad)


def _inproj_a(x_p, x_s, g, w_in, w_lr, b_lr):
    (n_p, d), n_s = x_p.shape, x_s.shape[0]
    n = n_p + n_s
    tm = _row_tile(n_s)
    assert n_p % tm == 0
    pt = n_p // tm
    c0, c1, c2, c3, c4 = (GLA_KEY_WIDTH, 2 * GLA_KEY_WIDTH, 2 * GLA_KEY_WIDTH + MAIN_WIDTH,
                          2 * GLA_KEY_WIDTH + 2 * MAIN_WIDTH,
                          2 * GLA_KEY_WIDTH + 2 * MAIN_WIDTH + GLA_GATE_RANK)
    wb = w_in.astype(BF16)
    wq = _pad_heads(wb[:, :c0], GLA_DK, GLA_DK_PAD)
    wk = _pad_heads(wb[:, c0:c1], GLA_DK, GLA_DK_PAD)
    wv = wb[:, c1:c2].reshape(d, GLA_HEADS, GLA_DV).transpose(1, 0, 2)
    wog = wb[:, c2:c3].reshape(d, GLA_HEADS, GLA_DV).transpose(1, 0, 2)
    wlr = jnp.pad(wb[:, c3:c4], ((0, 0), (0, LANES - GLA_GATE_RANK)))
    wmq = wb[:, c4:]
    wgl = jnp.pad(_pad_heads(w_lr.astype(BF16), GLA_DK, GLA_DK_PAD), ((0, LANES - GLA_GATE_RANK), (0, 0)))
    bgl = _pad_heads(b_lr.reshape(1, -1), GLA_DK, GLA_DK_PAD)
    row = lambda w: pl.BlockSpec((tm, w), lambda i: (i, 0))
    hrow = pl.BlockSpec((GLA_HEADS, tm, GLA_DV), lambda i: (0, i, 0))
    key = jax.ShapeDtypeStruct((n, GLA_KEY_PAD), F32)
    val = jax.ShapeDtypeStruct((GLA_HEADS, n, GLA_DV), F32)
    return pl.pallas_call(
        functools.partial(_inproj_a_kernel, prompt_tiles=pt),
        grid=(n // tm,),
        in_specs=_group_specs(tm, d, pt) + [
            _const((1, d)), _const(wq.shape), _const(wk.shape), _const(wv.shape),
            _const(wog.shape), _const(wlr.shape), _const(wmq.shape), _const(wgl.shape),
            _const(bgl.shape)],
        out_specs=[row(GLA_KEY_PAD), row(GLA_KEY_PAD), row(GLA_KEY_PAD), hrow, hrow, row(MEM_WIDTH)],
        out_shape=[key, key, key, jax.ShapeDtypeStruct(val.shape, BF16), val,
                   jax.ShapeDtypeStruct((n, MEM_WIDTH), F32)],
        compiler_params=_params(("parallel",)),
        name="inproj_a",
    )(x_p, x_s, g.reshape(1, d), wq, wk, wv, wog, wlr, wmq, wgl, bgl)


def _gla_kernel(q_ref, k_ref, la_ref, v_ref, og_ref, s0_ref, gn_ref, mcum_ref, mall_ref, sel_ref,
                o_ref, sout_ref, s_ref, *, chunk, n_sub, n_seg):
    j = pl.program_id(1)
    seg = chunk // n_seg

    @pl.when(j == 0)
    def _():
        s_ref[...] = jnp.zeros_like(s_ref)
        s_ref[:, :, :GLA_DK, :] = s0_ref[...]

    mcum = mcum_ref[...]
    causal = mcum.astype(F32) > 0.0
    row = lax.broadcasted_iota(jnp.int32, (chunk, GLA_DK_PAD), 0)
    gn = gn_ref[...]
    hcols = [slice(h * GLA_DK_PAD, (h + 1) * GLA_DK_PAD) for h in range(GLA_HEADS)]
    crows = [slice(c * chunk, (c + 1) * chunk) for c in range(n_sub)]
    qts, kts, kds, e_ends = [], [], [], []
    for rows in crows:
        la = la_ref[rows, :]
        b = _exact_left_dot(mcum, la)
        if n_seg == 1:
            b_end = b[chunk - 1:chunk, :]
            e_ends.append(jnp.broadcast_to(jnp.exp(b_end), (LANES, b.shape[1])).T)
        else:
            b_end = _exact_left_dot(mall_ref[...], la)
            e_ends.append(jnp.exp(_exact_left_dot(sel_ref[...], la)).T)
        k = k_ref[rows, :]
        qts.append(q_ref[rows, :] * jnp.exp(b))
        kts.append((k * jnp.exp(-b)).astype(BF16))
        kds.append(k * jnp.exp(b_end - b))
    vbs = [[v_ref[h, rows, :].astype(BF16) for h in range(GLA_HEADS)] for rows in crows]
    scores = [[_bdot_nt(qts[c][:, cols], kts[c][:, cols]) for cols in hcols] for c in range(n_sub)]
    kvs = []
    for c in range(n_sub):
        per_head = []
        for h, cols in enumerate(hcols):
            per_seg = []
            for s in range(n_seg):
                kd = kds[c][:, cols]
                if n_seg > 1:
                    kd = jnp.where((row >= s * seg) & (row < (s + 1) * seg), kd, 0.0)
                per_seg.append(_bdot_tn(kd, vbs[c][h]))
            per_head.append(per_seg)
        kvs.append(per_head)
    state = [[s_ref[s, h] for s in range(n_seg)] for h in range(GLA_HEADS)]
    inters = []
    for c in range(n_sub):
        per_head = []
        for h, cols in enumerate(hcols):
            parts = []
            for s in range(n_seg):
                parts.append(_bdot(qts[c][s * seg:(s + 1) * seg, cols], state[h][s]))
                state[h][s] = e_ends[c][cols, s:s + 1] * state[h][s] + kvs[c][h][s]
            per_head.append(parts[0] if n_seg == 1 else jnp.concatenate(parts, axis=0))
        inters.append(per_head)
    for h in range(GLA_HEADS):
        for s in range(n_seg):
            s_ref[s, h] = state[h][s]
    for c, rows in enumerate(crows):
        for h in range(GLA_HEADS):
            a = jnp.where(causal, scores[c][h], 0.0)
            o = _bdot(a, vbs[c][h]) + inters[c][h]
            on = o * lax.rsqrt(jnp.mean(o * o, axis=-1, keepdims=True) + EPS) * gn
            og = og_ref[h, rows, :]
            o_ref[h, rows, :] = (on * (og * jax.nn.sigmoid(og))).astype(BF16)

    @pl.when(j == pl.num_programs(1) - 1)
    def _():
        sout_ref[...] = s_ref[:, :, :GLA_DK, :]


def _gla(q, k, la, v, og, s0, gnorm, *, row_off, seq, n_seg, n_sub):
    batch = s0.shape[0]
    chunk = GLA_CHUNK
    assert chunk % n_seg == 0 and batch % n_seg == 0
    seg = chunk // n_seg
    step_rows = n_sub * chunk
    if n_seg > 1:
        assert seq == seg and n_sub == 1
        t_steps = 1
    else:
        assert seq % step_rows == 0
        t_steps = seq // step_rows
    assert row_off % step_rows == 0
    off = row_off // step_rows
    i = np.arange(chunk)
    same = (i[:, None] // seg) == (i[None, :] // seg)
    mcum = jnp.asarray(same & (i[None, :] <= i[:, None]), BF16)
    mall = jnp.asarray(same, BF16)
    sel = jnp.asarray((i[None, :] // seg) == np.arange(LANES)[:, None], BF16)
    ridx = lambda g, j: (off + g * t_steps + j, 0)
    hidx = lambda g, j: (0, off + g * t_steps + j, 0)
    key_spec = pl.BlockSpec((step_rows, GLA_KEY_PAD), ridx)
    val_spec = pl.BlockSpec((GLA_HEADS, step_rows, GLA_DV), hidx)
    st_spec = pl.BlockSpec((n_seg, GLA_HEADS, GLA_DK, GLA_DV), lambda g, j: (g, 0, 0, 0))
    in_specs = [key_spec, key_spec, key_spec, val_spec, val_spec, st_spec, _const((1, GLA_DV)),
                _const((chunk, chunk)), _const((chunk, chunk)), _const((LANES, chunk))]
    args = [q, k, la, v, og, s0, gnorm.reshape(1, GLA_DV), mcum, mall, sel]
    out_spec = pl.BlockSpec((GLA_HEADS, step_rows, GLA_DV), lambda g, j: (0, g * t_steps + j, 0))
    return pl.pallas_call(
        functools.partial(_gla_kernel, chunk=chunk, n_sub=n_sub, n_seg=n_seg),
        grid=(batch // n_seg, t_steps),
        in_specs=in_specs,
        out_specs=[out_spec, st_spec],
        out_shape=[jax.ShapeDtypeStruct((GLA_HEADS, batch * seq, GLA_DV), BF16),
                   jax.ShapeDtypeStruct(s0.shape, F32)],
        scratch_shapes=[pltpu.VMEM((n_seg, GLA_HEADS, GLA_DK_PAD, GLA_DV), F32)],
        compiler_params=_params(("arbitrary", "arbitrary")),
        name="gla",
    )(*args)


def _mem_attn_kernel(q_ref, k_ref, v_ref, g_ref, bd_ref, hm_ref, o_ref, *, tm, bb):
    g = g_ref[...]
    sub = min(tm, 128)
    units = [(i, i * tm + r) for i in range(bb) for r in range(0, tm, sub)]
    scores = []
    for i, r in units:
        q = q_ref[r:r + sub, :]
        qn = q * lax.rsqrt(_seg_mean(q * q, bd_ref[...]) + EPS) * g
        qs = jnp.concatenate([(qn * hm_ref[h]).astype(BF16) for h in range(MEM_HEADS)], axis=0)
        scores.append(_bdot(qs, k_ref[i]))
    probs = []
    for s in scores:
        e = jnp.exp(s - jnp.max(s, axis=-1, keepdims=True))
        probs.append(e * (1.0 / jnp.sum(e, axis=-1, keepdims=True)))
    outs = [_bdot_nt(p, v_ref[i]) for (i, _), p in zip(units, probs)]
    rows = []
    for o in outs:
        acc = o[:sub] * hm_ref[0]
        for h in range(1, MEM_HEADS):
            acc = acc + o[h * sub:(h + 1) * sub] * hm_ref[h]
        rows.append(acc)
    o_ref[...] = jnp.concatenate(rows, axis=0).astype(BF16)


def _mem_attn(mq, mk, mv, qng, *, row_off, seq, tm, bb, layer):
    depth, batch, m, _ = mk.shape
    mk = mk.reshape(depth * batch, m, MEM_WIDTH)
    mv = mv.reshape(depth * batch, m, MEM_WIDTH)
    kv_off = layer * batch // bb
    assert seq % tm == 0 and batch % bb == 0 and (bb == 1 or seq == tm)
    t_steps = seq // tm
    step_rows = bb * tm
    assert row_off % step_rows == 0
    off = row_off // step_rows
    row_spec = pl.BlockSpec((step_rows, MEM_WIDTH), lambda g, j: (off + g * t_steps + j, 0))
    kv_spec = pl.BlockSpec((bb, m, MEM_WIDTH), lambda g, j: (kv_off + g, 0, 0))
    in_specs = [row_spec, kv_spec, kv_spec, _const((1, MEM_WIDTH)), _const((MEM_WIDTH, MEM_WIDTH)),
                _const((MEM_HEADS, 1, MEM_WIDTH))]
    args = [mq, mk, mv, (jnp.tile(qng, MEM_HEADS) * HEAD_DIM ** -0.5).reshape(1, MEM_WIDTH),
            _block_diag_mean(MEM_WIDTH), _head_masks(MEM_HEADS)]
    return pl.pallas_call(
        functools.partial(_mem_attn_kernel, tm=tm, bb=bb),
        grid=(batch // bb, t_steps),
        in_specs=in_specs,
        out_specs=pl.BlockSpec((step_rows, MEM_WIDTH), lambda g, j: (g * t_steps + j, 0)),
        out_shape=jax.ShapeDtypeStruct((batch * seq, MEM_WIDTH), BF16),
        compiler_params=_params(("parallel", "parallel")),
        name="mem_attn",
    )(*args)


def _outproj_kernel(*refs, heads, prompt_tiles, split_residual):
    n_h = 2 if split_residual else 1
    h_refs, refs = refs[:n_h], refs[n_h:]
    (main_p_ref, main_s_ref, mo_p_ref, mo_s_ref, wmain_ref, wmo_ref), refs = refs[:6], refs[6:]
    route_in, (o_ref, *route_out) = refs[:6], refs[6:]
    is_prompt = pl.program_id(0) < prompt_tiles
    pick = lambda p, s: jnp.where(is_prompt, p, s)
    tm = o_ref.shape[0]
    sub = min(tm, 128)
    blocks = []
    for r in range(0, tm, sub):
        rows = slice(r, r + sub)
        acc = pick(h_refs[0][rows, :], h_refs[1][rows, :]) if split_residual else h_refs[0][rows, :]
        acc = acc + _bdot(pick(mo_p_ref[rows, :], mo_s_ref[rows, :]), wmo_ref[...])
        if heads:
            for h in range(heads):
                acc = acc + _bdot(pick(main_p_ref[h, rows, :], main_s_ref[h, rows, :]), wmain_ref[h])
        else:
            acc = acc + _bdot(pick(main_p_ref[rows, :], main_s_ref[rows, :]), wmain_ref[...])
        o_ref[rows, :] = acc
        blocks.append(acc)
    _route_tile(blocks, *route_in, *route_out)


def _outproj(h, main_p, mo_p, main_s, mo_s, w_main, w_mo, router, tm):
    n_p, n_s = mo_p.shape[0], mo_s.shape[0]
    n, d = n_p + n_s, w_mo.shape[1]
    assert n_p % tm == 0 and n_s % tm == 0
    pt = n_p // tm
    heads = main_p.shape[0] if main_p.ndim == 3 else 0
    row = pl.BlockSpec((tm, d), lambda i: (i, 0))
    split = isinstance(h, tuple)
    h_specs, h_args = (_group_specs(tm, d, pt), list(h)) if split else ([row], [h])
    main_specs = _group_specs(tm, main_p.shape[-1], pt, lead=heads or None)
    r_args, r_in_specs, r_out_specs, r_out_shape, r_scratch = _route_operands(*router, n, tm)
    h_new, *routed = pl.pallas_call(
        functools.partial(_outproj_kernel, heads=heads, prompt_tiles=pt, split_residual=split),
        grid=(n // tm,),
        in_specs=(h_specs + main_specs + _group_specs(tm, MEM_WIDTH, pt)
                  + [_const(w_main.shape), _const(w_mo.shape)] + r_in_specs),
        out_specs=[row] + r_out_specs,
        out_shape=[jax.ShapeDtypeStruct((n, d), F32)] + r_out_shape,
        scratch_shapes=[r_scratch],
        compiler_params=_params(("arbitrary",)),
        name="outproj_route",
    )(*h_args, main_p, main_s, mo_p, mo_s, w_main.astype(BF16), w_mo.astype(BF16), *r_args)
    return h_new, routed


def _route_tile(x_blocks, g_ref, whi_ref, wlo_ref, b_ref, before_tok_ref, before_row_ref,
                mi_ref, mf_ref, cnt_ref, tt_ref, carry_ref):
    @pl.when(pl.program_id(0) == 0)
    def _():
        carry_ref[...] = jnp.zeros_like(carry_ref)

    nt = lambda a, b: lax.dot_general(a, b, (((1,), (1,)), ((), ())), preferred_element_type=F32)
    logits = []
    for x in x_blocks:
        x_hi, x_lo = _split(x * _rms_scale(x) * g_ref[...], 2)
        logits.append((nt(whi_ref[...], x_hi) + nt(wlo_ref[...], x_hi) + nt(whi_ref[...], x_lo))[:ROUTER_ROWS])
    logits = jnp.concatenate(logits, axis=1) + b_ref[...]
    tm = logits.shape[1]
    row = lax.broadcasted_iota(jnp.int32, (ROUTER_ROWS, tm), 0)
    far = jnp.int32(2 * LANES)

    def first_max(vals):
        m = jnp.max(vals, axis=0, keepdims=True)
        return m, jnp.min(jnp.where(vals == m, row, far), axis=0, keepdims=True)

    gl = jnp.where(row < N_GROUPS, logits, -jnp.inf)
    gmax, grp = first_max(gl)
    pg_sel = 1.0 / jnp.sum(jnp.exp(gl - gmax), axis=0, keepdims=True)
    lo = ROUTER_LANE0 + grp * EXPERTS_PER_GROUP
    el = jnp.where((row >= lo) & (row < lo + EXPERTS_PER_GROUP), logits, -jnp.inf)
    m1, i1 = first_max(el)
    m2, i2 = first_max(jnp.where(row == i1, -jnp.inf, el))
    e2 = jnp.exp(m2 - m1)
    g1 = pg_sel / (1.0 + e2)
    g2 = pg_sel * e2 / (1.0 + e2)

    oh1 = row == i1
    oh2 = row == i2
    picked = jnp.where(oh1 | oh2, 1.0, 0.0)
    earlier = jnp.dot(picked.astype(BF16), before_tok_ref[...], preferred_element_type=F32)
    cnt_col = jnp.sum(picked, axis=1, keepdims=True)
    cnt_tile = jnp.concatenate([jnp.broadcast_to(cnt_col, (ROUTER_ROWS, LANES)),
                                jnp.zeros((LANES - ROUTER_ROWS, LANES), F32)], axis=0)
    c_hi = jnp.floor(cnt_tile * (1.0 / 32.0))
    c_lo = cnt_tile - 32.0 * c_hi
    first = (32.0 * jnp.dot(before_row_ref[...], c_hi.astype(BF16), preferred_element_type=F32)
             + jnp.dot(before_row_ref[...], c_lo.astype(BF16), preferred_element_type=F32))
    local = first[:ROUTER_ROWS, 0:1] + earlier
    lpos1 = jnp.sum(jnp.where(oh1, local, 0.0), axis=0, keepdims=True)
    lpos2 = jnp.sum(jnp.where(oh2, local, 0.0), axis=0, keepdims=True)
    carry_before = carry_ref[...]
    carry = carry_before + cnt_tile
    carry_ref[...] = carry

    lane = lax.broadcasted_iota(jnp.int32, (LANES, LANES), 1)
    cols = jnp.where(lane == 0, carry_before, jnp.where(lane == 1, cnt_tile, jnp.where(lane == 2, first,
                     jnp.where(lane == 3, carry, 0.0))))
    tables = cols.T
    tt_ref[...] = tables[:8].astype(jnp.int32)
    cnt_ref[...] = tables[3:4]
    row8 = lax.broadcasted_iota(jnp.int32, (ROUTER_META_ROWS, tm), 0)
    zero8 = jnp.zeros((ROUTER_META_ROWS, tm), F32)
    mi_ref[...] = jnp.where(row8 == 0, lpos1, jnp.where(row8 == 1, lpos2, zero8)).astype(jnp.int32)
    stacked = jnp.where(row8 == 0, g1, jnp.where(row8 == 1, g2, jnp.where(row8 == 2, lpos1,
                        jnp.where(row8 == 3, lpos2, zero8))))
    mf_ref[...] = jnp.concatenate([stacked, jnp.zeros((LANES - ROUTER_META_ROWS, tm), F32)], axis=0).T


def _route_operands(g, w_rg, b_rg, w_re, b_re, n, tm):
    d = g.shape[0]
    assert n % tm == 0 and 2 * tm <= 32 * 32
    n_real = N_GROUPS + N_EXPERTS
    w = jnp.pad(jnp.concatenate([w_rg, w_re], axis=1), ((0, 0), (0, LANES - n_real))).T
    b = jnp.pad(jnp.concatenate([b_rg, b_re]), (0, ROUTER_ROWS - n_real)).reshape(ROUTER_ROWS, 1)
    w_hi = w.astype(BF16)
    w_lo = (w - w_hi.astype(F32)).astype(BF16)
    i = np.arange(tm)
    before_tok = jnp.asarray(i[:, None] < i[None, :], BF16)
    e = np.arange(LANES)
    before_row = jnp.asarray(e[None, :] < e[:, None], BF16)
    args = [g.reshape(1, d), w_hi, w_lo, b, before_tok, before_row]
    in_specs = [_const(a.shape) for a in args]
    out_specs = [pl.BlockSpec((ROUTER_META_ROWS, tm), lambda i: (0, i)), pl.BlockSpec((tm, LANES), lambda i: (i, 0)),
                 _const((1, LANES)), pl.BlockSpec((8, LANES), lambda i: (i, 0))]
    out_shape = [jax.ShapeDtypeStruct((ROUTER_META_ROWS, n), jnp.int32), jax.ShapeDtypeStruct((n, LANES), F32),
                 jax.ShapeDtypeStruct((1, LANES), F32), jax.ShapeDtypeStruct((n // tm * 8, LANES), jnp.int32)]
    return args, in_specs, out_specs, out_shape, pltpu.VMEM((LANES, LANES), F32)


def _pack_rows(ref, x, rows, lead=(), row0=0):
    u32 = jnp.uint32
    for w in range(PACKED_SLABS):
        lo = x[:, (2 * w) * LANES:(2 * w + 1) * LANES].astype(BF16).astype(F32)
        hi = x[:, (2 * w + 1) * LANES:(2 * w + 2) * LANES].astype(BF16).astype(F32)
        word = (lax.bitcast_convert_type(lo, u32) >> 16) | (lax.bitcast_convert_type(hi, u32) & u32(0xFFFF0000))
        ref[lead + (pl.ds(row0 * PACKED_SLABS + w, rows, stride=PACKED_SLABS), slice(None))] = word


def _unpack_rows(ref, rows, lead=(), row0=0):
    u32 = jnp.uint32
    slabs = []
    for w in range(PACKED_SLABS):
        word = ref[lead + (pl.ds(row0 * PACKED_SLABS + w, rows, stride=PACKED_SLABS), slice(None))]
        slabs.append(lax.bitcast_convert_type(word << 16, F32).astype(BF16))
        slabs.append(lax.bitcast_convert_type(word & u32(0xFFFF0000), F32).astype(BF16))
    return jnp.concatenate(slabs, axis=1)


RUN_FIELDS = 3
RUN_CHUNK_BITS = 6


def _copy_runs(runs_ref, tile, local_rows, global_rows, sem, *, to_global):
    ps = PACKED_SLABS
    base = tile * (RUN_FIELDS * N_EXPERTS)

    def piece(g0, l0, off, size):
        g = global_rows(pl.multiple_of((g0 + off) * ps, ps), size * ps)
        l = local_rows(pl.multiple_of((l0 + off) * ps, ps), size * ps)
        src, dst = (l, g) if to_global else (g, l)
        pltpu.make_async_copy(src, dst, sem).start()

    def per_expert(e, carry):
        g0 = runs_ref[base + e]
        length = runs_ref[base + N_EXPERTS + e]
        l0 = runs_ref[base + 2 * N_EXPERTS + e]
        big = 1 << RUN_CHUNK_BITS

        def big_piece(c, inner):
            piece(g0, l0, c * big, big)
            return inner

        n_big = length >> RUN_CHUNK_BITS
        lax.fori_loop(0, n_big, big_piece, 0)
        off = n_big * big
        for bit in reversed(range(RUN_CHUNK_BITS)):
            size = 1 << bit

            @pl.when((length & size) != 0)
            def _(off=off, size=size):
                piece(g0, l0, off, size)

            off = off + (length & size)
        return carry

    lax.fori_loop(0, N_EXPERTS, per_expert, 0)


def _fill_pads(pads_ref, buf0, xs_hbm, sem):
    ps = PACKED_SLABS
    for wait in (False, True):
        def per_tail_tile(t, carry, wait=wait):
            rows = MOE_TILE * ps
            copy = pltpu.make_async_copy(
                buf0.at[pl.ds(0, rows)],
                xs_hbm.at[pl.ds(pl.multiple_of((pads_ref[2 * N_EXPERTS] + t) * rows, rows), rows)], sem)
            copy.wait() if wait else copy.start()
            return carry

        lax.fori_loop(0, pads_ref[2 * N_EXPERTS + 1], per_tail_tile, 0)

        def per_expert(e, carry, wait=wait):
            first, length = pads_ref[e], pads_ref[N_EXPERTS + e]
            off = 0
            for bit in reversed(range((MOE_TILE - 1).bit_length())):
                size = 1 << bit

                @pl.when((length & size) != 0)
                def _(off=off, size=size):
                    copy = pltpu.make_async_copy(
                        buf0.at[pl.ds(pl.multiple_of(off * ps, ps), size * ps)],
                        xs_hbm.at[pl.ds(pl.multiple_of((first + off) * ps, ps), size * ps)], sem)
                    copy.wait() if wait else copy.start()

                off = off + (length & size)
            return carry

        lax.fori_loop(0, N_EXPERTS, per_expert, 0)


def _dispatch_kernel(runs_ref, pads_ref, h_ref, g_ref, meta_ref, xs_hbm, buf, sem, pad_sem, *, tm, steps):
    i = pl.program_id(0)
    slot = lax.rem(i, 2)
    ns = 2 * tm
    assert ns >= MOE_TILE

    def wait_slot(sl):
        pltpu.make_async_copy(buf.at[sl], xs_hbm.at[pl.ds(0, ns * PACKED_SLABS)], sem.at[sl]).wait()

    @pl.when(i >= 2)
    def _():
        wait_slot(slot)

    x = h_ref[...]
    xn = (x * _rms_scale(x) * g_ref[...]).astype(BF16)
    j = lax.broadcasted_iota(jnp.int32, (ns, tm), 0)
    pick = jnp.where((j == meta_ref[0:1, :]) | (j == meta_ref[1:2, :]), 1.0, 0.0).astype(BF16)
    _pack_rows(buf, jnp.dot(pick, xn, preferred_element_type=F32), ns, (slot,))
    _copy_runs(runs_ref, i, lambda start, size: buf.at[slot, pl.ds(start, size)],
               lambda start, size: xs_hbm.at[pl.ds(start, size)], sem.at[slot], to_global=True)

    @pl.when(i == 0)
    def _():
        _fill_pads(pads_ref, buf.at[0], xs_hbm, pad_sem)

    @pl.when(i == steps - 1)
    def _():
        wait_slot(slot)
        if steps > 1:
            wait_slot(1 - slot)


def _dispatch(h, g, meta, runs, pads, tm, n_slots):
    n, d = h.shape
    steps = n // tm
    return pl.pallas_call(
        functools.partial(_dispatch_kernel, tm=tm, steps=steps),
        grid_spec=pltpu.PrefetchScalarGridSpec(
            num_scalar_prefetch=2,
            grid=(steps,),
            in_specs=[pl.BlockSpec((tm, d), lambda i, *_: (i, 0)),
                      pl.BlockSpec((1, d), lambda i, *_: (0, 0)),
                      pl.BlockSpec((ROUTER_META_ROWS, tm), lambda i, *_: (0, i))],
            out_specs=pl.BlockSpec(memory_space=pl.ANY),
            scratch_shapes=[pltpu.VMEM((2, 2 * tm * PACKED_SLABS, LANES), jnp.uint32),
                            pltpu.SemaphoreType.DMA((2,)), pltpu.SemaphoreType.DMA],
        ),
        out_shape=jax.ShapeDtypeStruct((n_slots * PACKED_SLABS, LANES), jnp.uint32),
        compiler_params=_params(("arbitrary",)),
        name="moe_dispatch",
    )(runs, pads, h, g.reshape(1, d), meta)


def _expert_kernel(tile_ref, exp_ref, new_ref, wslot_ref, next_ref, n_ref,
                   xs_ref, wg_hbm, wu_hbm, wd_hbm, ys_ref,
                   wgf, wuf, wdf, wsem, wgb, wub, wdb, *, tm, sub, layer):
    w = pl.program_id(0)

    def weight_copies(e, slot):
        return [pltpu.make_async_copy(hbm.at[layer, e], buf.at[slot], wsem.at[slot])
                for hbm, buf in ((wg_hbm, wgf), (wu_hbm, wuf), (wd_hbm, wdf))]

    @pl.when(w < n_ref[0])
    def _():
        @pl.when(new_ref[w] != 0)
        def _():
            slot = wslot_ref[w]

            @pl.when(w == 0)
            def _():
                for c in weight_copies(exp_ref[w], slot):
                    c.start()

            for c in weight_copies(exp_ref[w], slot):
                c.wait()
            wgb[...] = wgf[slot].astype(BF16)
            wub[...] = wuf[slot].astype(BF16)
            wdb[...] = wdf[slot].astype(BF16)

            @pl.when(next_ref[w] >= 0)
            def _():
                for c in weight_copies(next_ref[w], 1 - slot):
                    c.start()

        n_blocks = tm // sub

        def up(s):
            x = _unpack_rows(xs_ref, sub, row0=s * sub)
            return (jnp.dot(x, wgb[...], preferred_element_type=F32),
                    jnp.dot(x, wub[...], preferred_element_type=F32))

        ups = {0: up(0)}
        for s in range(n_blocks):
            if s + 1 < n_blocks:
                ups[s + 1] = up(s + 1)
            hg, hu = ups.pop(s)
            act = (hg * jax.nn.sigmoid(hg) * hu).astype(BF16)
            y = jnp.dot(act, wdb[...], preferred_element_type=F32)
            _pack_rows(ys_ref, y, sub, row0=s * sub)

    @pl.when(w >= n_ref[0])
    def _():
        ys_ref[...] = jnp.zeros_like(ys_ref)


def _experts(xs, items, w_g, w_u, w_d, layer):
    d = D_MODEL
    tm = MOE_TILE
    packed_spec = pl.BlockSpec((tm * PACKED_SLABS, LANES), lambda w, tile, *_: (tile[w], 0))
    any_spec = pl.BlockSpec(memory_space=pl.ANY)
    return pl.pallas_call(
        functools.partial(_expert_kernel, tm=tm, sub=MOE_SUB, layer=layer),
        grid_spec=pltpu.PrefetchScalarGridSpec(
            num_scalar_prefetch=len(items),
            grid=(items[0].shape[0],),
            in_specs=[packed_spec, any_spec, any_spec, any_spec],
            out_specs=pl.BlockSpec((tm * PACKED_SLABS, LANES), lambda w, *_: (w, 0)),
            scratch_shapes=[pltpu.VMEM((2, d, D_EXPERT), F32), pltpu.VMEM((2, d, D_EXPERT), F32),
                            pltpu.VMEM((2, D_EXPERT, d), F32), pltpu.SemaphoreType.DMA((2,)),
                            pltpu.VMEM((d, D_EXPERT), BF16), pltpu.VMEM((d, D_EXPERT), BF16),
                            pltpu.VMEM((D_EXPERT, d), BF16)],
        ),
        out_shape=jax.ShapeDtypeStruct(xs.shape, jnp.uint32),
        compiler_params=_params(("arbitrary",)),
        name="moe_experts",
    )(*items, xs, w_g, w_u, w_d)


def _combine_kernel(runs_ref, h_ref, gate_ref, ys_hbm, *refs, tm, steps, prompt_tiles):
    out_refs, (buf, sem) = refs[:-2], refs[-2:]
    i = pl.program_id(0)
    slot = lax.rem(i, 2)
    ns = 2 * tm

    def issue(step, sl):
        _copy_runs(runs_ref, step, lambda start, size: buf.at[sl, pl.ds(start, size)],
                   lambda start, size: ys_hbm.at[pl.ds(start, size)], sem.at[sl], to_global=False)

    @pl.when(i == 0)
    def _():
        issue(0, 0)

    @pl.when(i + 1 < steps)
    def _():
        issue(i + 1, 1 - slot)

    pltpu.make_async_copy(ys_hbm.at[pl.ds(0, ns * PACKED_SLABS)], buf.at[slot], sem.at[slot]).wait()
    y = _unpack_rows(buf, ns, (slot,))
    g = gate_ref[...]
    j = lax.broadcasted_iota(jnp.int32, (tm, ns), 1)
    mix = (jnp.where(j == g[:, 2:3].astype(jnp.int32), g[:, 0:1], 0.0)
           + jnp.where(j == g[:, 3:4].astype(jnp.int32), g[:, 1:2], 0.0)).astype(BF16)
    out = h_ref[...] + jnp.dot(mix, y, preferred_element_type=F32)
    if len(out_refs) == 1:
        out_refs[0][...] = out
    else:
        @pl.when(i < prompt_tiles)
        def _():
            out_refs[0][...] = out

        @pl.when(i >= prompt_tiles)
        def _():
            out_refs[1][...] = out


def _combine(h, gates, ys, runs, tm, split_rows=None):
    n, d = h.shape
    steps = n // tm
    row = pl.BlockSpec((tm, d), lambda i, pos: (i, 0))
    if split_rows is None:
        pt, out_specs, out_shape = 0, row, jax.ShapeDtypeStruct((n, d), F32)
    else:
        assert split_rows % tm == 0
        pt = split_rows // tm
        out_specs = _group_specs(tm, d, pt)
        out_shape = [jax.ShapeDtypeStruct((split_rows, d), F32), jax.ShapeDtypeStruct((n - split_rows, d), F32)]
    return pl.pallas_call(
        functools.partial(_combine_kernel, tm=tm, steps=steps, prompt_tiles=pt),
        grid_spec=pltpu.PrefetchScalarGridSpec(
            num_scalar_prefetch=1,
            grid=(steps,),
            in_specs=[row, pl.BlockSpec((tm, LANES), lambda i, pos: (i, 0)),
                      pl.BlockSpec(memory_space=pl.ANY)],
            out_specs=out_specs,
            scratch_shapes=[pltpu.VMEM((2, 2 * tm * PACKED_SLABS, LANES), jnp.uint32),
                            pltpu.SemaphoreType.DMA((2,))],
        ),
        out_shape=out_shape,
        compiler_params=_params(("arbitrary",)),
        name="moe_combine",
    )(runs, h, gates, ys)


def _lookup(tables, idx):
    hit = idx[:, None] == jnp.arange(tables.shape[1], dtype=idx.dtype)[None, :]
    return jnp.sum(jnp.where(hit[None], tables[:, None, :], 0), axis=2)


def _work_items(tiles_e, max_items):
    item_end = jnp.cumsum(tiles_e)
    n_items = item_end[-1]
    w = jnp.minimum(jnp.arange(max_items, dtype=jnp.int32), n_items - 1)
    expert = jnp.sum(w[:, None] >= item_end[None, :], axis=1).astype(jnp.int32)
    prev_expert = jnp.concatenate([jnp.full((1,), -1, jnp.int32), expert[:-1]])
    new_expert = expert != prev_expert
    weight_slot = (jnp.cumsum(new_expert.astype(jnp.int32)) - 1) % 2
    ids = jnp.arange(N_EXPERTS, dtype=jnp.int32)
    later = (ids[None, :] > ids[:, None]) & (tiles_e[None, :] > 0)
    following = jnp.min(jnp.where(later, ids[None, :], N_EXPERTS), axis=1)
    following = jnp.where(following == N_EXPERTS, -1, following)
    next_expert, = _lookup(following[None, :], expert)
    as_i32 = lambda a: a.astype(jnp.int32)
    return (as_i32(w), expert, as_i32(new_expert), as_i32(weight_slot), as_i32(next_expert),
            as_i32(n_items).reshape(1))


def _moe(h, routed, g, w_g, w_u, w_d, layer, tm, split_rows=None):
    n, _ = h.shape
    meta, gates, cnt, tables = routed
    experts = slice(ROUTER_LANE0, ROUTER_LANE0 + N_EXPERTS)
    counts = cnt[0, experts].astype(jnp.int32)
    tiles_e = (counts + MOE_TILE - 1) // MOE_TILE
    starts = (jnp.cumsum(tiles_e) - tiles_e) * MOE_TILE
    max_tiles = -(-2 * n // MOE_TILE) + N_EXPERTS
    tables = tables.reshape(n // tm, 8, LANES)[:, :RUN_FIELDS, experts]
    runs = tables.at[:, 0, :].add(starts[None, :]).reshape(-1)
    used = jnp.sum(tiles_e)
    pads = jnp.concatenate([starts + counts, tiles_e * MOE_TILE - counts, jnp.stack([used, max_tiles - used])])
    xs = _dispatch(h, g, meta, runs, pads, tm, max_tiles * MOE_TILE)
    ys = _experts(xs, _work_items(tiles_e, max_tiles), w_g, w_u, w_d, layer)
    return _combine(h, gates, ys, runs, tm, split_rows)


def _inproj_b_kernel(x_ref, gkv_ref, gmix_ref, wkv_ref, win_ref, kng_ref, qng_ref, bdk_ref, hsum_ref, hexp_ref,
                     q_ref, mq_ref, k_ref, v_ref, kt_ref):
    x = x_ref[...]
    xr = x * _rms_scale(x)
    kv = _bdot(xr * gkv_ref[...], wkv_ref[...])
    k = kv[:, :KV_WIDTH]
    k = k * lax.rsqrt(_seg_mean(k * k, bdk_ref[...]) + EPS) * kng_ref[...]
    k_ref[...] = k
    kt_ref[...] = k.T.astype(BF16)
    v_ref[...] = kv[:, KV_WIDTH:]
    proj = _bdot(xr * gmix_ref[...], win_ref[...])
    q = proj[:, :MAIN_WIDTH]
    ms = None
    for p in _split(q * q, 2):
        t = jnp.dot(p, hsum_ref[...], preferred_element_type=F32)
        ms = t if ms is None else ms + t
    scale = None
    for p in _split(lax.rsqrt(ms + EPS), 2):
        t = jnp.dot(p, hexp_ref[...], preferred_element_type=F32)
        scale = t if scale is None else scale + t
    q_ref[...] = (q * scale * qng_ref[...]).astype(BF16)
    mq_ref[...] = proj[:, MAIN_WIDTH:]


def _swa_perm():
    g, kh, dd = np.meshgrid(np.arange(SWA_GROUP), np.arange(SWA_KV_HEADS), np.arange(HEAD_DIM), indexing="ij")
    return ((kh * SWA_GROUP + g) * HEAD_DIM + dd).reshape(-1)


def _inproj_b(x, g_kv, g_mix, w_kv, w_in, kng, qng):
    n, d = x.shape
    tm = _row_tile(n)
    perm = _swa_perm()
    w_in_p = jnp.concatenate([w_in[:, :MAIN_WIDTH][:, perm], w_in[:, MAIN_WIDTH:]], axis=1).astype(BF16)
    qng_t = (jnp.tile(qng, SWA_HEADS) * HEAD_DIM ** -0.5).reshape(1, MAIN_WIDTH)
    member = (np.arange(MAIN_WIDTH)[:, None] // HEAD_DIM == np.arange(LANES)[None, :]).astype(np.float32)
    row = lambda w: pl.BlockSpec((tm, w), lambda i: (i, 0))
    return pl.pallas_call(
        _inproj_b_kernel,
        grid=(n // tm,),
        in_specs=[row(d), _const((1, d)), _const((1, d)), _const((d, 2 * KV_WIDTH)), _const((d, d)),
                  _const((1, KV_WIDTH)), _const((1, MAIN_WIDTH)), _const((KV_WIDTH, KV_WIDTH)),
                  _const((MAIN_WIDTH, LANES)), _const((LANES, MAIN_WIDTH))],
        out_specs=[row(MAIN_WIDTH), row(MEM_WIDTH), row(KV_WIDTH), row(KV_WIDTH),
                   pl.BlockSpec((KV_WIDTH, tm), lambda i: (0, i))],
        out_shape=[jax.ShapeDtypeStruct((n, MAIN_WIDTH), BF16), jax.ShapeDtypeStruct((n, MEM_WIDTH), F32),
                   jax.ShapeDtypeStruct((n, KV_WIDTH), F32), jax.ShapeDtypeStruct((n, KV_WIDTH), F32),
                   jax.ShapeDtypeStruct((KV_WIDTH, n), BF16)],
        compiler_params=_params(("parallel",)),
        name="inproj_b",
    )(x, g_kv.reshape(1, d), g_mix.reshape(1, d), w_kv.astype(BF16), w_in_p,
      jnp.tile(kng, SWA_KV_HEADS).reshape(1, KV_WIDTH), qng_t,
      _block_diag_mean(KV_WIDTH), jnp.asarray(member / HEAD_DIM, BF16), jnp.asarray(member.T, BF16))


def _softmax_with_sink(s, sink):
    m = jnp.maximum(jnp.max(s, axis=-1, keepdims=True), sink)
    e = jnp.exp(s - m)
    r = 1.0 / (jnp.sum(e, axis=-1, keepdims=True) + jnp.exp(sink - m))
    return (e * r).astype(BF16)


def _swa_bias(tq):
    slopes = 2.0 ** (-8.0 * np.arange(1, SWA_HEADS + 1, dtype=np.float64) / SWA_HEADS)
    dist = np.arange(tq)[:, None] + WINDOW - np.arange(WINDOW + tq)[None, :]
    valid = (dist >= 0) & (dist <= WINDOW)
    return np.stack([np.where(valid, -s * dist, NEG_BIG) for s in slopes]).astype(np.float32)


def _swa_prompt_kernel(sink_ref, q_ref, ktp_ref, kto_ref, vp_ref, vo_ref, bias_ref, hm_ref, o_ref, *, nb):
    w = WINDOW
    key = lax.broadcasted_iota(jnp.int32, (w, 2 * w), 1)
    has_prev = (pl.program_id(0) > 0) | (key >= w)
    heads = [(g, kh) for g in range(SWA_GROUP) for kh in range(SWA_KV_HEADS)]
    kts, vvs = [], []
    for b in range(nb):
        kt_prev = ktp_ref[...] if b == 0 else kto_ref[:, (b - 1) * w:b * w]
        v_prev = vp_ref[...] if b == 0 else vo_ref[(b - 1) * w:b * w, :]
        kts.append(jnp.concatenate([kt_prev, kto_ref[:, b * w:(b + 1) * w]], axis=1).astype(BF16))
        vvs.append(jnp.concatenate([v_prev, vo_ref[b * w:(b + 1) * w, :]], axis=0).astype(BF16))
    scores = [[jnp.dot(q_ref[b * w:(b + 1) * w, g * KV_WIDTH:(g + 1) * KV_WIDTH] * hm_ref[kh].astype(BF16),
                       kts[b], preferred_element_type=F32) for g, kh in heads] for b in range(nb)]
    for b in range(nb):
        probs = []
        for (g, kh), s in zip(heads, scores[b]):
            h = kh * SWA_GROUP + g
            s = s + bias_ref[h]
            if b == 0:
                s = jnp.where(has_prev, s, NEG_BIG)
            probs.append(_softmax_with_sink(s, sink_ref[h]))
        outs = [jnp.dot(p, vvs[b], preferred_element_type=F32) for p in probs]
        for g in range(SWA_GROUP):
            acc = None
            for (cg, kh), o in zip(heads, outs):
                if cg == g:
                    t = o * hm_ref[kh]
                    acc = t if acc is None else acc + t
            o_ref[b * w:(b + 1) * w, g * KV_WIDTH:(g + 1) * KV_WIDTH] = acc.astype(BF16)


def _swa_prompt(q, kt, v, sinks, *, n_rows, nb):
    w = WINDOW
    step = nb * w
    assert n_rows % step == 0
    prev = lambda j, sink: jnp.maximum(j * nb - 1, 0)
    return pl.pallas_call(
        functools.partial(_swa_prompt_kernel, nb=nb),
        grid_spec=pltpu.PrefetchScalarGridSpec(
            num_scalar_prefetch=1,
            grid=(n_rows // step,),
            in_specs=[pl.BlockSpec((step, MAIN_WIDTH), lambda j, sink: (j, 0)),
                      pl.BlockSpec((KV_WIDTH, w), lambda j, sink: (0, prev(j, sink))),
                      pl.BlockSpec((KV_WIDTH, step), lambda j, sink: (0, j)),
                      pl.BlockSpec((w, KV_WIDTH), lambda j, sink: (prev(j, sink), 0)),
                      pl.BlockSpec((step, KV_WIDTH), lambda j, sink: (j, 0)),
                      pl.BlockSpec((SWA_HEADS, w, 2 * w), lambda j, sink: (0, 0, 0)),
                      pl.BlockSpec((SWA_KV_HEADS, 1, KV_WIDTH), lambda j, sink: (0, 0, 0))],
            out_specs=pl.BlockSpec((step, MAIN_WIDTH), lambda j, sink: (j, 0)),
        ),
        out_shape=jax.ShapeDtypeStruct((n_rows, MAIN_WIDTH), BF16),
        compiler_params=_params(("arbitrary",)),
        name="swa_prompt",
    )(sinks.astype(F32), q, kt, kt, v, v, jnp.asarray(_swa_bias(w)), _head_masks(SWA_KV_HEADS))


def _swa_sample_kernel(q_ref, kp_ref, ko_ref, vp_ref, vo_ref, bias_ref, sink_ref, hm_ref, o_ref, *, nb, tq):
    w = WINDOW
    heads = [(kh, g) for kh in range(SWA_KV_HEADS) for g in range(SWA_GROUP)]
    kks, vvs, scores = [], [], []
    q = q_ref[...].astype(F32)
    for i in range(nb):
        win = slice(i * KV_WIDTH, (i + 1) * KV_WIDTH)
        kks.append(jnp.concatenate([kp_ref[win, :].T, ko_ref[i * tq:(i + 1) * tq, :]], axis=0))
        vvs.append(jnp.concatenate([vp_ref[win, :].T, vo_ref[i * tq:(i + 1) * tq, :]], axis=0))
        qs = jnp.concatenate([q[i * tq:(i + 1) * tq, g * KV_WIDTH:(g + 1) * KV_WIDTH] * hm_ref[kh]
                              for kh, g in heads], axis=0)
        scores.append(_bdot_nt(qs, kks[i]))
    probs = [_softmax_with_sink(s + bias_ref[...], sink_ref[...]) for s in scores]
    outs = [_bdot(p, vv) for p, vv in zip(probs, vvs)]
    for g in range(SWA_GROUP):
        rows = []
        for i in range(nb):
            acc = None
            for r, (kh, hg) in enumerate(heads):
                if hg == g:
                    t = outs[i][r * tq:(r + 1) * tq] * hm_ref[kh]
                    acc = t if acc is None else acc + t
            rows.append(acc)
        o_ref[:, g * KV_WIDTH:(g + 1) * KV_WIDTH] = jnp.concatenate(rows, axis=0).astype(BF16)


def _swa_sample(q, k_win, v_win, k, v, sinks, *, row_off, batch, tq, nb):
    w = WINDOW
    assert batch % nb == 0 and row_off % (nb * tq) == 0
    off = row_off // (nb * tq)
    bias = jnp.asarray(_swa_bias(tq).reshape(SWA_HEADS * tq, w + tq))
    sink_col = jnp.repeat(sinks.astype(F32), tq).reshape(SWA_HEADS * tq, 1)
    own = lambda width: pl.BlockSpec((nb * tq, width), lambda b: (off + b, 0))
    win = pl.BlockSpec((nb * KV_WIDTH, w), lambda b: (b, 0))
    return pl.pallas_call(
        functools.partial(_swa_sample_kernel, nb=nb, tq=tq),
        grid=(batch // nb,),
        in_specs=[own(MAIN_WIDTH), win, own(KV_WIDTH), win, own(KV_WIDTH), _const(bias.shape),
                  _const(sink_col.shape), _const((SWA_KV_HEADS, 1, KV_WIDTH))],
        out_specs=pl.BlockSpec((nb * tq, MAIN_WIDTH), lambda b: (b, 0)),
        out_shape=jax.ShapeDtypeStruct((batch * tq, MAIN_WIDTH), BF16),
        compiler_params=_params(("arbitrary",)),
        name="swa_sample",
    )(q, k_win, k, v_win, v, bias, sink_col, _head_masks(SWA_KV_HEADS))


def kernel(x_prompt, x_sample, state_gla, cache_win_k, cache_win_v, cache_mem_k, cache_mem_v, mem_prompt, norm_mix_g, norm_ffn_g, norm_mem_g, w_mem_kv, mem_qn_g, mem_kn_g, w_out, w_in_a, w_gate_lr, b_gate_lr, gla_norm_g, w_in_b, swa_qn_g, swa_sinks, norm_kv_g, w_kv, swa_kn_g, w_router_group, b_router_group, w_router_expert, b_router_expert, w_exp_gate, w_exp_up, w_exp_down):
    bp, tp, d = x_prompt.shape
    bs, ts, _ = x_sample.shape
    assert bp == 1 and tp % WINDOW == 0 and ts * (GLA_CHUNK // ts) == GLA_CHUNK
    n_p, n_s = bp * tp, bs * ts
    w_buf = cache_win_k.shape[1]
    assert w_buf == WINDOW
    x_p, x_s = x_prompt.reshape(n_p, d), x_sample.reshape(n_s, d)

    mem_k_p, mem_v_p = _mem_kv(mem_prompt, norm_mem_g, w_mem_kv, mem_kn_g)
    feature_major = lambda c: jnp.moveaxis(c, -3, -1).reshape(*c.shape[:-3], c.shape[-2] * c.shape[-1], c.shape[-3])
    cmk, cmv = feature_major(cache_mem_k), feature_major(cache_mem_v)

    def mem_attend(mq, l):
        tm_p = _row_tile(tp, 512)
        mo_p = _mem_attn(mq, mem_k_p, mem_v_p, mem_qn_g[l], row_off=0, seq=tp, tm=tm_p, bb=1, layer=l)
        mo_s = _mem_attn(mq, cmk, cmv, mem_qn_g[l], row_off=n_p, seq=ts, tm=ts, bb=16, layer=l)
        return mo_p, mo_s

    tm = _row_tile(math.gcd(n_p, n_s), MOE_TILE)
    router = lambda l: (norm_ffn_g[l], w_router_group[l], b_router_group[l], w_router_expert[l],
                        b_router_expert[l])

    def moe(h, routed, l, split_rows=None):
        return _moe(h, routed, norm_ffn_g[l], w_exp_gate, w_exp_up, w_exp_down, l, tm, split_rows)

    q, k, la, v, og, mq = _inproj_a(x_p, x_s, norm_mix_g[0], w_in_a[0], w_gate_lr[0], b_gate_lr[0])
    zero_state = jnp.zeros((bp, GLA_HEADS, GLA_DK, GLA_DV), F32)
    n_sub = max(1, min(8, tp // GLA_CHUNK))
    main_p, gla_p = _gla(q, k, la, v, og, zero_state, gla_norm_g[0], row_off=0, seq=tp, n_seg=1, n_sub=n_sub)
    main_s, gla_s = _gla(q, k, la, v, og, state_gla[0], gla_norm_g[0], row_off=n_p, seq=ts,
                         n_seg=GLA_CHUNK // ts, n_sub=1)
    mo_p, mo_s = mem_attend(mq, 0)
    w_o = w_out[0]
    h, routed = _outproj((x_p, x_s), main_p, mo_p, main_s, mo_s, w_o[:MAIN_WIDTH].reshape(GLA_HEADS, GLA_DV, d),
                         w_o[MAIN_WIDTH:], router(0), tm)
    h = moe(h, routed, 0)

    q, mq, k_sh, v_sh, kt_sh = _inproj_b(h, norm_kv_g, norm_mix_g[1], w_kv, w_in_b[0], swa_kn_g, swa_qn_g[0])
    ck = feature_major(cache_win_k).reshape(bs * KV_WIDTH, w_buf)
    cv = feature_major(cache_win_v).reshape(bs * KV_WIDTH, w_buf)
    main_p = _swa_prompt(q, kt_sh, v_sh, swa_sinks[0], n_rows=n_p, nb=4)
    main_s = _swa_sample(q, ck, cv, k_sh, v_sh, swa_sinks[0], row_off=n_p, batch=bs, tq=ts, nb=16)
    mo_p, mo_s = mem_attend(mq, 1)
    w_o = w_out[1]
    h, routed = _outproj(h, main_p, mo_p, main_s, mo_s, w_o[:MAIN_WIDTH][_swa_perm()], w_o[MAIN_WIDTH:],
                         router(1), tm)
    y_p, y_s = moe(h, routed, 1, split_rows=n_p)

    y_prompt = y_p.reshape(bp, tp, d)
    y_sample = y_s.reshape(bs, ts, d)
    k_new = k_sh[n_p:].reshape(bs, ts, SWA_KV_HEADS, HEAD_DIM)
    v_new = v_sh[n_p:].reshape(bs, ts, SWA_KV_HEADS, HEAD_DIM)
    win_k_s = jnp.concatenate([cache_win_k, k_new], axis=1)[:, -w_buf:]
    win_v_s = jnp.concatenate([cache_win_v, v_new], axis=1)[:, -w_buf:]
    win_k_p = k_sh[n_p - WINDOW:n_p].reshape(bp, WINDOW, SWA_KV_HEADS, HEAD_DIM)
    win_v_p = v_sh[n_p - WINDOW:n_p].reshape(bp, WINDOW, SWA_KV_HEADS, HEAD_DIM)
    token_major = lambda c: jnp.moveaxis(c.reshape(*c.shape[:-2], MEM_HEADS, HEAD_DIM, c.shape[-1]), -1, -3)
    return (y_prompt, y_sample, gla_p[None], gla_s[None], win_k_p, win_v_p, win_k_s, win_v_s,
            token_major(mem_k_p), token_major(mem_v_p))
```

```python
import functools
import math

import numpy as np
import jax
import jax.numpy as jnp
from jax import lax
from jax.experimental import pallas as pl
from jax.experimental.pallas import tpu as pltpu

F32 = jnp.float32
BF16 = jnp.bfloat16

D_MODEL = 1024
MEM_LEN = 256
MEM_HEADS = 4
HEAD_DIM = 64
MEM_WIDTH = MEM_HEADS * HEAD_DIM
MAIN_WIDTH = D_MODEL - MEM_WIDTH
GLA_HEADS = 4
GLA_DV = MAIN_WIDTH // GLA_HEADS
GLA_DK = GLA_DV // 2
GLA_DK_PAD = 128
GLA_KEY_WIDTH = GLA_HEADS * GLA_DK
GLA_KEY_PAD = GLA_HEADS * GLA_DK_PAD
GLA_GATE_RANK = 16
GLA_TAU = 16.0
GLA_CHUNK = 64
SWA_HEADS = MAIN_WIDTH // HEAD_DIM
SWA_KV_HEADS = 4
SWA_GROUP = SWA_HEADS // SWA_KV_HEADS
KV_WIDTH = SWA_KV_HEADS * HEAD_DIM
WINDOW = 128
N_GROUPS = 4
EXPERTS_PER_GROUP = 8
N_EXPERTS = N_GROUPS * EXPERTS_PER_GROUP
D_EXPERT = 512
EPS = 1e-6
LANES = 128
NEG_BIG = -1e30
VMEM_LIMIT = 56 * 1024 * 1024
MOE_TILE = 512
MOE_SUB = 128
ROUTER_LANE0 = N_GROUPS
ROUTER_META_ROWS = 8
ROUTER_ROWS = 40
SLABS = D_MODEL // LANES
PACKED_SLABS = SLABS // 2


def _bdot(a, b):
    return jnp.dot(a.astype(BF16), b.astype(BF16), preferred_element_type=F32)


def _bdot_nt(a, b):
    return lax.dot_general(a.astype(BF16), b.astype(BF16), (((1,), (1,)), ((), ())),
                           preferred_element_type=F32)


def _bdot_tn(a, b):
    return lax.dot_general(a.astype(BF16), b.astype(BF16), (((0,), (0,)), ((), ())),
                           preferred_element_type=F32)


def _split(x, n):
    parts = []
    for _ in range(n - 1):
        p = x.astype(BF16)
        parts.append(p)
        x = x - p.astype(F32)
    parts.append(x.astype(BF16))
    return parts


def _exact_left_dot(m, x, n=2):
    out = None
    for p in _split(x, n):
        t = jnp.dot(m, p, preferred_element_type=F32)
        out = t if out is None else out + t
    return out


def _seg_mean(x2, bd):
    out = None
    for p in _split(x2, 2):
        t = jnp.dot(p, bd, preferred_element_type=F32)
        out = t if out is None else out + t
    return out


def _rms_scale(x):
    return lax.rsqrt(jnp.mean(x * x, axis=-1, keepdims=True) + EPS)


def _row_tile(n, cap=512):
    t = cap
    while t > 8 and n % t:
        t //= 2
    assert n % t == 0, n
    return t


def _params(sem):
    return pltpu.CompilerParams(dimension_semantics=sem, vmem_limit_bytes=VMEM_LIMIT)


def _const(shape):
    nd = len(shape)
    return pl.BlockSpec(shape, lambda *_: (0,) * nd)


def _group_specs(tm, width, prompt_tiles, lead=None):
    p_idx = lambda i, *_: jnp.minimum(i, prompt_tiles - 1)
    s_idx = lambda i, *_: jnp.maximum(i - prompt_tiles, 0)
    if lead is None:
        return [pl.BlockSpec((tm, width), lambda i, *_, f=f: (f(i), 0)) for f in (p_idx, s_idx)]
    return [pl.BlockSpec((lead, tm, width), lambda i, *_, f=f: (0, f(i), 0)) for f in (p_idx, s_idx)]


def _block_diag_mean(width):
    i = np.arange(width)
    return jnp.asarray((i[:, None] // HEAD_DIM == i[None, :] // HEAD_DIM) / HEAD_DIM, BF16)


def _head_masks(n_heads):
    i = np.arange(n_heads * HEAD_DIM)
    return jnp.asarray((i[None, :] // HEAD_DIM == np.arange(n_heads)[:, None]), F32)[:, None, :]


def _mem_kv_kernel(mem_ref, g_ref, w_ref, kng_ref, bd_ref, k_ref, v_ref):
    x = mem_ref[0]
    hn = x * _rms_scale(x) * g_ref[0]
    kv = _bdot(hn, w_ref[0])
    k = kv[:, :MEM_WIDTH]
    k = k * lax.rsqrt(_seg_mean(k * k, bd_ref[...]) + EPS) * kng_ref[0]
    k_ref[0, 0] = k.T
    v_ref[0, 0] = kv[:, MEM_WIDTH:].T


def _mem_kv(mem, g, w, kng):
    depth, (b, m, d) = w.shape[0], mem.shape
    out = jax.ShapeDtypeStruct((depth, b, m, MEM_WIDTH), F32)
    blk = pl.BlockSpec((1, 1, m, MEM_WIDTH), lambda l, i: (l, i, 0, 0))
    return pl.pallas_call(
        _mem_kv_kernel,
        grid=(depth, b),
        in_specs=[pl.BlockSpec((1, m, d), lambda l, i: (i, 0, 0)),
                  pl.BlockSpec((1, 1, d), lambda l, i: (l, 0, 0)),
                  pl.BlockSpec((1, d, 2 * MEM_WIDTH), lambda l, i: (l, 0, 0)),
                  pl.BlockSpec((1, 1, MEM_WIDTH), lambda l, i: (l, 0, 0)),
                  _const((MEM_WIDTH, MEM_WIDTH))],
        out_specs=[blk, blk],
        out_shape=[out, out],
        compiler_params=_params(("arbitrary", "arbitrary")),
        name="mem_kv",
    )(mem, g.reshape(depth, 1, d), w.astype(BF16),
      jnp.tile(kng, (1, MEM_HEADS)).reshape(depth, 1, MEM_WIDTH), _block_diag_mean(MEM_WIDTH))


def _inproj_a_kernel(xp_ref, xs_ref, g_ref, wq_ref, wk_ref, wv_ref, wog_ref, wlr_ref, wmq_ref, wgl_ref, bgl_ref,
                     q_ref, k_ref, la_ref, v_ref, og_ref, mq_ref, *, prompt_tiles):
    x = jnp.where(pl.program_id(0) < prompt_tiles, xp_ref[...], xs_ref[...])
    hn = (x * _rms_scale(x) * g_ref[...]).astype(BF16)
    q_ref[...] = jnp.dot(hn, wq_ref[...], preferred_element_type=F32) * (GLA_DK ** -0.5)
    k_ref[...] = jnp.dot(hn, wk_ref[...], preferred_element_type=F32)
    for h in range(GLA_HEADS):
        v_ref[h] = jnp.dot(hn, wv_ref[h], preferred_element_type=F32).astype(BF16)
        og_ref[h] = jnp.dot(hn, wog_ref[h], preferred_element_type=F32)
    lr = jnp.dot(hn, wlr_ref[...], preferred_element_type=F32)
    z = _bdot(lr, wgl_ref[...]) + bgl_ref[...]
    la_ref[...] = (jnp.minimum(z, 0.0) - jnp.log(1.0 + jnp.exp(-jnp.abs(z)))) * (1.0 / GLA_TAU)
    mq_ref[...] = jnp.dot(hn, wmq_ref[...], preferred_element_type=F32)


def _pad_heads(w, width, pad):
    lead = w.shape[:-1]
    w = w.reshape(*lead, GLA_HEADS, width)
    w = jnp.pad(w, [(0, 0)] * len(lead) + [(0, 0), (0, pad - width)])
    return w.reshape(*lead, GLA_HEADS * pad)


def _inproj_a(x_p, x_s, g, w_in, w_lr, b_lr):
    (n_p, d), n_s = x_p.shape, x_s.shape[0]
    n = n_p + n_s
    tm = _row_tile(n_s)
    assert n_p % tm == 0
    pt = n_p // tm
    c0, c1, c2, c3, c4 = (GLA_KEY_WIDTH, 2 * GLA_KEY_WIDTH, 2 * GLA_KEY_WIDTH + MAIN_WIDTH,
                          2 * GLA_KEY_WIDTH + 2 * MAIN_WIDTH,
                          2 * GLA_KEY_WIDTH + 2 * MAIN_WIDTH + GLA_GATE_RANK)
    wb = w_in.astype(BF16)
    wq = _pad_heads(wb[:, :c0], GLA_DK, GLA_DK_PAD)
    wk = _pad_heads(wb[:, c0:c1], GLA_DK, GLA_DK_PAD)
    wv = wb[:, c1:c2].reshape(d, GLA_HEADS, GLA_DV).transpose(1, 0, 2)
    wog = wb[:, c2:c3].reshape(d, GLA_HEADS, GLA_DV).transpose(1, 0, 2)
    wlr = jnp.pad(wb[:, c3:c4], ((0, 0), (0, LANES - GLA_GATE_RANK)))
    wmq = wb[:, c4:]
    wgl = jnp.pad(_pad_heads(w_lr.astype(BF16), GLA_DK, GLA_DK_PAD), ((0, LANES - GLA_GATE_RANK), (0, 0)))
    bgl = _pad_heads(b_lr.reshape(1, -1), GLA_DK, GLA_DK_PAD)
    row = lambda w: pl.BlockSpec((tm, w), lambda i: (i, 0))
    hrow = pl.BlockSpec((GLA_HEADS, tm, GLA_DV), lambda i: (0, i, 0))
    key = jax.ShapeDtypeStruct((n, GLA_KEY_PAD), F32)
    val = jax.ShapeDtypeStruct((GLA_HEADS, n, GLA_DV), F32)
    return pl.pallas_call(
        functools.partial(_inproj_a_kernel, prompt_tiles=pt),
        grid=(n // tm,),
        in_specs=_group_specs(tm, d, pt) + [
            _const((1, d)), _const(wq.shape), _const(wk.shape), _const(wv.shape),
            _const(wog.shape), _const(wlr.shape), _const(wmq.shape), _const(wgl.shape),
            _const(bgl.shape)],
        out_specs=[row(GLA_KEY_PAD), row(GLA_KEY_PAD), row(GLA_KEY_PAD), hrow, hrow, row(MEM_WIDTH)],
        out_shape=[key, key, key, jax.ShapeDtypeStruct(val.shape, BF16), val,
                   jax.ShapeDtypeStruct((n, MEM_WIDTH), F32)],
        compiler_params=_params(("parallel",)),
        name="inproj_a",
    )(x_p, x_s, g.reshape(1, d), wq, wk, wv, wog, wlr, wmq, wgl, bgl)


def _gla_kernel(q_ref, k_ref, la_ref, v_ref, og_ref, s0_ref, gn_ref, mcum_ref, mall_ref, sel_ref,
                o_ref, sout_ref, s_ref, *, chunk, n_sub, n_seg):
    j = pl.program_id(1)
    seg = chunk // n_seg

    @pl.when(j == 0)
    def _():
        s_ref[...] = jnp.zeros_like(s_ref)
        s_ref[:, :, :GLA_DK, :] = s0_ref[...]

    mcum = mcum_ref[...]
    causal = mcum.astype(F32) > 0.0
    row = lax.broadcasted_iota(jnp.int32, (chunk, GLA_DK_PAD), 0)
    gn = gn_ref[...]
    hcols = [slice(h * GLA_DK_PAD, (h + 1) * GLA_DK_PAD) for h in range(GLA_HEADS)]
    crows = [slice(c * chunk, (c + 1) * chunk) for c in range(n_sub)]
    qts, kts, kds, e_ends = [], [], [], []
    for rows in crows:
        la = la_ref[rows, :]
        b = _exact_left_dot(mcum, la)
        if n_seg == 1:
            b_end = b[chunk - 1:chunk, :]
            e_ends.append(jnp.broadcast_to(jnp.exp(b_end), (LANES, b.shape[1])).T)
        else:
            b_end = _exact_left_dot(mall_ref[...], la)
            e_ends.append(jnp.exp(_exact_left_dot(sel_ref[...], la)).T)
        k = k_ref[rows, :]
        qts.append(q_ref[rows, :] * jnp.exp(b))
        kts.append((k * jnp.exp(-b)).astype(BF16))
        kds.append(k * jnp.exp(b_end - b))
    vbs = [[v_ref[h, rows, :].astype(BF16) for h in range(GLA_HEADS)] for rows in crows]
    scores = [[_bdot_nt(qts[c][:, cols], kts[c][:, cols]) for cols in hcols] for c in range(n_sub)]
    kvs = []
    for c in range(n_sub):
        per_head = []
        for h, cols in enumerate(hcols):
            per_seg = []
            for s in range(n_seg):
                kd = kds[c][:, cols]
                if n_seg > 1:
                    kd = jnp.where((row >= s * seg) & (row < (s + 1) * seg), kd, 0.0)
                per_seg.append(_bdot_tn(kd, vbs[c][h]))
            per_head.append(per_seg)
        kvs.append(per_head)
    state = [[s_ref[s, h] for s in range(n_seg)] for h in range(GLA_HEADS)]
    inters = []
    for c in range(n_sub):
        per_head = []
        for h, cols in enumerate(hcols):
            parts = []
            for s in range(n_seg):
                parts.append(_bdot(qts[c][s * seg:(s + 1) * seg, cols], state[h][s]))
                state[h][s] = e_ends[c][cols, s:s + 1] * state[h][s] + kvs[c][h][s]
            per_head.append(parts[0] if n_seg == 1 else jnp.concatenate(parts, axis=0))
        inters.append(per_head)
    for h in range(GLA_HEADS):
        for s in range(n_seg):
            s_ref[s, h] = state[h][s]
    for c, rows in enumerate(crows):
        for h in range(GLA_HEADS):
            a = jnp.where(causal, scores[c][h], 0.0)
            o = _bdot(a, vbs[c][h]) + inters[c][h]
            on = o * lax.rsqrt(jnp.mean(o * o, axis=-1, keepdims=True) + EPS) * gn
            og = og_ref[h, rows, :]
            o_ref[h, rows, :] = (on * (og * jax.nn.sigmoid(og))).astype(BF16)

    @pl.when(j == pl.num_programs(1) - 1)
    def _():
        sout_ref[...] = s_ref[:, :, :GLA_DK, :]


def _gla(q, k, la, v, og, s0, gnorm, *, row_off, seq, n_seg, n_sub):
    batch = s0.shape[0]
    chunk = GLA_CHUNK
    assert chunk % n_seg == 0 and batch % n_seg == 0
    seg = chunk // n_seg
    step_rows = n_sub * chunk
    if n_seg > 1:
        assert seq == seg and n_sub == 1
        t_steps = 1
    else:
        assert seq % step_rows == 0
        t_steps = seq // step_rows
    assert row_off % step_rows == 0
    off = row_off // step_rows
    i = np.arange(chunk)
    same = (i[:, None] // seg) == (i[None, :] // seg)
    mcum = jnp.asarray(same & (i[None, :] <= i[:, None]), BF16)
    mall = jnp.asarray(same, BF16)
    sel = jnp.asarray((i[None, :] // seg) == np.arange(LANES)[:, None], BF16)
    ridx = lambda g, j: (off + g * t_steps + j, 0)
    hidx = lambda g, j: (0, off + g * t_steps + j, 0)
    key_spec = pl.BlockSpec((step_rows, GLA_KEY_PAD), ridx)
    val_spec = pl.BlockSpec((GLA_HEADS, step_rows, GLA_DV), hidx)
    st_spec = pl.BlockSpec((n_seg, GLA_HEADS, GLA_DK, GLA_DV), lambda g, j: (g, 0, 0, 0))
    in_specs = [key_spec, key_spec, key_spec, val_spec, val_spec, st_spec, _const((1, GLA_DV)),
                _const((chunk, chunk)), _const((chunk, chunk)), _const((LANES, chunk))]
    args = [q, k, la, v, og, s0, gnorm.reshape(1, GLA_DV), mcum, mall, sel]
    out_spec = pl.BlockSpec((GLA_HEADS, step_rows, GLA_DV), lambda g, j: (0, g * t_steps + j, 0))
    return pl.pallas_call(
        functools.partial(_gla_kernel, chunk=chunk, n_sub=n_sub, n_seg=n_seg),
        grid=(batch // n_seg, t_steps),
        in_specs=in_specs,
        out_specs=[out_spec, st_spec],
        out_shape=[jax.ShapeDtypeStruct((GLA_HEADS, batch * seq, GLA_DV), BF16),
                   jax.ShapeDtypeStruct(s0.shape, F32)],
        scratch_shapes=[pltpu.VMEM((n_seg, GLA_HEADS, GLA_DK_PAD, GLA_DV), F32)],
        compiler_params=_params(("arbitrary", "arbitrary")),
        name="gla",
    )(*args)


def _mem_attn_kernel(q_ref, k_ref, v_ref, g_ref, bd_ref, hm_ref, o_ref, *, tm, bb):
    g = g_ref[...]
    sub = min(tm, 128)
    units = [(i, i * tm + r) for i in range(bb) for r in range(0, tm, sub)]
    scores = []
    for i, r in units:
        q = q_ref[r:r + sub, :]
        qn = q * lax.rsqrt(_seg_mean(q * q, bd_ref[...]) + EPS) * g
        qs = jnp.concatenate([(qn * hm_ref[h]).astype(BF16) for h in range(MEM_HEADS)], axis=0)
        scores.append(_bdot(qs, k_ref[i]))
    probs = []
    for s in scores:
        e = jnp.exp(s - jnp.max(s, axis=-1, keepdims=True))
        probs.append(e * (1.0 / jnp.sum(e, axis=-1, keepdims=True)))
    outs = [_bdot_nt(p, v_ref[i]) for (i, _), p in zip(units, probs)]
    rows = []
    for o in outs:
        acc = o[:sub] * hm_ref[0]
        for h in range(1, MEM_HEADS):
            acc = acc + o[h * sub:(h + 1) * sub] * hm_ref[h]
        rows.append(acc)
    o_ref[...] = jnp.concatenate(rows, axis=0).astype(BF16)


def _mem_attn(mq, mk, mv, qng, *, row_off, seq, tm, bb, layer):
    depth, batch, m, _ = mk.shape
    mk = mk.reshape(depth * batch, m, MEM_WIDTH)
    mv = mv.reshape(depth * batch, m, MEM_WIDTH)
    kv_off = layer * batch // bb
    assert seq % tm == 0 and batch % bb == 0 and (bb == 1 or seq == tm)
    t_steps = seq // tm
    step_rows = bb * tm
    assert row_off % step_rows == 0
    off = row_off // step_rows
    row_spec = pl.BlockSpec((step_rows, MEM_WIDTH), lambda g, j: (off + g * t_steps + j, 0))
    kv_spec = pl.BlockSpec((bb, m, MEM_WIDTH), lambda g, j: (kv_off + g, 0, 0))
    in_specs = [row_spec, kv_spec, kv_spec, _const((1, MEM_WIDTH)), _const((MEM_WIDTH, MEM_WIDTH)),
                _const((MEM_HEADS, 1, MEM_WIDTH))]
    args = [mq, mk, mv, (jnp.tile(qng, MEM_HEADS) * HEAD_DIM ** -0.5).reshape(1, MEM_WIDTH),
            _block_diag_mean(MEM_WIDTH), _head_masks(MEM_HEADS)]
    return pl.pallas_call(
        functools.partial(_mem_attn_kernel, tm=tm, bb=bb),
        grid=(batch // bb, t_steps),
        in_specs=in_specs,
        out_specs=pl.BlockSpec((step_rows, MEM_WIDTH), lambda g, j: (g * t_steps + j, 0)),
        out_shape=jax.ShapeDtypeStruct((batch * seq, MEM_WIDTH), BF16),
        compiler_params=_params(("parallel", "parallel")),
        name="mem_attn",
    )(*args)


def _outproj_kernel(*refs, heads, prompt_tiles, split_residual):
    n_h = 2 if split_residual else 1
    h_refs, refs = refs[:n_h], refs[n_h:]
    (main_p_ref, main_s_ref, mo_p_ref, mo_s_ref, wmain_ref, wmo_ref), refs = refs[:6], refs[6:]
    route_in, (o_ref, *route_out) = refs[:6], refs[6:]
    is_prompt = pl.program_id(0) < prompt_tiles
    pick = lambda p, s: jnp.where(is_prompt, p, s)
    tm = o_ref.shape[0]
    sub = min(tm, 128)
    blocks = []
    for r in range(0, tm, sub):
        rows = slice(r, r + sub)
        acc = pick(h_refs[0][rows, :], h_refs[1][rows, :]) if split_residual else h_refs[0][rows, :]
        acc = acc + _bdot(pick(mo_p_ref[rows, :], mo_s_ref[rows, :]), wmo_ref[...])
        if heads:
            for h in range(heads):
                acc = acc + _bdot(pick(main_p_ref[h, rows, :], main_s_ref[h, rows, :]), wmain_ref[h])
        else:
            acc = acc + _bdot(pick(main_p_ref[rows, :], main_s_ref[rows, :]), wmain_ref[...])
        o_ref[rows, :] = acc
        blocks.append(acc)
    _route_tile(blocks, *route_in, *route_out)


def _outproj(h, main_p, mo_p, main_s, mo_s, w_main, w_mo, router, tm):
    n_p, n_s = mo_p.shape[0], mo_s.shape[0]
    n, d = n_p + n_s, w_mo.shape[1]
    assert n_p % tm == 0 and n_s % tm == 0
    pt = n_p // tm
    heads = main_p.shape[0] if main_p.ndim == 3 else 0
    row = pl.BlockSpec((tm, d), lambda i: (i, 0))
    split = isinstance(h, tuple)
    h_specs, h_args = (_group_specs(tm, d, pt), list(h)) if split else ([row], [h])
    main_specs = _group_specs(tm, main_p.shape[-1], pt, lead=heads or None)
    r_args, r_in_specs, r_out_specs, r_out_shape, r_scratch = _route_operands(*router, n, tm)
    h_new, *routed = pl.pallas_call(
        functools.partial(_outproj_kernel, heads=heads, prompt_tiles=pt, split_residual=split),
        grid=(n // tm,),
        in_specs=(h_specs + main_specs + _group_specs(tm, MEM_WIDTH, pt)
                  + [_const(w_main.shape), _const(w_mo.shape)] + r_in_specs),
        out_specs=[row] + r_out_specs,
        out_shape=[jax.ShapeDtypeStruct((n, d), F32)] + r_out_shape,
        scratch_shapes=[r_scratch],
        compiler_params=_params(("arbitrary",)),
        name="outproj_route",
    )(*h_args, main_p, main_s, mo_p, mo_s, w_main.astype(BF16), w_mo.astype(BF16), *r_args)
    return h_new, routed


def _route_tile(x_blocks, g_ref, whi_ref, wlo_ref, b_ref, before_tok_ref, before_row_ref,
                mi_ref, mf_ref, cnt_ref, tt_ref, carry_ref):
    @pl.when(pl.program_id(0) == 0)
    def _():
        carry_ref[...] = jnp.zeros_like(carry_ref)

    nt = lambda a, b: lax.dot_general(a, b, (((1,), (1,)), ((), ())), preferred_element_type=F32)
    logits = []
    for x in x_blocks:
        x_hi, x_lo = _split(x * _rms_scale(x) * g_ref[...], 2)
        logits.append((nt(whi_ref[...], x_hi) + nt(wlo_ref[...], x_hi) + nt(whi_ref[...], x_lo))[:ROUTER_ROWS])
    logits = jnp.concatenate(logits, axis=1) + b_ref[...]
    tm = logits.shape[1]
    row = lax.broadcasted_iota(jnp.int32, (ROUTER_ROWS, tm), 0)
    far = jnp.int32(2 * LANES)

    def first_max(vals):
        m = jnp.max(vals, axis=0, keepdims=True)
        return m, jnp.min(jnp.where(vals == m, row, far), axis=0, keepdims=True)

    gl = jnp.where(row < N_GROUPS, logits, -jnp.inf)
    gmax, grp = first_max(gl)
    pg_sel = 1.0 / jnp.sum(jnp.exp(gl - gmax), axis=0, keepdims=True)
    lo = ROUTER_LANE0 + grp * EXPERTS_PER_GROUP
    el = jnp.where((row >= lo) & (row < lo + EXPERTS_PER_GROUP), logits, -jnp.inf)
    m1, i1 = first_max(el)
    m2, i2 = first_max(jnp.where(row == i1, -jnp.inf, el))
    e2 = jnp.exp(m2 - m1)
    g1 = pg_sel / (1.0 + e2)
    g2 = pg_sel * e2 / (1.0 + e2)

    oh1 = row == i1
    oh2 = row == i2
    picked = jnp.where(oh1 | oh2, 1.0, 0.0)
    earlier = jnp.dot(picked.astype(BF16), before_tok_ref[...], preferred_element_type=F32)
    cnt_col = jnp.sum(picked, axis=1, keepdims=True)
    cnt_tile = jnp.concatenate([jnp.broadcast_to(cnt_col, (ROUTER_ROWS, LANES)),
                                jnp.zeros((LANES - ROUTER_ROWS, LANES), F32)], axis=0)
    c_hi = jnp.floor(cnt_tile * (1.0 / 32.0))
    c_lo = cnt_tile - 32.0 * c_hi
    first = (32.0 * jnp.dot(before_row_ref[...], c_hi.astype(BF16), preferred_element_type=F32)
             + jnp.dot(before_row_ref[...], c_lo.astype(BF16), preferred_element_type=F32))
    local = first[:ROUTER_ROWS, 0:1] + earlier
    lpos1 = jnp.sum(jnp.where(oh1, local, 0.0), axis=0, keepdims=True)
    lpos2 = jnp.sum(jnp.where(oh2, local, 0.0), axis=0, keepdims=True)
    carry_before = carry_ref[...]
    carry = carry_before + cnt_tile
    carry_ref[...] = carry

    lane = lax.broadcasted_iota(jnp.int32, (LANES, LANES), 1)
    cols = jnp.where(lane == 0, carry_before, jnp.where(lane == 1, cnt_tile, jnp.where(lane == 2, first,
                     jnp.where(lane == 3, carry, 0.0))))
    tables = cols.T
    tt_ref[...] = tables[:8].astype(jnp.int32)
    cnt_ref[...] = tables[3:4]
    row8 = lax.broadcasted_iota(jnp.int32, (ROUTER_META_ROWS, tm), 0)
    zero8 = jnp.zeros((ROUTER_META_ROWS, tm), F32)
    mi_ref[...] = jnp.where(row8 == 0, lpos1, jnp.where(row8 == 1, lpos2, zero8)).astype(jnp.int32)
    stacked = jnp.where(row8 == 0, g1, jnp.where(row8 == 1, g2, jnp.where(row8 == 2, lpos1,
                        jnp.where(row8 == 3, lpos2, zero8))))
    mf_ref[...] = jnp.concatenate([stacked, jnp.zeros((LANES - ROUTER_META_ROWS, tm), F32)], axis=0).T


def _route_operands(g, w_rg, b_rg, w_re, b_re, n, tm):
    d = g.shape[0]
    assert n % tm == 0 and 2 * tm <= 32 * 32
    n_real = N_GROUPS + N_EXPERTS
    w = jnp.pad(jnp.concatenate([w_rg, w_re], axis=1), ((0, 0), (0, LANES - n_real))).T
    b = jnp.pad(jnp.concatenate([b_rg, b_re]), (0, ROUTER_ROWS - n_real)).reshape(ROUTER_ROWS, 1)
    w_hi = w.astype(BF16)
    w_lo = (w - w_hi.astype(F32)).astype(BF16)
    i = np.arange(tm)
    before_tok = jnp.asarray(i[:, None] < i[None, :], BF16)
    e = np.arange(LANES)
    before_row = jnp.asarray(e[None, :] < e[:, None], BF16)
    args = [g.reshape(1, d), w_hi, w_lo, b, before_tok, before_row]
    in_specs = [_const(a.shape) for a in args]
    out_specs = [pl.BlockSpec((ROUTER_META_ROWS, tm), lambda i: (0, i)), pl.BlockSpec((tm, LANES), lambda i: (i, 0)),
                 _const((1, LANES)), pl.BlockSpec((8, LANES), lambda i: (i, 0))]
    out_shape = [jax.ShapeDtypeStruct((ROUTER_META_ROWS, n), jnp.int32), jax.ShapeDtypeStruct((n, LANES), F32),
                 jax.ShapeDtypeStruct((1, LANES), F32), jax.ShapeDtypeStruct((n // tm * 8, LANES), jnp.int32)]
    return args, in_specs, out_specs, out_shape, pltpu.VMEM((LANES, LANES), F32)


def _pack_rows(ref, x, rows, lead=(), row0=0):
    u32 = jnp.uint32
    for w in range(PACKED_SLABS):
        lo = x[:, (2 * w) * LANES:(2 * w + 1) * LANES].astype(BF16).astype(F32)
        hi = x[:, (2 * w + 1) * LANES:(2 * w + 2) * LANES].astype(BF16).astype(F32)
        word = (lax.bitcast_convert_type(lo, u32) >> 16) | (lax.bitcast_convert_type(hi, u32) & u32(0xFFFF0000))
        ref[lead + (pl.ds(row0 * PACKED_SLABS + w, rows, stride=PACKED_SLABS), slice(None))] = word


def _unpack_rows(ref, rows, lead=(), row0=0):
    u32 = jnp.uint32
    slabs = []
    for w in range(PACKED_SLABS):
        word = ref[lead + (pl.ds(row0 * PACKED_SLABS + w, rows, stride=PACKED_SLABS), slice(None))]
        slabs.append(lax.bitcast_convert_type(word << 16, F32).astype(BF16))
        slabs.append(lax.bitcast_convert_type(word & u32(0xFFFF0000), F32).astype(BF16))
    return jnp.concatenate(slabs, axis=1)


RUN_FIELDS = 3
RUN_CHUNK_BITS = 6


def _copy_runs(runs_ref, tile, local_rows, global_rows, sem, *, to_global):
    ps = PACKED_SLABS
    base = tile * (RUN_FIELDS * N_EXPERTS)

    def piece(g0, l0, off, size):
        g = global_rows(pl.multiple_of((g0 + off) * ps, ps), size * ps)
        l = local_rows(pl.multiple_of((l0 + off) * ps, ps), size * ps)
        src, dst = (l, g) if to_global else (g, l)
        pltpu.make_async_copy(src, dst, sem).start()

    def per_expert(e, carry):
        g0 = runs_ref[base + e]
        length = runs_ref[base + N_EXPERTS + e]
        l0 = runs_ref[base + 2 * N_EXPERTS + e]
        big = 1 << RUN_CHUNK_BITS

        def big_piece(c, inner):
            piece(g0, l0, c * big, big)
            return inner

        n_big = length >> RUN_CHUNK_BITS
        lax.fori_loop(0, n_big, big_piece, 0)
        off = n_big * big
        for bit in reversed(range(RUN_CHUNK_BITS)):
            size = 1 << bit

            @pl.when((length & size) != 0)
            def _(off=off, size=size):
                piece(g0, l0, off, size)

            off = off + (length & size)
        return carry

    lax.fori_loop(0, N_EXPERTS, per_expert, 0)


def _fill_pads(pads_ref, buf0, xs_hbm, sem, *, wait):
    ps = PACKED_SLABS

    def go(copy):
        copy.wait() if wait else copy.start()

    def per_tail_tile(t, carry):
        rows = MOE_TILE * ps
        go(pltpu.make_async_copy(
            buf0.at[pl.ds(0, rows)],
            xs_hbm.at[pl.ds(pl.multiple_of((pads_ref[2 * N_EXPERTS] + t) * rows, rows), rows)], sem))
        return carry

    lax.fori_loop(0, pads_ref[2 * N_EXPERTS + 1], per_tail_tile, 0)

    def per_expert(e, carry):
        first, length = pads_ref[e], pads_ref[N_EXPERTS + e]
        off = 0
        for bit in reversed(range((MOE_TILE - 1).bit_length())):
            size = 1 << bit

            @pl.when((length & size) != 0)
            def _(off=off, size=size):
                go(pltpu.make_async_copy(
                    buf0.at[pl.ds(pl.multiple_of(off * ps, ps), size * ps)],
                    xs_hbm.at[pl.ds(pl.multiple_of((first + off) * ps, ps), size * ps)], sem))

            off = off + (length & size)
        return carry

    lax.fori_loop(0, N_EXPERTS, per_expert, 0)


def _dispatch_kernel(runs_ref, pads_ref, h_ref, g_ref, meta_ref, xs_hbm, buf, sem, pad_sem, *, tm, steps):
    i = pl.program_id(0)
    slot = lax.rem(i, 2)
    ns = 2 * tm
    assert ns >= MOE_TILE

    def wait_slot(sl):
        pltpu.make_async_copy(buf.at[sl], xs_hbm.at[pl.ds(0, ns * PACKED_SLABS)], sem.at[sl]).wait()

    @pl.when(i >= 2)
    def _():
        wait_slot(slot)

    x = h_ref[...]
    xn = (x * _rms_scale(x) * g_ref[...]).astype(BF16)
    j = lax.broadcasted_iota(jnp.int32, (ns, tm), 0)
    pick = jnp.where((j == meta_ref[0:1, :]) | (j == meta_ref[1:2, :]), 1.0, 0.0).astype(BF16)
    _pack_rows(buf, jnp.dot(pick, xn, preferred_element_type=F32), ns, (slot,))
    _copy_runs(runs_ref, i, lambda start, size: buf.at[slot, pl.ds(start, size)],
               lambda start, size: xs_hbm.at[pl.ds(start, size)], sem.at[slot], to_global=True)

    @pl.when(i == 0)
    def _():
        _fill_pads(pads_ref, buf.at[0], xs_hbm, pad_sem, wait=False)

    @pl.when(i == min(1, steps - 1))
    def _():
        _fill_pads(pads_ref, buf.at[0], xs_hbm, pad_sem, wait=True)

    @pl.when(i == steps - 1)
    def _():
        wait_slot(slot)
        if steps > 1:
            wait_slot(1 - slot)


def _dispatch(h, g, meta, runs, pads, tm, n_slots):
    n, d = h.shape
    steps = n // tm
    return pl.pallas_call(
        functools.partial(_dispatch_kernel, tm=tm, steps=steps),
        grid_spec=pltpu.PrefetchScalarGridSpec(
            num_scalar_prefetch=2,
            grid=(steps,),
            in_specs=[pl.BlockSpec((tm, d), lambda i, *_: (i, 0)),
                      pl.BlockSpec((1, d), lambda i, *_: (0, 0)),
                      pl.BlockSpec((ROUTER_META_ROWS, tm), lambda i, *_: (0, i))],
            out_specs=pl.BlockSpec(memory_space=pl.ANY),
            scratch_shapes=[pltpu.VMEM((2, 2 * tm * PACKED_SLABS, LANES), jnp.uint32),
                            pltpu.SemaphoreType.DMA((2,)), pltpu.SemaphoreType.DMA],
        ),
        out_shape=jax.ShapeDtypeStruct((n_slots * PACKED_SLABS, LANES), jnp.uint32),
        compiler_params=_params(("arbitrary",)),
        name="moe_dispatch",
    )(runs, pads, h, g.reshape(1, d), meta)


def _expert_kernel(tile_ref, exp_ref, new_ref, wslot_ref, next_ref, n_ref,
                   xs_ref, wg_hbm, wu_hbm, wd_hbm, ys_ref,
                   wgf, wuf, wdf, wsem, wgb, wub, wdb, *, tm, sub, layer):
    w = pl.program_id(0)

    def weight_copies(e, slot):
        return [pltpu.make_async_copy(hbm.at[layer, e], buf.at[slot], wsem.at[slot])
                for hbm, buf in ((wg_hbm, wgf), (wu_hbm, wuf), (wd_hbm, wdf))]

    @pl.when(w < n_ref[0])
    def _():
        @pl.when(new_ref[w] != 0)
        def _():
            slot = wslot_ref[w]

            @pl.when(w == 0)
            def _():
                for c in weight_copies(exp_ref[w], slot):
                    c.start()

            for c in weight_copies(exp_ref[w], slot):
                c.wait()
            wgb[...] = wgf[slot].astype(BF16)
            wub[...] = wuf[slot].astype(BF16)
            wdb[...] = wdf[slot].astype(BF16)

            @pl.when(next_ref[w] >= 0)
            def _():
                for c in weight_copies(next_ref[w], 1 - slot):
                    c.start()

        n_blocks = tm // sub

        def up(s):
            x = _unpack_rows(xs_ref, sub, row0=s * sub)
            return (jnp.dot(x, wgb[...], preferred_element_type=F32),
                    jnp.dot(x, wub[...], preferred_element_type=F32))

        ups = {0: up(0)}
        for s in range(n_blocks):
            if s + 1 < n_blocks:
                ups[s + 1] = up(s + 1)
            hg, hu = ups.pop(s)
            act = (hg * jax.nn.sigmoid(hg) * hu).astype(BF16)
            y = jnp.dot(act, wdb[...], preferred_element_type=F32)
            _pack_rows(ys_ref, y, sub, row0=s * sub)

    @pl.when(w >= n_ref[0])
    def _():
        ys_ref[...] = jnp.zeros_like(ys_ref)


def _experts(xs, items, w_g, w_u, w_d, layer):
    d = D_MODEL
    tm = MOE_TILE
    packed_spec = pl.BlockSpec((tm * PACKED_SLABS, LANES), lambda w, tile, *_: (tile[w], 0))
    any_spec = pl.BlockSpec(memory_space=pl.ANY)
    return pl.pallas_call(
        functools.partial(_expert_kernel, tm=tm, sub=MOE_SUB, layer=layer),
        grid_spec=pltpu.PrefetchScalarGridSpec(
            num_scalar_prefetch=len(items),
            grid=(items[0].shape[0],),
            in_specs=[packed_spec, any_spec, any_spec, any_spec],
            out_specs=pl.BlockSpec((tm * PACKED_SLABS, LANES), lambda w, *_: (w, 0)),
            scratch_shapes=[pltpu.VMEM((2, d, D_EXPERT), F32), pltpu.VMEM((2, d, D_EXPERT), F32),
                            pltpu.VMEM((2, D_EXPERT, d), F32), pltpu.SemaphoreType.DMA((2,)),
                            pltpu.VMEM((d, D_EXPERT), BF16), pltpu.VMEM((d, D_EXPERT), BF16),
                            pltpu.VMEM((D_EXPERT, d), BF16)],
        ),
        out_shape=jax.ShapeDtypeStruct(xs.shape, jnp.uint32),
        compiler_params=_params(("arbitrary",)),
        name="moe_experts",
    )(*items, xs, w_g, w_u, w_d)


def _combine_kernel(runs_ref, h_ref, gate_ref, ys_hbm, *refs, tm, steps, prompt_tiles):
    out_refs, (buf, sem) = refs[:-2], refs[-2:]
    i = pl.program_id(0)
    slot = lax.rem(i, 2)
    ns = 2 * tm

    def issue(step, sl):
        _copy_runs(runs_ref, step, lambda start, size: buf.at[sl, pl.ds(start, size)],
                   lambda start, size: ys_hbm.at[pl.ds(start, size)], sem.at[sl], to_global=False)

    @pl.when(i == 0)
    def _():
        issue(0, 0)

    @pl.when(i + 1 < steps)
    def _():
        issue(i + 1, 1 - slot)

    pltpu.make_async_copy(ys_hbm.at[pl.ds(0, ns * PACKED_SLABS)], buf.at[slot], sem.at[slot]).wait()
    y = _unpack_rows(buf, ns, (slot,))
    g = gate_ref[...]
    j = lax.broadcasted_iota(jnp.int32, (tm, ns), 1)
    mix = (jnp.where(j == g[:, 2:3].astype(jnp.int32), g[:, 0:1], 0.0)
           + jnp.where(j == g[:, 3:4].astype(jnp.int32), g[:, 1:2], 0.0)).astype(BF16)
    out = h_ref[...] + jnp.dot(mix, y, preferred_element_type=F32)
    if len(out_refs) == 1:
        out_refs[0][...] = out
    else:
        @pl.when(i < prompt_tiles)
        def _():
            out_refs[0][...] = out

        @pl.when(i >= prompt_tiles)
        def _():
            out_refs[1][...] = out


def _combine(h, gates, ys, runs, tm, split_rows=None):
    n, d = h.shape
    steps = n // tm
    row = pl.BlockSpec((tm, d), lambda i, pos: (i, 0))
    if split_rows is None:
        pt, out_specs, out_shape = 0, row, jax.ShapeDtypeStruct((n, d), F32)
    else:
        assert split_rows % tm == 0
        pt = split_rows // tm
        out_specs = _group_specs(tm, d, pt)
        out_shape = [jax.ShapeDtypeStruct((split_rows, d), F32), jax.ShapeDtypeStruct((n - split_rows, d), F32)]
    return pl.pallas_call(
        functools.partial(_combine_kernel, tm=tm, steps=steps, prompt_tiles=pt),
        grid_spec=pltpu.PrefetchScalarGridSpec(
            num_scalar_prefetch=1,
            grid=(steps,),
            in_specs=[row, pl.BlockSpec((tm, LANES), lambda i, pos: (i, 0)),
                      pl.BlockSpec(memory_space=pl.ANY)],
            out_specs=out_specs,
            scratch_shapes=[pltpu.VMEM((2, 2 * tm * PACKED_SLABS, LANES), jnp.uint32),
                            pltpu.SemaphoreType.DMA((2,))],
        ),
        out_shape=out_shape,
        compiler_params=_params(("arbitrary",)),
        name="moe_combine",
    )(runs, h, gates, ys)


def _lookup(tables, idx):
    hit = idx[:, None] == jnp.arange(tables.shape[1], dtype=idx.dtype)[None, :]
    return jnp.sum(jnp.where(hit[None], tables[:, None, :], 0), axis=2)


def _work_items(tiles_e, max_items):
    item_end = jnp.cumsum(tiles_e)
    n_items = item_end[-1]
    w = jnp.minimum(jnp.arange(max_items, dtype=jnp.int32), n_items - 1)
    expert = jnp.sum(w[:, None] >= item_end[None, :], axis=1).astype(jnp.int32)
    prev_expert = jnp.concatenate([jnp.full((1,), -1, jnp.int32), expert[:-1]])
    new_expert = expert != prev_expert
    weight_slot = (jnp.cumsum(new_expert.astype(jnp.int32)) - 1) % 2
    ids = jnp.arange(N_EXPERTS, dtype=jnp.int32)
    later = (ids[None, :] > ids[:, None]) & (tiles_e[None, :] > 0)
    following = jnp.min(jnp.where(later, ids[None, :], N_EXPERTS), axis=1)
    following = jnp.where(following == N_EXPERTS, -1, following)
    next_expert, = _lookup(following[None, :], expert)
    as_i32 = lambda a: a.astype(jnp.int32)
    return (as_i32(w), expert, as_i32(new_expert), as_i32(weight_slot), as_i32(next_expert),
            as_i32(n_items).reshape(1))


def _moe(h, routed, g, w_g, w_u, w_d, layer, tm, split_rows=None):
    n, _ = h.shape
    meta, gates, cnt, tables = routed
    experts = slice(ROUTER_LANE0, ROUTER_LANE0 + N_EXPERTS)
    counts = cnt[0, experts].astype(jnp.int32)
    tiles_e = (counts + MOE_TILE - 1) // MOE_TILE
    starts = (jnp.cumsum(tiles_e) - tiles_e) * MOE_TILE
    max_tiles = -(-2 * n // MOE_TILE) + N_EXPERTS
    tables = tables.reshape(n // tm, 8, LANES)[:, :RUN_FIELDS, experts]
    runs = tables.at[:, 0, :].add(starts[None, :]).reshape(-1)
    used = jnp.sum(tiles_e)
    pads = jnp.concatenate([starts + counts, tiles_e * MOE_TILE - counts, jnp.stack([used, max_tiles - used])])
    xs = _dispatch(h, g, meta, runs, pads, tm, max_tiles * MOE_TILE)
    ys = _experts(xs, _work_items(tiles_e, max_tiles), w_g, w_u, w_d, layer)
    return _combine(h, gates, ys, runs, tm, split_rows)


def _inproj_b_kernel(x_ref, gkv_ref, gmix_ref, wkv_ref, win_ref, kng_ref, qng_ref, bdk_ref, hsum_ref, hexp_ref,
                     q_ref, mq_ref, k_ref, v_ref, kt_ref):
    x = x_ref[...]
    xr = x * _rms_scale(x)
    kv = _bdot(xr * gkv_ref[...], wkv_ref[...])
    k = kv[:, :KV_WIDTH]
    k = k * lax.rsqrt(_seg_mean(k * k, bdk_ref[...]) + EPS) * kng_ref[...]
    k_ref[...] = k
    kt_ref[...] = k.T.astype(BF16)
    v_ref[...] = kv[:, KV_WIDTH:]
    proj = _bdot(xr * gmix_ref[...], win_ref[...])
    q = proj[:, :MAIN_WIDTH]
    ms = None
    for p in _split(q * q, 2):
        t = jnp.dot(p, hsum_ref[...], preferred_element_type=F32)
        ms = t if ms is None else ms + t
    scale = None
    for p in _split(lax.rsqrt(ms + EPS), 2):
        t = jnp.dot(p, hexp_ref[...], preferred_element_type=F32)
        scale = t if scale is None else scale + t
    q_ref[...] = (q * scale * qng_ref[...]).astype(BF16)
    mq_ref[...] = proj[:, MAIN_WIDTH:]


def _swa_perm():
    g, kh, dd = np.meshgrid(np.arange(SWA_GROUP), np.arange(SWA_KV_HEADS), np.arange(HEAD_DIM), indexing="ij")
    return ((kh * SWA_GROUP + g) * HEAD_DIM + dd).reshape(-1)


def _inproj_b(x, g_kv, g_mix, w_kv, w_in, kng, qng):
    n, d = x.shape
    tm = _row_tile(n)
    perm = _swa_perm()
    w_in_p = jnp.concatenate([w_in[:, :MAIN_WIDTH][:, perm], w_in[:, MAIN_WIDTH:]], axis=1).astype(BF16)
    qng_t = (jnp.tile(qng, SWA_HEADS) * HEAD_DIM ** -0.5).reshape(1, MAIN_WIDTH)
    member = (np.arange(MAIN_WIDTH)[:, None] // HEAD_DIM == np.arange(LANES)[None, :]).astype(np.float32)
    row = lambda w: pl.BlockSpec((tm, w), lambda i: (i, 0))
    return pl.pallas_call(
        _inproj_b_kernel,
        grid=(n // tm,),
        in_specs=[row(d), _const((1, d)), _const((1, d)), _const((d, 2 * KV_WIDTH)), _const((d, d)),
                  _const((1, KV_WIDTH)), _const((1, MAIN_WIDTH)), _const((KV_WIDTH, KV_WIDTH)),
                  _const((MAIN_WIDTH, LANES)), _const((LANES, MAIN_WIDTH))],
        out_specs=[row(MAIN_WIDTH), row(MEM_WIDTH), row(KV_WIDTH), row(KV_WIDTH),
                   pl.BlockSpec((KV_WIDTH, tm), lambda i: (0, i))],
        out_shape=[jax.ShapeDtypeStruct((n, MAIN_WIDTH), BF16), jax.ShapeDtypeStruct((n, MEM_WIDTH), F32),
                   jax.ShapeDtypeStruct((n, KV_WIDTH), F32), jax.ShapeDtypeStruct((n, KV_WIDTH), F32),
                   jax.ShapeDtypeStruct((KV_WIDTH, n), BF16)],
        compiler_params=_params(("parallel",)),
        name="inproj_b",
    )(x, g_kv.reshape(1, d), g_mix.reshape(1, d), w_kv.astype(BF16), w_in_p,
      jnp.tile(kng, SWA_KV_HEADS).reshape(1, KV_WIDTH), qng_t,
      _block_diag_mean(KV_WIDTH), jnp.asarray(member / HEAD_DIM, BF16), jnp.asarray(member.T, BF16))


def _softmax_with_sink(s, sink):
    m = jnp.maximum(jnp.max(s, axis=-1, keepdims=True), sink)
    e = jnp.exp(s - m)
    r = 1.0 / (jnp.sum(e, axis=-1, keepdims=True) + jnp.exp(sink - m))
    return (e * r).astype(BF16)


def _swa_bias(tq):
    slopes = 2.0 ** (-8.0 * np.arange(1, SWA_HEADS + 1, dtype=np.float64) / SWA_HEADS)
    dist = np.arange(tq)[:, None] + WINDOW - np.arange(WINDOW + tq)[None, :]
    valid = (dist >= 0) & (dist <= WINDOW)
    return np.stack([np.where(valid, -s * dist, NEG_BIG) for s in slopes]).astype(np.float32)


def _swa_prompt_kernel(sink_ref, q_ref, ktp_ref, kto_ref, vp_ref, vo_ref, bias_ref, hm_ref, o_ref, *, nb):
    w = WINDOW
    key = lax.broadcasted_iota(jnp.int32, (w, 2 * w), 1)
    has_prev = (pl.program_id(0) > 0) | (key >= w)
    heads = [(g, kh) for g in range(SWA_GROUP) for kh in range(SWA_KV_HEADS)]
    kts, vvs = [], []
    for b in range(nb):
        kt_prev = ktp_ref[...] if b == 0 else kto_ref[:, (b - 1) * w:b * w]
        v_prev = vp_ref[...] if b == 0 else vo_ref[(b - 1) * w:b * w, :]
        kts.append(jnp.concatenate([kt_prev, kto_ref[:, b * w:(b + 1) * w]], axis=1).astype(BF16))
        vvs.append(jnp.concatenate([v_prev, vo_ref[b * w:(b + 1) * w, :]], axis=0).astype(BF16))
    scores = [[jnp.dot(q_ref[b * w:(b + 1) * w, g * KV_WIDTH:(g + 1) * KV_WIDTH] * hm_ref[kh].astype(BF16),
                       kts[b], preferred_element_type=F32) for g, kh in heads] for b in range(nb)]
    for b in range(nb):
        probs = []
        for (g, kh), s in zip(heads, scores[b]):
            h = kh * SWA_GROUP + g
            s = s + bias_ref[h]
            if b == 0:
                s = jnp.where(has_prev, s, NEG_BIG)
            probs.append(_softmax_with_sink(s, sink_ref[h]))
        outs = [jnp.dot(p, vvs[b], preferred_element_type=F32) for p in probs]
        for g in range(SWA_GROUP):
            acc = None
            for (cg, kh), o in zip(heads, outs):
                if cg == g:
                    t = o * hm_ref[kh]
                    acc = t if acc is None else acc + t
            o_ref[b * w:(b + 1) * w, g * KV_WIDTH:(g + 1) * KV_WIDTH] = acc.astype(BF16)


def _swa_prompt(q, kt, v, sinks, *, n_rows, nb):
    w = WINDOW
    step = nb * w
    assert n_rows % step == 0
    prev = lambda j, sink: jnp.maximum(j * nb - 1, 0)
    return pl.pallas_call(
        functools.partial(_swa_prompt_kernel, nb=nb),
        grid_spec=pltpu.PrefetchScalarGridSpec(
            num_scalar_prefetch=1,
            grid=(n_rows // step,),
            in_specs=[pl.BlockSpec((step, MAIN_WIDTH), lambda j, sink: (j, 0)),
                      pl.BlockSpec((KV_WIDTH, w), lambda j, sink: (0, prev(j, sink))),
                      pl.BlockSpec((KV_WIDTH, step), lambda j, sink: (0, j)),
                      pl.BlockSpec((w, KV_WIDTH), lambda j, sink: (prev(j, sink), 0)),
                      pl.BlockSpec((step, KV_WIDTH), lambda j, sink: (j, 0)),
                      pl.BlockSpec((SWA_HEADS, w, 2 * w), lambda j, sink: (0, 0, 0)),
                      pl.BlockSpec((SWA_KV_HEADS, 1, KV_WIDTH), lambda j, sink: (0, 0, 0))],
            out_specs=pl.BlockSpec((step, MAIN_WIDTH), lambda j, sink: (j, 0)),
        ),
        out_shape=jax.ShapeDtypeStruct((n_rows, MAIN_WIDTH), BF16),
        compiler_params=_params(("arbitrary",)),
        name="swa_prompt",
    )(sinks.astype(F32), q, kt, kt, v, v, jnp.asarray(_swa_bias(w)), _head_masks(SWA_KV_HEADS))


def _swa_sample_kernel(q_ref, kp_ref, ko_ref, vp_ref, vo_ref, bias_ref, sink_ref, hm_ref, o_ref, *, nb, tq):
    w = WINDOW
    heads = [(kh, g) for kh in range(SWA_KV_HEADS) for g in range(SWA_GROUP)]
    kks, vvs, scores = [], [], []
    q = q_ref[...].astype(F32)
    for i in range(nb):
        win = slice(i * KV_WIDTH, (i + 1) * KV_WIDTH)
        kks.append(jnp.concatenate([kp_ref[win, :].T, ko_ref[i * tq:(i + 1) * tq, :]], axis=0))
        vvs.append(jnp.concatenate([vp_ref[win, :].T, vo_ref[i * tq:(i + 1) * tq, :]], axis=0))
        qs = jnp.concatenate([q[i * tq:(i + 1) * tq, g * KV_WIDTH:(g + 1) * KV_WIDTH] * hm_ref[kh]
                              for kh, g in heads], axis=0)
        scores.append(_bdot_nt(qs, kks[i]))
    probs = [_softmax_with_sink(s + bias_ref[...], sink_ref[...]) for s in scores]
    outs = [_bdot(p, vv) for p, vv in zip(probs, vvs)]
    for g in range(SWA_GROUP):
        rows = []
        for i in range(nb):
            acc = None
            for r, (kh, hg) in enumerate(heads):
                if hg == g:
                    t = outs[i][r * tq:(r + 1) * tq] * hm_ref[kh]
                    acc = t if acc is None else acc + t
            rows.append(acc)
        o_ref[:, g * KV_WIDTH:(g + 1) * KV_WIDTH] = jnp.concatenate(rows, axis=0).astype(BF16)


def _swa_sample(q, k_win, v_win, k, v, sinks, *, row_off, batch, tq, nb):
    w = WINDOW
    assert batch % nb == 0 and row_off % (nb * tq) == 0
    off = row_off // (nb * tq)
    bias = jnp.asarray(_swa_bias(tq).reshape(SWA_HEADS * tq, w + tq))
    sink_col = jnp.repeat(sinks.astype(F32), tq).reshape(SWA_HEADS * tq, 1)
    own = lambda width: pl.BlockSpec((nb * tq, width), lambda b: (off + b, 0))
    win = pl.BlockSpec((nb * KV_WIDTH, w), lambda b: (b, 0))
    return pl.pallas_call(
        functools.partial(_swa_sample_kernel, nb=nb, tq=tq),
        grid=(batch // nb,),
        in_specs=[own(MAIN_WIDTH), win, own(KV_WIDTH), win, own(KV_WIDTH), _const(bias.shape),
                  _const(sink_col.shape), _const((SWA_KV_HEADS, 1, KV_WIDTH))],
        out_specs=pl.BlockSpec((nb * tq, MAIN_WIDTH), lambda b: (b, 0)),
        out_shape=jax.ShapeDtypeStruct((batch * tq, MAIN_WIDTH), BF16),
        compiler_params=_params(("arbitrary",)),
        name="swa_sample",
    )(q, k_win, k, v_win, v, bias, sink_col, _head_masks(SWA_KV_HEADS))


def kernel(x_prompt, x_sample, state_gla, cache_win_k, cache_win_v, cache_mem_k, cache_mem_v, mem_prompt, norm_mix_g, norm_ffn_g, norm_mem_g, w_mem_kv, mem_qn_g, mem_kn_g, w_out, w_in_a, w_gate_lr, b_gate_lr, gla_norm_g, w_in_b, swa_qn_g, swa_sinks, norm_kv_g, w_kv, swa_kn_g, w_router_group, b_router_group, w_router_expert, b_router_expert, w_exp_gate, w_exp_up, w_exp_down):
    bp, tp, d = x_prompt.shape
    bs, ts, _ = x_sample.shape
    assert bp == 1 and tp % WINDOW == 0 and ts * (GLA_CHUNK // ts) == GLA_CHUNK
    n_p, n_s = bp * tp, bs * ts
    w_buf = cache_win_k.shape[1]
    assert w_buf == WINDOW
    x_p, x_s = x_prompt.reshape(n_p, d), x_sample.reshape(n_s, d)

    mem_k_p, mem_v_p = _mem_kv(mem_prompt, norm_mem_g, w_mem_kv, mem_kn_g)
    feature_major = lambda c: jnp.moveaxis(c, -3, -1).reshape(*c.shape[:-3], c.shape[-2] * c.shape[-1], c.shape[-3])
    cmk, cmv = feature_major(cache_mem_k), feature_major(cache_mem_v)

    def mem_attend(mq, l):
        tm_p = _row_tile(tp, 1024)
        mo_p = _mem_attn(mq, mem_k_p, mem_v_p, mem_qn_g[l], row_off=0, seq=tp, tm=tm_p, bb=1, layer=l)
        mo_s = _mem_attn(mq, cmk, cmv, mem_qn_g[l], row_off=n_p, seq=ts, tm=ts, bb=16, layer=l)
        return mo_p, mo_s

    tm = _row_tile(math.gcd(n_p, n_s), MOE_TILE)
    router = lambda l: (norm_ffn_g[l], w_router_group[l], b_router_group[l], w_router_expert[l],
                        b_router_expert[l])

    def moe(h, routed, l, split_rows=None):
        return _moe(h, routed, norm_ffn_g[l], w_exp_gate, w_exp_up, w_exp_down, l, tm, split_rows)

    q, k, la, v, og, mq = _inproj_a(x_p, x_s, norm_mix_g[0], w_in_a[0], w_gate_lr[0], b_gate_lr[0])
    zero_state = jnp.zeros((bp, GLA_HEADS, GLA_DK, GLA_DV), F32)
    n_sub = max(1, min(8, tp // GLA_CHUNK))
    main_p, gla_p = _gla(q, k, la, v, og, zero_state, gla_norm_g[0], row_off=0, seq=tp, n_seg=1, n_sub=n_sub)
    main_s, gla_s = _gla(q, k, la, v, og, state_gla[0], gla_norm_g[0], row_off=n_p, seq=ts,
                         n_seg=GLA_CHUNK // ts, n_sub=1)
    mo_p, mo_s = mem_attend(mq, 0)
    w_o = w_out[0]
    h, routed = _outproj((x_p, x_s), main_p, mo_p, main_s, mo_s, w_o[:MAIN_WIDTH].reshape(GLA_HEADS, GLA_DV, d),
                         w_o[MAIN_WIDTH:], router(0), tm)
    h = moe(h, routed, 0)

    q, mq, k_sh, v_sh, kt_sh = _inproj_b(h, norm_kv_g, norm_mix_g[1], w_kv, w_in_b[0], swa_kn_g, swa_qn_g[0])
    ck = feature_major(cache_win_k).reshape(bs * KV_WIDTH, w_buf)
    cv = feature_major(cache_win_v).reshape(bs * KV_WIDTH, w_buf)
    main_p = _swa_prompt(q, kt_sh, v_sh, swa_sinks[0], n_rows=n_p, nb=8)
    main_s = _swa_sample(q, ck, cv, k_sh, v_sh, swa_sinks[0], row_off=n_p, batch=bs, tq=ts, nb=16)
    mo_p, mo_s = mem_attend(mq, 1)
    w_o = w_out[1]
    h, routed = _outproj(h, main_p, mo_p, main_s, mo_s, w_o[:MAIN_WIDTH][_swa_perm()], w_o[MAIN_WIDTH:],
                         router(1), tm)
    y_p, y_s = moe(h, routed, 1, split_rows=n_p)

    y_prompt = y_p.reshape(bp, tp, d)
    y_sample = y_s.reshape(bs, ts, d)
    k_new = k_sh[n_p:].reshape(bs, ts, SWA_KV_HEADS, HEAD_DIM)
    v_new = v_sh[n_p:].reshape(bs, ts, SWA_KV_HEADS, HEAD_DIM)
    win_k_s = jnp.concatenate([cache_win_k, k_new], axis=1)[:, -w_buf:]
    win_v_s = jnp.concatenate([cache_win_v, v_new], axis=1)[:, -w_buf:]
    win_k_p = k_sh[n_p - WINDOW:n_p].reshape(bp, WINDOW, SWA_KV_HEADS, HEAD_DIM)
    win_v_p = v_sh[n_p - WINDOW:n_p].reshape(bp, WINDOW, SWA_KV_HEADS, HEAD_DIM)
    token_major = lambda c: jnp.moveaxis(c.reshape(*c.shape[:-2], MEM_HEADS, HEAD_DIM, c.shape[-1]), -1, -3)
    return (y_prompt, y_sample, gla_p[None], gla_s[None], win_k_p, win_v_p, win_k_s, win_v_s,
            token_major(mem_k_p), token_major(mem_v_p))
```

```python
import functools
import math

import numpy as np
import jax
import jax.numpy as jnp
from jax import lax
from jax.experimental import pallas as pl
from jax.experimental.pallas import tpu as pltpu

F32 = jnp.float32
BF16 = jnp.bfloat16

D_MODEL = 1024
MEM_LEN = 256
MEM_HEADS = 4
HEAD_DIM = 64
MEM_WIDTH = MEM_HEADS * HEAD_DIM
MAIN_WIDTH = D_MODEL - MEM_WIDTH
GLA_HEADS = 4
GLA_DV = MAIN_WIDTH // GLA_HEADS
GLA_DK = GLA_DV // 2
GLA_DK_PAD = 128
GLA_KEY_WIDTH = GLA_HEADS * GLA_DK
GLA_KEY_PAD = GLA_HEADS * GLA_DK_PAD
GLA_GATE_RANK = 16
GLA_TAU = 16.0
GLA_CHUNK = 64
SWA_HEADS = MAIN_WIDTH // HEAD_DIM
SWA_KV_HEADS = 4
SWA_GROUP = SWA_HEADS // SWA_KV_HEADS
KV_WIDTH = SWA_KV_HEADS * HEAD_DIM
WINDOW = 128
N_GROUPS = 4
EXPERTS_PER_GROUP = 8
N_EXPERTS = N_GROUPS * EXPERTS_PER_GROUP
D_EXPERT = 512
EPS = 1e-6
LANES = 128
NEG_BIG = -1e30
VMEM_LIMIT = 56 * 1024 * 1024
MOE_TILE = 512
MOE_SUB = 128
ROUTER_LANE0 = N_GROUPS
ROUTER_META_ROWS = 8
ROUTER_ROWS = 40
SLABS = D_MODEL // LANES
PACKED_SLABS = SLABS // 2


def _bdot(a, b):
    return jnp.dot(a.astype(BF16), b.astype(BF16), preferred_element_type=F32)


def _bdot_nt(a, b):
    return lax.dot_general(a.astype(BF16), b.astype(BF16), (((1,), (1,)), ((), ())),
                           preferred_element_type=F32)


def _bdot_tn(a, b):
    return lax.dot_general(a.astype(BF16), b.astype(BF16), (((0,), (0,)), ((), ())),
                           preferred_element_type=F32)


def _split(x, n):
    parts = []
    for _ in range(n - 1):
        p = x.astype(BF16)
        parts.append(p)
        x = x - p.astype(F32)
    parts.append(x.astype(BF16))
    return parts


def _exact_left_dot(m, x, n=2):
    out = None
    for p in _split(x, n):
        t = jnp.dot(m, p, preferred_element_type=F32)
        out = t if out is None else out + t
    return out


def _seg_mean(x2, bd):
    out = None
    for p in _split(x2, 2):
        t = jnp.dot(p, bd, preferred_element_type=F32)
        out = t if out is None else out + t
    return out


def _rms_scale(x):
    return lax.rsqrt(jnp.mean(x * x, axis=-1, keepdims=True) + EPS)


def _row_tile(n, cap=512):
    t = cap
    while t > 8 and n % t:
        t //= 2
    assert n % t == 0, n
    return t


def _params(sem):
    return pltpu.CompilerParams(dimension_semantics=sem, vmem_limit_bytes=VMEM_LIMIT)


def _const(shape):
    nd = len(shape)
    return pl.BlockSpec(shape, lambda *_: (0,) * nd)


def _group_specs(tm, width, prompt_tiles, lead=None):
    p_idx = lambda i, *_: jnp.minimum(i, prompt_tiles - 1)
    s_idx = lambda i, *_: jnp.maximum(i - prompt_tiles, 0)
    if lead is None:
        return [pl.BlockSpec((tm, width), lambda i, *_, f=f: (f(i), 0)) for f in (p_idx, s_idx)]
    return [pl.BlockSpec((lead, tm, width), lambda i, *_, f=f: (0, f(i), 0)) for f in (p_idx, s_idx)]


def _block_diag_mean(width):
    i = np.arange(width)
    return jnp.asarray((i[:, None] // HEAD_DIM == i[None, :] // HEAD_DIM) / HEAD_DIM, BF16)


def _head_masks(n_heads):
    i = np.arange(n_heads * HEAD_DIM)
    return jnp.asarray((i[None, :] // HEAD_DIM == np.arange(n_heads)[:, None]), F32)[:, None, :]


def _mem_kv_kernel(mem_ref, g_ref, w_ref, kng_ref, bd_ref, k_ref, v_ref):
    x = mem_ref[0]
    hn = x * _rms_scale(x) * g_ref[0]
    kv = _bdot(hn, w_ref[0])
    k = kv[:, :MEM_WIDTH]
    k = k * lax.rsqrt(_seg_mean(k * k, bd_ref[...]) + EPS) * kng_ref[0]
    k_ref[0, 0] = k.T
    v_ref[0, 0] = kv[:, MEM_WIDTH:].T


def _mem_kv(mem, g, w, kng):
    depth, (b, m, d) = w.shape[0], mem.shape
    out = jax.ShapeDtypeStruct((depth, b, m, MEM_WIDTH), F32)
    blk = pl.BlockSpec((1, 1, m, MEM_WIDTH), lambda l, i: (l, i, 0, 0))
    return pl.pallas_call(
        _mem_kv_kernel,
        grid=(depth, b),
        in_specs=[pl.BlockSpec((1, m, d), lambda l, i: (i, 0, 0)),
                  pl.BlockSpec((1, 1, d), lambda l, i: (l, 0, 0)),
                  pl.BlockSpec((1, d, 2 * MEM_WIDTH), lambda l, i: (l, 0, 0)),
                  pl.BlockSpec((1, 1, MEM_WIDTH), lambda l, i: (l, 0, 0)),
                  _const((MEM_WIDTH, MEM_WIDTH))],
        out_specs=[blk, blk],
        out_shape=[out, out],
        compiler_params=_params(("arbitrary", "arbitrary")),
        name="mem_kv",
    )(mem, g.reshape(depth, 1, d), w.astype(BF16),
      jnp.tile(kng, (1, MEM_HEADS)).reshape(depth, 1, MEM_WIDTH), _block_diag_mean(MEM_WIDTH))


def _inproj_a_kernel(xp_ref, xs_ref, g_ref, wq_ref, wk_ref, wv_ref, wog_ref, wlr_ref, wmq_ref, wgl_ref, bgl_ref,
                     q_ref, k_ref, la_ref, v_ref, og_ref, mq_ref, *, prompt_tiles):
    x = jnp.where(pl.program_id(0) < prompt_tiles, xp_ref[...], xs_ref[...])
    hn = (x * _rms_scale(x) * g_ref[...]).astype(BF16)
    q_ref[...] = jnp.dot(hn, wq_ref[...], preferred_element_type=F32) * (GLA_DK ** -0.5)
    k_ref[...] = jnp.dot(hn, wk_ref[...], preferred_element_type=F32)
    for h in range(GLA_HEADS):
        v_ref[h] = jnp.dot(hn, wv_ref[h], preferred_element_type=F32).astype(BF16)
        og_ref[h] = jnp.dot(hn, wog_ref[h], preferred_element_type=F32)
    lr = jnp.dot(hn, wlr_ref[...], preferred_element_type=F32)
    z = _bdot(lr, wgl_ref[...]) + bgl_ref[...]
    la_ref[...] = (jnp.minimum(z, 0.0) - jnp.log(1.0 + jnp.exp(-jnp.abs(z)))) * (1.0 / GLA_TAU)
    mq_ref[...] = jnp.dot(hn, wmq_ref[...], preferred_element_type=F32)


def _pad_heads(w, width, pad):
    lead = w.shape[:-1]
    w = w.reshape(*lead, GLA_HEADS, width)
    w = jnp.pad(w, [(0, 0)] * len(lead) + [(0, 0), (0, pad - width)])
    return w.reshape(*lead, GLA_HEADS * pad)


def _inproj_a(x_p, x_s, g, w_in, w_lr, b_lr):
    (n_p, d), n_s = x_p.shape, x_s.shape[0]
    n = n_p + n_s
    tm = _row_tile(n_s)
    assert n_p % tm == 0
    pt = n_p // tm
    c0, c1, c2, c3, c4 = (GLA_KEY_WIDTH, 2 * GLA_KEY_WIDTH, 2 * GLA_KEY_WIDTH + MAIN_WIDTH,
                          2 * GLA_KEY_WIDTH + 2 * MAIN_WIDTH,
                          2 * GLA_KEY_WIDTH + 2 * MAIN_WIDTH + GLA_GATE_RANK)
    wb = w_in.astype(BF16)
    wq = _pad_heads(wb[:, :c0], GLA_DK, GLA_DK_PAD)
    wk = _pad_heads(wb[:, c0:c1], GLA_DK, GLA_DK_PAD)
    wv = wb[:, c1:c2].reshape(d, GLA_HEADS, GLA_DV).transpose(1, 0, 2)
    wog = wb[:, c2:c3].reshape(d, GLA_HEADS, GLA_DV).transpose(1, 0, 2)
    wlr = jnp.pad(wb[:, c3:c4], ((0, 0), (0, LANES - GLA_GATE_RANK)))
    wmq = wb[:, c4:]
    wgl = jnp.pad(_pad_heads(w_lr.astype(BF16), GLA_DK, GLA_DK_PAD), ((0, LANES - GLA_GATE_RANK), (0, 0)))
    bgl = _pad_heads(b_lr.reshape(1, -1), GLA_DK, GLA_DK_PAD)
    row = lambda w: pl.BlockSpec((tm, w), lambda i: (i, 0))
    hrow = pl.BlockSpec((GLA_HEADS, tm, GLA_DV), lambda i: (0, i, 0))
    key = jax.ShapeDtypeStruct((n, GLA_KEY_PAD), F32)
    val = jax.ShapeDtypeStruct((GLA_HEADS, n, GLA_DV), F32)
    return pl.pallas_call(
        functools.partial(_inproj_a_kernel, prompt_tiles=pt),
        grid=(n // tm,),
        in_specs=_group_specs(tm, d, pt) + [
            _const((1, d)), _const(wq.shape), _const(wk.shape), _const(wv.shape),
            _const(wog.shape), _const(wlr.shape), _const(wmq.shape), _const(wgl.shape),
            _const(bgl.shape)],
        out_specs=[row(GLA_KEY_PAD), row(GLA_KEY_PAD), row(GLA_KEY_PAD), hrow, hrow, row(MEM_WIDTH)],
        out_shape=[key, key, key, jax.ShapeDtypeStruct(val.shape, BF16), val,
                   jax.ShapeDtypeStruct((n, MEM_WIDTH), F32)],
        compiler_params=_params(("parallel",)),
        name="inproj_a",
    )(x_p, x_s, g.reshape(1, d), wq, wk, wv, wog, wlr, wmq, wgl, bgl)


def _gla_kernel(q_ref, k_ref, la_ref, v_ref, og_ref, s0_ref, gn_ref, mcum_ref, mall_ref, sel_ref,
                o_ref, sout_ref, s_ref, *, chunk, n_sub, n_seg):
    j = pl.program_id(1)
    seg = chunk // n_seg

    @pl.when(j == 0)
    def _():
        s_ref[...] = jnp.zeros_like(s_ref)
        s_ref[:, :, :GLA_DK, :] = s0_ref[...]

    mcum = mcum_ref[...]
    causal = mcum.astype(F32) > 0.0
    row = lax.broadcasted_iota(jnp.int32, (chunk, GLA_DK_PAD), 0)
    gn = gn_ref[...]
    hcols = [slice(h * GLA_DK_PAD, (h + 1) * GLA_DK_PAD) for h in range(GLA_HEADS)]
    crows = [slice(c * chunk, (c + 1) * chunk) for c in range(n_sub)]
    qts, kts, kds, e_ends = [], [], [], []
    for rows in crows:
        la = la_ref[rows, :]
        b = _exact_left_dot(mcum, la)
        if n_seg == 1:
            b_end = b[chunk - 1:chunk, :]
            e_ends.append(jnp.broadcast_to(jnp.exp(b_end), (LANES, b.shape[1])).T)
        else:
            b_end = _exact_left_dot(mall_ref[...], la)
            e_ends.append(jnp.exp(_exact_left_dot(sel_ref[...], la)).T)
        k = k_ref[rows, :]
        qts.append(q_ref[rows, :] * jnp.exp(b))
        kts.append((k * jnp.exp(-b)).astype(BF16))
        kds.append(k * jnp.exp(b_end - b))
    vbs = [[v_ref[h, rows, :].astype(BF16) for h in range(GLA_HEADS)] for rows in crows]
    scores = [[_bdot_nt(qts[c][:, cols], kts[c][:, cols]) for cols in hcols] for c in range(n_sub)]
    kvs = []
    for c in range(n_sub):
        per_head = []
        for h, cols in enumerate(hcols):
            per_seg = []
            for s in range(n_seg):
                kd = kds[c][:, cols]
                if n_seg > 1:
                    kd = jnp.where((row >= s * seg) & (row < (s + 1) * seg), kd, 0.0)
                per_seg.append(_bdot_tn(kd, vbs[c][h]))
            per_head.append(per_seg)
        kvs.append(per_head)
    state = [[s_ref[s, h] for s in range(n_seg)] for h in range(GLA_HEADS)]
    inters = []
    for c in range(n_sub):
        per_head = []
        for h, cols in enumerate(hcols):
            parts = []
            for s in range(n_seg):
                parts.append(_bdot(qts[c][s * seg:(s + 1) * seg, cols], state[h][s]))
                state[h][s] = e_ends[c][cols, s:s + 1] * state[h][s] + kvs[c][h][s]
            per_head.append(parts[0] if n_seg == 1 else jnp.concatenate(parts, axis=0))
        inters.append(per_head)
    for h in range(GLA_HEADS):
        for s in range(n_seg):
            s_ref[s, h] = state[h][s]
    for c, rows in enumerate(crows):
        for h in range(GLA_HEADS):
            a = jnp.where(causal, scores[c][h], 0.0)
            o = _bdot(a, vbs[c][h]) + inters[c][h]
            on = o * lax.rsqrt(jnp.mean(o * o, axis=-1, keepdims=True) + EPS) * gn
            og = og_ref[h, rows, :]
            o_ref[h, rows, :] = (on * (og * jax.nn.sigmoid(og))).astype(BF16)

    @pl.when(j == pl.num_programs(1) - 1)
    def _():
        sout_ref[...] = s_ref[:, :, :GLA_DK, :]


def _gla(q, k, la, v, og, s0, gnorm, *, row_off, seq, n_seg, n_sub):
    batch = s0.shape[0]
    chunk = GLA_CHUNK
    assert chunk % n_seg == 0 and batch % n_seg == 0
    seg = chunk // n_seg
    step_rows = n_sub * chunk
    if n_seg > 1:
        assert seq == seg and n_sub == 1
        t_steps = 1
    else:
        assert seq % step_rows == 0
        t_steps = seq // step_rows
    assert row_off % step_rows == 0
    off = row_off // step_rows
    i = np.arange(chunk)
    same = (i[:, None] // seg) == (i[None, :] // seg)
    mcum = jnp.asarray(same & (i[None, :] <= i[:, None]), BF16)
    mall = jnp.asarray(same, BF16)
    sel = jnp.asarray((i[None, :] // seg) == np.arange(LANES)[:, None], BF16)
    ridx = lambda g, j: (off + g * t_steps + j, 0)
    hidx = lambda g, j: (0, off + g * t_steps + j, 0)
    key_spec = pl.BlockSpec((step_rows, GLA_KEY_PAD), ridx)
    val_spec = pl.BlockSpec((GLA_HEADS, step_rows, GLA_DV), hidx)
    st_spec = pl.BlockSpec((n_seg, GLA_HEADS, GLA_DK, GLA_DV), lambda g, j: (g, 0, 0, 0))
    in_specs = [key_spec, key_spec, key_spec, val_spec, val_spec, st_spec, _const((1, GLA_DV)),
                _const((chunk, chunk)), _const((chunk, chunk)), _const((LANES, chunk))]
    args = [q, k, la, v, og, s0, gnorm.reshape(1, GLA_DV), mcum, mall, sel]
    out_spec = pl.BlockSpec((GLA_HEADS, step_rows, GLA_DV), lambda g, j: (0, g * t_steps + j, 0))
    return pl.pallas_call(
        functools.partial(_gla_kernel, chunk=chunk, n_sub=n_sub, n_seg=n_seg),
        grid=(batch // n_seg, t_steps),
        in_specs=in_specs,
        out_specs=[out_spec, st_spec],
        out_shape=[jax.ShapeDtypeStruct((GLA_HEADS, batch * seq, GLA_DV), BF16),
                   jax.ShapeDtypeStruct(s0.shape, F32)],
        scratch_shapes=[pltpu.VMEM((n_seg, GLA_HEADS, GLA_DK_PAD, GLA_DV), F32)],
        compiler_params=_params(("arbitrary", "arbitrary")),
        name="gla",
    )(*args)


def _mem_attn_kernel(q_ref, k_ref, v_ref, g_ref, bd_ref, hm_ref, o_ref, *, tm, bb):
    g = g_ref[...]
    sub = min(tm, 128)
    units = [(i, i * tm + r) for i in range(bb) for r in range(0, tm, sub)]
    scores = []
    for i, r in units:
        q = q_ref[r:r + sub, :]
        qn = q * lax.rsqrt(_seg_mean(q * q, bd_ref[...]) + EPS) * g
        qs = jnp.concatenate([(qn * hm_ref[h]).astype(BF16) for h in range(MEM_HEADS)], axis=0)
        scores.append(_bdot(qs, k_ref[i]))
    probs = []
    for s in scores:
        e = jnp.exp(s - jnp.max(s, axis=-1, keepdims=True))
        probs.append(e * (1.0 / jnp.sum(e, axis=-1, keepdims=True)))
    outs = [_bdot_nt(p, v_ref[i]) for (i, _), p in zip(units, probs)]
    rows = []
    for o in outs:
        acc = o[:sub] * hm_ref[0]
        for h in range(1, MEM_HEADS):
            acc = acc + o[h * sub:(h + 1) * sub] * hm_ref[h]
        rows.append(acc)
    o_ref[...] = jnp.concatenate(rows, axis=0).astype(BF16)


def _mem_attn(mq, mk, mv, qng, *, row_off, seq, tm, bb, layer):
    depth, batch, m, _ = mk.shape
    mk = mk.reshape(depth * batch, m, MEM_WIDTH)
    mv = mv.reshape(depth * batch, m, MEM_WIDTH)
    kv_off = layer * batch // bb
    assert seq % tm == 0 and batch % bb == 0 and (bb == 1 or seq == tm)
    t_steps = seq // tm
    step_rows = bb * tm
    assert row_off % step_rows == 0
    off = row_off // step_rows
    row_spec = pl.BlockSpec((step_rows, MEM_WIDTH), lambda g, j: (off + g * t_steps + j, 0))
    kv_spec = pl.BlockSpec((bb, m, MEM_WIDTH), lambda g, j: (kv_off + g, 0, 0))
    in_specs = [row_spec, kv_spec, kv_spec, _const((1, MEM_WIDTH)), _const((MEM_WIDTH, MEM_WIDTH)),
                _const((MEM_HEADS, 1, MEM_WIDTH))]
    args = [mq, mk, mv, (jnp.tile(qng, MEM_HEADS) * HEAD_DIM ** -0.5).reshape(1, MEM_WIDTH),
            _block_diag_mean(MEM_WIDTH), _head_masks(MEM_HEADS)]
    return pl.pallas_call(
        functools.partial(_mem_attn_kernel, tm=tm, bb=bb),
        grid=(batch // bb, t_steps),
        in_specs=in_specs,
        out_specs=pl.BlockSpec((step_rows, MEM_WIDTH), lambda g, j: (g * t_steps + j, 0)),
        out_shape=jax.ShapeDtypeStruct((batch * seq, MEM_WIDTH), BF16),
        compiler_params=_params(("parallel", "parallel")),
        name="mem_attn",
    )(*args)


def _outproj_kernel(*refs, heads, prompt_tiles, split_residual):
    n_h = 2 if split_residual else 1
    h_refs, refs = refs[:n_h], refs[n_h:]
    (main_p_ref, main_s_ref, mo_p_ref, mo_s_ref, wmain_ref, wmo_ref), refs = refs[:6], refs[6:]
    route_in, (o_ref, *route_out) = refs[:6], refs[6:]
    is_prompt = pl.program_id(0) < prompt_tiles
    pick = lambda p, s: jnp.where(is_prompt, p, s)
    tm = o_ref.shape[0]
    sub = min(tm, 128)
    blocks = []
    for r in range(0, tm, sub):
        rows = slice(r, r + sub)
        acc = pick(h_refs[0][rows, :], h_refs[1][rows, :]) if split_residual else h_refs[0][rows, :]
        acc = acc + _bdot(pick(mo_p_ref[rows, :], mo_s_ref[rows, :]), wmo_ref[...])
        if heads:
            for h in range(heads):
                acc = acc + _bdot(pick(main_p_ref[h, rows, :], main_s_ref[h, rows, :]), wmain_ref[h])
        else:
            acc = acc + _bdot(pick(main_p_ref[rows, :], main_s_ref[rows, :]), wmain_ref[...])
        o_ref[rows, :] = acc
        blocks.append(acc)
    _route_tile(blocks, *route_in, *route_out)


def _outproj(h, main_p, mo_p, main_s, mo_s, w_main, w_mo, router, tm):
    n_p, n_s = mo_p.shape[0], mo_s.shape[0]
    n, d = n_p + n_s, w_mo.shape[1]
    assert n_p % tm == 0 and n_s % tm == 0
    pt = n_p // tm
    heads = main_p.shape[0] if main_p.ndim == 3 else 0
    row = pl.BlockSpec((tm, d), lambda i: (i, 0))
    split = isinstance(h, tuple)
    h_specs, h_args = (_group_specs(tm, d, pt), list(h)) if split else ([row], [h])
    main_specs = _group_specs(tm, main_p.shape[-1], pt, lead=heads or None)
    r_args, r_in_specs, r_out_specs, r_out_shape, r_scratch = _route_operands(*router, n, tm)
    h_new, *routed = pl.pallas_call(
        functools.partial(_outproj_kernel, heads=heads, prompt_tiles=pt, split_residual=split),
        grid=(n // tm,),
        in_specs=(h_specs + main_specs + _group_specs(tm, MEM_WIDTH, pt)
                  + [_const(w_main.shape), _const(w_mo.shape)] + r_in_specs),
        out_specs=[row] + r_out_specs,
        out_shape=[jax.ShapeDtypeStruct((n, d), F32)] + r_out_shape,
        scratch_shapes=[r_scratch],
        compiler_params=_params(("arbitrary",)),
        name="outproj_route",
    )(*h_args, main_p, main_s, mo_p, mo_s, w_main.astype(BF16), w_mo.astype(BF16), *r_args)
    return h_new, routed


def _route_tile(x_blocks, g_ref, whi_ref, wlo_ref, b_ref, before_tok_ref, before_row_ref,
                mi_ref, mf_ref, cnt_ref, tt_ref, carry_ref):
    @pl.when(pl.program_id(0) == 0)
    def _():
        carry_ref[...] = jnp.zeros_like(carry_ref)

    nt = lambda a, b: lax.dot_general(a, b, (((1,), (1,)), ((), ())), preferred_element_type=F32)
    logits = []
    for x in x_blocks:
        x_hi, x_lo = _split(x * _rms_scale(x) * g_ref[...], 2)
        logits.append((nt(whi_ref[...], x_hi) + nt(wlo_ref[...], x_hi) + nt(whi_ref[...], x_lo))[:ROUTER_ROWS])
    logits = jnp.concatenate(logits, axis=1) + b_ref[...]
    tm = logits.shape[1]
    row = lax.broadcasted_iota(jnp.int32, (ROUTER_ROWS, tm), 0)
    far = jnp.int32(2 * LANES)

    def first_max(vals):
        m = jnp.max(vals, axis=0, keepdims=True)
        return m, jnp.min(jnp.where(vals == m, row, far), axis=0, keepdims=True)

    gl = jnp.where(row < N_GROUPS, logits, -jnp.inf)
    gmax, grp = first_max(gl)
    pg_sel = 1.0 / jnp.sum(jnp.exp(gl - gmax), axis=0, keepdims=True)
    lo = ROUTER_LANE0 + grp * EXPERTS_PER_GROUP
    el = jnp.where((row >= lo) & (row < lo + EXPERTS_PER_GROUP), logits, -jnp.inf)
    m1, i1 = first_max(el)
    m2, i2 = first_max(jnp.where(row == i1, -jnp.inf, el))
    e2 = jnp.exp(m2 - m1)
    g1 = pg_sel / (1.0 + e2)
    g2 = pg_sel * e2 / (1.0 + e2)

    oh1 = row == i1
    oh2 = row == i2
    picked = jnp.where(oh1 | oh2, 1.0, 0.0)
    earlier = jnp.dot(picked.astype(BF16), before_tok_ref[...], preferred_element_type=F32)
    cnt_col = jnp.sum(picked, axis=1, keepdims=True)
    cnt_tile = jnp.concatenate([jnp.broadcast_to(cnt_col, (ROUTER_ROWS, LANES)),
                                jnp.zeros((LANES - ROUTER_ROWS, LANES), F32)], axis=0)
    c_hi = jnp.floor(cnt_tile * (1.0 / 32.0))
    c_lo = cnt_tile - 32.0 * c_hi
    first = (32.0 * jnp.dot(before_row_ref[...], c_hi.astype(BF16), preferred_element_type=F32)
             + jnp.dot(before_row_ref[...], c_lo.astype(BF16), preferred_element_type=F32))
    local = first[:ROUTER_ROWS, 0:1] + earlier
    lpos1 = jnp.sum(jnp.where(oh1, local, 0.0), axis=0, keepdims=True)
    lpos2 = jnp.sum(jnp.where(oh2, local, 0.0), axis=0, keepdims=True)
    carry_before = carry_ref[...]
    carry = carry_before + cnt_tile
    carry_ref[...] = carry

    lane = lax.broadcasted_iota(jnp.int32, (LANES, LANES), 1)
    cols = jnp.where(lane == 0, carry_before, jnp.where(lane == 1, cnt_tile, jnp.where(lane == 2, first,
                     jnp.where(lane == 3, carry, 0.0))))
    tables = cols.T
    tt_ref[...] = tables[:8].astype(jnp.int32)
    cnt_ref[...] = tables[3:4]
    row8 = lax.broadcasted_iota(jnp.int32, (ROUTER_META_ROWS, tm), 0)
    zero8 = jnp.zeros((ROUTER_META_ROWS, tm), F32)
    mi_ref[...] = jnp.where(row8 == 0, lpos1, jnp.where(row8 == 1, lpos2, zero8)).astype(jnp.int32)
    stacked = jnp.where(row8 == 0, g1, jnp.where(row8 == 1, g2, jnp.where(row8 == 2, lpos1,
                        jnp.where(row8 == 3, lpos2, zero8))))
    mf_ref[...] = jnp.concatenate([stacked, jnp.zeros((LANES - ROUTER_META_ROWS, tm), F32)], axis=0).T


def _route_operands(g, w_rg, b_rg, w_re, b_re, n, tm):
    d = g.shape[0]
    assert n % tm == 0 and 2 * tm <= 32 * 32
    n_real = N_GROUPS + N_EXPERTS
    w = jnp.pad(jnp.concatenate([w_rg, w_re], axis=1), ((0, 0), (0, LANES - n_real))).T
    b = jnp.pad(jnp.concatenate([b_rg, b_re]), (0, ROUTER_ROWS - n_real)).reshape(ROUTER_ROWS, 1)
    w_hi = w.astype(BF16)
    w_lo = (w - w_hi.astype(F32)).astype(BF16)
    i = np.arange(tm)
    before_tok = jnp.asarray(i[:, None] < i[None, :], BF16)
    e = np.arange(LANES)
    before_row = jnp.asarray(e[None, :] < e[:, None], BF16)
    args = [g.reshape(1, d), w_hi, w_lo, b, before_tok, before_row]
    in_specs = [_const(a.shape) for a in args]
    out_specs = [pl.BlockSpec((ROUTER_META_ROWS, tm), lambda i: (0, i)), pl.BlockSpec((tm, LANES), lambda i: (i, 0)),
                 _const((1, LANES)), pl.BlockSpec((8, LANES), lambda i: (i, 0))]
    out_shape = [jax.ShapeDtypeStruct((ROUTER_META_ROWS, n), jnp.int32), jax.ShapeDtypeStruct((n, LANES), F32),
                 jax.ShapeDtypeStruct((1, LANES), F32), jax.ShapeDtypeStruct((n // tm * 8, LANES), jnp.int32)]
    return args, in_specs, out_specs, out_shape, pltpu.VMEM((LANES, LANES), F32)


def _pack_rows(ref, x, rows, lead=(), row0=0):
    u32 = jnp.uint32
    for w in range(PACKED_SLABS):
        lo = x[:, (2 * w) * LANES:(2 * w + 1) * LANES].astype(BF16).astype(F32)
        hi = x[:, (2 * w + 1) * LANES:(2 * w + 2) * LANES].astype(BF16).astype(F32)
        word = (lax.bitcast_convert_type(lo, u32) >> 16) | (lax.bitcast_convert_type(hi, u32) & u32(0xFFFF0000))
        ref[lead + (pl.ds(row0 * PACKED_SLABS + w, rows, stride=PACKED_SLABS), slice(None))] = word


def _unpack_rows(ref, rows, lead=(), row0=0):
    u32 = jnp.uint32
    slabs = []
    for w in range(PACKED_SLABS):
        word = ref[lead + (pl.ds(row0 * PACKED_SLABS + w, rows, stride=PACKED_SLABS), slice(None))]
        slabs.append(lax.bitcast_convert_type(word << 16, F32).astype(BF16))
        slabs.append(lax.bitcast_convert_type(word & u32(0xFFFF0000), F32).astype(BF16))
    return jnp.concatenate(slabs, axis=1)


RUN_FIELDS = 3
RUN_CHUNK_BITS = 6


def _copy_runs(runs_ref, tile, local_rows, global_rows, sem, *, to_global):
    ps = PACKED_SLABS
    base = tile * (RUN_FIELDS * N_EXPERTS)

    def piece(g0, l0, off, size):
        g = global_rows(pl.multiple_of((g0 + off) * ps, ps), size * ps)
        l = local_rows(pl.multiple_of((l0 + off) * ps, ps), size * ps)
        src, dst = (l, g) if to_global else (g, l)
        pltpu.make_async_copy(src, dst, sem).start()

    def per_expert(e, carry):
        g0 = runs_ref[base + e]
        length = runs_ref[base + N_EXPERTS + e]
        l0 = runs_ref[base + 2 * N_EXPERTS + e]
        big = 1 << RUN_CHUNK_BITS

        def big_piece(c, inner):
            piece(g0, l0, c * big, big)
            return inner

        n_big = length >> RUN_CHUNK_BITS
        lax.fori_loop(0, n_big, big_piece, 0)
        off = n_big * big
        for bit in reversed(range(RUN_CHUNK_BITS)):
            size = 1 << bit

            @pl.when((length & size) != 0)
            def _(off=off, size=size):
                piece(g0, l0, off, size)

            off = off + (length & size)
        return carry

    lax.fori_loop(0, N_EXPERTS, per_expert, 0)


def _fill_pads(pads_ref, buf0, xs_hbm, sem, *, wait):
    ps = PACKED_SLABS

    def go(copy):
        copy.wait() if wait else copy.start()

    def per_tail_tile(t, carry):
        rows = MOE_TILE * ps
        go(pltpu.make_async_copy(
            buf0.at[pl.ds(0, rows)],
            xs_hbm.at[pl.ds(pl.multiple_of((pads_ref[2 * N_EXPERTS] + t) * rows, rows), rows)], sem))
        return carry

    lax.fori_loop(0, pads_ref[2 * N_EXPERTS + 1], per_tail_tile, 0)

    def per_expert(e, carry):
        first, length = pads_ref[e], pads_ref[N_EXPERTS + e]
        off = 0
        for bit in reversed(range((MOE_TILE - 1).bit_length())):
            size = 1 << bit

            @pl.when((length & size) != 0)
            def _(off=off, size=size):
                go(pltpu.make_async_copy(
                    buf0.at[pl.ds(pl.multiple_of(off * ps, ps), size * ps)],
                    xs_hbm.at[pl.ds(pl.multiple_of((first + off) * ps, ps), size * ps)], sem))

            off = off + (length & size)
        return carry

    lax.fori_loop(0, N_EXPERTS, per_expert, 0)


def _dispatch_kernel(runs_ref, pads_ref, h_ref, g_ref, meta_ref, xs_hbm, buf, sem, pad_sem, *, tm, steps):
    i = pl.program_id(0)
    slot = lax.rem(i, 2)
    ns = 2 * tm
    assert ns >= MOE_TILE

    def wait_slot(sl):
        pltpu.make_async_copy(buf.at[sl], xs_hbm.at[pl.ds(0, ns * PACKED_SLABS)], sem.at[sl]).wait()

    @pl.when(i >= 2)
    def _():
        wait_slot(slot)

    x = h_ref[...]
    xn = (x * _rms_scale(x) * g_ref[...]).astype(BF16)
    j = lax.broadcasted_iota(jnp.int32, (ns, tm), 0)
    pick = jnp.where((j == meta_ref[0:1, :]) | (j == meta_ref[1:2, :]), 1.0, 0.0).astype(BF16)
    _pack_rows(buf, jnp.dot(pick, xn, preferred_element_type=F32), ns, (slot,))
    _copy_runs(runs_ref, i, lambda start, size: buf.at[slot, pl.ds(start, size)],
               lambda start, size: xs_hbm.at[pl.ds(start, size)], sem.at[slot], to_global=True)

    @pl.when(i == 0)
    def _():
        _fill_pads(pads_ref, buf.at[0], xs_hbm, pad_sem, wait=False)

    @pl.when(i == min(1, steps - 1))
    def _():
        _fill_pads(pads_ref, buf.at[0], xs_hbm, pad_sem, wait=True)

    @pl.when(i == steps - 1)
    def _():
        wait_slot(slot)
        if steps > 1:
            wait_slot(1 - slot)


def _dispatch(h, g, meta, runs, pads, tm, n_slots):
    n, d = h.shape
    steps = n // tm
    return pl.pallas_call(
        functools.partial(_dispatch_kernel, tm=tm, steps=steps),
        grid_spec=pltpu.PrefetchScalarGridSpec(
            num_scalar_prefetch=2,
            grid=(steps,),
            in_specs=[pl.BlockSpec((tm, d), lambda i, *_: (i, 0)),
                      pl.BlockSpec((1, d), lambda i, *_: (0, 0)),
                      pl.BlockSpec((ROUTER_META_ROWS, tm), lambda i, *_: (0, i))],
            out_specs=pl.BlockSpec(memory_space=pl.ANY),
            scratch_shapes=[pltpu.VMEM((2, 2 * tm * PACKED_SLABS, LANES), jnp.uint32),
                            pltpu.SemaphoreType.DMA((2,)), pltpu.SemaphoreType.DMA],
        ),
        out_shape=jax.ShapeDtypeStruct((n_slots * PACKED_SLABS, LANES), jnp.uint32),
        compiler_params=_params(("arbitrary",)),
        name="moe_dispatch",
    )(runs, pads, h, g.reshape(1, d), meta)


def _expert_kernel(tile_ref, exp_ref, new_ref, wslot_ref, next_ref, n_ref,
                   xs_hbm, wg_hbm, wu_hbm, wd_hbm, ys_hbm,
                   xbuf, ybuf, xsem, ysem, wgf, wuf, wdf, wsem, wgb, wub, wdb, *, tm, sub, layer, max_tiles):
    n = n_ref[0]
    rows = tm * PACKED_SLABS
    tile_rows = lambda hbm, t: hbm.at[pl.ds(pl.multiple_of(t * rows, rows), rows)]
    x_copy = lambda t, slot: pltpu.make_async_copy(tile_rows(xs_hbm, t), xbuf.at[slot], xsem.at[slot])
    y_copy = lambda t, slot: pltpu.make_async_copy(ybuf.at[slot], tile_rows(ys_hbm, t), ysem.at[slot])

    def weight_copies(e, slot):
        return [pltpu.make_async_copy(hbm.at[layer, e], buf.at[slot], wsem.at[slot])
                for hbm, buf in ((wg_hbm, wgf), (wu_hbm, wuf), (wd_hbm, wdf))]

    x_copy(0, 0).start()

    def item(w, carry):
        io = lax.rem(w, 2)
        x_copy(w, io).wait()

        @pl.when(w + 1 < n)
        def _():
            x_copy(w + 1, 1 - io).start()

        @pl.when(w >= 2)
        def _():
            y_copy(w - 2, io).wait()

        @pl.when(new_ref[w] != 0)
        def _():
            slot = wslot_ref[w]

            @pl.when(w == 0)
            def _():
                for c in weight_copies(exp_ref[w], slot):
                    c.start()

            for c in weight_copies(exp_ref[w], slot):
                c.wait()
            wgb[...] = wgf[slot].astype(BF16)
            wub[...] = wuf[slot].astype(BF16)
            wdb[...] = wdf[slot].astype(BF16)

            @pl.when(next_ref[w] >= 0)
            def _():
                for c in weight_copies(next_ref[w], 1 - slot):
                    c.start()

        n_blocks = tm // sub

        def up(s):
            x = _unpack_rows(xbuf, sub, (io,), row0=s * sub)
            return (jnp.dot(x, wgb[...], preferred_element_type=F32),
                    jnp.dot(x, wub[...], preferred_element_type=F32))

        ups = {0: up(0)}
        for s in range(n_blocks):
            if s + 1 < n_blocks:
                ups[s + 1] = up(s + 1)
            hg, hu = ups.pop(s)
            act = (hg * jax.nn.sigmoid(hg) * hu).astype(BF16)
            y = jnp.dot(act, wdb[...], preferred_element_type=F32)
            _pack_rows(ybuf, y, sub, (io,), row0=s * sub)
        y_copy(w, io).start()
        return carry

    lax.fori_loop(0, n, item, 0)

    @pl.when(n >= 2)
    def _():
        y_copy(n - 2, lax.rem(n, 2)).wait()
    y_copy(n - 1, lax.rem(n - 1, 2)).wait()

    for wait in (False, True):
        def spare(t, carry, wait=wait):
            copy = y_copy(n + t, 0)
            copy.wait() if wait else copy.start()
            return carry

        lax.fori_loop(0, max_tiles - n, spare, 0)


def _experts(xs, items, w_g, w_u, w_d, layer):
    d = D_MODEL
    tm = MOE_TILE
    max_tiles = items[0].shape[0]
    any_spec = pl.BlockSpec(memory_space=pl.ANY)
    io_buf = pltpu.VMEM((2, tm * PACKED_SLABS, LANES), jnp.uint32)
    return pl.pallas_call(
        functools.partial(_expert_kernel, tm=tm, sub=MOE_SUB, layer=layer, max_tiles=max_tiles),
        grid_spec=pltpu.PrefetchScalarGridSpec(
            num_scalar_prefetch=len(items),
            grid=(1,),
            in_specs=[any_spec, any_spec, any_spec, any_spec],
            out_specs=any_spec,
            scratch_shapes=[io_buf, io_buf, pltpu.SemaphoreType.DMA((2,)), pltpu.SemaphoreType.DMA((2,)),
                            pltpu.VMEM((2, d, D_EXPERT), F32), pltpu.VMEM((2, d, D_EXPERT), F32),
                            pltpu.VMEM((2, D_EXPERT, d), F32), pltpu.SemaphoreType.DMA((2,)),
                            pltpu.VMEM((d, D_EXPERT), BF16), pltpu.VMEM((d, D_EXPERT), BF16),
                            pltpu.VMEM((D_EXPERT, d), BF16)],
        ),
        out_shape=jax.ShapeDtypeStruct(xs.shape, jnp.uint32),
        compiler_params=_params(("arbitrary",)),
        name="moe_experts",
    )(*items, xs, w_g, w_u, w_d)


def _combine_kernel(runs_ref, h_ref, gate_ref, ys_hbm, *refs, tm, steps, prompt_tiles):
    out_refs, (buf, sem) = refs[:-2], refs[-2:]
    i = pl.program_id(0)
    slot = lax.rem(i, 2)
    ns = 2 * tm

    def issue(step, sl):
        _copy_runs(runs_ref, step, lambda start, size: buf.at[sl, pl.ds(start, size)],
                   lambda start, size: ys_hbm.at[pl.ds(start, size)], sem.at[sl], to_global=False)

    @pl.when(i == 0)
    def _():
        issue(0, 0)

    @pl.when(i + 1 < steps)
    def _():
        issue(i + 1, 1 - slot)

    pltpu.make_async_copy(ys_hbm.at[pl.ds(0, ns * PACKED_SLABS)], buf.at[slot], sem.at[slot]).wait()
    y = _unpack_rows(buf, ns, (slot,))
    g = gate_ref[...]
    j = lax.broadcasted_iota(jnp.int32, (tm, ns), 1)
    mix = (jnp.where(j == g[:, 2:3].astype(jnp.int32), g[:, 0:1], 0.0)
           + jnp.where(j == g[:, 3:4].astype(jnp.int32), g[:, 1:2], 0.0)).astype(BF16)
    out = h_ref[...] + jnp.dot(mix, y, preferred_element_type=F32)
    if len(out_refs) == 1:
        out_refs[0][...] = out
    else:
        @pl.when(i < prompt_tiles)
        def _():
            out_refs[0][...] = out

        @pl.when(i >= prompt_tiles)
        def _():
            out_refs[1][...] = out


def _combine(h, gates, ys, runs, tm, split_rows=None):
    n, d = h.shape
    steps = n // tm
    row = pl.BlockSpec((tm, d), lambda i, pos: (i, 0))
    if split_rows is None:
        pt, out_specs, out_shape = 0, row, jax.ShapeDtypeStruct((n, d), F32)
    else:
        assert split_rows % tm == 0
        pt = split_rows // tm
        out_specs = _group_specs(tm, d, pt)
        out_shape = [jax.ShapeDtypeStruct((split_rows, d), F32), jax.ShapeDtypeStruct((n - split_rows, d), F32)]
    return pl.pallas_call(
        functools.partial(_combine_kernel, tm=tm, steps=steps, prompt_tiles=pt),
        grid_spec=pltpu.PrefetchScalarGridSpec(
            num_scalar_prefetch=1,
            grid=(steps,),
            in_specs=[row, pl.BlockSpec((tm, LANES), lambda i, pos: (i, 0)),
                      pl.BlockSpec(memory_space=pl.ANY)],
            out_specs=out_specs,
            scratch_shapes=[pltpu.VMEM((2, 2 * tm * PACKED_SLABS, LANES), jnp.uint32),
                            pltpu.SemaphoreType.DMA((2,))],
        ),
        out_shape=out_shape,
        compiler_params=_params(("arbitrary",)),
        name="moe_combine",
    )(runs, h, gates, ys)


def _lookup(tables, idx):
    hit = idx[:, None] == jnp.arange(tables.shape[1], dtype=idx.dtype)[None, :]
    return jnp.sum(jnp.where(hit[None], tables[:, None, :], 0), axis=2)


def _work_items(tiles_e, max_items):
    item_end = jnp.cumsum(tiles_e)
    n_items = item_end[-1]
    w = jnp.minimum(jnp.arange(max_items, dtype=jnp.int32), n_items - 1)
    expert = jnp.sum(w[:, None] >= item_end[None, :], axis=1).astype(jnp.int32)
    prev_expert = jnp.concatenate([jnp.full((1,), -1, jnp.int32), expert[:-1]])
    new_expert = expert != prev_expert
    weight_slot = (jnp.cumsum(new_expert.astype(jnp.int32)) - 1) % 2
    ids = jnp.arange(N_EXPERTS, dtype=jnp.int32)
    later = (ids[None, :] > ids[:, None]) & (tiles_e[None, :] > 0)
    following = jnp.min(jnp.where(later, ids[None, :], N_EXPERTS), axis=1)
    following = jnp.where(following == N_EXPERTS, -1, following)
    next_expert, = _lookup(following[None, :], expert)
    as_i32 = lambda a: a.astype(jnp.int32)
    return (as_i32(w), expert, as_i32(new_expert), as_i32(weight_slot), as_i32(next_expert),
            as_i32(n_items).reshape(1))


def _moe(h, routed, g, w_g, w_u, w_d, layer, tm, split_rows=None):
    n, _ = h.shape
    meta, gates, cnt, tables = routed
    experts = slice(ROUTER_LANE0, ROUTER_LANE0 + N_EXPERTS)
    counts = cnt[0, experts].astype(jnp.int32)
    tiles_e = (counts + MOE_TILE - 1) // MOE_TILE
    starts = (jnp.cumsum(tiles_e) - tiles_e) * MOE_TILE
    max_tiles = -(-2 * n // MOE_TILE) + N_EXPERTS
    tables = tables.reshape(n // tm, 8, LANES)[:, :RUN_FIELDS, experts]
    runs = tables.at[:, 0, :].add(starts[None, :]).reshape(-1)
    used = jnp.sum(tiles_e)
    pads = jnp.concatenate([starts + counts, tiles_e * MOE_TILE - counts, jnp.stack([used, max_tiles - used])])
    xs = _dispatch(h, g, meta, runs, pads, tm, max_tiles * MOE_TILE)
    ys = _experts(xs, _work_items(tiles_e, max_tiles), w_g, w_u, w_d, layer)
    return _combine(h, gates, ys, runs, tm, split_rows)


def _inproj_b_kernel(x_ref, gkv_ref, gmix_ref, wkv_ref, win_ref, kng_ref, qng_ref, bdk_ref, hsum_ref, hexp_ref,
                     q_ref, mq_ref, k_ref, v_ref, kt_ref):
    x = x_ref[...]
    xr = x * _rms_scale(x)
    kv = _bdot(xr * gkv_ref[...], wkv_ref[...])
    k = kv[:, :KV_WIDTH]
    k = k * lax.rsqrt(_seg_mean(k * k, bdk_ref[...]) + EPS) * kng_ref[...]
    k_ref[...] = k
    kt_ref[...] = k.T.astype(BF16)
    v_ref[...] = kv[:, KV_WIDTH:]
    proj = _bdot(xr * gmix_ref[...], win_ref[...])
    q = proj[:, :MAIN_WIDTH]
    ms = None
    for p in _split(q * q, 2):
        t = jnp.dot(p, hsum_ref[...], preferred_element_type=F32)
        ms = t if ms is None else ms + t
    scale = None
    for p in _split(lax.rsqrt(ms + EPS), 2):
        t = jnp.dot(p, hexp_ref[...], preferred_element_type=F32)
        scale = t if scale is None else scale + t
    q_ref[...] = (q * scale * qng_ref[...]).astype(BF16)
    mq_ref[...] = proj[:, MAIN_WIDTH:]


def _swa_perm():
    g, kh, dd = np.meshgrid(np.arange(SWA_GROUP), np.arange(SWA_KV_HEADS), np.arange(HEAD_DIM), indexing="ij")
    return ((kh * SWA_GROUP + g) * HEAD_DIM + dd).reshape(-1)


def _inproj_b(x, g_kv, g_mix, w_kv, w_in, kng, qng):
    n, d = x.shape
    tm = _row_tile(n)
    perm = _swa_perm()
    w_in_p = jnp.concatenate([w_in[:, :MAIN_WIDTH][:, perm], w_in[:, MAIN_WIDTH:]], axis=1).astype(BF16)
    qng_t = (jnp.tile(qng, SWA_HEADS) * HEAD_DIM ** -0.5).reshape(1, MAIN_WIDTH)
    member = (np.arange(MAIN_WIDTH)[:, None] // HEAD_DIM == np.arange(LANES)[None, :]).astype(np.float32)
    row = lambda w: pl.BlockSpec((tm, w), lambda i: (i, 0))
    return pl.pallas_call(
        _inproj_b_kernel,
        grid=(n // tm,),
        in_specs=[row(d), _const((1, d)), _const((1, d)), _const((d, 2 * KV_WIDTH)), _const((d, d)),
                  _const((1, KV_WIDTH)), _const((1, MAIN_WIDTH)), _const((KV_WIDTH, KV_WIDTH)),
                  _const((MAIN_WIDTH, LANES)), _const((LANES, MAIN_WIDTH))],
        out_specs=[row(MAIN_WIDTH), row(MEM_WIDTH), row(KV_WIDTH), row(KV_WIDTH),
                   pl.BlockSpec((KV_WIDTH, tm), lambda i: (0, i))],
        out_shape=[jax.ShapeDtypeStruct((n, MAIN_WIDTH), BF16), jax.ShapeDtypeStruct((n, MEM_WIDTH), F32),
                   jax.ShapeDtypeStruct((n, KV_WIDTH), F32), jax.ShapeDtypeStruct((n, KV_WIDTH), F32),
                   jax.ShapeDtypeStruct((KV_WIDTH, n), BF16)],
        compiler_params=_params(("parallel",)),
        name="inproj_b",
    )(x, g_kv.reshape(1, d), g_mix.reshape(1, d), w_kv.astype(BF16), w_in_p,
      jnp.tile(kng, SWA_KV_HEADS).reshape(1, KV_WIDTH), qng_t,
      _block_diag_mean(KV_WIDTH), jnp.asarray(member / HEAD_DIM, BF16), jnp.asarray(member.T, BF16))


def _softmax_with_sink(s, sink):
    m = jnp.maximum(jnp.max(s, axis=-1, keepdims=True), sink)
    e = jnp.exp(s - m)
    r = 1.0 / (jnp.sum(e, axis=-1, keepdims=True) + jnp.exp(sink - m))
    return (e * r).astype(BF16)


def _swa_bias(tq):
    slopes = 2.0 ** (-8.0 * np.arange(1, SWA_HEADS + 1, dtype=np.float64) / SWA_HEADS)
    dist = np.arange(tq)[:, None] + WINDOW - np.arange(WINDOW + tq)[None, :]
    valid = (dist >= 0) & (dist <= WINDOW)
    return np.stack([np.where(valid, -s * dist, NEG_BIG) for s in slopes]).astype(np.float32)


def _swa_prompt_kernel(sink_ref, q_ref, ktp_ref, kto_ref, vp_ref, vo_ref, bias_ref, hm_ref, o_ref, *, nb):
    w = WINDOW
    key = lax.broadcasted_iota(jnp.int32, (w, 2 * w), 1)
    has_prev = (pl.program_id(0) > 0) | (key >= w)
    heads = [(g, kh) for g in range(SWA_GROUP) for kh in range(SWA_KV_HEADS)]
    kts, vvs = [], []
    for b in range(nb):
        kt_prev = ktp_ref[...] if b == 0 else kto_ref[:, (b - 1) * w:b * w]
        v_prev = vp_ref[...] if b == 0 else vo_ref[(b - 1) * w:b * w, :]
        kts.append(jnp.concatenate([kt_prev, kto_ref[:, b * w:(b + 1) * w]], axis=1).astype(BF16))
        vvs.append(jnp.concatenate([v_prev, vo_ref[b * w:(b + 1) * w, :]], axis=0).astype(BF16))
    scores = [[jnp.dot(q_ref[b * w:(b + 1) * w, g * KV_WIDTH:(g + 1) * KV_WIDTH] * hm_ref[kh].astype(BF16),
                       kts[b], preferred_element_type=F32) for g, kh in heads] for b in range(nb)]
    for b in range(nb):
        probs = []
        for (g, kh), s in zip(heads, scores[b]):
            h = kh * SWA_GROUP + g
            s = s + bias_ref[h]
            if b == 0:
                s = jnp.where(has_prev, s, NEG_BIG)
            probs.append(_softmax_with_sink(s, sink_ref[h]))
        outs = [jnp.dot(p, vvs[b], preferred_element_type=F32) for p in probs]
        for g in range(SWA_GROUP):
            acc = None
            for (cg, kh), o in zip(heads, outs):
                if cg == g:
                    t = o * hm_ref[kh]
                    acc = t if acc is None else acc + t
            o_ref[b * w:(b + 1) * w, g * KV_WIDTH:(g + 1) * KV_WIDTH] = acc.astype(BF16)


def _swa_prompt(q, kt, v, sinks, *, n_rows, nb):
    w = WINDOW
    step = nb * w
    assert n_rows % step == 0
    prev = lambda j, sink: jnp.maximum(j * nb - 1, 0)
    return pl.pallas_call(
        functools.partial(_swa_prompt_kernel, nb=nb),
        grid_spec=pltpu.PrefetchScalarGridSpec(
            num_scalar_prefetch=1,
            grid=(n_rows // step,),
            in_specs=[pl.BlockSpec((step, MAIN_WIDTH), lambda j, sink: (j, 0)),
                      pl.BlockSpec((KV_WIDTH, w), lambda j, sink: (0, prev(j, sink))),
                      pl.BlockSpec((KV_WIDTH, step), lambda j, sink: (0, j)),
                      pl.BlockSpec((w, KV_WIDTH), lambda j, sink: (prev(j, sink), 0)),
                      pl.BlockSpec((step, KV_WIDTH), lambda j, sink: (j, 0)),
                      pl.BlockSpec((SWA_HEADS, w, 2 * w), lambda j, sink: (0, 0, 0)),
                      pl.BlockSpec((SWA_KV_HEADS, 1, KV_WIDTH), lambda j, sink: (0, 0, 0))],
            out_specs=pl.BlockSpec((step, MAIN_WIDTH), lambda j, sink: (j, 0)),
        ),
        out_shape=jax.ShapeDtypeStruct((n_rows, MAIN_WIDTH), BF16),
        compiler_params=_params(("arbitrary",)),
        name="swa_prompt",
    )(sinks.astype(F32), q, kt, kt, v, v, jnp.asarray(_swa_bias(w)), _head_masks(SWA_KV_HEADS))


def _swa_sample_kernel(q_ref, kp_ref, ko_ref, vp_ref, vo_ref, bias_ref, sink_ref, hm_ref, o_ref, *, nb, tq):
    w = WINDOW
    heads = [(kh, g) for kh in range(SWA_KV_HEADS) for g in range(SWA_GROUP)]
    kks, vvs, scores = [], [], []
    q = q_ref[...].astype(F32)
    for i in range(nb):
        win = slice(i * KV_WIDTH, (i + 1) * KV_WIDTH)
        kks.append(jnp.concatenate([kp_ref[win, :].T, ko_ref[i * tq:(i + 1) * tq, :]], axis=0))
        vvs.append(jnp.concatenate([vp_ref[win, :].T, vo_ref[i * tq:(i + 1) * tq, :]], axis=0))
        qs = jnp.concatenate([q[i * tq:(i + 1) * tq, g * KV_WIDTH:(g + 1) * KV_WIDTH] * hm_ref[kh]
                              for kh, g in heads], axis=0)
        scores.append(_bdot_nt(qs, kks[i]))
    probs = [_softmax_with_sink(s + bias_ref[...], sink_ref[...]) for s in scores]
    outs = [_bdot(p, vv) for p, vv in zip(probs, vvs)]
    for g in range(SWA_GROUP):
        rows = []
        for i in range(nb):
            acc = None
            for r, (kh, hg) in enumerate(heads):
                if hg == g:
                    t = outs[i][r * tq:(r + 1) * tq] * hm_ref[kh]
                    acc = t if acc is None else acc + t
            rows.append(acc)
        o_ref[:, g * KV_WIDTH:(g + 1) * KV_WIDTH] = jnp.concatenate(rows, axis=0).astype(BF16)


def _swa_sample(q, k_win, v_win, k, v, sinks, *, row_off, batch, tq, nb):
    w = WINDOW
    assert batch % nb == 0 and row_off % (nb * tq) == 0
    off = row_off // (nb * tq)
    bias = jnp.asarray(_swa_bias(tq).reshape(SWA_HEADS * tq, w + tq))
    sink_col = jnp.repeat(sinks.astype(F32), tq).reshape(SWA_HEADS * tq, 1)
    own = lambda width: pl.BlockSpec((nb * tq, width), lambda b: (off + b, 0))
    win = pl.BlockSpec((nb * KV_WIDTH, w), lambda b: (b, 0))
    return pl.pallas_call(
        functools.partial(_swa_sample_kernel, nb=nb, tq=tq),
        grid=(batch // nb,),
        in_specs=[own(MAIN_WIDTH), win, own(KV_WIDTH), win, own(KV_WIDTH), _const(bias.shape),
                  _const(sink_col.shape), _const((SWA_KV_HEADS, 1, KV_WIDTH))],
        out_specs=pl.BlockSpec((nb * tq, MAIN_WIDTH), lambda b: (b, 0)),
        out_shape=jax.ShapeDtypeStruct((batch * tq, MAIN_WIDTH), BF16),
        compiler_params=_params(("arbitrary",)),
        name="swa_sample",
    )(q, k_win, k, v_win, v, bias, sink_col, _head_masks(SWA_KV_HEADS))


def kernel(x_prompt, x_sample, state_gla, cache_win_k, cache_win_v, cache_mem_k, cache_mem_v, mem_prompt, norm_mix_g, norm_ffn_g, norm_mem_g, w_mem_kv, mem_qn_g, mem_kn_g, w_out, w_in_a, w_gate_lr, b_gate_lr, gla_norm_g, w_in_b, swa_qn_g, swa_sinks, norm_kv_g, w_kv, swa_kn_g, w_router_group, b_router_group, w_router_expert, b_router_expert, w_exp_gate, w_exp_up, w_exp_down):
    bp, tp, d = x_prompt.shape
    bs, ts, _ = x_sample.shape
    assert bp == 1 and tp % WINDOW == 0 and ts * (GLA_CHUNK // ts) == GLA_CHUNK
    n_p, n_s = bp * tp, bs * ts
    w_buf = cache_win_k.shape[1]
    assert w_buf == WINDOW
    x_p, x_s = x_prompt.reshape(n_p, d), x_sample.reshape(n_s, d)

    mem_k_p, mem_v_p = _mem_kv(mem_prompt, norm_mem_g, w_mem_kv, mem_kn_g)
    feature_major = lambda c: jnp.moveaxis(c, -3, -1).reshape(*c.shape[:-3], c.shape[-2] * c.shape[-1], c.shape[-3])
    cmk, cmv = feature_major(cache_mem_k), feature_major(cache_mem_v)

    def mem_attend(mq, l):
        tm_p = _row_tile(tp, 1024)
        mo_p = _mem_attn(mq, mem_k_p, mem_v_p, mem_qn_g[l], row_off=0, seq=tp, tm=tm_p, bb=1, layer=l)
        mo_s = _mem_attn(mq, cmk, cmv, mem_qn_g[l], row_off=n_p, seq=ts, tm=ts, bb=16, layer=l)
        return mo_p, mo_s

    tm = _row_tile(math.gcd(n_p, n_s), MOE_TILE)
    router = lambda l: (norm_ffn_g[l], w_router_group[l], b_router_group[l], w_router_expert[l],
                        b_router_expert[l])

    def moe(h, routed, l, split_rows=None):
        return _moe(h, routed, norm_ffn_g[l], w_exp_gate, w_exp_up, w_exp_down, l, tm, split_rows)

    q, k, la, v, og, mq = _inproj_a(x_p, x_s, norm_mix_g[0], w_in_a[0], w_gate_lr[0], b_gate_lr[0])
    zero_state = jnp.zeros((bp, GLA_HEADS, GLA_DK, GLA_DV), F32)
    n_sub = max(1, min(8, tp // GLA_CHUNK))
    main_p, gla_p = _gla(q, k, la, v, og, zero_state, gla_norm_g[0], row_off=0, seq=tp, n_seg=1, n_sub=n_sub)
    main_s, gla_s = _gla(q, k, la, v, og, state_gla[0], gla_norm_g[0], row_off=n_p, seq=ts,
                         n_seg=GLA_CHUNK // ts, n_sub=1)
    mo_p, mo_s = mem_attend(mq, 0)
    w_o = w_out[0]
    h, routed = _outproj((x_p, x_s), main_p, mo_p, main_s, mo_s, w_o[:MAIN_WIDTH].reshape(GLA_HEADS, GLA_DV, d),
                         w_o[MAIN_WIDTH:], router(0), tm)
    h = moe(h, routed, 0)

    q, mq, k_sh, v_sh, kt_sh = _inproj_b(h, norm_kv_g, norm_mix_g[1], w_kv, w_in_b[0], swa_kn_g, swa_qn_g[0])
    ck = feature_major(cache_win_k).reshape(bs * KV_WIDTH, w_buf)
    cv = feature_major(cache_win_v).reshape(bs * KV_WIDTH, w_buf)
    main_p = _swa_prompt(q, kt_sh, v_sh, swa_sinks[0], n_rows=n_p, nb=8)
    main_s = _swa_sample(q, ck, cv, k_sh, v_sh, swa_sinks[0], row_off=n_p, batch=bs, tq=ts, nb=16)
    mo_p, mo_s = mem_attend(mq, 1)
    w_o = w_out[1]
    h, routed = _outproj(h, main_p, mo_p, main_s, mo_s, w_o[:MAIN_WIDTH][_swa_perm()], w_o[MAIN_WIDTH:],
                         router(1), tm)
    y_p, y_s = moe(h, routed, 1, split_rows=n_p)

    y_prompt = y_p.reshape(bp, tp, d)
    y_sample = y_s.reshape(bs, ts, d)
    k_new = k_sh[n_p:].reshape(bs, ts, SWA_KV_HEADS, HEAD_DIM)
    v_new = v_sh[n_p:].reshape(bs, ts, SWA_KV_HEADS, HEAD_DIM)
    win_k_s = jnp.concatenate([cache_win_k, k_new], axis=1)[:, -w_buf:]
    win_v_s = jnp.concatenate([cache_win_v, v_new], axis=1)[:, -w_buf:]
    win_k_p = k_sh[n_p - WINDOW:n_p].reshape(bp, WINDOW, SWA_KV_HEADS, HEAD_DIM)
    win_v_p = v_sh[n_p - WINDOW:n_p].reshape(bp, WINDOW, SWA_KV_HEADS, HEAD_DIM)
    token_major = lambda c: jnp.moveaxis(c.reshape(*c.shape[:-2], MEM_HEADS, HEAD_DIM, c.shape[-1]), -1, -3)
    return (y_prompt, y_sample, gla_p[None], gla_s[None], win_k_p, win_v_p, win_k_s, win_v_s,
            token_major(mem_k_p), token_major(mem_v_p))
```

```python
import functools
import math

import numpy as np
import jax
import jax.numpy as jnp
from jax import lax
from jax.experimental import pallas as pl
from jax.experimental.pallas import tpu as pltpu

F32 = jnp.float32
BF16 = jnp.bfloat16

D_MODEL = 1024
MEM_HEADS = 4
HEAD_DIM = 64
MEM_WIDTH = MEM_HEADS * HEAD_DIM
MAIN_WIDTH = D_MODEL - MEM_WIDTH
GLA_HEADS = 4
GLA_DV = MAIN_WIDTH // GLA_HEADS
GLA_DK = GLA_DV // 2
GLA_DK_PAD = 128
GLA_KEY_WIDTH = GLA_HEADS * GLA_DK
GLA_KEY_PAD = GLA_HEADS * GLA_DK_PAD
GLA_GATE_RANK = 16
GLA_TAU = 16.0
GLA_CHUNK = 64
SWA_HEADS = MAIN_WIDTH // HEAD_DIM
SWA_KV_HEADS = 4
SWA_GROUP = SWA_HEADS // SWA_KV_HEADS
KV_WIDTH = SWA_KV_HEADS * HEAD_DIM
WINDOW = 128
N_GROUPS = 4
EXPERTS_PER_GROUP = 8
N_EXPERTS = N_GROUPS * EXPERTS_PER_GROUP
D_EXPERT = 512
EPS = 1e-6
LANES = 128
NEG_BIG = -1e30
VMEM_LIMIT = 56 * 1024 * 1024
MOE_TILE = 512
MOE_SUB = 128
ROUTER_LANE0 = N_GROUPS
ROUTER_META_ROWS = 8
ROUTER_ROWS = 40
SLABS = D_MODEL // LANES
PACKED_SLABS = SLABS // 2


def _bdot(a, b):
    return jnp.dot(a.astype(BF16), b.astype(BF16), preferred_element_type=F32)


def _bdot_nt(a, b):
    return lax.dot_general(a.astype(BF16), b.astype(BF16), (((1,), (1,)), ((), ())),
                           preferred_element_type=F32)


def _bdot_tn(a, b):
    return lax.dot_general(a.astype(BF16), b.astype(BF16), (((0,), (0,)), ((), ())),
                           preferred_element_type=F32)


def _split(x, n):
    parts = []
    for _ in range(n - 1):
        p = x.astype(BF16)
        parts.append(p)
        x = x - p.astype(F32)
    parts.append(x.astype(BF16))
    return parts


def _exact_left_dot(m, x, n=2):
    out = None
    for p in _split(x, n):
        t = jnp.dot(m, p, preferred_element_type=F32)
        out = t if out is None else out + t
    return out


def _seg_mean(x2, bd):
    out = None
    for p in _split(x2, 2):
        t = jnp.dot(p, bd, preferred_element_type=F32)
        out = t if out is None else out + t
    return out


def _rms_scale(x):
    return lax.rsqrt(jnp.mean(x * x, axis=-1, keepdims=True) + EPS)


def _row_tile(n, cap=512):
    t = cap
    while t > 8 and n % t:
        t //= 2
    assert n % t == 0, n
    return t


def _params(sem):
    return pltpu.CompilerParams(dimension_semantics=sem, vmem_limit_bytes=VMEM_LIMIT)


def _const(shape):
    nd = len(shape)
    return pl.BlockSpec(shape, lambda *_: (0,) * nd)


def _group_specs(tm, width, prompt_tiles, lead=None):
    p_idx = lambda i, *_: jnp.minimum(i, prompt_tiles - 1)
    s_idx = lambda i, *_: jnp.maximum(i - prompt_tiles, 0)
    if lead is None:
        return [pl.BlockSpec((tm, width), lambda i, *_, f=f: (f(i), 0)) for f in (p_idx, s_idx)]
    return [pl.BlockSpec((lead, tm, width), lambda i, *_, f=f: (0, f(i), 0)) for f in (p_idx, s_idx)]


def _block_diag_mean(width):
    i = np.arange(width)
    return jnp.asarray((i[:, None] // HEAD_DIM == i[None, :] // HEAD_DIM) / HEAD_DIM, BF16)


def _head_masks(n_heads):
    i = np.arange(n_heads * HEAD_DIM)
    return jnp.asarray((i[None, :] // HEAD_DIM == np.arange(n_heads)[:, None]), F32)[:, None, :]


def _mem_kv_kernel(mem_ref, g_ref, w_ref, kng_ref, bd_ref, k_ref, v_ref):
    x = mem_ref[0]
    hn = x * _rms_scale(x) * g_ref[0]
    kv = _bdot(hn, w_ref[0])
    k = kv[:, :MEM_WIDTH]
    k = k * lax.rsqrt(_seg_mean(k * k, bd_ref[...]) + EPS) * kng_ref[0]
    k_ref[0, 0] = k.T
    v_ref[0, 0] = kv[:, MEM_WIDTH:].T


def _mem_kv(mem, g, w, kng):
    depth, (b, m, d) = w.shape[0], mem.shape
    out = jax.ShapeDtypeStruct((depth, b, m, MEM_WIDTH), F32)
    blk = pl.BlockSpec((1, 1, m, MEM_WIDTH), lambda l, i: (l, i, 0, 0))
    return pl.pallas_call(
        _mem_kv_kernel,
        grid=(depth, b),
        in_specs=[pl.BlockSpec((1, m, d), lambda l, i: (i, 0, 0)),
                  pl.BlockSpec((1, 1, d), lambda l, i: (l, 0, 0)),
                  pl.BlockSpec((1, d, 2 * MEM_WIDTH), lambda l, i: (l, 0, 0)),
                  pl.BlockSpec((1, 1, MEM_WIDTH), lambda l, i: (l, 0, 0)),
                  _const((MEM_WIDTH, MEM_WIDTH))],
        out_specs=[blk, blk],
        out_shape=[out, out],
        compiler_params=_params(("arbitrary", "arbitrary")),
        name="mem_kv",
    )(mem, g.reshape(depth, 1, d), w.astype(BF16),
      jnp.tile(kng, (1, MEM_HEADS)).reshape(depth, 1, MEM_WIDTH), _block_diag_mean(MEM_WIDTH))


def _inproj_a_kernel(xp_ref, xs_ref, g_ref, wq_ref, wk_ref, wv_ref, wog_ref, wlr_ref, wmq_ref, wgl_ref, bgl_ref,
                     q_ref, k_ref, la_ref, v_ref, og_ref, mq_ref, *, prompt_tiles):
    x = jnp.where(pl.program_id(0) < prompt_tiles, xp_ref[...], xs_ref[...])
    hn = (x * _rms_scale(x) * g_ref[...]).astype(BF16)
    q_ref[...] = jnp.dot(hn, wq_ref[...], preferred_element_type=F32) * (GLA_DK ** -0.5)
    k_ref[...] = jnp.dot(hn, wk_ref[...], preferred_element_type=F32)
    for h in range(GLA_HEADS):
        v_ref[h] = jnp.dot(hn, wv_ref[h], preferred_element_type=F32).astype(BF16)
        og_ref[h] = jnp.dot(hn, wog_ref[h], preferred_element_type=F32)
    lr = jnp.dot(hn, wlr_ref[...], preferred_element_type=F32)
    z = _bdot(lr, wgl_ref[...]) + bgl_ref[...]
    la_ref[...] = (jnp.minimum(z, 0.0) - jnp.log(1.0 + jnp.exp(-jnp.abs(z)))) * (1.0 / GLA_TAU)
    mq_ref[...] = jnp.dot(hn, wmq_ref[...], preferred_element_type=F32)


def _pad_heads(w, width, pad):
    lead = w.shape[:-1]
    w = w.reshape(*lead, GLA_HEADS, width)
    w = jnp.pad(w, [(0, 0)] * len(lead) + [(0, 0), (0, pad - width)])
    return w.reshape(*lead, GLA_HEADS * pad)


def _inproj_a(x_p, x_s, g, w_in, w_lr, b_lr):
    (n_p, d), n_s = x_p.shape, x_s.shape[0]
    n = n_p + n_s
    tm = _row_tile(n_s)
    assert n_p % tm == 0
    pt = n_p // tm
    c0, c1, c2, c3, c4 = (GLA_KEY_WIDTH, 2 * GLA_KEY_WIDTH, 2 * GLA_KEY_WIDTH + MAIN_WIDTH,
                          2 * GLA_KEY_WIDTH + 2 * MAIN_WIDTH,
                          2 * GLA_KEY_WIDTH + 2 * MAIN_WIDTH + GLA_GATE_RANK)
    wb = w_in.astype(BF16)
    wq = _pad_heads(wb[:, :c0], GLA_DK, GLA_DK_PAD)
    wk = _pad_heads(wb[:, c0:c1], GLA_DK, GLA_DK_PAD)
    wv = wb[:, c1:c2].reshape(d, GLA_HEADS, GLA_DV).transpose(1, 0, 2)
    wog = wb[:, c2:c3].reshape(d, GLA_HEADS, GLA_DV).transpose(1, 0, 2)
    wlr = jnp.pad(wb[:, c3:c4], ((0, 0), (0, LANES - GLA_GATE_RANK)))
    wmq = wb[:, c4:]
    wgl = jnp.pad(_pad_heads(w_lr.astype(BF16), GLA_DK, GLA_DK_PAD), ((0, LANES - GLA_GATE_RANK), (0, 0)))
    bgl = _pad_heads(b_lr.reshape(1, -1), GLA_DK, GLA_DK_PAD)
    row = lambda w: pl.BlockSpec((tm, w), lambda i: (i, 0))
    hrow = pl.BlockSpec((GLA_HEADS, tm, GLA_DV), lambda i: (0, i, 0))
    key = jax.ShapeDtypeStruct((n, GLA_KEY_PAD), F32)
    val = jax.ShapeDtypeStruct((GLA_HEADS, n, GLA_DV), F32)
    return pl.pallas_call(
        functools.partial(_inproj_a_kernel, prompt_tiles=pt),
        grid=(n // tm,),
        in_specs=_group_specs(tm, d, pt) + [
            _const((1, d)), _const(wq.shape), _const(wk.shape), _const(wv.shape),
            _const(wog.shape), _const(wlr.shape), _const(wmq.shape), _const(wgl.shape),
            _const(bgl.shape)],
        out_specs=[row(GLA_KEY_PAD), row(GLA_KEY_PAD), row(GLA_KEY_PAD), hrow, hrow, row(MEM_WIDTH)],
        out_shape=[key, key, key, jax.ShapeDtypeStruct(val.shape, BF16), val,
                   jax.ShapeDtypeStruct((n, MEM_WIDTH), F32)],
        compiler_params=_params(("parallel",)),
        name="inproj_a",
    )(x_p, x_s, g.reshape(1, d), wq, wk, wv, wog, wlr, wmq, wgl, bgl)


def _gla_kernel(q_ref, k_ref, la_ref, v_ref, og_ref, s0_ref, gn_ref, mcum_ref, mall_ref, sel_ref,
                o_ref, sout_ref, s_ref, *, chunk, n_sub, n_seg):
    j = pl.program_id(1)
    seg = chunk // n_seg

    @pl.when(j == 0)
    def _():
        s_ref[...] = jnp.zeros_like(s_ref)
        s_ref[:, :, :GLA_DK, :] = s0_ref[...]

    mcum = mcum_ref[...]
    causal = mcum.astype(F32) > 0.0
    row = lax.broadcasted_iota(jnp.int32, (chunk, GLA_DK_PAD), 0)
    gn = gn_ref[...]
    hcols = [slice(h * GLA_DK_PAD, (h + 1) * GLA_DK_PAD) for h in range(GLA_HEADS)]
    crows = [slice(c * chunk, (c + 1) * chunk) for c in range(n_sub)]
    qts, kts, kds, e_ends = [], [], [], []
    for rows in crows:
        la = la_ref[rows, :]
        b = _exact_left_dot(mcum, la)
        if n_seg == 1:
            b_end = b[chunk - 1:chunk, :]
            e_ends.append(jnp.broadcast_to(jnp.exp(b_end), (LANES, b.shape[1])).T)
        else:
            b_end = _exact_left_dot(mall_ref[...], la)
            e_ends.append(jnp.exp(_exact_left_dot(sel_ref[...], la)).T)
        k = k_ref[rows, :]
        qts.append(q_ref[rows, :] * jnp.exp(b))
        kts.append((k * jnp.exp(-b)).astype(BF16))
        kds.append(k * jnp.exp(b_end - b))
    vbs = [[v_ref[h, rows, :].astype(BF16) for h in range(GLA_HEADS)] for rows in crows]
    scores = [[_bdot_nt(qts[c][:, cols], kts[c][:, cols]) for cols in hcols] for c in range(n_sub)]
    kvs = []
    for c in range(n_sub):
        per_head = []
        for h, cols in enumerate(hcols):
            per_seg = []
            for s in range(n_seg):
                kd = kds[c][:, cols]
                if n_seg > 1:
                    kd = jnp.where((row >= s * seg) & (row < (s + 1) * seg), kd, 0.0)
                per_seg.append(_bdot_tn(kd, vbs[c][h]))
            per_head.append(per_seg)
        kvs.append(per_head)
    state = [[s_ref[s, h] for s in range(n_seg)] for h in range(GLA_HEADS)]
    inters = []
    for c in range(n_sub):
        per_head = []
        for h, cols in enumerate(hcols):
            parts = []
            for s in range(n_seg):
                parts.append(_bdot(qts[c][s * seg:(s + 1) * seg, cols], state[h][s]))
                state[h][s] = e_ends[c][cols, s:s + 1] * state[h][s] + kvs[c][h][s]
            per_head.append(parts[0] if n_seg == 1 else jnp.concatenate(parts, axis=0))
        inters.append(per_head)
    for h in range(GLA_HEADS):
        for s in range(n_seg):
            s_ref[s, h] = state[h][s]
    for c, rows in enumerate(crows):
        for h in range(GLA_HEADS):
            a = jnp.where(causal, scores[c][h], 0.0)
            o = _bdot(a, vbs[c][h]) + inters[c][h]
            on = o * lax.rsqrt(jnp.mean(o * o, axis=-1, keepdims=True) + EPS) * gn
            og = og_ref[h, rows, :]
            o_ref[h, rows, :] = (on * (og * jax.nn.sigmoid(og))).astype(BF16)

    @pl.when(j == pl.num_programs(1) - 1)
    def _():
        sout_ref[...] = s_ref[:, :, :GLA_DK, :]


def _gla(q, k, la, v, og, s0, gnorm, *, row_off, seq, n_seg, n_sub):
    batch = s0.shape[0]
    chunk = GLA_CHUNK
    assert chunk % n_seg == 0 and batch % n_seg == 0
    seg = chunk // n_seg
    step_rows = n_sub * chunk
    if n_seg > 1:
        assert seq == seg and n_sub == 1
        t_steps = 1
    else:
        assert seq % step_rows == 0
        t_steps = seq // step_rows
    assert row_off % step_rows == 0
    off = row_off // step_rows
    i = np.arange(chunk)
    same = (i[:, None] // seg) == (i[None, :] // seg)
    mcum = jnp.asarray(same & (i[None, :] <= i[:, None]), BF16)
    mall = jnp.asarray(same, BF16)
    sel = jnp.asarray((i[None, :] // seg) == np.arange(LANES)[:, None], BF16)
    ridx = lambda g, j: (off + g * t_steps + j, 0)
    hidx = lambda g, j: (0, off + g * t_steps + j, 0)
    key_spec = pl.BlockSpec((step_rows, GLA_KEY_PAD), ridx)
    val_spec = pl.BlockSpec((GLA_HEADS, step_rows, GLA_DV), hidx)
    st_spec = pl.BlockSpec((n_seg, GLA_HEADS, GLA_DK, GLA_DV), lambda g, j: (g, 0, 0, 0))
    in_specs = [key_spec, key_spec, key_spec, val_spec, val_spec, st_spec, _const((1, GLA_DV)),
                _const((chunk, chunk)), _const((chunk, chunk)), _const((LANES, chunk))]
    args = [q, k, la, v, og, s0, gnorm.reshape(1, GLA_DV), mcum, mall, sel]
    out_spec = pl.BlockSpec((GLA_HEADS, step_rows, GLA_DV), lambda g, j: (0, g * t_steps + j, 0))
    return pl.pallas_call(
        functools.partial(_gla_kernel, chunk=chunk, n_sub=n_sub, n_seg=n_seg),
        grid=(batch // n_seg, t_steps),
        in_specs=in_specs,
        out_specs=[out_spec, st_spec],
        out_shape=[jax.ShapeDtypeStruct((GLA_HEADS, batch * seq, GLA_DV), BF16),
                   jax.ShapeDtypeStruct(s0.shape, F32)],
        scratch_shapes=[pltpu.VMEM((n_seg, GLA_HEADS, GLA_DK_PAD, GLA_DV), F32)],
        compiler_params=_params(("arbitrary", "arbitrary")),
        name="gla",
    )(*args)


def _mem_attn_kernel(q_ref, k_ref, v_ref, g_ref, bd_ref, hm_ref, o_ref, *, tm, bb):
    g = g_ref[...]
    sub = min(tm, 128)
    units = [(i, i * tm + r) for i in range(bb) for r in range(0, tm, sub)]
    scores = []
    for i, r in units:
        q = q_ref[r:r + sub, :]
        qn = q * lax.rsqrt(_seg_mean(q * q, bd_ref[...]) + EPS) * g
        qs = jnp.concatenate([(qn * hm_ref[h]).astype(BF16) for h in range(MEM_HEADS)], axis=0)
        scores.append(_bdot(qs, k_ref[i]))
    probs = []
    for s in scores:
        e = jnp.exp(s - jnp.max(s, axis=-1, keepdims=True))
        probs.append(e * (1.0 / jnp.sum(e, axis=-1, keepdims=True)))
    outs = [_bdot_nt(p, v_ref[i]) for (i, _), p in zip(units, probs)]
    rows = []
    for o in outs:
        acc = o[:sub] * hm_ref[0]
        for h in range(1, MEM_HEADS):
            acc = acc + o[h * sub:(h + 1) * sub] * hm_ref[h]
        rows.append(acc)
    o_ref[...] = jnp.concatenate(rows, axis=0).astype(BF16)


def _mem_attn(mq, mk, mv, qng, *, row_off, seq, tm, bb, layer):
    depth, batch, m, _ = mk.shape
    mk = mk.reshape(depth * batch, m, MEM_WIDTH)
    mv = mv.reshape(depth * batch, m, MEM_WIDTH)
    kv_off = layer * batch // bb
    assert seq % tm == 0 and batch % bb == 0 and (bb == 1 or seq == tm)
    t_steps = seq // tm
    step_rows = bb * tm
    assert row_off % step_rows == 0
    off = row_off // step_rows
    row_spec = pl.BlockSpec((step_rows, MEM_WIDTH), lambda g, j: (off + g * t_steps + j, 0))
    kv_spec = pl.BlockSpec((bb, m, MEM_WIDTH), lambda g, j: (kv_off + g, 0, 0))
    in_specs = [row_spec, kv_spec, kv_spec, _const((1, MEM_WIDTH)), _const((MEM_WIDTH, MEM_WIDTH)),
                _const((MEM_HEADS, 1, MEM_WIDTH))]
    args = [mq, mk, mv, (jnp.tile(qng, MEM_HEADS) * HEAD_DIM ** -0.5).reshape(1, MEM_WIDTH),
            _block_diag_mean(MEM_WIDTH), _head_masks(MEM_HEADS)]
    return pl.pallas_call(
        functools.partial(_mem_attn_kernel, tm=tm, bb=bb),
        grid=(batch // bb, t_steps),
        in_specs=in_specs,
        out_specs=pl.BlockSpec((step_rows, MEM_WIDTH), lambda g, j: (g * t_steps + j, 0)),
        out_shape=jax.ShapeDtypeStruct((batch * seq, MEM_WIDTH), BF16),
        compiler_params=_params(("parallel", "parallel")),
        name="mem_attn",
    )(*args)


def _outproj_kernel(*refs, heads, prompt_tiles, split_residual):
    n_h = 2 if split_residual else 1
    h_refs, refs = refs[:n_h], refs[n_h:]
    (main_p_ref, main_s_ref, mo_p_ref, mo_s_ref, wmain_ref, wmo_ref), refs = refs[:6], refs[6:]
    route_in, (o_ref, *route_out) = refs[:6], refs[6:]
    is_prompt = pl.program_id(0) < prompt_tiles
    pick = lambda p, s: jnp.where(is_prompt, p, s)
    tm = o_ref.shape[0]
    sub = min(tm, 128)
    blocks = []
    for r in range(0, tm, sub):
        rows = slice(r, r + sub)
        acc = pick(h_refs[0][rows, :], h_refs[1][rows, :]) if split_residual else h_refs[0][rows, :]
        acc = acc + _bdot(pick(mo_p_ref[rows, :], mo_s_ref[rows, :]), wmo_ref[...])
        if heads:
            for h in range(heads):
                acc = acc + _bdot(pick(main_p_ref[h, rows, :], main_s_ref[h, rows, :]), wmain_ref[h])
        else:
            acc = acc + _bdot(pick(main_p_ref[rows, :], main_s_ref[rows, :]), wmain_ref[...])
        o_ref[rows, :] = acc
        blocks.append(acc)
    _route_tile(blocks, *route_in, *route_out)


def _outproj(h, main_p, mo_p, main_s, mo_s, w_main, w_mo, router, tm):
    n_p, n_s = mo_p.shape[0], mo_s.shape[0]
    n, d = n_p + n_s, w_mo.shape[1]
    assert n_p % tm == 0 and n_s % tm == 0
    pt = n_p // tm
    heads = main_p.shape[0] if main_p.ndim == 3 else 0
    row = pl.BlockSpec((tm, d), lambda i: (i, 0))
    split = isinstance(h, tuple)
    h_specs, h_args = (_group_specs(tm, d, pt), list(h)) if split else ([row], [h])
    main_specs = _group_specs(tm, main_p.shape[-1], pt, lead=heads or None)
    r_args, r_in_specs, r_out_specs, r_out_shape, r_scratch = _route_operands(*router, n, tm)
    h_new, *routed = pl.pallas_call(
        functools.partial(_outproj_kernel, heads=heads, prompt_tiles=pt, split_residual=split),
        grid=(n // tm,),
        in_specs=(h_specs + main_specs + _group_specs(tm, MEM_WIDTH, pt)
                  + [_const(w_main.shape), _const(w_mo.shape)] + r_in_specs),
        out_specs=[row] + r_out_specs,
        out_shape=[jax.ShapeDtypeStruct((n, d), F32)] + r_out_shape,
        scratch_shapes=[r_scratch],
        compiler_params=_params(("arbitrary",)),
        name="outproj_route",
    )(*h_args, main_p, main_s, mo_p, mo_s, w_main.astype(BF16), w_mo.astype(BF16), *r_args)
    return h_new, routed


def _route_tile(x_blocks, g_ref, whi_ref, wlo_ref, b_ref, before_tok_ref, before_row_ref,
                mi_ref, mf_ref, cnt_ref, tt_ref, carry_ref):
    @pl.when(pl.program_id(0) == 0)
    def _():
        carry_ref[...] = jnp.zeros_like(carry_ref)

    nt = lambda a, b: lax.dot_general(a, b, (((1,), (1,)), ((), ())), preferred_element_type=F32)
    logits = []
    for x in x_blocks:
        x_hi, x_lo = _split(x * _rms_scale(x) * g_ref[...], 2)
        logits.append((nt(whi_ref[...], x_hi) + nt(wlo_ref[...], x_hi) + nt(whi_ref[...], x_lo))[:ROUTER_ROWS])
    logits = jnp.concatenate(logits, axis=1) + b_ref[...]
    tm = logits.shape[1]
    row = lax.broadcasted_iota(jnp.int32, (ROUTER_ROWS, tm), 0)
    far = jnp.int32(2 * LANES)

    def first_max(vals):
        m = jnp.max(vals, axis=0, keepdims=True)
        return m, jnp.min(jnp.where(vals == m, row, far), axis=0, keepdims=True)

    gl = jnp.where(row < N_GROUPS, logits, -jnp.inf)
    gmax, grp = first_max(gl)
    pg_sel = 1.0 / jnp.sum(jnp.exp(gl - gmax), axis=0, keepdims=True)
    lo = ROUTER_LANE0 + grp * EXPERTS_PER_GROUP
    el = jnp.where((row >= lo) & (row < lo + EXPERTS_PER_GROUP), logits, -jnp.inf)
    m1, i1 = first_max(el)
    m2, i2 = first_max(jnp.where(row == i1, -jnp.inf, el))
    e2 = jnp.exp(m2 - m1)
    g1 = pg_sel / (1.0 + e2)
    g2 = pg_sel * e2 / (1.0 + e2)

    oh1 = row == i1
    oh2 = row == i2
    picked = jnp.where(oh1 | oh2, 1.0, 0.0)
    earlier = jnp.dot(picked.astype(BF16), before_tok_ref[...], preferred_element_type=F32)
    cnt_col = jnp.sum(picked, axis=1, keepdims=True)
    cnt_tile = jnp.concatenate([jnp.broadcast_to(cnt_col, (ROUTER_ROWS, LANES)),
                                jnp.zeros((LANES - ROUTER_ROWS, LANES), F32)], axis=0)
    c_hi = jnp.floor(cnt_tile * (1.0 / 32.0))
    c_lo = cnt_tile - 32.0 * c_hi
    first = (32.0 * jnp.dot(before_row_ref[...], c_hi.astype(BF16), preferred_element_type=F32)
             + jnp.dot(before_row_ref[...], c_lo.astype(BF16), preferred_element_type=F32))
    local = first[:ROUTER_ROWS, 0:1] + earlier
    lpos1 = jnp.sum(jnp.where(oh1, local, 0.0), axis=0, keepdims=True)
    lpos2 = jnp.sum(jnp.where(oh2, local, 0.0), axis=0, keepdims=True)
    carry_before = carry_ref[...]
    carry = carry_before + cnt_tile
    carry_ref[...] = carry

    lane = lax.broadcasted_iota(jnp.int32, (LANES, LANES), 1)
    cols = jnp.where(lane == 0, carry_before, jnp.where(lane == 1, cnt_tile, jnp.where(lane == 2, first,
                     jnp.where(lane == 3, carry, 0.0))))
    tables = cols.T
    tt_ref[...] = tables[:8].astype(jnp.int32)
    cnt_ref[...] = tables[3:4]
    row8 = lax.broadcasted_iota(jnp.int32, (ROUTER_META_ROWS, tm), 0)
    zero8 = jnp.zeros((ROUTER_META_ROWS, tm), F32)
    mi_ref[...] = jnp.where(row8 == 0, lpos1, jnp.where(row8 == 1, lpos2, zero8)).astype(jnp.int32)
    stacked = jnp.where(row8 == 0, g1, jnp.where(row8 == 1, g2, jnp.where(row8 == 2, lpos1,
                        jnp.where(row8 == 3, lpos2, zero8))))
    mf_ref[...] = jnp.concatenate([stacked, jnp.zeros((LANES - ROUTER_META_ROWS, tm), F32)], axis=0).T


def _route_operands(g, w_rg, b_rg, w_re, b_re, n, tm):
    d = g.shape[0]
    assert n % tm == 0 and 2 * tm <= 32 * 32
    n_real = N_GROUPS + N_EXPERTS
    w = jnp.pad(jnp.concatenate([w_rg, w_re], axis=1), ((0, 0), (0, LANES - n_real))).T
    b = jnp.pad(jnp.concatenate([b_rg, b_re]), (0, ROUTER_ROWS - n_real)).reshape(ROUTER_ROWS, 1)
    w_hi = w.astype(BF16)
    w_lo = (w - w_hi.astype(F32)).astype(BF16)
    i = np.arange(tm)
    before_tok = jnp.asarray(i[:, None] < i[None, :], BF16)
    e = np.arange(LANES)
    before_row = jnp.asarray(e[None, :] < e[:, None], BF16)
    args = [g.reshape(1, d), w_hi, w_lo, b, before_tok, before_row]
    in_specs = [_const(a.shape) for a in args]
    out_specs = [pl.BlockSpec((ROUTER_META_ROWS, tm), lambda i: (0, i)), pl.BlockSpec((tm, LANES), lambda i: (i, 0)),
                 _const((1, LANES)), pl.BlockSpec((8, LANES), lambda i: (i, 0))]
    out_shape = [jax.ShapeDtypeStruct((ROUTER_META_ROWS, n), jnp.int32), jax.ShapeDtypeStruct((n, LANES), F32),
                 jax.ShapeDtypeStruct((1, LANES), F32), jax.ShapeDtypeStruct((n // tm * 8, LANES), jnp.int32)]
    return args, in_specs, out_specs, out_shape, pltpu.VMEM((LANES, LANES), F32)


def _pack_rows(ref, x, rows, lead=(), row0=0):
    u32 = jnp.uint32
    for w in range(PACKED_SLABS):
        lo = x[:, (2 * w) * LANES:(2 * w + 1) * LANES].astype(BF16).astype(F32)
        hi = x[:, (2 * w + 1) * LANES:(2 * w + 2) * LANES].astype(BF16).astype(F32)
        word = (lax.bitcast_convert_type(lo, u32) >> 16) | (lax.bitcast_convert_type(hi, u32) & u32(0xFFFF0000))
        ref[lead + (pl.ds(row0 * PACKED_SLABS + w, rows, stride=PACKED_SLABS), slice(None))] = word


def _unpack_rows(ref, rows, lead=(), row0=0):
    u32 = jnp.uint32
    slabs = []
    for w in range(PACKED_SLABS):
        word = ref[lead + (pl.ds(row0 * PACKED_SLABS + w, rows, stride=PACKED_SLABS), slice(None))]
        slabs.append(lax.bitcast_convert_type(word << 16, F32).astype(BF16))
        slabs.append(lax.bitcast_convert_type(word & u32(0xFFFF0000), F32).astype(BF16))
    return jnp.concatenate(slabs, axis=1)


RUN_FIELDS = 3
RUN_CHUNK_BITS = 6


def _copy_runs(runs_ref, tile, local_rows, global_rows, sem, *, to_global):
    ps = PACKED_SLABS
    base = tile * (RUN_FIELDS * N_EXPERTS)

    def piece(g0, l0, off, size):
        g = global_rows(pl.multiple_of((g0 + off) * ps, ps), size * ps)
        l = local_rows(pl.multiple_of((l0 + off) * ps, ps), size * ps)
        src, dst = (l, g) if to_global else (g, l)
        pltpu.make_async_copy(src, dst, sem).start()

    def per_expert(e, carry):
        g0 = runs_ref[base + e]
        length = runs_ref[base + N_EXPERTS + e]
        l0 = runs_ref[base + 2 * N_EXPERTS + e]
        big = 1 << RUN_CHUNK_BITS

        def big_piece(c, inner):
            piece(g0, l0, c * big, big)
            return inner

        n_big = length >> RUN_CHUNK_BITS
        lax.fori_loop(0, n_big, big_piece, 0)
        off = n_big * big
        for bit in reversed(range(RUN_CHUNK_BITS)):
            size = 1 << bit

            @pl.when((length & size) != 0)
            def _(off=off, size=size):
                piece(g0, l0, off, size)

            off = off + (length & size)
        return carry

    lax.fori_loop(0, N_EXPERTS, per_expert, 0)


def _fill_pads(pads_ref, buf0, xs_hbm, sem, *, wait):
    ps = PACKED_SLABS

    def go(copy):
        copy.wait() if wait else copy.start()

    def per_tail_tile(t, carry):
        rows = MOE_TILE * ps
        go(pltpu.make_async_copy(
            buf0.at[pl.ds(0, rows)],
            xs_hbm.at[pl.ds(pl.multiple_of((pads_ref[2 * N_EXPERTS] + t) * rows, rows), rows)], sem))
        return carry

    lax.fori_loop(0, pads_ref[2 * N_EXPERTS + 1], per_tail_tile, 0)

    def per_expert(e, carry):
        first, length = pads_ref[e], pads_ref[N_EXPERTS + e]
        off = 0
        for bit in reversed(range((MOE_TILE - 1).bit_length())):
            size = 1 << bit

            @pl.when((length & size) != 0)
            def _(off=off, size=size):
                go(pltpu.make_async_copy(
                    buf0.at[pl.ds(pl.multiple_of(off * ps, ps), size * ps)],
                    xs_hbm.at[pl.ds(pl.multiple_of((first + off) * ps, ps), size * ps)], sem))

            off = off + (length & size)
        return carry

    lax.fori_loop(0, N_EXPERTS, per_expert, 0)


def _dispatch_kernel(runs_ref, pads_ref, h_ref, g_ref, meta_ref, xs_hbm, buf, sem, pad_sem, *, tm, steps):
    i = pl.program_id(0)
    slot = lax.rem(i, 2)
    ns = 2 * tm
    assert ns >= MOE_TILE

    def wait_slot(sl):
        pltpu.make_async_copy(buf.at[sl], xs_hbm.at[pl.ds(0, ns * PACKED_SLABS)], sem.at[sl]).wait()

    @pl.when(i >= 2)
    def _():
        wait_slot(slot)

    x = h_ref[...]
    xn = (x * _rms_scale(x) * g_ref[...]).astype(BF16)
    j = lax.broadcasted_iota(jnp.int32, (ns, tm), 0)
    pick = jnp.where((j == meta_ref[0:1, :]) | (j == meta_ref[1:2, :]), 1.0, 0.0).astype(BF16)
    _pack_rows(buf, jnp.dot(pick, xn, preferred_element_type=F32), ns, (slot,))
    _copy_runs(runs_ref, i, lambda start, size: buf.at[slot, pl.ds(start, size)],
               lambda start, size: xs_hbm.at[pl.ds(start, size)], sem.at[slot], to_global=True)

    @pl.when(i == 0)
    def _():
        _fill_pads(pads_ref, buf.at[0], xs_hbm, pad_sem, wait=False)

    @pl.when(i == min(1, steps - 1))
    def _():
        _fill_pads(pads_ref, buf.at[0], xs_hbm, pad_sem, wait=True)

    @pl.when(i == steps - 1)
    def _():
        wait_slot(slot)
        if steps > 1:
            wait_slot(1 - slot)


def _dispatch(h, g, meta, runs, pads, tm, n_slots):
    n, d = h.shape
    steps = n // tm
    return pl.pallas_call(
        functools.partial(_dispatch_kernel, tm=tm, steps=steps),
        grid_spec=pltpu.PrefetchScalarGridSpec(
            num_scalar_prefetch=2,
            grid=(steps,),
            in_specs=[pl.BlockSpec((tm, d), lambda i, *_: (i, 0)),
                      pl.BlockSpec((1, d), lambda i, *_: (0, 0)),
                      pl.BlockSpec((ROUTER_META_ROWS, tm), lambda i, *_: (0, i))],
            out_specs=pl.BlockSpec(memory_space=pl.ANY),
            scratch_shapes=[pltpu.VMEM((2, 2 * tm * PACKED_SLABS, LANES), jnp.uint32),
                            pltpu.SemaphoreType.DMA((2,)), pltpu.SemaphoreType.DMA],
        ),
        out_shape=jax.ShapeDtypeStruct((n_slots * PACKED_SLABS, LANES), jnp.uint32),
        compiler_params=_params(("arbitrary",)),
        name="moe_dispatch",
    )(runs, pads, h, g.reshape(1, d), meta)


def _expert_kernel(tile_ref, exp_ref, new_ref, wslot_ref, next_ref, n_ref,
                   xs_hbm, wg_hbm, wu_hbm, wd_hbm, ys_hbm,
                   xbuf, ybuf, xsem, ysem, wgf, wuf, wdf, wsem, wgb, wub, wdb, *, tm, sub, layer, max_tiles):
    n = n_ref[0]
    rows = tm * PACKED_SLABS
    tile_rows = lambda hbm, t: hbm.at[pl.ds(pl.multiple_of(t * rows, rows), rows)]
    x_copy = lambda t, slot: pltpu.make_async_copy(tile_rows(xs_hbm, t), xbuf.at[slot], xsem.at[slot])
    y_copy = lambda t, slot: pltpu.make_async_copy(ybuf.at[slot], tile_rows(ys_hbm, t), ysem.at[slot])

    def weight_copies(e, slot):
        return [pltpu.make_async_copy(hbm.at[layer, e], buf.at[slot], wsem.at[slot])
                for hbm, buf in ((wg_hbm, wgf), (wu_hbm, wuf), (wd_hbm, wdf))]

    x_copy(0, 0).start()

    def item(w, carry):
        io = lax.rem(w, 2)
        x_copy(w, io).wait()

        @pl.when(w + 1 < n)
        def _():
            x_copy(w + 1, 1 - io).start()

        @pl.when(w >= 2)
        def _():
            y_copy(w - 2, io).wait()

        @pl.when(new_ref[w] != 0)
        def _():
            slot = wslot_ref[w]

            @pl.when(w == 0)
            def _():
                for c in weight_copies(exp_ref[w], slot):
                    c.start()

            for c in weight_copies(exp_ref[w], slot):
                c.wait()
            wgb[...] = wgf[slot].astype(BF16)
            wub[...] = wuf[slot].astype(BF16)
            wdb[...] = wdf[slot].astype(BF16)

            @pl.when(next_ref[w] >= 0)
            def _():
                for c in weight_copies(next_ref[w], 1 - slot):
                    c.start()

        n_blocks = tm // sub

        def up(s):
            x = _unpack_rows(xbuf, sub, (io,), row0=s * sub)
            return (jnp.dot(x, wgb[...], preferred_element_type=F32),
                    jnp.dot(x, wub[...], preferred_element_type=F32))

        ups = {0: up(0)}
        for s in range(n_blocks):
            if s + 1 < n_blocks:
                ups[s + 1] = up(s + 1)
            hg, hu = ups.pop(s)
            act = (hg * jax.nn.sigmoid(hg) * hu).astype(BF16)
            y = jnp.dot(act, wdb[...], preferred_element_type=F32)
            _pack_rows(ybuf, y, sub, (io,), row0=s * sub)
        y_copy(w, io).start()
        return carry

    lax.fori_loop(0, n, item, 0)

    @pl.when(n >= 2)
    def _():
        y_copy(n - 2, lax.rem(n, 2)).wait()
    y_copy(n - 1, lax.rem(n - 1, 2)).wait()

    for wait in (False, True):
        def spare(t, carry, wait=wait):
            copy = y_copy(n + t, 0)
            copy.wait() if wait else copy.start()
            return carry

        lax.fori_loop(0, max_tiles - n, spare, 0)


def _experts(xs, items, w_g, w_u, w_d, layer):
    d = D_MODEL
    tm = MOE_TILE
    max_tiles = items[0].shape[0]
    any_spec = pl.BlockSpec(memory_space=pl.ANY)
    io_buf = pltpu.VMEM((2, tm * PACKED_SLABS, LANES), jnp.uint32)
    return pl.pallas_call(
        functools.partial(_expert_kernel, tm=tm, sub=MOE_SUB, layer=layer, max_tiles=max_tiles),
        grid_spec=pltpu.PrefetchScalarGridSpec(
            num_scalar_prefetch=len(items),
            grid=(1,),
            in_specs=[any_spec, any_spec, any_spec, any_spec],
            out_specs=any_spec,
            scratch_shapes=[io_buf, io_buf, pltpu.SemaphoreType.DMA((2,)), pltpu.SemaphoreType.DMA((2,)),
                            pltpu.VMEM((2, d, D_EXPERT), F32), pltpu.VMEM((2, d, D_EXPERT), F32),
                            pltpu.VMEM((2, D_EXPERT, d), F32), pltpu.SemaphoreType.DMA((2,)),
                            pltpu.VMEM((d, D_EXPERT), BF16), pltpu.VMEM((d, D_EXPERT), BF16),
                            pltpu.VMEM((D_EXPERT, d), BF16)],
        ),
        out_shape=jax.ShapeDtypeStruct(xs.shape, jnp.uint32),
        compiler_params=_params(("arbitrary",)),
        name="moe_experts",
    )(*items, xs, w_g, w_u, w_d)


def _combine_kernel(runs_ref, h_ref, gate_ref, ys_hbm, *refs, tm, steps, prompt_tiles):
    out_refs, (buf, sem) = refs[:-2], refs[-2:]
    i = pl.program_id(0)
    slot = lax.rem(i, 2)
    ns = 2 * tm

    def issue(step, sl):
        _copy_runs(runs_ref, step, lambda start, size: buf.at[sl, pl.ds(start, size)],
                   lambda start, size: ys_hbm.at[pl.ds(start, size)], sem.at[sl], to_global=False)

    @pl.when(i == 0)
    def _():
        issue(0, 0)

    @pl.when(i + 1 < steps)
    def _():
        issue(i + 1, 1 - slot)

    pltpu.make_async_copy(ys_hbm.at[pl.ds(0, ns * PACKED_SLABS)], buf.at[slot], sem.at[slot]).wait()
    y = _unpack_rows(buf, ns, (slot,))
    g = gate_ref[...]
    j = lax.broadcasted_iota(jnp.int32, (tm, ns), 1)
    mix = (jnp.where(j == g[:, 2:3].astype(jnp.int32), g[:, 0:1], 0.0)
           + jnp.where(j == g[:, 3:4].astype(jnp.int32), g[:, 1:2], 0.0)).astype(BF16)
    out = h_ref[...] + jnp.dot(mix, y, preferred_element_type=F32)
    if len(out_refs) == 1:
        out_refs[0][...] = out
    else:
        @pl.when(i < prompt_tiles)
        def _():
            out_refs[0][...] = out

        @pl.when(i >= prompt_tiles)
        def _():
            out_refs[1][...] = out


def _combine(h, gates, ys, runs, tm, split_rows=None):
    n, d = h.shape
    steps = n // tm
    row = pl.BlockSpec((tm, d), lambda i, pos: (i, 0))
    if split_rows is None:
        pt, out_specs, out_shape = 0, row, jax.ShapeDtypeStruct((n, d), F32)
    else:
        assert split_rows % tm == 0
        pt = split_rows // tm
        out_specs = _group_specs(tm, d, pt)
        out_shape = [jax.ShapeDtypeStruct((split_rows, d), F32), jax.ShapeDtypeStruct((n - split_rows, d), F32)]
    return pl.pallas_call(
        functools.partial(_combine_kernel, tm=tm, steps=steps, prompt_tiles=pt),
        grid_spec=pltpu.PrefetchScalarGridSpec(
            num_scalar_prefetch=1,
            grid=(steps,),
            in_specs=[row, pl.BlockSpec((tm, LANES), lambda i, pos: (i, 0)),
                      pl.BlockSpec(memory_space=pl.ANY)],
            out_specs=out_specs,
            scratch_shapes=[pltpu.VMEM((2, 2 * tm * PACKED_SLABS, LANES), jnp.uint32),
                            pltpu.SemaphoreType.DMA((2,))],
        ),
        out_shape=out_shape,
        compiler_params=_params(("arbitrary",)),
        name="moe_combine",
    )(runs, h, gates, ys)


def _lookup(tables, idx):
    hit = idx[:, None] == jnp.arange(tables.shape[1], dtype=idx.dtype)[None, :]
    return jnp.sum(jnp.where(hit[None], tables[:, None, :], 0), axis=2)


def _work_items(tiles_e, max_items):
    item_end = jnp.cumsum(tiles_e)
    n_items = item_end[-1]
    w = jnp.minimum(jnp.arange(max_items, dtype=jnp.int32), n_items - 1)
    expert = jnp.sum(w[:, None] >= item_end[None, :], axis=1).astype(jnp.int32)
    prev_expert = jnp.concatenate([jnp.full((1,), -1, jnp.int32), expert[:-1]])
    new_expert = expert != prev_expert
    weight_slot = (jnp.cumsum(new_expert.astype(jnp.int32)) - 1) % 2
    ids = jnp.arange(N_EXPERTS, dtype=jnp.int32)
    later = (ids[None, :] > ids[:, None]) & (tiles_e[None, :] > 0)
    following = jnp.min(jnp.where(later, ids[None, :], N_EXPERTS), axis=1)
    following = jnp.where(following == N_EXPERTS, -1, following)
    next_expert, = _lookup(following[None, :], expert)
    as_i32 = lambda a: a.astype(jnp.int32)
    return (as_i32(w), expert, as_i32(new_expert), as_i32(weight_slot), as_i32(next_expert),
            as_i32(n_items).reshape(1))


def _moe(h, routed, g, w_g, w_u, w_d, layer, tm, split_rows=None):
    n, _ = h.shape
    meta, gates, cnt, tables = routed
    experts = slice(ROUTER_LANE0, ROUTER_LANE0 + N_EXPERTS)
    counts = cnt[0, experts].astype(jnp.int32)
    tiles_e = (counts + MOE_TILE - 1) // MOE_TILE
    starts = (jnp.cumsum(tiles_e) - tiles_e) * MOE_TILE
    max_tiles = -(-2 * n // MOE_TILE) + N_EXPERTS
    tables = tables.reshape(n // tm, 8, LANES)[:, :RUN_FIELDS, experts]
    runs = tables.at[:, 0, :].add(starts[None, :]).reshape(-1)
    used = jnp.sum(tiles_e)
    pads = jnp.concatenate([starts + counts, tiles_e * MOE_TILE - counts, jnp.stack([used, max_tiles - used])])
    xs = _dispatch(h, g, meta, runs, pads, tm, max_tiles * MOE_TILE)
    ys = _experts(xs, _work_items(tiles_e, max_tiles), w_g, w_u, w_d, layer)
    return _combine(h, gates, ys, runs, tm, split_rows)


def _inproj_b_kernel(x_ref, gkv_ref, gmix_ref, wkv_ref, win_ref, kng_ref, qng_ref, bdk_ref, hsum_ref, hexp_ref,
                     q_ref, mq_ref, k_ref, v_ref, kt_ref):
    x = x_ref[...]
    xr = x * _rms_scale(x)
    kv = _bdot(xr * gkv_ref[...], wkv_ref[...])
    k = kv[:, :KV_WIDTH]
    k = k * lax.rsqrt(_seg_mean(k * k, bdk_ref[...]) + EPS) * kng_ref[...]
    k_ref[...] = k
    kt_ref[...] = k.T.astype(BF16)
    v_ref[...] = kv[:, KV_WIDTH:]
    proj = _bdot(xr * gmix_ref[...], win_ref[...])
    q = proj[:, :MAIN_WIDTH]
    ms = None
    for p in _split(q * q, 2):
        t = jnp.dot(p, hsum_ref[...], preferred_element_type=F32)
        ms = t if ms is None else ms + t
    scale = None
    for p in _split(lax.rsqrt(ms + EPS), 2):
        t = jnp.dot(p, hexp_ref[...], preferred_element_type=F32)
        scale = t if scale is None else scale + t
    q_ref[...] = (q * scale * qng_ref[...]).astype(BF16)
    mq_ref[...] = proj[:, MAIN_WIDTH:]


def _swa_perm():
    g, kh, dd = np.meshgrid(np.arange(SWA_GROUP), np.arange(SWA_KV_HEADS), np.arange(HEAD_DIM), indexing="ij")
    return ((kh * SWA_GROUP + g) * HEAD_DIM + dd).reshape(-1)


def _inproj_b(x, g_kv, g_mix, w_kv, w_in, kng, qng):
    n, d = x.shape
    tm = _row_tile(n, 1024)
    perm = _swa_perm()
    w_in_p = jnp.concatenate([w_in[:, :MAIN_WIDTH][:, perm], w_in[:, MAIN_WIDTH:]], axis=1).astype(BF16)
    qng_t = (jnp.tile(qng, SWA_HEADS) * HEAD_DIM ** -0.5).reshape(1, MAIN_WIDTH)
    member = (np.arange(MAIN_WIDTH)[:, None] // HEAD_DIM == np.arange(LANES)[None, :]).astype(np.float32)
    row = lambda w: pl.BlockSpec((tm, w), lambda i: (i, 0))
    return pl.pallas_call(
        _inproj_b_kernel,
        grid=(n // tm,),
        in_specs=[row(d), _const((1, d)), _const((1, d)), _const((d, 2 * KV_WIDTH)), _const((d, d)),
                  _const((1, KV_WIDTH)), _const((1, MAIN_WIDTH)), _const((KV_WIDTH, KV_WIDTH)),
                  _const((MAIN_WIDTH, LANES)), _const((LANES, MAIN_WIDTH))],
        out_specs=[row(MAIN_WIDTH), row(MEM_WIDTH), row(KV_WIDTH), row(KV_WIDTH),
                   pl.BlockSpec((KV_WIDTH, tm), lambda i: (0, i))],
        out_shape=[jax.ShapeDtypeStruct((n, MAIN_WIDTH), BF16), jax.ShapeDtypeStruct((n, MEM_WIDTH), F32),
                   jax.ShapeDtypeStruct((n, KV_WIDTH), F32), jax.ShapeDtypeStruct((n, KV_WIDTH), F32),
                   jax.ShapeDtypeStruct((KV_WIDTH, n), BF16)],
        compiler_params=_params(("parallel",)),
        name="inproj_b",
    )(x, g_kv.reshape(1, d), g_mix.reshape(1, d), w_kv.astype(BF16), w_in_p,
      jnp.tile(kng, SWA_KV_HEADS).reshape(1, KV_WIDTH), qng_t,
      _block_diag_mean(KV_WIDTH), jnp.asarray(member / HEAD_DIM, BF16), jnp.asarray(member.T, BF16))


def _softmax_with_sink(s, sink):
    m = jnp.maximum(jnp.max(s, axis=-1, keepdims=True), sink)
    e = jnp.exp(s - m)
    r = 1.0 / (jnp.sum(e, axis=-1, keepdims=True) + jnp.exp(sink - m))
    return (e * r).astype(BF16)


def _swa_bias(tq):
    slopes = 2.0 ** (-8.0 * np.arange(1, SWA_HEADS + 1, dtype=np.float64) / SWA_HEADS)
    dist = np.arange(tq)[:, None] + WINDOW - np.arange(WINDOW + tq)[None, :]
    valid = (dist >= 0) & (dist <= WINDOW)
    return np.stack([np.where(valid, -s * dist, NEG_BIG) for s in slopes]).astype(np.float32)


def _swa_prompt_kernel(sink_ref, q_ref, ktp_ref, kto_ref, vp_ref, vo_ref, bias_ref, hm_ref, o_ref, *, nb):
    w = WINDOW
    key = lax.broadcasted_iota(jnp.int32, (w, 2 * w), 1)
    has_prev = (pl.program_id(0) > 0) | (key >= w)
    heads = [(g, kh) for g in range(SWA_GROUP) for kh in range(SWA_KV_HEADS)]
    kts, vvs = [], []
    for b in range(nb):
        kt_prev = ktp_ref[...] if b == 0 else kto_ref[:, (b - 1) * w:b * w]
        v_prev = vp_ref[...] if b == 0 else vo_ref[(b - 1) * w:b * w, :]
        kts.append(jnp.concatenate([kt_prev, kto_ref[:, b * w:(b + 1) * w]], axis=1).astype(BF16))
        vvs.append(jnp.concatenate([v_prev, vo_ref[b * w:(b + 1) * w, :]], axis=0).astype(BF16))
    scores = [[jnp.dot(q_ref[b * w:(b + 1) * w, g * KV_WIDTH:(g + 1) * KV_WIDTH] * hm_ref[kh].astype(BF16),
                       kts[b], preferred_element_type=F32) for g, kh in heads] for b in range(nb)]
    for b in range(nb):
        probs = []
        for (g, kh), s in zip(heads, scores[b]):
            h = kh * SWA_GROUP + g
            s = s + bias_ref[h]
            if b == 0:
                s = jnp.where(has_prev, s, NEG_BIG)
            probs.append(_softmax_with_sink(s, sink_ref[h]))
        outs = [jnp.dot(p, vvs[b], preferred_element_type=F32) for p in probs]
        lane_head = lax.broadcasted_iota(jnp.int32, (w, KV_WIDTH), 1) // HEAD_DIM
        for g in range(SWA_GROUP):
            acc = None
            for (cg, kh), o in zip(heads, outs):
                if cg == g:
                    acc = o if acc is None else jnp.where(lane_head == kh, o, acc)
            o_ref[b * w:(b + 1) * w, g * KV_WIDTH:(g + 1) * KV_WIDTH] = acc.astype(BF16)


def _swa_prompt(q, kt, v, sinks, *, n_rows, nb):
    w = WINDOW
    step = nb * w
    assert n_rows % step == 0
    prev = lambda j, sink: jnp.maximum(j * nb - 1, 0)
    return pl.pallas_call(
        functools.partial(_swa_prompt_kernel, nb=nb),
        grid_spec=pltpu.PrefetchScalarGridSpec(
            num_scalar_prefetch=1,
            grid=(n_rows // step,),
            in_specs=[pl.BlockSpec((step, MAIN_WIDTH), lambda j, sink: (j, 0)),
                      pl.BlockSpec((KV_WIDTH, w), lambda j, sink: (0, prev(j, sink))),
                      pl.BlockSpec((KV_WIDTH, step), lambda j, sink: (0, j)),
                      pl.BlockSpec((w, KV_WIDTH), lambda j, sink: (prev(j, sink), 0)),
                      pl.BlockSpec((step, KV_WIDTH), lambda j, sink: (j, 0)),
                      pl.BlockSpec((SWA_HEADS, w, 2 * w), lambda j, sink: (0, 0, 0)),
                      pl.BlockSpec((SWA_KV_HEADS, 1, KV_WIDTH), lambda j, sink: (0, 0, 0))],
            out_specs=pl.BlockSpec((step, MAIN_WIDTH), lambda j, sink: (j, 0)),
        ),
        out_shape=jax.ShapeDtypeStruct((n_rows, MAIN_WIDTH), BF16),
        compiler_params=_params(("arbitrary",)),
        name="swa_prompt",
    )(sinks.astype(F32), q, kt, kt, v, v, jnp.asarray(_swa_bias(w)), _head_masks(SWA_KV_HEADS))


def _swa_sample_kernel(q_ref, kp_ref, ko_ref, vp_ref, vo_ref, bias_ref, sink_ref, hm_ref, o_ref, *, nb, tq):
    w = WINDOW
    heads = [(kh, g) for kh in range(SWA_KV_HEADS) for g in range(SWA_GROUP)]
    kks, vvs, scores = [], [], []
    q = q_ref[...].astype(F32)
    for i in range(nb):
        win = slice(i * KV_WIDTH, (i + 1) * KV_WIDTH)
        kks.append(jnp.concatenate([kp_ref[win, :].T, ko_ref[i * tq:(i + 1) * tq, :]], axis=0))
        vvs.append(jnp.concatenate([vp_ref[win, :].T, vo_ref[i * tq:(i + 1) * tq, :]], axis=0))
        qs = jnp.concatenate([q[i * tq:(i + 1) * tq, g * KV_WIDTH:(g + 1) * KV_WIDTH] * hm_ref[kh]
                              for kh, g in heads], axis=0)
        scores.append(_bdot_nt(qs, kks[i]))
    probs = [_softmax_with_sink(s + bias_ref[...], sink_ref[...]) for s in scores]
    outs = [_bdot(p, vv) for p, vv in zip(probs, vvs)]
    for g in range(SWA_GROUP):
        rows = []
        for i in range(nb):
            acc = None
            for r, (kh, hg) in enumerate(heads):
                if hg == g:
                    t = outs[i][r * tq:(r + 1) * tq] * hm_ref[kh]
                    acc = t if acc is None else acc + t
            rows.append(acc)
        o_ref[:, g * KV_WIDTH:(g + 1) * KV_WIDTH] = jnp.concatenate(rows, axis=0).astype(BF16)


def _swa_sample(q, k_win, v_win, k, v, sinks, *, row_off, batch, tq, nb):
    w = WINDOW
    assert batch % nb == 0 and row_off % (nb * tq) == 0
    off = row_off // (nb * tq)
    bias = jnp.asarray(_swa_bias(tq).reshape(SWA_HEADS * tq, w + tq))
    sink_col = jnp.repeat(sinks.astype(F32), tq).reshape(SWA_HEADS * tq, 1)
    own = lambda width: pl.BlockSpec((nb * tq, width), lambda b: (off + b, 0))
    win = pl.BlockSpec((nb * KV_WIDTH, w), lambda b: (b, 0))
    return pl.pallas_call(
        functools.partial(_swa_sample_kernel, nb=nb, tq=tq),
        grid=(batch // nb,),
        in_specs=[own(MAIN_WIDTH), win, own(KV_WIDTH), win, own(KV_WIDTH), _const(bias.shape),
                  _const(sink_col.shape), _const((SWA_KV_HEADS, 1, KV_WIDTH))],
        out_specs=pl.BlockSpec((nb * tq, MAIN_WIDTH), lambda b: (b, 0)),
        out_shape=jax.ShapeDtypeStruct((batch * tq, MAIN_WIDTH), BF16),
        compiler_params=_params(("arbitrary",)),
        name="swa_sample",
    )(q, k_win, k, v_win, v, bias, sink_col, _head_masks(SWA_KV_HEADS))


def kernel(x_prompt, x_sample, state_gla, cache_win_k, cache_win_v, cache_mem_k, cache_mem_v, mem_prompt, norm_mix_g, norm_ffn_g, norm_mem_g, w_mem_kv, mem_qn_g, mem_kn_g, w_out, w_in_a, w_gate_lr, b_gate_lr, gla_norm_g, w_in_b, swa_qn_g, swa_sinks, norm_kv_g, w_kv, swa_kn_g, w_router_group, b_router_group, w_router_expert, b_router_expert, w_exp_gate, w_exp_up, w_exp_down):
    bp, tp, d = x_prompt.shape
    bs, ts, _ = x_sample.shape
    assert bp == 1 and tp % WINDOW == 0 and ts * (GLA_CHUNK // ts) == GLA_CHUNK
    n_p, n_s = bp * tp, bs * ts
    w_buf = cache_win_k.shape[1]
    assert w_buf == WINDOW
    x_p, x_s = x_prompt.reshape(n_p, d), x_sample.reshape(n_s, d)

    mem_k_p, mem_v_p = _mem_kv(mem_prompt, norm_mem_g, w_mem_kv, mem_kn_g)
    feature_major = lambda c: jnp.moveaxis(c, -3, -1).reshape(*c.shape[:-3], c.shape[-2] * c.shape[-1], c.shape[-3])
    cmk, cmv = feature_major(cache_mem_k), feature_major(cache_mem_v)

    def mem_attend(mq, l):
        tm_p = _row_tile(tp, 1024)
        mo_p = _mem_attn(mq, mem_k_p, mem_v_p, mem_qn_g[l], row_off=0, seq=tp, tm=tm_p, bb=1, layer=l)
        mo_s = _mem_attn(mq, cmk, cmv, mem_qn_g[l], row_off=n_p, seq=ts, tm=ts, bb=16, layer=l)
        return mo_p, mo_s

    tm = _row_tile(math.gcd(n_p, n_s), MOE_TILE)
    router = lambda l: (norm_ffn_g[l], w_router_group[l], b_router_group[l], w_router_expert[l],
                        b_router_expert[l])

    def moe(h, routed, l, split_rows=None):
        return _moe(h, routed, norm_ffn_g[l], w_exp_gate, w_exp_up, w_exp_down, l, tm, split_rows)

    q, k, la, v, og, mq = _inproj_a(x_p, x_s, norm_mix_g[0], w_in_a[0], w_gate_lr[0], b_gate_lr[0])
    zero_state = jnp.zeros((bp, GLA_HEADS, GLA_DK, GLA_DV), F32)
    n_sub = max(1, min(16, tp // GLA_CHUNK))
    main_p, gla_p = _gla(q, k, la, v, og, zero_state, gla_norm_g[0], row_off=0, seq=tp, n_seg=1, n_sub=n_sub)
    main_s, gla_s = _gla(q, k, la, v, og, state_gla[0], gla_norm_g[0], row_off=n_p, seq=ts,
                         n_seg=GLA_CHUNK // ts, n_sub=1)
    mo_p, mo_s = mem_attend(mq, 0)
    w_o = w_out[0]
    h, routed = _outproj((x_p, x_s), main_p, mo_p, main_s, mo_s, w_o[:MAIN_WIDTH].reshape(GLA_HEADS, GLA_DV, d),
                         w_o[MAIN_WIDTH:], router(0), tm)
    h = moe(h, routed, 0)

    q, mq, k_sh, v_sh, kt_sh = _inproj_b(h, norm_kv_g, norm_mix_g[1], w_kv, w_in_b[0], swa_kn_g, swa_qn_g[0])
    ck = feature_major(cache_win_k).reshape(bs * KV_WIDTH, w_buf)
    cv = feature_major(cache_win_v).reshape(bs * KV_WIDTH, w_buf)
    main_p = _swa_prompt(q, kt_sh, v_sh, swa_sinks[0], n_rows=n_p, nb=8)
    main_s = _swa_sample(q, ck, cv, k_sh, v_sh, swa_sinks[0], row_off=n_p, batch=bs, tq=ts, nb=16)
    mo_p, mo_s = mem_attend(mq, 1)
    w_o = w_out[1]
    h, routed = _outproj(h, main_p, mo_p, main_s, mo_s, w_o[:MAIN_WIDTH][_swa_perm()], w_o[MAIN_WIDTH:],
                         router(1), tm)
    y_p, y_s = moe(h, routed, 1, split_rows=n_p)

    y_prompt = y_p.reshape(bp, tp, d)
    y_sample = y_s.reshape(bs, ts, d)
    k_new = k_sh[n_p:].reshape(bs, ts, SWA_KV_HEADS, HEAD_DIM)
    v_new = v_sh[n_p:].reshape(bs, ts, SWA_KV_HEADS, HEAD_DIM)
    win_k_s = jnp.concatenate([cache_win_k, k_new], axis=1)[:, -w_buf:]
    win_v_s = jnp.concatenate([cache_win_v, v_new], axis=1)[:, -w_buf:]
    win_k_p = k_sh[n_p - WINDOW:n_p].reshape(bp, WINDOW, SWA_KV_HEADS, HEAD_DIM)
    win_v_p = v_sh[n_p - WINDOW:n_p].reshape(bp, WINDOW, SWA_KV_HEADS, HEAD_DIM)
    token_major = lambda c: jnp.moveaxis(c.reshape(*c.shape[:-2], MEM_HEADS, HEAD_DIM, c.shape[-1]), -1, -3)
    return (y_prompt, y_sample, gla_p[None], gla_s[None], win_k_p, win_v_p, win_k_s, win_v_s,
            token_major(mem_k_p), token_major(mem_v_p))
```

```python
import functools
import math

import numpy as np
import jax
import jax.numpy as jnp
from jax import lax
from jax.experimental import pallas as pl
from jax.experimental.pallas import tpu as pltpu

F32 = jnp.float32
BF16 = jnp.bfloat16

D_MODEL = 1024
MEM_HEADS = 4
HEAD_DIM = 64
MEM_WIDTH = MEM_HEADS * HEAD_DIM
MAIN_WIDTH = D_MODEL - MEM_WIDTH
GLA_HEADS = 4
GLA_DV = MAIN_WIDTH // GLA_HEADS
GLA_DK = GLA_DV // 2
GLA_DK_PAD = 128
GLA_KEY_WIDTH = GLA_HEADS * GLA_DK
GLA_KEY_PAD = GLA_HEADS * GLA_DK_PAD
GLA_GATE_RANK = 16
GLA_TAU = 16.0
GLA_CHUNK = 64
SWA_HEADS = MAIN_WIDTH // HEAD_DIM
SWA_KV_HEADS = 4
SWA_GROUP = SWA_HEADS // SWA_KV_HEADS
KV_WIDTH = SWA_KV_HEADS * HEAD_DIM
WINDOW = 128
N_GROUPS = 4
EXPERTS_PER_GROUP = 8
N_EXPERTS = N_GROUPS * EXPERTS_PER_GROUP
D_EXPERT = 512
EPS = 1e-6
LANES = 128
NEG_BIG = -1e30
VMEM_LIMIT = 56 * 1024 * 1024
MOE_TILE = 512
MOE_SUB = 128
ROUTER_LANE0 = N_GROUPS
ROUTER_META_ROWS = 8
ROUTER_ROWS = 40
SLABS = D_MODEL // LANES
PACKED_SLABS = SLABS // 2


def _bdot(a, b):
    return jnp.dot(a.astype(BF16), b.astype(BF16), preferred_element_type=F32)


def _bdot_nt(a, b):
    return lax.dot_general(a.astype(BF16), b.astype(BF16), (((1,), (1,)), ((), ())),
                           preferred_element_type=F32)


def _bdot_tn(a, b):
    return lax.dot_general(a.astype(BF16), b.astype(BF16), (((0,), (0,)), ((), ())),
                           preferred_element_type=F32)


def _split(x, n):
    parts = []
    for _ in range(n - 1):
        p = x.astype(BF16)
        parts.append(p)
        x = x - p.astype(F32)
    parts.append(x.astype(BF16))
    return parts


def _exact_left_dot(m, x, n=2):
    out = None
    for p in _split(x, n):
        t = jnp.dot(m, p, preferred_element_type=F32)
        out = t if out is None else out + t
    return out


def _seg_mean(x2, bd):
    out = None
    for p in _split(x2, 2):
        t = jnp.dot(p, bd, preferred_element_type=F32)
        out = t if out is None else out + t
    return out


def _rms_scale(x):
    return lax.rsqrt(jnp.mean(x * x, axis=-1, keepdims=True) + EPS)


def _row_tile(n, cap=512):
    t = cap
    while t > 8 and n % t:
        t //= 2
    assert n % t == 0, n
    return t


def _params(sem):
    return pltpu.CompilerParams(dimension_semantics=sem, vmem_limit_bytes=VMEM_LIMIT)


def _const(shape):
    nd = len(shape)
    return pl.BlockSpec(shape, lambda *_: (0,) * nd)


def _group_specs(tm, width, prompt_tiles, lead=None):
    p_idx = lambda i, *_: jnp.minimum(i, prompt_tiles - 1)
    s_idx = lambda i, *_: jnp.maximum(i - prompt_tiles, 0)
    if lead is None:
        return [pl.BlockSpec((tm, width), lambda i, *_, f=f: (f(i), 0)) for f in (p_idx, s_idx)]
    return [pl.BlockSpec((lead, tm, width), lambda i, *_, f=f: (0, f(i), 0)) for f in (p_idx, s_idx)]


def _block_diag_mean(width):
    i = np.arange(width)
    return jnp.asarray((i[:, None] // HEAD_DIM == i[None, :] // HEAD_DIM) / HEAD_DIM, BF16)


def _head_masks(n_heads):
    i = np.arange(n_heads * HEAD_DIM)
    return jnp.asarray((i[None, :] // HEAD_DIM == np.arange(n_heads)[:, None]), F32)[:, None, :]


def _mem_kv_kernel(mem_ref, g_ref, w_ref, kng_ref, bd_ref, k_ref, v_ref):
    x = mem_ref[0]
    hn = x * _rms_scale(x) * g_ref[0]
    kv = _bdot(hn, w_ref[0])
    k = kv[:, :MEM_WIDTH]
    k = k * lax.rsqrt(_seg_mean(k * k, bd_ref[...]) + EPS) * kng_ref[0]
    k_ref[0, 0] = k.T
    v_ref[0, 0] = kv[:, MEM_WIDTH:].T


def _mem_kv(mem, g, w, kng):
    depth, (b, m, d) = w.shape[0], mem.shape
    out = jax.ShapeDtypeStruct((depth, b, m, MEM_WIDTH), F32)
    blk = pl.BlockSpec((1, 1, m, MEM_WIDTH), lambda l, i: (l, i, 0, 0))
    return pl.pallas_call(
        _mem_kv_kernel,
        grid=(depth, b),
        in_specs=[pl.BlockSpec((1, m, d), lambda l, i: (i, 0, 0)),
                  pl.BlockSpec((1, 1, d), lambda l, i: (l, 0, 0)),
                  pl.BlockSpec((1, d, 2 * MEM_WIDTH), lambda l, i: (l, 0, 0)),
                  pl.BlockSpec((1, 1, MEM_WIDTH), lambda l, i: (l, 0, 0)),
                  _const((MEM_WIDTH, MEM_WIDTH))],
        out_specs=[blk, blk],
        out_shape=[out, out],
        compiler_params=_params(("arbitrary", "arbitrary")),
        name="mem_kv",
    )(mem, g.reshape(depth, 1, d), w.astype(BF16),
      jnp.tile(kng, (1, MEM_HEADS)).reshape(depth, 1, MEM_WIDTH), _block_diag_mean(MEM_WIDTH))


def _inproj_a_kernel(xp_ref, xs_ref, g_ref, wq_ref, wk_ref, wv_ref, wog_ref, wlr_ref, wmq_ref, wgl_ref, bgl_ref,
                     q_ref, k_ref, la_ref, v_ref, og_ref, mq_ref, *, prompt_tiles):
    x = jnp.where(pl.program_id(0) < prompt_tiles, xp_ref[...], xs_ref[...])
    hn = (x * _rms_scale(x) * g_ref[...]).astype(BF16)
    q_ref[...] = jnp.dot(hn, wq_ref[...], preferred_element_type=F32) * (GLA_DK ** -0.5)
    k_ref[...] = jnp.dot(hn, wk_ref[...], preferred_element_type=F32)
    for h in range(GLA_HEADS):
        v_ref[h] = jnp.dot(hn, wv_ref[h], preferred_element_type=F32).astype(BF16)
        og_ref[h] = jnp.dot(hn, wog_ref[h], preferred_element_type=F32)
    lr = jnp.dot(hn, wlr_ref[...], preferred_element_type=F32)
    z = _bdot(lr, wgl_ref[...]) + bgl_ref[...]
    la_ref[...] = (jnp.minimum(z, 0.0) - jnp.log(1.0 + jnp.exp(-jnp.abs(z)))) * (1.0 / GLA_TAU)
    mq_ref[...] = jnp.dot(hn, wmq_ref[...], preferred_element_type=F32)


def _pad_heads(w, width, pad):
    lead = w.shape[:-1]
    w = w.reshape(*lead, GLA_HEADS, width)
    w = jnp.pad(w, [(0, 0)] * len(lead) + [(0, 0), (0, pad - width)])
    return w.reshape(*lead, GLA_HEADS * pad)


def _inproj_a(x_p, x_s, g, w_in, w_lr, b_lr):
    (n_p, d), n_s = x_p.shape, x_s.shape[0]
    n = n_p + n_s
    tm = _row_tile(n_s)
    assert n_p % tm == 0
    pt = n_p // tm
    c0, c1, c2, c3, c4 = (GLA_KEY_WIDTH, 2 * GLA_KEY_WIDTH, 2 * GLA_KEY_WIDTH + MAIN_WIDTH,
                          2 * GLA_KEY_WIDTH + 2 * MAIN_WIDTH,
                          2 * GLA_KEY_WIDTH + 2 * MAIN_WIDTH + GLA_GATE_RANK)
    wb = w_in.astype(BF16)
    wq = _pad_heads(wb[:, :c0], GLA_DK, GLA_DK_PAD)
    wk = _pad_heads(wb[:, c0:c1], GLA_DK, GLA_DK_PAD)
    wv = wb[:, c1:c2].reshape(d, GLA_HEADS, GLA_DV).transpose(1, 0, 2)
    wog = wb[:, c2:c3].reshape(d, GLA_HEADS, GLA_DV).transpose(1, 0, 2)
    wlr = jnp.pad(wb[:, c3:c4], ((0, 0), (0, LANES - GLA_GATE_RANK)))
    wmq = wb[:, c4:]
    wgl = jnp.pad(_pad_heads(w_lr.astype(BF16), GLA_DK, GLA_DK_PAD), ((0, LANES - GLA_GATE_RANK), (0, 0)))
    bgl = _pad_heads(b_lr.reshape(1, -1), GLA_DK, GLA_DK_PAD)
    row = lambda w: pl.BlockSpec((tm, w), lambda i: (i, 0))
    hrow = pl.BlockSpec((GLA_HEADS, tm, GLA_DV), lambda i: (0, i, 0))
    key = jax.ShapeDtypeStruct((n, GLA_KEY_PAD), F32)
    val = jax.ShapeDtypeStruct((GLA_HEADS, n, GLA_DV), F32)
    return pl.pallas_call(
        functools.partial(_inproj_a_kernel, prompt_tiles=pt),
        grid=(n // tm,),
        in_specs=_group_specs(tm, d, pt) + [
            _const((1, d)), _const(wq.shape), _const(wk.shape), _const(wv.shape),
            _const(wog.shape), _const(wlr.shape), _const(wmq.shape), _const(wgl.shape),
            _const(bgl.shape)],
        out_specs=[row(GLA_KEY_PAD), row(GLA_KEY_PAD), row(GLA_KEY_PAD), hrow, hrow, row(MEM_WIDTH)],
        out_shape=[key, key, key, jax.ShapeDtypeStruct(val.shape, BF16), val,
                   jax.ShapeDtypeStruct((n, MEM_WIDTH), F32)],
        compiler_params=_params(("parallel",)),
        name="inproj_a",
    )(x_p, x_s, g.reshape(1, d), wq, wk, wv, wog, wlr, wmq, wgl, bgl)


def _gla_kernel(q_ref, k_ref, la_ref, v_ref, og_ref, s0_ref, gn_ref, mcum_ref, mall_ref, sel_ref,
                o_ref, sout_ref, s_ref, *, chunk, n_sub, n_seg):
    j = pl.program_id(1)
    seg = chunk // n_seg

    @pl.when(j == 0)
    def _():
        s_ref[...] = jnp.zeros_like(s_ref)
        s_ref[:, :, :GLA_DK, :] = s0_ref[...]

    mcum = mcum_ref[...]
    causal = mcum.astype(F32) > 0.0
    row = lax.broadcasted_iota(jnp.int32, (chunk, GLA_DK_PAD), 0)
    gn = gn_ref[...]
    hcols = [slice(h * GLA_DK_PAD, (h + 1) * GLA_DK_PAD) for h in range(GLA_HEADS)]
    crows = [slice(c * chunk, (c + 1) * chunk) for c in range(n_sub)]
    qts, kts, kds, e_ends = [], [], [], []
    for rows in crows:
        la = la_ref[rows, :]
        b = _exact_left_dot(mcum, la)
        if n_seg == 1:
            b_end = b[chunk - 1:chunk, :]
            e_ends.append(jnp.broadcast_to(jnp.exp(b_end), (LANES, b.shape[1])).T)
        else:
            b_end = _exact_left_dot(mall_ref[...], la)
            e_ends.append(jnp.exp(_exact_left_dot(sel_ref[...], la)).T)
        k = k_ref[rows, :]
        qts.append(q_ref[rows, :] * jnp.exp(b))
        kts.append((k * jnp.exp(-b)).astype(BF16))
        kds.append(k * jnp.exp(b_end - b))
    vbs = [[v_ref[h, rows, :].astype(BF16) for h in range(GLA_HEADS)] for rows in crows]
    scores = [[_bdot_nt(qts[c][:, cols], kts[c][:, cols]) for cols in hcols] for c in range(n_sub)]
    kvs = []
    for c in range(n_sub):
        per_head = []
        for h, cols in enumerate(hcols):
            per_seg = []
            for s in range(n_seg):
                kd = kds[c][:, cols]
                if n_seg > 1:
                    kd = jnp.where((row >= s * seg) & (row < (s + 1) * seg), kd, 0.0)
                per_seg.append(_bdot_tn(kd, vbs[c][h]))
            per_head.append(per_seg)
        kvs.append(per_head)
    state = [[s_ref[s, h] for s in range(n_seg)] for h in range(GLA_HEADS)]
    inters = []
    for c in range(n_sub):
        per_head = []
        for h, cols in enumerate(hcols):
            parts = []
            for s in range(n_seg):
                parts.append(_bdot(qts[c][s * seg:(s + 1) * seg, cols], state[h][s]))
                state[h][s] = e_ends[c][cols, s:s + 1] * state[h][s] + kvs[c][h][s]
            per_head.append(parts[0] if n_seg == 1 else jnp.concatenate(parts, axis=0))
        inters.append(per_head)
    for h in range(GLA_HEADS):
        for s in range(n_seg):
            s_ref[s, h] = state[h][s]
    for c, rows in enumerate(crows):
        for h in range(GLA_HEADS):
            a = jnp.where(causal, scores[c][h], 0.0)
            o = _bdot(a, vbs[c][h]) + inters[c][h]
            on = o * lax.rsqrt(jnp.mean(o * o, axis=-1, keepdims=True) + EPS) * gn
            og = og_ref[h, rows, :]
            o_ref[h, rows, :] = (on * (og * jax.nn.sigmoid(og))).astype(BF16)

    @pl.when(j == pl.num_programs(1) - 1)
    def _():
        sout_ref[...] = s_ref[:, :, :GLA_DK, :]


def _gla(q, k, la, v, og, s0, gnorm, *, row_off, seq, n_seg, n_sub):
    batch = s0.shape[0]
    chunk = GLA_CHUNK
    assert chunk % n_seg == 0 and batch % n_seg == 0
    seg = chunk // n_seg
    step_rows = n_sub * chunk
    if n_seg > 1:
        assert seq == seg and n_sub == 1
        t_steps = 1
    else:
        assert seq % step_rows == 0
        t_steps = seq // step_rows
    assert row_off % step_rows == 0
    off = row_off // step_rows
    i = np.arange(chunk)
    same = (i[:, None] // seg) == (i[None, :] // seg)
    mcum = jnp.asarray(same & (i[None, :] <= i[:, None]), BF16)
    mall = jnp.asarray(same, BF16)
    sel = jnp.asarray((i[None, :] // seg) == np.arange(LANES)[:, None], BF16)
    ridx = lambda g, j: (off + g * t_steps + j, 0)
    hidx = lambda g, j: (0, off + g * t_steps + j, 0)
    key_spec = pl.BlockSpec((step_rows, GLA_KEY_PAD), ridx)
    val_spec = pl.BlockSpec((GLA_HEADS, step_rows, GLA_DV), hidx)
    st_spec = pl.BlockSpec((n_seg, GLA_HEADS, GLA_DK, GLA_DV), lambda g, j: (g, 0, 0, 0))
    in_specs = [key_spec, key_spec, key_spec, val_spec, val_spec, st_spec, _const((1, GLA_DV)),
                _const((chunk, chunk)), _const((chunk, chunk)), _const((LANES, chunk))]
    args = [q, k, la, v, og, s0, gnorm.reshape(1, GLA_DV), mcum, mall, sel]
    out_spec = pl.BlockSpec((GLA_HEADS, step_rows, GLA_DV), lambda g, j: (0, g * t_steps + j, 0))
    return pl.pallas_call(
        functools.partial(_gla_kernel, chunk=chunk, n_sub=n_sub, n_seg=n_seg),
        grid=(batch // n_seg, t_steps),
        in_specs=in_specs,
        out_specs=[out_spec, st_spec],
        out_shape=[jax.ShapeDtypeStruct((GLA_HEADS, batch * seq, GLA_DV), BF16),
                   jax.ShapeDtypeStruct(s0.shape, F32)],
        scratch_shapes=[pltpu.VMEM((n_seg, GLA_HEADS, GLA_DK_PAD, GLA_DV), F32)],
        compiler_params=_params(("arbitrary", "arbitrary")),
        name="gla",
    )(*args)


def _mem_attn_kernel(q_ref, k_ref, v_ref, g_ref, bd_ref, hm_ref, o_ref, *, tm, bb):
    g = g_ref[...]
    sub = min(tm, 128)
    units = [(i, i * tm + r) for i in range(bb) for r in range(0, tm, sub)]
    scores = []
    for i, r in units:
        q = q_ref[r:r + sub, :]
        qn = q * lax.rsqrt(_seg_mean(q * q, bd_ref[...]) + EPS) * g
        qs = jnp.concatenate([(qn * hm_ref[h]).astype(BF16) for h in range(MEM_HEADS)], axis=0)
        scores.append(_bdot(qs, k_ref[i]))
    probs = []
    for s in scores:
        e = jnp.exp(s - jnp.max(s, axis=-1, keepdims=True))
        probs.append(e * (1.0 / jnp.sum(e, axis=-1, keepdims=True)))
    outs = [_bdot_nt(p, v_ref[i]) for (i, _), p in zip(units, probs)]
    rows = []
    for o in outs:
        acc = o[:sub] * hm_ref[0]
        for h in range(1, MEM_HEADS):
            acc = acc + o[h * sub:(h + 1) * sub] * hm_ref[h]
        rows.append(acc)
    o_ref[...] = jnp.concatenate(rows, axis=0).astype(BF16)


def _mem_attn(mq, mk, mv, qng, *, row_off, seq, tm, bb, layer):
    depth, batch, m, _ = mk.shape
    mk = mk.reshape(depth * batch, m, MEM_WIDTH)
    mv = mv.reshape(depth * batch, m, MEM_WIDTH)
    kv_off = layer * batch // bb
    assert seq % tm == 0 and batch % bb == 0 and (bb == 1 or seq == tm)
    t_steps = seq // tm
    step_rows = bb * tm
    assert row_off % step_rows == 0
    off = row_off // step_rows
    row_spec = pl.BlockSpec((step_rows, MEM_WIDTH), lambda g, j: (off + g * t_steps + j, 0))
    kv_spec = pl.BlockSpec((bb, m, MEM_WIDTH), lambda g, j: (kv_off + g, 0, 0))
    in_specs = [row_spec, kv_spec, kv_spec, _const((1, MEM_WIDTH)), _const((MEM_WIDTH, MEM_WIDTH)),
                _const((MEM_HEADS, 1, MEM_WIDTH))]
    args = [mq, mk, mv, (jnp.tile(qng, MEM_HEADS) * HEAD_DIM ** -0.5).reshape(1, MEM_WIDTH),
            _block_diag_mean(MEM_WIDTH), _head_masks(MEM_HEADS)]
    return pl.pallas_call(
        functools.partial(_mem_attn_kernel, tm=tm, bb=bb),
        grid=(batch // bb, t_steps),
        in_specs=in_specs,
        out_specs=pl.BlockSpec((step_rows, MEM_WIDTH), lambda g, j: (g * t_steps + j, 0)),
        out_shape=jax.ShapeDtypeStruct((batch * seq, MEM_WIDTH), BF16),
        compiler_params=_params(("parallel", "parallel")),
        name="mem_attn",
    )(*args)


def _outproj_kernel(*refs, heads, prompt_tiles, split_residual):
    n_h = 2 if split_residual else 1
    h_refs, refs = refs[:n_h], refs[n_h:]
    (main_p_ref, main_s_ref, mo_p_ref, mo_s_ref, wmain_ref, wmo_ref), refs = refs[:6], refs[6:]
    route_in, (o_ref, *route_out) = refs[:6], refs[6:]
    is_prompt = pl.program_id(0) < prompt_tiles
    pick = lambda p, s: jnp.where(is_prompt, p, s)
    tm = o_ref.shape[0]
    sub = min(tm, 128)
    blocks = []
    for r in range(0, tm, sub):
        rows = slice(r, r + sub)
        acc = pick(h_refs[0][rows, :], h_refs[1][rows, :]) if split_residual else h_refs[0][rows, :]
        acc = acc + _bdot(pick(mo_p_ref[rows, :], mo_s_ref[rows, :]), wmo_ref[...])
        if heads:
            for h in range(heads):
                acc = acc + _bdot(pick(main_p_ref[h, rows, :], main_s_ref[h, rows, :]), wmain_ref[h])
        else:
            acc = acc + _bdot(pick(main_p_ref[rows, :], main_s_ref[rows, :]), wmain_ref[...])
        o_ref[rows, :] = acc
        blocks.append(acc)
    _route_tile(blocks, *route_in, *route_out)


def _outproj(h, main_p, mo_p, main_s, mo_s, w_main, w_mo, router, tm):
    n_p, n_s = mo_p.shape[0], mo_s.shape[0]
    n, d = n_p + n_s, w_mo.shape[1]
    assert n_p % tm == 0 and n_s % tm == 0
    pt = n_p // tm
    heads = main_p.shape[0] if main_p.ndim == 3 else 0
    row = pl.BlockSpec((tm, d), lambda i: (i, 0))
    split = isinstance(h, tuple)
    h_specs, h_args = (_group_specs(tm, d, pt), list(h)) if split else ([row], [h])
    main_specs = _group_specs(tm, main_p.shape[-1], pt, lead=heads or None)
    r_args, r_in_specs, r_out_specs, r_out_shape, r_scratch = _route_operands(*router, n, tm)
    h_new, *routed = pl.pallas_call(
        functools.partial(_outproj_kernel, heads=heads, prompt_tiles=pt, split_residual=split),
        grid=(n // tm,),
        in_specs=(h_specs + main_specs + _group_specs(tm, MEM_WIDTH, pt)
                  + [_const(w_main.shape), _const(w_mo.shape)] + r_in_specs),
        out_specs=[row] + r_out_specs,
        out_shape=[jax.ShapeDtypeStruct((n, d), F32)] + r_out_shape,
        scratch_shapes=[r_scratch],
        compiler_params=_params(("arbitrary",)),
        name="outproj_route",
    )(*h_args, main_p, main_s, mo_p, mo_s, w_main.astype(BF16), w_mo.astype(BF16), *r_args)
    return h_new, routed


def _route_tile(x_blocks, g_ref, whi_ref, wlo_ref, b_ref, before_tok_ref, before_row_ref,
                mi_ref, mf_ref, cnt_ref, tt_ref, carry_ref):
    @pl.when(pl.program_id(0) == 0)
    def _():
        carry_ref[...] = jnp.zeros_like(carry_ref)

    nt = lambda a, b: lax.dot_general(a, b, (((1,), (1,)), ((), ())), preferred_element_type=F32)
    logits = []
    for x in x_blocks:
        x_hi, x_lo = _split(x * _rms_scale(x) * g_ref[...], 2)
        logits.append((nt(whi_ref[...], x_hi) + nt(wlo_ref[...], x_hi) + nt(whi_ref[...], x_lo))[:ROUTER_ROWS])
    logits = jnp.concatenate(logits, axis=1) + b_ref[...]
    tm = logits.shape[1]
    row = lax.broadcasted_iota(jnp.int32, (ROUTER_ROWS, tm), 0)
    far = jnp.int32(2 * LANES)

    def first_max(vals):
        m = jnp.max(vals, axis=0, keepdims=True)
        return m, jnp.min(jnp.where(vals == m, row, far), axis=0, keepdims=True)

    gl = jnp.where(row < N_GROUPS, logits, -jnp.inf)
    gmax, grp = first_max(gl)
    pg_sel = 1.0 / jnp.sum(jnp.exp(gl - gmax), axis=0, keepdims=True)
    lo = ROUTER_LANE0 + grp * EXPERTS_PER_GROUP
    el = jnp.where((row >= lo) & (row < lo + EXPERTS_PER_GROUP), logits, -jnp.inf)
    m1, i1 = first_max(el)
    m2, i2 = first_max(jnp.where(row == i1, -jnp.inf, el))
    e2 = jnp.exp(m2 - m1)
    g1 = pg_sel / (1.0 + e2)
    g2 = pg_sel * e2 / (1.0 + e2)

    oh1 = row == i1
    oh2 = row == i2
    picked = jnp.where(oh1 | oh2, 1.0, 0.0)
    earlier = jnp.dot(picked.astype(BF16), before_tok_ref[...], preferred_element_type=F32)
    cnt_col = jnp.sum(picked, axis=1, keepdims=True)
    cnt_tile = jnp.concatenate([jnp.broadcast_to(cnt_col, (ROUTER_ROWS, LANES)),
                                jnp.zeros((LANES - ROUTER_ROWS, LANES), F32)], axis=0)
    c_hi = jnp.floor(cnt_tile * (1.0 / 32.0))
    c_lo = cnt_tile - 32.0 * c_hi
    first = (32.0 * jnp.dot(before_row_ref[...], c_hi.astype(BF16), preferred_element_type=F32)
             + jnp.dot(before_row_ref[...], c_lo.astype(BF16), preferred_element_type=F32))
    local = first[:ROUTER_ROWS, 0:1] + earlier
    lpos1 = jnp.sum(jnp.where(oh1, local, 0.0), axis=0, keepdims=True)
    lpos2 = jnp.sum(jnp.where(oh2, local, 0.0), axis=0, keepdims=True)
    carry_before = carry_ref[...]
    carry = carry_before + cnt_tile
    carry_ref[...] = carry

    lane = lax.broadcasted_iota(jnp.int32, (LANES, LANES), 1)
    cols = jnp.where(lane == 0, carry_before, jnp.where(lane == 1, cnt_tile, jnp.where(lane == 2, first,
                     jnp.where(lane == 3, carry, 0.0))))
    tables = cols.T
    tt_ref[...] = tables[:8].astype(jnp.int32)
    cnt_ref[...] = tables[3:4]
    row8 = lax.broadcasted_iota(jnp.int32, (ROUTER_META_ROWS, tm), 0)
    zero8 = jnp.zeros((ROUTER_META_ROWS, tm), F32)
    mi_ref[...] = jnp.where(row8 == 0, lpos1, jnp.where(row8 == 1, lpos2, zero8)).astype(jnp.int32)
    stacked = jnp.where(row8 == 0, g1, jnp.where(row8 == 1, g2, jnp.where(row8 == 2, lpos1,
                        jnp.where(row8 == 3, lpos2, zero8))))
    mf_ref[...] = jnp.concatenate([stacked, jnp.zeros((LANES - ROUTER_META_ROWS, tm), F32)], axis=0).T


def _route_operands(g, w_rg, b_rg, w_re, b_re, n, tm):
    d = g.shape[0]
    assert n % tm == 0 and 2 * tm <= 32 * 32
    n_real = N_GROUPS + N_EXPERTS
    w = jnp.pad(jnp.concatenate([w_rg, w_re], axis=1), ((0, 0), (0, LANES - n_real))).T
    b = jnp.pad(jnp.concatenate([b_rg, b_re]), (0, ROUTER_ROWS - n_real)).reshape(ROUTER_ROWS, 1)
    w_hi = w.astype(BF16)
    w_lo = (w - w_hi.astype(F32)).astype(BF16)
    i = np.arange(tm)
    before_tok = jnp.asarray(i[:, None] < i[None, :], BF16)
    e = np.arange(LANES)
    before_row = jnp.asarray(e[None, :] < e[:, None], BF16)
    args = [g.reshape(1, d), w_hi, w_lo, b, before_tok, before_row]
    in_specs = [_const(a.shape) for a in args]
    out_specs = [pl.BlockSpec((ROUTER_META_ROWS, tm), lambda i: (0, i)), pl.BlockSpec((tm, LANES), lambda i: (i, 0)),
                 _const((1, LANES)), pl.BlockSpec((8, LANES), lambda i: (i, 0))]
    out_shape = [jax.ShapeDtypeStruct((ROUTER_META_ROWS, n), jnp.int32), jax.ShapeDtypeStruct((n, LANES), F32),
                 jax.ShapeDtypeStruct((1, LANES), F32), jax.ShapeDtypeStruct((n // tm * 8, LANES), jnp.int32)]
    return args, in_specs, out_specs, out_shape, pltpu.VMEM((LANES, LANES), F32)


def _pack_rows(ref, x, rows, lead=(), row0=0):
    u32 = jnp.uint32
    for w in range(PACKED_SLABS):
        lo = x[:, (2 * w) * LANES:(2 * w + 1) * LANES].astype(BF16).astype(F32)
        hi = x[:, (2 * w + 1) * LANES:(2 * w + 2) * LANES].astype(BF16).astype(F32)
        word = (lax.bitcast_convert_type(lo, u32) >> 16) | (lax.bitcast_convert_type(hi, u32) & u32(0xFFFF0000))
        ref[lead + (pl.ds(row0 * PACKED_SLABS + w, rows, stride=PACKED_SLABS), slice(None))] = word


def _unpack_rows(ref, rows, lead=(), row0=0):
    u32 = jnp.uint32
    slabs = []
    for w in range(PACKED_SLABS):
        word = ref[lead + (pl.ds(row0 * PACKED_SLABS + w, rows, stride=PACKED_SLABS), slice(None))]
        slabs.append(lax.bitcast_convert_type(word << 16, F32).astype(BF16))
        slabs.append(lax.bitcast_convert_type(word & u32(0xFFFF0000), F32).astype(BF16))
    return jnp.concatenate(slabs, axis=1)


RUN_FIELDS = 3
RUN_CHUNK_BITS = 6


def _copy_runs(runs_ref, tile, local_rows, global_rows, sem, *, to_global):
    ps = PACKED_SLABS
    base = tile * (RUN_FIELDS * N_EXPERTS)

    def piece(g0, l0, off, size):
        g = global_rows(pl.multiple_of((g0 + off) * ps, ps), size * ps)
        l = local_rows(pl.multiple_of((l0 + off) * ps, ps), size * ps)
        src, dst = (l, g) if to_global else (g, l)
        pltpu.make_async_copy(src, dst, sem).start()

    def per_expert(e, carry):
        g0 = runs_ref[base + e]
        length = runs_ref[base + N_EXPERTS + e]
        l0 = runs_ref[base + 2 * N_EXPERTS + e]
        big = 1 << RUN_CHUNK_BITS

        def big_piece(c, inner):
            piece(g0, l0, c * big, big)
            return inner

        n_big = length >> RUN_CHUNK_BITS
        lax.fori_loop(0, n_big, big_piece, 0)
        off = n_big * big
        for bit in reversed(range(RUN_CHUNK_BITS)):
            size = 1 << bit

            @pl.when((length & size) != 0)
            def _(off=off, size=size):
                piece(g0, l0, off, size)

            off = off + (length & size)
        return carry

    lax.fori_loop(0, N_EXPERTS, per_expert, 0)


def _fill_pads(pads_ref, buf0, xs_hbm, sem, *, wait):
    ps = PACKED_SLABS

    def go(copy):
        copy.wait() if wait else copy.start()

    def per_tail_tile(t, carry):
        rows = MOE_TILE * ps
        go(pltpu.make_async_copy(
            buf0.at[pl.ds(0, rows)],
            xs_hbm.at[pl.ds(pl.multiple_of((pads_ref[2 * N_EXPERTS] + t) * rows, rows), rows)], sem))
        return carry

    lax.fori_loop(0, pads_ref[2 * N_EXPERTS + 1], per_tail_tile, 0)

    def per_expert(e, carry):
        first, length = pads_ref[e], pads_ref[N_EXPERTS + e]
        off = 0
        for bit in reversed(range((MOE_TILE - 1).bit_length())):
            size = 1 << bit

            @pl.when((length & size) != 0)
            def _(off=off, size=size):
                go(pltpu.make_async_copy(
                    buf0.at[pl.ds(pl.multiple_of(off * ps, ps), size * ps)],
                    xs_hbm.at[pl.ds(pl.multiple_of((first + off) * ps, ps), size * ps)], sem))

            off = off + (length & size)
        return carry

    lax.fori_loop(0, N_EXPERTS, per_expert, 0)


def _dispatch_kernel(runs_ref, pads_ref, h_ref, g_ref, meta_ref, xs_hbm, buf, sem, pad_sem, *, tm, steps):
    i = pl.program_id(0)
    slot = lax.rem(i, 2)
    ns = 2 * tm
    assert ns >= MOE_TILE

    def wait_slot(sl):
        pltpu.make_async_copy(buf.at[sl], xs_hbm.at[pl.ds(0, ns * PACKED_SLABS)], sem.at[sl]).wait()

    @pl.when(i >= 2)
    def _():
        wait_slot(slot)

    x = h_ref[...]
    xn = (x * _rms_scale(x) * g_ref[...]).astype(BF16)
    j = lax.broadcasted_iota(jnp.int32, (ns, tm), 0)
    pick = jnp.where((j == meta_ref[0:1, :]) | (j == meta_ref[1:2, :]), 1.0, 0.0).astype(BF16)
    _pack_rows(buf, jnp.dot(pick, xn, preferred_element_type=F32), ns, (slot,))
    _copy_runs(runs_ref, i, lambda start, size: buf.at[slot, pl.ds(start, size)],
               lambda start, size: xs_hbm.at[pl.ds(start, size)], sem.at[slot], to_global=True)

    @pl.when(i == 0)
    def _():
        _fill_pads(pads_ref, buf.at[0], xs_hbm, pad_sem, wait=False)

    @pl.when(i == min(1, steps - 1))
    def _():
        _fill_pads(pads_ref, buf.at[0], xs_hbm, pad_sem, wait=True)

    @pl.when(i == steps - 1)
    def _():
        wait_slot(slot)
        if steps > 1:
            wait_slot(1 - slot)


def _dispatch(h, g, meta, runs, pads, tm, n_slots):
    n, d = h.shape
    steps = n // tm
    return pl.pallas_call(
        functools.partial(_dispatch_kernel, tm=tm, steps=steps),
        grid_spec=pltpu.PrefetchScalarGridSpec(
            num_scalar_prefetch=2,
            grid=(steps,),
            in_specs=[pl.BlockSpec((tm, d), lambda i, *_: (i, 0)),
                      pl.BlockSpec((1, d), lambda i, *_: (0, 0)),
                      pl.BlockSpec((ROUTER_META_ROWS, tm), lambda i, *_: (0, i))],
            out_specs=pl.BlockSpec(memory_space=pl.ANY),
            scratch_shapes=[pltpu.VMEM((2, 2 * tm * PACKED_SLABS, LANES), jnp.uint32),
                            pltpu.SemaphoreType.DMA((2,)), pltpu.SemaphoreType.DMA],
        ),
        out_shape=jax.ShapeDtypeStruct((n_slots * PACKED_SLABS, LANES), jnp.uint32),
        compiler_params=_params(("arbitrary",)),
        name="moe_dispatch",
    )(runs, pads, h, g.reshape(1, d), meta)


def _expert_kernel(exp_ref, new_ref, wslot_ref, next_ref, n_ref,
                   xs_hbm, wg_hbm, wu_hbm, wd_hbm, ys_hbm,
                   xbuf, ybuf, xsem, ysem, wgf, wuf, wdf, wsem, wgb, wub, wdb, *, tm, sub, layer, max_tiles):
    n = n_ref[0]
    rows = tm * PACKED_SLABS
    tile_rows = lambda hbm, t: hbm.at[pl.ds(pl.multiple_of(t * rows, rows), rows)]
    x_copy = lambda t, slot: pltpu.make_async_copy(tile_rows(xs_hbm, t), xbuf.at[slot], xsem.at[slot])
    y_copy = lambda t, slot: pltpu.make_async_copy(ybuf.at[slot], tile_rows(ys_hbm, t), ysem.at[slot])

    def weight_copies(e, slot):
        return [pltpu.make_async_copy(hbm.at[layer, e], buf.at[slot], wsem.at[slot])
                for hbm, buf in ((wg_hbm, wgf), (wu_hbm, wuf), (wd_hbm, wdf))]

    x_copy(0, 0).start()

    def item(w, carry):
        io = lax.rem(w, 2)
        x_copy(w, io).wait()

        @pl.when(w + 1 < n)
        def _():
            x_copy(w + 1, 1 - io).start()

        @pl.when(w >= 2)
        def _():
            y_copy(w - 2, io).wait()

        @pl.when(new_ref[w] != 0)
        def _():
            slot = wslot_ref[w]

            @pl.when(w == 0)
            def _():
                for c in weight_copies(exp_ref[w], slot):
                    c.start()

            for c in weight_copies(exp_ref[w], slot):
                c.wait()
            wgb[...] = wgf[slot].astype(BF16)
            wub[...] = wuf[slot].astype(BF16)
            wdb[...] = wdf[slot].astype(BF16)

            @pl.when(next_ref[w] >= 0)
            def _():
                for c in weight_copies(next_ref[w], 1 - slot):
                    c.start()

        n_blocks = tm // sub

        def up(s):
            x = _unpack_rows(xbuf, sub, (io,), row0=s * sub)
            return (jnp.dot(x, wgb[...], preferred_element_type=F32),
                    jnp.dot(x, wub[...], preferred_element_type=F32))

        ups = {0: up(0)}
        for s in range(n_blocks):
            if s + 1 < n_blocks:
                ups[s + 1] = up(s + 1)
            hg, hu = ups.pop(s)
            act = (hg * jax.nn.sigmoid(hg) * hu).astype(BF16)
            y = jnp.dot(act, wdb[...], preferred_element_type=F32)
            _pack_rows(ybuf, y, sub, (io,), row0=s * sub)
        y_copy(w, io).start()
        return carry

    lax.fori_loop(0, n, item, 0)

    @pl.when(n >= 2)
    def _():
        y_copy(n - 2, lax.rem(n, 2)).wait()
    y_copy(n - 1, lax.rem(n - 1, 2)).wait()

    for wait in (False, True):
        def spare(t, carry, wait=wait):
            copy = y_copy(n + t, 0)
            copy.wait() if wait else copy.start()
            return carry

        lax.fori_loop(0, max_tiles - n, spare, 0)


def _experts(xs, items, w_g, w_u, w_d, layer):
    d = D_MODEL
    tm = MOE_TILE
    max_tiles = items[0].shape[0]
    any_spec = pl.BlockSpec(memory_space=pl.ANY)
    io_buf = pltpu.VMEM((2, tm * PACKED_SLABS, LANES), jnp.uint32)
    return pl.pallas_call(
        functools.partial(_expert_kernel, tm=tm, sub=MOE_SUB, layer=layer, max_tiles=max_tiles),
        grid_spec=pltpu.PrefetchScalarGridSpec(
            num_scalar_prefetch=len(items),
            grid=(1,),
            in_specs=[any_spec, any_spec, any_spec, any_spec],
            out_specs=any_spec,
            scratch_shapes=[io_buf, io_buf, pltpu.SemaphoreType.DMA((2,)), pltpu.SemaphoreType.DMA((2,)),
                            pltpu.VMEM((2, d, D_EXPERT), F32), pltpu.VMEM((2, d, D_EXPERT), F32),
                            pltpu.VMEM((2, D_EXPERT, d), F32), pltpu.SemaphoreType.DMA((2,)),
                            pltpu.VMEM((d, D_EXPERT), BF16), pltpu.VMEM((d, D_EXPERT), BF16),
                            pltpu.VMEM((D_EXPERT, d), BF16)],
        ),
        out_shape=jax.ShapeDtypeStruct(xs.shape, jnp.uint32),
        compiler_params=_params(("arbitrary",)),
        name="moe_experts",
    )(*items, xs, w_g, w_u, w_d)


def _combine_kernel(runs_ref, h_ref, gate_ref, ys_hbm, *refs, tm, steps, prompt_tiles):
    out_refs, (buf, sem) = refs[:-2], refs[-2:]
    i = pl.program_id(0)
    slot = lax.rem(i, 2)
    ns = 2 * tm

    def issue(step, sl):
        _copy_runs(runs_ref, step, lambda start, size: buf.at[sl, pl.ds(start, size)],
                   lambda start, size: ys_hbm.at[pl.ds(start, size)], sem.at[sl], to_global=False)

    @pl.when(i == 0)
    def _():
        issue(0, 0)

    @pl.when(i + 1 < steps)
    def _():
        issue(i + 1, 1 - slot)

    pltpu.make_async_copy(ys_hbm.at[pl.ds(0, ns * PACKED_SLABS)], buf.at[slot], sem.at[slot]).wait()
    y = _unpack_rows(buf, ns, (slot,))
    g = gate_ref[...]
    j = lax.broadcasted_iota(jnp.int32, (tm, ns), 1)
    mix = (jnp.where(j == g[:, 2:3].astype(jnp.int32), g[:, 0:1], 0.0)
           + jnp.where(j == g[:, 3:4].astype(jnp.int32), g[:, 1:2], 0.0)).astype(BF16)
    out = h_ref[...] + jnp.dot(mix, y, preferred_element_type=F32)
    if len(out_refs) == 1:
        out_refs[0][...] = out
    else:
        @pl.when(i < prompt_tiles)
        def _():
            out_refs[0][...] = out

        @pl.when(i >= prompt_tiles)
        def _():
            out_refs[1][...] = out


def _combine(h, gates, ys, runs, tm, split_rows=None):
    n, d = h.shape
    steps = n // tm
    row = pl.BlockSpec((tm, d), lambda i, pos: (i, 0))
    if split_rows is None:
        pt, out_specs, out_shape = 0, row, jax.ShapeDtypeStruct((n, d), F32)
    else:
        assert split_rows % tm == 0
        pt = split_rows // tm
        out_specs = _group_specs(tm, d, pt)
        out_shape = [jax.ShapeDtypeStruct((split_rows, d), F32), jax.ShapeDtypeStruct((n - split_rows, d), F32)]
    return pl.pallas_call(
        functools.partial(_combine_kernel, tm=tm, steps=steps, prompt_tiles=pt),
        grid_spec=pltpu.PrefetchScalarGridSpec(
            num_scalar_prefetch=1,
            grid=(steps,),
            in_specs=[row, pl.BlockSpec((tm, LANES), lambda i, pos: (i, 0)),
                      pl.BlockSpec(memory_space=pl.ANY)],
            out_specs=out_specs,
            scratch_shapes=[pltpu.VMEM((2, 2 * tm * PACKED_SLABS, LANES), jnp.uint32),
                            pltpu.SemaphoreType.DMA((2,))],
        ),
        out_shape=out_shape,
        compiler_params=_params(("arbitrary",)),
        name="moe_combine",
    )(runs, h, gates, ys)


def _lookup(tables, idx):
    hit = idx[:, None] == jnp.arange(tables.shape[1], dtype=idx.dtype)[None, :]
    return jnp.sum(jnp.where(hit[None], tables[:, None, :], 0), axis=2)


def _work_items(tiles_e, max_items):
    item_end = jnp.cumsum(tiles_e)
    n_items = item_end[-1]
    w = jnp.minimum(jnp.arange(max_items, dtype=jnp.int32), n_items - 1)
    expert = jnp.sum(w[:, None] >= item_end[None, :], axis=1).astype(jnp.int32)
    prev_expert = jnp.concatenate([jnp.full((1,), -1, jnp.int32), expert[:-1]])
    new_expert = expert != prev_expert
    weight_slot = (jnp.cumsum(new_expert.astype(jnp.int32)) - 1) % 2
    ids = jnp.arange(N_EXPERTS, dtype=jnp.int32)
    later = (ids[None, :] > ids[:, None]) & (tiles_e[None, :] > 0)
    following = jnp.min(jnp.where(later, ids[None, :], N_EXPERTS), axis=1)
    following = jnp.where(following == N_EXPERTS, -1, following)
    next_expert, = _lookup(following[None, :], expert)
    as_i32 = lambda a: a.astype(jnp.int32)
    return (expert, as_i32(new_expert), as_i32(weight_slot), as_i32(next_expert),
            as_i32(n_items).reshape(1))


def _moe(h, routed, g, w_g, w_u, w_d, layer, tm, split_rows=None):
    n, _ = h.shape
    meta, gates, cnt, tables = routed
    experts = slice(ROUTER_LANE0, ROUTER_LANE0 + N_EXPERTS)
    counts = cnt[0, experts].astype(jnp.int32)
    tiles_e = (counts + MOE_TILE - 1) // MOE_TILE
    starts = (jnp.cumsum(tiles_e) - tiles_e) * MOE_TILE
    max_tiles = -(-2 * n // MOE_TILE) + N_EXPERTS
    tables = tables.reshape(n // tm, 8, LANES)[:, :RUN_FIELDS, experts]
    runs = tables.at[:, 0, :].add(starts[None, :]).reshape(-1)
    used = jnp.sum(tiles_e)
    pads = jnp.concatenate([starts + counts, tiles_e * MOE_TILE - counts, jnp.stack([used, max_tiles - used])])
    xs = _dispatch(h, g, meta, runs, pads, tm, max_tiles * MOE_TILE)
    ys = _experts(xs, _work_items(tiles_e, max_tiles), w_g, w_u, w_d, layer)
    return _combine(h, gates, ys, runs, tm, split_rows)


def _inproj_b_kernel(x_ref, gkv_ref, gmix_ref, wkv_ref, win_ref, kng_ref, qng_ref, bdk_ref, hsum_ref, hexp_ref,
                     q_ref, mq_ref, k_ref, v_ref, kt_ref):
    x = x_ref[...]
    xr = x * _rms_scale(x)
    kv = _bdot(xr * gkv_ref[...], wkv_ref[...])
    k = kv[:, :KV_WIDTH]
    k = k * lax.rsqrt(_seg_mean(k * k, bdk_ref[...]) + EPS) * kng_ref[...]
    k_ref[...] = k
    kt_ref[...] = k.T.astype(BF16)
    v_ref[...] = kv[:, KV_WIDTH:]
    proj = _bdot(xr * gmix_ref[...], win_ref[...])
    q = proj[:, :MAIN_WIDTH]
    ms = None
    for p in _split(q * q, 2):
        t = jnp.dot(p, hsum_ref[...], preferred_element_type=F32)
        ms = t if ms is None else ms + t
    scale = None
    for p in _split(lax.rsqrt(ms + EPS), 2):
        t = jnp.dot(p, hexp_ref[...], preferred_element_type=F32)
        scale = t if scale is None else scale + t
    q_ref[...] = (q * scale * qng_ref[...]).astype(BF16)
    mq_ref[...] = proj[:, MAIN_WIDTH:]


def _swa_perm():
    g, kh, dd = np.meshgrid(np.arange(SWA_GROUP), np.arange(SWA_KV_HEADS), np.arange(HEAD_DIM), indexing="ij")
    return ((kh * SWA_GROUP + g) * HEAD_DIM + dd).reshape(-1)


def _inproj_b(x, g_kv, g_mix, w_kv, w_in, kng, qng):
    n, d = x.shape
    tm = _row_tile(n, 1024)
    perm = _swa_perm()
    w_in_p = jnp.concatenate([w_in[:, :MAIN_WIDTH][:, perm], w_in[:, MAIN_WIDTH:]], axis=1).astype(BF16)
    qng_t = (jnp.tile(qng, SWA_HEADS) * HEAD_DIM ** -0.5).reshape(1, MAIN_WIDTH)
    member = (np.arange(MAIN_WIDTH)[:, None] // HEAD_DIM == np.arange(LANES)[None, :]).astype(np.float32)
    row = lambda w: pl.BlockSpec((tm, w), lambda i: (i, 0))
    return pl.pallas_call(
        _inproj_b_kernel,
        grid=(n // tm,),
        in_specs=[row(d), _const((1, d)), _const((1, d)), _const((d, 2 * KV_WIDTH)), _const((d, d)),
                  _const((1, KV_WIDTH)), _const((1, MAIN_WIDTH)), _const((KV_WIDTH, KV_WIDTH)),
                  _const((MAIN_WIDTH, LANES)), _const((LANES, MAIN_WIDTH))],
        out_specs=[row(MAIN_WIDTH), row(MEM_WIDTH), row(KV_WIDTH), row(KV_WIDTH),
                   pl.BlockSpec((KV_WIDTH, tm), lambda i: (0, i))],
        out_shape=[jax.ShapeDtypeStruct((n, MAIN_WIDTH), BF16), jax.ShapeDtypeStruct((n, MEM_WIDTH), F32),
                   jax.ShapeDtypeStruct((n, KV_WIDTH), F32), jax.ShapeDtypeStruct((n, KV_WIDTH), F32),
                   jax.ShapeDtypeStruct((KV_WIDTH, n), BF16)],
        compiler_params=_params(("parallel",)),
        name="inproj_b",
    )(x, g_kv.reshape(1, d), g_mix.reshape(1, d), w_kv.astype(BF16), w_in_p,
      jnp.tile(kng, SWA_KV_HEADS).reshape(1, KV_WIDTH), qng_t,
      _block_diag_mean(KV_WIDTH), jnp.asarray(member / HEAD_DIM, BF16), jnp.asarray(member.T, BF16))


def _softmax_with_sink(s, sink):
    m = jnp.maximum(jnp.max(s, axis=-1, keepdims=True), sink)
    e = jnp.exp(s - m)
    r = 1.0 / (jnp.sum(e, axis=-1, keepdims=True) + jnp.exp(sink - m))
    return (e * r).astype(BF16)


def _swa_bias(tq):
    slopes = 2.0 ** (-8.0 * np.arange(1, SWA_HEADS + 1, dtype=np.float64) / SWA_HEADS)
    dist = np.arange(tq)[:, None] + WINDOW - np.arange(WINDOW + tq)[None, :]
    valid = (dist >= 0) & (dist <= WINDOW)
    return np.stack([np.where(valid, -s * dist, NEG_BIG) for s in slopes]).astype(np.float32)


def _swa_prompt_kernel(sink_ref, q_ref, ktp_ref, kto_ref, vp_ref, vo_ref, bias_ref, hm_ref, o_ref, *, nb):
    w = WINDOW
    key = lax.broadcasted_iota(jnp.int32, (w, 2 * w), 1)
    has_prev = (pl.program_id(0) > 0) | (key >= w)
    heads = [(g, kh) for g in range(SWA_GROUP) for kh in range(SWA_KV_HEADS)]
    kts, vvs = [], []
    for b in range(nb):
        kt_prev = ktp_ref[...] if b == 0 else kto_ref[:, (b - 1) * w:b * w]
        v_prev = vp_ref[...] if b == 0 else vo_ref[(b - 1) * w:b * w, :]
        kts.append(jnp.concatenate([kt_prev, kto_ref[:, b * w:(b + 1) * w]], axis=1).astype(BF16))
        vvs.append(jnp.concatenate([v_prev, vo_ref[b * w:(b + 1) * w, :]], axis=0).astype(BF16))
    scores = [[jnp.dot(q_ref[b * w:(b + 1) * w, g * KV_WIDTH:(g + 1) * KV_WIDTH] * hm_ref[kh].astype(BF16),
                       kts[b], preferred_element_type=F32) for g, kh in heads] for b in range(nb)]
    for b in range(nb):
        probs = []
        for (g, kh), s in zip(heads, scores[b]):
            h = kh * SWA_GROUP + g
            s = s + bias_ref[h]
            if b == 0:
                s = jnp.where(has_prev, s, NEG_BIG)
            probs.append(_softmax_with_sink(s, sink_ref[h]))
        outs = [jnp.dot(p, vvs[b], preferred_element_type=F32) for p in probs]
        lane_head = lax.broadcasted_iota(jnp.int32, (w, KV_WIDTH), 1) // HEAD_DIM
        for g in range(SWA_GROUP):
            acc = None
            for (cg, kh), o in zip(heads, outs):
                if cg == g:
                    acc = o if acc is None else jnp.where(lane_head == kh, o, acc)
            o_ref[b * w:(b + 1) * w, g * KV_WIDTH:(g + 1) * KV_WIDTH] = acc.astype(BF16)


def _swa_prompt(q, kt, v, sinks, *, n_rows, nb):
    w = WINDOW
    step = nb * w
    assert n_rows % step == 0
    prev = lambda j, sink: jnp.maximum(j * nb - 1, 0)
    return pl.pallas_call(
        functools.partial(_swa_prompt_kernel, nb=nb),
        grid_spec=pltpu.PrefetchScalarGridSpec(
            num_scalar_prefetch=1,
            grid=(n_rows // step,),
            in_specs=[pl.BlockSpec((step, MAIN_WIDTH), lambda j, sink: (j, 0)),
                      pl.BlockSpec((KV_WIDTH, w), lambda j, sink: (0, prev(j, sink))),
                      pl.BlockSpec((KV_WIDTH, step), lambda j, sink: (0, j)),
                      pl.BlockSpec((w, KV_WIDTH), lambda j, sink: (prev(j, sink), 0)),
                      pl.BlockSpec((step, KV_WIDTH), lambda j, sink: (j, 0)),
                      pl.BlockSpec((SWA_HEADS, w, 2 * w), lambda j, sink: (0, 0, 0)),
                      pl.BlockSpec((SWA_KV_HEADS, 1, KV_WIDTH), lambda j, sink: (0, 0, 0))],
            out_specs=pl.BlockSpec((step, MAIN_WIDTH), lambda j, sink: (j, 0)),
        ),
        out_shape=jax.ShapeDtypeStruct((n_rows, MAIN_WIDTH), BF16),
        compiler_params=_params(("arbitrary",)),
        name="swa_prompt",
    )(sinks.astype(F32), q, kt, kt, v, v, jnp.asarray(_swa_bias(w)), _head_masks(SWA_KV_HEADS))


def _swa_sample_kernel(q_ref, kp_ref, ko_ref, vp_ref, vo_ref, bias_ref, sink_ref, hm_ref, o_ref, *, nb, tq):
    w = WINDOW
    heads = [(kh, g) for kh in range(SWA_KV_HEADS) for g in range(SWA_GROUP)]
    kks, vvs, scores = [], [], []
    q = q_ref[...].astype(F32)
    for i in range(nb):
        win = slice(i * KV_WIDTH, (i + 1) * KV_WIDTH)
        kks.append(jnp.concatenate([kp_ref[win, :].T, ko_ref[i * tq:(i + 1) * tq, :]], axis=0))
        vvs.append(jnp.concatenate([vp_ref[win, :].T, vo_ref[i * tq:(i + 1) * tq, :]], axis=0))
        qs = jnp.concatenate([q[i * tq:(i + 1) * tq, g * KV_WIDTH:(g + 1) * KV_WIDTH] * hm_ref[kh]
                              for kh, g in heads], axis=0)
        scores.append(_bdot_nt(qs, kks[i]))
    probs = [_softmax_with_sink(s + bias_ref[...], sink_ref[...]) for s in scores]
    outs = [_bdot(p, vv) for p, vv in zip(probs, vvs)]
    for g in range(SWA_GROUP):
        rows = []
        for i in range(nb):
            acc = None
            for r, (kh, hg) in enumerate(heads):
                if hg == g:
                    t = outs[i][r * tq:(r + 1) * tq] * hm_ref[kh]
                    acc = t if acc is None else acc + t
            rows.append(acc)
        o_ref[:, g * KV_WIDTH:(g + 1) * KV_WIDTH] = jnp.concatenate(rows, axis=0).astype(BF16)


def _swa_sample(q, k_win, v_win, k, v, sinks, *, row_off, batch, tq, nb):
    w = WINDOW
    assert batch % nb == 0 and row_off % (nb * tq) == 0
    off = row_off // (nb * tq)
    bias = jnp.asarray(_swa_bias(tq).reshape(SWA_HEADS * tq, w + tq))
    sink_col = jnp.repeat(sinks.astype(F32), tq).reshape(SWA_HEADS * tq, 1)
    own = lambda width: pl.BlockSpec((nb * tq, width), lambda b: (off + b, 0))
    win = pl.BlockSpec((nb * KV_WIDTH, w), lambda b: (b, 0))
    return pl.pallas_call(
        functools.partial(_swa_sample_kernel, nb=nb, tq=tq),
        grid=(batch // nb,),
        in_specs=[own(MAIN_WIDTH), win, own(KV_WIDTH), win, own(KV_WIDTH), _const(bias.shape),
                  _const(sink_col.shape), _const((SWA_KV_HEADS, 1, KV_WIDTH))],
        out_specs=pl.BlockSpec((nb * tq, MAIN_WIDTH), lambda b: (b, 0)),
        out_shape=jax.ShapeDtypeStruct((batch * tq, MAIN_WIDTH), BF16),
        compiler_params=_params(("arbitrary",)),
        name="swa_sample",
    )(q, k_win, k, v_win, v, bias, sink_col, _head_masks(SWA_KV_HEADS))


def kernel(x_prompt, x_sample, state_gla, cache_win_k, cache_win_v, cache_mem_k, cache_mem_v, mem_prompt, norm_mix_g, norm_ffn_g, norm_mem_g, w_mem_kv, mem_qn_g, mem_kn_g, w_out, w_in_a, w_gate_lr, b_gate_lr, gla_norm_g, w_in_b, swa_qn_g, swa_sinks, norm_kv_g, w_kv, swa_kn_g, w_router_group, b_router_group, w_router_expert, b_router_expert, w_exp_gate, w_exp_up, w_exp_down):
    bp, tp, d = x_prompt.shape
    bs, ts, _ = x_sample.shape
    assert bp == 1 and tp % WINDOW == 0 and ts * (GLA_CHUNK // ts) == GLA_CHUNK
    n_p, n_s = bp * tp, bs * ts
    w_buf = cache_win_k.shape[1]
    assert w_buf == WINDOW
    x_p, x_s = x_prompt.reshape(n_p, d), x_sample.reshape(n_s, d)

    mem_k_p, mem_v_p = _mem_kv(mem_prompt, norm_mem_g, w_mem_kv, mem_kn_g)
    feature_major = lambda c: jnp.moveaxis(c, -3, -1).reshape(*c.shape[:-3], c.shape[-2] * c.shape[-1], c.shape[-3])
    cmk, cmv = feature_major(cache_mem_k), feature_major(cache_mem_v)

    def mem_attend(mq, l):
        tm_p = _row_tile(tp, 1024)
        mo_p = _mem_attn(mq, mem_k_p, mem_v_p, mem_qn_g[l], row_off=0, seq=tp, tm=tm_p, bb=1, layer=l)
        mo_s = _mem_attn(mq, cmk, cmv, mem_qn_g[l], row_off=n_p, seq=ts, tm=ts, bb=16, layer=l)
        return mo_p, mo_s

    tm = _row_tile(math.gcd(n_p, n_s), MOE_TILE)
    router = lambda l: (norm_ffn_g[l], w_router_group[l], b_router_group[l], w_router_expert[l],
                        b_router_expert[l])

    def moe(h, routed, l, split_rows=None):
        return _moe(h, routed, norm_ffn_g[l], w_exp_gate, w_exp_up, w_exp_down, l, tm, split_rows)

    q, k, la, v, og, mq = _inproj_a(x_p, x_s, norm_mix_g[0], w_in_a[0], w_gate_lr[0], b_gate_lr[0])
    zero_state = jnp.zeros((bp, GLA_HEADS, GLA_DK, GLA_DV), F32)
    n_sub = max(1, min(16, tp // GLA_CHUNK))
    main_p, gla_p = _gla(q, k, la, v, og, zero_state, gla_norm_g[0], row_off=0, seq=tp, n_seg=1, n_sub=n_sub)
    main_s, gla_s = _gla(q, k, la, v, og, state_gla[0], gla_norm_g[0], row_off=n_p, seq=ts,
                         n_seg=GLA_CHUNK // ts, n_sub=1)
    mo_p, mo_s = mem_attend(mq, 0)
    w_o = w_out[0]
    h, routed = _outproj((x_p, x_s), main_p, mo_p, main_s, mo_s, w_o[:MAIN_WIDTH].reshape(GLA_HEADS, GLA_DV, d),
                         w_o[MAIN_WIDTH:], router(0), tm)
    h = moe(h, routed, 0)

    q, mq, k_sh, v_sh, kt_sh = _inproj_b(h, norm_kv_g, norm_mix_g[1], w_kv, w_in_b[0], swa_kn_g, swa_qn_g[0])
    ck = feature_major(cache_win_k).reshape(bs * KV_WIDTH, w_buf)
    cv = feature_major(cache_win_v).reshape(bs * KV_WIDTH, w_buf)
    main_p = _swa_prompt(q, kt_sh, v_sh, swa_sinks[0], n_rows=n_p, nb=8)
    main_s = _swa_sample(q, ck, cv, k_sh, v_sh, swa_sinks[0], row_off=n_p, batch=bs, tq=ts, nb=16)
    mo_p, mo_s = mem_attend(mq, 1)
    w_o = w_out[1]
    h, routed = _outproj(h, main_p, mo_p, main_s, mo_s, w_o[:MAIN_WIDTH][_swa_perm()], w_o[MAIN_WIDTH:],
                         router(1), tm)
    y_p, y_s = moe(h, routed, 1, split_rows=n_p)

    y_prompt = y_p.reshape(bp, tp, d)
    y_sample = y_s.reshape(bs, ts, d)
    k_new = k_sh[n_p:].reshape(bs, ts, SWA_KV_HEADS, HEAD_DIM)
    v_new = v_sh[n_p:].reshape(bs, ts, SWA_KV_HEADS, HEAD_DIM)
    win_k_s = jnp.concatenate([cache_win_k, k_new], axis=1)[:, -w_buf:]
    win_v_s = jnp.concatenate([cache_win_v, v_new], axis=1)[:, -w_buf:]
    win_k_p = k_sh[n_p - WINDOW:n_p].reshape(bp, WINDOW, SWA_KV_HEADS, HEAD_DIM)
    win_v_p = v_sh[n_p - WINDOW:n_p].reshape(bp, WINDOW, SWA_KV_HEADS, HEAD_DIM)
    token_major = lambda c: jnp.moveaxis(c.reshape(*c.shape[:-2], MEM_HEADS, HEAD_DIM, c.shape[-1]), -1, -3)
    return (y_prompt, y_sample, gla_p[None], gla_s[None], win_k_p, win_v_p, win_k_s, win_v_s,
            token_major(mem_k_p), token_major(mem_v_p))
```

```python
import functools
import math

import numpy as np
import jax
import jax.numpy as jnp
from jax import lax
from jax.experimental import pallas as pl
from jax.experimental.pallas import tpu as pltpu

F32 = jnp.float32
BF16 = jnp.bfloat16

D_MODEL = 1024
MEM_HEADS = 4
HEAD_DIM = 64
MEM_WIDTH = MEM_HEADS * HEAD_DIM
MAIN_WIDTH = D_MODEL - MEM_WIDTH
GLA_HEADS = 4
GLA_DV = MAIN_WIDTH // GLA_HEADS
GLA_DK = GLA_DV // 2
GLA_DK_PAD = 128
GLA_KEY_WIDTH = GLA_HEADS * GLA_DK
GLA_KEY_PAD = GLA_HEADS * GLA_DK_PAD
GLA_GATE_RANK = 16
GLA_TAU = 16.0
GLA_CHUNK = 64
SWA_HEADS = MAIN_WIDTH // HEAD_DIM
SWA_KV_HEADS = 4
SWA_GROUP = SWA_HEADS // SWA_KV_HEADS
KV_WIDTH = SWA_KV_HEADS * HEAD_DIM
WINDOW = 128
N_GROUPS = 4
EXPERTS_PER_GROUP = 8
N_EXPERTS = N_GROUPS * EXPERTS_PER_GROUP
D_EXPERT = 512
EPS = 1e-6
LANES = 128
NEG_BIG = -1e30
VMEM_LIMIT = 56 * 1024 * 1024
MOE_TILE = 512
MOE_SUB = 128
ROUTER_LANE0 = N_GROUPS
ROUTER_META_ROWS = 8
ROUTER_ROWS = 40
SLABS = D_MODEL // LANES
PACKED_SLABS = SLABS // 2


def _bdot(a, b):
    return jnp.dot(a.astype(BF16), b.astype(BF16), preferred_element_type=F32)


def _bdot_nt(a, b):
    return lax.dot_general(a.astype(BF16), b.astype(BF16), (((1,), (1,)), ((), ())),
                           preferred_element_type=F32)


def _bdot_tn(a, b):
    return lax.dot_general(a.astype(BF16), b.astype(BF16), (((0,), (0,)), ((), ())),
                           preferred_element_type=F32)


def _split(x, n):
    parts = []
    for _ in range(n - 1):
        p = x.astype(BF16)
        parts.append(p)
        x = x - p.astype(F32)
    parts.append(x.astype(BF16))
    return parts


def _exact_left_dot(m, x, n=2):
    out = None
    for p in _split(x, n):
        t = jnp.dot(m, p, preferred_element_type=F32)
        out = t if out is None else out + t
    return out


def _seg_mean(x2, bd):
    out = None
    for p in _split(x2, 2):
        t = jnp.dot(p, bd, preferred_element_type=F32)
        out = t if out is None else out + t
    return out


def _rms_scale(x):
    return lax.rsqrt(jnp.mean(x * x, axis=-1, keepdims=True) + EPS)


def _row_tile(n, cap=512):
    t = cap
    while t > 8 and n % t:
        t //= 2
    assert n % t == 0, n
    return t


def _params(sem):
    return pltpu.CompilerParams(dimension_semantics=sem, vmem_limit_bytes=VMEM_LIMIT)


def _const(shape):
    nd = len(shape)
    return pl.BlockSpec(shape, lambda *_: (0,) * nd)


def _group_specs(tm, width, prompt_tiles, lead=None):
    p_idx = lambda i, *_: jnp.minimum(i, prompt_tiles - 1)
    s_idx = lambda i, *_: jnp.maximum(i - prompt_tiles, 0)
    if lead is None:
        return [pl.BlockSpec((tm, width), lambda i, *_, f=f: (f(i), 0)) for f in (p_idx, s_idx)]
    return [pl.BlockSpec((lead, tm, width), lambda i, *_, f=f: (0, f(i), 0)) for f in (p_idx, s_idx)]


def _block_diag_mean(width):
    i = np.arange(width)
    return jnp.asarray((i[:, None] // HEAD_DIM == i[None, :] // HEAD_DIM) / HEAD_DIM, BF16)


def _head_masks(n_heads):
    i = np.arange(n_heads * HEAD_DIM)
    return jnp.asarray((i[None, :] // HEAD_DIM == np.arange(n_heads)[:, None]), F32)[:, None, :]


def _mem_kv_kernel(mem_ref, g_ref, w_ref, kng_ref, bd_ref, k_ref, v_ref):
    x = mem_ref[0]
    hn = x * _rms_scale(x) * g_ref[0]
    kv = _bdot(hn, w_ref[0])
    k = kv[:, :MEM_WIDTH]
    k = k * lax.rsqrt(_seg_mean(k * k, bd_ref[...]) + EPS) * kng_ref[0]
    k_ref[0, 0] = k.T
    v_ref[0, 0] = kv[:, MEM_WIDTH:].T


def _mem_kv(mem, g, w, kng):
    depth, (b, m, d) = w.shape[0], mem.shape
    out = jax.ShapeDtypeStruct((depth, b, m, MEM_WIDTH), F32)
    blk = pl.BlockSpec((1, 1, m, MEM_WIDTH), lambda l, i: (l, i, 0, 0))
    return pl.pallas_call(
        _mem_kv_kernel,
        grid=(depth, b),
        in_specs=[pl.BlockSpec((1, m, d), lambda l, i: (i, 0, 0)),
                  pl.BlockSpec((1, 1, d), lambda l, i: (l, 0, 0)),
                  pl.BlockSpec((1, d, 2 * MEM_WIDTH), lambda l, i: (l, 0, 0)),
                  pl.BlockSpec((1, 1, MEM_WIDTH), lambda l, i: (l, 0, 0)),
                  _const((MEM_WIDTH, MEM_WIDTH))],
        out_specs=[blk, blk],
        out_shape=[out, out],
        compiler_params=_params(("arbitrary", "arbitrary")),
        name="mem_kv",
    )(mem, g.reshape(depth, 1, d), w.astype(BF16),
      jnp.tile(kng, (1, MEM_HEADS)).reshape(depth, 1, MEM_WIDTH), _block_diag_mean(MEM_WIDTH))


def _inproj_a_kernel(xp_ref, xs_ref, g_ref, wq_ref, wk_ref, wv_ref, wog_ref, wlr_ref, wmq_ref, wgl_ref, bgl_ref,
                     q_ref, k_ref, la_ref, v_ref, og_ref, mq_ref, *, prompt_tiles):
    is_prompt = pl.program_id(0) < prompt_tiles
    tm = q_ref.shape[0]
    sub = min(tm, 128)
    for r in range(0, tm, sub):
        rows = slice(r, r + sub)
        x = jnp.where(is_prompt, xp_ref[rows, :], xs_ref[rows, :])
        hn = (x * _rms_scale(x) * g_ref[...]).astype(BF16)
        q_ref[rows, :] = jnp.dot(hn, wq_ref[...], preferred_element_type=F32) * (GLA_DK ** -0.5)
        k_ref[rows, :] = jnp.dot(hn, wk_ref[...], preferred_element_type=F32)
        for h in range(GLA_HEADS):
            v_ref[h, rows, :] = jnp.dot(hn, wv_ref[h], preferred_element_type=F32).astype(BF16)
            og_ref[h, rows, :] = jnp.dot(hn, wog_ref[h], preferred_element_type=F32)
        lr = jnp.dot(hn, wlr_ref[...], preferred_element_type=F32)
        z = _bdot(lr, wgl_ref[...]) + bgl_ref[...]
        la_ref[rows, :] = (jnp.minimum(z, 0.0) - jnp.log(1.0 + jnp.exp(-jnp.abs(z)))) * (1.0 / GLA_TAU)
        mq_ref[rows, :] = jnp.dot(hn, wmq_ref[...], preferred_element_type=F32)


def _pad_heads(w, width, pad):
    lead = w.shape[:-1]
    w = w.reshape(*lead, GLA_HEADS, width)
    w = jnp.pad(w, [(0, 0)] * len(lead) + [(0, 0), (0, pad - width)])
    return w.reshape(*lead, GLA_HEADS * pad)


def _inproj_a(x_p, x_s, g, w_in, w_lr, b_lr):
    (n_p, d), n_s = x_p.shape, x_s.shape[0]
    n = n_p + n_s
    tm = _row_tile(n_s)
    assert n_p % tm == 0
    pt = n_p // tm
    c0, c1, c2, c3, c4 = (GLA_KEY_WIDTH, 2 * GLA_KEY_WIDTH, 2 * GLA_KEY_WIDTH + MAIN_WIDTH,
                          2 * GLA_KEY_WIDTH + 2 * MAIN_WIDTH,
                          2 * GLA_KEY_WIDTH + 2 * MAIN_WIDTH + GLA_GATE_RANK)
    wb = w_in.astype(BF16)
    wq = _pad_heads(wb[:, :c0], GLA_DK, GLA_DK_PAD)
    wk = _pad_heads(wb[:, c0:c1], GLA_DK, GLA_DK_PAD)
    wv = wb[:, c1:c2].reshape(d, GLA_HEADS, GLA_DV).transpose(1, 0, 2)
    wog = wb[:, c2:c3].reshape(d, GLA_HEADS, GLA_DV).transpose(1, 0, 2)
    wlr = jnp.pad(wb[:, c3:c4], ((0, 0), (0, LANES - GLA_GATE_RANK)))
    wmq = wb[:, c4:]
    wgl = jnp.pad(_pad_heads(w_lr.astype(BF16), GLA_DK, GLA_DK_PAD), ((0, LANES - GLA_GATE_RANK), (0, 0)))
    bgl = _pad_heads(b_lr.reshape(1, -1), GLA_DK, GLA_DK_PAD)
    row = lambda w: pl.BlockSpec((tm, w), lambda i: (i, 0))
    hrow = pl.BlockSpec((GLA_HEADS, tm, GLA_DV), lambda i: (0, i, 0))
    key = jax.ShapeDtypeStruct((n, GLA_KEY_PAD), F32)
    val = jax.ShapeDtypeStruct((GLA_HEADS, n, GLA_DV), F32)
    return pl.pallas_call(
        functools.partial(_inproj_a_kernel, prompt_tiles=pt),
        grid=(n // tm,),
        in_specs=_group_specs(tm, d, pt) + [
            _const((1, d)), _const(wq.shape), _const(wk.shape), _const(wv.shape),
            _const(wog.shape), _const(wlr.shape), _const(wmq.shape), _const(wgl.shape),
            _const(bgl.shape)],
        out_specs=[row(GLA_KEY_PAD), row(GLA_KEY_PAD), row(GLA_KEY_PAD), hrow, hrow, row(MEM_WIDTH)],
        out_shape=[key, key, key, jax.ShapeDtypeStruct(val.shape, BF16), val,
                   jax.ShapeDtypeStruct((n, MEM_WIDTH), F32)],
        compiler_params=_params(("parallel",)),
        name="inproj_a",
    )(x_p, x_s, g.reshape(1, d), wq, wk, wv, wog, wlr, wmq, wgl, bgl)


def _gla_kernel(q_ref, k_ref, la_ref, v_ref, og_ref, s0_ref, gn_ref, mcum_ref, mall_ref, sel_ref,
                o_ref, sout_ref, s_ref, *, chunk, n_sub, n_seg):
    j = pl.program_id(1)
    seg = chunk // n_seg

    @pl.when(j == 0)
    def _():
        s_ref[...] = jnp.zeros_like(s_ref)
        s_ref[:, :, :GLA_DK, :] = s0_ref[...]

    mcum = mcum_ref[...]
    causal = mcum.astype(F32) > 0.0
    row = lax.broadcasted_iota(jnp.int32, (chunk, GLA_DK_PAD), 0)
    gn = gn_ref[...]
    hcols = [slice(h * GLA_DK_PAD, (h + 1) * GLA_DK_PAD) for h in range(GLA_HEADS)]
    crows = [slice(c * chunk, (c + 1) * chunk) for c in range(n_sub)]
    qts, kts, kds, e_ends = [], [], [], []
    for rows in crows:
        la = la_ref[rows, :]
        b = _exact_left_dot(mcum, la)
        if n_seg == 1:
            b_end = b[chunk - 1:chunk, :]
            e_ends.append(jnp.broadcast_to(jnp.exp(b_end), (LANES, b.shape[1])).T)
        else:
            b_end = _exact_left_dot(mall_ref[...], la)
            e_ends.append(jnp.exp(_exact_left_dot(sel_ref[...], la)).T)
        k = k_ref[rows, :]
        qts.append(q_ref[rows, :] * jnp.exp(b))
        kts.append((k * jnp.exp(-b)).astype(BF16))
        kds.append(k * jnp.exp(b_end - b))
    vbs = [[v_ref[h, rows, :].astype(BF16) for h in range(GLA_HEADS)] for rows in crows]
    scores = [[_bdot_nt(qts[c][:, cols], kts[c][:, cols]) for cols in hcols] for c in range(n_sub)]
    kvs = []
    for c in range(n_sub):
        per_head = []
        for h, cols in enumerate(hcols):
            per_seg = []
            for s in range(n_seg):
                kd = kds[c][:, cols]
                if n_seg > 1:
                    kd = jnp.where((row >= s * seg) & (row < (s + 1) * seg), kd, 0.0)
                per_seg.append(_bdot_tn(kd, vbs[c][h]))
            per_head.append(per_seg)
        kvs.append(per_head)
    state = [[s_ref[s, h] for s in range(n_seg)] for h in range(GLA_HEADS)]
    inters = []
    for c in range(n_sub):
        per_head = []
        for h, cols in enumerate(hcols):
            parts = []
            for s in range(n_seg):
                parts.append(_bdot(qts[c][s * seg:(s + 1) * seg, cols], state[h][s]))
                state[h][s] = e_ends[c][cols, s:s + 1] * state[h][s] + kvs[c][h][s]
            per_head.append(parts[0] if n_seg == 1 else jnp.concatenate(parts, axis=0))
        inters.append(per_head)
    for h in range(GLA_HEADS):
        for s in range(n_seg):
            s_ref[s, h] = state[h][s]
    for c, rows in enumerate(crows):
        for h in range(GLA_HEADS):
            a = jnp.where(causal, scores[c][h], 0.0)
            o = _bdot(a, vbs[c][h]) + inters[c][h]
            on = o * lax.rsqrt(jnp.mean(o * o, axis=-1, keepdims=True) + EPS) * gn
            og = og_ref[h, rows, :]
            o_ref[h, rows, :] = (on * (og * jax.nn.sigmoid(og))).astype(BF16)

    @pl.when(j == pl.num_programs(1) - 1)
    def _():
        sout_ref[...] = s_ref[:, :, :GLA_DK, :]


def _gla(q, k, la, v, og, s0, gnorm, *, row_off, seq, n_seg, n_sub):
    batch = s0.shape[0]
    chunk = GLA_CHUNK
    assert chunk % n_seg == 0 and batch % n_seg == 0
    seg = chunk // n_seg
    step_rows = n_sub * chunk
    if n_seg > 1:
        assert seq == seg and n_sub == 1
        t_steps = 1
    else:
        assert seq % step_rows == 0
        t_steps = seq // step_rows
    assert row_off % step_rows == 0
    off = row_off // step_rows
    i = np.arange(chunk)
    same = (i[:, None] // seg) == (i[None, :] // seg)
    mcum = jnp.asarray(same & (i[None, :] <= i[:, None]), BF16)
    mall = jnp.asarray(same, BF16)
    sel = jnp.asarray((i[None, :] // seg) == np.arange(LANES)[:, None], BF16)
    ridx = lambda g, j: (off + g * t_steps + j, 0)
    hidx = lambda g, j: (0, off + g * t_steps + j, 0)
    key_spec = pl.BlockSpec((step_rows, GLA_KEY_PAD), ridx)
    val_spec = pl.BlockSpec((GLA_HEADS, step_rows, GLA_DV), hidx)
    st_spec = pl.BlockSpec((n_seg, GLA_HEADS, GLA_DK, GLA_DV), lambda g, j: (g, 0, 0, 0))
    in_specs = [key_spec, key_spec, key_spec, val_spec, val_spec, st_spec, _const((1, GLA_DV)),
                _const((chunk, chunk)), _const((chunk, chunk)), _const((LANES, chunk))]
    args = [q, k, la, v, og, s0, gnorm.reshape(1, GLA_DV), mcum, mall, sel]
    out_spec = pl.BlockSpec((GLA_HEADS, step_rows, GLA_DV), lambda g, j: (0, g * t_steps + j, 0))
    return pl.pallas_call(
        functools.partial(_gla_kernel, chunk=chunk, n_sub=n_sub, n_seg=n_seg),
        grid=(batch // n_seg, t_steps),
        in_specs=in_specs,
        out_specs=[out_spec, st_spec],
        out_shape=[jax.ShapeDtypeStruct((GLA_HEADS, batch * seq, GLA_DV), BF16),
                   jax.ShapeDtypeStruct(s0.shape, F32)],
        scratch_shapes=[pltpu.VMEM((n_seg, GLA_HEADS, GLA_DK_PAD, GLA_DV), F32)],
        compiler_params=_params(("arbitrary", "arbitrary")),
        name="gla",
    )(*args)


def _mem_attn_kernel(q_ref, k_ref, v_ref, g_ref, bd_ref, hm_ref, o_ref, *, tm, bb):
    g = g_ref[...]
    sub = min(tm, 128)
    units = [(i, i * tm + r) for i in range(bb) for r in range(0, tm, sub)]
    scores = []
    for i, r in units:
        q = q_ref[r:r + sub, :]
        qn = q * lax.rsqrt(_seg_mean(q * q, bd_ref[...]) + EPS) * g
        qs = jnp.concatenate([(qn * hm_ref[h]).astype(BF16) for h in range(MEM_HEADS)], axis=0)
        scores.append(_bdot(qs, k_ref[i]))
    probs = []
    for s in scores:
        e = jnp.exp(s - jnp.max(s, axis=-1, keepdims=True))
        probs.append(e * (1.0 / jnp.sum(e, axis=-1, keepdims=True)))
    outs = [_bdot_nt(p, v_ref[i]) for (i, _), p in zip(units, probs)]
    rows = []
    for o in outs:
        acc = o[:sub] * hm_ref[0]
        for h in range(1, MEM_HEADS):
            acc = acc + o[h * sub:(h + 1) * sub] * hm_ref[h]
        rows.append(acc)
    o_ref[...] = jnp.concatenate(rows, axis=0).astype(BF16)


def _mem_attn(mq, mk, mv, qng, *, row_off, seq, tm, bb, layer):
    depth, batch, m, _ = mk.shape
    mk = mk.reshape(depth * batch, m, MEM_WIDTH)
    mv = mv.reshape(depth * batch, m, MEM_WIDTH)
    kv_off = layer * batch // bb
    assert seq % tm == 0 and batch % bb == 0 and (bb == 1 or seq == tm)
    t_steps = seq // tm
    step_rows = bb * tm
    assert row_off % step_rows == 0
    off = row_off // step_rows
    row_spec = pl.BlockSpec((step_rows, MEM_WIDTH), lambda g, j: (off + g * t_steps + j, 0))
    kv_spec = pl.BlockSpec((bb, m, MEM_WIDTH), lambda g, j: (kv_off + g, 0, 0))
    in_specs = [row_spec, kv_spec, kv_spec, _const((1, MEM_WIDTH)), _const((MEM_WIDTH, MEM_WIDTH)),
                _const((MEM_HEADS, 1, MEM_WIDTH))]
    args = [mq, mk, mv, (jnp.tile(qng, MEM_HEADS) * HEAD_DIM ** -0.5).reshape(1, MEM_WIDTH),
            _block_diag_mean(MEM_WIDTH), _head_masks(MEM_HEADS)]
    return pl.pallas_call(
        functools.partial(_mem_attn_kernel, tm=tm, bb=bb),
        grid=(batch // bb, t_steps),
        in_specs=in_specs,
        out_specs=pl.BlockSpec((step_rows, MEM_WIDTH), lambda g, j: (g * t_steps + j, 0)),
        out_shape=jax.ShapeDtypeStruct((batch * seq, MEM_WIDTH), BF16),
        compiler_params=_params(("parallel", "parallel")),
        name="mem_attn",
    )(*args)


def _outproj_kernel(*refs, heads, prompt_tiles, split_residual):
    n_h = 2 if split_residual else 1
    h_refs, refs = refs[:n_h], refs[n_h:]
    (main_p_ref, main_s_ref, mo_p_ref, mo_s_ref, wmain_ref, wmo_ref), refs = refs[:6], refs[6:]
    route_in, (o_ref, *route_out) = refs[:6], refs[6:]
    is_prompt = pl.program_id(0) < prompt_tiles
    pick = lambda p, s: jnp.where(is_prompt, p, s)
    tm = o_ref.shape[0]
    sub = min(tm, 128)
    blocks = []
    for r in range(0, tm, sub):
        rows = slice(r, r + sub)
        acc = pick(h_refs[0][rows, :], h_refs[1][rows, :]) if split_residual else h_refs[0][rows, :]
        acc = acc + _bdot(pick(mo_p_ref[rows, :], mo_s_ref[rows, :]), wmo_ref[...])
        if heads:
            for h in range(heads):
                acc = acc + _bdot(pick(main_p_ref[h, rows, :], main_s_ref[h, rows, :]), wmain_ref[h])
        else:
            acc = acc + _bdot(pick(main_p_ref[rows, :], main_s_ref[rows, :]), wmain_ref[...])
        o_ref[rows, :] = acc
        blocks.append(acc)
    _route_tile(blocks, *route_in, *route_out)


def _outproj(h, main_p, mo_p, main_s, mo_s, w_main, w_mo, router, tm):
    n_p, n_s = mo_p.shape[0], mo_s.shape[0]
    n, d = n_p + n_s, w_mo.shape[1]
    assert n_p % tm == 0 and n_s % tm == 0
    pt = n_p // tm
    heads = main_p.shape[0] if main_p.ndim == 3 else 0
    row = pl.BlockSpec((tm, d), lambda i: (i, 0))
    split = isinstance(h, tuple)
    h_specs, h_args = (_group_specs(tm, d, pt), list(h)) if split else ([row], [h])
    main_specs = _group_specs(tm, main_p.shape[-1], pt, lead=heads or None)
    r_args, r_in_specs, r_out_specs, r_out_shape, r_scratch = _route_operands(*router, n, tm)
    h_new, *routed = pl.pallas_call(
        functools.partial(_outproj_kernel, heads=heads, prompt_tiles=pt, split_residual=split),
        grid=(n // tm,),
        in_specs=(h_specs + main_specs + _group_specs(tm, MEM_WIDTH, pt)
                  + [_const(w_main.shape), _const(w_mo.shape)] + r_in_specs),
        out_specs=[row] + r_out_specs,
        out_shape=[jax.ShapeDtypeStruct((n, d), F32)] + r_out_shape,
        scratch_shapes=[r_scratch],
        compiler_params=_params(("arbitrary",)),
        name="outproj_route",
    )(*h_args, main_p, main_s, mo_p, mo_s, w_main.astype(BF16), w_mo.astype(BF16), *r_args)
    return h_new, routed


def _route_tile(x_blocks, g_ref, whi_ref, wlo_ref, b_ref, before_tok_ref, before_row_ref,
                mi_ref, mf_ref, cnt_ref, tt_ref, carry_ref):
    @pl.when(pl.program_id(0) == 0)
    def _():
        carry_ref[...] = jnp.zeros_like(carry_ref)

    nt = lambda a, b: lax.dot_general(a, b, (((1,), (1,)), ((), ())), preferred_element_type=F32)
    logits = []
    for x in x_blocks:
        x_hi, x_lo = _split(x * _rms_scale(x) * g_ref[...], 2)
        logits.append((nt(whi_ref[...], x_hi) + nt(wlo_ref[...], x_hi) + nt(whi_ref[...], x_lo))[:ROUTER_ROWS])
    logits = jnp.concatenate(logits, axis=1) + b_ref[...]
    tm = logits.shape[1]
    row = lax.broadcasted_iota(jnp.int32, (ROUTER_ROWS, tm), 0)
    far = jnp.int32(2 * LANES)

    def first_max(vals):
        m = jnp.max(vals, axis=0, keepdims=True)
        return m, jnp.min(jnp.where(vals == m, row, far), axis=0, keepdims=True)

    gl = jnp.where(row < N_GROUPS, logits, -jnp.inf)
    gmax, grp = first_max(gl)
    pg_sel = 1.0 / jnp.sum(jnp.exp(gl - gmax), axis=0, keepdims=True)
    lo = ROUTER_LANE0 + grp * EXPERTS_PER_GROUP
    el = jnp.where((row >= lo) & (row < lo + EXPERTS_PER_GROUP), logits, -jnp.inf)
    m1, i1 = first_max(el)
    m2, i2 = first_max(jnp.where(row == i1, -jnp.inf, el))
    e2 = jnp.exp(m2 - m1)
    g1 = pg_sel / (1.0 + e2)
    g2 = pg_sel * e2 / (1.0 + e2)

    oh1 = row == i1
    oh2 = row == i2
    picked = jnp.where(oh1 | oh2, 1.0, 0.0)
    earlier = jnp.dot(picked.astype(BF16), before_tok_ref[...], preferred_element_type=F32)
    cnt_col = jnp.sum(picked, axis=1, keepdims=True)
    cnt_tile = jnp.concatenate([jnp.broadcast_to(cnt_col, (ROUTER_ROWS, LANES)),
                                jnp.zeros((LANES - ROUTER_ROWS, LANES), F32)], axis=0)
    c_hi = jnp.floor(cnt_tile * (1.0 / 32.0))
    c_lo = cnt_tile - 32.0 * c_hi
    first = (32.0 * jnp.dot(before_row_ref[...], c_hi.astype(BF16), preferred_element_type=F32)
             + jnp.dot(before_row_ref[...], c_lo.astype(BF16), preferred_element_type=F32))
    local = first[:ROUTER_ROWS, 0:1] + earlier
    lpos1 = jnp.sum(jnp.where(oh1, local, 0.0), axis=0, keepdims=True)
    lpos2 = jnp.sum(jnp.where(oh2, local, 0.0), axis=0, keepdims=True)
    carry_before = carry_ref[...]
    carry = carry_before + cnt_tile
    carry_ref[...] = carry

    lane = lax.broadcasted_iota(jnp.int32, (LANES, LANES), 1)
    cols = jnp.where(lane == 0, carry_before, jnp.where(lane == 1, cnt_tile, jnp.where(lane == 2, first,
                     jnp.where(lane == 3, carry, 0.0))))
    tables = cols.T
    tt_ref[...] = tables[:8].astype(jnp.int32)
    cnt_ref[...] = tables[3:4]
    row8 = lax.broadcasted_iota(jnp.int32, (ROUTER_META_ROWS, tm), 0)
    zero8 = jnp.zeros((ROUTER_META_ROWS, tm), F32)
    mi_ref[...] = jnp.where(row8 == 0, lpos1, jnp.where(row8 == 1, lpos2, zero8)).astype(jnp.int32)
    stacked = jnp.where(row8 == 0, g1, jnp.where(row8 == 1, g2, jnp.where(row8 == 2, lpos1,
                        jnp.where(row8 == 3, lpos2, zero8))))
    mf_ref[...] = jnp.concatenate([stacked, jnp.zeros((LANES - ROUTER_META_ROWS, tm), F32)], axis=0).T


def _route_operands(g, w_rg, b_rg, w_re, b_re, n, tm):
    d = g.shape[0]
    assert n % tm == 0 and 2 * tm <= 32 * 32
    n_real = N_GROUPS + N_EXPERTS
    w = jnp.pad(jnp.concatenate([w_rg, w_re], axis=1), ((0, 0), (0, LANES - n_real))).T
    b = jnp.pad(jnp.concatenate([b_rg, b_re]), (0, ROUTER_ROWS - n_real)).reshape(ROUTER_ROWS, 1)
    w_hi = w.astype(BF16)
    w_lo = (w - w_hi.astype(F32)).astype(BF16)
    i = np.arange(tm)
    before_tok = jnp.asarray(i[:, None] < i[None, :], BF16)
    e = np.arange(LANES)
    before_row = jnp.asarray(e[None, :] < e[:, None], BF16)
    args = [g.reshape(1, d), w_hi, w_lo, b, before_tok, before_row]
    in_specs = [_const(a.shape) for a in args]
    out_specs = [pl.BlockSpec((ROUTER_META_ROWS, tm), lambda i: (0, i)), pl.BlockSpec((tm, LANES), lambda i: (i, 0)),
                 _const((1, LANES)), pl.BlockSpec((8, LANES), lambda i: (i, 0))]
    out_shape = [jax.ShapeDtypeStruct((ROUTER_META_ROWS, n), jnp.int32), jax.ShapeDtypeStruct((n, LANES), F32),
                 jax.ShapeDtypeStruct((1, LANES), F32), jax.ShapeDtypeStruct((n // tm * 8, LANES), jnp.int32)]
    return args, in_specs, out_specs, out_shape, pltpu.VMEM((LANES, LANES), F32)


def _pack_rows(ref, x, rows, lead=(), row0=0):
    u32 = jnp.uint32
    for w in range(PACKED_SLABS):
        lo = x[:, (2 * w) * LANES:(2 * w + 1) * LANES].astype(BF16).astype(F32)
        hi = x[:, (2 * w + 1) * LANES:(2 * w + 2) * LANES].astype(BF16).astype(F32)
        word = (lax.bitcast_convert_type(lo, u32) >> 16) | (lax.bitcast_convert_type(hi, u32) & u32(0xFFFF0000))
        ref[lead + (pl.ds(row0 * PACKED_SLABS + w, rows, stride=PACKED_SLABS), slice(None))] = word


def _unpack_rows(ref, rows, lead=(), row0=0):
    u32 = jnp.uint32
    slabs = []
    for w in range(PACKED_SLABS):
        word = ref[lead + (pl.ds(row0 * PACKED_SLABS + w, rows, stride=PACKED_SLABS), slice(None))]
        slabs.append(lax.bitcast_convert_type(word << 16, F32).astype(BF16))
        slabs.append(lax.bitcast_convert_type(word & u32(0xFFFF0000), F32).astype(BF16))
    return jnp.concatenate(slabs, axis=1)


RUN_FIELDS = 3
RUN_CHUNK_BITS = 6


def _copy_runs(runs_ref, tile, local_rows, global_rows, sem, *, to_global):
    ps = PACKED_SLABS
    base = tile * (RUN_FIELDS * N_EXPERTS)

    def piece(g0, l0, off, size):
        g = global_rows(pl.multiple_of((g0 + off) * ps, ps), size * ps)
        l = local_rows(pl.multiple_of((l0 + off) * ps, ps), size * ps)
        src, dst = (l, g) if to_global else (g, l)
        pltpu.make_async_copy(src, dst, sem).start()

    def per_expert(e, carry):
        g0 = runs_ref[base + e]
        length = runs_ref[base + N_EXPERTS + e]
        l0 = runs_ref[base + 2 * N_EXPERTS + e]
        big = 1 << RUN_CHUNK_BITS

        def big_piece(c, inner):
            piece(g0, l0, c * big, big)
            return inner

        n_big = length >> RUN_CHUNK_BITS
        lax.fori_loop(0, n_big, big_piece, 0)
        off = n_big * big
        for bit in reversed(range(RUN_CHUNK_BITS)):
            size = 1 << bit

            @pl.when((length & size) != 0)
            def _(off=off, size=size):
                piece(g0, l0, off, size)

            off = off + (length & size)
        return carry

    lax.fori_loop(0, N_EXPERTS, per_expert, 0)


def _fill_pads(pads_ref, buf0, xs_hbm, sem, *, wait):
    ps = PACKED_SLABS

    def go(copy):
        copy.wait() if wait else copy.start()

    def per_tail_tile(t, carry):
        rows = MOE_TILE * ps
        go(pltpu.make_async_copy(
            buf0.at[pl.ds(0, rows)],
            xs_hbm.at[pl.ds(pl.multiple_of((pads_ref[2 * N_EXPERTS] + t) * rows, rows), rows)], sem))
        return carry

    lax.fori_loop(0, pads_ref[2 * N_EXPERTS + 1], per_tail_tile, 0)

    def per_expert(e, carry):
        first, length = pads_ref[e], pads_ref[N_EXPERTS + e]
        off = 0
        for bit in reversed(range((MOE_TILE - 1).bit_length())):
            size = 1 << bit

            @pl.when((length & size) != 0)
            def _(off=off, size=size):
                go(pltpu.make_async_copy(
                    buf0.at[pl.ds(pl.multiple_of(off * ps, ps), size * ps)],
                    xs_hbm.at[pl.ds(pl.multiple_of((first + off) * ps, ps), size * ps)], sem))

            off = off + (length & size)
        return carry

    lax.fori_loop(0, N_EXPERTS, per_expert, 0)


def _dispatch_kernel(runs_ref, pads_ref, h_ref, g_ref, meta_ref, xs_hbm, buf, sem, pad_sem, *, tm, steps):
    i = pl.program_id(0)
    slot = lax.rem(i, 2)
    ns = 2 * tm
    assert ns >= MOE_TILE

    def wait_slot(sl):
        pltpu.make_async_copy(buf.at[sl], xs_hbm.at[pl.ds(0, ns * PACKED_SLABS)], sem.at[sl]).wait()

    @pl.when(i >= 2)
    def _():
        wait_slot(slot)

    x = h_ref[...]
    xn = (x * _rms_scale(x) * g_ref[...]).astype(BF16)
    j = lax.broadcasted_iota(jnp.int32, (ns, tm), 0)
    pick = jnp.where((j == meta_ref[0:1, :]) | (j == meta_ref[1:2, :]), 1.0, 0.0).astype(BF16)
    _pack_rows(buf, jnp.dot(pick, xn, preferred_element_type=F32), ns, (slot,))
    _copy_runs(runs_ref, i, lambda start, size: buf.at[slot, pl.ds(start, size)],
               lambda start, size: xs_hbm.at[pl.ds(start, size)], sem.at[slot], to_global=True)

    @pl.when(i == 0)
    def _():
        _fill_pads(pads_ref, buf.at[0], xs_hbm, pad_sem, wait=False)

    @pl.when(i == min(1, steps - 1))
    def _():
        _fill_pads(pads_ref, buf.at[0], xs_hbm, pad_sem, wait=True)

    @pl.when(i == steps - 1)
    def _():
        wait_slot(slot)
        if steps > 1:
            wait_slot(1 - slot)


def _dispatch(h, g, meta, runs, pads, tm, n_slots):
    n, d = h.shape
    steps = n // tm
    return pl.pallas_call(
        functools.partial(_dispatch_kernel, tm=tm, steps=steps),
        grid_spec=pltpu.PrefetchScalarGridSpec(
            num_scalar_prefetch=2,
            grid=(steps,),
            in_specs=[pl.BlockSpec((tm, d), lambda i, *_: (i, 0)),
                      pl.BlockSpec((1, d), lambda i, *_: (0, 0)),
                      pl.BlockSpec((ROUTER_META_ROWS, tm), lambda i, *_: (0, i))],
            out_specs=pl.BlockSpec(memory_space=pl.ANY),
            scratch_shapes=[pltpu.VMEM((2, 2 * tm * PACKED_SLABS, LANES), jnp.uint32),
                            pltpu.SemaphoreType.DMA((2,)), pltpu.SemaphoreType.DMA],
        ),
        out_shape=jax.ShapeDtypeStruct((n_slots * PACKED_SLABS, LANES), jnp.uint32),
        compiler_params=_params(("arbitrary",)),
        name="moe_dispatch",
    )(runs, pads, h, g.reshape(1, d), meta)


def _expert_kernel(exp_ref, new_ref, wslot_ref, next_ref, n_ref,
                   xs_hbm, wg_hbm, wu_hbm, wd_hbm, ys_hbm,
                   xbuf, ybuf, xsem, ysem, wgf, wuf, wdf, wsem, wgb, wub, wdb, *, tm, sub, layer, max_tiles):
    n = n_ref[0]
    rows = tm * PACKED_SLABS
    tile_rows = lambda hbm, t: hbm.at[pl.ds(pl.multiple_of(t * rows, rows), rows)]
    x_copy = lambda t, slot: pltpu.make_async_copy(tile_rows(xs_hbm, t), xbuf.at[slot], xsem.at[slot])
    y_copy = lambda t, slot: pltpu.make_async_copy(ybuf.at[slot], tile_rows(ys_hbm, t), ysem.at[slot])

    def weight_copies(e, slot):
        return [pltpu.make_async_copy(hbm.at[layer, e], buf.at[slot], wsem.at[slot])
                for hbm, buf in ((wg_hbm, wgf), (wu_hbm, wuf), (wd_hbm, wdf))]

    x_copy(0, 0).start()

    def item(w, carry):
        io = lax.rem(w, 2)
        x_copy(w, io).wait()

        @pl.when(w + 1 < n)
        def _():
            x_copy(w + 1, 1 - io).start()

        @pl.when(w >= 2)
        def _():
            y_copy(w - 2, io).wait()

        @pl.when(new_ref[w] != 0)
        def _():
            slot = wslot_ref[w]

            @pl.when(w == 0)
            def _():
                for c in weight_copies(exp_ref[w], slot):
                    c.start()

            for c in weight_copies(exp_ref[w], slot):
                c.wait()
            wgb[...] = wgf[slot].astype(BF16)
            wub[...] = wuf[slot].astype(BF16)
            wdb[...] = wdf[slot].astype(BF16)

            @pl.when(next_ref[w] >= 0)
            def _():
                for c in weight_copies(next_ref[w], 1 - slot):
                    c.start()

        n_blocks = tm // sub

        def up(s):
            x = _unpack_rows(xbuf, sub, (io,), row0=s * sub)
            return (jnp.dot(x, wgb[...], preferred_element_type=F32),
                    jnp.dot(x, wub[...], preferred_element_type=F32))

        ups = {0: up(0)}
        for s in range(n_blocks):
            if s + 1 < n_blocks:
                ups[s + 1] = up(s + 1)
            hg, hu = ups.pop(s)
            act = (hg * jax.nn.sigmoid(hg) * hu).astype(BF16)
            y = jnp.dot(act, wdb[...], preferred_element_type=F32)
            _pack_rows(ybuf, y, sub, (io,), row0=s * sub)
        y_copy(w, io).start()
        return carry

    lax.fori_loop(0, n, item, 0)

    @pl.when(n >= 2)
    def _():
        y_copy(n - 2, lax.rem(n, 2)).wait()
    y_copy(n - 1, lax.rem(n - 1, 2)).wait()

    for wait in (False, True):
        def spare(t, carry, wait=wait):
            copy = y_copy(n + t, 0)
            copy.wait() if wait else copy.start()
            return carry

        lax.fori_loop(0, max_tiles - n, spare, 0)


def _experts(xs, items, w_g, w_u, w_d, layer):
    d = D_MODEL
    tm = MOE_TILE
    max_tiles = items[0].shape[0]
    any_spec = pl.BlockSpec(memory_space=pl.ANY)
    io_buf = pltpu.VMEM((2, tm * PACKED_SLABS, LANES), jnp.uint32)
    return pl.pallas_call(
        functools.partial(_expert_kernel, tm=tm, sub=MOE_SUB, layer=layer, max_tiles=max_tiles),
        grid_spec=pltpu.PrefetchScalarGridSpec(
            num_scalar_prefetch=len(items),
            grid=(1,),
            in_specs=[any_spec, any_spec, any_spec, any_spec],
            out_specs=any_spec,
            scratch_shapes=[io_buf, io_buf, pltpu.SemaphoreType.DMA((2,)), pltpu.SemaphoreType.DMA((2,)),
                            pltpu.VMEM((2, d, D_EXPERT), F32), pltpu.VMEM((2, d, D_EXPERT), F32),
                            pltpu.VMEM((2, D_EXPERT, d), F32), pltpu.SemaphoreType.DMA((2,)),
                            pltpu.VMEM((d, D_EXPERT), BF16), pltpu.VMEM((d, D_EXPERT), BF16),
                            pltpu.VMEM((D_EXPERT, d), BF16)],
        ),
        out_shape=jax.ShapeDtypeStruct(xs.shape, jnp.uint32),
        compiler_params=_params(("arbitrary",)),
        name="moe_experts",
    )(*items, xs, w_g, w_u, w_d)


def _combine_kernel(runs_ref, h_ref, gate_ref, ys_hbm, *refs, tm, steps, prompt_tiles):
    out_refs, (buf, sem) = refs[:-2], refs[-2:]
    i = pl.program_id(0)
    slot = lax.rem(i, 2)
    ns = 2 * tm

    def issue(step, sl):
        _copy_runs(runs_ref, step, lambda start, size: buf.at[sl, pl.ds(start, size)],
                   lambda start, size: ys_hbm.at[pl.ds(start, size)], sem.at[sl], to_global=False)

    @pl.when(i == 0)
    def _():
        issue(0, 0)

    @pl.when(i + 1 < steps)
    def _():
        issue(i + 1, 1 - slot)

    pltpu.make_async_copy(ys_hbm.at[pl.ds(0, ns * PACKED_SLABS)], buf.at[slot], sem.at[slot]).wait()
    y = _unpack_rows(buf, ns, (slot,))
    g = gate_ref[...]
    j = lax.broadcasted_iota(jnp.int32, (tm, ns), 1)
    mix = (jnp.where(j == g[:, 2:3].astype(jnp.int32), g[:, 0:1], 0.0)
           + jnp.where(j == g[:, 3:4].astype(jnp.int32), g[:, 1:2], 0.0)).astype(BF16)
    out = h_ref[...] + jnp.dot(mix, y, preferred_element_type=F32)
    if len(out_refs) == 1:
        out_refs[0][...] = out
    else:
        @pl.when(i < prompt_tiles)
        def _():
            out_refs[0][...] = out

        @pl.when(i >= prompt_tiles)
        def _():
            out_refs[1][...] = out


def _combine(h, gates, ys, runs, tm, split_rows=None):
    n, d = h.shape
    steps = n // tm
    row = pl.BlockSpec((tm, d), lambda i, pos: (i, 0))
    if split_rows is None:
        pt, out_specs, out_shape = 0, row, jax.ShapeDtypeStruct((n, d), F32)
    else:
        assert split_rows % tm == 0
        pt = split_rows // tm
        out_specs = _group_specs(tm, d, pt)
        out_shape = [jax.ShapeDtypeStruct((split_rows, d), F32), jax.ShapeDtypeStruct((n - split_rows, d), F32)]
    return pl.pallas_call(
        functools.partial(_combine_kernel, tm=tm, steps=steps, prompt_tiles=pt),
        grid_spec=pltpu.PrefetchScalarGridSpec(
            num_scalar_prefetch=1,
            grid=(steps,),
            in_specs=[row, pl.BlockSpec((tm, LANES), lambda i, pos: (i, 0)),
                      pl.BlockSpec(memory_space=pl.ANY)],
            out_specs=out_specs,
            scratch_shapes=[pltpu.VMEM((2, 2 * tm * PACKED_SLABS, LANES), jnp.uint32),
                            pltpu.SemaphoreType.DMA((2,))],
        ),
        out_shape=out_shape,
        compiler_params=_params(("arbitrary",)),
        name="moe_combine",
    )(runs, h, gates, ys)


def _lookup(tables, idx):
    hit = idx[:, None] == jnp.arange(tables.shape[1], dtype=idx.dtype)[None, :]
    return jnp.sum(jnp.where(hit[None], tables[:, None, :], 0), axis=2)


def _work_items(tiles_e, max_items):
    item_end = jnp.cumsum(tiles_e)
    n_items = item_end[-1]
    w = jnp.minimum(jnp.arange(max_items, dtype=jnp.int32), n_items - 1)
    expert = jnp.sum(w[:, None] >= item_end[None, :], axis=1).astype(jnp.int32)
    prev_expert = jnp.concatenate([jnp.full((1,), -1, jnp.int32), expert[:-1]])
    new_expert = expert != prev_expert
    weight_slot = (jnp.cumsum(new_expert.astype(jnp.int32)) - 1) % 2
    ids = jnp.arange(N_EXPERTS, dtype=jnp.int32)
    later = (ids[None, :] > ids[:, None]) & (tiles_e[None, :] > 0)
    following = jnp.min(jnp.where(later, ids[None, :], N_EXPERTS), axis=1)
    following = jnp.where(following == N_EXPERTS, -1, following)
    next_expert, = _lookup(following[None, :], expert)
    as_i32 = lambda a: a.astype(jnp.int32)
    return (expert, as_i32(new_expert), as_i32(weight_slot), as_i32(next_expert),
            as_i32(n_items).reshape(1))


def _moe(h, routed, g, w_g, w_u, w_d, layer, tm, split_rows=None):
    n, _ = h.shape
    meta, gates, cnt, tables = routed
    experts = slice(ROUTER_LANE0, ROUTER_LANE0 + N_EXPERTS)
    counts = cnt[0, experts].astype(jnp.int32)
    tiles_e = (counts + MOE_TILE - 1) // MOE_TILE
    starts = (jnp.cumsum(tiles_e) - tiles_e) * MOE_TILE
    max_tiles = -(-2 * n // MOE_TILE) + N_EXPERTS
    tables = tables.reshape(n // tm, 8, LANES)[:, :RUN_FIELDS, experts]
    runs = tables.at[:, 0, :].add(starts[None, :]).reshape(-1)
    used = jnp.sum(tiles_e)
    pads = jnp.concatenate([starts + counts, tiles_e * MOE_TILE - counts, jnp.stack([used, max_tiles - used])])
    xs = _dispatch(h, g, meta, runs, pads, tm, max_tiles * MOE_TILE)
    ys = _experts(xs, _work_items(tiles_e, max_tiles), w_g, w_u, w_d, layer)
    return _combine(h, gates, ys, runs, tm, split_rows)


def _inproj_b_kernel(x_ref, gkv_ref, gmix_ref, wkv_ref, win_ref, kng_ref, qng_ref, bdk_ref, hsum_ref, hexp_ref,
                     q_ref, mq_ref, k_ref, v_ref, kt_ref):
    tm = x_ref.shape[0]
    sub = min(tm, 256)
    blocks = [slice(r, r + sub) for r in range(0, tm, sub)]

    def two_piece_dot(x, m_ref):
        out = None
        for p in _split(x, 2):
            t = jnp.dot(p, m_ref[...], preferred_element_type=F32)
            out = t if out is None else out + t
        return out

    kvs, projs = [], []
    for rows in blocks:
        x = x_ref[rows, :]
        xr = x * _rms_scale(x)
        kvs.append(_bdot(xr * gkv_ref[...], wkv_ref[...]))
        projs.append(_bdot(xr * gmix_ref[...], win_ref[...]))
    k_ms = [two_piece_dot(kv[:, :KV_WIDTH] * kv[:, :KV_WIDTH], bdk_ref) for kv in kvs]
    q_ms = [two_piece_dot(proj[:, :MAIN_WIDTH] * proj[:, :MAIN_WIDTH], hsum_ref) for proj in projs]
    q_scale = [two_piece_dot(lax.rsqrt(ms + EPS), hexp_ref) for ms in q_ms]
    for rows, kv, proj, kms, qs in zip(blocks, kvs, projs, k_ms, q_scale):
        k = kv[:, :KV_WIDTH] * lax.rsqrt(kms + EPS) * kng_ref[...]
        k_ref[rows, :] = k
        kt_ref[:, rows] = k.T.astype(BF16)
        v_ref[rows, :] = kv[:, KV_WIDTH:]
        q_ref[rows, :] = (proj[:, :MAIN_WIDTH] * qs * qng_ref[...]).astype(BF16)
        mq_ref[rows, :] = proj[:, MAIN_WIDTH:]


def _swa_perm():
    g, kh, dd = np.meshgrid(np.arange(SWA_GROUP), np.arange(SWA_KV_HEADS), np.arange(HEAD_DIM), indexing="ij")
    return ((kh * SWA_GROUP + g) * HEAD_DIM + dd).reshape(-1)


def _inproj_b(x, g_kv, g_mix, w_kv, w_in, kng, qng):
    n, d = x.shape
    tm = _row_tile(n, 1024)
    perm = _swa_perm()
    w_in_p = jnp.concatenate([w_in[:, :MAIN_WIDTH][:, perm], w_in[:, MAIN_WIDTH:]], axis=1).astype(BF16)
    qng_t = (jnp.tile(qng, SWA_HEADS) * HEAD_DIM ** -0.5).reshape(1, MAIN_WIDTH)
    member = (np.arange(MAIN_WIDTH)[:, None] // HEAD_DIM == np.arange(LANES)[None, :]).astype(np.float32)
    row = lambda w: pl.BlockSpec((tm, w), lambda i: (i, 0))
    return pl.pallas_call(
        _inproj_b_kernel,
        grid=(n // tm,),
        in_specs=[row(d), _const((1, d)), _const((1, d)), _const((d, 2 * KV_WIDTH)), _const((d, d)),
                  _const((1, KV_WIDTH)), _const((1, MAIN_WIDTH)), _const((KV_WIDTH, KV_WIDTH)),
                  _const((MAIN_WIDTH, LANES)), _const((LANES, MAIN_WIDTH))],
        out_specs=[row(MAIN_WIDTH), row(MEM_WIDTH), row(KV_WIDTH), row(KV_WIDTH),
                   pl.BlockSpec((KV_WIDTH, tm), lambda i: (0, i))],
        out_shape=[jax.ShapeDtypeStruct((n, MAIN_WIDTH), BF16), jax.ShapeDtypeStruct((n, MEM_WIDTH), F32),
                   jax.ShapeDtypeStruct((n, KV_WIDTH), F32), jax.ShapeDtypeStruct((n, KV_WIDTH), F32),
                   jax.ShapeDtypeStruct((KV_WIDTH, n), BF16)],
        compiler_params=_params(("parallel",)),
        name="inproj_b",
    )(x, g_kv.reshape(1, d), g_mix.reshape(1, d), w_kv.astype(BF16), w_in_p,
      jnp.tile(kng, SWA_KV_HEADS).reshape(1, KV_WIDTH), qng_t,
      _block_diag_mean(KV_WIDTH), jnp.asarray(member / HEAD_DIM, BF16), jnp.asarray(member.T, BF16))


def _softmax_with_sink(s, sink):
    m = jnp.maximum(jnp.max(s, axis=-1, keepdims=True), sink)
    e = jnp.exp(s - m)
    r = 1.0 / (jnp.sum(e, axis=-1, keepdims=True) + jnp.exp(sink - m))
    return (e * r).astype(BF16)


def _swa_bias(tq):
    slopes = 2.0 ** (-8.0 * np.arange(1, SWA_HEADS + 1, dtype=np.float64) / SWA_HEADS)
    dist = np.arange(tq)[:, None] + WINDOW - np.arange(WINDOW + tq)[None, :]
    valid = (dist >= 0) & (dist <= WINDOW)
    return np.stack([np.where(valid, -s * dist, NEG_BIG) for s in slopes]).astype(np.float32)


def _swa_prompt_kernel(sink_ref, q_ref, ktp_ref, kto_ref, vp_ref, vo_ref, bias_ref, hm_ref, o_ref, *, nb):
    w = WINDOW
    key = lax.broadcasted_iota(jnp.int32, (w, 2 * w), 1)
    has_prev = (pl.program_id(0) > 0) | (key >= w)
    heads = [(g, kh) for g in range(SWA_GROUP) for kh in range(SWA_KV_HEADS)]
    kts, vvs = [], []
    for b in range(nb):
        kt_prev = ktp_ref[...] if b == 0 else kto_ref[:, (b - 1) * w:b * w]
        v_prev = vp_ref[...] if b == 0 else vo_ref[(b - 1) * w:b * w, :]
        kts.append(jnp.concatenate([kt_prev, kto_ref[:, b * w:(b + 1) * w]], axis=1).astype(BF16))
        vvs.append(jnp.concatenate([v_prev, vo_ref[b * w:(b + 1) * w, :]], axis=0).astype(BF16))
    scores = [[jnp.dot(q_ref[b * w:(b + 1) * w, g * KV_WIDTH:(g + 1) * KV_WIDTH] * hm_ref[kh].astype(BF16),
                       kts[b], preferred_element_type=F32) for g, kh in heads] for b in range(nb)]
    for b in range(nb):
        probs = []
        for (g, kh), s in zip(heads, scores[b]):
            h = kh * SWA_GROUP + g
            s = s + bias_ref[h]
            if b == 0:
                s = jnp.where(has_prev, s, NEG_BIG)
            probs.append(_softmax_with_sink(s, sink_ref[h]))
        outs = [jnp.dot(p, vvs[b], preferred_element_type=F32) for p in probs]
        lane_head = lax.broadcasted_iota(jnp.int32, (w, KV_WIDTH), 1) // HEAD_DIM
        for g in range(SWA_GROUP):
            acc = None
            for (cg, kh), o in zip(heads, outs):
                if cg == g:
                    acc = o if acc is None else jnp.where(lane_head == kh, o, acc)
            o_ref[b * w:(b + 1) * w, g * KV_WIDTH:(g + 1) * KV_WIDTH] = acc.astype(BF16)


def _swa_prompt(q, kt, v, sinks, *, n_rows, nb):
    w = WINDOW
    step = nb * w
    assert n_rows % step == 0
    prev = lambda j, sink: jnp.maximum(j * nb - 1, 0)
    return pl.pallas_call(
        functools.partial(_swa_prompt_kernel, nb=nb),
        grid_spec=pltpu.PrefetchScalarGridSpec(
            num_scalar_prefetch=1,
            grid=(n_rows // step,),
            in_specs=[pl.BlockSpec((step, MAIN_WIDTH), lambda j, sink: (j, 0)),
                      pl.BlockSpec((KV_WIDTH, w), lambda j, sink: (0, prev(j, sink))),
                      pl.BlockSpec((KV_WIDTH, step), lambda j, sink: (0, j)),
                      pl.BlockSpec((w, KV_WIDTH), lambda j, sink: (prev(j, sink), 0)),
                      pl.BlockSpec((step, KV_WIDTH), lambda j, sink: (j, 0)),
                      pl.BlockSpec((SWA_HEADS, w, 2 * w), lambda j, sink: (0, 0, 0)),
                      pl.BlockSpec((SWA_KV_HEADS, 1, KV_WIDTH), lambda j, sink: (0, 0, 0))],
            out_specs=pl.BlockSpec((step, MAIN_WIDTH), lambda j, sink: (j, 0)),
        ),
        out_shape=jax.ShapeDtypeStruct((n_rows, MAIN_WIDTH), BF16),
        compiler_params=_params(("arbitrary",)),
        name="swa_prompt",
    )(sinks.astype(F32), q, kt, kt, v, v, jnp.asarray(_swa_bias(w)), _head_masks(SWA_KV_HEADS))


def _swa_sample_kernel(q_ref, kp_ref, ko_ref, vp_ref, vo_ref, bias_ref, sink_ref, hm_ref, o_ref, *, nb, tq):
    w = WINDOW
    heads = [(kh, g) for kh in range(SWA_KV_HEADS) for g in range(SWA_GROUP)]
    kks, vvs, scores = [], [], []
    q = q_ref[...].astype(F32)
    for i in range(nb):
        win = slice(i * KV_WIDTH, (i + 1) * KV_WIDTH)
        kks.append(jnp.concatenate([kp_ref[win, :].T, ko_ref[i * tq:(i + 1) * tq, :]], axis=0))
        vvs.append(jnp.concatenate([vp_ref[win, :].T, vo_ref[i * tq:(i + 1) * tq, :]], axis=0))
        qs = jnp.concatenate([q[i * tq:(i + 1) * tq, g * KV_WIDTH:(g + 1) * KV_WIDTH] * hm_ref[kh]
                              for kh, g in heads], axis=0)
        scores.append(_bdot_nt(qs, kks[i]))
    probs = [_softmax_with_sink(s + bias_ref[...], sink_ref[...]) for s in scores]
    outs = [_bdot(p, vv) for p, vv in zip(probs, vvs)]
    for g in range(SWA_GROUP):
        rows = []
        for i in range(nb):
            acc = None
            for r, (kh, hg) in enumerate(heads):
                if hg == g:
                    t = outs[i][r * tq:(r + 1) * tq] * hm_ref[kh]
                    acc = t if acc is None else acc + t
            rows.append(acc)
        o_ref[:, g * KV_WIDTH:(g + 1) * KV_WIDTH] = jnp.concatenate(rows, axis=0).astype(BF16)


def _swa_sample(q, k_win, v_win, k, v, sinks, *, row_off, batch, tq, nb):
    w = WINDOW
    assert batch % nb == 0 and row_off % (nb * tq) == 0
    off = row_off // (nb * tq)
    bias = jnp.asarray(_swa_bias(tq).reshape(SWA_HEADS * tq, w + tq))
    sink_col = jnp.repeat(sinks.astype(F32), tq).reshape(SWA_HEADS * tq, 1)
    own = lambda width: pl.BlockSpec((nb * tq, width), lambda b: (off + b, 0))
    win = pl.BlockSpec((nb * KV_WIDTH, w), lambda b: (b, 0))
    return pl.pallas_call(
        functools.partial(_swa_sample_kernel, nb=nb, tq=tq),
        grid=(batch // nb,),
        in_specs=[own(MAIN_WIDTH), win, own(KV_WIDTH), win, own(KV_WIDTH), _const(bias.shape),
                  _const(sink_col.shape), _const((SWA_KV_HEADS, 1, KV_WIDTH))],
        out_specs=pl.BlockSpec((nb * tq, MAIN_WIDTH), lambda b: (b, 0)),
        out_shape=jax.ShapeDtypeStruct((batch * tq, MAIN_WIDTH), BF16),
        compiler_params=_params(("arbitrary",)),
        name="swa_sample",
    )(q, k_win, k, v_win, v, bias, sink_col, _head_masks(SWA_KV_HEADS))


def kernel(x_prompt, x_sample, state_gla, cache_win_k, cache_win_v, cache_mem_k, cache_mem_v, mem_prompt, norm_mix_g, norm_ffn_g, norm_mem_g, w_mem_kv, mem_qn_g, mem_kn_g, w_out, w_in_a, w_gate_lr, b_gate_lr, gla_norm_g, w_in_b, swa_qn_g, swa_sinks, norm_kv_g, w_kv, swa_kn_g, w_router_group, b_router_group, w_router_expert, b_router_expert, w_exp_gate, w_exp_up, w_exp_down):
    bp, tp, d = x_prompt.shape
    bs, ts, _ = x_sample.shape
    assert bp == 1 and tp % WINDOW == 0 and ts * (GLA_CHUNK // ts) == GLA_CHUNK
    n_p, n_s = bp * tp, bs * ts
    w_buf = cache_win_k.shape[1]
    assert w_buf == WINDOW
    x_p, x_s = x_prompt.reshape(n_p, d), x_sample.reshape(n_s, d)

    mem_k_p, mem_v_p = _mem_kv(mem_prompt, norm_mem_g, w_mem_kv, mem_kn_g)
    feature_major = lambda c: jnp.moveaxis(c, -3, -1).reshape(*c.shape[:-3], c.shape[-2] * c.shape[-1], c.shape[-3])
    cmk, cmv = feature_major(cache_mem_k), feature_major(cache_mem_v)

    def mem_attend(mq, l):
        tm_p = _row_tile(tp, 1024)
        mo_p = _mem_attn(mq, mem_k_p, mem_v_p, mem_qn_g[l], row_off=0, seq=tp, tm=tm_p, bb=1, layer=l)
        mo_s = _mem_attn(mq, cmk, cmv, mem_qn_g[l], row_off=n_p, seq=ts, tm=ts, bb=16, layer=l)
        return mo_p, mo_s

    tm = _row_tile(math.gcd(n_p, n_s), MOE_TILE)
    router = lambda l: (norm_ffn_g[l], w_router_group[l], b_router_group[l], w_router_expert[l],
                        b_router_expert[l])

    def moe(h, routed, l, split_rows=None):
        return _moe(h, routed, norm_ffn_g[l], w_exp_gate, w_exp_up, w_exp_down, l, tm, split_rows)

    q, k, la, v, og, mq = _inproj_a(x_p, x_s, norm_mix_g[0], w_in_a[0], w_gate_lr[0], b_gate_lr[0])
    zero_state = jnp.zeros((bp, GLA_HEADS, GLA_DK, GLA_DV), F32)
    n_sub = max(1, min(16, tp // GLA_CHUNK))
    main_p, gla_p = _gla(q, k, la, v, og, zero_state, gla_norm_g[0], row_off=0, seq=tp, n_seg=1, n_sub=n_sub)
    main_s, gla_s = _gla(q, k, la, v, og, state_gla[0], gla_norm_g[0], row_off=n_p, seq=ts,
                         n_seg=GLA_CHUNK // ts, n_sub=1)
    mo_p, mo_s = mem_attend(mq, 0)
    w_o = w_out[0]
    h, routed = _outproj((x_p, x_s), main_p, mo_p, main_s, mo_s, w_o[:MAIN_WIDTH].reshape(GLA_HEADS, GLA_DV, d),
                         w_o[MAIN_WIDTH:], router(0), tm)
    h = moe(h, routed, 0)

    q, mq, k_sh, v_sh, kt_sh = _inproj_b(h, norm_kv_g, norm_mix_g[1], w_kv, w_in_b[0], swa_kn_g, swa_qn_g[0])
    ck = feature_major(cache_win_k).reshape(bs * KV_WIDTH, w_buf)
    cv = feature_major(cache_win_v).reshape(bs * KV_WIDTH, w_buf)
    main_p = _swa_prompt(q, kt_sh, v_sh, swa_sinks[0], n_rows=n_p, nb=8)
    main_s = _swa_sample(q, ck, cv, k_sh, v_sh, swa_sinks[0], row_off=n_p, batch=bs, tq=ts, nb=16)
    mo_p, mo_s = mem_attend(mq, 1)
    w_o = w_out[1]
    h, routed = _outproj(h, main_p, mo_p, main_s, mo_s, w_o[:MAIN_WIDTH][_swa_perm()], w_o[MAIN_WIDTH:],
                         router(1), tm)
    y_p, y_s = moe(h, routed, 1, split_rows=n_p)

    y_prompt = y_p.reshape(bp, tp, d)
    y_sample = y_s.reshape(bs, ts, d)
    k_new = k_sh[n_p:].reshape(bs, ts, SWA_KV_HEADS, HEAD_DIM)
    v_new = v_sh[n_p:].reshape(bs, ts, SWA_KV_HEADS, HEAD_DIM)
    win_k_s = jnp.concatenate([cache_win_k, k_new], axis=1)[:, -w_buf:]
    win_v_s = jnp.concatenate([cache_win_v, v_new], axis=1)[:, -w_buf:]
    win_k_p = k_sh[n_p - WINDOW:n_p].reshape(bp, WINDOW, SWA_KV_HEADS, HEAD_DIM)
    win_v_p = v_sh[n_p - WINDOW:n_p].reshape(bp, WINDOW, SWA_KV_HEADS, HEAD_DIM)
    token_major = lambda c: jnp.moveaxis(c.reshape(*c.shape[:-2], MEM_HEADS, HEAD_DIM, c.shape[-1]), -1, -3)
    return (y_prompt, y_sample, gla_p[None], gla_s[None], win_k_p, win_v_p, win_k_s, win_v_s,
            token_major(mem_k_p), token_major(mem_v_p))
```

```python
import functools
import math

import numpy as np
import jax
import jax.numpy as jnp
from jax import lax
from jax.experimental import pallas as pl
from jax.experimental.pallas import tpu as pltpu

F32 = jnp.float32
BF16 = jnp.bfloat16

D_MODEL = 1024
MEM_HEADS = 4
HEAD_DIM = 64
MEM_WIDTH = MEM_HEADS * HEAD_DIM
MAIN_WIDTH = D_MODEL - MEM_WIDTH
GLA_HEADS = 4
GLA_DV = MAIN_WIDTH // GLA_HEADS
GLA_DK = GLA_DV // 2
GLA_DK_PAD = 128
GLA_KEY_WIDTH = GLA_HEADS * GLA_DK
GLA_KEY_PAD = GLA_HEADS * GLA_DK_PAD
GLA_GATE_RANK = 16
GLA_TAU = 16.0
GLA_CHUNK = 64
SWA_HEADS = MAIN_WIDTH // HEAD_DIM
SWA_KV_HEADS = 4
SWA_GROUP = SWA_HEADS // SWA_KV_HEADS
KV_WIDTH = SWA_KV_HEADS * HEAD_DIM
WINDOW = 128
N_GROUPS = 4
EXPERTS_PER_GROUP = 8
N_EXPERTS = N_GROUPS * EXPERTS_PER_GROUP
D_EXPERT = 512
EPS = 1e-6
LANES = 128
NEG_BIG = -1e30
VMEM_LIMIT = 56 * 1024 * 1024
MOE_TILE = 512
MOE_SUB = 128
ROUTER_LANE0 = N_GROUPS
ROUTER_META_ROWS = 8
ROUTER_ROWS = 40
SLABS = D_MODEL // LANES
PACKED_SLABS = SLABS // 2


def _bdot(a, b):
    return jnp.dot(a.astype(BF16), b.astype(BF16), preferred_element_type=F32)


def _bdot_nt(a, b):
    return lax.dot_general(a.astype(BF16), b.astype(BF16), (((1,), (1,)), ((), ())),
                           preferred_element_type=F32)


def _bdot_tn(a, b):
    return lax.dot_general(a.astype(BF16), b.astype(BF16), (((0,), (0,)), ((), ())),
                           preferred_element_type=F32)


def _split(x, n):
    parts = []
    for _ in range(n - 1):
        p = x.astype(BF16)
        parts.append(p)
        x = x - p.astype(F32)
    parts.append(x.astype(BF16))
    return parts


def _exact_left_dot(m, x, n=2):
    out = None
    for p in _split(x, n):
        t = jnp.dot(m, p, preferred_element_type=F32)
        out = t if out is None else out + t
    return out


def _seg_mean(x2, bd):
    out = None
    for p in _split(x2, 2):
        t = jnp.dot(p, bd, preferred_element_type=F32)
        out = t if out is None else out + t
    return out


def _rms_scale(x):
    return lax.rsqrt(jnp.mean(x * x, axis=-1, keepdims=True) + EPS)


def _row_tile(n, cap=512):
    t = cap
    while t > 8 and n % t:
        t //= 2
    assert n % t == 0, n
    return t


def _params(sem):
    return pltpu.CompilerParams(dimension_semantics=sem, vmem_limit_bytes=VMEM_LIMIT)


def _const(shape):
    nd = len(shape)
    return pl.BlockSpec(shape, lambda *_: (0,) * nd)


def _group_specs(tm, width, prompt_tiles, lead=None):
    p_idx = lambda i, *_: jnp.minimum(i, prompt_tiles - 1)
    s_idx = lambda i, *_: jnp.maximum(i - prompt_tiles, 0)
    if lead is None:
        return [pl.BlockSpec((tm, width), lambda i, *_, f=f: (f(i), 0)) for f in (p_idx, s_idx)]
    return [pl.BlockSpec((lead, tm, width), lambda i, *_, f=f: (0, f(i), 0)) for f in (p_idx, s_idx)]


def _block_diag_mean(width):
    i = np.arange(width)
    return jnp.asarray((i[:, None] // HEAD_DIM == i[None, :] // HEAD_DIM) / HEAD_DIM, BF16)


def _head_masks(n_heads):
    i = np.arange(n_heads * HEAD_DIM)
    return jnp.asarray((i[None, :] // HEAD_DIM == np.arange(n_heads)[:, None]), F32)[:, None, :]


def _mem_kv_kernel(mem_ref, g_ref, w_ref, kng_ref, bd_ref, k_ref, v_ref):
    x = mem_ref[0]
    hn = x * _rms_scale(x) * g_ref[0]
    kv = _bdot(hn, w_ref[0])
    k = kv[:, :MEM_WIDTH]
    k = k * lax.rsqrt(_seg_mean(k * k, bd_ref[...]) + EPS) * kng_ref[0]
    k_ref[0, 0] = k.T
    v_ref[0, 0] = kv[:, MEM_WIDTH:].T


def _mem_kv(mem, g, w, kng):
    depth, (b, m, d) = w.shape[0], mem.shape
    out = jax.ShapeDtypeStruct((depth, b, m, MEM_WIDTH), F32)
    blk = pl.BlockSpec((1, 1, m, MEM_WIDTH), lambda l, i: (l, i, 0, 0))
    return pl.pallas_call(
        _mem_kv_kernel,
        grid=(depth, b),
        in_specs=[pl.BlockSpec((1, m, d), lambda l, i: (i, 0, 0)),
                  pl.BlockSpec((1, 1, d), lambda l, i: (l, 0, 0)),
                  pl.BlockSpec((1, d, 2 * MEM_WIDTH), lambda l, i: (l, 0, 0)),
                  pl.BlockSpec((1, 1, MEM_WIDTH), lambda l, i: (l, 0, 0)),
                  _const((MEM_WIDTH, MEM_WIDTH))],
        out_specs=[blk, blk],
        out_shape=[out, out],
        compiler_params=_params(("arbitrary", "arbitrary")),
        name="mem_kv",
    )(mem, g.reshape(depth, 1, d), w.astype(BF16),
      jnp.tile(kng, (1, MEM_HEADS)).reshape(depth, 1, MEM_WIDTH), _block_diag_mean(MEM_WIDTH))


def _inproj_a_kernel(xp_ref, xs_ref, g_ref, wq_ref, wk_ref, wv_ref, wog_ref, wlr_ref, wmq_ref, wgl_ref, bgl_ref,
                     q_ref, k_ref, la_ref, v_ref, og_ref, mq_ref, *, prompt_tiles):
    is_prompt = pl.program_id(0) < prompt_tiles
    tm = q_ref.shape[0]
    sub = min(tm, 128)
    for r in range(0, tm, sub):
        rows = slice(r, r + sub)
        x = jnp.where(is_prompt, xp_ref[rows, :], xs_ref[rows, :])
        hn = (x * _rms_scale(x) * g_ref[...]).astype(BF16)
        q_ref[rows, :] = jnp.dot(hn, wq_ref[...], preferred_element_type=F32) * (GLA_DK ** -0.5)
        k_ref[rows, :] = jnp.dot(hn, wk_ref[...], preferred_element_type=F32)
        for h in range(GLA_HEADS):
            v_ref[h, rows, :] = jnp.dot(hn, wv_ref[h], preferred_element_type=F32).astype(BF16)
            og_ref[h, rows, :] = jnp.dot(hn, wog_ref[h], preferred_element_type=F32).astype(BF16)
        lr = jnp.dot(hn, wlr_ref[...], preferred_element_type=F32)
        z = _bdot(lr, wgl_ref[...]) + bgl_ref[...]
        la_ref[rows, :] = (jnp.minimum(z, 0.0) - jnp.log(1.0 + jnp.exp(-jnp.abs(z)))) * (1.0 / GLA_TAU)
        mq_ref[rows, :] = jnp.dot(hn, wmq_ref[...], preferred_element_type=F32)


def _pad_heads(w, width, pad):
    lead = w.shape[:-1]
    w = w.reshape(*lead, GLA_HEADS, width)
    w = jnp.pad(w, [(0, 0)] * len(lead) + [(0, 0), (0, pad - width)])
    return w.reshape(*lead, GLA_HEADS * pad)


def _inproj_a(x_p, x_s, g, w_in, w_lr, b_lr):
    (n_p, d), n_s = x_p.shape, x_s.shape[0]
    n = n_p + n_s
    tm = _row_tile(n_s)
    assert n_p % tm == 0
    pt = n_p // tm
    c0, c1, c2, c3, c4 = (GLA_KEY_WIDTH, 2 * GLA_KEY_WIDTH, 2 * GLA_KEY_WIDTH + MAIN_WIDTH,
                          2 * GLA_KEY_WIDTH + 2 * MAIN_WIDTH,
                          2 * GLA_KEY_WIDTH + 2 * MAIN_WIDTH + GLA_GATE_RANK)
    wb = w_in.astype(BF16)
    wq = _pad_heads(wb[:, :c0], GLA_DK, GLA_DK_PAD)
    wk = _pad_heads(wb[:, c0:c1], GLA_DK, GLA_DK_PAD)
    wv = wb[:, c1:c2].reshape(d, GLA_HEADS, GLA_DV).transpose(1, 0, 2)
    wog = wb[:, c2:c3].reshape(d, GLA_HEADS, GLA_DV).transpose(1, 0, 2)
    wlr = jnp.pad(wb[:, c3:c4], ((0, 0), (0, LANES - GLA_GATE_RANK)))
    wmq = wb[:, c4:]
    wgl = jnp.pad(_pad_heads(w_lr.astype(BF16), GLA_DK, GLA_DK_PAD), ((0, LANES - GLA_GATE_RANK), (0, 0)))
    bgl = _pad_heads(b_lr.reshape(1, -1), GLA_DK, GLA_DK_PAD)
    row = lambda w: pl.BlockSpec((tm, w), lambda i: (i, 0))
    hrow = pl.BlockSpec((GLA_HEADS, tm, GLA_DV), lambda i: (0, i, 0))
    key = jax.ShapeDtypeStruct((n, GLA_KEY_PAD), F32)
    val = jax.ShapeDtypeStruct((GLA_HEADS, n, GLA_DV), F32)
    return pl.pallas_call(
        functools.partial(_inproj_a_kernel, prompt_tiles=pt),
        grid=(n // tm,),
        in_specs=_group_specs(tm, d, pt) + [
            _const((1, d)), _const(wq.shape), _const(wk.shape), _const(wv.shape),
            _const(wog.shape), _const(wlr.shape), _const(wmq.shape), _const(wgl.shape),
            _const(bgl.shape)],
        out_specs=[row(GLA_KEY_PAD), row(GLA_KEY_PAD), row(GLA_KEY_PAD), hrow, hrow, row(MEM_WIDTH)],
        out_shape=[key, key, key, jax.ShapeDtypeStruct(val.shape, BF16), jax.ShapeDtypeStruct(val.shape, BF16),
                   jax.ShapeDtypeStruct((n, MEM_WIDTH), F32)],
        compiler_params=_params(("parallel",)),
        name="inproj_a",
    )(x_p, x_s, g.reshape(1, d), wq, wk, wv, wog, wlr, wmq, wgl, bgl)


def _gla_kernel(q_ref, k_ref, la_ref, v_ref, og_ref, s0_ref, gn_ref, mcum_ref, mall_ref, sel_ref,
                o_ref, sout_ref, s_ref, *, chunk, n_sub, n_seg):
    j = pl.program_id(1)
    seg = chunk // n_seg

    @pl.when(j == 0)
    def _():
        s_ref[...] = jnp.zeros_like(s_ref)
        s_ref[:, :, :GLA_DK, :] = s0_ref[...]

    mcum = mcum_ref[...]
    causal = mcum.astype(F32) > 0.0
    row = lax.broadcasted_iota(jnp.int32, (chunk, GLA_DK_PAD), 0)
    gn = gn_ref[...]
    hcols = [slice(h * GLA_DK_PAD, (h + 1) * GLA_DK_PAD) for h in range(GLA_HEADS)]
    crows = [slice(c * chunk, (c + 1) * chunk) for c in range(n_sub)]
    qts, kts, kds, e_ends = [], [], [], []
    for rows in crows:
        la = la_ref[rows, :]
        b = _exact_left_dot(mcum, la)
        if n_seg == 1:
            b_end = b[chunk - 1:chunk, :]
            e_ends.append(jnp.broadcast_to(jnp.exp(b_end), (LANES, b.shape[1])).T)
        else:
            b_end = _exact_left_dot(mall_ref[...], la)
            e_ends.append(jnp.exp(_exact_left_dot(sel_ref[...], la)).T)
        k = k_ref[rows, :]
        qts.append(q_ref[rows, :] * jnp.exp(b))
        kts.append((k * jnp.exp(-b)).astype(BF16))
        kds.append(k * jnp.exp(b_end - b))
    vbs = [[v_ref[h, rows, :].astype(BF16) for h in range(GLA_HEADS)] for rows in crows]
    scores = [[_bdot_nt(qts[c][:, cols], kts[c][:, cols]) for cols in hcols] for c in range(n_sub)]
    kvs = []
    for c in range(n_sub):
        per_head = []
        for h, cols in enumerate(hcols):
            per_seg = []
            for s in range(n_seg):
                kd = kds[c][:, cols]
                if n_seg > 1:
                    kd = jnp.where((row >= s * seg) & (row < (s + 1) * seg), kd, 0.0)
                per_seg.append(_bdot_tn(kd, vbs[c][h]))
            per_head.append(per_seg)
        kvs.append(per_head)
    state = [[s_ref[s, h] for s in range(n_seg)] for h in range(GLA_HEADS)]
    inters = []
    for c in range(n_sub):
        per_head = []
        for h, cols in enumerate(hcols):
            parts = []
            for s in range(n_seg):
                parts.append(_bdot(qts[c][s * seg:(s + 1) * seg, cols], state[h][s]))
                state[h][s] = e_ends[c][cols, s:s + 1] * state[h][s] + kvs[c][h][s]
            per_head.append(parts[0] if n_seg == 1 else jnp.concatenate(parts, axis=0))
        inters.append(per_head)
    for h in range(GLA_HEADS):
        for s in range(n_seg):
            s_ref[s, h] = state[h][s]
    for c, rows in enumerate(crows):
        for h in range(GLA_HEADS):
            a = jnp.where(causal, scores[c][h], 0.0)
            o = _bdot(a, vbs[c][h]) + inters[c][h]
            on = o * lax.rsqrt(jnp.mean(o * o, axis=-1, keepdims=True) + EPS) * gn
            og = og_ref[h, rows, :].astype(F32)
            o_ref[h, rows, :] = (on * (og * jax.nn.sigmoid(og))).astype(BF16)

    @pl.when(j == pl.num_programs(1) - 1)
    def _():
        sout_ref[...] = s_ref[:, :, :GLA_DK, :]


def _gla(q, k, la, v, og, s0, gnorm, *, row_off, seq, n_seg, n_sub):
    batch = s0.shape[0]
    chunk = GLA_CHUNK
    assert chunk % n_seg == 0 and batch % n_seg == 0
    seg = chunk // n_seg
    step_rows = n_sub * chunk
    if n_seg > 1:
        assert seq == seg and n_sub == 1
        t_steps = 1
    else:
        assert seq % step_rows == 0
        t_steps = seq // step_rows
    assert row_off % step_rows == 0
    off = row_off // step_rows
    i = np.arange(chunk)
    same = (i[:, None] // seg) == (i[None, :] // seg)
    mcum = jnp.asarray(same & (i[None, :] <= i[:, None]), BF16)
    mall = jnp.asarray(same, BF16)
    sel = jnp.asarray((i[None, :] // seg) == np.arange(LANES)[:, None], BF16)
    ridx = lambda g, j: (off + g * t_steps + j, 0)
    hidx = lambda g, j: (0, off + g * t_steps + j, 0)
    key_spec = pl.BlockSpec((step_rows, GLA_KEY_PAD), ridx)
    val_spec = pl.BlockSpec((GLA_HEADS, step_rows, GLA_DV), hidx)
    st_spec = pl.BlockSpec((n_seg, GLA_HEADS, GLA_DK, GLA_DV), lambda g, j: (g, 0, 0, 0))
    in_specs = [key_spec, key_spec, key_spec, val_spec, val_spec, st_spec, _const((1, GLA_DV)),
                _const((chunk, chunk)), _const((chunk, chunk)), _const((LANES, chunk))]
    args = [q, k, la, v, og, s0, gnorm.reshape(1, GLA_DV), mcum, mall, sel]
    out_spec = pl.BlockSpec((GLA_HEADS, step_rows, GLA_DV), lambda g, j: (0, g * t_steps + j, 0))
    return pl.pallas_call(
        functools.partial(_gla_kernel, chunk=chunk, n_sub=n_sub, n_seg=n_seg),
        grid=(batch // n_seg, t_steps),
        in_specs=in_specs,
        out_specs=[out_spec, st_spec],
        out_shape=[jax.ShapeDtypeStruct((GLA_HEADS, batch * seq, GLA_DV), BF16),
                   jax.ShapeDtypeStruct(s0.shape, F32)],
        scratch_shapes=[pltpu.VMEM((n_seg, GLA_HEADS, GLA_DK_PAD, GLA_DV), F32)],
        compiler_params=_params(("arbitrary", "arbitrary")),
        name="gla",
    )(*args)


def _mem_attn_kernel(q_ref, k_ref, v_ref, g_ref, bd_ref, hm_ref, o_ref, *, tm, bb):
    g = g_ref[...]
    sub = min(tm, 128)
    units = [(i, i * tm + r) for i in range(bb) for r in range(0, tm, sub)]
    scores = []
    for i, r in units:
        q = q_ref[r:r + sub, :]
        qn = q * lax.rsqrt(_seg_mean(q * q, bd_ref[...]) + EPS) * g
        qs = jnp.concatenate([(qn * hm_ref[h]).astype(BF16) for h in range(MEM_HEADS)], axis=0)
        scores.append(_bdot(qs, k_ref[i]))
    probs = []
    for s in scores:
        e = jnp.exp(s - jnp.max(s, axis=-1, keepdims=True))
        probs.append(e * (1.0 / jnp.sum(e, axis=-1, keepdims=True)))
    outs = [_bdot_nt(p, v_ref[i]) for (i, _), p in zip(units, probs)]
    rows = []
    for o in outs:
        acc = o[:sub] * hm_ref[0]
        for h in range(1, MEM_HEADS):
            acc = acc + o[h * sub:(h + 1) * sub] * hm_ref[h]
        rows.append(acc)
    o_ref[...] = jnp.concatenate(rows, axis=0).astype(BF16)


def _mem_attn(mq, mk, mv, qng, *, row_off, seq, tm, bb, layer):
    depth, batch, m, _ = mk.shape
    mk = mk.reshape(depth * batch, m, MEM_WIDTH)
    mv = mv.reshape(depth * batch, m, MEM_WIDTH)
    kv_off = layer * batch // bb
    assert seq % tm == 0 and batch % bb == 0 and (bb == 1 or seq == tm)
    t_steps = seq // tm
    step_rows = bb * tm
    assert row_off % step_rows == 0
    off = row_off // step_rows
    row_spec = pl.BlockSpec((step_rows, MEM_WIDTH), lambda g, j: (off + g * t_steps + j, 0))
    kv_spec = pl.BlockSpec((bb, m, MEM_WIDTH), lambda g, j: (kv_off + g, 0, 0))
    in_specs = [row_spec, kv_spec, kv_spec, _const((1, MEM_WIDTH)), _const((MEM_WIDTH, MEM_WIDTH)),
                _const((MEM_HEADS, 1, MEM_WIDTH))]
    args = [mq, mk, mv, (jnp.tile(qng, MEM_HEADS) * HEAD_DIM ** -0.5).reshape(1, MEM_WIDTH),
            _block_diag_mean(MEM_WIDTH), _head_masks(MEM_HEADS)]
    return pl.pallas_call(
        functools.partial(_mem_attn_kernel, tm=tm, bb=bb),
        grid=(batch // bb, t_steps),
        in_specs=in_specs,
        out_specs=pl.BlockSpec((step_rows, MEM_WIDTH), lambda g, j: (g * t_steps + j, 0)),
        out_shape=jax.ShapeDtypeStruct((batch * seq, MEM_WIDTH), BF16),
        compiler_params=_params(("parallel", "parallel")),
        name="mem_attn",
    )(*args)


def _outproj_kernel(*refs, heads, prompt_tiles, split_residual):
    n_h = 2 if split_residual else 1
    h_refs, refs = refs[:n_h], refs[n_h:]
    (main_p_ref, main_s_ref, mo_p_ref, mo_s_ref, wmain_ref, wmo_ref), refs = refs[:6], refs[6:]
    route_in, (o_ref, *route_out) = refs[:6], refs[6:]
    is_prompt = pl.program_id(0) < prompt_tiles
    pick = lambda p, s: jnp.where(is_prompt, p, s)
    tm = o_ref.shape[0]
    sub = min(tm, 128)
    blocks = []
    for r in range(0, tm, sub):
        rows = slice(r, r + sub)
        acc = pick(h_refs[0][rows, :], h_refs[1][rows, :]) if split_residual else h_refs[0][rows, :]
        acc = acc + _bdot(pick(mo_p_ref[rows, :], mo_s_ref[rows, :]), wmo_ref[...])
        if heads:
            for h in range(heads):
                acc = acc + _bdot(pick(main_p_ref[h, rows, :], main_s_ref[h, rows, :]), wmain_ref[h])
        else:
            acc = acc + _bdot(pick(main_p_ref[rows, :], main_s_ref[rows, :]), wmain_ref[...])
        o_ref[rows, :] = acc
        blocks.append(acc)
    _route_tile(blocks, *route_in, *route_out)


def _outproj(h, main_p, mo_p, main_s, mo_s, w_main, w_mo, router, tm):
    n_p, n_s = mo_p.shape[0], mo_s.shape[0]
    n, d = n_p + n_s, w_mo.shape[1]
    assert n_p % tm == 0 and n_s % tm == 0
    pt = n_p // tm
    heads = main_p.shape[0] if main_p.ndim == 3 else 0
    row = pl.BlockSpec((tm, d), lambda i: (i, 0))
    split = isinstance(h, tuple)
    h_specs, h_args = (_group_specs(tm, d, pt), list(h)) if split else ([row], [h])
    main_specs = _group_specs(tm, main_p.shape[-1], pt, lead=heads or None)
    r_args, r_in_specs, r_out_specs, r_out_shape, r_scratch = _route_operands(*router, n, tm)
    h_new, *routed = pl.pallas_call(
        functools.partial(_outproj_kernel, heads=heads, prompt_tiles=pt, split_residual=split),
        grid=(n // tm,),
        in_specs=(h_specs + main_specs + _group_specs(tm, MEM_WIDTH, pt)
                  + [_const(w_main.shape), _const(w_mo.shape)] + r_in_specs),
        out_specs=[row] + r_out_specs,
        out_shape=[jax.ShapeDtypeStruct((n, d), F32)] + r_out_shape,
        scratch_shapes=[r_scratch],
        compiler_params=_params(("arbitrary",)),
        name="outproj_route",
    )(*h_args, main_p, main_s, mo_p, mo_s, w_main.astype(BF16), w_mo.astype(BF16), *r_args)
    return h_new, routed


def _route_tile(x_blocks, g_ref, whi_ref, wlo_ref, b_ref, before_tok_ref, before_row_ref,
                mi_ref, mf_ref, cnt_ref, tt_ref, carry_ref):
    @pl.when(pl.program_id(0) == 0)
    def _():
        carry_ref[...] = jnp.zeros_like(carry_ref)

    nt = lambda a, b: lax.dot_general(a, b, (((1,), (1,)), ((), ())), preferred_element_type=F32)
    logits = []
    for x in x_blocks:
        x_hi, x_lo = _split(x * _rms_scale(x) * g_ref[...], 2)
        logits.append((nt(whi_ref[...], x_hi) + nt(wlo_ref[...], x_hi) + nt(whi_ref[...], x_lo))[:ROUTER_ROWS])
    logits = jnp.concatenate(logits, axis=1) + b_ref[...]
    tm = logits.shape[1]
    row = lax.broadcasted_iota(jnp.int32, (ROUTER_ROWS, tm), 0)
    far = jnp.int32(2 * LANES)

    def first_max(vals):
        m = jnp.max(vals, axis=0, keepdims=True)
        return m, jnp.min(jnp.where(vals == m, row, far), axis=0, keepdims=True)

    gl = jnp.where(row < N_GROUPS, logits, -jnp.inf)
    gmax, grp = first_max(gl)
    pg_sel = 1.0 / jnp.sum(jnp.exp(gl - gmax), axis=0, keepdims=True)
    lo = ROUTER_LANE0 + grp * EXPERTS_PER_GROUP
    el = jnp.where((row >= lo) & (row < lo + EXPERTS_PER_GROUP), logits, -jnp.inf)
    m1, i1 = first_max(el)
    m2, i2 = first_max(jnp.where(row == i1, -jnp.inf, el))
    e2 = jnp.exp(m2 - m1)
    g1 = pg_sel / (1.0 + e2)
    g2 = pg_sel * e2 / (1.0 + e2)

    oh1 = row == i1
    oh2 = row == i2
    picked = jnp.where(oh1 | oh2, 1.0, 0.0)
    earlier = jnp.dot(picked.astype(BF16), before_tok_ref[...], preferred_element_type=F32)
    cnt_col = jnp.sum(picked, axis=1, keepdims=True)
    cnt_tile = jnp.concatenate([jnp.broadcast_to(cnt_col, (ROUTER_ROWS, LANES)),
                                jnp.zeros((LANES - ROUTER_ROWS, LANES), F32)], axis=0)
    c_hi = jnp.floor(cnt_tile * (1.0 / 32.0))
    c_lo = cnt_tile - 32.0 * c_hi
    first = (32.0 * jnp.dot(before_row_ref[...], c_hi.astype(BF16), preferred_element_type=F32)
             + jnp.dot(before_row_ref[...], c_lo.astype(BF16), preferred_element_type=F32))
    local = first[:ROUTER_ROWS, 0:1] + earlier
    lpos1 = jnp.sum(jnp.where(oh1, local, 0.0), axis=0, keepdims=True)
    lpos2 = jnp.sum(jnp.where(oh2, local, 0.0), axis=0, keepdims=True)
    carry_before = carry_ref[...]
    carry = carry_before + cnt_tile
    carry_ref[...] = carry

    lane = lax.broadcasted_iota(jnp.int32, (LANES, LANES), 1)
    cols = jnp.where(lane == 0, carry_before, jnp.where(lane == 1, cnt_tile, jnp.where(lane == 2, first,
                     jnp.where(lane == 3, carry, 0.0))))
    tables = cols.T
    tt_ref[...] = tables[:8].astype(jnp.int32)
    cnt_ref[...] = tables[3:4]
    row8 = lax.broadcasted_iota(jnp.int32, (ROUTER_META_ROWS, tm), 0)
    zero8 = jnp.zeros((ROUTER_META_ROWS, tm), F32)
    mi_ref[...] = jnp.where(row8 == 0, lpos1, jnp.where(row8 == 1, lpos2, zero8)).astype(jnp.int32)
    stacked = jnp.where(row8 == 0, g1, jnp.where(row8 == 1, g2, jnp.where(row8 == 2, lpos1,
                        jnp.where(row8 == 3, lpos2, zero8))))
    mf_ref[...] = jnp.concatenate([stacked, jnp.zeros((LANES - ROUTER_META_ROWS, tm), F32)], axis=0).T


def _route_operands(g, w_rg, b_rg, w_re, b_re, n, tm):
    d = g.shape[0]
    assert n % tm == 0 and 2 * tm <= 32 * 32
    n_real = N_GROUPS + N_EXPERTS
    w = jnp.pad(jnp.concatenate([w_rg, w_re], axis=1), ((0, 0), (0, LANES - n_real))).T
    b = jnp.pad(jnp.concatenate([b_rg, b_re]), (0, ROUTER_ROWS - n_real)).reshape(ROUTER_ROWS, 1)
    w_hi = w.astype(BF16)
    w_lo = (w - w_hi.astype(F32)).astype(BF16)
    i = np.arange(tm)
    before_tok = jnp.asarray(i[:, None] < i[None, :], BF16)
    e = np.arange(LANES)
    before_row = jnp.asarray(e[None, :] < e[:, None], BF16)
    args = [g.reshape(1, d), w_hi, w_lo, b, before_tok, before_row]
    in_specs = [_const(a.shape) for a in args]
    out_specs = [pl.BlockSpec((ROUTER_META_ROWS, tm), lambda i: (0, i)), pl.BlockSpec((tm, LANES), lambda i: (i, 0)),
                 _const((1, LANES)), pl.BlockSpec((8, LANES), lambda i: (i, 0))]
    out_shape = [jax.ShapeDtypeStruct((ROUTER_META_ROWS, n), jnp.int32), jax.ShapeDtypeStruct((n, LANES), F32),
                 jax.ShapeDtypeStruct((1, LANES), F32), jax.ShapeDtypeStruct((n // tm * 8, LANES), jnp.int32)]
    return args, in_specs, out_specs, out_shape, pltpu.VMEM((LANES, LANES), F32)


def _pack_rows(ref, x, rows, lead=(), row0=0):
    u32 = jnp.uint32
    for w in range(PACKED_SLABS):
        lo = x[:, (2 * w) * LANES:(2 * w + 1) * LANES].astype(BF16).astype(F32)
        hi = x[:, (2 * w + 1) * LANES:(2 * w + 2) * LANES].astype(BF16).astype(F32)
        word = (lax.bitcast_convert_type(lo, u32) >> 16) | (lax.bitcast_convert_type(hi, u32) & u32(0xFFFF0000))
        ref[lead + (pl.ds(row0 * PACKED_SLABS + w, rows, stride=PACKED_SLABS), slice(None))] = word


def _unpack_rows(ref, rows, lead=(), row0=0):
    u32 = jnp.uint32
    slabs = []
    for w in range(PACKED_SLABS):
        word = ref[lead + (pl.ds(row0 * PACKED_SLABS + w, rows, stride=PACKED_SLABS), slice(None))]
        slabs.append(lax.bitcast_convert_type(word << 16, F32).astype(BF16))
        slabs.append(lax.bitcast_convert_type(word & u32(0xFFFF0000), F32).astype(BF16))
    return jnp.concatenate(slabs, axis=1)


RUN_FIELDS = 3
RUN_CHUNK_BITS = 6


def _copy_runs(runs_ref, tile, local_rows, global_rows, sem, *, to_global):
    ps = PACKED_SLABS
    base = tile * (RUN_FIELDS * N_EXPERTS)

    def piece(g0, l0, off, size):
        g = global_rows(pl.multiple_of((g0 + off) * ps, ps), size * ps)
        l = local_rows(pl.multiple_of((l0 + off) * ps, ps), size * ps)
        src, dst = (l, g) if to_global else (g, l)
        pltpu.make_async_copy(src, dst, sem).start()

    def per_expert(e, carry):
        g0 = runs_ref[base + e]
        length = runs_ref[base + N_EXPERTS + e]
        l0 = runs_ref[base + 2 * N_EXPERTS + e]
        big = 1 << RUN_CHUNK_BITS

        def big_piece(c, inner):
            piece(g0, l0, c * big, big)
            return inner

        n_big = length >> RUN_CHUNK_BITS
        lax.fori_loop(0, n_big, big_piece, 0)
        off = n_big * big
        for bit in reversed(range(RUN_CHUNK_BITS)):
            size = 1 << bit

            @pl.when((length & size) != 0)
            def _(off=off, size=size):
                piece(g0, l0, off, size)

            off = off + (length & size)
        return carry

    lax.fori_loop(0, N_EXPERTS, per_expert, 0)


def _fill_pads(pads_ref, buf0, xs_hbm, sem, *, wait):
    ps = PACKED_SLABS

    def go(copy):
        copy.wait() if wait else copy.start()

    def per_tail_tile(t, carry):
        rows = MOE_TILE * ps
        go(pltpu.make_async_copy(
            buf0.at[pl.ds(0, rows)],
            xs_hbm.at[pl.ds(pl.multiple_of((pads_ref[2 * N_EXPERTS] + t) * rows, rows), rows)], sem))
        return carry

    lax.fori_loop(0, pads_ref[2 * N_EXPERTS + 1], per_tail_tile, 0)

    def per_expert(e, carry):
        first, length = pads_ref[e], pads_ref[N_EXPERTS + e]
        off = 0
        for bit in reversed(range((MOE_TILE - 1).bit_length())):
            size = 1 << bit

            @pl.when((length & size) != 0)
            def _(off=off, size=size):
                go(pltpu.make_async_copy(
                    buf0.at[pl.ds(pl.multiple_of(off * ps, ps), size * ps)],
                    xs_hbm.at[pl.ds(pl.multiple_of((first + off) * ps, ps), size * ps)], sem))

            off = off + (length & size)
        return carry

    lax.fori_loop(0, N_EXPERTS, per_expert, 0)


def _dispatch_kernel(runs_ref, pads_ref, h_ref, g_ref, meta_ref, xs_hbm, buf, sem, pad_sem, *, tm, steps):
    i = pl.program_id(0)
    slot = lax.rem(i, 2)
    ns = 2 * tm
    assert ns >= MOE_TILE

    def wait_slot(sl):
        pltpu.make_async_copy(buf.at[sl], xs_hbm.at[pl.ds(0, ns * PACKED_SLABS)], sem.at[sl]).wait()

    @pl.when(i >= 2)
    def _():
        wait_slot(slot)

    x = h_ref[...]
    xn = (x * _rms_scale(x) * g_ref[...]).astype(BF16)
    j = lax.broadcasted_iota(jnp.int32, (ns, tm), 0)
    pick = jnp.where((j == meta_ref[0:1, :]) | (j == meta_ref[1:2, :]), 1.0, 0.0).astype(BF16)
    _pack_rows(buf, jnp.dot(pick, xn, preferred_element_type=F32), ns, (slot,))
    _copy_runs(runs_ref, i, lambda start, size: buf.at[slot, pl.ds(start, size)],
               lambda start, size: xs_hbm.at[pl.ds(start, size)], sem.at[slot], to_global=True)

    @pl.when(i == 0)
    def _():
        _fill_pads(pads_ref, buf.at[0], xs_hbm, pad_sem, wait=False)

    @pl.when(i == min(1, steps - 1))
    def _():
        _fill_pads(pads_ref, buf.at[0], xs_hbm, pad_sem, wait=True)

    @pl.when(i == steps - 1)
    def _():
        wait_slot(slot)
        if steps > 1:
            wait_slot(1 - slot)


def _dispatch(h, g, meta, runs, pads, tm, n_slots):
    n, d = h.shape
    steps = n // tm
    return pl.pallas_call(
        functools.partial(_dispatch_kernel, tm=tm, steps=steps),
        grid_spec=pltpu.PrefetchScalarGridSpec(
            num_scalar_prefetch=2,
            grid=(steps,),
            in_specs=[pl.BlockSpec((tm, d), lambda i, *_: (i, 0)),
                      pl.BlockSpec((1, d), lambda i, *_: (0, 0)),
                      pl.BlockSpec((ROUTER_META_ROWS, tm), lambda i, *_: (0, i))],
            out_specs=pl.BlockSpec(memory_space=pl.ANY),
            scratch_shapes=[pltpu.VMEM((2, 2 * tm * PACKED_SLABS, LANES), jnp.uint32),
                            pltpu.SemaphoreType.DMA((2,)), pltpu.SemaphoreType.DMA],
        ),
        out_shape=jax.ShapeDtypeStruct((n_slots * PACKED_SLABS, LANES), jnp.uint32),
        compiler_params=_params(("arbitrary",)),
        name="moe_dispatch",
    )(runs, pads, h, g.reshape(1, d), meta)


def _expert_kernel(exp_ref, new_ref, wslot_ref, next_ref, n_ref,
                   xs_hbm, wg_hbm, wu_hbm, wd_hbm, ys_hbm,
                   xbuf, ybuf, xsem, ysem, wgf, wuf, wdf, wsem, wgb, wub, wdb, *, tm, sub, layer, max_tiles):
    n = n_ref[0]
    rows = tm * PACKED_SLABS
    tile_rows = lambda hbm, t: hbm.at[pl.ds(pl.multiple_of(t * rows, rows), rows)]
    x_copy = lambda t, slot: pltpu.make_async_copy(tile_rows(xs_hbm, t), xbuf.at[slot], xsem.at[slot])
    y_copy = lambda t, slot: pltpu.make_async_copy(ybuf.at[slot], tile_rows(ys_hbm, t), ysem.at[slot])

    def weight_copies(e, slot):
        return [pltpu.make_async_copy(hbm.at[layer, e], buf.at[slot], wsem.at[slot])
                for hbm, buf in ((wg_hbm, wgf), (wu_hbm, wuf), (wd_hbm, wdf))]

    x_copy(0, 0).start()

    def item(w, carry):
        io = lax.rem(w, 2)
        x_copy(w, io).wait()

        @pl.when(w + 1 < n)
        def _():
            x_copy(w + 1, 1 - io).start()

        @pl.when(w >= 2)
        def _():
            y_copy(w - 2, io).wait()

        @pl.when(new_ref[w] != 0)
        def _():
            slot = wslot_ref[w]

            @pl.when(w == 0)
            def _():
                for c in weight_copies(exp_ref[w], slot):
                    c.start()

            for c in weight_copies(exp_ref[w], slot):
                c.wait()
            wgb[...] = wgf[slot].astype(BF16)
            wub[...] = wuf[slot].astype(BF16)
            wdb[...] = wdf[slot].astype(BF16)

            @pl.when(next_ref[w] >= 0)
            def _():
                for c in weight_copies(next_ref[w], 1 - slot):
                    c.start()

        n_blocks = tm // sub

        def up(s):
            x = _unpack_rows(xbuf, sub, (io,), row0=s * sub)
            return (jnp.dot(x, wgb[...], preferred_element_type=F32),
                    jnp.dot(x, wub[...], preferred_element_type=F32))

        ups = {0: up(0)}
        for s in range(n_blocks):
            if s + 1 < n_blocks:
                ups[s + 1] = up(s + 1)
            hg, hu = ups.pop(s)
            act = (hg * jax.nn.sigmoid(hg) * hu).astype(BF16)
            y = jnp.dot(act, wdb[...], preferred_element_type=F32)
            _pack_rows(ybuf, y, sub, (io,), row0=s * sub)
        y_copy(w, io).start()
        return carry

    lax.fori_loop(0, n, item, 0)

    @pl.when(n >= 2)
    def _():
        y_copy(n - 2, lax.rem(n, 2)).wait()
    y_copy(n - 1, lax.rem(n - 1, 2)).wait()

    for wait in (False, True):
        def spare(t, carry, wait=wait):
            copy = y_copy(n + t, 0)
            copy.wait() if wait else copy.start()
            return carry

        lax.fori_loop(0, max_tiles - n, spare, 0)


def _experts(xs, items, w_g, w_u, w_d, layer):
    d = D_MODEL
    tm = MOE_TILE
    max_tiles = items[0].shape[0]
    any_spec = pl.BlockSpec(memory_space=pl.ANY)
    io_buf = pltpu.VMEM((2, tm * PACKED_SLABS, LANES), jnp.uint32)
    return pl.pallas_call(
        functools.partial(_expert_kernel, tm=tm, sub=MOE_SUB, layer=layer, max_tiles=max_tiles),
        grid_spec=pltpu.PrefetchScalarGridSpec(
            num_scalar_prefetch=len(items),
            grid=(1,),
            in_specs=[any_spec, any_spec, any_spec, any_spec],
            out_specs=any_spec,
            scratch_shapes=[io_buf, io_buf, pltpu.SemaphoreType.DMA((2,)), pltpu.SemaphoreType.DMA((2,)),
                            pltpu.VMEM((2, d, D_EXPERT), F32), pltpu.VMEM((2, d, D_EXPERT), F32),
                            pltpu.VMEM((2, D_EXPERT, d), F32), pltpu.SemaphoreType.DMA((2,)),
                            pltpu.VMEM((d, D_EXPERT), BF16), pltpu.VMEM((d, D_EXPERT), BF16),
                            pltpu.VMEM((D_EXPERT, d), BF16)],
        ),
        out_shape=jax.ShapeDtypeStruct(xs.shape, jnp.uint32),
        compiler_params=_params(("arbitrary",)),
        name="moe_experts",
    )(*items, xs, w_g, w_u, w_d)


def _combine_kernel(runs_ref, h_ref, gate_ref, ys_hbm, *refs, tm, steps, prompt_tiles):
    out_refs, (buf, sem) = refs[:-2], refs[-2:]
    i = pl.program_id(0)
    slot = lax.rem(i, 2)
    ns = 2 * tm

    def issue(step, sl):
        _copy_runs(runs_ref, step, lambda start, size: buf.at[sl, pl.ds(start, size)],
                   lambda start, size: ys_hbm.at[pl.ds(start, size)], sem.at[sl], to_global=False)

    @pl.when(i == 0)
    def _():
        issue(0, 0)

    @pl.when(i + 1 < steps)
    def _():
        issue(i + 1, 1 - slot)

    pltpu.make_async_copy(ys_hbm.at[pl.ds(0, ns * PACKED_SLABS)], buf.at[slot], sem.at[slot]).wait()
    y = _unpack_rows(buf, ns, (slot,))
    g = gate_ref[...]
    j = lax.broadcasted_iota(jnp.int32, (tm, ns), 1)
    mix = (jnp.where(j == g[:, 2:3].astype(jnp.int32), g[:, 0:1], 0.0)
           + jnp.where(j == g[:, 3:4].astype(jnp.int32), g[:, 1:2], 0.0)).astype(BF16)
    out = h_ref[...] + jnp.dot(mix, y, preferred_element_type=F32)
    if len(out_refs) == 1:
        out_refs[0][...] = out
    else:
        @pl.when(i < prompt_tiles)
        def _():
            out_refs[0][...] = out

        @pl.when(i >= prompt_tiles)
        def _():
            out_refs[1][...] = out


def _combine(h, gates, ys, runs, tm, split_rows=None):
    n, d = h.shape
    steps = n // tm
    row = pl.BlockSpec((tm, d), lambda i, pos: (i, 0))
    if split_rows is None:
        pt, out_specs, out_shape = 0, row, jax.ShapeDtypeStruct((n, d), F32)
    else:
        assert split_rows % tm == 0
        pt = split_rows // tm
        out_specs = _group_specs(tm, d, pt)
        out_shape = [jax.ShapeDtypeStruct((split_rows, d), F32), jax.ShapeDtypeStruct((n - split_rows, d), F32)]
    return pl.pallas_call(
        functools.partial(_combine_kernel, tm=tm, steps=steps, prompt_tiles=pt),
        grid_spec=pltpu.PrefetchScalarGridSpec(
            num_scalar_prefetch=1,
            grid=(steps,),
            in_specs=[row, pl.BlockSpec((tm, LANES), lambda i, pos: (i, 0)),
                      pl.BlockSpec(memory_space=pl.ANY)],
            out_specs=out_specs,
            scratch_shapes=[pltpu.VMEM((2, 2 * tm * PACKED_SLABS, LANES), jnp.uint32),
                            pltpu.SemaphoreType.DMA((2,))],
        ),
        out_shape=out_shape,
        compiler_params=_params(("arbitrary",)),
        name="moe_combine",
    )(runs, h, gates, ys)


def _lookup(tables, idx):
    hit = idx[:, None] == jnp.arange(tables.shape[1], dtype=idx.dtype)[None, :]
    return jnp.sum(jnp.where(hit[None], tables[:, None, :], 0), axis=2)


def _work_items(tiles_e, max_items):
    item_end = jnp.cumsum(tiles_e)
    n_items = item_end[-1]
    w = jnp.minimum(jnp.arange(max_items, dtype=jnp.int32), n_items - 1)
    expert = jnp.sum(w[:, None] >= item_end[None, :], axis=1).astype(jnp.int32)
    prev_expert = jnp.concatenate([jnp.full((1,), -1, jnp.int32), expert[:-1]])
    new_expert = expert != prev_expert
    weight_slot = (jnp.cumsum(new_expert.astype(jnp.int32)) - 1) % 2
    ids = jnp.arange(N_EXPERTS, dtype=jnp.int32)
    later = (ids[None, :] > ids[:, None]) & (tiles_e[None, :] > 0)
    following = jnp.min(jnp.where(later, ids[None, :], N_EXPERTS), axis=1)
    following = jnp.where(following == N_EXPERTS, -1, following)
    next_expert, = _lookup(following[None, :], expert)
    as_i32 = lambda a: a.astype(jnp.int32)
    return (expert, as_i32(new_expert), as_i32(weight_slot), as_i32(next_expert),
            as_i32(n_items).reshape(1))


def _moe(h, routed, g, w_g, w_u, w_d, layer, tm, split_rows=None):
    n, _ = h.shape
    meta, gates, cnt, tables = routed
    experts = slice(ROUTER_LANE0, ROUTER_LANE0 + N_EXPERTS)
    counts = cnt[0, experts].astype(jnp.int32)
    tiles_e = (counts + MOE_TILE - 1) // MOE_TILE
    starts = (jnp.cumsum(tiles_e) - tiles_e) * MOE_TILE
    max_tiles = -(-2 * n // MOE_TILE) + N_EXPERTS
    tables = tables.reshape(n // tm, 8, LANES)[:, :RUN_FIELDS, experts]
    runs = tables.at[:, 0, :].add(starts[None, :]).reshape(-1)
    used = jnp.sum(tiles_e)
    pads = jnp.concatenate([starts + counts, tiles_e * MOE_TILE - counts, jnp.stack([used, max_tiles - used])])
    xs = _dispatch(h, g, meta, runs, pads, tm, max_tiles * MOE_TILE)
    ys = _experts(xs, _work_items(tiles_e, max_tiles), w_g, w_u, w_d, layer)
    return _combine(h, gates, ys, runs, tm, split_rows)


def _inproj_b_kernel(x_ref, gkv_ref, gmix_ref, wkv_ref, win_ref, kng_ref, qng_ref, bdk_ref, hsum_ref, hexp_ref,
                     q_ref, mq_ref, k_ref, v_ref, kt_ref):
    tm = x_ref.shape[0]
    sub = min(tm, 256)
    blocks = [slice(r, r + sub) for r in range(0, tm, sub)]

    def two_piece_dot(x, m_ref):
        out = None
        for p in _split(x, 2):
            t = jnp.dot(p, m_ref[...], preferred_element_type=F32)
            out = t if out is None else out + t
        return out

    kvs, projs = [], []
    for rows in blocks:
        x = x_ref[rows, :]
        xr = x * _rms_scale(x)
        kvs.append(_bdot(xr * gkv_ref[...], wkv_ref[...]))
        projs.append(_bdot(xr * gmix_ref[...], win_ref[...]))
    k_ms = [two_piece_dot(kv[:, :KV_WIDTH] * kv[:, :KV_WIDTH], bdk_ref) for kv in kvs]
    q_ms = [two_piece_dot(proj[:, :MAIN_WIDTH] * proj[:, :MAIN_WIDTH], hsum_ref) for proj in projs]
    q_scale = [two_piece_dot(lax.rsqrt(ms + EPS), hexp_ref) for ms in q_ms]
    for rows, kv, proj, kms, qs in zip(blocks, kvs, projs, k_ms, q_scale):
        k = kv[:, :KV_WIDTH] * lax.rsqrt(kms + EPS) * kng_ref[...]
        k_ref[rows, :] = k
        kt_ref[:, rows] = k.T.astype(BF16)
        v_ref[rows, :] = kv[:, KV_WIDTH:]
        q_ref[rows, :] = (proj[:, :MAIN_WIDTH] * qs * qng_ref[...]).astype(BF16)
        mq_ref[rows, :] = proj[:, MAIN_WIDTH:]


def _swa_perm():
    g, kh, dd = np.meshgrid(np.arange(SWA_GROUP), np.arange(SWA_KV_HEADS), np.arange(HEAD_DIM), indexing="ij")
    return ((kh * SWA_GROUP + g) * HEAD_DIM + dd).reshape(-1)


def _inproj_b(x, g_kv, g_mix, w_kv, w_in, kng, qng):
    n, d = x.shape
    tm = _row_tile(n, 1024)
    perm = _swa_perm()
    w_in_p = jnp.concatenate([w_in[:, :MAIN_WIDTH][:, perm], w_in[:, MAIN_WIDTH:]], axis=1).astype(BF16)
    qng_t = (jnp.tile(qng, SWA_HEADS) * HEAD_DIM ** -0.5).reshape(1, MAIN_WIDTH)
    member = (np.arange(MAIN_WIDTH)[:, None] // HEAD_DIM == np.arange(LANES)[None, :]).astype(np.float32)
    row = lambda w: pl.BlockSpec((tm, w), lambda i: (i, 0))
    return pl.pallas_call(
        _inproj_b_kernel,
        grid=(n // tm,),
        in_specs=[row(d), _const((1, d)), _const((1, d)), _const((d, 2 * KV_WIDTH)), _const((d, d)),
                  _const((1, KV_WIDTH)), _const((1, MAIN_WIDTH)), _const((KV_WIDTH, KV_WIDTH)),
                  _const((MAIN_WIDTH, LANES)), _const((LANES, MAIN_WIDTH))],
        out_specs=[row(MAIN_WIDTH), row(MEM_WIDTH), row(KV_WIDTH), row(KV_WIDTH),
                   pl.BlockSpec((KV_WIDTH, tm), lambda i: (0, i))],
        out_shape=[jax.ShapeDtypeStruct((n, MAIN_WIDTH), BF16), jax.ShapeDtypeStruct((n, MEM_WIDTH), F32),
                   jax.ShapeDtypeStruct((n, KV_WIDTH), F32), jax.ShapeDtypeStruct((n, KV_WIDTH), F32),
                   jax.ShapeDtypeStruct((KV_WIDTH, n), BF16)],
        compiler_params=_params(("parallel",)),
        name="inproj_b",
    )(x, g_kv.reshape(1, d), g_mix.reshape(1, d), w_kv.astype(BF16), w_in_p,
      jnp.tile(kng, SWA_KV_HEADS).reshape(1, KV_WIDTH), qng_t,
      _block_diag_mean(KV_WIDTH), jnp.asarray(member / HEAD_DIM, BF16), jnp.asarray(member.T, BF16))


def _softmax_with_sink(s, sink):
    m = jnp.maximum(jnp.max(s, axis=-1, keepdims=True), sink)
    e = jnp.exp(s - m)
    r = 1.0 / (jnp.sum(e, axis=-1, keepdims=True) + jnp.exp(sink - m))
    return (e * r).astype(BF16)


def _swa_bias(tq):
    slopes = 2.0 ** (-8.0 * np.arange(1, SWA_HEADS + 1, dtype=np.float64) / SWA_HEADS)
    dist = np.arange(tq)[:, None] + WINDOW - np.arange(WINDOW + tq)[None, :]
    valid = (dist >= 0) & (dist <= WINDOW)
    return np.stack([np.where(valid, -s * dist, NEG_BIG) for s in slopes]).astype(np.float32)


def _swa_prompt_kernel(sink_ref, q_ref, ktp_ref, kto_ref, vp_ref, vo_ref, bias_ref, hm_ref, o_ref, *, nb):
    w = WINDOW
    key = lax.broadcasted_iota(jnp.int32, (w, 2 * w), 1)
    has_prev = (pl.program_id(0) > 0) | (key >= w)
    heads = [(g, kh) for g in range(SWA_GROUP) for kh in range(SWA_KV_HEADS)]
    kts, vvs = [], []
    for b in range(nb):
        kt_prev = ktp_ref[...] if b == 0 else kto_ref[:, (b - 1) * w:b * w]
        v_prev = vp_ref[...] if b == 0 else vo_ref[(b - 1) * w:b * w, :]
        kts.append(jnp.concatenate([kt_prev, kto_ref[:, b * w:(b + 1) * w]], axis=1).astype(BF16))
        vvs.append(jnp.concatenate([v_prev, vo_ref[b * w:(b + 1) * w, :]], axis=0).astype(BF16))
    scores = [[jnp.dot(q_ref[b * w:(b + 1) * w, g * KV_WIDTH:(g + 1) * KV_WIDTH] * hm_ref[kh].astype(BF16),
                       kts[b], preferred_element_type=F32) for g, kh in heads] for b in range(nb)]
    for b in range(nb):
        probs = []
        for (g, kh), s in zip(heads, scores[b]):
            h = kh * SWA_GROUP + g
            s = s + bias_ref[h]
            if b == 0:
                s = jnp.where(has_prev, s, NEG_BIG)
            probs.append(_softmax_with_sink(s, sink_ref[h]))
        outs = [jnp.dot(p, vvs[b], preferred_element_type=F32) for p in probs]
        lane_head = lax.broadcasted_iota(jnp.int32, (w, KV_WIDTH), 1) // HEAD_DIM
        for g in range(SWA_GROUP):
            acc = None
            for (cg, kh), o in zip(heads, outs):
                if cg == g:
                    acc = o if acc is None else jnp.where(lane_head == kh, o, acc)
            o_ref[b * w:(b + 1) * w, g * KV_WIDTH:(g + 1) * KV_WIDTH] = acc.astype(BF16)


def _swa_prompt(q, kt, v, sinks, *, n_rows, nb):
    w = WINDOW
    step = nb * w
    assert n_rows % step == 0
    prev = lambda j, sink: jnp.maximum(j * nb - 1, 0)
    return pl.pallas_call(
        functools.partial(_swa_prompt_kernel, nb=nb),
        grid_spec=pltpu.PrefetchScalarGridSpec(
            num_scalar_prefetch=1,
            grid=(n_rows // step,),
            in_specs=[pl.BlockSpec((step, MAIN_WIDTH), lambda j, sink: (j, 0)),
                      pl.BlockSpec((KV_WIDTH, w), lambda j, sink: (0, prev(j, sink))),
                      pl.BlockSpec((KV_WIDTH, step), lambda j, sink: (0, j)),
                      pl.BlockSpec((w, KV_WIDTH), lambda j, sink: (prev(j, sink), 0)),
                      pl.BlockSpec((step, KV_WIDTH), lambda j, sink: (j, 0)),
                      pl.BlockSpec((SWA_HEADS, w, 2 * w), lambda j, sink: (0, 0, 0)),
                      pl.BlockSpec((SWA_KV_HEADS, 1, KV_WIDTH), lambda j, sink: (0, 0, 0))],
            out_specs=pl.BlockSpec((step, MAIN_WIDTH), lambda j, sink: (j, 0)),
        ),
        out_shape=jax.ShapeDtypeStruct((n_rows, MAIN_WIDTH), BF16),
        compiler_params=_params(("arbitrary",)),
        name="swa_prompt",
    )(sinks.astype(F32), q, kt, kt, v, v, jnp.asarray(_swa_bias(w)), _head_masks(SWA_KV_HEADS))


def _swa_sample_kernel(q_ref, kp_ref, ko_ref, vp_ref, vo_ref, bias_ref, sink_ref, hm_ref, o_ref, *, nb, tq):
    w = WINDOW
    heads = [(kh, g) for kh in range(SWA_KV_HEADS) for g in range(SWA_GROUP)]
    kks, vvs, scores = [], [], []
    q = q_ref[...].astype(F32)
    for i in range(nb):
        win = slice(i * KV_WIDTH, (i + 1) * KV_WIDTH)
        kks.append(jnp.concatenate([kp_ref[win, :].T, ko_ref[i * tq:(i + 1) * tq, :]], axis=0))
        vvs.append(jnp.concatenate([vp_ref[win, :].T, vo_ref[i * tq:(i + 1) * tq, :]], axis=0))
        qs = jnp.concatenate([q[i * tq:(i + 1) * tq, g * KV_WIDTH:(g + 1) * KV_WIDTH] * hm_ref[kh]
                              for kh, g in heads], axis=0)
        scores.append(_bdot_nt(qs, kks[i]))
    probs = [_softmax_with_sink(s + bias_ref[...], sink_ref[...]) for s in scores]
    outs = [_bdot(p, vv) for p, vv in zip(probs, vvs)]
    for g in range(SWA_GROUP):
        rows = []
        for i in range(nb):
            acc = None
            for r, (kh, hg) in enumerate(heads):
                if hg == g:
                    t = outs[i][r * tq:(r + 1) * tq] * hm_ref[kh]
                    acc = t if acc is None else acc + t
            rows.append(acc)
        o_ref[:, g * KV_WIDTH:(g + 1) * KV_WIDTH] = jnp.concatenate(rows, axis=0).astype(BF16)


def _swa_sample(q, k_win, v_win, k, v, sinks, *, row_off, batch, tq, nb):
    w = WINDOW
    assert batch % nb == 0 and row_off % (nb * tq) == 0
    off = row_off // (nb * tq)
    bias = jnp.asarray(_swa_bias(tq).reshape(SWA_HEADS * tq, w + tq))
    sink_col = jnp.repeat(sinks.astype(F32), tq).reshape(SWA_HEADS * tq, 1)
    own = lambda width: pl.BlockSpec((nb * tq, width), lambda b: (off + b, 0))
    win = pl.BlockSpec((nb * KV_WIDTH, w), lambda b: (b, 0))
    return pl.pallas_call(
        functools.partial(_swa_sample_kernel, nb=nb, tq=tq),
        grid=(batch // nb,),
        in_specs=[own(MAIN_WIDTH), win, own(KV_WIDTH), win, own(KV_WIDTH), _const(bias.shape),
                  _const(sink_col.shape), _const((SWA_KV_HEADS, 1, KV_WIDTH))],
        out_specs=pl.BlockSpec((nb * tq, MAIN_WIDTH), lambda b: (b, 0)),
        out_shape=jax.ShapeDtypeStruct((batch * tq, MAIN_WIDTH), BF16),
        compiler_params=_params(("arbitrary",)),
        name="swa_sample",
    )(q, k_win, k, v_win, v, bias, sink_col, _head_masks(SWA_KV_HEADS))


def kernel(x_prompt, x_sample, state_gla, cache_win_k, cache_win_v, cache_mem_k, cache_mem_v, mem_prompt, norm_mix_g, norm_ffn_g, norm_mem_g, w_mem_kv, mem_qn_g, mem_kn_g, w_out, w_in_a, w_gate_lr, b_gate_lr, gla_norm_g, w_in_b, swa_qn_g, swa_sinks, norm_kv_g, w_kv, swa_kn_g, w_router_group, b_router_group, w_router_expert, b_router_expert, w_exp_gate, w_exp_up, w_exp_down):
    bp, tp, d = x_prompt.shape
    bs, ts, _ = x_sample.shape
    assert bp == 1 and tp % WINDOW == 0 and ts * (GLA_CHUNK // ts) == GLA_CHUNK
    n_p, n_s = bp * tp, bs * ts
    w_buf = cache_win_k.shape[1]
    assert w_buf == WINDOW
    x_p, x_s = x_prompt.reshape(n_p, d), x_sample.reshape(n_s, d)

    mem_k_p, mem_v_p = _mem_kv(mem_prompt, norm_mem_g, w_mem_kv, mem_kn_g)
    feature_major = lambda c: jnp.moveaxis(c, -3, -1).reshape(*c.shape[:-3], c.shape[-2] * c.shape[-1], c.shape[-3])
    cmk, cmv = feature_major(cache_mem_k), feature_major(cache_mem_v)

    def mem_attend(mq, l):
        tm_p = _row_tile(tp, 1024)
        mo_p = _mem_attn(mq, mem_k_p, mem_v_p, mem_qn_g[l], row_off=0, seq=tp, tm=tm_p, bb=1, layer=l)
        mo_s = _mem_attn(mq, cmk, cmv, mem_qn_g[l], row_off=n_p, seq=ts, tm=ts, bb=16, layer=l)
        return mo_p, mo_s

    tm = _row_tile(math.gcd(n_p, n_s), MOE_TILE)
    router = lambda l: (norm_ffn_g[l], w_router_group[l], b_router_group[l], w_router_expert[l],
                        b_router_expert[l])

    def moe(h, routed, l, split_rows=None):
        return _moe(h, routed, norm_ffn_g[l], w_exp_gate, w_exp_up, w_exp_down, l, tm, split_rows)

    q, k, la, v, og, mq = _inproj_a(x_p, x_s, norm_mix_g[0], w_in_a[0], w_gate_lr[0], b_gate_lr[0])
    zero_state = jnp.zeros((bp, GLA_HEADS, GLA_DK, GLA_DV), F32)
    n_sub = max(1, min(16, tp // GLA_CHUNK))
    main_p, gla_p = _gla(q, k, la, v, og, zero_state, gla_norm_g[0], row_off=0, seq=tp, n_seg=1, n_sub=n_sub)
    main_s, gla_s = _gla(q, k, la, v, og, state_gla[0], gla_norm_g[0], row_off=n_p, seq=ts,
                         n_seg=GLA_CHUNK // ts, n_sub=1)
    mo_p, mo_s = mem_attend(mq, 0)
    w_o = w_out[0]
    h, routed = _outproj((x_p, x_s), main_p, mo_p, main_s, mo_s, w_o[:MAIN_WIDTH].reshape(GLA_HEADS, GLA_DV, d),
                         w_o[MAIN_WIDTH:], router(0), tm)
    h = moe(h, routed, 0)

    q, mq, k_sh, v_sh, kt_sh = _inproj_b(h, norm_kv_g, norm_mix_g[1], w_kv, w_in_b[0], swa_kn_g, swa_qn_g[0])
    ck = feature_major(cache_win_k).reshape(bs * KV_WIDTH, w_buf)
    cv = feature_major(cache_win_v).reshape(bs * KV_WIDTH, w_buf)
    main_p = _swa_prompt(q, kt_sh, v_sh, swa_sinks[0], n_rows=n_p, nb=8)
    main_s = _swa_sample(q, ck, cv, k_sh, v_sh, swa_sinks[0], row_off=n_p, batch=bs, tq=ts, nb=16)
    mo_p, mo_s = mem_attend(mq, 1)
    w_o = w_out[1]
    h, routed = _outproj(h, main_p, mo_p, main_s, mo_s, w_o[:MAIN_WIDTH][_swa_perm()], w_o[MAIN_WIDTH:],
                         router(1), tm)
    y_p, y_s = moe(h, routed, 1, split_rows=n_p)

    y_prompt = y_p.reshape(bp, tp, d)
    y_sample = y_s.reshape(bs, ts, d)
    k_new = k_sh[n_p:].reshape(bs, ts, SWA_KV_HEADS, HEAD_DIM)
    v_new = v_sh[n_p:].reshape(bs, ts, SWA_KV_HEADS, HEAD_DIM)
    win_k_s = jnp.concatenate([cache_win_k, k_new], axis=1)[:, -w_buf:]
    win_v_s = jnp.concatenate([cache_win_v, v_new], axis=1)[:, -w_buf:]
    win_k_p = k_sh[n_p - WINDOW:n_p].reshape(bp, WINDOW, SWA_KV_HEADS, HEAD_DIM)
    win_v_p = v_sh[n_p - WINDOW:n_p].reshape(bp, WINDOW, SWA_KV_HEADS, HEAD_DIM)
    token_major = lambda c: jnp.moveaxis(c.reshape(*c.shape[:-2], MEM_HEADS, HEAD_DIM, c.shape[-1]), -1, -3)
    return (y_prompt, y_sample, gla_p[None], gla_s[None], win_k_p, win_v_p, win_k_s, win_v_s,
            token_major(mem_k_p), token_major(mem_v_p))
```

```python
import functools
import math

import numpy as np
import jax
import jax.numpy as jnp
from jax import lax
from jax.experimental import pallas as pl
from jax.experimental.pallas import tpu as pltpu

F32 = jnp.float32
BF16 = jnp.bfloat16

D_MODEL = 1024
MEM_HEADS = 4
HEAD_DIM = 64
MEM_WIDTH = MEM_HEADS * HEAD_DIM
MAIN_WIDTH = D_MODEL - MEM_WIDTH
GLA_HEADS = 4
GLA_DV = MAIN_WIDTH // GLA_HEADS
GLA_DK = GLA_DV // 2
GLA_DK_PAD = 128
GLA_KEY_WIDTH = GLA_HEADS * GLA_DK
GLA_KEY_PAD = GLA_HEADS * GLA_DK_PAD
GLA_GATE_RANK = 16
GLA_TAU = 16.0
GLA_CHUNK = 64
SWA_HEADS = MAIN_WIDTH // HEAD_DIM
SWA_KV_HEADS = 4
SWA_GROUP = SWA_HEADS // SWA_KV_HEADS
KV_WIDTH = SWA_KV_HEADS * HEAD_DIM
WINDOW = 128
N_GROUPS = 4
EXPERTS_PER_GROUP = 8
N_EXPERTS = N_GROUPS * EXPERTS_PER_GROUP
D_EXPERT = 512
EPS = 1e-6
LANES = 128
NEG_BIG = -1e30
VMEM_LIMIT = 56 * 1024 * 1024
MOE_TILE = 512
MOE_SUB = 128
ROUTER_LANE0 = N_GROUPS
ROUTER_META_ROWS = 8
ROUTER_ROWS = 40
SLABS = D_MODEL // LANES
PACKED_SLABS = SLABS // 2


def _bdot(a, b):
    return jnp.dot(a.astype(BF16), b.astype(BF16), preferred_element_type=F32)


def _bdot_nt(a, b):
    return lax.dot_general(a.astype(BF16), b.astype(BF16), (((1,), (1,)), ((), ())),
                           preferred_element_type=F32)


def _bdot_tn(a, b):
    return lax.dot_general(a.astype(BF16), b.astype(BF16), (((0,), (0,)), ((), ())),
                           preferred_element_type=F32)


def _split(x, n):
    parts = []
    for _ in range(n - 1):
        p = x.astype(BF16)
        parts.append(p)
        x = x - p.astype(F32)
    parts.append(x.astype(BF16))
    return parts


def _exact_left_dot(m, x, n=2):
    out = None
    for p in _split(x, n):
        t = jnp.dot(m, p, preferred_element_type=F32)
        out = t if out is None else out + t
    return out


def _seg_mean(x2, bd):
    out = None
    for p in _split(x2, 2):
        t = jnp.dot(p, bd, preferred_element_type=F32)
        out = t if out is None else out + t
    return out


def _rms_scale(x):
    return lax.rsqrt(jnp.mean(x * x, axis=-1, keepdims=True) + EPS)


def _row_tile(n, cap=512):
    t = cap
    while t > 8 and n % t:
        t //= 2
    assert n % t == 0, n
    return t


def _params(sem):
    return pltpu.CompilerParams(dimension_semantics=sem, vmem_limit_bytes=VMEM_LIMIT)


def _const(shape):
    nd = len(shape)
    return pl.BlockSpec(shape, lambda *_: (0,) * nd)


def _group_specs(tm, width, prompt_tiles, lead=None):
    p_idx = lambda i, *_: jnp.minimum(i, prompt_tiles - 1)
    s_idx = lambda i, *_: jnp.maximum(i - prompt_tiles, 0)
    if lead is None:
        return [pl.BlockSpec((tm, width), lambda i, *_, f=f: (f(i), 0)) for f in (p_idx, s_idx)]
    return [pl.BlockSpec((lead, tm, width), lambda i, *_, f=f: (0, f(i), 0)) for f in (p_idx, s_idx)]


def _block_diag_mean(width):
    i = np.arange(width)
    return jnp.asarray((i[:, None] // HEAD_DIM == i[None, :] // HEAD_DIM) / HEAD_DIM, BF16)


def _head_masks(n_heads):
    i = np.arange(n_heads * HEAD_DIM)
    return jnp.asarray((i[None, :] // HEAD_DIM == np.arange(n_heads)[:, None]), F32)[:, None, :]


def _mem_kv_kernel(mem_ref, g_ref, w_ref, kng_ref, bd_ref, k_ref, v_ref):
    x = mem_ref[0]
    hn = x * _rms_scale(x) * g_ref[0]
    kv = _bdot(hn, w_ref[0])
    k = kv[:, :MEM_WIDTH]
    k = k * lax.rsqrt(_seg_mean(k * k, bd_ref[...]) + EPS) * kng_ref[0]
    k_ref[0, 0] = k.T
    v_ref[0, 0] = kv[:, MEM_WIDTH:].T


def _mem_kv(mem, g, w, kng):
    depth, (b, m, d) = w.shape[0], mem.shape
    out = jax.ShapeDtypeStruct((depth, b, m, MEM_WIDTH), F32)
    blk = pl.BlockSpec((1, 1, m, MEM_WIDTH), lambda l, i: (l, i, 0, 0))
    return pl.pallas_call(
        _mem_kv_kernel,
        grid=(depth, b),
        in_specs=[pl.BlockSpec((1, m, d), lambda l, i: (i, 0, 0)),
                  pl.BlockSpec((1, 1, d), lambda l, i: (l, 0, 0)),
                  pl.BlockSpec((1, d, 2 * MEM_WIDTH), lambda l, i: (l, 0, 0)),
                  pl.BlockSpec((1, 1, MEM_WIDTH), lambda l, i: (l, 0, 0)),
                  _const((MEM_WIDTH, MEM_WIDTH))],
        out_specs=[blk, blk],
        out_shape=[out, out],
        compiler_params=_params(("arbitrary", "arbitrary")),
        name="mem_kv",
    )(mem, g.reshape(depth, 1, d), w.astype(BF16),
      jnp.tile(kng, (1, MEM_HEADS)).reshape(depth, 1, MEM_WIDTH), _block_diag_mean(MEM_WIDTH))


def _inproj_a_kernel(xp_ref, xs_ref, g_ref, wq_ref, wk_ref, wv_ref, wog_ref, wlr_ref, wmq_ref, wgl_ref, bgl_ref,
                     q_ref, k_ref, la_ref, v_ref, og_ref, mq_ref, *, prompt_tiles):
    is_prompt = pl.program_id(0) < prompt_tiles
    tm = q_ref.shape[0]
    sub = min(tm, 128)
    for r in range(0, tm, sub):
        rows = slice(r, r + sub)
        x = jnp.where(is_prompt, xp_ref[rows, :], xs_ref[rows, :])
        hn = (x * _rms_scale(x) * g_ref[...]).astype(BF16)
        q_ref[rows, :] = jnp.dot(hn, wq_ref[...], preferred_element_type=F32) * (GLA_DK ** -0.5)
        k_ref[rows, :] = jnp.dot(hn, wk_ref[...], preferred_element_type=F32)
        for h in range(GLA_HEADS):
            v_ref[h, rows, :] = jnp.dot(hn, wv_ref[h], preferred_element_type=F32).astype(BF16)
            og_ref[h, rows, :] = jnp.dot(hn, wog_ref[h], preferred_element_type=F32)
        lr = jnp.dot(hn, wlr_ref[...], preferred_element_type=F32)
        z = _bdot(lr, wgl_ref[...]) + bgl_ref[...]
        la_ref[rows, :] = (jnp.minimum(z, 0.0) - jnp.log(1.0 + jnp.exp(-jnp.abs(z)))) * (1.0 / GLA_TAU)
        mq_ref[rows, :] = jnp.dot(hn, wmq_ref[...], preferred_element_type=F32)


def _pad_heads(w, width, pad):
    lead = w.shape[:-1]
    w = w.reshape(*lead, GLA_HEADS, width)
    w = jnp.pad(w, [(0, 0)] * len(lead) + [(0, 0), (0, pad - width)])
    return w.reshape(*lead, GLA_HEADS * pad)


def _inproj_a(x_p, x_s, g, w_in, w_lr, b_lr):
    (n_p, d), n_s = x_p.shape, x_s.shape[0]
    n = n_p + n_s
    tm = _row_tile(n_s)
    assert n_p % tm == 0
    pt = n_p // tm
    c0, c1, c2, c3, c4 = (GLA_KEY_WIDTH, 2 * GLA_KEY_WIDTH, 2 * GLA_KEY_WIDTH + MAIN_WIDTH,
                          2 * GLA_KEY_WIDTH + 2 * MAIN_WIDTH,
                          2 * GLA_KEY_WIDTH + 2 * MAIN_WIDTH + GLA_GATE_RANK)
    wb = w_in.astype(BF16)
    wq = _pad_heads(wb[:, :c0], GLA_DK, GLA_DK_PAD)
    wk = _pad_heads(wb[:, c0:c1], GLA_DK, GLA_DK_PAD)
    wv = wb[:, c1:c2].reshape(d, GLA_HEADS, GLA_DV).transpose(1, 0, 2)
    wog = wb[:, c2:c3].reshape(d, GLA_HEADS, GLA_DV).transpose(1, 0, 2)
    wlr = jnp.pad(wb[:, c3:c4], ((0, 0), (0, LANES - GLA_GATE_RANK)))
    wmq = wb[:, c4:]
    wgl = jnp.pad(_pad_heads(w_lr.astype(BF16), GLA_DK, GLA_DK_PAD), ((0, LANES - GLA_GATE_RANK), (0, 0)))
    bgl = _pad_heads(b_lr.reshape(1, -1), GLA_DK, GLA_DK_PAD)
    row = lambda w: pl.BlockSpec((tm, w), lambda i: (i, 0))
    hrow = pl.BlockSpec((GLA_HEADS, tm, GLA_DV), lambda i: (0, i, 0))
    key = jax.ShapeDtypeStruct((n, GLA_KEY_PAD), F32)
    val = jax.ShapeDtypeStruct((GLA_HEADS, n, GLA_DV), F32)
    return pl.pallas_call(
        functools.partial(_inproj_a_kernel, prompt_tiles=pt),
        grid=(n // tm,),
        in_specs=_group_specs(tm, d, pt) + [
            _const((1, d)), _const(wq.shape), _const(wk.shape), _const(wv.shape),
            _const(wog.shape), _const(wlr.shape), _const(wmq.shape), _const(wgl.shape),
            _const(bgl.shape)],
        out_specs=[row(GLA_KEY_PAD), row(GLA_KEY_PAD), row(GLA_KEY_PAD), hrow, hrow, row(MEM_WIDTH)],
        out_shape=[key, key, key, jax.ShapeDtypeStruct(val.shape, BF16), val,
                   jax.ShapeDtypeStruct((n, MEM_WIDTH), F32)],
        compiler_params=_params(("parallel",)),
        name="inproj_a",
    )(x_p, x_s, g.reshape(1, d), wq, wk, wv, wog, wlr, wmq, wgl, bgl)


def _gla_kernel(q_ref, k_ref, la_ref, v_ref, og_ref, s0_ref, gn_ref, mcum_ref, mall_ref, sel_ref,
                o_ref, sout_ref, s_ref, *, chunk, n_sub, n_seg):
    j = pl.program_id(1)
    seg = chunk // n_seg

    @pl.when(j == 0)
    def _():
        s_ref[...] = jnp.zeros_like(s_ref)
        s_ref[:, :, :GLA_DK, :] = s0_ref[...]

    mcum = mcum_ref[...]
    causal = mcum.astype(F32) > 0.0
    row = lax.broadcasted_iota(jnp.int32, (chunk, GLA_DK_PAD), 0)
    gn = gn_ref[...]
    hcols = [slice(h * GLA_DK_PAD, (h + 1) * GLA_DK_PAD) for h in range(GLA_HEADS)]
    crows = [slice(c * chunk, (c + 1) * chunk) for c in range(n_sub)]
    qts, kts, kds, e_ends = [], [], [], []
    for rows in crows:
        la = la_ref[rows, :]
        b = _exact_left_dot(mcum, la)
        if n_seg == 1:
            b_end = b[chunk - 1:chunk, :]
            e_ends.append(jnp.broadcast_to(jnp.exp(b_end), (LANES, b.shape[1])).T)
        else:
            b_end = _exact_left_dot(mall_ref[...], la)
            e_ends.append(jnp.exp(_exact_left_dot(sel_ref[...], la)).T)
        k = k_ref[rows, :]
        qts.append(q_ref[rows, :] * jnp.exp(b))
        kts.append((k * jnp.exp(-b)).astype(BF16))
        kds.append(k * jnp.exp(b_end - b))
    vbs = [[v_ref[h, rows, :].astype(BF16) for h in range(GLA_HEADS)] for rows in crows]
    scores = [[_bdot_nt(qts[c][:, cols], kts[c][:, cols]) for cols in hcols] for c in range(n_sub)]
    kvs = []
    for c in range(n_sub):
        per_head = []
        for h, cols in enumerate(hcols):
            per_seg = []
            for s in range(n_seg):
                kd = kds[c][:, cols]
                if n_seg > 1:
                    kd = jnp.where((row >= s * seg) & (row < (s + 1) * seg), kd, 0.0)
                per_seg.append(_bdot_tn(kd, vbs[c][h]))
            per_head.append(per_seg)
        kvs.append(per_head)
    state = [[s_ref[s, h] for s in range(n_seg)] for h in range(GLA_HEADS)]
    inters = []
    for c in range(n_sub):
        per_head = []
        for h, cols in enumerate(hcols):
            parts = []
            for s in range(n_seg):
                parts.append(_bdot(qts[c][s * seg:(s + 1) * seg, cols], state[h][s]))
                state[h][s] = e_ends[c][cols, s:s + 1] * state[h][s] + kvs[c][h][s]
            per_head.append(parts[0] if n_seg == 1 else jnp.concatenate(parts, axis=0))
        inters.append(per_head)
    for h in range(GLA_HEADS):
        for s in range(n_seg):
            s_ref[s, h] = state[h][s]
    for c, rows in enumerate(crows):
        for h in range(GLA_HEADS):
            a = jnp.where(causal, scores[c][h], 0.0)
            o = _bdot(a, vbs[c][h]) + inters[c][h]
            on = o * lax.rsqrt(jnp.mean(o * o, axis=-1, keepdims=True) + EPS) * gn
            og = og_ref[h, rows, :]
            o_ref[h, rows, :] = (on * (og * jax.nn.sigmoid(og))).astype(BF16)

    @pl.when(j == pl.num_programs(1) - 1)
    def _():
        sout_ref[...] = s_ref[:, :, :GLA_DK, :]


def _gla(q, k, la, v, og, s0, gnorm, *, row_off, seq, n_seg, n_sub):
    batch = s0.shape[0]
    chunk = GLA_CHUNK
    assert chunk % n_seg == 0 and batch % n_seg == 0
    seg = chunk // n_seg
    step_rows = n_sub * chunk
    if n_seg > 1:
        assert seq == seg and n_sub == 1
        t_steps = 1
    else:
        assert seq % step_rows == 0
        t_steps = seq // step_rows
    assert row_off % step_rows == 0
    off = row_off // step_rows
    i = np.arange(chunk)
    same = (i[:, None] // seg) == (i[None, :] // seg)
    mcum = jnp.asarray(same & (i[None, :] <= i[:, None]), BF16)
    mall = jnp.asarray(same, BF16)
    sel = jnp.asarray((i[None, :] // seg) == np.arange(LANES)[:, None], BF16)
    ridx = lambda g, j: (off + g * t_steps + j, 0)
    hidx = lambda g, j: (0, off + g * t_steps + j, 0)
    key_spec = pl.BlockSpec((step_rows, GLA_KEY_PAD), ridx)
    val_spec = pl.BlockSpec((GLA_HEADS, step_rows, GLA_DV), hidx)
    st_spec = pl.BlockSpec((n_seg, GLA_HEADS, GLA_DK, GLA_DV), lambda g, j: (g, 0, 0, 0))
    in_specs = [key_spec, key_spec, key_spec, val_spec, val_spec, st_spec, _const((1, GLA_DV)),
                _const((chunk, chunk)), _const((chunk, chunk)), _const((LANES, chunk))]
    args = [q, k, la, v, og, s0, gnorm.reshape(1, GLA_DV), mcum, mall, sel]
    out_spec = pl.BlockSpec((GLA_HEADS, step_rows, GLA_DV), lambda g, j: (0, g * t_steps + j, 0))
    return pl.pallas_call(
        functools.partial(_gla_kernel, chunk=chunk, n_sub=n_sub, n_seg=n_seg),
        grid=(batch // n_seg, t_steps),
        in_specs=in_specs,
        out_specs=[out_spec, st_spec],
        out_shape=[jax.ShapeDtypeStruct((GLA_HEADS, batch * seq, GLA_DV), BF16),
                   jax.ShapeDtypeStruct(s0.shape, F32)],
        scratch_shapes=[pltpu.VMEM((n_seg, GLA_HEADS, GLA_DK_PAD, GLA_DV), F32)],
        compiler_params=_params(("arbitrary", "arbitrary")),
        name="gla",
    )(*args)


def _mem_attn_kernel(q_ref, k_ref, v_ref, g_ref, bd_ref, hm_ref, o_ref, *, tm, bb):
    g = g_ref[...]
    sub = min(tm, 128)
    units = [(i, i * tm + r) for i in range(bb) for r in range(0, tm, sub)]
    scores = []
    for i, r in units:
        q = q_ref[r:r + sub, :]
        qn = q * lax.rsqrt(_seg_mean(q * q, bd_ref[...]) + EPS) * g
        qs = jnp.concatenate([(qn * hm_ref[h]).astype(BF16) for h in range(MEM_HEADS)], axis=0)
        scores.append(_bdot(qs, k_ref[i]))
    probs = []
    for s in scores:
        e = jnp.exp(s - jnp.max(s, axis=-1, keepdims=True))
        probs.append(e * (1.0 / jnp.sum(e, axis=-1, keepdims=True)))
    outs = [_bdot_nt(p, v_ref[i]) for (i, _), p in zip(units, probs)]
    rows = []
    for o in outs:
        acc = o[:sub] * hm_ref[0]
        for h in range(1, MEM_HEADS):
            acc = acc + o[h * sub:(h + 1) * sub] * hm_ref[h]
        rows.append(acc)
    o_ref[...] = jnp.concatenate(rows, axis=0).astype(BF16)


def _mem_attn(mq, mk, mv, qng, *, row_off, seq, tm, bb, layer):
    depth, batch, m, _ = mk.shape
    mk = mk.reshape(depth * batch, m, MEM_WIDTH)
    mv = mv.reshape(depth * batch, m, MEM_WIDTH)
    kv_off = layer * batch // bb
    assert seq % tm == 0 and batch % bb == 0 and (bb == 1 or seq == tm)
    t_steps = seq // tm
    step_rows = bb * tm
    assert row_off % step_rows == 0
    off = row_off // step_rows
    row_spec = pl.BlockSpec((step_rows, MEM_WIDTH), lambda g, j: (off + g * t_steps + j, 0))
    kv_spec = pl.BlockSpec((bb, m, MEM_WIDTH), lambda g, j: (kv_off + g, 0, 0))
    in_specs = [row_spec, kv_spec, kv_spec, _const((1, MEM_WIDTH)), _const((MEM_WIDTH, MEM_WIDTH)),
                _const((MEM_HEADS, 1, MEM_WIDTH))]
    args = [mq, mk, mv, (jnp.tile(qng, MEM_HEADS) * HEAD_DIM ** -0.5).reshape(1, MEM_WIDTH),
            _block_diag_mean(MEM_WIDTH), _head_masks(MEM_HEADS)]
    return pl.pallas_call(
        functools.partial(_mem_attn_kernel, tm=tm, bb=bb),
        grid=(batch // bb, t_steps),
        in_specs=in_specs,
        out_specs=pl.BlockSpec((step_rows, MEM_WIDTH), lambda g, j: (g * t_steps + j, 0)),
        out_shape=jax.ShapeDtypeStruct((batch * seq, MEM_WIDTH), BF16),
        compiler_params=_params(("parallel", "parallel")),
        name="mem_attn",
    )(*args)


def _outproj_kernel(*refs, heads, prompt_tiles, split_residual):
    n_h = 2 if split_residual else 1
    h_refs, refs = refs[:n_h], refs[n_h:]
    (main_p_ref, main_s_ref, mo_p_ref, mo_s_ref, wmain_ref, wmo_ref), refs = refs[:6], refs[6:]
    route_in, (o_ref, *route_out) = refs[:6], refs[6:]
    is_prompt = pl.program_id(0) < prompt_tiles
    pick = lambda p, s: jnp.where(is_prompt, p, s)
    tm = o_ref.shape[0]
    sub = min(tm, 128)
    blocks = []
    for r in range(0, tm, sub):
        rows = slice(r, r + sub)
        acc = pick(h_refs[0][rows, :], h_refs[1][rows, :]) if split_residual else h_refs[0][rows, :]
        acc = acc + _bdot(pick(mo_p_ref[rows, :], mo_s_ref[rows, :]), wmo_ref[...])
        if heads:
            for h in range(heads):
                acc = acc + _bdot(pick(main_p_ref[h, rows, :], main_s_ref[h, rows, :]), wmain_ref[h])
        else:
            acc = acc + _bdot(pick(main_p_ref[rows, :], main_s_ref[rows, :]), wmain_ref[...])
        o_ref[rows, :] = acc
        blocks.append(acc)
    _route_tile(blocks, *route_in, *route_out)


def _outproj(h, main_p, mo_p, main_s, mo_s, w_main, w_mo, router, tm):
    n_p, n_s = mo_p.shape[0], mo_s.shape[0]
    n, d = n_p + n_s, w_mo.shape[1]
    assert n_p % tm == 0 and n_s % tm == 0
    pt = n_p // tm
    heads = main_p.shape[0] if main_p.ndim == 3 else 0
    row = pl.BlockSpec((tm, d), lambda i: (i, 0))
    split = isinstance(h, tuple)
    h_specs, h_args = (_group_specs(tm, d, pt), list(h)) if split else ([row], [h])
    main_specs = _group_specs(tm, main_p.shape[-1], pt, lead=heads or None)
    r_args, r_in_specs, r_out_specs, r_out_shape, r_scratch = _route_operands(*router, n, tm)
    h_new, *routed = pl.pallas_call(
        functools.partial(_outproj_kernel, heads=heads, prompt_tiles=pt, split_residual=split),
        grid=(n // tm,),
        in_specs=(h_specs + main_specs + _group_specs(tm, MEM_WIDTH, pt)
                  + [_const(w_main.shape), _const(w_mo.shape)] + r_in_specs),
        out_specs=[row] + r_out_specs,
        out_shape=[jax.ShapeDtypeStruct((n, d), F32)] + r_out_shape,
        scratch_shapes=[r_scratch],
        compiler_params=_params(("arbitrary",)),
        name="outproj_route",
    )(*h_args, main_p, main_s, mo_p, mo_s, w_main.astype(BF16), w_mo.astype(BF16), *r_args)
    return h_new, routed


def _route_tile(x_blocks, g_ref, whi_ref, wlo_ref, b_ref, before_tok_ref, before_row_ref,
                mi_ref, mf_ref, cnt_ref, tt_ref, carry_ref):
    @pl.when(pl.program_id(0) == 0)
    def _():
        carry_ref[...] = jnp.zeros_like(carry_ref)

    nt = lambda a, b: lax.dot_general(a, b, (((1,), (1,)), ((), ())), preferred_element_type=F32)
    logits = []
    for x in x_blocks:
        x_hi, x_lo = _split(x * _rms_scale(x) * g_ref[...], 2)
        logits.append((nt(whi_ref[...], x_hi) + nt(wlo_ref[...], x_hi) + nt(whi_ref[...], x_lo))[:ROUTER_ROWS])
    logits = jnp.concatenate(logits, axis=1) + b_ref[...]
    tm = logits.shape[1]
    row = lax.broadcasted_iota(jnp.int32, (ROUTER_ROWS, tm), 0)
    far = jnp.int32(2 * LANES)

    def first_max(vals):
        m = jnp.max(vals, axis=0, keepdims=True)
        return m, jnp.min(jnp.where(vals == m, row, far), axis=0, keepdims=True)

    gl = jnp.where(row < N_GROUPS, logits, -jnp.inf)
    gmax, grp = first_max(gl)
    pg_sel = 1.0 / jnp.sum(jnp.exp(gl - gmax), axis=0, keepdims=True)
    lo = ROUTER_LANE0 + grp * EXPERTS_PER_GROUP
    el = jnp.where((row >= lo) & (row < lo + EXPERTS_PER_GROUP), logits, -jnp.inf)
    m1, i1 = first_max(el)
    m2, i2 = first_max(jnp.where(row == i1, -jnp.inf, el))
    e2 = jnp.exp(m2 - m1)
    g1 = pg_sel / (1.0 + e2)
    g2 = pg_sel * e2 / (1.0 + e2)

    oh1 = row == i1
    oh2 = row == i2
    picked = jnp.where(oh1 | oh2, 1.0, 0.0)
    earlier = jnp.dot(picked.astype(BF16), before_tok_ref[...], preferred_element_type=F32)
    cnt_col = jnp.sum(picked, axis=1, keepdims=True)
    cnt_tile = jnp.concatenate([jnp.broadcast_to(cnt_col, (ROUTER_ROWS, LANES)),
                                jnp.zeros((LANES - ROUTER_ROWS, LANES), F32)], axis=0)
    c_hi = jnp.floor(cnt_tile * (1.0 / 32.0))
    c_lo = cnt_tile - 32.0 * c_hi
    first = (32.0 * jnp.dot(before_row_ref[...], c_hi.astype(BF16), preferred_element_type=F32)
             + jnp.dot(before_row_ref[...], c_lo.astype(BF16), preferred_element_type=F32))
    local = first[:ROUTER_ROWS, 0:1] + earlier
    lpos1 = jnp.sum(jnp.where(oh1, local, 0.0), axis=0, keepdims=True)
    lpos2 = jnp.sum(jnp.where(oh2, local, 0.0), axis=0, keepdims=True)
    carry_before = carry_ref[...]
    carry = carry_before + cnt_tile
    carry_ref[...] = carry

    lane = lax.broadcasted_iota(jnp.int32, (LANES, LANES), 1)
    cols = jnp.where(lane == 0, carry_before, jnp.where(lane == 1, cnt_tile, jnp.where(lane == 2, first,
                     jnp.where(lane == 3, carry, 0.0))))
    tables = cols.T
    tt_ref[...] = tables[:8].astype(jnp.int32)
    cnt_ref[...] = tables[3:4]
    row8 = lax.broadcasted_iota(jnp.int32, (ROUTER_META_ROWS, tm), 0)
    zero8 = jnp.zeros((ROUTER_META_ROWS, tm), F32)
    mi_ref[...] = jnp.where(row8 == 0, lpos1, jnp.where(row8 == 1, lpos2, zero8)).astype(jnp.int32)
    stacked = jnp.where(row8 == 0, g1, jnp.where(row8 == 1, g2, jnp.where(row8 == 2, lpos1,
                        jnp.where(row8 == 3, lpos2, zero8))))
    mf_ref[...] = jnp.concatenate([stacked, jnp.zeros((LANES - ROUTER_META_ROWS, tm), F32)], axis=0).T


def _route_operands(g, w_rg, b_rg, w_re, b_re, n, tm):
    d = g.shape[0]
    assert n % tm == 0 and 2 * tm <= 32 * 32
    n_real = N_GROUPS + N_EXPERTS
    w = jnp.pad(jnp.concatenate([w_rg, w_re], axis=1), ((0, 0), (0, LANES - n_real))).T
    b = jnp.pad(jnp.concatenate([b_rg, b_re]), (0, ROUTER_ROWS - n_real)).reshape(ROUTER_ROWS, 1)
    w_hi = w.astype(BF16)
    w_lo = (w - w_hi.astype(F32)).astype(BF16)
    i = np.arange(tm)
    before_tok = jnp.asarray(i[:, None] < i[None, :], BF16)
    e = np.arange(LANES)
    before_row = jnp.asarray(e[None, :] < e[:, None], BF16)
    args = [g.reshape(1, d), w_hi, w_lo, b, before_tok, before_row]
    in_specs = [_const(a.shape) for a in args]
    out_specs = [pl.BlockSpec((ROUTER_META_ROWS, tm), lambda i: (0, i)), pl.BlockSpec((tm, LANES), lambda i: (i, 0)),
                 _const((1, LANES)), pl.BlockSpec((8, LANES), lambda i: (i, 0))]
    out_shape = [jax.ShapeDtypeStruct((ROUTER_META_ROWS, n), jnp.int32), jax.ShapeDtypeStruct((n, LANES), F32),
                 jax.ShapeDtypeStruct((1, LANES), F32), jax.ShapeDtypeStruct((n // tm * 8, LANES), jnp.int32)]
    return args, in_specs, out_specs, out_shape, pltpu.VMEM((LANES, LANES), F32)


def _pack_rows(ref, x, rows, lead=(), row0=0):
    u32 = jnp.uint32
    for w in range(PACKED_SLABS):
        lo = x[:, (2 * w) * LANES:(2 * w + 1) * LANES].astype(BF16).astype(F32)
        hi = x[:, (2 * w + 1) * LANES:(2 * w + 2) * LANES].astype(BF16).astype(F32)
        word = (lax.bitcast_convert_type(lo, u32) >> 16) | (lax.bitcast_convert_type(hi, u32) & u32(0xFFFF0000))
        ref[lead + (pl.ds(row0 * PACKED_SLABS + w, rows, stride=PACKED_SLABS), slice(None))] = word


def _unpack_rows(ref, rows, lead=(), row0=0):
    u32 = jnp.uint32
    slabs = []
    for w in range(PACKED_SLABS):
        word = ref[lead + (pl.ds(row0 * PACKED_SLABS + w, rows, stride=PACKED_SLABS), slice(None))]
        slabs.append(lax.bitcast_convert_type(word << 16, F32).astype(BF16))
        slabs.append(lax.bitcast_convert_type(word & u32(0xFFFF0000), F32).astype(BF16))
    return jnp.concatenate(slabs, axis=1)


RUN_FIELDS = 3
RUN_CHUNK_BITS = 6


def _copy_runs(runs_ref, tile, local_rows, global_rows, sem, *, to_global):
    ps = PACKED_SLABS
    base = tile * (RUN_FIELDS * N_EXPERTS)

    def piece(g0, l0, off, size):
        g = global_rows(pl.multiple_of((g0 + off) * ps, ps), size * ps)
        l = local_rows(pl.multiple_of((l0 + off) * ps, ps), size * ps)
        src, dst = (l, g) if to_global else (g, l)
        pltpu.make_async_copy(src, dst, sem).start()

    def per_expert(e, carry):
        g0 = runs_ref[base + e]
        length = runs_ref[base + N_EXPERTS + e]
        l0 = runs_ref[base + 2 * N_EXPERTS + e]
        big = 1 << RUN_CHUNK_BITS

        def big_piece(c, inner):
            piece(g0, l0, c * big, big)
            return inner

        n_big = length >> RUN_CHUNK_BITS
        lax.fori_loop(0, n_big, big_piece, 0)
        off = n_big * big
        for bit in reversed(range(RUN_CHUNK_BITS)):
            size = 1 << bit

            @pl.when((length & size) != 0)
            def _(off=off, size=size):
                piece(g0, l0, off, size)

            off = off + (length & size)
        return carry

    lax.fori_loop(0, N_EXPERTS, per_expert, 0)


def _fill_pads(pads_ref, buf0, xs_hbm, sem, *, wait):
    ps = PACKED_SLABS

    def go(copy):
        copy.wait() if wait else copy.start()

    def per_tail_tile(t, carry):
        rows = MOE_TILE * ps
        go(pltpu.make_async_copy(
            buf0.at[pl.ds(0, rows)],
            xs_hbm.at[pl.ds(pl.multiple_of((pads_ref[2 * N_EXPERTS] + t) * rows, rows), rows)], sem))
        return carry

    lax.fori_loop(0, pads_ref[2 * N_EXPERTS + 1], per_tail_tile, 0)

    def per_expert(e, carry):
        first, length = pads_ref[e], pads_ref[N_EXPERTS + e]
        off = 0
        for bit in reversed(range((MOE_TILE - 1).bit_length())):
            size = 1 << bit

            @pl.when((length & size) != 0)
            def _(off=off, size=size):
                go(pltpu.make_async_copy(
                    buf0.at[pl.ds(pl.multiple_of(off * ps, ps), size * ps)],
                    xs_hbm.at[pl.ds(pl.multiple_of((first + off) * ps, ps), size * ps)], sem))

            off = off + (length & size)
        return carry

    lax.fori_loop(0, N_EXPERTS, per_expert, 0)


def _dispatch_kernel(runs_ref, pads_ref, h_ref, g_ref, meta_ref, xs_hbm, buf, sem, pad_sem, *, tm, steps):
    i = pl.program_id(0)
    slot = lax.rem(i, 2)
    ns = 2 * tm
    assert ns >= MOE_TILE

    def wait_slot(sl):
        pltpu.make_async_copy(buf.at[sl], xs_hbm.at[pl.ds(0, ns * PACKED_SLABS)], sem.at[sl]).wait()

    @pl.when(i >= 2)
    def _():
        wait_slot(slot)

    x = h_ref[...]
    xn = (x * _rms_scale(x) * g_ref[...]).astype(BF16)
    j = lax.broadcasted_iota(jnp.int32, (ns, tm), 0)
    pick = jnp.where((j == meta_ref[0:1, :]) | (j == meta_ref[1:2, :]), 1.0, 0.0).astype(BF16)
    _pack_rows(buf, jnp.dot(pick, xn, preferred_element_type=F32), ns, (slot,))
    _copy_runs(runs_ref, i, lambda start, size: buf.at[slot, pl.ds(start, size)],
               lambda start, size: xs_hbm.at[pl.ds(start, size)], sem.at[slot], to_global=True)

    @pl.when(i == 0)
    def _():
        _fill_pads(pads_ref, buf.at[0], xs_hbm, pad_sem, wait=False)

    @pl.when(i == min(1, steps - 1))
    def _():
        _fill_pads(pads_ref, buf.at[0], xs_hbm, pad_sem, wait=True)

    @pl.when(i == steps - 1)
    def _():
        wait_slot(slot)
        if steps > 1:
            wait_slot(1 - slot)


def _dispatch(h, g, meta, runs, pads, tm, n_slots):
    n, d = h.shape
    steps = n // tm
    return pl.pallas_call(
        functools.partial(_dispatch_kernel, tm=tm, steps=steps),
        grid_spec=pltpu.PrefetchScalarGridSpec(
            num_scalar_prefetch=2,
            grid=(steps,),
            in_specs=[pl.BlockSpec((tm, d), lambda i, *_: (i, 0)),
                      pl.BlockSpec((1, d), lambda i, *_: (0, 0)),
                      pl.BlockSpec((ROUTER_META_ROWS, tm), lambda i, *_: (0, i))],
            out_specs=pl.BlockSpec(memory_space=pl.ANY),
            scratch_shapes=[pltpu.VMEM((2, 2 * tm * PACKED_SLABS, LANES), jnp.uint32),
                            pltpu.SemaphoreType.DMA((2,)), pltpu.SemaphoreType.DMA],
        ),
        out_shape=jax.ShapeDtypeStruct((n_slots * PACKED_SLABS, LANES), jnp.uint32),
        compiler_params=_params(("arbitrary",)),
        name="moe_dispatch",
    )(runs, pads, h, g.reshape(1, d), meta)


def _expert_kernel(exp_ref, new_ref, wslot_ref, next_ref, n_ref,
                   xs_hbm, wg_hbm, wu_hbm, wd_hbm, ys_hbm,
                   xbuf, ybuf, xsem, ysem, wgf, wuf, wdf, wsem, wgb, wub, wdb, *, tm, sub, layer, max_tiles):
    n = n_ref[0]
    rows = tm * PACKED_SLABS
    tile_rows = lambda hbm, t: hbm.at[pl.ds(pl.multiple_of(t * rows, rows), rows)]
    x_copy = lambda t, slot: pltpu.make_async_copy(tile_rows(xs_hbm, t), xbuf.at[slot], xsem.at[slot])
    y_copy = lambda t, slot: pltpu.make_async_copy(ybuf.at[slot], tile_rows(ys_hbm, t), ysem.at[slot])

    def weight_copies(e, slot):
        return [pltpu.make_async_copy(hbm.at[layer, e], buf.at[slot], wsem.at[slot])
                for hbm, buf in ((wg_hbm, wgf), (wu_hbm, wuf), (wd_hbm, wdf))]

    x_copy(0, 0).start()

    def item(w, carry):
        io = lax.rem(w, 2)
        x_copy(w, io).wait()

        @pl.when(w + 1 < n)
        def _():
            x_copy(w + 1, 1 - io).start()

        @pl.when(w >= 2)
        def _():
            y_copy(w - 2, io).wait()

        @pl.when(new_ref[w] != 0)
        def _():
            slot = wslot_ref[w]

            @pl.when(w == 0)
            def _():
                for c in weight_copies(exp_ref[w], slot):
                    c.start(priority=1)

            for c in weight_copies(exp_ref[w], slot):
                c.wait()
            wgb[...] = wgf[slot].astype(BF16)
            wub[...] = wuf[slot].astype(BF16)
            wdb[...] = wdf[slot].astype(BF16)

            @pl.when(next_ref[w] >= 0)
            def _():
                for c in weight_copies(next_ref[w], 1 - slot):
                    c.start(priority=1)

        n_blocks = tm // sub

        def up(s):
            x = _unpack_rows(xbuf, sub, (io,), row0=s * sub)
            return (jnp.dot(x, wgb[...], preferred_element_type=F32),
                    jnp.dot(x, wub[...], preferred_element_type=F32))

        ups = {0: up(0)}
        for s in range(n_blocks):
            if s + 1 < n_blocks:
                ups[s + 1] = up(s + 1)
            hg, hu = ups.pop(s)
            act = (hg * jax.nn.sigmoid(hg) * hu).astype(BF16)
            y = jnp.dot(act, wdb[...], preferred_element_type=F32)
            _pack_rows(ybuf, y, sub, (io,), row0=s * sub)
        y_copy(w, io).start()
        return carry

    lax.fori_loop(0, n, item, 0)

    @pl.when(n >= 2)
    def _():
        y_copy(n - 2, lax.rem(n, 2)).wait()
    y_copy(n - 1, lax.rem(n - 1, 2)).wait()

    for wait in (False, True):
        def spare(t, carry, wait=wait):
            copy = y_copy(n + t, 0)
            copy.wait() if wait else copy.start()
            return carry

        lax.fori_loop(0, max_tiles - n, spare, 0)


def _experts(xs, items, w_g, w_u, w_d, layer):
    d = D_MODEL
    tm = MOE_TILE
    max_tiles = items[0].shape[0]
    any_spec = pl.BlockSpec(memory_space=pl.ANY)
    io_buf = pltpu.VMEM((2, tm * PACKED_SLABS, LANES), jnp.uint32)
    return pl.pallas_call(
        functools.partial(_expert_kernel, tm=tm, sub=MOE_SUB, layer=layer, max_tiles=max_tiles),
        grid_spec=pltpu.PrefetchScalarGridSpec(
            num_scalar_prefetch=len(items),
            grid=(1,),
            in_specs=[any_spec, any_spec, any_spec, any_spec],
            out_specs=any_spec,
            scratch_shapes=[io_buf, io_buf, pltpu.SemaphoreType.DMA((2,)), pltpu.SemaphoreType.DMA((2,)),
                            pltpu.VMEM((2, d, D_EXPERT), F32), pltpu.VMEM((2, d, D_EXPERT), F32),
                            pltpu.VMEM((2, D_EXPERT, d), F32), pltpu.SemaphoreType.DMA((2,)),
                            pltpu.VMEM((d, D_EXPERT), BF16), pltpu.VMEM((d, D_EXPERT), BF16),
                            pltpu.VMEM((D_EXPERT, d), BF16)],
        ),
        out_shape=jax.ShapeDtypeStruct(xs.shape, jnp.uint32),
        compiler_params=_params(("arbitrary",)),
        name="moe_experts",
    )(*items, xs, w_g, w_u, w_d)


def _combine_kernel(runs_ref, h_ref, gate_ref, ys_hbm, *refs, tm, steps, prompt_tiles):
    out_refs, (buf, sem) = refs[:-2], refs[-2:]
    i = pl.program_id(0)
    slot = lax.rem(i, 2)
    ns = 2 * tm

    def issue(step, sl):
        _copy_runs(runs_ref, step, lambda start, size: buf.at[sl, pl.ds(start, size)],
                   lambda start, size: ys_hbm.at[pl.ds(start, size)], sem.at[sl], to_global=False)

    @pl.when(i == 0)
    def _():
        issue(0, 0)

    @pl.when(i + 1 < steps)
    def _():
        issue(i + 1, 1 - slot)

    pltpu.make_async_copy(ys_hbm.at[pl.ds(0, ns * PACKED_SLABS)], buf.at[slot], sem.at[slot]).wait()
    y = _unpack_rows(buf, ns, (slot,))
    g = gate_ref[...]
    j = lax.broadcasted_iota(jnp.int32, (tm, ns), 1)
    mix = (jnp.where(j == g[:, 2:3].astype(jnp.int32), g[:, 0:1], 0.0)
           + jnp.where(j == g[:, 3:4].astype(jnp.int32), g[:, 1:2], 0.0)).astype(BF16)
    out = h_ref[...] + jnp.dot(mix, y, preferred_element_type=F32)
    if len(out_refs) == 1:
        out_refs[0][...] = out
    else:
        @pl.when(i < prompt_tiles)
        def _():
            out_refs[0][...] = out

        @pl.when(i >= prompt_tiles)
        def _():
            out_refs[1][...] = out


def _combine(h, gates, ys, runs, tm, split_rows=None):
    n, d = h.shape
    steps = n // tm
    row = pl.BlockSpec((tm, d), lambda i, pos: (i, 0))
    if split_rows is None:
        pt, out_specs, out_shape = 0, row, jax.ShapeDtypeStruct((n, d), F32)
    else:
        assert split_rows % tm == 0
        pt = split_rows // tm
        out_specs = _group_specs(tm, d, pt)
        out_shape = [jax.ShapeDtypeStruct((split_rows, d), F32), jax.ShapeDtypeStruct((n - split_rows, d), F32)]
    return pl.pallas_call(
        functools.partial(_combine_kernel, tm=tm, steps=steps, prompt_tiles=pt),
        grid_spec=pltpu.PrefetchScalarGridSpec(
            num_scalar_prefetch=1,
            grid=(steps,),
            in_specs=[row, pl.BlockSpec((tm, LANES), lambda i, pos: (i, 0)),
                      pl.BlockSpec(memory_space=pl.ANY)],
            out_specs=out_specs,
            scratch_shapes=[pltpu.VMEM((2, 2 * tm * PACKED_SLABS, LANES), jnp.uint32),
                            pltpu.SemaphoreType.DMA((2,))],
        ),
        out_shape=out_shape,
        compiler_params=_params(("arbitrary",)),
        name="moe_combine",
    )(runs, h, gates, ys)


def _lookup(tables, idx):
    hit = idx[:, None] == jnp.arange(tables.shape[1], dtype=idx.dtype)[None, :]
    return jnp.sum(jnp.where(hit[None], tables[:, None, :], 0), axis=2)


def _work_items(tiles_e, max_items):
    item_end = jnp.cumsum(tiles_e)
    n_items = item_end[-1]
    w = jnp.minimum(jnp.arange(max_items, dtype=jnp.int32), n_items - 1)
    expert = jnp.sum(w[:, None] >= item_end[None, :], axis=1).astype(jnp.int32)
    prev_expert = jnp.concatenate([jnp.full((1,), -1, jnp.int32), expert[:-1]])
    new_expert = expert != prev_expert
    weight_slot = (jnp.cumsum(new_expert.astype(jnp.int32)) - 1) % 2
    ids = jnp.arange(N_EXPERTS, dtype=jnp.int32)
    later = (ids[None, :] > ids[:, None]) & (tiles_e[None, :] > 0)
    following = jnp.min(jnp.where(later, ids[None, :], N_EXPERTS), axis=1)
    following = jnp.where(following == N_EXPERTS, -1, following)
    next_expert, = _lookup(following[None, :], expert)
    as_i32 = lambda a: a.astype(jnp.int32)
    return (expert, as_i32(new_expert), as_i32(weight_slot), as_i32(next_expert),
            as_i32(n_items).reshape(1))


def _moe(h, routed, g, w_g, w_u, w_d, layer, tm, split_rows=None):
    n, _ = h.shape
    meta, gates, cnt, tables = routed
    experts = slice(ROUTER_LANE0, ROUTER_LANE0 + N_EXPERTS)
    counts = cnt[0, experts].astype(jnp.int32)
    tiles_e = (counts + MOE_TILE - 1) // MOE_TILE
    starts = (jnp.cumsum(tiles_e) - tiles_e) * MOE_TILE
    max_tiles = -(-2 * n // MOE_TILE) + N_EXPERTS
    tables = tables.reshape(n // tm, 8, LANES)[:, :RUN_FIELDS, experts]
    runs = tables.at[:, 0, :].add(starts[None, :]).reshape(-1)
    used = jnp.sum(tiles_e)
    pads = jnp.concatenate([starts + counts, tiles_e * MOE_TILE - counts, jnp.stack([used, max_tiles - used])])
    xs = _dispatch(h, g, meta, runs, pads, tm, max_tiles * MOE_TILE)
    ys = _experts(xs, _work_items(tiles_e, max_tiles), w_g, w_u, w_d, layer)
    return _combine(h, gates, ys, runs, tm, split_rows)


def _inproj_b_kernel(x_ref, gkv_ref, gmix_ref, wkv_ref, win_ref, kng_ref, qng_ref, bdk_ref, hsum_ref, hexp_ref,
                     q_ref, mq_ref, k_ref, v_ref, kt_ref):
    tm = x_ref.shape[0]
    sub = min(tm, 256)
    blocks = [slice(r, r + sub) for r in range(0, tm, sub)]

    def two_piece_dot(x, m_ref):
        out = None
        for p in _split(x, 2):
            t = jnp.dot(p, m_ref[...], preferred_element_type=F32)
            out = t if out is None else out + t
        return out

    kvs, projs = [], []
    for rows in blocks:
        x = x_ref[rows, :]
        xr = x * _rms_scale(x)
        kvs.append(_bdot(xr * gkv_ref[...], wkv_ref[...]))
        projs.append(_bdot(xr * gmix_ref[...], win_ref[...]))
    k_ms = [two_piece_dot(kv[:, :KV_WIDTH] * kv[:, :KV_WIDTH], bdk_ref) for kv in kvs]
    q_ms = [two_piece_dot(proj[:, :MAIN_WIDTH] * proj[:, :MAIN_WIDTH], hsum_ref) for proj in projs]
    q_scale = [two_piece_dot(lax.rsqrt(ms + EPS), hexp_ref) for ms in q_ms]
    for rows, kv, proj, kms, qs in zip(blocks, kvs, projs, k_ms, q_scale):
        k = kv[:, :KV_WIDTH] * lax.rsqrt(kms + EPS) * kng_ref[...]
        k_ref[rows, :] = k
        kt_ref[:, rows] = k.T.astype(BF16)
        v_ref[rows, :] = kv[:, KV_WIDTH:]
        q_ref[rows, :] = (proj[:, :MAIN_WIDTH] * qs * qng_ref[...]).astype(BF16)
        mq_ref[rows, :] = proj[:, MAIN_WIDTH:]


def _swa_perm():
    g, kh, dd = np.meshgrid(np.arange(SWA_GROUP), np.arange(SWA_KV_HEADS), np.arange(HEAD_DIM), indexing="ij")
    return ((kh * SWA_GROUP + g) * HEAD_DIM + dd).reshape(-1)


def _inproj_b(x, g_kv, g_mix, w_kv, w_in, kng, qng):
    n, d = x.shape
    tm = _row_tile(n, 1024)
    perm = _swa_perm()
    w_in_p = jnp.concatenate([w_in[:, :MAIN_WIDTH][:, perm], w_in[:, MAIN_WIDTH:]], axis=1).astype(BF16)
    qng_t = (jnp.tile(qng, SWA_HEADS) * HEAD_DIM ** -0.5).reshape(1, MAIN_WIDTH)
    member = (np.arange(MAIN_WIDTH)[:, None] // HEAD_DIM == np.arange(LANES)[None, :]).astype(np.float32)
    row = lambda w: pl.BlockSpec((tm, w), lambda i: (i, 0))
    return pl.pallas_call(
        _inproj_b_kernel,
        grid=(n // tm,),
        in_specs=[row(d), _const((1, d)), _const((1, d)), _const((d, 2 * KV_WIDTH)), _const((d, d)),
                  _const((1, KV_WIDTH)), _const((1, MAIN_WIDTH)), _const((KV_WIDTH, KV_WIDTH)),
                  _const((MAIN_WIDTH, LANES)), _const((LANES, MAIN_WIDTH))],
        out_specs=[row(MAIN_WIDTH), row(MEM_WIDTH), row(KV_WIDTH), row(KV_WIDTH),
                   pl.BlockSpec((KV_WIDTH, tm), lambda i: (0, i))],
        out_shape=[jax.ShapeDtypeStruct((n, MAIN_WIDTH), BF16), jax.ShapeDtypeStruct((n, MEM_WIDTH), F32),
                   jax.ShapeDtypeStruct((n, KV_WIDTH), F32), jax.ShapeDtypeStruct((n, KV_WIDTH), F32),
                   jax.ShapeDtypeStruct((KV_WIDTH, n), BF16)],
        compiler_params=_params(("parallel",)),
        name="inproj_b",
    )(x, g_kv.reshape(1, d), g_mix.reshape(1, d), w_kv.astype(BF16), w_in_p,
      jnp.tile(kng, SWA_KV_HEADS).reshape(1, KV_WIDTH), qng_t,
      _block_diag_mean(KV_WIDTH), jnp.asarray(member / HEAD_DIM, BF16), jnp.asarray(member.T, BF16))


def _softmax_with_sink(s, sink):
    m = jnp.maximum(jnp.max(s, axis=-1, keepdims=True), sink)
    e = jnp.exp(s - m)
    r = 1.0 / (jnp.sum(e, axis=-1, keepdims=True) + jnp.exp(sink - m))
    return (e * r).astype(BF16)


def _swa_bias(tq):
    slopes = 2.0 ** (-8.0 * np.arange(1, SWA_HEADS + 1, dtype=np.float64) / SWA_HEADS)
    dist = np.arange(tq)[:, None] + WINDOW - np.arange(WINDOW + tq)[None, :]
    valid = (dist >= 0) & (dist <= WINDOW)
    return np.stack([np.where(valid, -s * dist, NEG_BIG) for s in slopes]).astype(np.float32)


def _swa_prompt_kernel(sink_ref, q_ref, ktp_ref, kto_ref, vp_ref, vo_ref, bias_ref, hm_ref, o_ref, *, nb):
    w = WINDOW
    key = lax.broadcasted_iota(jnp.int32, (w, 2 * w), 1)
    has_prev = (pl.program_id(0) > 0) | (key >= w)
    heads = [(g, kh) for g in range(SWA_GROUP) for kh in range(SWA_KV_HEADS)]
    kts, vvs = [], []
    for b in range(nb):
        kt_prev = ktp_ref[...] if b == 0 else kto_ref[:, (b - 1) * w:b * w]
        v_prev = vp_ref[...] if b == 0 else vo_ref[(b - 1) * w:b * w, :]
        kts.append(jnp.concatenate([kt_prev, kto_ref[:, b * w:(b + 1) * w]], axis=1).astype(BF16))
        vvs.append(jnp.concatenate([v_prev, vo_ref[b * w:(b + 1) * w, :]], axis=0).astype(BF16))
    scores = [[jnp.dot(q_ref[b * w:(b + 1) * w, g * KV_WIDTH:(g + 1) * KV_WIDTH] * hm_ref[kh].astype(BF16),
                       kts[b], preferred_element_type=F32) for g, kh in heads] for b in range(nb)]
    for b in range(nb):
        probs = []
        for (g, kh), s in zip(heads, scores[b]):
            h = kh * SWA_GROUP + g
            s = s + bias_ref[h]
            if b == 0:
                s = jnp.where(has_prev, s, NEG_BIG)
            probs.append(_softmax_with_sink(s, sink_ref[h]))
        outs = [jnp.dot(p, vvs[b], preferred_element_type=F32) for p in probs]
        lane_head = lax.broadcasted_iota(jnp.int32, (w, KV_WIDTH), 1) // HEAD_DIM
        for g in range(SWA_GROUP):
            acc = None
            for (cg, kh), o in zip(heads, outs):
                if cg == g:
                    acc = o if acc is None else jnp.where(lane_head == kh, o, acc)
            o_ref[b * w:(b + 1) * w, g * KV_WIDTH:(g + 1) * KV_WIDTH] = acc.astype(BF16)


def _swa_prompt(q, kt, v, sinks, *, n_rows, nb):
    w = WINDOW
    step = nb * w
    assert n_rows % step == 0
    prev = lambda j, sink: jnp.maximum(j * nb - 1, 0)
    return pl.pallas_call(
        functools.partial(_swa_prompt_kernel, nb=nb),
        grid_spec=pltpu.PrefetchScalarGridSpec(
            num_scalar_prefetch=1,
            grid=(n_rows // step,),
            in_specs=[pl.BlockSpec((step, MAIN_WIDTH), lambda j, sink: (j, 0)),
                      pl.BlockSpec((KV_WIDTH, w), lambda j, sink: (0, prev(j, sink))),
                      pl.BlockSpec((KV_WIDTH, step), lambda j, sink: (0, j)),
                      pl.BlockSpec((w, KV_WIDTH), lambda j, sink: (prev(j, sink), 0)),
                      pl.BlockSpec((step, KV_WIDTH), lambda j, sink: (j, 0)),
                      pl.BlockSpec((SWA_HEADS, w, 2 * w), lambda j, sink: (0, 0, 0)),
                      pl.BlockSpec((SWA_KV_HEADS, 1, KV_WIDTH), lambda j, sink: (0, 0, 0))],
            out_specs=pl.BlockSpec((step, MAIN_WIDTH), lambda j, sink: (j, 0)),
        ),
        out_shape=jax.ShapeDtypeStruct((n_rows, MAIN_WIDTH), BF16),
        compiler_params=_params(("arbitrary",)),
        name="swa_prompt",
    )(sinks.astype(F32), q, kt, kt, v, v, jnp.asarray(_swa_bias(w)), _head_masks(SWA_KV_HEADS))


def _swa_sample_kernel(q_ref, kp_ref, ko_ref, vp_ref, vo_ref, bias_ref, sink_ref, hm_ref, o_ref, *, nb, tq):
    w = WINDOW
    heads = [(kh, g) for kh in range(SWA_KV_HEADS) for g in range(SWA_GROUP)]
    kks, vvs, scores = [], [], []
    q = q_ref[...].astype(F32)
    for i in range(nb):
        win = slice(i * KV_WIDTH, (i + 1) * KV_WIDTH)
        kks.append(jnp.concatenate([kp_ref[win, :].T, ko_ref[i * tq:(i + 1) * tq, :]], axis=0))
        vvs.append(jnp.concatenate([vp_ref[win, :].T, vo_ref[i * tq:(i + 1) * tq, :]], axis=0))
        qs = jnp.concatenate([q[i * tq:(i + 1) * tq, g * KV_WIDTH:(g + 1) * KV_WIDTH] * hm_ref[kh]
                              for kh, g in heads], axis=0)
        scores.append(_bdot_nt(qs, kks[i]))
    probs = [_softmax_with_sink(s + bias_ref[...], sink_ref[...]) for s in scores]
    outs = [_bdot(p, vv) for p, vv in zip(probs, vvs)]
    for g in range(SWA_GROUP):
        rows = []
        for i in range(nb):
            acc = None
            for r, (kh, hg) in enumerate(heads):
                if hg == g:
                    t = outs[i][r * tq:(r + 1) * tq] * hm_ref[kh]
                    acc = t if acc is None else acc + t
            rows.append(acc)
        o_ref[:, g * KV_WIDTH:(g + 1) * KV_WIDTH] = jnp.concatenate(rows, axis=0).astype(BF16)


def _swa_sample(q, k_win, v_win, k, v, sinks, *, row_off, batch, tq, nb):
    w = WINDOW
    assert batch % nb == 0 and row_off % (nb * tq) == 0
    off = row_off // (nb * tq)
    bias = jnp.asarray(_swa_bias(tq).reshape(SWA_HEADS * tq, w + tq))
    sink_col = jnp.repeat(sinks.astype(F32), tq).reshape(SWA_HEADS * tq, 1)
    own = lambda width: pl.BlockSpec((nb * tq, width), lambda b: (off + b, 0))
    win = pl.BlockSpec((nb * KV_WIDTH, w), lambda b: (b, 0))
    return pl.pallas_call(
        functools.partial(_swa_sample_kernel, nb=nb, tq=tq),
        grid=(batch // nb,),
        in_specs=[own(MAIN_WIDTH), win, own(KV_WIDTH), win, own(KV_WIDTH), _const(bias.shape),
                  _const(sink_col.shape), _const((SWA_KV_HEADS, 1, KV_WIDTH))],
        out_specs=pl.BlockSpec((nb * tq, MAIN_WIDTH), lambda b: (b, 0)),
        out_shape=jax.ShapeDtypeStruct((batch * tq, MAIN_WIDTH), BF16),
        compiler_params=_params(("arbitrary",)),
        name="swa_sample",
    )(q, k_win, k, v_win, v, bias, sink_col, _head_masks(SWA_KV_HEADS))


def kernel(x_prompt, x_sample, state_gla, cache_win_k, cache_win_v, cache_mem_k, cache_mem_v, mem_prompt, norm_mix_g, norm_ffn_g, norm_mem_g, w_mem_kv, mem_qn_g, mem_kn_g, w_out, w_in_a, w_gate_lr, b_gate_lr, gla_norm_g, w_in_b, swa_qn_g, swa_sinks, norm_kv_g, w_kv, swa_kn_g, w_router_group, b_router_group, w_router_expert, b_router_expert, w_exp_gate, w_exp_up, w_exp_down):
    bp, tp, d = x_prompt.shape
    bs, ts, _ = x_sample.shape
    assert bp == 1 and tp % WINDOW == 0 and ts * (GLA_CHUNK // ts) == GLA_CHUNK
    n_p, n_s = bp * tp, bs * ts
    w_buf = cache_win_k.shape[1]
    assert w_buf == WINDOW
    x_p, x_s = x_prompt.reshape(n_p, d), x_sample.reshape(n_s, d)

    mem_k_p, mem_v_p = _mem_kv(mem_prompt, norm_mem_g, w_mem_kv, mem_kn_g)
    feature_major = lambda c: jnp.moveaxis(c, -3, -1).reshape(*c.shape[:-3], c.shape[-2] * c.shape[-1], c.shape[-3])
    cmk, cmv = feature_major(cache_mem_k), feature_major(cache_mem_v)

    def mem_attend(mq, l):
        tm_p = _row_tile(tp, 1024)
        mo_p = _mem_attn(mq, mem_k_p, mem_v_p, mem_qn_g[l], row_off=0, seq=tp, tm=tm_p, bb=1, layer=l)
        mo_s = _mem_attn(mq, cmk, cmv, mem_qn_g[l], row_off=n_p, seq=ts, tm=ts, bb=16, layer=l)
        return mo_p, mo_s

    tm = _row_tile(math.gcd(n_p, n_s), MOE_TILE)
    router = lambda l: (norm_ffn_g[l], w_router_group[l], b_router_group[l], w_router_expert[l],
                        b_router_expert[l])

    def moe(h, routed, l, split_rows=None):
        return _moe(h, routed, norm_ffn_g[l], w_exp_gate, w_exp_up, w_exp_down, l, tm, split_rows)

    q, k, la, v, og, mq = _inproj_a(x_p, x_s, norm_mix_g[0], w_in_a[0], w_gate_lr[0], b_gate_lr[0])
    zero_state = jnp.zeros((bp, GLA_HEADS, GLA_DK, GLA_DV), F32)
    n_sub = max(1, min(16, tp // GLA_CHUNK))
    main_p, gla_p = _gla(q, k, la, v, og, zero_state, gla_norm_g[0], row_off=0, seq=tp, n_seg=1, n_sub=n_sub)
    main_s, gla_s = _gla(q, k, la, v, og, state_gla[0], gla_norm_g[0], row_off=n_p, seq=ts,
                         n_seg=GLA_CHUNK // ts, n_sub=1)
    mo_p, mo_s = mem_attend(mq, 0)
    w_o = w_out[0]
    h, routed = _outproj((x_p, x_s), main_p, mo_p, main_s, mo_s, w_o[:MAIN_WIDTH].reshape(GLA_HEADS, GLA_DV, d),
                         w_o[MAIN_WIDTH:], router(0), tm)
    h = moe(h, routed, 0)

    q, mq, k_sh, v_sh, kt_sh = _inproj_b(h, norm_kv_g, norm_mix_g[1], w_kv, w_in_b[0], swa_kn_g, swa_qn_g[0])
    ck = feature_major(cache_win_k).reshape(bs * KV_WIDTH, w_buf)
    cv = feature_major(cache_win_v).reshape(bs * KV_WIDTH, w_buf)
    main_p = _swa_prompt(q, kt_sh, v_sh, swa_sinks[0], n_rows=n_p, nb=8)
    main_s = _swa_sample(q, ck, cv, k_sh, v_sh, swa_sinks[0], row_off=n_p, batch=bs, tq=ts, nb=16)
    mo_p, mo_s = mem_attend(mq, 1)
    w_o = w_out[1]
    h, routed = _outproj(h, main_p, mo_p, main_s, mo_s, w_o[:MAIN_WIDTH][_swa_perm()], w_o[MAIN_WIDTH:],
                         router(1), tm)
    y_p, y_s = moe(h, routed, 1, split_rows=n_p)

    y_prompt = y_p.reshape(bp, tp, d)
    y_sample = y_s.reshape(bs, ts, d)
    k_new = k_sh[n_p:].reshape(bs, ts, SWA_KV_HEADS, HEAD_DIM)
    v_new = v_sh[n_p:].reshape(bs, ts, SWA_KV_HEADS, HEAD_DIM)
    win_k_s = jnp.concatenate([cache_win_k, k_new], axis=1)[:, -w_buf:]
    win_v_s = jnp.concatenate([cache_win_v, v_new], axis=1)[:, -w_buf:]
    win_k_p = k_sh[n_p - WINDOW:n_p].reshape(bp, WINDOW, SWA_KV_HEADS, HEAD_DIM)
    win_v_p = v_sh[n_p - WINDOW:n_p].reshape(bp, WINDOW, SWA_KV_HEADS, HEAD_DIM)
    token_major = lambda c: jnp.moveaxis(c.reshape(*c.shape[:-2], MEM_HEADS, HEAD_DIM, c.shape[-1]), -1, -3)
    return (y_prompt, y_sample, gla_p[None], gla_s[None], win_k_p, win_v_p, win_k_s, win_v_s,
            token_major(mem_k_p), token_major(mem_v_p))
```
